```python
import jax
import jax.numpy as jnp
from jax import lax
import numpy as np

D_MODEL = 1024
BATCH = 8
SEQ = 4096
DEPTH = 2

GRID_W = 64
CTX_LEN = 256
N_EVEN = (DEPTH + 1) // 2
N_ODD = DEPTH // 2

RET_HEADS = 4
RET_DIM = 128
HGRN_HEADS = 4
HGRN_DIM = 128
RET_WIDTH = RET_HEADS * RET_DIM
HGRN_WIDTH = HGRN_HEADS * HGRN_DIM
MIX_IN_SIZES = [RET_WIDTH] * 4 + [HGRN_WIDTH] * 5
MIX_IN_COLS = sum(MIX_IN_SIZES)
MIX_IN_SPLITS = np.cumsum(MIX_IN_SIZES)[:-1].tolist()
RET_CHUNK = 128
HGRN_CHUNK = 64

MLA_HEADS = 8
MLA_NOPE = 128
MLA_ROPE = 64
MLA_V = 128
MLA_Q_RANK = 384
MLA_KV_RANK = 256
MLA_DOWN_COLS = MLA_Q_RANK + MLA_KV_RANK + MLA_ROPE
Q_BLOCK = 128

FFN_DIM = 2816
N_EXPERTS = 8
TOP_K = 2
MOE_DIM = 2816

ROPE_BASE = 10000.0
NORM_EPS = 1e-6

kernel_name = 'hybrid_retention_hgrn2_mla_moe_prefix_dit'


def rmsnorm(x, gain=None):
    xf = x.astype(jnp.float32)
    y = (xf * lax.rsqrt(jnp.mean(xf * xf, axis=-1, keepdims=True) + NORM_EPS)).astype(x.dtype)
    return y if gain is None else y * gain


def axial_rope_tables(n_rows, dim):
    rows = jnp.repeat(jnp.arange(n_rows), GRID_W).astype(jnp.float32)
    cols = jnp.tile(jnp.arange(GRID_W), n_rows).astype(jnp.float32)
    n_freq = dim // 4
    inv = ROPE_BASE ** (-jnp.arange(n_freq, dtype=jnp.float32) / n_freq)
    ang = jnp.concatenate([rows[:, None] * inv, cols[:, None] * inv], axis=-1)
    return jnp.cos(ang), jnp.sin(ang)


def apply_rope(x, cos, sin):
    half = x.shape[-1] // 2
    x1, x2 = x[..., :half], x[..., half:]
    cos = cos.astype(x.dtype)
    sin = sin.astype(x.dtype)
    return jnp.concatenate([x1 * cos - x2 * sin, x1 * sin + x2 * cos], axis=-1)


def _to_chunks(t, size):
    b, h, n, d = t.shape
    return jnp.moveaxis(t.astype(jnp.float32).reshape(b, h, n // size, size, d), 2, 0)


def _from_chunks(t):
    n, b, h, c, d = t.shape
    return jnp.moveaxis(t, 0, 2).reshape(b, h, n * c, d)


def retention_scan(q, k, v, log_gamma, s0):
    cs = RET_CHUNK
    pos = jnp.arange(cs, dtype=jnp.float32)
    lg = log_gamma.astype(jnp.float32)[:, None]
    rel = pos[:, None] - pos[None, :]
    intra = jnp.where(rel >= 0, jnp.exp(lg[:, :, None] * jnp.maximum(rel, 0.0)), 0.0)
    q_dec = jnp.exp(lg * (pos + 1.0))[:, :, None]
    k_dec = jnp.exp(lg * (cs - 1.0 - pos))[:, :, None]
    c_dec = jnp.exp(lg * cs)[:, :, None]

    def step(state, inp):
        qc, kc, vc = inp
        att = jnp.einsum('bhid,bhjd->bhij', qc, kc) * intra
        o = jnp.einsum('bhij,bhje->bhie', att, vc) + jnp.einsum('bhid,bhde->bhie', qc * q_dec, state)
        state = state * c_dec + jnp.einsum('bhjd,bhje->bhde', kc * k_dec, vc)
        return state, o

    state, o = lax.scan(step, s0, (_to_chunks(q, cs), _to_chunks(k, cs), _to_chunks(v, cs)))
    return _from_chunks(o), state


def gated_scan(q, k, v, log_f, s0):
    cs = HGRN_CHUNK
    lower = jnp.tril(jnp.ones((cs, cs), dtype=bool))[:, :, None]

    def step(state, inp):
        qc, kc, vc, gc = inp
        b = jnp.cumsum(gc, axis=2)
        diff = b[:, :, :, None, :] - b[:, :, None, :, :]
        w = jnp.exp(jnp.where(lower, diff, -jnp.inf))
        att = jnp.einsum('bhtsd,bhsd->bhts', qc[:, :, :, None, :] * w, kc)
        o = jnp.einsum('bhts,bhse->bhte', att, vc) + jnp.einsum('bhtd,bhde->bhte', qc * jnp.exp(b), state)
        b_last = b[:, :, -1:, :]
        state = jnp.exp(b_last[:, :, 0, :])[..., None] * state + jnp.einsum(
            'bhsd,bhse->bhde', kc * jnp.exp(b_last - b), vc)
        return state, o

    state, o = lax.scan(step, s0, (_to_chunks(q, cs), _to_chunks(k, cs), _to_chunks(v, cs),
                                   _to_chunks(log_f, cs)))
    return _from_chunks(o), state


def bidirectional_scan(scan_fn, directions, per_token_decay):
    out_c, out_x = 0.0, 0.0
    for reverse, (ctx_args, lat_args) in zip((False, True), directions):
        def orient(args):
            if not reverse:
                return args
            q, k, v, d = args
            fl = lambda t: jnp.flip(t, axis=2)
            return fl(q), fl(k), fl(v), (fl(d) if per_token_decay else d)
        qc, kc, vc, dc = orient(ctx_args)
        s0 = jnp.zeros(qc.shape[:2] + (qc.shape[-1], vc.shape[-1]), jnp.float32)
        o_c, s_c = scan_fn(qc, kc, vc, dc, s0)
        o_x, _ = scan_fn(*orient(lat_args), s_c)
        if reverse:
            o_c, o_x = jnp.flip(o_c, axis=2), jnp.flip(o_x, axis=2)
        out_c = out_c + o_c
        out_x = out_x + o_x
    return out_c, out_x


def _heads(t, d):
    b, n, _ = t.shape
    return t.reshape(b, n, -1, d).transpose(0, 2, 1, 3)


def retention_hgrn_mixer(hx, hc, w_in, ret_decay_logit, hgrn_lb, w_out, rope):
    lb = hgrn_lb.astype(jnp.float32).reshape(HGRN_HEADS, 1, HGRN_DIM)

    def project(h, rope_tab):
        aq, ak, av, ag, bq, bff, bfb, bi, bg = jnp.split(h @ w_in, MIX_IN_SPLITS, axis=-1)
        aq = _heads(aq, RET_DIM)
        ak = _heads(ak, RET_DIM) * (RET_DIM ** -0.5)
        if rope_tab is not None:
            aq = apply_rope(aq, *rope_tab)
            ak = apply_rope(ak, *rope_tab)

        def forget(z):
            f = lb + (1.0 - lb) * jax.nn.sigmoid(_heads(z, HGRN_DIM).astype(jnp.float32))
            return 1.0 - f, jnp.log(f)
        k_f, g_f = forget(bff)
        k_b, g_b = forget(bfb)
        bq = jax.nn.silu(_heads(bq, HGRN_DIM))
        bi = _heads(bi, HGRN_DIM)
        return (aq, ak, _heads(av, RET_DIM)), (bq, k_f, bi, g_f), (bq, k_b, bi, g_b), ag, bg

    c_ret, c_fwd, c_bwd, c_ag, c_bg = project(hc, None)
    x_ret, x_fwd, x_bwd, x_ag, x_bg = project(hx, rope)
    log_gamma = jax.nn.log_sigmoid(ret_decay_logit.astype(jnp.float32))
    ret_c, ret_x = bidirectional_scan(
        retention_scan,
        [((*c_ret, log_gamma[0]), (*x_ret, log_gamma[0])),
         ((*c_ret, log_gamma[1]), (*x_ret, log_gamma[1]))],
        per_token_decay=False)
    hg_c, hg_x = bidirectional_scan(gated_scan, [(c_fwd, x_fwd), (c_bwd, x_bwd)], per_token_decay=True)

    def merge(ret, hg, ag, bg, dtype):
        b, _, n, _ = ret.shape
        flat = lambda t: rmsnorm(t).transpose(0, 2, 1, 3).reshape(b, n, -1).astype(dtype)
        y = jnp.concatenate([flat(ret) * jax.nn.silu(ag), flat(hg) * jax.nn.silu(bg)], axis=-1)
        return y @ w_out

    return merge(ret_x, hg_x, x_ag, x_bg, hx.dtype), merge(ret_c, hg_c, c_ag, c_bg, hc.dtype)


def _mla_queries(down, w_uq, q_norm_g, rope_tab):
    b, n, _ = down.shape
    q = (rmsnorm(down[..., :MLA_Q_RANK], q_norm_g) @ w_uq).reshape(b, n, MLA_HEADS, MLA_NOPE + MLA_ROPE)
    q_nope, q_rope = q[..., :MLA_NOPE], q[..., MLA_NOPE:]
    if rope_tab is not None:
        q_rope = apply_rope(q_rope, rope_tab[0][:, None, :], rope_tab[1][:, None, :])
    return q_nope, q_rope


def _mla_keys_values(down, w_ukv, kv_norm_g, rope_tab):
    b, n, _ = down.shape
    ckv = down[..., MLA_Q_RANK:MLA_Q_RANK + MLA_KV_RANK]
    k_rope = down[..., MLA_Q_RANK + MLA_KV_RANK:]
    kv = (rmsnorm(ckv, kv_norm_g) @ w_ukv).reshape(b, n, MLA_HEADS, MLA_NOPE + MLA_V)
    if rope_tab is not None:
        k_rope = apply_rope(k_rope, *rope_tab)
    return kv[..., :MLA_NOPE], k_rope, kv[..., MLA_NOPE:]


def block_attention(q_nope, q_rope, k_nope, k_rope, v, scale):
    b, n, h, dn = q_nope.shape
    nb = n // Q_BLOCK
    qn = q_nope.reshape(b, nb, Q_BLOCK, h, dn).swapaxes(0, 1)
    qr = q_rope.reshape(b, nb, Q_BLOCK, h, q_rope.shape[-1]).swapaxes(0, 1)

    def one_block(args):
        qn_b, qr_b = args
        s = jnp.einsum('bqhd,bkhd->bhqk', qn_b, k_nope) + jnp.einsum('bqhr,bkr->bhqk', qr_b, k_rope)
        p = jax.nn.softmax(s.astype(jnp.float32) * scale, axis=-1).astype(v.dtype)
        return jnp.einsum('bhqk,bkhd->bqhd', p, v)

    o = lax.map(one_block, (qn, qr))
    return o.swapaxes(0, 1).reshape(b, n, h * v.shape[-1])


def mla_mixer(hx, hc, w_down, q_norm_g, kv_norm_g, w_uq, w_ukv, w_out, rope, need_ctx_out):
    scale = (MLA_NOPE + MLA_ROPE) ** -0.5
    down_c = hc @ w_down
    down_x = hx @ w_down
    c_kn, c_kr, c_v = _mla_keys_values(down_c, w_ukv, kv_norm_g, None)
    x_kn, x_kr, x_v = _mla_keys_values(down_x, w_ukv, kv_norm_g, rope)
    x_qn, x_qr = _mla_queries(down_x, w_uq, q_norm_g, rope)
    k_nope = jnp.concatenate([c_kn, x_kn], axis=1)
    k_rope = jnp.concatenate([c_kr, x_kr], axis=1)
    v = jnp.concatenate([c_v, x_v], axis=1)
    out_x = block_attention(x_qn, x_qr, k_nope, k_rope, v, scale) @ w_out
    out_c = None
    if need_ctx_out:
        c_qn, c_qr = _mla_queries(down_c, w_uq, q_norm_g, None)
        out_c = block_attention(c_qn, c_qr, c_kn, c_kr, c_v, scale) @ w_out
    return out_x, out_c


def swiglu(h, w_in, w_out):
    a, b = jnp.split(h @ w_in, 2, axis=-1)
    return (jax.nn.silu(a) * b) @ w_out


def moe_swiglu(h, router_w, w_in, w_out):
    logits = (h @ router_w).astype(jnp.float32)
    top_v, top_i = lax.top_k(logits, TOP_K)
    weights = jax.nn.softmax(top_v, axis=-1)
    gates = jnp.sum(jax.nn.one_hot(top_i, N_EXPERTS, dtype=jnp.float32) * weights[..., None], axis=-2)
    gates = gates.astype(h.dtype)
    y = 0.0
    for e in range(N_EXPERTS):
        y = y + gates[..., e:e + 1] * swiglu(h, w_in[e], w_out[e])
    return y


def setup_inputs(seed: int = 0) -> dict:
    key = jax.random.key(seed)
    keys = iter(jax.random.split(key, 32))
    f32 = jnp.float32

    def normal(shape, scale=1.0):
        return scale * jax.random.normal(next(keys), shape, f32)

    def weight(shape, fan_in, gain=1.0):
        return normal(shape, gain * fan_in ** -0.5)

    gammas = 1.0 - 2.0 ** (-5.0 - jnp.arange(RET_HEADS, dtype=f32))
    ret_logit = jnp.log(gammas) - jnp.log1p(-gammas)
    mix_w = RET_WIDTH + HGRN_WIDTH
    return {
        'x': normal((BATCH, SEQ, D_MODEL)),
        'c': normal((BATCH, D_MODEL)),
        'ctx': normal((BATCH, CTX_LEN, D_MODEL)),
        'c_ctx': normal((D_MODEL,)),
        'mod_w': weight((DEPTH, D_MODEL, 6 * D_MODEL), D_MODEL, 0.5),
        'mod_b': normal((DEPTH, 6 * D_MODEL), 0.02),
        'norm_g': 1.0 + normal((DEPTH, 4, D_MODEL), 0.05),
        'mix_in_w': weight((N_EVEN, D_MODEL, MIX_IN_COLS), D_MODEL),
        'ret_decay_logit': ret_logit + normal((N_EVEN, 2, RET_HEADS), 0.05),
        'hgrn_lb_logit': normal((N_EVEN + 1, HGRN_WIDTH), 0.5),
        'mix_out_w': weight((N_EVEN, mix_w, D_MODEL), mix_w),
        'ffn_in_w': weight((N_EVEN, D_MODEL, 2 * FFN_DIM), D_MODEL),
        'ffn_out_w': weight((N_EVEN, FFN_DIM, D_MODEL), FFN_DIM),
        'mla_down_w': weight((N_ODD, D_MODEL, MLA_DOWN_COLS), D_MODEL),
        'mla_q_norm_g': 1.0 + normal((N_ODD, MLA_Q_RANK), 0.05),
        'mla_kv_norm_g': 1.0 + normal((N_ODD, MLA_KV_RANK), 0.05),
        'mla_uq_w': weight((N_ODD, MLA_Q_RANK, MLA_HEADS * (MLA_NOPE + MLA_ROPE)), MLA_Q_RANK),
        'mla_ukv_w': weight((N_ODD, MLA_KV_RANK, MLA_HEADS * (MLA_NOPE + MLA_V)), MLA_KV_RANK),
        'mla_out_w': weight((N_ODD, MLA_HEADS * MLA_V, D_MODEL), MLA_HEADS * MLA_V),
        'router_w': weight((N_ODD, D_MODEL, N_EXPERTS), D_MODEL),
        'moe_in_w': weight((N_ODD, N_EXPERTS, D_MODEL, 2 * MOE_DIM), D_MODEL),
        'moe_out_w': weight((N_ODD, N_EXPERTS, MOE_DIM, D_MODEL), MOE_DIM),
    }


def reference(x, c, ctx, c_ctx, mod_w, mod_b, norm_g, mix_in_w, ret_decay_logit, hgrn_lb_logit,
              mix_out_w, ffn_in_w, ffn_out_w, mla_down_w, mla_q_norm_g, mla_kv_norm_g, mla_uq_w,
              mla_ukv_w, mla_out_w, router_w, moe_in_w, moe_out_w):
    n_rows = x.shape[1] // GRID_W
    ret_rope = axial_rope_tables(n_rows, RET_DIM)
    mla_rope = axial_rope_tables(n_rows, MLA_ROPE)
    lb_all = jnp.cumsum(jax.nn.softmax(hgrn_lb_logit.astype(jnp.float32), axis=0), axis=0)
    silu_c = jax.nn.silu(c)
    silu_cc = jax.nn.silu(c_ctx)
    for layer in range(DEPTH):
        last = layer == DEPTH - 1
        j = layer // 2
        g = norm_g[layer]
        mx = [m[:, None, :] for m in jnp.split(silu_c @ mod_w[layer] + mod_b[layer], 6, axis=-1)]
        mc = jnp.split(silu_cc @ mod_w[layer] + mod_b[layer], 6, axis=-1)
        hx = rmsnorm(x, g[0]) * (1.0 + mx[1]) + mx[0]
        hc = rmsnorm(ctx, g[0]) * (1.0 + mc[1]) + mc[0]
        if layer % 2 == 0:
            ox, oc = retention_hgrn_mixer(hx, hc, mix_in_w[j], ret_decay_logit[j], lb_all[j],
                                          mix_out_w[j], ret_rope)
            ffn = lambda h, j=j: swiglu(h, ffn_in_w[j], ffn_out_w[j])
        else:
            ox, oc = mla_mixer(hx, hc, mla_down_w[j], mla_q_norm_g[j], mla_kv_norm_g[j], mla_uq_w[j],
                               mla_ukv_w[j], mla_out_w[j], mla_rope, need_ctx_out=not last)
            ffn = lambda h, j=j: moe_swiglu(h, router_w[j], moe_in_w[j], moe_out_w[j])
        x = x + mx[2] * rmsnorm(ox, g[1])
        if not last:
            ctx = ctx + mc[2] * rmsnorm(oc, g[1])
        hx = rmsnorm(x, g[2]) * (1.0 + mx[4]) + mx[3]
        x = x + mx[5] * rmsnorm(ffn(hx), g[3])
        if not last:
            hc = rmsnorm(ctx, g[2]) * (1.0 + mc[4]) + mc[3]
            ctx = ctx + mc[5] * rmsnorm(ffn(hc), g[3])
    return x
```

```python
import functools
import math

import jax
import jax.numpy as jnp
import numpy as np
from jax import lax
from jax.experimental import pallas as pl
from jax.experimental.pallas import tpu as pltpu

F32 = jnp.float32
BF16 = jnp.bfloat16

NORM_EPS = 1e-6
ROPE_BASE = 10000.0
GRID_W = 64
LANE = 128

RET_HEADS = 4
HGRN_HEADS = 4
HEAD_DIM = 128
RET_CHUNK = 128
HGRN_CHUNK = 128

MLA_HEADS = 8
MLA_NOPE = 128
MLA_ROPE = 64
MLA_V = 128
MLA_Q_RANK = 384
MLA_KV_RANK = 256
N_EXPERTS = 8

VMEM_LIMIT = 56 * 1024 * 1024


def _cparams(*sem):
    return pltpu.CompilerParams(dimension_semantics=sem, vmem_limit_bytes=VMEM_LIMIT)


def _mm(a, b):
    return jnp.dot(a.astype(BF16), b.astype(BF16), preferred_element_type=F32)


def _mm_nt(a, b):
    return lax.dot_general(a.astype(BF16), b.astype(BF16), (((1,), (1,)), ((), ())),
                           preferred_element_type=F32)


def _mm_tn(a, b):
    return lax.dot_general(a.astype(BF16), b.astype(BF16), (((0,), (0,)), ((), ())),
                           preferred_element_type=F32)


def _rms(x):
    return x * lax.rsqrt(jnp.mean(x * x, axis=-1, keepdims=True) + NORM_EPS)


def _silu(x):
    return x * (1.0 / (1.0 + jnp.exp(-x)))


def _modnorm(x, g, mod_ref, shift_row, scale_row):
    shift = mod_ref[shift_row:shift_row + 1, :]
    scale = mod_ref[scale_row:scale_row + 1, :]
    return _rms(x) * g * (1.0 + scale) + shift


def _mod_spec(mod):
    d = mod.shape[-1]
    if mod.shape[0] == 1:
        return pl.BlockSpec((None, 6, d), lambda b, *_: (0, 0, 0))
    return pl.BlockSpec((None, 6, d), lambda b, *_: (b, 0, 0))


def _pick(n, pref):
    t = min(pref, n)
    while n % t:
        t -= LANE
    return t


def _mod_kernel(c_ref, w_ref, b_ref, o_ref):
    a = _silu(c_ref[...])
    o_ref[...] = jnp.dot(a, w_ref[...], preferred_element_type=F32,
                         precision=lax.Precision.HIGHEST) + b_ref[...]


def _modulation(cond, mod_w, mod_b):
    n_layers, d, n = mod_w.shape
    r = cond.shape[0]
    tn = _pick(n, 1536)
    return pl.pallas_call(
        _mod_kernel,
        grid=(n_layers, n // tn),
        in_specs=[pl.BlockSpec((r, d), lambda l, j: (0, 0)),
                  pl.BlockSpec((None, d, tn), lambda l, j: (l, 0, j)),
                  pl.BlockSpec((None, 1, tn), lambda l, j: (l, 0, j))],
        out_specs=pl.BlockSpec((None, r, tn), lambda l, j: (l, 0, j)),
        out_shape=jax.ShapeDtypeStruct((n_layers, r, n), F32),
        compiler_params=_cparams("parallel", "parallel"),
        name="modulation",
    )(cond, mod_w, mod_b.reshape(n_layers, 1, n))


def _norm_matmul_kernel(x_ref, mod_ref, g_ref, w_ref, o_ref, h_ref, *, shift_row, scale_row):
    @pl.when(pl.program_id(2) == 0)
    def _():
        h_ref[...] = _modnorm(x_ref[...], g_ref[...], mod_ref, shift_row, scale_row).astype(BF16)

    o_ref[...] = jnp.dot(h_ref[...], w_ref[...], preferred_element_type=F32).astype(o_ref.dtype)


def _norm_matmul(x, mod, g, w, shift_row, scale_row, out_dtype, tm=512, tn=1536):
    b, t, d = x.shape
    n = w.shape[1]
    tm = _pick(t, tm)
    tn = _pick(n, tn)
    return pl.pallas_call(
        functools.partial(_norm_matmul_kernel, shift_row=shift_row, scale_row=scale_row),
        grid=(b, t // tm, n // tn),
        in_specs=[pl.BlockSpec((None, tm, d), lambda bi, i, j: (bi, i, 0)),
                  _mod_spec(mod),
                  pl.BlockSpec((1, d), lambda bi, i, j: (0, 0)),
                  pl.BlockSpec((d, tn), lambda bi, i, j: (0, j))],
        out_specs=pl.BlockSpec((None, tm, tn), lambda bi, i, j: (bi, i, j)),
        out_shape=jax.ShapeDtypeStruct((b, t, n), out_dtype),
        scratch_shapes=[pltpu.VMEM((tm, d), BF16)],
        compiler_params=_cparams("parallel", "parallel", "arbitrary"),
        name="norm_matmul",
    )(x, mod, g.reshape(1, d), w)


def _proj_residual_kernel(*refs, n_lhs, gate_row):
    lhs = refs[:n_lhs]
    ws = refs[n_lhs:2 * n_lhs]
    x_ref, mod_ref, g_ref, o_ref = refs[2 * n_lhs:]
    y = jnp.dot(lhs[0][...], ws[0][...], preferred_element_type=F32)
    for a_ref, w_ref in zip(lhs[1:], ws[1:]):
        y += jnp.dot(a_ref[...], w_ref[...], preferred_element_type=F32)
    gate = mod_ref[gate_row:gate_row + 1, :]
    o_ref[...] = x_ref[...] + gate * (_rms(y) * g_ref[...])


def _proj_residual(lhs, ws, x, mod, g, gate_row, tm=512):
    b, t, d = x.shape
    tm = _pick(t, tm)
    n_lhs = len(lhs)
    in_specs = [pl.BlockSpec((None, tm, a.shape[-1]), lambda bi, i: (bi, i, 0)) for a in lhs]
    in_specs += [pl.BlockSpec(w.shape, lambda bi, i: (0, 0)) for w in ws]
    in_specs += [pl.BlockSpec((None, tm, d), lambda bi, i: (bi, i, 0)),
                 _mod_spec(mod),
                 pl.BlockSpec((1, d), lambda bi, i: (0, 0))]
    return pl.pallas_call(
        functools.partial(_proj_residual_kernel, n_lhs=n_lhs, gate_row=gate_row),
        grid=(b, t // tm),
        in_specs=in_specs,
        out_specs=pl.BlockSpec((None, tm, d), lambda bi, i: (bi, i, 0)),
        out_shape=jax.ShapeDtypeStruct((b, t, d), F32),
        compiler_params=_cparams("parallel", "parallel"),
        name="proj_residual",
    )(*lhs, *ws, x, mod, g.reshape(1, d))


def _ffn_kernel(x_ref, mod_ref, g_in_ref, g_out_ref, wa_ref, wb_ref, wo_ref, o_ref, h_ref, acc_ref):
    f = pl.program_id(2)

    @pl.when(f == 0)
    def _():
        h_ref[...] = _modnorm(x_ref[...], g_in_ref[...], mod_ref, 3, 4).astype(BF16)
        acc_ref[...] = jnp.zeros_like(acc_ref)

    h = h_ref[...]
    a = jnp.dot(h, wa_ref[...], preferred_element_type=F32)
    bb = jnp.dot(h, wb_ref[...], preferred_element_type=F32)
    acc_ref[...] += jnp.dot((_silu(a) * bb).astype(BF16), wo_ref[...], preferred_element_type=F32)

    @pl.when(f == pl.num_programs(2) - 1)
    def _():
        gate = mod_ref[5:6, :]
        o_ref[...] = x_ref[...] + gate * (_rms(acc_ref[...]) * g_out_ref[...])


def _ffn(x, mod, g_in, g_out, w_in, w_out, tm=512, tf=1408):
    b, t, d = x.shape
    fdim = w_out.shape[0]
    tm = _pick(t, tm)
    tf = _pick(fdim, tf)
    nf = fdim // tf
    return pl.pallas_call(
        _ffn_kernel,
        grid=(b, t // tm, nf),
        in_specs=[pl.BlockSpec((None, tm, d), lambda bi, i, f: (bi, i, 0)),
                  _mod_spec(mod),
                  pl.BlockSpec((1, d), lambda bi, i, f: (0, 0)),
                  pl.BlockSpec((1, d), lambda bi, i, f: (0, 0)),
                  pl.BlockSpec((d, tf), lambda bi, i, f: (0, f)),
                  pl.BlockSpec((d, tf), lambda bi, i, f: (0, f + nf)),
                  pl.BlockSpec((tf, d), lambda bi, i, f: (f, 0))],
        out_specs=pl.BlockSpec((None, tm, d), lambda bi, i, f: (bi, i, 0)),
        out_shape=jax.ShapeDtypeStruct((b, t, d), F32),
        scratch_shapes=[pltpu.VMEM((tm, d), BF16), pltpu.VMEM((tm, d), F32)],
        compiler_params=_cparams("parallel", "parallel", "arbitrary"),
        name="ffn",
    )(x, mod, g_in.reshape(1, d), g_out.reshape(1, d), w_in, w_in, w_out)


def _retention_kernel(lg_ref, qc_ref, kc_ref, vc_ref, gc_ref, qx_ref, kx_ref, vx_ref, gx_ref,
                      cos_ref, sin_ref, yc_ref, yx_ref, oc_ref, ox_ref, s_ref):
    cs = RET_CHUNK
    head = pl.program_id(1)
    row = lax.broadcasted_iota(jnp.int32, (cs, cs), 0).astype(F32)
    col = lax.broadcasted_iota(jnp.int32, (cs, cs), 1).astype(F32)
    pos = lax.broadcasted_iota(jnp.int32, (cs, 1), 0).astype(F32)
    k_scale = HEAD_DIM ** -0.5

    def load(q_ref, k_ref, start, rope):
        q = q_ref[pl.ds(start, cs), :]
        k = k_ref[pl.ds(start, cs), :] * k_scale
        if rope:
            cos = cos_ref[pl.ds(start, cs), :]
            sin = sin_ref[pl.ds(start, cs), :]
            q = q * cos + pltpu.roll(q, HEAD_DIM // 2, 1) * sin
            k = k * cos + pltpu.roll(k, HEAD_DIM // 2, 1) * sin
        return q, k

    for reverse in (False, True):
        lg = lg_ref[1 if reverse else 0, head]
        rel = (col - row) if reverse else (row - col)
        intra = jnp.where(rel >= 0, jnp.exp(lg * jnp.maximum(rel, 0.0)), 0.0)
        p = (cs - 1.0 - pos) if reverse else pos
        q_dec = jnp.exp(lg * (p + 1.0))
        k_dec = jnp.exp(lg * (cs - 1.0 - p))
        c_dec = jnp.exp(lg * cs)
        s_ref[...] = jnp.zeros_like(s_ref)

        for (q_ref, k_ref, v_ref, g_ref, o_ref, y_ref, rope) in (
                (qc_ref, kc_ref, vc_ref, gc_ref, oc_ref, yc_ref, False),
                (qx_ref, kx_ref, vx_ref, gx_ref, ox_ref, yx_ref, True)):
            n_chunks = q_ref.shape[0] // cs

            def step(i, carry, q_ref=q_ref, k_ref=k_ref, v_ref=v_ref, g_ref=g_ref, o_ref=o_ref,
                     y_ref=y_ref, rope=rope, n_chunks=n_chunks, reverse=reverse, intra=intra,
                     q_dec=q_dec, k_dec=k_dec, c_dec=c_dec):
                idx = (n_chunks - 1 - i) if reverse else i
                start = pl.multiple_of(idx * cs, cs)
                q, k = load(q_ref, k_ref, start, rope)
                v = v_ref[pl.ds(start, cs), :]
                state = s_ref[...]
                att = _mm_nt(q, k) * intra
                o = _mm(att, v) + _mm(q * q_dec, state)
                s_ref[...] = state * c_dec + _mm_tn(k * k_dec, v)
                if not reverse:
                    o_ref[pl.ds(start, cs), :] = o
                else:
                    o = o + o_ref[pl.ds(start, cs), :]
                    gate = g_ref[pl.ds(start, cs), :]
                    y_ref[pl.ds(start, cs), :] = (_rms(o) * _silu(gate)).astype(y_ref.dtype)
                return carry

            lax.fori_loop(0, n_chunks, step, 0)


def _retention(proj_c, proj_x, log_gamma, cos2, sin2):
    b, tc, _ = proj_c.shape
    tx = proj_x.shape[1]
    h = RET_HEADS

    def col(t, group):
        return pl.BlockSpec((None, t, HEAD_DIM), lambda bi, hi, lg: (bi, 0, group * h + hi))

    grid_spec = pltpu.PrefetchScalarGridSpec(
        num_scalar_prefetch=1,
        grid=(b, h),
        in_specs=[col(tc, 0), col(tc, 1), col(tc, 2), col(tc, 3),
                  col(tx, 0), col(tx, 1), col(tx, 2), col(tx, 3),
                  pl.BlockSpec((tx, HEAD_DIM), lambda bi, hi, lg: (0, 0)),
                  pl.BlockSpec((tx, HEAD_DIM), lambda bi, hi, lg: (0, 0))],
        out_specs=[pl.BlockSpec((None, tc, HEAD_DIM), lambda bi, hi, lg: (bi, 0, hi)),
                   pl.BlockSpec((None, tx, HEAD_DIM), lambda bi, hi, lg: (bi, 0, hi))],
        scratch_shapes=[pltpu.VMEM((tc, HEAD_DIM), F32), pltpu.VMEM((tx, HEAD_DIM), F32),
                        pltpu.VMEM((HEAD_DIM, HEAD_DIM), F32)])
    return pl.pallas_call(
        _retention_kernel,
        grid_spec=grid_spec,
        out_shape=[jax.ShapeDtypeStruct((b, tc, h * HEAD_DIM), BF16),
                   jax.ShapeDtypeStruct((b, tx, h * HEAD_DIM), BF16)],
        compiler_params=_cparams("parallel", "parallel"),
        name="retention",
    )(log_gamma, proj_c, proj_c, proj_c, proj_c, proj_x, proj_x, proj_x, proj_x, cos2, sin2)


def _hgrn_levels(cs):
    return [cs >> (i + 1) for i in range(int(math.log2(cs)))]


def _hgrn_constants(cs):
    t = np.arange(cs)[:, None]
    s = np.arange(cs)[None, :]
    tri = np.stack([(s <= t), (s >= t)]).astype(np.float32)
    masks = []
    for reverse in (False, True):
        lv = []
        for h in _hgrn_levels(cs):
            same = (t // (2 * h)) == (s // (2 * h))
            t_hi = (t % (2 * h)) >= h
            s_hi = (s % (2 * h)) >= h
            lv.append(same & (~t_hi & s_hi if reverse else t_hi & ~s_hi))
        lv.append(t == s)
        masks.append(np.stack(lv))
    return jnp.asarray(tri, BF16), jnp.asarray(np.stack(masks).astype(np.float32))


def _split3(x):
    hi = x.astype(BF16)
    r = x - hi.astype(F32)
    mid = r.astype(BF16)
    lo = (r - mid.astype(F32)).astype(BF16)
    return hi, mid, lo


def _level_reference(b, h, reverse):
    cs = b.shape[0]
    ref_local = h if reverse else h - 1
    if 2 * h >= 8:
        b3 = b.reshape(cs // (2 * h), 2 * h, b.shape[1])
        r = b3[:, ref_local:ref_local + 1, :]
        return jnp.broadcast_to(r, b3.shape).reshape(b.shape)
    local = lax.broadcasted_iota(jnp.int32, b.shape, 0) % (2 * h)
    out = b
    for m in range(2 * h):
        if m != ref_local:
            out = jnp.where(local == m, pltpu.roll(b, (m - ref_local) % cs, 0), out)
    return out


def _hgrn_kernel(lb_ref, tri_ref, mask_ref,
                 qc_ref, zfc_ref, zbc_ref, vc_ref, gc_ref,
                 qx_ref, zfx_ref, zbx_ref, vx_ref, gx_ref,
                 yc_ref, yx_ref, oc_ref, ox_ref, st_ref):
    cs = HGRN_CHUNK
    levels = _hgrn_levels(cs)
    lb = lb_ref[...]

    for reverse in (False, True):
        d = 1 if reverse else 0
        st_ref[...] = jnp.zeros_like(st_ref)
        for (q_ref, zf_ref, zb_ref, v_ref, g_ref, o_ref, y_ref) in (
                (qc_ref, zfc_ref, zbc_ref, vc_ref, gc_ref, oc_ref, yc_ref),
                (qx_ref, zfx_ref, zbx_ref, vx_ref, gx_ref, ox_ref, yx_ref)):
            z_ref = zb_ref if reverse else zf_ref
            n_chunks = q_ref.shape[0] // cs

            def step(i, carry, q_ref=q_ref, z_ref=z_ref, v_ref=v_ref, g_ref=g_ref, o_ref=o_ref,
                     y_ref=y_ref, n_chunks=n_chunks, reverse=reverse, d=d):
                idx = (n_chunks - 1 - i) if reverse else i
                start = pl.multiple_of(idx * cs, cs)
                q = _silu(q_ref[pl.ds(start, cs), :])
                z = z_ref[pl.ds(start, cs), :]
                v = v_ref[pl.ds(start, cs), :]
                f = lb + (1.0 - lb) * (1.0 / (1.0 + jnp.exp(-z)))
                k = 1.0 - f
                logf = jnp.log(f)
                tri = tri_ref[d]
                hi, mid, lo = _split3(logf)
                bsum = (jnp.dot(tri, hi, preferred_element_type=F32)
                        + jnp.dot(tri, mid, preferred_element_type=F32)
                        + jnp.dot(tri, lo, preferred_element_type=F32))
                att = _mm_nt(q, k) * mask_ref[d, len(levels)]
                for li, h in enumerate(levels):
                    w = jnp.exp(-jnp.abs(bsum - _level_reference(bsum, h, reverse)))
                    att += _mm_nt(q * w, k * w) * mask_ref[d, li]
                state_t = st_ref[...]
                o = _mm(att, v) + _mm_nt(q * jnp.exp(bsum), state_t)
                b_end = bsum[0:1, :] if reverse else bsum[cs - 1:cs, :]
                st_ref[...] = state_t * jnp.exp(b_end) + _mm_tn(v, k * jnp.exp(b_end - bsum))
                if not reverse:
                    o_ref[pl.ds(start, cs), :] = o
                else:
                    o = o + o_ref[pl.ds(start, cs), :]
                    gate = g_ref[pl.ds(start, cs), :]
                    y_ref[pl.ds(start, cs), :] = (_rms(o) * _silu(gate)).astype(y_ref.dtype)
                return carry

            lax.fori_loop(0, n_chunks, step, 0)


def _hgrn(proj_c, proj_x, lb):
    b, tc, _ = proj_c.shape
    tx = proj_x.shape[1]
    h = HGRN_HEADS
    first = 4 * RET_HEADS
    tri, masks = _hgrn_constants(HGRN_CHUNK)

    def col(t, group):
        return pl.BlockSpec((None, t, HEAD_DIM), lambda bi, hi: (bi, 0, first + group * h + hi))

    ins = [col(tc, 0), col(tc, 1), col(tc, 2), col(tc, 3), col(tc, 4),
           col(tx, 0), col(tx, 1), col(tx, 2), col(tx, 3), col(tx, 4)]
    return pl.pallas_call(
        _hgrn_kernel,
        grid=(b, h),
        in_specs=[pl.BlockSpec((None, 1, HEAD_DIM), lambda bi, hi: (hi, 0, 0)),
                  pl.BlockSpec(tri.shape, lambda bi, hi: (0, 0, 0)),
                  pl.BlockSpec(masks.shape, lambda bi, hi: (0, 0, 0, 0))] + ins,
        out_specs=[pl.BlockSpec((None, tc, HEAD_DIM), lambda bi, hi: (bi, 0, hi)),
                   pl.BlockSpec((None, tx, HEAD_DIM), lambda bi, hi: (bi, 0, hi))],
        out_shape=[jax.ShapeDtypeStruct((b, tc, h * HEAD_DIM), BF16),
                   jax.ShapeDtypeStruct((b, tx, h * HEAD_DIM), BF16)],
        scratch_shapes=[pltpu.VMEM((tc, HEAD_DIM), F32), pltpu.VMEM((tx, HEAD_DIM), F32),
                        pltpu.VMEM((HEAD_DIM, HEAD_DIM), F32)],
        compiler_params=_cparams("parallel", "parallel"),
        name="hgrn2",
    )(lb.reshape(h, 1, HEAD_DIM), tri, masks, *([proj_c] * 5), *([proj_x] * 5))


MLA_QW = 2 * LANE
MLA_DOWN_PAD = MLA_Q_RANK + MLA_KV_RANK + LANE


def _mla_proj_kernel(x_ref, mod_ref, g_ref, wd_ref, gq_ref, gkv_ref, wq_ref, wkv_ref, cos_ref, sin_ref,
                     *out_refs, rope, want_q):
    h = _modnorm(x_ref[...], g_ref[...], mod_ref, 0, 1).astype(BF16)
    down = jnp.dot(h, wd_ref[...], preferred_element_type=F32)
    if rope:
        cos = cos_ref[...]
        sin = sin_ref[...]

    def rot(v):
        return v * cos + pltpu.roll(v, LANE // 2, 1) * sin if rope else v

    if want_q:
        q_ref, kn_ref, kr_ref, v_ref = out_refs
        qn = (_rms(down[:, :MLA_Q_RANK]) * gq_ref[...]).astype(BF16)
        q = jnp.dot(qn, wq_ref[...], preferred_element_type=F32)
        for hd in range(MLA_HEADS):
            lo = hd * MLA_QW
            q_ref[:, lo:lo + LANE] = q[:, lo:lo + LANE].astype(q_ref.dtype)
            q_ref[:, lo + LANE:lo + 2 * LANE] = rot(q[:, lo + LANE:lo + 2 * LANE]).astype(q_ref.dtype)
    else:
        kn_ref, kr_ref, v_ref = out_refs
    kvn = (_rms(down[:, MLA_Q_RANK:MLA_Q_RANK + MLA_KV_RANK]) * gkv_ref[...]).astype(BF16)
    kv = jnp.dot(kvn, wkv_ref[...], preferred_element_type=F32)
    nk = MLA_HEADS * MLA_NOPE
    kn_ref[...] = kv[:, :nk].astype(kn_ref.dtype)
    v_ref[...] = kv[:, nk:].astype(v_ref.dtype)
    kr_ref[...] = rot(down[:, MLA_Q_RANK + MLA_KV_RANK:]).astype(kr_ref.dtype)


def _mla_proj(x, mod, g, wd, gq, gkv, wq, wkv, cos, sin, rope, want_q, tm=512):
    b, t, d = x.shape
    tm = _pick(t, tm)
    const = lambda bi, i: (0, 0)
    tok = lambda bi, i: (bi, i, 0)
    nk, nv = MLA_HEADS * MLA_NOPE, MLA_HEADS * MLA_V
    out_shape = [jax.ShapeDtypeStruct((b, t, nk), BF16), jax.ShapeDtypeStruct((b, t, LANE), BF16),
                 jax.ShapeDtypeStruct((b, t, nv), BF16)]
    out_specs = [pl.BlockSpec((None, tm, nk), tok), pl.BlockSpec((None, tm, LANE), tok),
                 pl.BlockSpec((None, tm, nv), tok)]
    if want_q:
        out_shape = [jax.ShapeDtypeStruct((b, t, MLA_HEADS * MLA_QW), BF16)] + out_shape
        out_specs = [pl.BlockSpec((None, tm, MLA_HEADS * MLA_QW), tok)] + out_specs
    return pl.pallas_call(
        functools.partial(_mla_proj_kernel, rope=rope, want_q=want_q),
        grid=(b, t // tm),
        in_specs=[pl.BlockSpec((None, tm, d), tok), _mod_spec(mod), pl.BlockSpec((1, d), const),
                  pl.BlockSpec(wd.shape, const), pl.BlockSpec((1, MLA_Q_RANK), const),
                  pl.BlockSpec((1, MLA_KV_RANK), const), pl.BlockSpec(wq.shape, const),
                  pl.BlockSpec(wkv.shape, const),
                  pl.BlockSpec((tm, LANE), lambda bi, i: (i, 0)),
                  pl.BlockSpec((tm, LANE), lambda bi, i: (i, 0))],
        out_specs=out_specs,
        out_shape=out_shape,
        compiler_params=_cparams("parallel", "parallel"),
        name="mla_proj",
    )(x, mod, g.reshape(1, d), wd, gq.reshape(1, -1), gkv.reshape(1, -1), wq, wkv, cos, sin)


def _attention_kernel(q_ref, knc_ref, krc_ref, vc_ref, knx_ref, krx_ref, vx_ref, o_ref,
                      kc_ref, kx_ref, *, tk, exp_scale):
    @pl.when(pl.program_id(2) == 0)
    def _():
        kc_ref[:, :LANE] = knc_ref[...]
        kc_ref[:, LANE:] = krc_ref[...]
        kx_ref[:, :LANE] = knx_ref[...]
        kx_ref[:, LANE:] = krx_ref[...]

    q = q_ref[...]
    tq = q.shape[0]

    def block(k, v, m, l, acc):
        s = lax.dot_general(q, k, (((1,), (1,)), ((), ())), preferred_element_type=F32)
        m_new = jnp.maximum(m, jnp.max(s, axis=-1, keepdims=True))
        alpha = jnp.exp2((m - m_new) * exp_scale)
        p = jnp.exp2((s - m_new) * exp_scale)
        l = alpha * l + jnp.sum(p, axis=-1, keepdims=True)
        acc = alpha * acc + jnp.dot(p.astype(BF16), v, preferred_element_type=F32)
        return m_new, l, acc

    m0 = jnp.full((tq, 1), -jnp.inf, F32)
    l0 = jnp.zeros((tq, 1), F32)
    acc0 = jnp.zeros((tq, MLA_V), F32)
    carry = block(kc_ref[...], vc_ref[...], m0, l0, acc0)

    def step(i, carry):
        start = pl.multiple_of(i * tk, tk)
        return block(kx_ref[pl.ds(start, tk), :], vx_ref[pl.ds(start, tk), :], *carry)

    _, l, acc = lax.fori_loop(0, kx_ref.shape[0] // tk, step, carry)
    o_ref[...] = (acc / l).astype(o_ref.dtype)


def _attention(q, kn_c, kr_c, v_c, kn_x, kr_x, v_x, scale, tq=256, tk=512):
    b, t, _ = q.shape
    tc = kn_c.shape[1]
    tq = _pick(t, tq)
    tk = _pick(t, tk)
    head_c = lambda bi, hi, i: (bi, 0, hi)
    shared = lambda bi, hi, i: (bi, 0, 0)
    return pl.pallas_call(
        functools.partial(_attention_kernel, tk=tk, exp_scale=scale * math.log2(math.e)),
        grid=(b, MLA_HEADS, t // tq),
        in_specs=[pl.BlockSpec((None, tq, MLA_QW), lambda bi, hi, i: (bi, i, hi)),
                  pl.BlockSpec((None, tc, LANE), head_c), pl.BlockSpec((None, tc, LANE), shared),
                  pl.BlockSpec((None, tc, LANE), head_c),
                  pl.BlockSpec((None, t, LANE), head_c), pl.BlockSpec((None, t, LANE), shared),
                  pl.BlockSpec((None, t, LANE), head_c)],
        out_specs=pl.BlockSpec((None, tq, MLA_V), lambda bi, hi, i: (bi, i, hi)),
        out_shape=jax.ShapeDtypeStruct((b, t, MLA_HEADS * MLA_V), BF16),
        scratch_shapes=[pltpu.VMEM((tc, MLA_QW), BF16), pltpu.VMEM((t, MLA_QW), BF16)],
        compiler_params=_cparams("parallel", "parallel", "arbitrary"),
        name="mla_attention",
    )(q, kn_c, kr_c, v_c, kn_x, kr_x, v_x)


def _top2_gates(logits):
    lane = lax.broadcasted_iota(jnp.int32, logits.shape, 1)
    neg = jnp.float32(-jnp.inf)
    lg = jnp.where(lane < N_EXPERTS, logits, neg)
    m1 = jnp.max(lg, axis=-1, keepdims=True)
    i1 = jnp.min(jnp.where(lg == m1, lane, LANE), axis=-1, keepdims=True)
    lg2 = jnp.where(lane == i1, neg, lg)
    m2 = jnp.max(lg2, axis=-1, keepdims=True)
    i2 = jnp.min(jnp.where(lg2 == m2, lane, LANE), axis=-1, keepdims=True)
    e2 = jnp.exp(m2 - m1)
    w1 = 1.0 / (1.0 + e2)
    w2 = e2 / (1.0 + e2)
    return jnp.where(lane == i1, w1, 0.0) + jnp.where(lane == i2, w2, 0.0)


def _moe_kernel(x_ref, mod_ref, g_in_ref, g_out_ref, wr_ref, wa_ref, wb_ref, wo_ref, o_ref,
                h_ref, gates_ref, acc_ref):
    e = pl.program_id(2)
    f = pl.program_id(3)

    @pl.when((e == 0) & (f == 0))
    def _():
        h = _modnorm(x_ref[...], g_in_ref[...], mod_ref, 3, 4)
        h_ref[...] = h.astype(BF16)
        logits = jnp.dot(h, wr_ref[...], preferred_element_type=F32, precision=lax.Precision.HIGHEST)
        gates_ref[...] = _top2_gates(logits)
        acc_ref[...] = jnp.zeros_like(acc_ref)

    lane = lax.broadcasted_iota(jnp.int32, gates_ref.shape, 1)
    gate_e = jnp.sum(jnp.where(lane == e, gates_ref[...], 0.0), axis=-1, keepdims=True)
    h = h_ref[...]
    a = jnp.dot(h, wa_ref[...], preferred_element_type=F32)
    bb = jnp.dot(h, wb_ref[...], preferred_element_type=F32)
    acc_ref[...] += jnp.dot((_silu(a) * bb * gate_e).astype(BF16), wo_ref[...],
                            preferred_element_type=F32)

    @pl.when((e == pl.num_programs(2) - 1) & (f == pl.num_programs(3) - 1))
    def _():
        gate = mod_ref[5:6, :]
        o_ref[...] = x_ref[...] + gate * (_rms(acc_ref[...]) * g_out_ref[...])


def _moe(x, mod, g_in, g_out, router_pad, w_in, w_out, tm=512, tf=1408):
    b, t, d = x.shape
    n_exp, fdim, _ = w_out.shape
    tm = _pick(t, tm)
    tf = _pick(fdim, tf)
    nf = fdim // tf
    const = lambda bi, i, e, f: (0, 0)
    tok = lambda bi, i, e, f: (bi, i, 0)
    return pl.pallas_call(
        _moe_kernel,
        grid=(b, t // tm, n_exp, nf),
        in_specs=[pl.BlockSpec((None, tm, d), tok), _mod_spec(mod),
                  pl.BlockSpec((1, d), const), pl.BlockSpec((1, d), const),
                  pl.BlockSpec((d, LANE), const),
                  pl.BlockSpec((None, d, tf), lambda bi, i, e, f: (e, 0, f)),
                  pl.BlockSpec((None, d, tf), lambda bi, i, e, f: (e, 0, f + nf)),
                  pl.BlockSpec((None, tf, d), lambda bi, i, e, f: (e, f, 0))],
        out_specs=pl.BlockSpec((None, tm, d), tok),
        out_shape=jax.ShapeDtypeStruct((b, t, d), F32),
        scratch_shapes=[pltpu.VMEM((tm, d), BF16), pltpu.VMEM((tm, LANE), F32), pltpu.VMEM((tm, d), F32)],
        compiler_params=_cparams("parallel", "parallel", "arbitrary", "arbitrary"),
        name="moe",
    )(x, mod, g_in.reshape(1, d), g_out.reshape(1, d), router_pad, w_in, w_in, w_out)


def _axial_angles(n_tokens, dim):
    t = jnp.arange(n_tokens)
    rows = (t // GRID_W).astype(F32)
    cols = (t % GRID_W).astype(F32)
    n_freq = dim // 4
    inv = ROPE_BASE ** (-jnp.arange(n_freq, dtype=F32) / n_freq)
    return jnp.concatenate([rows[:, None] * inv, cols[:, None] * inv], axis=-1)


def _rope_tables(n_tokens, dim):
    ang = _axial_angles(n_tokens, dim)
    cos, sin = jnp.cos(ang), jnp.sin(ang)
    if dim == LANE:
        return jnp.concatenate([cos, cos], -1), jnp.concatenate([-sin, sin], -1)
    z = jnp.zeros_like(cos)
    return jnp.concatenate([cos, z, cos, z], -1), jnp.concatenate([-sin, z, sin, z], -1)


def _spread_rope_cols(w):
    half = MLA_ROPE // 2
    z = jnp.zeros(w.shape[:-1] + (half,), w.dtype)
    return jnp.concatenate([w[..., :half], z, w[..., half:], z], axis=-1)


def kernel(x, c, ctx, c_ctx, mod_w, mod_b, norm_g, mix_in_w, ret_decay_logit, hgrn_lb_logit, mix_out_w,
           ffn_in_w, ffn_out_w, mla_down_w, mla_q_norm_g, mla_kv_norm_g, mla_uq_w, mla_ukv_w, mla_out_w,
           router_w, moe_in_w, moe_out_w):
    batch, seq, d = x.shape
    assert mod_w.shape[0] == 2, "two layers: retention/HGRN2 then MLA/MoE"

    rows = -(-(batch + 1) // 8) * 8
    cond = jnp.zeros((rows, d), F32).at[:batch].set(c).at[batch].set(c_ctx)
    mod = _modulation(cond, mod_w, mod_b)
    mod_x = [mod[l, :batch].reshape(batch, 6, d) for l in range(2)]
    mod_c = [mod[l, batch:batch + 1].reshape(1, 6, d) for l in range(2)]

    g = norm_g[0]
    w_in = mix_in_w[0].astype(BF16)
    proj_x = _norm_matmul(x, mod_x[0], g[0], w_in, 0, 1, F32)
    proj_c = _norm_matmul(ctx, mod_c[0], g[0], w_in, 0, 1, F32)
    cos2, sin2 = _rope_tables(seq, HEAD_DIM)
    log_gamma = jax.nn.log_sigmoid(ret_decay_logit[0].astype(F32))
    lb = jnp.cumsum(jax.nn.softmax(hgrn_lb_logit.astype(F32), axis=0), axis=0)[0]
    ret_c, ret_x = _retention(proj_c, proj_x, log_gamma, cos2, sin2)
    hg_c, hg_x = _hgrn(proj_c, proj_x, lb)
    w_out = mix_out_w[0].astype(BF16)
    ret_w = RET_HEADS * HEAD_DIM
    w_outs = [w_out[:ret_w], w_out[ret_w:]]
    x = _proj_residual([ret_x, hg_x], w_outs, x, mod_x[0], g[1], 2)
    ctx = _proj_residual([ret_c, hg_c], w_outs, ctx, mod_c[0], g[1], 2)
    f_in, f_out = ffn_in_w[0].astype(BF16), ffn_out_w[0].astype(BF16)
    x = _ffn(x, mod_x[0], g[2], g[3], f_in, f_out)
    ctx = _ffn(ctx, mod_c[0], g[2], g[3], f_in, f_out)

    g = norm_g[1]
    dq = MLA_NOPE + MLA_ROPE
    wd = mla_down_w[0]
    wd = jnp.concatenate([wd[:, :MLA_Q_RANK + MLA_KV_RANK],
                          _spread_rope_cols(wd[:, MLA_Q_RANK + MLA_KV_RANK:])], axis=-1).astype(BF16)
    wq = mla_uq_w[0].reshape(MLA_Q_RANK, MLA_HEADS, dq)
    wq = jnp.concatenate([wq[..., :MLA_NOPE], _spread_rope_cols(wq[..., MLA_NOPE:])], axis=-1)
    wq = wq.reshape(MLA_Q_RANK, MLA_HEADS * MLA_QW).astype(BF16)
    wkv = mla_ukv_w[0].reshape(MLA_KV_RANK, MLA_HEADS, MLA_NOPE + MLA_V)
    wkv = jnp.concatenate([wkv[..., :MLA_NOPE].reshape(MLA_KV_RANK, -1),
                           wkv[..., MLA_NOPE:].reshape(MLA_KV_RANK, -1)], axis=-1).astype(BF16)
    cos_m, sin_m = _rope_tables(seq, MLA_ROPE)
    q, kn_x, kr_x, v_x = _mla_proj(x, mod_x[1], g[0], wd, mla_q_norm_g[0], mla_kv_norm_g[0], wq, wkv,
                                   cos_m, sin_m, rope=True, want_q=True)
    tc = ctx.shape[1]
    kn_c, kr_c, v_c = _mla_proj(ctx, mod_c[1], g[0], wd, mla_q_norm_g[0], mla_kv_norm_g[0], wq, wkv,
                                cos_m[:tc], sin_m[:tc], rope=False, want_q=False)
    att = _attention(q, kn_c, kr_c, v_c, kn_x, kr_x, v_x, dq ** -0.5)
    x = _proj_residual([att], [mla_out_w[0].astype(BF16)], x, mod_x[1], g[1], 2)
    router_pad = jnp.zeros((d, LANE), F32).at[:, :N_EXPERTS].set(router_w[0])
    x = _moe(x, mod_x[1], g[2], g[3], router_pad, moe_in_w[0].astype(BF16), moe_out_w[0].astype(BF16))
    return x
```

```python
import functools
import math

import jax
import jax.numpy as jnp
import numpy as np
from jax import lax
from jax.experimental import pallas as pl
from jax.experimental.pallas import tpu as pltpu

F32 = jnp.float32
BF16 = jnp.bfloat16

NORM_EPS = 1e-6
ROPE_BASE = 10000.0
GRID_W = 64
LANE = 128

RET_HEADS = 4
HGRN_HEADS = 4
HEAD_DIM = 128
RET_CHUNK = 128
HGRN_CHUNK = 128

MLA_HEADS = 8
MLA_NOPE = 128
MLA_ROPE = 64
MLA_V = 128
MLA_Q_RANK = 384
MLA_KV_RANK = 256
N_EXPERTS = 8

VMEM_LIMIT = 56 * 1024 * 1024


def _cparams(*sem):
    return pltpu.CompilerParams(dimension_semantics=sem, vmem_limit_bytes=VMEM_LIMIT)


def _mm(a, b):
    return jnp.dot(a.astype(BF16), b.astype(BF16), preferred_element_type=F32)


def _mm_nt(a, b):
    return lax.dot_general(a.astype(BF16), b.astype(BF16), (((1,), (1,)), ((), ())),
                           preferred_element_type=F32)


def _mm_tn(a, b):
    return lax.dot_general(a.astype(BF16), b.astype(BF16), (((0,), (0,)), ((), ())),
                           preferred_element_type=F32)


def _rms(x):
    return x * lax.rsqrt(jnp.mean(x * x, axis=-1, keepdims=True) + NORM_EPS)


def _silu(x):
    return x * (1.0 / (1.0 + jnp.exp(-x)))


def _modnorm(x, g, mod_ref, shift_row, scale_row):
    shift = mod_ref[shift_row:shift_row + 1, :]
    scale = mod_ref[scale_row:scale_row + 1, :]
    return _rms(x) * g * (1.0 + scale) + shift


def _mod_spec(mod):
    d = mod.shape[-1]
    if mod.shape[0] == 1:
        return pl.BlockSpec((None, 6, d), lambda b, *_: (0, 0, 0))
    return pl.BlockSpec((None, 6, d), lambda b, *_: (b, 0, 0))


def _pick(n, pref):
    t = min(pref, n)
    while n % t:
        t -= LANE
    return t


def _mod_kernel(c_ref, w_ref, b_ref, o_ref):
    a = _silu(c_ref[...])
    o_ref[...] = jnp.dot(a, w_ref[...], preferred_element_type=F32,
                         precision=lax.Precision.HIGHEST) + b_ref[...]


def _modulation(cond, mod_w, mod_b):
    n_layers, d, n = mod_w.shape
    r = cond.shape[0]
    tn = _pick(n, 1536)
    return pl.pallas_call(
        _mod_kernel,
        grid=(n_layers, n // tn),
        in_specs=[pl.BlockSpec((r, d), lambda l, j: (0, 0)),
                  pl.BlockSpec((None, d, tn), lambda l, j: (l, 0, j)),
                  pl.BlockSpec((None, 1, tn), lambda l, j: (l, 0, j))],
        out_specs=pl.BlockSpec((None, r, tn), lambda l, j: (l, 0, j)),
        out_shape=jax.ShapeDtypeStruct((n_layers, r, n), F32),
        compiler_params=_cparams("parallel", "parallel"),
        name="modulation",
    )(cond, mod_w, mod_b.reshape(n_layers, 1, n))


def _norm_matmul_kernel(x_ref, mod_ref, g_ref, w_ref, o_ref, h_ref, *, shift_row, scale_row):
    @pl.when(pl.program_id(2) == 0)
    def _():
        h_ref[...] = _modnorm(x_ref[...], g_ref[...], mod_ref, shift_row, scale_row).astype(BF16)

    o_ref[...] = jnp.dot(h_ref[...], w_ref[...], preferred_element_type=F32).astype(o_ref.dtype)


def _norm_matmul(x, mod, g, w, shift_row, scale_row, out_dtype, tm=512, tn=1536):
    b, t, d = x.shape
    n = w.shape[1]
    tm = _pick(t, tm)
    tn = _pick(n, tn)
    return pl.pallas_call(
        functools.partial(_norm_matmul_kernel, shift_row=shift_row, scale_row=scale_row),
        grid=(b, t // tm, n // tn),
        in_specs=[pl.BlockSpec((None, tm, d), lambda bi, i, j: (bi, i, 0)),
                  _mod_spec(mod),
                  pl.BlockSpec((1, d), lambda bi, i, j: (0, 0)),
                  pl.BlockSpec((d, tn), lambda bi, i, j: (0, j))],
        out_specs=pl.BlockSpec((None, tm, tn), lambda bi, i, j: (bi, i, j)),
        out_shape=jax.ShapeDtypeStruct((b, t, n), out_dtype),
        scratch_shapes=[pltpu.VMEM((tm, d), BF16)],
        compiler_params=_cparams("parallel", "parallel", "arbitrary"),
        name="norm_matmul",
    )(x, mod, g.reshape(1, d), w)


def _proj_residual_kernel(*refs, n_lhs, gate_row):
    lhs = refs[:n_lhs]
    ws = refs[n_lhs:2 * n_lhs]
    x_ref, mod_ref, g_ref, o_ref = refs[2 * n_lhs:]
    y = jnp.dot(lhs[0][...], ws[0][...], preferred_element_type=F32)
    for a_ref, w_ref in zip(lhs[1:], ws[1:]):
        y += jnp.dot(a_ref[...], w_ref[...], preferred_element_type=F32)
    gate = mod_ref[gate_row:gate_row + 1, :]
    o_ref[...] = x_ref[...] + gate * (_rms(y) * g_ref[...])


def _proj_residual(lhs, ws, x, mod, g, gate_row, tm=512):
    b, t, d = x.shape
    tm = _pick(t, tm)
    n_lhs = len(lhs)
    in_specs = [pl.BlockSpec((None, tm, a.shape[-1]), lambda bi, i: (bi, i, 0)) for a in lhs]
    in_specs += [pl.BlockSpec(w.shape, lambda bi, i: (0, 0)) for w in ws]
    in_specs += [pl.BlockSpec((None, tm, d), lambda bi, i: (bi, i, 0)),
                 _mod_spec(mod),
                 pl.BlockSpec((1, d), lambda bi, i: (0, 0))]
    return pl.pallas_call(
        functools.partial(_proj_residual_kernel, n_lhs=n_lhs, gate_row=gate_row),
        grid=(b, t // tm),
        in_specs=in_specs,
        out_specs=pl.BlockSpec((None, tm, d), lambda bi, i: (bi, i, 0)),
        out_shape=jax.ShapeDtypeStruct((b, t, d), F32),
        compiler_params=_cparams("parallel", "parallel"),
        name="proj_residual",
    )(*lhs, *ws, x, mod, g.reshape(1, d))


def _ffn_kernel(x_ref, mod_ref, g_in_ref, g_out_ref, wa_ref, wb_ref, wo_ref, o_ref, h_ref, acc_ref):
    f = pl.program_id(2)

    @pl.when(f == 0)
    def _():
        h_ref[...] = _modnorm(x_ref[...], g_in_ref[...], mod_ref, 3, 4).astype(BF16)
        acc_ref[...] = jnp.zeros_like(acc_ref)

    h = h_ref[...]
    a = jnp.dot(h, wa_ref[...], preferred_element_type=F32)
    bb = jnp.dot(h, wb_ref[...], preferred_element_type=F32)
    acc_ref[...] += jnp.dot((_silu(a) * bb).astype(BF16), wo_ref[...], preferred_element_type=F32)

    @pl.when(f == pl.num_programs(2) - 1)
    def _():
        gate = mod_ref[5:6, :]
        o_ref[...] = x_ref[...] + gate * (_rms(acc_ref[...]) * g_out_ref[...])


def _ffn(x, mod, g_in, g_out, w_in, w_out, tm=512, tf=1408):
    b, t, d = x.shape
    fdim = w_out.shape[0]
    tm = _pick(t, tm)
    tf = _pick(fdim, tf)
    nf = fdim // tf
    return pl.pallas_call(
        _ffn_kernel,
        grid=(b, t // tm, nf),
        in_specs=[pl.BlockSpec((None, tm, d), lambda bi, i, f: (bi, i, 0)),
                  _mod_spec(mod),
                  pl.BlockSpec((1, d), lambda bi, i, f: (0, 0)),
                  pl.BlockSpec((1, d), lambda bi, i, f: (0, 0)),
                  pl.BlockSpec((d, tf), lambda bi, i, f: (0, f)),
                  pl.BlockSpec((d, tf), lambda bi, i, f: (0, f + nf)),
                  pl.BlockSpec((tf, d), lambda bi, i, f: (f, 0))],
        out_specs=pl.BlockSpec((None, tm, d), lambda bi, i, f: (bi, i, 0)),
        out_shape=jax.ShapeDtypeStruct((b, t, d), F32),
        scratch_shapes=[pltpu.VMEM((tm, d), BF16), pltpu.VMEM((tm, d), F32)],
        compiler_params=_cparams("parallel", "parallel", "arbitrary"),
        name="ffn",
    )(x, mod, g_in.reshape(1, d), g_out.reshape(1, d), w_in, w_in, w_out)


def _retention_kernel(lg_ref, qc_ref, kc_ref, vc_ref, gc_ref, qx_ref, kx_ref, vx_ref, gx_ref,
                      cos_ref, sin_ref, yc_ref, yx_ref, oc_ref, ox_ref, s_ref):
    cs = RET_CHUNK
    head = pl.program_id(1)
    row = lax.broadcasted_iota(jnp.int32, (cs, cs), 0).astype(F32)
    col = lax.broadcasted_iota(jnp.int32, (cs, cs), 1).astype(F32)
    pos = lax.broadcasted_iota(jnp.int32, (cs, 1), 0).astype(F32)
    k_scale = HEAD_DIM ** -0.5

    def load(q_ref, k_ref, start, rope):
        q = q_ref[pl.ds(start, cs), :]
        k = k_ref[pl.ds(start, cs), :] * k_scale
        if rope:
            cos = cos_ref[pl.ds(start, cs), :]
            sin = sin_ref[pl.ds(start, cs), :]
            q = q * cos + pltpu.roll(q, HEAD_DIM // 2, 1) * sin
            k = k * cos + pltpu.roll(k, HEAD_DIM // 2, 1) * sin
        return q, k

    for reverse in (False, True):
        lg = lg_ref[1 if reverse else 0, head]
        rel = (col - row) if reverse else (row - col)
        intra = jnp.where(rel >= 0, jnp.exp(lg * jnp.maximum(rel, 0.0)), 0.0)
        p = (cs - 1.0 - pos) if reverse else pos
        q_dec = jnp.exp(lg * (p + 1.0))
        k_dec = jnp.exp(lg * (cs - 1.0 - p))
        c_dec = jnp.exp(lg * cs)
        s_ref[...] = jnp.zeros_like(s_ref)

        for (q_ref, k_ref, v_ref, g_ref, o_ref, y_ref, rope) in (
                (qc_ref, kc_ref, vc_ref, gc_ref, oc_ref, yc_ref, False),
                (qx_ref, kx_ref, vx_ref, gx_ref, ox_ref, yx_ref, True)):
            n_chunks = q_ref.shape[0] // cs

            def step(i, carry, q_ref=q_ref, k_ref=k_ref, v_ref=v_ref, g_ref=g_ref, o_ref=o_ref,
                     y_ref=y_ref, rope=rope, n_chunks=n_chunks, reverse=reverse, intra=intra,
                     q_dec=q_dec, k_dec=k_dec, c_dec=c_dec):
                idx = (n_chunks - 1 - i) if reverse else i
                start = pl.multiple_of(idx * cs, cs)
                q, k = load(q_ref, k_ref, start, rope)
                v = v_ref[pl.ds(start, cs), :]
                state = s_ref[...]
                att = _mm_nt(q, k) * intra
                o = _mm(att, v) + _mm(q * q_dec, state)
                s_ref[...] = state * c_dec + _mm_tn(k * k_dec, v)
                if not reverse:
                    o_ref[pl.ds(start, cs), :] = o
                else:
                    o = o + o_ref[pl.ds(start, cs), :]
                    gate = g_ref[pl.ds(start, cs), :]
                    y_ref[pl.ds(start, cs), :] = (_rms(o) * _silu(gate)).astype(y_ref.dtype)
                return carry

            lax.fori_loop(0, n_chunks, step, 0)


def _retention(proj_c, proj_x, log_gamma, cos2, sin2):
    b, tc, _ = proj_c.shape
    tx = proj_x.shape[1]
    h = RET_HEADS

    def col(t, group):
        return pl.BlockSpec((None, t, HEAD_DIM), lambda bi, hi, lg: (bi, 0, group * h + hi))

    grid_spec = pltpu.PrefetchScalarGridSpec(
        num_scalar_prefetch=1,
        grid=(b, h),
        in_specs=[col(tc, 0), col(tc, 1), col(tc, 2), col(tc, 3),
                  col(tx, 0), col(tx, 1), col(tx, 2), col(tx, 3),
                  pl.BlockSpec((tx, HEAD_DIM), lambda bi, hi, lg: (0, 0)),
                  pl.BlockSpec((tx, HEAD_DIM), lambda bi, hi, lg: (0, 0))],
        out_specs=[pl.BlockSpec((None, tc, HEAD_DIM), lambda bi, hi, lg: (bi, 0, hi)),
                   pl.BlockSpec((None, tx, HEAD_DIM), lambda bi, hi, lg: (bi, 0, hi))],
        scratch_shapes=[pltpu.VMEM((tc, HEAD_DIM), F32), pltpu.VMEM((tx, HEAD_DIM), F32),
                        pltpu.VMEM((HEAD_DIM, HEAD_DIM), F32)])
    return pl.pallas_call(
        _retention_kernel,
        grid_spec=grid_spec,
        out_shape=[jax.ShapeDtypeStruct((b, tc, h * HEAD_DIM), BF16),
                   jax.ShapeDtypeStruct((b, tx, h * HEAD_DIM), BF16)],
        compiler_params=_cparams("parallel", "parallel"),
        name="retention",
    )(log_gamma, proj_c, proj_c, proj_c, proj_c, proj_x, proj_x, proj_x, proj_x, cos2, sin2)


def _hgrn_levels(cs):
    return [cs >> (i + 1) for i in range(int(math.log2(cs)))]


def _hgrn_constants(cs):
    t = np.arange(cs)[:, None]
    s = np.arange(cs)[None, :]
    tri = np.stack([(s <= t), (s >= t)]).astype(np.float32)
    masks = []
    for reverse in (False, True):
        lv = []
        for h in _hgrn_levels(cs):
            same = (t // (2 * h)) == (s // (2 * h))
            t_hi = (t % (2 * h)) >= h
            s_hi = (s % (2 * h)) >= h
            lv.append(same & (~t_hi & s_hi if reverse else t_hi & ~s_hi))
        lv.append(t == s)
        masks.append(np.stack(lv))
    return jnp.asarray(tri, BF16), jnp.asarray(np.stack(masks).astype(np.float32))


def _split3(x):
    hi = x.astype(BF16)
    r = x - hi.astype(F32)
    mid = r.astype(BF16)
    lo = (r - mid.astype(F32)).astype(BF16)
    return hi, mid, lo


def _level_reference(b, h, reverse):
    cs = b.shape[0]
    ref_local = h if reverse else h - 1
    if 2 * h >= 8:
        b3 = b.reshape(cs // (2 * h), 2 * h, b.shape[1])
        r = b3[:, ref_local:ref_local + 1, :]
        return jnp.broadcast_to(r, b3.shape).reshape(b.shape)
    local = lax.broadcasted_iota(jnp.int32, b.shape, 0) % (2 * h)
    out = b
    for m in range(2 * h):
        if m != ref_local:
            out = jnp.where(local == m, pltpu.roll(b, (m - ref_local) % cs, 0), out)
    return out


def _hgrn_kernel(lb_ref, tri_ref, mask_ref,
                 qc_ref, zfc_ref, zbc_ref, vc_ref, gc_ref,
                 qx_ref, zfx_ref, zbx_ref, vx_ref, gx_ref,
                 yc_ref, yx_ref, oc_ref, ox_ref, st_ref):
    cs = HGRN_CHUNK
    levels = _hgrn_levels(cs)
    lb = lb_ref[...]

    for reverse in (False, True):
        d = 1 if reverse else 0
        st_ref[...] = jnp.zeros_like(st_ref)
        for (q_ref, zf_ref, zb_ref, v_ref, g_ref, o_ref, y_ref) in (
                (qc_ref, zfc_ref, zbc_ref, vc_ref, gc_ref, oc_ref, yc_ref),
                (qx_ref, zfx_ref, zbx_ref, vx_ref, gx_ref, ox_ref, yx_ref)):
            z_ref = zb_ref if reverse else zf_ref
            n_chunks = q_ref.shape[0] // cs

            def step(i, carry, q_ref=q_ref, z_ref=z_ref, v_ref=v_ref, g_ref=g_ref, o_ref=o_ref,
                     y_ref=y_ref, n_chunks=n_chunks, reverse=reverse, d=d):
                idx = (n_chunks - 1 - i) if reverse else i
                start = pl.multiple_of(idx * cs, cs)
                q = _silu(q_ref[pl.ds(start, cs), :])
                z = z_ref[pl.ds(start, cs), :]
                v = v_ref[pl.ds(start, cs), :]
                f = lb + (1.0 - lb) * (1.0 / (1.0 + jnp.exp(-z)))
                k = 1.0 - f
                logf = jnp.log(f)
                tri = tri_ref[d]
                hi, mid, lo = _split3(logf)
                bsum = (jnp.dot(tri, hi, preferred_element_type=F32)
                        + jnp.dot(tri, mid, preferred_element_type=F32)
                        + jnp.dot(tri, lo, preferred_element_type=F32))
                att = _mm_nt(q, k) * mask_ref[d, len(levels)]
                for li, h in enumerate(levels):
                    w = jnp.exp(-jnp.abs(bsum - _level_reference(bsum, h, reverse)))
                    att += _mm_nt(q * w, k * w) * mask_ref[d, li]
                state_t = st_ref[...]
                o = _mm(att, v) + _mm_nt(q * jnp.exp(bsum), state_t)
                b_end = bsum[0:1, :] if reverse else bsum[cs - 1:cs, :]
                st_ref[...] = state_t * jnp.exp(b_end) + _mm_tn(v, k * jnp.exp(b_end - bsum))
                if not reverse:
                    o_ref[pl.ds(start, cs), :] = o
                else:
                    o = o + o_ref[pl.ds(start, cs), :]
                    gate = g_ref[pl.ds(start, cs), :]
                    y_ref[pl.ds(start, cs), :] = (_rms(o) * _silu(gate)).astype(y_ref.dtype)
                return carry

            lax.fori_loop(0, n_chunks, step, 0)


def _hgrn(proj_c, proj_x, lb):
    b, tc, _ = proj_c.shape
    tx = proj_x.shape[1]
    h = HGRN_HEADS
    first = 4 * RET_HEADS
    tri, masks = _hgrn_constants(HGRN_CHUNK)

    def col(t, group):
        return pl.BlockSpec((None, t, HEAD_DIM), lambda bi, hi: (bi, 0, first + group * h + hi))

    ins = [col(tc, 0), col(tc, 1), col(tc, 2), col(tc, 3), col(tc, 4),
           col(tx, 0), col(tx, 1), col(tx, 2), col(tx, 3), col(tx, 4)]
    return pl.pallas_call(
        _hgrn_kernel,
        grid=(b, h),
        in_specs=[pl.BlockSpec((None, 1, HEAD_DIM), lambda bi, hi: (hi, 0, 0)),
                  pl.BlockSpec(tri.shape, lambda bi, hi: (0, 0, 0)),
                  pl.BlockSpec(masks.shape, lambda bi, hi: (0, 0, 0, 0))] + ins,
        out_specs=[pl.BlockSpec((None, tc, HEAD_DIM), lambda bi, hi: (bi, 0, hi)),
                   pl.BlockSpec((None, tx, HEAD_DIM), lambda bi, hi: (bi, 0, hi))],
        out_shape=[jax.ShapeDtypeStruct((b, tc, h * HEAD_DIM), BF16),
                   jax.ShapeDtypeStruct((b, tx, h * HEAD_DIM), BF16)],
        scratch_shapes=[pltpu.VMEM((tc, HEAD_DIM), F32), pltpu.VMEM((tx, HEAD_DIM), F32),
                        pltpu.VMEM((HEAD_DIM, HEAD_DIM), F32)],
        compiler_params=_cparams("parallel", "parallel"),
        name="hgrn2",
    )(lb.reshape(h, 1, HEAD_DIM), tri, masks, *([proj_c] * 5), *([proj_x] * 5))


MLA_QW = 2 * LANE
MLA_DOWN_PAD = MLA_Q_RANK + MLA_KV_RANK + LANE


def _mla_proj_kernel(x_ref, mod_ref, g_ref, wd_ref, gq_ref, gkv_ref, wq_ref, wkv_ref, cos_ref, sin_ref,
                     *out_refs, rope, want_q):
    h = _modnorm(x_ref[...], g_ref[...], mod_ref, 0, 1).astype(BF16)
    down = jnp.dot(h, wd_ref[...], preferred_element_type=F32)
    if rope:
        cos = cos_ref[...]
        sin = sin_ref[...]

    def rot(v):
        return v * cos + pltpu.roll(v, LANE // 2, 1) * sin if rope else v

    if want_q:
        q_ref, kn_ref, kr_ref, v_ref = out_refs
        qn = (_rms(down[:, :MLA_Q_RANK]) * gq_ref[...]).astype(BF16)
        q = jnp.dot(qn, wq_ref[...], preferred_element_type=F32)
        for hd in range(MLA_HEADS):
            lo = hd * MLA_QW
            q_ref[:, lo:lo + LANE] = q[:, lo:lo + LANE].astype(q_ref.dtype)
            q_ref[:, lo + LANE:lo + 2 * LANE] = rot(q[:, lo + LANE:lo + 2 * LANE]).astype(q_ref.dtype)
    else:
        kn_ref, kr_ref, v_ref = out_refs
    kvn = (_rms(down[:, MLA_Q_RANK:MLA_Q_RANK + MLA_KV_RANK]) * gkv_ref[...]).astype(BF16)
    kv = jnp.dot(kvn, wkv_ref[...], preferred_element_type=F32)
    nk = MLA_HEADS * MLA_NOPE
    kn_ref[...] = kv[:, :nk].astype(kn_ref.dtype)
    v_ref[...] = kv[:, nk:].astype(v_ref.dtype)
    kr_ref[...] = rot(down[:, MLA_Q_RANK + MLA_KV_RANK:]).astype(kr_ref.dtype)


def _mla_proj(x, mod, g, wd, gq, gkv, wq, wkv, cos, sin, rope, want_q, tm=512):
    b, t, d = x.shape
    tm = _pick(t, tm)
    const = lambda bi, i: (0, 0)
    tok = lambda bi, i: (bi, i, 0)
    nk, nv = MLA_HEADS * MLA_NOPE, MLA_HEADS * MLA_V
    out_shape = [jax.ShapeDtypeStruct((b, t, nk), BF16), jax.ShapeDtypeStruct((b, t, LANE), BF16),
                 jax.ShapeDtypeStruct((b, t, nv), BF16)]
    out_specs = [pl.BlockSpec((None, tm, nk), tok), pl.BlockSpec((None, tm, LANE), tok),
                 pl.BlockSpec((None, tm, nv), tok)]
    if want_q:
        out_shape = [jax.ShapeDtypeStruct((b, t, MLA_HEADS * MLA_QW), BF16)] + out_shape
        out_specs = [pl.BlockSpec((None, tm, MLA_HEADS * MLA_QW), tok)] + out_specs
    return pl.pallas_call(
        functools.partial(_mla_proj_kernel, rope=rope, want_q=want_q),
        grid=(b, t // tm),
        in_specs=[pl.BlockSpec((None, tm, d), tok), _mod_spec(mod), pl.BlockSpec((1, d), const),
                  pl.BlockSpec(wd.shape, const), pl.BlockSpec((1, MLA_Q_RANK), const),
                  pl.BlockSpec((1, MLA_KV_RANK), const), pl.BlockSpec(wq.shape, const),
                  pl.BlockSpec(wkv.shape, const),
                  pl.BlockSpec((tm, LANE), lambda bi, i: (i, 0)),
                  pl.BlockSpec((tm, LANE), lambda bi, i: (i, 0))],
        out_specs=out_specs,
        out_shape=out_shape,
        compiler_params=_cparams("parallel", "parallel"),
        name="mla_proj",
    )(x, mod, g.reshape(1, d), wd, gq.reshape(1, -1), gkv.reshape(1, -1), wq, wkv, cos, sin)


def _lane_fold(x, op):
    out = x[:, :LANE]
    for j in range(1, x.shape[1] // LANE):
        out = op(out, x[:, j * LANE:(j + 1) * LANE])
    return out


def _attention_kernel(q_ref, knc_ref, krc_ref, vc_ref, knx_ref, krx_ref, vx_ref, o_ref,
                      kc_ref, kx_ref, sc0_ref, sc1_ref, sx0_ref, sx1_ref, m0_ref, m1_ref, *, tk, exp_scale):
    step = pl.program_id(2)

    @pl.when(step == 0)
    def _():
        kc_ref[:, :LANE] = knc_ref[...]
        kc_ref[:, LANE:] = krc_ref[...]
        kx_ref[:, :LANE] = knx_ref[...]
        kx_ref[:, LANE:] = krx_ref[...]
        sc1_ref[...] = jnp.zeros(sc1_ref.shape, F32)
        sx1_ref[...] = jnp.zeros(sx1_ref.shape, F32)
        m1_ref[...] = jnp.zeros(m1_ref.shape, F32)

    n_blocks = kx_ref.shape[0] // tk
    nt = (((1,), (1,)), ((), ()))

    def body(sc_cur, sx_cur, m_cur, sc_prev, sx_prev, m_prev):
        q = q_ref[...]
        m = jnp.max(m_prev[...], axis=-1, keepdims=True)
        s = lax.dot_general(q, kc_ref[...], nt, preferred_element_type=F32)
        sc_cur[...] = s
        m_part = _lane_fold(s, jnp.maximum)
        p = jnp.exp2((sc_prev[...] - m) * exp_scale)
        l_part = _lane_fold(p, jnp.add)
        acc = jnp.dot(p.astype(BF16), vc_ref[...], preferred_element_type=F32)
        for i in range(n_blocks):
            s = lax.dot_general(q, kx_ref[i * tk:(i + 1) * tk, :], nt, preferred_element_type=F32)
            sx_cur[i] = s
            m_part = jnp.maximum(m_part, _lane_fold(s, jnp.maximum))
            p = jnp.exp2((sx_prev[i] - m) * exp_scale)
            l_part = l_part + _lane_fold(p, jnp.add)
            acc = acc + jnp.dot(p.astype(BF16), vx_ref[i * tk:(i + 1) * tk, :],
                                preferred_element_type=F32)
        m_cur[...] = m_part
        l = jnp.sum(l_part, axis=-1, keepdims=True)
        o_ref[...] = (acc / l).astype(o_ref.dtype)

    @pl.when(step % 2 == 0)
    def _():
        body(sc0_ref, sx0_ref, m0_ref, sc1_ref, sx1_ref, m1_ref)

    @pl.when(step % 2 == 1)
    def _():
        body(sc1_ref, sx1_ref, m1_ref, sc0_ref, sx0_ref, m0_ref)


def _attention(q, kn_c, kr_c, v_c, kn_x, kr_x, v_x, scale, tq=256, tk=512):
    b, t, _ = q.shape
    tc = kn_c.shape[1]
    tq = _pick(t, tq)
    tk = _pick(t, tk)
    nq = t // tq
    head_c = lambda bi, hi, i: (bi, 0, hi)
    shared = lambda bi, hi, i: (bi, 0, 0)
    return pl.pallas_call(
        functools.partial(_attention_kernel, tk=tk, exp_scale=scale * math.log2(math.e)),
        grid=(b, MLA_HEADS, nq + 1),
        in_specs=[pl.BlockSpec((None, tq, MLA_QW), lambda bi, hi, i: (bi, jnp.minimum(i, nq - 1), hi)),
                  pl.BlockSpec((None, tc, LANE), head_c), pl.BlockSpec((None, tc, LANE), shared),
                  pl.BlockSpec((None, tc, LANE), head_c),
                  pl.BlockSpec((None, t, LANE), head_c), pl.BlockSpec((None, t, LANE), shared),
                  pl.BlockSpec((None, t, LANE), head_c)],
        out_specs=pl.BlockSpec((None, tq, MLA_V), lambda bi, hi, i: (bi, jnp.maximum(i - 1, 0), hi)),
        out_shape=jax.ShapeDtypeStruct((b, t, MLA_HEADS * MLA_V), BF16),
        scratch_shapes=[pltpu.VMEM((tc, MLA_QW), BF16), pltpu.VMEM((t, MLA_QW), BF16),
                        pltpu.VMEM((tq, tc), F32), pltpu.VMEM((tq, tc), F32),
                        pltpu.VMEM((t // tk, tq, tk), F32), pltpu.VMEM((t // tk, tq, tk), F32),
                        pltpu.VMEM((tq, LANE), F32), pltpu.VMEM((tq, LANE), F32)],
        compiler_params=_cparams("parallel", "parallel", "arbitrary"),
        name="mla_attention",
    )(q, kn_c, kr_c, v_c, kn_x, kr_x, v_x)


def _top2_gates(logits):
    lane = lax.broadcasted_iota(jnp.int32, logits.shape, 1)
    neg = jnp.float32(-jnp.inf)
    lg = jnp.where(lane < N_EXPERTS, logits, neg)
    m1 = jnp.max(lg, axis=-1, keepdims=True)
    i1 = jnp.min(jnp.where(lg == m1, lane, LANE), axis=-1, keepdims=True)
    lg2 = jnp.where(lane == i1, neg, lg)
    m2 = jnp.max(lg2, axis=-1, keepdims=True)
    i2 = jnp.min(jnp.where(lg2 == m2, lane, LANE), axis=-1, keepdims=True)
    e2 = jnp.exp(m2 - m1)
    w1 = 1.0 / (1.0 + e2)
    w2 = e2 / (1.0 + e2)
    return jnp.where(lane == i1, w1, 0.0) + jnp.where(lane == i2, w2, 0.0)


def _moe_kernel(x_ref, mod_ref, g_in_ref, g_out_ref, wr_ref, wa_ref, wb_ref, wo_ref, o_ref,
                h_ref, gates_ref, acc_ref):
    e = pl.program_id(2)
    f = pl.program_id(3)

    @pl.when((e == 0) & (f == 0))
    def _():
        h = _modnorm(x_ref[...], g_in_ref[...], mod_ref, 3, 4)
        h_ref[...] = h.astype(BF16)
        logits = jnp.dot(h, wr_ref[...], preferred_element_type=F32, precision=lax.Precision.HIGHEST)
        gates_ref[...] = _top2_gates(logits)
        acc_ref[...] = jnp.zeros_like(acc_ref)

    lane = lax.broadcasted_iota(jnp.int32, gates_ref.shape, 1)
    gate_e = jnp.sum(jnp.where(lane == e, gates_ref[...], 0.0), axis=-1, keepdims=True)
    h = h_ref[...]
    a = jnp.dot(h, wa_ref[...], preferred_element_type=F32)
    bb = jnp.dot(h, wb_ref[...], preferred_element_type=F32)
    acc_ref[...] += jnp.dot((_silu(a) * bb * gate_e).astype(BF16), wo_ref[...],
                            preferred_element_type=F32)

    @pl.when((e == pl.num_programs(2) - 1) & (f == pl.num_programs(3) - 1))
    def _():
        gate = mod_ref[5:6, :]
        o_ref[...] = x_ref[...] + gate * (_rms(acc_ref[...]) * g_out_ref[...])


def _moe(x, mod, g_in, g_out, router_pad, w_in, w_out, tm=512, tf=1408):
    b, t, d = x.shape
    n_exp, fdim, _ = w_out.shape
    tm = _pick(t, tm)
    tf = _pick(fdim, tf)
    nf = fdim // tf
    const = lambda bi, i, e, f: (0, 0)
    tok = lambda bi, i, e, f: (bi, i, 0)
    return pl.pallas_call(
        _moe_kernel,
        grid=(b, t // tm, n_exp, nf),
        in_specs=[pl.BlockSpec((None, tm, d), tok), _mod_spec(mod),
                  pl.BlockSpec((1, d), const), pl.BlockSpec((1, d), const),
                  pl.BlockSpec((d, LANE), const),
                  pl.BlockSpec((None, d, tf), lambda bi, i, e, f: (e, 0, f)),
                  pl.BlockSpec((None, d, tf), lambda bi, i, e, f: (e, 0, f + nf)),
                  pl.BlockSpec((None, tf, d), lambda bi, i, e, f: (e, f, 0))],
        out_specs=pl.BlockSpec((None, tm, d), tok),
        out_shape=jax.ShapeDtypeStruct((b, t, d), F32),
        scratch_shapes=[pltpu.VMEM((tm, d), BF16), pltpu.VMEM((tm, LANE), F32), pltpu.VMEM((tm, d), F32)],
        compiler_params=_cparams("parallel", "parallel", "arbitrary", "arbitrary"),
        name="moe",
    )(x, mod, g_in.reshape(1, d), g_out.reshape(1, d), router_pad, w_in, w_in, w_out)


def _axial_angles(n_tokens, dim):
    t = jnp.arange(n_tokens)
    rows = (t // GRID_W).astype(F32)
    cols = (t % GRID_W).astype(F32)
    n_freq = dim // 4
    inv = ROPE_BASE ** (-jnp.arange(n_freq, dtype=F32) / n_freq)
    return jnp.concatenate([rows[:, None] * inv, cols[:, None] * inv], axis=-1)


def _rope_tables(n_tokens, dim):
    ang = _axial_angles(n_tokens, dim)
    cos, sin = jnp.cos(ang), jnp.sin(ang)
    if dim == LANE:
        return jnp.concatenate([cos, cos], -1), jnp.concatenate([-sin, sin], -1)
    z = jnp.zeros_like(cos)
    return jnp.concatenate([cos, z, cos, z], -1), jnp.concatenate([-sin, z, sin, z], -1)


def _spread_rope_cols(w):
    half = MLA_ROPE // 2
    z = jnp.zeros(w.shape[:-1] + (half,), w.dtype)
    return jnp.concatenate([w[..., :half], z, w[..., half:], z], axis=-1)


def kernel(x, c, ctx, c_ctx, mod_w, mod_b, norm_g, mix_in_w, ret_decay_logit, hgrn_lb_logit, mix_out_w,
           ffn_in_w, ffn_out_w, mla_down_w, mla_q_norm_g, mla_kv_norm_g, mla_uq_w, mla_ukv_w, mla_out_w,
           router_w, moe_in_w, moe_out_w):
    batch, seq, d = x.shape
    assert mod_w.shape[0] == 2, "two layers: retention/HGRN2 then MLA/MoE"

    rows = -(-(batch + 1) // 8) * 8
    cond = jnp.zeros((rows, d), F32).at[:batch].set(c).at[batch].set(c_ctx)
    mod = _modulation(cond, mod_w, mod_b)
    mod_x = [mod[l, :batch].reshape(batch, 6, d) for l in range(2)]
    mod_c = [mod[l, batch:batch + 1].reshape(1, 6, d) for l in range(2)]

    g = norm_g[0]
    w_in = mix_in_w[0].astype(BF16)
    proj_x = _norm_matmul(x, mod_x[0], g[0], w_in, 0, 1, F32)
    proj_c = _norm_matmul(ctx, mod_c[0], g[0], w_in, 0, 1, F32)
    cos2, sin2 = _rope_tables(seq, HEAD_DIM)
    log_gamma = jax.nn.log_sigmoid(ret_decay_logit[0].astype(F32))
    lb = jnp.cumsum(jax.nn.softmax(hgrn_lb_logit.astype(F32), axis=0), axis=0)[0]
    ret_c, ret_x = _retention(proj_c, proj_x, log_gamma, cos2, sin2)
    hg_c, hg_x = _hgrn(proj_c, proj_x, lb)
    w_out = mix_out_w[0].astype(BF16)
    ret_w = RET_HEADS * HEAD_DIM
    w_outs = [w_out[:ret_w], w_out[ret_w:]]
    x = _proj_residual([ret_x, hg_x], w_outs, x, mod_x[0], g[1], 2)
    ctx = _proj_residual([ret_c, hg_c], w_outs, ctx, mod_c[0], g[1], 2)
    f_in, f_out = ffn_in_w[0].astype(BF16), ffn_out_w[0].astype(BF16)
    x = _ffn(x, mod_x[0], g[2], g[3], f_in, f_out)
    ctx = _ffn(ctx, mod_c[0], g[2], g[3], f_in, f_out)

    g = norm_g[1]
    dq = MLA_NOPE + MLA_ROPE
    wd = mla_down_w[0]
    wd = jnp.concatenate([wd[:, :MLA_Q_RANK + MLA_KV_RANK],
                          _spread_rope_cols(wd[:, MLA_Q_RANK + MLA_KV_RANK:])], axis=-1).astype(BF16)
    wq = mla_uq_w[0].reshape(MLA_Q_RANK, MLA_HEADS, dq)
    wq = jnp.concatenate([wq[..., :MLA_NOPE], _spread_rope_cols(wq[..., MLA_NOPE:])], axis=-1)
    wq = wq.reshape(MLA_Q_RANK, MLA_HEADS * MLA_QW).astype(BF16)
    wkv = mla_ukv_w[0].reshape(MLA_KV_RANK, MLA_HEADS, MLA_NOPE + MLA_V)
    wkv = jnp.concatenate([wkv[..., :MLA_NOPE].reshape(MLA_KV_RANK, -1),
                           wkv[..., MLA_NOPE:].reshape(MLA_KV_RANK, -1)], axis=-1).astype(BF16)
    cos_m, sin_m = _rope_tables(seq, MLA_ROPE)
    q, kn_x, kr_x, v_x = _mla_proj(x, mod_x[1], g[0], wd, mla_q_norm_g[0], mla_kv_norm_g[0], wq, wkv,
                                   cos_m, sin_m, rope=True, want_q=True)
    tc = ctx.shape[1]
    kn_c, kr_c, v_c = _mla_proj(ctx, mod_c[1], g[0], wd, mla_q_norm_g[0], mla_kv_norm_g[0], wq, wkv,
                                cos_m[:tc], sin_m[:tc], rope=False, want_q=False)
    att = _attention(q, kn_c, kr_c, v_c, kn_x, kr_x, v_x, dq ** -0.5)
    x = _proj_residual([att], [mla_out_w[0].astype(BF16)], x, mod_x[1], g[1], 2)
    router_pad = jnp.zeros((d, LANE), F32).at[:, :N_EXPERTS].set(router_w[0])
    x = _moe(x, mod_x[1], g[2], g[3], router_pad, moe_in_w[0].astype(BF16), moe_out_w[0].astype(BF16))
    return x
```

```python
import functools
import math

import jax
import jax.numpy as jnp
import numpy as np
from jax import lax
from jax.experimental import pallas as pl
from jax.experimental.pallas import tpu as pltpu
from jax.experimental.pallas import tpu_sc as plsc

F32 = jnp.float32
BF16 = jnp.bfloat16

NORM_EPS = 1e-6
ROPE_BASE = 10000.0
GRID_W = 64
LANE = 128

RET_HEADS = 4
HGRN_HEADS = 4
HEAD_DIM = 128
RET_CHUNK = 128
HGRN_CHUNK = 128

MLA_HEADS = 8
MLA_NOPE = 128
MLA_ROPE = 64
MLA_V = 128
MLA_Q_RANK = 384
MLA_KV_RANK = 256
N_EXPERTS = 8

VMEM_LIMIT = 56 * 1024 * 1024


def _cparams(*sem):
    return pltpu.CompilerParams(dimension_semantics=sem, vmem_limit_bytes=VMEM_LIMIT)


def _mm(a, b):
    return jnp.dot(a.astype(BF16), b.astype(BF16), preferred_element_type=F32)


def _mm_nt(a, b):
    return lax.dot_general(a.astype(BF16), b.astype(BF16), (((1,), (1,)), ((), ())),
                           preferred_element_type=F32)


def _mm_tn(a, b):
    return lax.dot_general(a.astype(BF16), b.astype(BF16), (((0,), (0,)), ((), ())),
                           preferred_element_type=F32)


def _rms(x):
    return x * lax.rsqrt(jnp.mean(x * x, axis=-1, keepdims=True) + NORM_EPS)


def _silu(x):
    return x * (1.0 / (1.0 + jnp.exp(-x)))


def _modnorm(x, g, mod_ref, shift_row, scale_row):
    shift = mod_ref[shift_row:shift_row + 1, :]
    scale = mod_ref[scale_row:scale_row + 1, :]
    return _rms(x) * g * (1.0 + scale) + shift


def _mod_spec(mod):
    d = mod.shape[-1]
    if mod.shape[0] == 1:
        return pl.BlockSpec((None, 6, d), lambda b, *_: (0, 0, 0))
    return pl.BlockSpec((None, 6, d), lambda b, *_: (b, 0, 0))


def _pick(n, pref):
    t = min(pref, n)
    while n % t:
        t -= LANE
    return t


def _mod_kernel(c_ref, w_ref, b_ref, o_ref):
    a = _silu(c_ref[...])
    o_ref[...] = jnp.dot(a, w_ref[...], preferred_element_type=F32,
                         precision=lax.Precision.HIGHEST) + b_ref[...]


def _modulation(cond, mod_w, mod_b):
    n_layers, d, n = mod_w.shape
    r = cond.shape[0]
    tn = _pick(n, 1536)
    return pl.pallas_call(
        _mod_kernel,
        grid=(n_layers, n // tn),
        in_specs=[pl.BlockSpec((r, d), lambda l, j: (0, 0)),
                  pl.BlockSpec((None, d, tn), lambda l, j: (l, 0, j)),
                  pl.BlockSpec((None, 1, tn), lambda l, j: (l, 0, j))],
        out_specs=pl.BlockSpec((None, r, tn), lambda l, j: (l, 0, j)),
        out_shape=jax.ShapeDtypeStruct((n_layers, r, n), F32),
        compiler_params=_cparams("parallel", "parallel"),
        name="modulation",
    )(cond, mod_w, mod_b.reshape(n_layers, 1, n))


def _norm_matmul_kernel(x_ref, mod_ref, g_ref, w_ref, o_ref, h_ref, *, shift_row, scale_row):
    @pl.when(pl.program_id(2) == 0)
    def _():
        h_ref[...] = _modnorm(x_ref[...], g_ref[...], mod_ref, shift_row, scale_row).astype(BF16)

    o_ref[...] = jnp.dot(h_ref[...], w_ref[...], preferred_element_type=F32).astype(o_ref.dtype)


def _norm_matmul(x, mod, g, w, shift_row, scale_row, out_dtype, tm=512, tn=1536):
    b, t, d = x.shape
    n = w.shape[1]
    tm = _pick(t, tm)
    tn = _pick(n, tn)
    return pl.pallas_call(
        functools.partial(_norm_matmul_kernel, shift_row=shift_row, scale_row=scale_row),
        grid=(b, t // tm, n // tn),
        in_specs=[pl.BlockSpec((None, tm, d), lambda bi, i, j: (bi, i, 0)),
                  _mod_spec(mod),
                  pl.BlockSpec((1, d), lambda bi, i, j: (0, 0)),
                  pl.BlockSpec((d, tn), lambda bi, i, j: (0, j))],
        out_specs=pl.BlockSpec((None, tm, tn), lambda bi, i, j: (bi, i, j)),
        out_shape=jax.ShapeDtypeStruct((b, t, n), out_dtype),
        scratch_shapes=[pltpu.VMEM((tm, d), BF16)],
        compiler_params=_cparams("parallel", "parallel", "arbitrary"),
        name="norm_matmul",
    )(x, mod, g.reshape(1, d), w)


def _proj_residual_kernel(*refs, n_lhs, gate_row):
    lhs = refs[:n_lhs]
    ws = refs[n_lhs:2 * n_lhs]
    x_ref, mod_ref, g_ref, o_ref = refs[2 * n_lhs:]
    y = jnp.dot(lhs[0][...], ws[0][...], preferred_element_type=F32)
    for a_ref, w_ref in zip(lhs[1:], ws[1:]):
        y += jnp.dot(a_ref[...], w_ref[...], preferred_element_type=F32)
    gate = mod_ref[gate_row:gate_row + 1, :]
    o_ref[...] = x_ref[...] + gate * (_rms(y) * g_ref[...])


def _proj_residual(lhs, ws, x, mod, g, gate_row, tm=512):
    b, t, d = x.shape
    tm = _pick(t, tm)
    n_lhs = len(lhs)
    in_specs = [pl.BlockSpec((None, tm, a.shape[-1]), lambda bi, i: (bi, i, 0)) for a in lhs]
    in_specs += [pl.BlockSpec(w.shape, lambda bi, i: (0, 0)) for w in ws]
    in_specs += [pl.BlockSpec((None, tm, d), lambda bi, i: (bi, i, 0)),
                 _mod_spec(mod),
                 pl.BlockSpec((1, d), lambda bi, i: (0, 0))]
    return pl.pallas_call(
        functools.partial(_proj_residual_kernel, n_lhs=n_lhs, gate_row=gate_row),
        grid=(b, t // tm),
        in_specs=in_specs,
        out_specs=pl.BlockSpec((None, tm, d), lambda bi, i: (bi, i, 0)),
        out_shape=jax.ShapeDtypeStruct((b, t, d), F32),
        compiler_params=_cparams("parallel", "parallel"),
        name="proj_residual",
    )(*lhs, *ws, x, mod, g.reshape(1, d))


def _ffn_kernel(x_ref, mod_ref, g_in_ref, g_out_ref, wa_ref, wb_ref, wo_ref, o_ref, h_ref, acc_ref):
    f = pl.program_id(2)

    @pl.when(f == 0)
    def _():
        h_ref[...] = _modnorm(x_ref[...], g_in_ref[...], mod_ref, 3, 4).astype(BF16)
        acc_ref[...] = jnp.zeros_like(acc_ref)

    h = h_ref[...]
    a = jnp.dot(h, wa_ref[...], preferred_element_type=F32)
    bb = jnp.dot(h, wb_ref[...], preferred_element_type=F32)
    acc_ref[...] += jnp.dot((_silu(a) * bb).astype(BF16), wo_ref[...], preferred_element_type=F32)

    @pl.when(f == pl.num_programs(2) - 1)
    def _():
        gate = mod_ref[5:6, :]
        o_ref[...] = x_ref[...] + gate * (_rms(acc_ref[...]) * g_out_ref[...])


def _ffn(x, mod, g_in, g_out, w_in, w_out, tm=512, tf=1408):
    b, t, d = x.shape
    fdim = w_out.shape[0]
    tm = _pick(t, tm)
    tf = _pick(fdim, tf)
    nf = fdim // tf
    return pl.pallas_call(
        _ffn_kernel,
        grid=(b, t // tm, nf),
        in_specs=[pl.BlockSpec((None, tm, d), lambda bi, i, f: (bi, i, 0)),
                  _mod_spec(mod),
                  pl.BlockSpec((1, d), lambda bi, i, f: (0, 0)),
                  pl.BlockSpec((1, d), lambda bi, i, f: (0, 0)),
                  pl.BlockSpec((d, tf), lambda bi, i, f: (0, f)),
                  pl.BlockSpec((d, tf), lambda bi, i, f: (0, f + nf)),
                  pl.BlockSpec((tf, d), lambda bi, i, f: (f, 0))],
        out_specs=pl.BlockSpec((None, tm, d), lambda bi, i, f: (bi, i, 0)),
        out_shape=jax.ShapeDtypeStruct((b, t, d), F32),
        scratch_shapes=[pltpu.VMEM((tm, d), BF16), pltpu.VMEM((tm, d), F32)],
        compiler_params=_cparams("parallel", "parallel", "arbitrary"),
        name="ffn",
    )(x, mod, g_in.reshape(1, d), g_out.reshape(1, d), w_in, w_in, w_out)


def _retention_kernel(lg_ref, qc_ref, kc_ref, vc_ref, gc_ref, qx_ref, kx_ref, vx_ref, gx_ref,
                      cos_ref, sin_ref, yc_ref, yx_ref, oc_ref, ox_ref, s_ref):
    cs = RET_CHUNK
    head = pl.program_id(1)
    row = lax.broadcasted_iota(jnp.int32, (cs, cs), 0).astype(F32)
    col = lax.broadcasted_iota(jnp.int32, (cs, cs), 1).astype(F32)
    pos = lax.broadcasted_iota(jnp.int32, (cs, 1), 0).astype(F32)
    k_scale = HEAD_DIM ** -0.5

    def load(q_ref, k_ref, start, rope):
        q = q_ref[pl.ds(start, cs), :]
        k = k_ref[pl.ds(start, cs), :] * k_scale
        if rope:
            cos = cos_ref[pl.ds(start, cs), :]
            sin = sin_ref[pl.ds(start, cs), :]
            q = q * cos + pltpu.roll(q, HEAD_DIM // 2, 1) * sin
            k = k * cos + pltpu.roll(k, HEAD_DIM // 2, 1) * sin
        return q, k

    for reverse in (False, True):
        lg = lg_ref[1 if reverse else 0, head]
        rel = (col - row) if reverse else (row - col)
        intra = jnp.where(rel >= 0, jnp.exp(lg * jnp.maximum(rel, 0.0)), 0.0)
        p = (cs - 1.0 - pos) if reverse else pos
        q_dec = jnp.exp(lg * (p + 1.0))
        k_dec = jnp.exp(lg * (cs - 1.0 - p))
        c_dec = jnp.exp(lg * cs)
        s_ref[...] = jnp.zeros_like(s_ref)

        for (q_ref, k_ref, v_ref, g_ref, o_ref, y_ref, rope) in (
                (qc_ref, kc_ref, vc_ref, gc_ref, oc_ref, yc_ref, False),
                (qx_ref, kx_ref, vx_ref, gx_ref, ox_ref, yx_ref, True)):
            n_chunks = q_ref.shape[0] // cs

            def step(i, carry, q_ref=q_ref, k_ref=k_ref, v_ref=v_ref, g_ref=g_ref, o_ref=o_ref,
                     y_ref=y_ref, rope=rope, n_chunks=n_chunks, reverse=reverse, intra=intra,
                     q_dec=q_dec, k_dec=k_dec, c_dec=c_dec):
                idx = (n_chunks - 1 - i) if reverse else i
                start = pl.multiple_of(idx * cs, cs)
                q, k = load(q_ref, k_ref, start, rope)
                v = v_ref[pl.ds(start, cs), :]
                state = s_ref[...]
                att = _mm_nt(q, k) * intra
                o = _mm(att, v) + _mm(q * q_dec, state)
                s_ref[...] = state * c_dec + _mm_tn(k * k_dec, v)
                if not reverse:
                    o_ref[pl.ds(start, cs), :] = o
                else:
                    o = o + o_ref[pl.ds(start, cs), :]
                    gate = g_ref[pl.ds(start, cs), :]
                    y_ref[pl.ds(start, cs), :] = (_rms(o) * _silu(gate)).astype(y_ref.dtype)
                return carry

            lax.fori_loop(0, n_chunks, step, 0)


def _retention(proj_c, proj_x, log_gamma, cos2, sin2):
    b, tc, _ = proj_c.shape
    tx = proj_x.shape[1]
    h = RET_HEADS

    def col(t, group):
        return pl.BlockSpec((None, t, HEAD_DIM), lambda bi, hi, lg: (bi, 0, group * h + hi))

    grid_spec = pltpu.PrefetchScalarGridSpec(
        num_scalar_prefetch=1,
        grid=(b, h),
        in_specs=[col(tc, 0), col(tc, 1), col(tc, 2), col(tc, 3),
                  col(tx, 0), col(tx, 1), col(tx, 2), col(tx, 3),
                  pl.BlockSpec((tx, HEAD_DIM), lambda bi, hi, lg: (0, 0)),
                  pl.BlockSpec((tx, HEAD_DIM), lambda bi, hi, lg: (0, 0))],
        out_specs=[pl.BlockSpec((None, tc, HEAD_DIM), lambda bi, hi, lg: (bi, 0, hi)),
                   pl.BlockSpec((None, tx, HEAD_DIM), lambda bi, hi, lg: (bi, 0, hi))],
        scratch_shapes=[pltpu.VMEM((tc, HEAD_DIM), F32), pltpu.VMEM((tx, HEAD_DIM), F32),
                        pltpu.VMEM((HEAD_DIM, HEAD_DIM), F32)])
    return pl.pallas_call(
        _retention_kernel,
        grid_spec=grid_spec,
        out_shape=[jax.ShapeDtypeStruct((b, tc, h * HEAD_DIM), BF16),
                   jax.ShapeDtypeStruct((b, tx, h * HEAD_DIM), BF16)],
        compiler_params=_cparams("parallel", "parallel"),
        name="retention",
    )(log_gamma, proj_c, proj_c, proj_c, proj_c, proj_x, proj_x, proj_x, proj_x, cos2, sin2)


def _hgrn_levels(cs):
    return [cs >> (i + 1) for i in range(int(math.log2(cs)))]


def _hgrn_constants(cs):
    t = np.arange(cs)[:, None]
    s = np.arange(cs)[None, :]
    tri = np.stack([(s <= t), (s >= t)]).astype(np.float32)
    masks = []
    for reverse in (False, True):
        lv = []
        for h in _hgrn_levels(cs):
            same = (t // (2 * h)) == (s // (2 * h))
            t_hi = (t % (2 * h)) >= h
            s_hi = (s % (2 * h)) >= h
            lv.append(same & (~t_hi & s_hi if reverse else t_hi & ~s_hi))
        lv.append(t == s)
        masks.append(np.stack(lv))
    return jnp.asarray(tri, BF16), jnp.asarray(np.stack(masks).astype(np.float32))


def _split3(x):
    hi = x.astype(BF16)
    r = x - hi.astype(F32)
    mid = r.astype(BF16)
    lo = (r - mid.astype(F32)).astype(BF16)
    return hi, mid, lo


def _level_reference(b, h, reverse):
    cs = b.shape[0]
    ref_local = h if reverse else h - 1
    if 2 * h >= 8:
        b3 = b.reshape(cs // (2 * h), 2 * h, b.shape[1])
        r = b3[:, ref_local:ref_local + 1, :]
        return jnp.broadcast_to(r, b3.shape).reshape(b.shape)
    local = lax.broadcasted_iota(jnp.int32, b.shape, 0) % (2 * h)
    out = b
    for m in range(2 * h):
        if m != ref_local:
            out = jnp.where(local == m, pltpu.roll(b, (m - ref_local) % cs, 0), out)
    return out


def _hgrn_kernel(lb_ref, tri_ref, mask_ref,
                 qc_ref, zfc_ref, zbc_ref, vc_ref, gc_ref,
                 qx_ref, zfx_ref, zbx_ref, vx_ref, gx_ref,
                 yc_ref, yx_ref, oc_ref, ox_ref, st_ref):
    cs = HGRN_CHUNK
    levels = _hgrn_levels(cs)
    lb = lb_ref[...]

    for reverse in (False, True):
        d = 1 if reverse else 0
        st_ref[...] = jnp.zeros_like(st_ref)
        for (q_ref, zf_ref, zb_ref, v_ref, g_ref, o_ref, y_ref) in (
                (qc_ref, zfc_ref, zbc_ref, vc_ref, gc_ref, oc_ref, yc_ref),
                (qx_ref, zfx_ref, zbx_ref, vx_ref, gx_ref, ox_ref, yx_ref)):
            z_ref = zb_ref if reverse else zf_ref
            n_chunks = q_ref.shape[0] // cs

            def step(i, carry, q_ref=q_ref, z_ref=z_ref, v_ref=v_ref, g_ref=g_ref, o_ref=o_ref,
                     y_ref=y_ref, n_chunks=n_chunks, reverse=reverse, d=d):
                idx = (n_chunks - 1 - i) if reverse else i
                start = pl.multiple_of(idx * cs, cs)
                q = _silu(q_ref[pl.ds(start, cs), :])
                z = z_ref[pl.ds(start, cs), :]
                v = v_ref[pl.ds(start, cs), :]
                f = lb + (1.0 - lb) * (1.0 / (1.0 + jnp.exp(-z)))
                k = 1.0 - f
                logf = jnp.log(f)
                tri = tri_ref[d]
                hi, mid, lo = _split3(logf)
                bsum = (jnp.dot(tri, hi, preferred_element_type=F32)
                        + jnp.dot(tri, mid, preferred_element_type=F32)
                        + jnp.dot(tri, lo, preferred_element_type=F32))
                att = _mm_nt(q, k) * mask_ref[d, len(levels)]
                for li, h in enumerate(levels):
                    w = jnp.exp(-jnp.abs(bsum - _level_reference(bsum, h, reverse)))
                    att += _mm_nt(q * w, k * w) * mask_ref[d, li]
                state_t = st_ref[...]
                o = _mm(att, v) + _mm_nt(q * jnp.exp(bsum), state_t)
                b_end = bsum[0:1, :] if reverse else bsum[cs - 1:cs, :]
                st_ref[...] = state_t * jnp.exp(b_end) + _mm_tn(v, k * jnp.exp(b_end - bsum))
                if not reverse:
                    o_ref[pl.ds(start, cs), :] = o
                else:
                    o = o + o_ref[pl.ds(start, cs), :]
                    gate = g_ref[pl.ds(start, cs), :]
                    y_ref[pl.ds(start, cs), :] = (_rms(o) * _silu(gate)).astype(y_ref.dtype)
                return carry

            lax.fori_loop(0, n_chunks, step, 0)


def _hgrn(proj_c, proj_x, lb):
    b, tc, _ = proj_c.shape
    tx = proj_x.shape[1]
    h = HGRN_HEADS
    first = 4 * RET_HEADS
    tri, masks = _hgrn_constants(HGRN_CHUNK)

    def col(t, group):
        return pl.BlockSpec((None, t, HEAD_DIM), lambda bi, hi: (bi, 0, first + group * h + hi))

    ins = [col(tc, 0), col(tc, 1), col(tc, 2), col(tc, 3), col(tc, 4),
           col(tx, 0), col(tx, 1), col(tx, 2), col(tx, 3), col(tx, 4)]
    return pl.pallas_call(
        _hgrn_kernel,
        grid=(b, h),
        in_specs=[pl.BlockSpec((None, 1, HEAD_DIM), lambda bi, hi: (hi, 0, 0)),
                  pl.BlockSpec(tri.shape, lambda bi, hi: (0, 0, 0)),
                  pl.BlockSpec(masks.shape, lambda bi, hi: (0, 0, 0, 0))] + ins,
        out_specs=[pl.BlockSpec((None, tc, HEAD_DIM), lambda bi, hi: (bi, 0, hi)),
                   pl.BlockSpec((None, tx, HEAD_DIM), lambda bi, hi: (bi, 0, hi))],
        out_shape=[jax.ShapeDtypeStruct((b, tc, h * HEAD_DIM), BF16),
                   jax.ShapeDtypeStruct((b, tx, h * HEAD_DIM), BF16)],
        scratch_shapes=[pltpu.VMEM((tc, HEAD_DIM), F32), pltpu.VMEM((tx, HEAD_DIM), F32),
                        pltpu.VMEM((HEAD_DIM, HEAD_DIM), F32)],
        compiler_params=_cparams("parallel", "parallel"),
        name="hgrn2",
    )(lb.reshape(h, 1, HEAD_DIM), tri, masks, *([proj_c] * 5), *([proj_x] * 5))


MLA_QW = 2 * LANE
MLA_DOWN_PAD = MLA_Q_RANK + MLA_KV_RANK + LANE


def _mla_proj_kernel(x_ref, mod_ref, g_ref, wd_ref, gq_ref, gkv_ref, wq_ref, wkv_ref, cos_ref, sin_ref,
                     *out_refs, rope, want_q):
    h = _modnorm(x_ref[...], g_ref[...], mod_ref, 0, 1).astype(BF16)
    down = jnp.dot(h, wd_ref[...], preferred_element_type=F32)
    if rope:
        cos = cos_ref[...]
        sin = sin_ref[...]

    def rot(v):
        return v * cos + pltpu.roll(v, LANE // 2, 1) * sin if rope else v

    if want_q:
        q_ref, kn_ref, kr_ref, v_ref = out_refs
        qn = (_rms(down[:, :MLA_Q_RANK]) * gq_ref[...]).astype(BF16)
        q = jnp.dot(qn, wq_ref[...], preferred_element_type=F32)
        for hd in range(MLA_HEADS):
            lo = hd * MLA_QW
            q_ref[:, lo:lo + LANE] = q[:, lo:lo + LANE].astype(q_ref.dtype)
            q_ref[:, lo + LANE:lo + 2 * LANE] = rot(q[:, lo + LANE:lo + 2 * LANE]).astype(q_ref.dtype)
    else:
        kn_ref, kr_ref, v_ref = out_refs
    kvn = (_rms(down[:, MLA_Q_RANK:MLA_Q_RANK + MLA_KV_RANK]) * gkv_ref[...]).astype(BF16)
    kv = jnp.dot(kvn, wkv_ref[...], preferred_element_type=F32)
    nk = MLA_HEADS * MLA_NOPE
    kn_ref[...] = kv[:, :nk].astype(kn_ref.dtype)
    v_ref[...] = kv[:, nk:].astype(v_ref.dtype)
    kr_ref[...] = rot(down[:, MLA_Q_RANK + MLA_KV_RANK:]).astype(kr_ref.dtype)


def _mla_proj(x, mod, g, wd, gq, gkv, wq, wkv, cos, sin, rope, want_q, tm=512):
    b, t, d = x.shape
    tm = _pick(t, tm)
    const = lambda bi, i: (0, 0)
    tok = lambda bi, i: (bi, i, 0)
    nk, nv = MLA_HEADS * MLA_NOPE, MLA_HEADS * MLA_V
    out_shape = [jax.ShapeDtypeStruct((b, t, nk), BF16), jax.ShapeDtypeStruct((b, t, LANE), BF16),
                 jax.ShapeDtypeStruct((b, t, nv), BF16)]
    out_specs = [pl.BlockSpec((None, tm, nk), tok), pl.BlockSpec((None, tm, LANE), tok),
                 pl.BlockSpec((None, tm, nv), tok)]
    if want_q:
        out_shape = [jax.ShapeDtypeStruct((b, t, MLA_HEADS * MLA_QW), BF16)] + out_shape
        out_specs = [pl.BlockSpec((None, tm, MLA_HEADS * MLA_QW), tok)] + out_specs
    return pl.pallas_call(
        functools.partial(_mla_proj_kernel, rope=rope, want_q=want_q),
        grid=(b, t // tm),
        in_specs=[pl.BlockSpec((None, tm, d), tok), _mod_spec(mod), pl.BlockSpec((1, d), const),
                  pl.BlockSpec(wd.shape, const), pl.BlockSpec((1, MLA_Q_RANK), const),
                  pl.BlockSpec((1, MLA_KV_RANK), const), pl.BlockSpec(wq.shape, const),
                  pl.BlockSpec(wkv.shape, const),
                  pl.BlockSpec((tm, LANE), lambda bi, i: (i, 0)),
                  pl.BlockSpec((tm, LANE), lambda bi, i: (i, 0))],
        out_specs=out_specs,
        out_shape=out_shape,
        compiler_params=_cparams("parallel", "parallel"),
        name="mla_proj",
    )(x, mod, g.reshape(1, d), wd, gq.reshape(1, -1), gkv.reshape(1, -1), wq, wkv, cos, sin)


def _lane_fold(x, op):
    out = x[:, :LANE]
    for j in range(1, x.shape[1] // LANE):
        out = op(out, x[:, j * LANE:(j + 1) * LANE])
    return out


def _attention_kernel(q_ref, knc_ref, krc_ref, vc_ref, knx_ref, krx_ref, vx_ref, o_ref,
                      kc_ref, kx_ref, sc0_ref, sc1_ref, sx0_ref, sx1_ref, m0_ref, m1_ref, *, tk, exp_scale):
    step = pl.program_id(2)

    @pl.when(step == 0)
    def _():
        kc_ref[:, :LANE] = knc_ref[...]
        kc_ref[:, LANE:] = krc_ref[...]
        kx_ref[:, :LANE] = knx_ref[...]
        kx_ref[:, LANE:] = krx_ref[...]
        sc1_ref[...] = jnp.zeros(sc1_ref.shape, F32)
        sx1_ref[...] = jnp.zeros(sx1_ref.shape, F32)
        m1_ref[...] = jnp.zeros(m1_ref.shape, F32)

    n_blocks = kx_ref.shape[0] // tk
    nt = (((1,), (1,)), ((), ()))

    def body(sc_cur, sx_cur, m_cur, sc_prev, sx_prev, m_prev):
        q = q_ref[...]
        m = jnp.max(m_prev[...], axis=-1, keepdims=True)
        s = lax.dot_general(q, kc_ref[...], nt, preferred_element_type=F32)
        sc_cur[...] = s
        m_part = _lane_fold(s, jnp.maximum)
        p = jnp.exp2((sc_prev[...] - m) * exp_scale)
        l_part = _lane_fold(p, jnp.add)
        acc = jnp.dot(p.astype(BF16), vc_ref[...], preferred_element_type=F32)
        for i in range(n_blocks):
            s = lax.dot_general(q, kx_ref[i * tk:(i + 1) * tk, :], nt, preferred_element_type=F32)
            sx_cur[i] = s
            m_part = jnp.maximum(m_part, _lane_fold(s, jnp.maximum))
            p = jnp.exp2((sx_prev[i] - m) * exp_scale)
            l_part = l_part + _lane_fold(p, jnp.add)
            acc = acc + jnp.dot(p.astype(BF16), vx_ref[i * tk:(i + 1) * tk, :],
                                preferred_element_type=F32)
        m_cur[...] = m_part
        l = jnp.sum(l_part, axis=-1, keepdims=True)
        o_ref[...] = (acc / l).astype(o_ref.dtype)

    @pl.when(step % 2 == 0)
    def _():
        body(sc0_ref, sx0_ref, m0_ref, sc1_ref, sx1_ref, m1_ref)

    @pl.when(step % 2 == 1)
    def _():
        body(sc1_ref, sx1_ref, m1_ref, sc0_ref, sx0_ref, m0_ref)


def _attention(q, kn_c, kr_c, v_c, kn_x, kr_x, v_x, scale, tq=256, tk=512):
    b, t, _ = q.shape
    tc = kn_c.shape[1]
    tq = _pick(t, tq)
    tk = _pick(t, tk)
    nq = t // tq
    head_c = lambda bi, hi, i: (bi, 0, hi)
    shared = lambda bi, hi, i: (bi, 0, 0)
    return pl.pallas_call(
        functools.partial(_attention_kernel, tk=tk, exp_scale=scale * math.log2(math.e)),
        grid=(b, MLA_HEADS, nq + 1),
        in_specs=[pl.BlockSpec((None, tq, MLA_QW), lambda bi, hi, i: (bi, jnp.minimum(i, nq - 1), hi)),
                  pl.BlockSpec((None, tc, LANE), head_c), pl.BlockSpec((None, tc, LANE), shared),
                  pl.BlockSpec((None, tc, LANE), head_c),
                  pl.BlockSpec((None, t, LANE), head_c), pl.BlockSpec((None, t, LANE), shared),
                  pl.BlockSpec((None, t, LANE), head_c)],
        out_specs=pl.BlockSpec((None, tq, MLA_V), lambda bi, hi, i: (bi, jnp.maximum(i - 1, 0), hi)),
        out_shape=jax.ShapeDtypeStruct((b, t, MLA_HEADS * MLA_V), BF16),
        scratch_shapes=[pltpu.VMEM((tc, MLA_QW), BF16), pltpu.VMEM((t, MLA_QW), BF16),
                        pltpu.VMEM((tq, tc), F32), pltpu.VMEM((tq, tc), F32),
                        pltpu.VMEM((t // tk, tq, tk), F32), pltpu.VMEM((t // tk, tq, tk), F32),
                        pltpu.VMEM((tq, LANE), F32), pltpu.VMEM((tq, LANE), F32)],
        compiler_params=_cparams("parallel", "parallel", "arbitrary"),
        name="mla_attention",
    )(q, kn_c, kr_c, v_c, kn_x, kr_x, v_x)


def _router_kernel(x_ref, mod_ref, g_ref, wr_ref, h_ref, w_ref, e_ref):
    h = _modnorm(x_ref[...], g_ref[...], mod_ref, 3, 4)
    h_ref[...] = h.astype(BF16)
    logits = jnp.dot(h, wr_ref[...], preferred_element_type=F32, precision=lax.Precision.HIGHEST)
    lane = lax.broadcasted_iota(jnp.int32, logits.shape, 1)
    neg = jnp.float32(-jnp.inf)
    lg = jnp.where(lane < N_EXPERTS, logits, neg)
    m1 = jnp.max(lg, axis=-1, keepdims=True)
    i1 = jnp.min(jnp.where(lg == m1, lane, LANE), axis=-1, keepdims=True)
    lg2 = jnp.where(lane == i1, neg, lg)
    m2 = jnp.max(lg2, axis=-1, keepdims=True)
    i2 = jnp.min(jnp.where(lg2 == m2, lane, LANE), axis=-1, keepdims=True)
    e2 = jnp.exp(m2 - m1)
    w1 = 1.0 / (1.0 + e2)
    w2 = e2 / (1.0 + e2)
    w_ref[...] = jnp.where(lane == 0, w1, jnp.where(lane == 1, w2, 0.0))
    e_ref[...] = jnp.where(lane == 0, i1, jnp.where(lane == 1, i2, 0))


def _router(x, mod, g, router_pad, tm=512):
    b, t, d = x.shape
    tm = _pick(t, tm)
    tok = lambda bi, i: (bi, i, 0)
    const = lambda bi, i: (0, 0)
    return pl.pallas_call(
        _router_kernel,
        grid=(b, t // tm),
        in_specs=[pl.BlockSpec((None, tm, d), tok), _mod_spec(mod), pl.BlockSpec((1, d), const),
                  pl.BlockSpec((d, LANE), const)],
        out_specs=[pl.BlockSpec((None, tm, d), tok), pl.BlockSpec((None, tm, LANE), tok),
                   pl.BlockSpec((None, tm, LANE), tok)],
        out_shape=[jax.ShapeDtypeStruct((b, t, d), BF16), jax.ShapeDtypeStruct((b, t, LANE), F32),
                   jax.ShapeDtypeStruct((b, t, LANE), jnp.int32)],
        compiler_params=_cparams("parallel", "parallel"),
        name="moe_router",
    )(x, mod, g.reshape(1, d), router_pad)


SC_GATHER_ROWS = 64


def _gather_rows(table, idx):
    info = plsc.get_sparse_core_info()
    n_workers = info.num_cores * info.num_subcores
    n, width = idx.shape[0], table.shape[1]
    chunk = SC_GATHER_ROWS
    assert n % (n_workers * chunk) == 0
    n_chunks = n // (n_workers * chunk)
    mesh = plsc.VectorSubcoreMesh(core_axis_name="c", subcore_axis_name="s")

    def body(table_hbm, idx_hbm, out_hbm, idx_v, rows_v, sem):
        wid = lax.axis_index("s") * info.num_cores + lax.axis_index("c")
        pltpu.sync_copy(idx_hbm.at[wid], idx_v)

        @pl.loop(0, n_chunks)
        def _(j):
            pltpu.async_copy(table_hbm.at[idx_v.at[j]], rows_v, sem).wait()
            pltpu.sync_copy(rows_v, out_hbm.at[pl.ds((wid * n_chunks + j) * chunk, chunk)])

    return pl.kernel(
        body,
        out_type=jax.ShapeDtypeStruct((n, width), table.dtype),
        mesh=mesh,
        scratch_types=[pltpu.VMEM((n_chunks, chunk), jnp.int32), pltpu.VMEM((chunk, width), table.dtype),
                       pltpu.SemaphoreType.DMA],
        name="sc_gather_rows",
    )(table, idx.reshape(n_workers, n_chunks, chunk))


def _as_words(a):
    return lax.bitcast_convert_type(a.reshape(a.shape[0], -1, 2), jnp.int32)


def _as_bf16(a):
    return lax.bitcast_convert_type(a, BF16).reshape(a.shape[0], -1)


def _expert_ffn_kernel(te_ref, tv_ref, x_ref, wrow_ref, wa_ref, wb_ref, wo_ref, o_ref, acc_ref):
    j = pl.program_id(0)
    f = pl.program_id(1)
    last = pl.num_programs(1) - 1
    valid = tv_ref[j] != 0

    @pl.when(valid)
    def _():
        @pl.when(f == 0)
        def _():
            acc_ref[...] = jnp.zeros_like(acc_ref)

        h = x_ref[...]
        a = jnp.dot(h, wa_ref[...], preferred_element_type=F32)
        bb = jnp.dot(h, wb_ref[...], preferred_element_type=F32)
        acc_ref[...] += jnp.dot((_silu(a) * bb).astype(BF16), wo_ref[...], preferred_element_type=F32)

        @pl.when(f == last)
        def _():
            o_ref[...] = (acc_ref[...] * wrow_ref[...]).astype(o_ref.dtype)

    @pl.when(jnp.logical_not(valid) & (f == last))
    def _():
        o_ref[...] = jnp.zeros_like(o_ref)


def _expert_ffn(xs, w_rows, tile_expert, tile_valid, w_in, w_out, tm, tf=1408):
    p, d = xs.shape
    n_exp, fdim, _ = w_out.shape
    tf = _pick(fdim, tf)
    nf = fdim // tf

    def f_eff(j, f, tv):
        return jnp.where(tv[j] != 0, f, nf - 1)

    grid_spec = pltpu.PrefetchScalarGridSpec(
        num_scalar_prefetch=2,
        grid=(p // tm, nf),
        in_specs=[pl.BlockSpec((tm, d), lambda j, f, te, tv: (j, 0)),
                  pl.BlockSpec((tm, 1), lambda j, f, te, tv: (j, 0)),
                  pl.BlockSpec((None, d, tf), lambda j, f, te, tv: (te[j], 0, f_eff(j, f, tv))),
                  pl.BlockSpec((None, d, tf), lambda j, f, te, tv: (te[j], 0, f_eff(j, f, tv) + nf)),
                  pl.BlockSpec((None, tf, d), lambda j, f, te, tv: (te[j], f_eff(j, f, tv), 0))],
        out_specs=pl.BlockSpec((tm, d), lambda j, f, te, tv: (j, 0)),
        scratch_shapes=[pltpu.VMEM((tm, d), F32)])
    return pl.pallas_call(
        _expert_ffn_kernel,
        grid_spec=grid_spec,
        out_shape=jax.ShapeDtypeStruct((p, d), BF16),
        compiler_params=_cparams("parallel", "arbitrary"),
        name="moe_expert_ffn",
    )(tile_expert, tile_valid, xs, w_rows, w_in, w_in, w_out)


def _moe_combine_kernel(x_ref, y_ref, mod_ref, g_ref, o_ref):
    d = x_ref.shape[-1]
    y = y_ref[:, :d].astype(F32) + y_ref[:, d:].astype(F32)
    o_ref[...] = x_ref[...] + mod_ref[5:6, :] * (_rms(y) * g_ref[...])


def _moe_combine(x, y2, mod, g, tm=512):
    b, t, d = x.shape
    tm = _pick(t, tm)
    tok = lambda bi, i: (bi, i, 0)
    return pl.pallas_call(
        _moe_combine_kernel,
        grid=(b, t // tm),
        in_specs=[pl.BlockSpec((None, tm, d), tok), pl.BlockSpec((None, tm, 2 * d), tok), _mod_spec(mod),
                  pl.BlockSpec((1, d), lambda bi, i: (0, 0))],
        out_specs=pl.BlockSpec((None, tm, d), tok),
        out_shape=jax.ShapeDtypeStruct((b, t, d), F32),
        compiler_params=_cparams("parallel", "parallel"),
        name="moe_combine",
    )(x, y2, mod, g.reshape(1, d))


def _moe(x, mod, g_in, g_out, router_pad, w_in, w_out, tm=512):
    b, t, d = x.shape
    n_exp = w_out.shape[0]
    n = b * t
    h, w, e = _router(x, mod, g_in, router_pad)
    e_flat = e[..., :2].reshape(2 * n)
    w_flat = w[..., :2].reshape(2 * n)
    onehot = (e_flat[:, None] == jnp.arange(n_exp, dtype=jnp.int32)).astype(jnp.int32)
    csum = jnp.cumsum(onehot, axis=0)
    rank = jnp.sum((csum - onehot) * onehot, axis=-1)
    counts = csum[-1]
    ends = jnp.cumsum((counts + tm - 1) // tm * tm)
    dest = (ends - (counts + tm - 1) // tm * tm)[e_flat] + rank
    p = 2 * n + n_exp * tm
    src = jnp.zeros((p,), jnp.int32).at[dest].set(jnp.arange(2 * n, dtype=jnp.int32) // 2)
    w_rows = jnp.zeros((p,), F32).at[dest].set(w_flat)
    tile_start = jnp.arange(p // tm, dtype=jnp.int32) * tm
    tile_expert = jnp.minimum(jnp.searchsorted(ends, tile_start, side="right"), n_exp - 1).astype(jnp.int32)
    tile_valid = (tile_start < ends[-1]).astype(jnp.int32)

    xs = _as_bf16(_gather_rows(_as_words(h.reshape(n, d)), src))
    ys = _expert_ffn(xs, w_rows.reshape(p, 1), tile_expert, tile_valid, w_in, w_out, tm)
    y2 = _as_bf16(_gather_rows(_as_words(ys), dest))
    return _moe_combine(x, y2.reshape(b, t, 2 * d), mod, g_out)


def _axial_angles(n_tokens, dim):
    t = jnp.arange(n_tokens)
    rows = (t // GRID_W).astype(F32)
    cols = (t % GRID_W).astype(F32)
    n_freq = dim // 4
    inv = ROPE_BASE ** (-jnp.arange(n_freq, dtype=F32) / n_freq)
    return jnp.concatenate([rows[:, None] * inv, cols[:, None] * inv], axis=-1)


def _rope_tables(n_tokens, dim):
    ang = _axial_angles(n_tokens, dim)
    cos, sin = jnp.cos(ang), jnp.sin(ang)
    if dim == LANE:
        return jnp.concatenate([cos, cos], -1), jnp.concatenate([-sin, sin], -1)
    z = jnp.zeros_like(cos)
    return jnp.concatenate([cos, z, cos, z], -1), jnp.concatenate([-sin, z, sin, z], -1)


def _spread_rope_cols(w):
    half = MLA_ROPE // 2
    z = jnp.zeros(w.shape[:-1] + (half,), w.dtype)
    return jnp.concatenate([w[..., :half], z, w[..., half:], z], axis=-1)


def kernel(x, c, ctx, c_ctx, mod_w, mod_b, norm_g, mix_in_w, ret_decay_logit, hgrn_lb_logit, mix_out_w,
           ffn_in_w, ffn_out_w, mla_down_w, mla_q_norm_g, mla_kv_norm_g, mla_uq_w, mla_ukv_w, mla_out_w,
           router_w, moe_in_w, moe_out_w):
    batch, seq, d = x.shape
    assert mod_w.shape[0] == 2, "two layers: retention/HGRN2 then MLA/MoE"

    rows = -(-(batch + 1) // 8) * 8
    cond = jnp.zeros((rows, d), F32).at[:batch].set(c).at[batch].set(c_ctx)
    mod = _modulation(cond, mod_w, mod_b)
    mod_x = [mod[l, :batch].reshape(batch, 6, d) for l in range(2)]
    mod_c = [mod[l, batch:batch + 1].reshape(1, 6, d) for l in range(2)]

    g = norm_g[0]
    w_in = mix_in_w[0].astype(BF16)
    proj_x = _norm_matmul(x, mod_x[0], g[0], w_in, 0, 1, F32)
    proj_c = _norm_matmul(ctx, mod_c[0], g[0], w_in, 0, 1, F32)
    cos2, sin2 = _rope_tables(seq, HEAD_DIM)
    log_gamma = jax.nn.log_sigmoid(ret_decay_logit[0].astype(F32))
    lb = jnp.cumsum(jax.nn.softmax(hgrn_lb_logit.astype(F32), axis=0), axis=0)[0]
    ret_c, ret_x = _retention(proj_c, proj_x, log_gamma, cos2, sin2)
    hg_c, hg_x = _hgrn(proj_c, proj_x, lb)
    w_out = mix_out_w[0].astype(BF16)
    ret_w = RET_HEADS * HEAD_DIM
    w_outs = [w_out[:ret_w], w_out[ret_w:]]
    x = _proj_residual([ret_x, hg_x], w_outs, x, mod_x[0], g[1], 2)
    ctx = _proj_residual([ret_c, hg_c], w_outs, ctx, mod_c[0], g[1], 2)
    f_in, f_out = ffn_in_w[0].astype(BF16), ffn_out_w[0].astype(BF16)
    x = _ffn(x, mod_x[0], g[2], g[3], f_in, f_out)
    ctx = _ffn(ctx, mod_c[0], g[2], g[3], f_in, f_out)

    g = norm_g[1]
    dq = MLA_NOPE + MLA_ROPE
    wd = mla_down_w[0]
    wd = jnp.concatenate([wd[:, :MLA_Q_RANK + MLA_KV_RANK],
                          _spread_rope_cols(wd[:, MLA_Q_RANK + MLA_KV_RANK:])], axis=-1).astype(BF16)
    wq = mla_uq_w[0].reshape(MLA_Q_RANK, MLA_HEADS, dq)
    wq = jnp.concatenate([wq[..., :MLA_NOPE], _spread_rope_cols(wq[..., MLA_NOPE:])], axis=-1)
    wq = wq.reshape(MLA_Q_RANK, MLA_HEADS * MLA_QW).astype(BF16)
    wkv = mla_ukv_w[0].reshape(MLA_KV_RANK, MLA_HEADS, MLA_NOPE + MLA_V)
    wkv = jnp.concatenate([wkv[..., :MLA_NOPE].reshape(MLA_KV_RANK, -1),
                           wkv[..., MLA_NOPE:].reshape(MLA_KV_RANK, -1)], axis=-1).astype(BF16)
    cos_m, sin_m = _rope_tables(seq, MLA_ROPE)
    q, kn_x, kr_x, v_x = _mla_proj(x, mod_x[1], g[0], wd, mla_q_norm_g[0], mla_kv_norm_g[0], wq, wkv,
                                   cos_m, sin_m, rope=True, want_q=True)
    tc = ctx.shape[1]
    kn_c, kr_c, v_c = _mla_proj(ctx, mod_c[1], g[0], wd, mla_q_norm_g[0], mla_kv_norm_g[0], wq, wkv,
                                cos_m[:tc], sin_m[:tc], rope=False, want_q=False)
    att = _attention(q, kn_c, kr_c, v_c, kn_x, kr_x, v_x, dq ** -0.5)
    x = _proj_residual([att], [mla_out_w[0].astype(BF16)], x, mod_x[1], g[1], 2)
    router_pad = jnp.zeros((d, LANE), F32).at[:, :N_EXPERTS].set(router_w[0])
    x = _moe(x, mod_x[1], g[2], g[3], router_pad, moe_in_w[0].astype(BF16), moe_out_w[0].astype(BF16))
    return x
```

```python
import functools
import math

import jax
import jax.numpy as jnp
import numpy as np
from jax import lax
from jax.experimental import pallas as pl
from jax.experimental.pallas import tpu as pltpu
from jax.experimental.pallas import tpu_sc as plsc

F32 = jnp.float32
BF16 = jnp.bfloat16

NORM_EPS = 1e-6
ROPE_BASE = 10000.0
GRID_W = 64
LANE = 128

RET_HEADS = 4
HGRN_HEADS = 4
HEAD_DIM = 128
RET_CHUNK = 128
HGRN_CHUNK = 128

MLA_HEADS = 8
MLA_NOPE = 128
MLA_ROPE = 64
MLA_V = 128
MLA_Q_RANK = 384
MLA_KV_RANK = 256
N_EXPERTS = 8

VMEM_LIMIT = 56 * 1024 * 1024


def _cparams(*sem):
    return pltpu.CompilerParams(dimension_semantics=sem, vmem_limit_bytes=VMEM_LIMIT)


def _mm(a, b):
    return jnp.dot(a.astype(BF16), b.astype(BF16), preferred_element_type=F32)


def _mm_nt(a, b):
    return lax.dot_general(a.astype(BF16), b.astype(BF16), (((1,), (1,)), ((), ())),
                           preferred_element_type=F32)


def _mm_tn(a, b):
    return lax.dot_general(a.astype(BF16), b.astype(BF16), (((0,), (0,)), ((), ())),
                           preferred_element_type=F32)


def _rms(x):
    return x * lax.rsqrt(jnp.mean(x * x, axis=-1, keepdims=True) + NORM_EPS)


def _silu(x):
    return x * (1.0 / (1.0 + jnp.exp(-x)))


def _modnorm(x, g, mod_ref, shift_row, scale_row):
    shift = mod_ref[shift_row:shift_row + 1, :]
    scale = mod_ref[scale_row:scale_row + 1, :]
    return _rms(x) * g * (1.0 + scale) + shift


def _mod_spec(mod):
    d = mod.shape[-1]
    if mod.shape[0] == 1:
        return pl.BlockSpec((None, 6, d), lambda b, *_: (0, 0, 0))
    return pl.BlockSpec((None, 6, d), lambda b, *_: (b, 0, 0))


def _pick(n, pref):
    t = min(pref, n)
    while n % t:
        t -= LANE
    return t


def _mod_kernel(c_ref, w_ref, b_ref, o_ref):
    a = _silu(c_ref[...])
    o_ref[...] = jnp.dot(a, w_ref[...], preferred_element_type=F32,
                         precision=lax.Precision.HIGHEST) + b_ref[...]


def _modulation(cond, mod_w, mod_b):
    n_layers, d, n = mod_w.shape
    r = cond.shape[0]
    tn = _pick(n, 1536)
    return pl.pallas_call(
        _mod_kernel,
        grid=(n_layers, n // tn),
        in_specs=[pl.BlockSpec((r, d), lambda l, j: (0, 0)),
                  pl.BlockSpec((None, d, tn), lambda l, j: (l, 0, j)),
                  pl.BlockSpec((None, 1, tn), lambda l, j: (l, 0, j))],
        out_specs=pl.BlockSpec((None, r, tn), lambda l, j: (l, 0, j)),
        out_shape=jax.ShapeDtypeStruct((n_layers, r, n), F32),
        compiler_params=_cparams("parallel", "parallel"),
        name="modulation",
    )(cond, mod_w, mod_b.reshape(n_layers, 1, n))


def _norm_matmul_kernel(x_ref, mod_ref, g_ref, w_ref, o_ref, h_ref, *, shift_row, scale_row):
    @pl.when(pl.program_id(2) == 0)
    def _():
        h_ref[...] = _modnorm(x_ref[...], g_ref[...], mod_ref, shift_row, scale_row).astype(BF16)

    o_ref[...] = jnp.dot(h_ref[...], w_ref[...], preferred_element_type=F32).astype(o_ref.dtype)


def _norm_matmul(x, mod, g, w, shift_row, scale_row, out_dtype, tm=512, tn=1536):
    b, t, d = x.shape
    n = w.shape[1]
    tm = _pick(t, tm)
    tn = _pick(n, tn)
    return pl.pallas_call(
        functools.partial(_norm_matmul_kernel, shift_row=shift_row, scale_row=scale_row),
        grid=(b, t // tm, n // tn),
        in_specs=[pl.BlockSpec((None, tm, d), lambda bi, i, j: (bi, i, 0)),
                  _mod_spec(mod),
                  pl.BlockSpec((1, d), lambda bi, i, j: (0, 0)),
                  pl.BlockSpec((d, tn), lambda bi, i, j: (0, j))],
        out_specs=pl.BlockSpec((None, tm, tn), lambda bi, i, j: (bi, i, j)),
        out_shape=jax.ShapeDtypeStruct((b, t, n), out_dtype),
        scratch_shapes=[pltpu.VMEM((tm, d), BF16)],
        compiler_params=_cparams("parallel", "parallel", "arbitrary"),
        name="norm_matmul",
    )(x, mod, g.reshape(1, d), w)


def _proj_residual_kernel(*refs, n_lhs, gate_row):
    lhs = refs[:n_lhs]
    ws = refs[n_lhs:2 * n_lhs]
    x_ref, mod_ref, g_ref, o_ref = refs[2 * n_lhs:]
    y = jnp.dot(lhs[0][...], ws[0][...], preferred_element_type=F32)
    for a_ref, w_ref in zip(lhs[1:], ws[1:]):
        y += jnp.dot(a_ref[...], w_ref[...], preferred_element_type=F32)
    gate = mod_ref[gate_row:gate_row + 1, :]
    o_ref[...] = x_ref[...] + gate * (_rms(y) * g_ref[...])


def _proj_residual(lhs, ws, x, mod, g, gate_row, tm=512):
    b, t, d = x.shape
    tm = _pick(t, tm)
    n_lhs = len(lhs)
    in_specs = [pl.BlockSpec((None, tm, a.shape[-1]), lambda bi, i: (bi, i, 0)) for a in lhs]
    in_specs += [pl.BlockSpec(w.shape, lambda bi, i: (0, 0)) for w in ws]
    in_specs += [pl.BlockSpec((None, tm, d), lambda bi, i: (bi, i, 0)),
                 _mod_spec(mod),
                 pl.BlockSpec((1, d), lambda bi, i: (0, 0))]
    return pl.pallas_call(
        functools.partial(_proj_residual_kernel, n_lhs=n_lhs, gate_row=gate_row),
        grid=(b, t // tm),
        in_specs=in_specs,
        out_specs=pl.BlockSpec((None, tm, d), lambda bi, i: (bi, i, 0)),
        out_shape=jax.ShapeDtypeStruct((b, t, d), F32),
        compiler_params=_cparams("parallel", "parallel"),
        name="proj_residual",
    )(*lhs, *ws, x, mod, g.reshape(1, d))


def _ffn_kernel(x_ref, mod_ref, g_in_ref, g_out_ref, wa_ref, wb_ref, wo_ref, o_ref, h_ref, acc_ref):
    f = pl.program_id(2)

    @pl.when(f == 0)
    def _():
        h_ref[...] = _modnorm(x_ref[...], g_in_ref[...], mod_ref, 3, 4).astype(BF16)
        acc_ref[...] = jnp.zeros_like(acc_ref)

    h = h_ref[...]
    a = jnp.dot(h, wa_ref[...], preferred_element_type=F32)
    bb = jnp.dot(h, wb_ref[...], preferred_element_type=F32)
    acc_ref[...] += jnp.dot((_silu(a) * bb).astype(BF16), wo_ref[...], preferred_element_type=F32)

    @pl.when(f == pl.num_programs(2) - 1)
    def _():
        gate = mod_ref[5:6, :]
        o_ref[...] = x_ref[...] + gate * (_rms(acc_ref[...]) * g_out_ref[...])


def _ffn(x, mod, g_in, g_out, w_in, w_out, tm=512, tf=1408):
    b, t, d = x.shape
    fdim = w_out.shape[0]
    tm = _pick(t, tm)
    tf = _pick(fdim, tf)
    nf = fdim // tf
    return pl.pallas_call(
        _ffn_kernel,
        grid=(b, t // tm, nf),
        in_specs=[pl.BlockSpec((None, tm, d), lambda bi, i, f: (bi, i, 0)),
                  _mod_spec(mod),
                  pl.BlockSpec((1, d), lambda bi, i, f: (0, 0)),
                  pl.BlockSpec((1, d), lambda bi, i, f: (0, 0)),
                  pl.BlockSpec((d, tf), lambda bi, i, f: (0, f)),
                  pl.BlockSpec((d, tf), lambda bi, i, f: (0, f + nf)),
                  pl.BlockSpec((tf, d), lambda bi, i, f: (f, 0))],
        out_specs=pl.BlockSpec((None, tm, d), lambda bi, i, f: (bi, i, 0)),
        out_shape=jax.ShapeDtypeStruct((b, t, d), F32),
        scratch_shapes=[pltpu.VMEM((tm, d), BF16), pltpu.VMEM((tm, d), F32)],
        compiler_params=_cparams("parallel", "parallel", "arbitrary"),
        name="ffn",
    )(x, mod, g_in.reshape(1, d), g_out.reshape(1, d), w_in, w_in, w_out)


def _retention_kernel(lg_ref, qc_ref, kc_ref, vc_ref, gc_ref, qx_ref, kx_ref, vx_ref, gx_ref,
                      cos_ref, sin_ref, yc_ref, yx_ref, oc_ref, ox_ref, s_ref):
    cs = RET_CHUNK
    head = pl.program_id(1)
    row = lax.broadcasted_iota(jnp.int32, (cs, cs), 0).astype(F32)
    col = lax.broadcasted_iota(jnp.int32, (cs, cs), 1).astype(F32)
    pos = lax.broadcasted_iota(jnp.int32, (cs, 1), 0).astype(F32)
    k_scale = HEAD_DIM ** -0.5

    def load(q_ref, k_ref, start, rope):
        q = q_ref[pl.ds(start, cs), :]
        k = k_ref[pl.ds(start, cs), :] * k_scale
        if rope:
            cos = cos_ref[pl.ds(start, cs), :]
            sin = sin_ref[pl.ds(start, cs), :]
            q = q * cos + pltpu.roll(q, HEAD_DIM // 2, 1) * sin
            k = k * cos + pltpu.roll(k, HEAD_DIM // 2, 1) * sin
        return q, k

    for reverse in (False, True):
        lg = lg_ref[1 if reverse else 0, head]
        rel = (col - row) if reverse else (row - col)
        intra = jnp.where(rel >= 0, jnp.exp(lg * jnp.maximum(rel, 0.0)), 0.0)
        p = (cs - 1.0 - pos) if reverse else pos
        q_dec = jnp.exp(lg * (p + 1.0))
        k_dec = jnp.exp(lg * (cs - 1.0 - p))
        c_dec = jnp.exp(lg * cs)
        s_ref[...] = jnp.zeros_like(s_ref)

        for (q_ref, k_ref, v_ref, g_ref, o_ref, y_ref, rope) in (
                (qc_ref, kc_ref, vc_ref, gc_ref, oc_ref, yc_ref, False),
                (qx_ref, kx_ref, vx_ref, gx_ref, ox_ref, yx_ref, True)):
            n_chunks = q_ref.shape[0] // cs

            def step(i, carry, q_ref=q_ref, k_ref=k_ref, v_ref=v_ref, g_ref=g_ref, o_ref=o_ref,
                     y_ref=y_ref, rope=rope, n_chunks=n_chunks, reverse=reverse, intra=intra,
                     q_dec=q_dec, k_dec=k_dec, c_dec=c_dec):
                idx = (n_chunks - 1 - i) if reverse else i
                start = pl.multiple_of(idx * cs, cs)
                q, k = load(q_ref, k_ref, start, rope)
                v = v_ref[pl.ds(start, cs), :]
                state = s_ref[...]
                att = _mm_nt(q, k) * intra
                o = _mm(att, v) + _mm(q * q_dec, state)
                s_ref[...] = state * c_dec + _mm_tn(k * k_dec, v)
                if not reverse:
                    o_ref[pl.ds(start, cs), :] = o
                else:
                    o = o + o_ref[pl.ds(start, cs), :]
                    gate = g_ref[pl.ds(start, cs), :]
                    y_ref[pl.ds(start, cs), :] = (_rms(o) * _silu(gate)).astype(y_ref.dtype)
                return carry

            lax.fori_loop(0, n_chunks, step, 0)


def _retention(proj_c, proj_x, log_gamma, cos2, sin2):
    b, tc, _ = proj_c.shape
    tx = proj_x.shape[1]
    h = RET_HEADS

    def col(t, group):
        return pl.BlockSpec((None, t, HEAD_DIM), lambda bi, hi, lg: (bi, 0, group * h + hi))

    grid_spec = pltpu.PrefetchScalarGridSpec(
        num_scalar_prefetch=1,
        grid=(b, h),
        in_specs=[col(tc, 0), col(tc, 1), col(tc, 2), col(tc, 3),
                  col(tx, 0), col(tx, 1), col(tx, 2), col(tx, 3),
                  pl.BlockSpec((tx, HEAD_DIM), lambda bi, hi, lg: (0, 0)),
                  pl.BlockSpec((tx, HEAD_DIM), lambda bi, hi, lg: (0, 0))],
        out_specs=[pl.BlockSpec((None, tc, HEAD_DIM), lambda bi, hi, lg: (bi, 0, hi)),
                   pl.BlockSpec((None, tx, HEAD_DIM), lambda bi, hi, lg: (bi, 0, hi))],
        scratch_shapes=[pltpu.VMEM((tc, HEAD_DIM), F32), pltpu.VMEM((tx, HEAD_DIM), F32),
                        pltpu.VMEM((HEAD_DIM, HEAD_DIM), F32)])
    return pl.pallas_call(
        _retention_kernel,
        grid_spec=grid_spec,
        out_shape=[jax.ShapeDtypeStruct((b, tc, h * HEAD_DIM), BF16),
                   jax.ShapeDtypeStruct((b, tx, h * HEAD_DIM), BF16)],
        compiler_params=_cparams("parallel", "parallel"),
        name="retention",
    )(log_gamma, proj_c, proj_c, proj_c, proj_c, proj_x, proj_x, proj_x, proj_x, cos2, sin2)


def _hgrn_levels(cs):
    return [cs >> (i + 1) for i in range(int(math.log2(cs)))]


def _hgrn_constants(cs):
    t = np.arange(cs)[:, None]
    s = np.arange(cs)[None, :]
    tri = np.stack([(s <= t), (s >= t)]).astype(np.float32)
    masks = []
    for reverse in (False, True):
        lv = []
        for h in _hgrn_levels(cs):
            same = (t // (2 * h)) == (s // (2 * h))
            t_hi = (t % (2 * h)) >= h
            s_hi = (s % (2 * h)) >= h
            lv.append(same & (~t_hi & s_hi if reverse else t_hi & ~s_hi))
        lv.append(t == s)
        masks.append(np.stack(lv))
    return jnp.asarray(tri, BF16), jnp.asarray(np.stack(masks).astype(np.float32))


def _split3(x):
    hi = x.astype(BF16)
    r = x - hi.astype(F32)
    mid = r.astype(BF16)
    lo = (r - mid.astype(F32)).astype(BF16)
    return hi, mid, lo


def _level_reference(b, h, reverse):
    cs = b.shape[0]
    ref_local = h if reverse else h - 1
    if 2 * h >= 8:
        b3 = b.reshape(cs // (2 * h), 2 * h, b.shape[1])
        r = b3[:, ref_local:ref_local + 1, :]
        return jnp.broadcast_to(r, b3.shape).reshape(b.shape)
    local = lax.broadcasted_iota(jnp.int32, b.shape, 0) % (2 * h)
    out = b
    for m in range(2 * h):
        if m != ref_local:
            out = jnp.where(local == m, pltpu.roll(b, (m - ref_local) % cs, 0), out)
    return out


def _hgrn_kernel(lb_ref, tri_ref, mask_ref,
                 qc_ref, zfc_ref, zbc_ref, vc_ref, gc_ref,
                 qx_ref, zfx_ref, zbx_ref, vx_ref, gx_ref,
                 yc_ref, yx_ref, oc_ref, ox_ref, st_ref):
    cs = HGRN_CHUNK
    levels = _hgrn_levels(cs)
    lb = lb_ref[...]

    for reverse in (False, True):
        d = 1 if reverse else 0
        st_ref[...] = jnp.zeros_like(st_ref)
        for (q_ref, zf_ref, zb_ref, v_ref, g_ref, o_ref, y_ref) in (
                (qc_ref, zfc_ref, zbc_ref, vc_ref, gc_ref, oc_ref, yc_ref),
                (qx_ref, zfx_ref, zbx_ref, vx_ref, gx_ref, ox_ref, yx_ref)):
            z_ref = zb_ref if reverse else zf_ref
            n_chunks = q_ref.shape[0] // cs

            def step(i, carry, q_ref=q_ref, z_ref=z_ref, v_ref=v_ref, g_ref=g_ref, o_ref=o_ref,
                     y_ref=y_ref, n_chunks=n_chunks, reverse=reverse, d=d):
                idx = (n_chunks - 1 - i) if reverse else i
                start = pl.multiple_of(idx * cs, cs)
                q = _silu(q_ref[pl.ds(start, cs), :])
                z = z_ref[pl.ds(start, cs), :]
                v = v_ref[pl.ds(start, cs), :]
                f = lb + (1.0 - lb) * (1.0 / (1.0 + jnp.exp(-z)))
                k = 1.0 - f
                logf = jnp.log(f)
                tri = tri_ref[d]
                hi, mid, lo = _split3(logf)
                bsum = (jnp.dot(tri, hi, preferred_element_type=F32)
                        + jnp.dot(tri, mid, preferred_element_type=F32)
                        + jnp.dot(tri, lo, preferred_element_type=F32))
                att = _mm_nt(q, k) * mask_ref[d, len(levels)]
                for li, h in enumerate(levels):
                    w = jnp.exp(-jnp.abs(bsum - _level_reference(bsum, h, reverse)))
                    att += _mm_nt(q * w, k * w) * mask_ref[d, li]
                state_t = st_ref[...]
                o = _mm(att, v) + _mm_nt(q * jnp.exp(bsum), state_t)
                b_end = bsum[0:1, :] if reverse else bsum[cs - 1:cs, :]
                st_ref[...] = state_t * jnp.exp(b_end) + _mm_tn(v, k * jnp.exp(b_end - bsum))
                if not reverse:
                    o_ref[pl.ds(start, cs), :] = o
                else:
                    o = o + o_ref[pl.ds(start, cs), :]
                    gate = g_ref[pl.ds(start, cs), :]
                    y_ref[pl.ds(start, cs), :] = (_rms(o) * _silu(gate)).astype(y_ref.dtype)
                return carry

            lax.fori_loop(0, n_chunks, step, 0)


def _hgrn(proj_c, proj_x, lb):
    b, tc, _ = proj_c.shape
    tx = proj_x.shape[1]
    h = HGRN_HEADS
    first = 4 * RET_HEADS
    tri, masks = _hgrn_constants(HGRN_CHUNK)

    def col(t, group):
        return pl.BlockSpec((None, t, HEAD_DIM), lambda bi, hi: (bi, 0, first + group * h + hi))

    ins = [col(tc, 0), col(tc, 1), col(tc, 2), col(tc, 3), col(tc, 4),
           col(tx, 0), col(tx, 1), col(tx, 2), col(tx, 3), col(tx, 4)]
    return pl.pallas_call(
        _hgrn_kernel,
        grid=(b, h),
        in_specs=[pl.BlockSpec((None, 1, HEAD_DIM), lambda bi, hi: (hi, 0, 0)),
                  pl.BlockSpec(tri.shape, lambda bi, hi: (0, 0, 0)),
                  pl.BlockSpec(masks.shape, lambda bi, hi: (0, 0, 0, 0))] + ins,
        out_specs=[pl.BlockSpec((None, tc, HEAD_DIM), lambda bi, hi: (bi, 0, hi)),
                   pl.BlockSpec((None, tx, HEAD_DIM), lambda bi, hi: (bi, 0, hi))],
        out_shape=[jax.ShapeDtypeStruct((b, tc, h * HEAD_DIM), BF16),
                   jax.ShapeDtypeStruct((b, tx, h * HEAD_DIM), BF16)],
        scratch_shapes=[pltpu.VMEM((tc, HEAD_DIM), F32), pltpu.VMEM((tx, HEAD_DIM), F32),
                        pltpu.VMEM((HEAD_DIM, HEAD_DIM), F32)],
        compiler_params=_cparams("parallel", "parallel"),
        name="hgrn2",
    )(lb.reshape(h, 1, HEAD_DIM), tri, masks, *([proj_c] * 5), *([proj_x] * 5))


MLA_QW = 2 * LANE
MLA_DOWN_PAD = MLA_Q_RANK + MLA_KV_RANK + LANE


def _mla_proj_kernel(x_ref, mod_ref, g_ref, wd_ref, gq_ref, gkv_ref, wq_ref, wkv_ref, cos_ref, sin_ref,
                     *out_refs, rope, want_q):
    h = _modnorm(x_ref[...], g_ref[...], mod_ref, 0, 1).astype(BF16)
    down = jnp.dot(h, wd_ref[...], preferred_element_type=F32)
    if rope:
        cos = cos_ref[...]
        sin = sin_ref[...]

    def rot(v):
        return v * cos + pltpu.roll(v, LANE // 2, 1) * sin if rope else v

    if want_q:
        q_ref, kn_ref, kr_ref, v_ref = out_refs
        qn = (_rms(down[:, :MLA_Q_RANK]) * gq_ref[...]).astype(BF16)
        q = jnp.dot(qn, wq_ref[...], preferred_element_type=F32)
        for hd in range(MLA_HEADS):
            lo = hd * MLA_QW
            q_ref[:, lo:lo + LANE] = q[:, lo:lo + LANE].astype(q_ref.dtype)
            q_ref[:, lo + LANE:lo + 2 * LANE] = rot(q[:, lo + LANE:lo + 2 * LANE]).astype(q_ref.dtype)
    else:
        kn_ref, kr_ref, v_ref = out_refs
    kvn = (_rms(down[:, MLA_Q_RANK:MLA_Q_RANK + MLA_KV_RANK]) * gkv_ref[...]).astype(BF16)
    kv = jnp.dot(kvn, wkv_ref[...], preferred_element_type=F32)
    nk = MLA_HEADS * MLA_NOPE
    kn_ref[...] = kv[:, :nk].astype(kn_ref.dtype)
    v_ref[...] = kv[:, nk:].astype(v_ref.dtype)
    kr_ref[...] = rot(down[:, MLA_Q_RANK + MLA_KV_RANK:]).astype(kr_ref.dtype)


def _mla_proj(x, mod, g, wd, gq, gkv, wq, wkv, cos, sin, rope, want_q, tm=512):
    b, t, d = x.shape
    tm = _pick(t, tm)
    const = lambda bi, i: (0, 0)
    tok = lambda bi, i: (bi, i, 0)
    nk, nv = MLA_HEADS * MLA_NOPE, MLA_HEADS * MLA_V
    out_shape = [jax.ShapeDtypeStruct((b, t, nk), BF16), jax.ShapeDtypeStruct((b, t, LANE), BF16),
                 jax.ShapeDtypeStruct((b, t, nv), BF16)]
    out_specs = [pl.BlockSpec((None, tm, nk), tok), pl.BlockSpec((None, tm, LANE), tok),
                 pl.BlockSpec((None, tm, nv), tok)]
    if want_q:
        out_shape = [jax.ShapeDtypeStruct((b, t, MLA_HEADS * MLA_QW), BF16)] + out_shape
        out_specs = [pl.BlockSpec((None, tm, MLA_HEADS * MLA_QW), tok)] + out_specs
    return pl.pallas_call(
        functools.partial(_mla_proj_kernel, rope=rope, want_q=want_q),
        grid=(b, t // tm),
        in_specs=[pl.BlockSpec((None, tm, d), tok), _mod_spec(mod), pl.BlockSpec((1, d), const),
                  pl.BlockSpec(wd.shape, const), pl.BlockSpec((1, MLA_Q_RANK), const),
                  pl.BlockSpec((1, MLA_KV_RANK), const), pl.BlockSpec(wq.shape, const),
                  pl.BlockSpec(wkv.shape, const),
                  pl.BlockSpec((tm, LANE), lambda bi, i: (i, 0)),
                  pl.BlockSpec((tm, LANE), lambda bi, i: (i, 0))],
        out_specs=out_specs,
        out_shape=out_shape,
        compiler_params=_cparams("parallel", "parallel"),
        name="mla_proj",
    )(x, mod, g.reshape(1, d), wd, gq.reshape(1, -1), gkv.reshape(1, -1), wq, wkv, cos, sin)


def _lane_fold(x, op):
    out = x[:, :LANE]
    for j in range(1, x.shape[1] // LANE):
        out = op(out, x[:, j * LANE:(j + 1) * LANE])
    return out


def _attention_kernel(q_ref, knc_ref, krc_ref, vc_ref, knx_ref, krx_ref, vx_ref, o_ref,
                      kc_ref, kx_ref, sc0_ref, sc1_ref, sx0_ref, sx1_ref, m0_ref, m1_ref, *, tk, exp_scale):
    step = pl.program_id(2)

    @pl.when(step == 0)
    def _():
        kc_ref[:, :LANE] = knc_ref[...]
        kc_ref[:, LANE:] = krc_ref[...]
        kx_ref[:, :LANE] = knx_ref[...]
        kx_ref[:, LANE:] = krx_ref[...]
        sc1_ref[...] = jnp.zeros(sc1_ref.shape, F32)
        sx1_ref[...] = jnp.zeros(sx1_ref.shape, F32)
        m1_ref[...] = jnp.zeros(m1_ref.shape, F32)

    n_blocks = kx_ref.shape[0] // tk
    nt = (((1,), (1,)), ((), ()))

    def body(sc_cur, sx_cur, m_cur, sc_prev, sx_prev, m_prev):
        q = q_ref[...]
        m = jnp.max(m_prev[...], axis=-1, keepdims=True)
        s = lax.dot_general(q, kc_ref[...], nt, preferred_element_type=F32)
        sc_cur[...] = s
        m_part = _lane_fold(s, jnp.maximum)
        p = jnp.exp2((sc_prev[...] - m) * exp_scale)
        l_part = _lane_fold(p, jnp.add)
        acc = jnp.dot(p.astype(BF16), vc_ref[...], preferred_element_type=F32)
        for i in range(n_blocks):
            s = lax.dot_general(q, kx_ref[i * tk:(i + 1) * tk, :], nt, preferred_element_type=F32)
            sx_cur[i] = s
            m_part = jnp.maximum(m_part, _lane_fold(s, jnp.maximum))
            p = jnp.exp2((sx_prev[i] - m) * exp_scale)
            l_part = l_part + _lane_fold(p, jnp.add)
            acc = acc + jnp.dot(p.astype(BF16), vx_ref[i * tk:(i + 1) * tk, :],
                                preferred_element_type=F32)
        m_cur[...] = m_part
        l = jnp.sum(l_part, axis=-1, keepdims=True)
        o_ref[...] = (acc / l).astype(o_ref.dtype)

    @pl.when(step % 2 == 0)
    def _():
        body(sc0_ref, sx0_ref, m0_ref, sc1_ref, sx1_ref, m1_ref)

    @pl.when(step % 2 == 1)
    def _():
        body(sc1_ref, sx1_ref, m1_ref, sc0_ref, sx0_ref, m0_ref)


def _attention(q, kn_c, kr_c, v_c, kn_x, kr_x, v_x, scale, tq=256, tk=512):
    b, t, _ = q.shape
    tc = kn_c.shape[1]
    tq = _pick(t, tq)
    tk = _pick(t, tk)
    nq = t // tq
    head_c = lambda bi, hi, i: (bi, 0, hi)
    shared = lambda bi, hi, i: (bi, 0, 0)
    return pl.pallas_call(
        functools.partial(_attention_kernel, tk=tk, exp_scale=scale * math.log2(math.e)),
        grid=(b, MLA_HEADS, nq + 1),
        in_specs=[pl.BlockSpec((None, tq, MLA_QW), lambda bi, hi, i: (bi, jnp.minimum(i, nq - 1), hi)),
                  pl.BlockSpec((None, tc, LANE), head_c), pl.BlockSpec((None, tc, LANE), shared),
                  pl.BlockSpec((None, tc, LANE), head_c),
                  pl.BlockSpec((None, t, LANE), head_c), pl.BlockSpec((None, t, LANE), shared),
                  pl.BlockSpec((None, t, LANE), head_c)],
        out_specs=pl.BlockSpec((None, tq, MLA_V), lambda bi, hi, i: (bi, jnp.maximum(i - 1, 0), hi)),
        out_shape=jax.ShapeDtypeStruct((b, t, MLA_HEADS * MLA_V), BF16),
        scratch_shapes=[pltpu.VMEM((tc, MLA_QW), BF16), pltpu.VMEM((t, MLA_QW), BF16),
                        pltpu.VMEM((tq, tc), F32), pltpu.VMEM((tq, tc), F32),
                        pltpu.VMEM((t // tk, tq, tk), F32), pltpu.VMEM((t // tk, tq, tk), F32),
                        pltpu.VMEM((tq, LANE), F32), pltpu.VMEM((tq, LANE), F32)],
        compiler_params=_cparams("parallel", "parallel", "arbitrary"),
        name="mla_attention",
    )(q, kn_c, kr_c, v_c, kn_x, kr_x, v_x)


def _router_kernel(x_ref, mod_ref, g_ref, wr_ref, h_ref, w_ref, e_ref):
    h = _modnorm(x_ref[...], g_ref[...], mod_ref, 3, 4)
    h_ref[...] = _pack_halves(h)
    logits = jnp.dot(h, wr_ref[...], preferred_element_type=F32, precision=lax.Precision.HIGHEST)
    lane = lax.broadcasted_iota(jnp.int32, logits.shape, 1)
    neg = jnp.float32(-jnp.inf)
    lg = jnp.where(lane < N_EXPERTS, logits, neg)
    m1 = jnp.max(lg, axis=-1, keepdims=True)
    i1 = jnp.min(jnp.where(lg == m1, lane, LANE), axis=-1, keepdims=True)
    lg2 = jnp.where(lane == i1, neg, lg)
    m2 = jnp.max(lg2, axis=-1, keepdims=True)
    i2 = jnp.min(jnp.where(lg2 == m2, lane, LANE), axis=-1, keepdims=True)
    e2 = jnp.exp(m2 - m1)
    w1 = 1.0 / (1.0 + e2)
    w2 = e2 / (1.0 + e2)
    w_ref[...] = jnp.where(lane == 0, w1, jnp.where(lane == 1, w2, 0.0))
    e_ref[...] = jnp.where(lane == 0, i1, jnp.where(lane == 1, i2, 0))


def _router(x, mod, g, router_pad, tm=512):
    b, t, d = x.shape
    tm = _pick(t, tm)
    tok = lambda bi, i: (bi, i, 0)
    const = lambda bi, i: (0, 0)
    return pl.pallas_call(
        _router_kernel,
        grid=(b, t // tm),
        in_specs=[pl.BlockSpec((None, tm, d), tok), _mod_spec(mod), pl.BlockSpec((1, d), const),
                  pl.BlockSpec((d, LANE), const)],
        out_specs=[pl.BlockSpec((None, tm, d // 2), tok), pl.BlockSpec((None, tm, LANE), tok),
                   pl.BlockSpec((None, tm, LANE), tok)],
        out_shape=[jax.ShapeDtypeStruct((b, t, d // 2), jnp.int32), jax.ShapeDtypeStruct((b, t, LANE), F32),
                   jax.ShapeDtypeStruct((b, t, LANE), jnp.int32)],
        compiler_params=_cparams("parallel", "parallel"),
        name="moe_router",
    )(x, mod, g.reshape(1, d), router_pad)


SC_GATHER_ROWS = 64


def _gather_rows(table, idx):
    info = plsc.get_sparse_core_info()
    n_workers = info.num_cores * info.num_subcores
    n, width = idx.shape[0], table.shape[1]
    chunk = SC_GATHER_ROWS
    assert n % (n_workers * chunk) == 0
    n_chunks = n // (n_workers * chunk)
    mesh = plsc.VectorSubcoreMesh(core_axis_name="c", subcore_axis_name="s")

    def body(table_hbm, idx_hbm, out_hbm, idx_v, rows_v, sem):
        wid = lax.axis_index("s") * info.num_cores + lax.axis_index("c")
        pltpu.sync_copy(idx_hbm.at[wid], idx_v)

        @pl.loop(0, n_chunks)
        def _(j):
            pltpu.async_copy(table_hbm.at[idx_v.at[j]], rows_v, sem).wait()
            pltpu.sync_copy(rows_v, out_hbm.at[pl.ds((wid * n_chunks + j) * chunk, chunk)])

    return pl.kernel(
        body,
        out_type=jax.ShapeDtypeStruct((n, width), table.dtype),
        mesh=mesh,
        scratch_types=[pltpu.VMEM((n_chunks, chunk), jnp.int32), pltpu.VMEM((chunk, width), table.dtype),
                       pltpu.SemaphoreType.DMA],
        name="sc_gather_rows",
    )(table, idx.reshape(n_workers, n_chunks, chunk))


def _pack_halves(x):
    k = x.shape[1] // 2
    bits = lambda v: lax.bitcast_convert_type(v.astype(BF16).astype(F32), jnp.uint32)
    word = lax.shift_right_logical(bits(x[:, :k]), jnp.uint32(16)) | bits(x[:, k:])
    return lax.bitcast_convert_type(word, jnp.int32)


def _unpack_halves(w):
    u = lax.bitcast_convert_type(w, jnp.uint32)
    lo = lax.bitcast_convert_type(lax.shift_left(u, jnp.uint32(16)), F32)
    hi = lax.bitcast_convert_type(u & jnp.uint32(0xFFFF0000), F32)
    return jnp.concatenate([lo, hi], axis=-1)


def _expert_ffn_kernel(te_ref, tv_ref, x_ref, wrow_ref, wa_ref, wb_ref, wo_ref, o_ref, acc_ref):
    j = pl.program_id(0)
    f = pl.program_id(1)
    last = pl.num_programs(1) - 1
    valid = tv_ref[j] != 0

    @pl.when(valid)
    def _():
        @pl.when(f == 0)
        def _():
            acc_ref[...] = jnp.zeros_like(acc_ref)

        h = _unpack_halves(x_ref[...]).astype(BF16)
        a = jnp.dot(h, wa_ref[...], preferred_element_type=F32)
        bb = jnp.dot(h, wb_ref[...], preferred_element_type=F32)
        acc_ref[...] += jnp.dot((_silu(a) * bb).astype(BF16), wo_ref[...], preferred_element_type=F32)

        @pl.when(f == last)
        def _():
            o_ref[...] = _pack_halves(acc_ref[...] * wrow_ref[...])

    @pl.when(jnp.logical_not(valid) & (f == last))
    def _():
        o_ref[...] = jnp.zeros_like(o_ref)


def _expert_ffn(xs, w_rows, tile_expert, tile_valid, w_in, w_out, tm, tf=1408):
    p = xs.shape[0]
    n_exp, fdim, d = w_out.shape
    tf = _pick(fdim, tf)
    nf = fdim // tf

    def f_eff(j, f, tv):
        return jnp.where(tv[j] != 0, f, nf - 1)

    grid_spec = pltpu.PrefetchScalarGridSpec(
        num_scalar_prefetch=2,
        grid=(p // tm, nf),
        in_specs=[pl.BlockSpec((tm, d // 2), lambda j, f, te, tv: (j, 0)),
                  pl.BlockSpec((tm, 1), lambda j, f, te, tv: (j, 0)),
                  pl.BlockSpec((None, d, tf), lambda j, f, te, tv: (te[j], 0, f_eff(j, f, tv))),
                  pl.BlockSpec((None, d, tf), lambda j, f, te, tv: (te[j], 0, f_eff(j, f, tv) + nf)),
                  pl.BlockSpec((None, tf, d), lambda j, f, te, tv: (te[j], f_eff(j, f, tv), 0))],
        out_specs=pl.BlockSpec((tm, d // 2), lambda j, f, te, tv: (j, 0)),
        scratch_shapes=[pltpu.VMEM((tm, d), F32)])
    return pl.pallas_call(
        _expert_ffn_kernel,
        grid_spec=grid_spec,
        out_shape=jax.ShapeDtypeStruct((p, d // 2), jnp.int32),
        compiler_params=_cparams("parallel", "arbitrary"),
        name="moe_expert_ffn",
    )(tile_expert, tile_valid, xs, w_rows, w_in, w_in, w_out)


def _moe_combine_kernel(x_ref, y_ref, mod_ref, g_ref, o_ref):
    k = x_ref.shape[-1] // 2
    y = _unpack_halves(y_ref[:, :k]) + _unpack_halves(y_ref[:, k:])
    o_ref[...] = x_ref[...] + mod_ref[5:6, :] * (_rms(y) * g_ref[...])


def _moe_combine(x, y2, mod, g, tm=512):
    b, t, d = x.shape
    tm = _pick(t, tm)
    tok = lambda bi, i: (bi, i, 0)
    return pl.pallas_call(
        _moe_combine_kernel,
        grid=(b, t // tm),
        in_specs=[pl.BlockSpec((None, tm, d), tok), pl.BlockSpec((None, tm, d), tok), _mod_spec(mod),
                  pl.BlockSpec((1, d), lambda bi, i: (0, 0))],
        out_specs=pl.BlockSpec((None, tm, d), tok),
        out_shape=jax.ShapeDtypeStruct((b, t, d), F32),
        compiler_params=_cparams("parallel", "parallel"),
        name="moe_combine",
    )(x, y2, mod, g.reshape(1, d))


def _moe(x, mod, g_in, g_out, router_pad, w_in, w_out, tm=512):
    b, t, d = x.shape
    n_exp = w_out.shape[0]
    n = b * t
    h, w, e = _router(x, mod, g_in, router_pad)
    e_flat = e[..., :2].reshape(2 * n)
    w_flat = w[..., :2].reshape(2 * n)
    onehot = (e_flat[:, None] == jnp.arange(n_exp, dtype=jnp.int32)).astype(jnp.int32)
    csum = jnp.cumsum(onehot, axis=0)
    rank = jnp.sum((csum - onehot) * onehot, axis=-1)
    counts = csum[-1]
    ends = jnp.cumsum((counts + tm - 1) // tm * tm)
    dest = (ends - (counts + tm - 1) // tm * tm)[e_flat] + rank
    p = 2 * n + n_exp * tm
    src = jnp.zeros((p,), jnp.int32).at[dest].set(jnp.arange(2 * n, dtype=jnp.int32) // 2)
    w_rows = jnp.zeros((p,), F32).at[dest].set(w_flat)
    tile_start = jnp.arange(p // tm, dtype=jnp.int32) * tm
    tile_expert = jnp.minimum(jnp.searchsorted(ends, tile_start, side="right"), n_exp - 1).astype(jnp.int32)
    tile_valid = (tile_start < ends[-1]).astype(jnp.int32)

    xs = _gather_rows(h.reshape(n, d // 2), src)
    ys = _expert_ffn(xs, w_rows.reshape(p, 1), tile_expert, tile_valid, w_in, w_out, tm)
    y2 = _gather_rows(ys, dest)
    return _moe_combine(x, y2.reshape(b, t, d), mod, g_out)


def _axial_angles(n_tokens, dim):
    t = jnp.arange(n_tokens)
    rows = (t // GRID_W).astype(F32)
    cols = (t % GRID_W).astype(F32)
    n_freq = dim // 4
    inv = ROPE_BASE ** (-jnp.arange(n_freq, dtype=F32) / n_freq)
    return jnp.concatenate([rows[:, None] * inv, cols[:, None] * inv], axis=-1)


def _rope_tables(n_tokens, dim):
    ang = _axial_angles(n_tokens, dim)
    cos, sin = jnp.cos(ang), jnp.sin(ang)
    if dim == LANE:
        return jnp.concatenate([cos, cos], -1), jnp.concatenate([-sin, sin], -1)
    z = jnp.zeros_like(cos)
    return jnp.concatenate([cos, z, cos, z], -1), jnp.concatenate([-sin, z, sin, z], -1)


def _spread_rope_cols(w):
    half = MLA_ROPE // 2
    z = jnp.zeros(w.shape[:-1] + (half,), w.dtype)
    return jnp.concatenate([w[..., :half], z, w[..., half:], z], axis=-1)


def kernel(x, c, ctx, c_ctx, mod_w, mod_b, norm_g, mix_in_w, ret_decay_logit, hgrn_lb_logit, mix_out_w,
           ffn_in_w, ffn_out_w, mla_down_w, mla_q_norm_g, mla_kv_norm_g, mla_uq_w, mla_ukv_w, mla_out_w,
           router_w, moe_in_w, moe_out_w):
    batch, seq, d = x.shape
    assert mod_w.shape[0] == 2, "two layers: retention/HGRN2 then MLA/MoE"

    rows = -(-(batch + 1) // 8) * 8
    cond = jnp.zeros((rows, d), F32).at[:batch].set(c).at[batch].set(c_ctx)
    mod = _modulation(cond, mod_w, mod_b)
    mod_x = [mod[l, :batch].reshape(batch, 6, d) for l in range(2)]
    mod_c = [mod[l, batch:batch + 1].reshape(1, 6, d) for l in range(2)]

    g = norm_g[0]
    w_in = mix_in_w[0].astype(BF16)
    proj_x = _norm_matmul(x, mod_x[0], g[0], w_in, 0, 1, F32)
    proj_c = _norm_matmul(ctx, mod_c[0], g[0], w_in, 0, 1, F32)
    cos2, sin2 = _rope_tables(seq, HEAD_DIM)
    log_gamma = jax.nn.log_sigmoid(ret_decay_logit[0].astype(F32))
    lb = jnp.cumsum(jax.nn.softmax(hgrn_lb_logit.astype(F32), axis=0), axis=0)[0]
    ret_c, ret_x = _retention(proj_c, proj_x, log_gamma, cos2, sin2)
    hg_c, hg_x = _hgrn(proj_c, proj_x, lb)
    w_out = mix_out_w[0].astype(BF16)
    ret_w = RET_HEADS * HEAD_DIM
    w_outs = [w_out[:ret_w], w_out[ret_w:]]
    x = _proj_residual([ret_x, hg_x], w_outs, x, mod_x[0], g[1], 2)
    ctx = _proj_residual([ret_c, hg_c], w_outs, ctx, mod_c[0], g[1], 2)
    f_in, f_out = ffn_in_w[0].astype(BF16), ffn_out_w[0].astype(BF16)
    x = _ffn(x, mod_x[0], g[2], g[3], f_in, f_out)
    ctx = _ffn(ctx, mod_c[0], g[2], g[3], f_in, f_out)

    g = norm_g[1]
    dq = MLA_NOPE + MLA_ROPE
    wd = mla_down_w[0]
    wd = jnp.concatenate([wd[:, :MLA_Q_RANK + MLA_KV_RANK],
                          _spread_rope_cols(wd[:, MLA_Q_RANK + MLA_KV_RANK:])], axis=-1).astype(BF16)
    wq = mla_uq_w[0].reshape(MLA_Q_RANK, MLA_HEADS, dq)
    wq = jnp.concatenate([wq[..., :MLA_NOPE], _spread_rope_cols(wq[..., MLA_NOPE:])], axis=-1)
    wq = wq.reshape(MLA_Q_RANK, MLA_HEADS * MLA_QW).astype(BF16)
    wkv = mla_ukv_w[0].reshape(MLA_KV_RANK, MLA_HEADS, MLA_NOPE + MLA_V)
    wkv = jnp.concatenate([wkv[..., :MLA_NOPE].reshape(MLA_KV_RANK, -1),
                           wkv[..., MLA_NOPE:].reshape(MLA_KV_RANK, -1)], axis=-1).astype(BF16)
    cos_m, sin_m = _rope_tables(seq, MLA_ROPE)
    q, kn_x, kr_x, v_x = _mla_proj(x, mod_x[1], g[0], wd, mla_q_norm_g[0], mla_kv_norm_g[0], wq, wkv,
                                   cos_m, sin_m, rope=True, want_q=True)
    tc = ctx.shape[1]
    kn_c, kr_c, v_c = _mla_proj(ctx, mod_c[1], g[0], wd, mla_q_norm_g[0], mla_kv_norm_g[0], wq, wkv,
                                cos_m[:tc], sin_m[:tc], rope=False, want_q=False)
    att = _attention(q, kn_c, kr_c, v_c, kn_x, kr_x, v_x, dq ** -0.5)
    x = _proj_residual([att], [mla_out_w[0].astype(BF16)], x, mod_x[1], g[1], 2)
    router_pad = jnp.zeros((d, LANE), F32).at[:, :N_EXPERTS].set(router_w[0])
    x = _moe(x, mod_x[1], g[2], g[3], router_pad, moe_in_w[0].astype(BF16), moe_out_w[0].astype(BF16))
    return x
```

```python
import functools
import math

import jax
import jax.numpy as jnp
import numpy as np
from jax import lax
from jax.experimental import pallas as pl
from jax.experimental.pallas import tpu as pltpu
from jax.experimental.pallas import tpu_sc as plsc

F32 = jnp.float32
BF16 = jnp.bfloat16

NORM_EPS = 1e-6
ROPE_BASE = 10000.0
GRID_W = 64
LANE = 128

RET_HEADS = 4
HGRN_HEADS = 4
HEAD_DIM = 128
RET_CHUNK = 128
HGRN_CHUNK = 128

MLA_HEADS = 8
MLA_NOPE = 128
MLA_ROPE = 64
MLA_V = 128
MLA_Q_RANK = 384
MLA_KV_RANK = 256
N_EXPERTS = 8

VMEM_LIMIT = 56 * 1024 * 1024


def _cparams(*sem):
    return pltpu.CompilerParams(dimension_semantics=sem, vmem_limit_bytes=VMEM_LIMIT)


def _mm(a, b):
    return jnp.dot(a.astype(BF16), b.astype(BF16), preferred_element_type=F32)


def _mm_nt(a, b):
    return lax.dot_general(a.astype(BF16), b.astype(BF16), (((1,), (1,)), ((), ())),
                           preferred_element_type=F32)


def _mm_tn(a, b):
    return lax.dot_general(a.astype(BF16), b.astype(BF16), (((0,), (0,)), ((), ())),
                           preferred_element_type=F32)


def _rms(x):
    return x * lax.rsqrt(jnp.mean(x * x, axis=-1, keepdims=True) + NORM_EPS)


def _silu(x):
    return x * (1.0 / (1.0 + jnp.exp(-x)))


def _modnorm(x, g, mod_ref, shift_row, scale_row):
    shift = mod_ref[shift_row:shift_row + 1, :]
    scale = mod_ref[scale_row:scale_row + 1, :]
    return _rms(x) * g * (1.0 + scale) + shift


def _mod_spec(mod):
    d = mod.shape[-1]
    if mod.shape[0] == 1:
        return pl.BlockSpec((None, 6, d), lambda b, *_: (0, 0, 0))
    return pl.BlockSpec((None, 6, d), lambda b, *_: (b, 0, 0))


def _pick(n, pref):
    t = min(pref, n)
    while n % t:
        t -= LANE
    return t


def _mod_kernel(c_ref, w_ref, b_ref, o_ref):
    a = _silu(c_ref[...])
    o_ref[...] = jnp.dot(a, w_ref[...], preferred_element_type=F32,
                         precision=lax.Precision.HIGHEST) + b_ref[...]


def _modulation(cond, mod_w, mod_b):
    n_layers, d, n = mod_w.shape
    r = cond.shape[0]
    tn = _pick(n, 1536)
    return pl.pallas_call(
        _mod_kernel,
        grid=(n_layers, n // tn),
        in_specs=[pl.BlockSpec((r, d), lambda l, j: (0, 0)),
                  pl.BlockSpec((None, d, tn), lambda l, j: (l, 0, j)),
                  pl.BlockSpec((None, 1, tn), lambda l, j: (l, 0, j))],
        out_specs=pl.BlockSpec((None, r, tn), lambda l, j: (l, 0, j)),
        out_shape=jax.ShapeDtypeStruct((n_layers, r, n), F32),
        compiler_params=_cparams("parallel", "parallel"),
        name="modulation",
    )(cond, mod_w, mod_b.reshape(n_layers, 1, n))


def _norm_matmul_kernel(x_ref, mod_ref, g_ref, w_ref, o_ref, h_ref, *, shift_row, scale_row):
    @pl.when(pl.program_id(2) == 0)
    def _():
        h_ref[...] = _modnorm(x_ref[...], g_ref[...], mod_ref, shift_row, scale_row).astype(BF16)

    o_ref[...] = jnp.dot(h_ref[...], w_ref[...], preferred_element_type=F32).astype(o_ref.dtype)


def _norm_matmul(x, mod, g, w, shift_row, scale_row, out_dtype, tm=512, tn=1536):
    b, t, d = x.shape
    n = w.shape[1]
    tm = _pick(t, tm)
    tn = _pick(n, tn)
    return pl.pallas_call(
        functools.partial(_norm_matmul_kernel, shift_row=shift_row, scale_row=scale_row),
        grid=(b, t // tm, n // tn),
        in_specs=[pl.BlockSpec((None, tm, d), lambda bi, i, j: (bi, i, 0)),
                  _mod_spec(mod),
                  pl.BlockSpec((1, d), lambda bi, i, j: (0, 0)),
                  pl.BlockSpec((d, tn), lambda bi, i, j: (0, j))],
        out_specs=pl.BlockSpec((None, tm, tn), lambda bi, i, j: (bi, i, j)),
        out_shape=jax.ShapeDtypeStruct((b, t, n), out_dtype),
        scratch_shapes=[pltpu.VMEM((tm, d), BF16)],
        compiler_params=_cparams("parallel", "parallel", "arbitrary"),
        name="norm_matmul",
    )(x, mod, g.reshape(1, d), w)


def _proj_residual_kernel(*refs, n_lhs, gate_row):
    lhs = refs[:n_lhs]
    ws = refs[n_lhs:2 * n_lhs]
    x_ref, mod_ref, g_ref, o_ref = refs[2 * n_lhs:]
    y = jnp.dot(lhs[0][...], ws[0][...], preferred_element_type=F32)
    for a_ref, w_ref in zip(lhs[1:], ws[1:]):
        y += jnp.dot(a_ref[...], w_ref[...], preferred_element_type=F32)
    gate = mod_ref[gate_row:gate_row + 1, :]
    o_ref[...] = x_ref[...] + gate * (_rms(y) * g_ref[...])


def _proj_residual(lhs, ws, x, mod, g, gate_row, tm=512):
    b, t, d = x.shape
    tm = _pick(t, tm)
    n_lhs = len(lhs)
    in_specs = [pl.BlockSpec((None, tm, a.shape[-1]), lambda bi, i: (bi, i, 0)) for a in lhs]
    in_specs += [pl.BlockSpec(w.shape, lambda bi, i: (0, 0)) for w in ws]
    in_specs += [pl.BlockSpec((None, tm, d), lambda bi, i: (bi, i, 0)),
                 _mod_spec(mod),
                 pl.BlockSpec((1, d), lambda bi, i: (0, 0))]
    return pl.pallas_call(
        functools.partial(_proj_residual_kernel, n_lhs=n_lhs, gate_row=gate_row),
        grid=(b, t // tm),
        in_specs=in_specs,
        out_specs=pl.BlockSpec((None, tm, d), lambda bi, i: (bi, i, 0)),
        out_shape=jax.ShapeDtypeStruct((b, t, d), F32),
        compiler_params=_cparams("parallel", "parallel"),
        name="proj_residual",
    )(*lhs, *ws, x, mod, g.reshape(1, d))


def _ffn_kernel(x_ref, mod_ref, g_in_ref, g_out_ref, wa_ref, wb_ref, wo_ref, o_ref, h_ref, acc_ref):
    f = pl.program_id(2)

    @pl.when(f == 0)
    def _():
        h_ref[...] = _modnorm(x_ref[...], g_in_ref[...], mod_ref, 3, 4).astype(BF16)
        acc_ref[...] = jnp.zeros_like(acc_ref)

    h = h_ref[...]
    a = jnp.dot(h, wa_ref[...], preferred_element_type=F32)
    bb = jnp.dot(h, wb_ref[...], preferred_element_type=F32)
    acc_ref[...] += jnp.dot((_silu(a) * bb).astype(BF16), wo_ref[...], preferred_element_type=F32)

    @pl.when(f == pl.num_programs(2) - 1)
    def _():
        gate = mod_ref[5:6, :]
        o_ref[...] = x_ref[...] + gate * (_rms(acc_ref[...]) * g_out_ref[...])


def _ffn(x, mod, g_in, g_out, w_in, w_out, tm=512, tf=1408):
    b, t, d = x.shape
    fdim = w_out.shape[0]
    tm = _pick(t, tm)
    tf = _pick(fdim, tf)
    nf = fdim // tf
    return pl.pallas_call(
        _ffn_kernel,
        grid=(b, t // tm, nf),
        in_specs=[pl.BlockSpec((None, tm, d), lambda bi, i, f: (bi, i, 0)),
                  _mod_spec(mod),
                  pl.BlockSpec((1, d), lambda bi, i, f: (0, 0)),
                  pl.BlockSpec((1, d), lambda bi, i, f: (0, 0)),
                  pl.BlockSpec((d, tf), lambda bi, i, f: (0, f)),
                  pl.BlockSpec((d, tf), lambda bi, i, f: (0, f + nf)),
                  pl.BlockSpec((tf, d), lambda bi, i, f: (f, 0))],
        out_specs=pl.BlockSpec((None, tm, d), lambda bi, i, f: (bi, i, 0)),
        out_shape=jax.ShapeDtypeStruct((b, t, d), F32),
        scratch_shapes=[pltpu.VMEM((tm, d), BF16), pltpu.VMEM((tm, d), F32)],
        compiler_params=_cparams("parallel", "parallel", "arbitrary"),
        name="ffn",
    )(x, mod, g_in.reshape(1, d), g_out.reshape(1, d), w_in, w_in, w_out)


SCAN_HEADS = 2


def _head_slice(hh):
    return slice(hh * HEAD_DIM, (hh + 1) * HEAD_DIM)


def _finish_scan(of_ref, ob_ref, g_ref, y_ref, cs):
    def step(i, carry):
        start = pl.multiple_of(i * cs, cs)
        for hh in range(y_ref.shape[1] // HEAD_DIM):
            sl = _head_slice(hh)
            o = of_ref[pl.ds(start, cs), sl] + ob_ref[pl.ds(start, cs), sl]
            gate = g_ref[pl.ds(start, cs), sl].astype(F32)
            y_ref[pl.ds(start, cs), sl] = (_rms(o) * _silu(gate)).astype(y_ref.dtype)
        return carry

    lax.fori_loop(0, y_ref.shape[0] // cs, step, 0)


def _retention_kernel(lg_ref, qc_ref, kc_ref, vc_ref, gc_ref, qx_ref, kx_ref, vx_ref, gx_ref,
                      cos_ref, sin_ref, yc_ref, yx_ref,
                      ofc_ref, obc_ref, ofx_ref, obx_ref, s_ref, intra_ref, qd_ref, kd_ref):
    cs = RET_CHUNK
    hp = SCAN_HEADS
    head0 = pl.program_id(1) * hp
    row = lax.broadcasted_iota(jnp.int32, (cs, cs), 0).astype(F32)
    col = lax.broadcasted_iota(jnp.int32, (cs, cs), 1).astype(F32)
    pos = lax.broadcasted_iota(jnp.int32, (cs, HEAD_DIM), 0).astype(F32)
    k_scale = HEAD_DIM ** -0.5

    for hh in range(hp):
        for d in (0, 1):
            c = 2 * hh + d
            lg = lg_ref[d, head0 + hh]
            rel = (col - row) if d else (row - col)
            intra_ref[c] = jnp.where(rel >= 0, jnp.exp(lg * jnp.maximum(rel, 0.0)), 0.0)
            p = (cs - 1.0 - pos) if d else pos
            qd_ref[c] = jnp.exp(lg * (p + 1.0))
            kd_ref[c] = jnp.exp(lg * (cs - 1.0 - p))
    s_ref[...] = jnp.zeros_like(s_ref)

    for (q_ref, k_ref, v_ref, of_ref, ob_ref, rope) in (
            (qc_ref, kc_ref, vc_ref, ofc_ref, obc_ref, False),
            (qx_ref, kx_ref, vx_ref, ofx_ref, obx_ref, True)):
        n_chunks = q_ref.shape[0] // cs

        def step(i, carry, q_ref=q_ref, k_ref=k_ref, v_ref=v_ref, of_ref=of_ref, ob_ref=ob_ref,
                 rope=rope, n_chunks=n_chunks):
            for d in (0, 1):
                start = pl.multiple_of(((n_chunks - 1 - i) if d else i) * cs, cs)
                o_ref = ob_ref if d else of_ref
                if rope:
                    cos = cos_ref[pl.ds(start, cs), :]
                    sin = sin_ref[pl.ds(start, cs), :]
                for hh in range(hp):
                    c = 2 * hh + d
                    sl = _head_slice(hh)
                    q = q_ref[pl.ds(start, cs), sl].astype(F32)
                    k = k_ref[pl.ds(start, cs), sl].astype(F32) * k_scale
                    if rope:
                        q = q * cos + pltpu.roll(q, HEAD_DIM // 2, 1) * sin
                        k = k * cos + pltpu.roll(k, HEAD_DIM // 2, 1) * sin
                    v = v_ref[pl.ds(start, cs), sl]
                    state = s_ref[c]
                    c_dec = jnp.exp(jnp.full((1, HEAD_DIM), lg_ref[d, head0 + hh] * cs, F32))
                    att = _mm_nt(q, k) * intra_ref[c]
                    o_ref[pl.ds(start, cs), sl] = _mm(att, v) + _mm(q * qd_ref[c], state)
                    s_ref[c] = state * c_dec + _mm_tn(k * kd_ref[c], v)
            return carry

        lax.fori_loop(0, n_chunks, step, 0)

    _finish_scan(ofc_ref, obc_ref, gc_ref, yc_ref, cs)
    _finish_scan(ofx_ref, obx_ref, gx_ref, yx_ref, cs)


def _retention(proj_c, proj_x, log_gamma, cos2, sin2):
    b, tc, _ = proj_c.shape
    tx = proj_x.shape[1]
    h, hp = RET_HEADS, SCAN_HEADS
    w = hp * HEAD_DIM

    def col(t, group):
        return pl.BlockSpec((None, t, w), lambda bi, hi, lg: (bi, 0, group * (h // hp) + hi))

    n_chain = 2 * hp
    grid_spec = pltpu.PrefetchScalarGridSpec(
        num_scalar_prefetch=1,
        grid=(b, h // hp),
        in_specs=[col(tc, 0), col(tc, 1), col(tc, 2), col(tc, 3),
                  col(tx, 0), col(tx, 1), col(tx, 2), col(tx, 3),
                  pl.BlockSpec((tx, HEAD_DIM), lambda bi, hi, lg: (0, 0)),
                  pl.BlockSpec((tx, HEAD_DIM), lambda bi, hi, lg: (0, 0))],
        out_specs=[pl.BlockSpec((None, tc, w), lambda bi, hi, lg: (bi, 0, hi)),
                   pl.BlockSpec((None, tx, w), lambda bi, hi, lg: (bi, 0, hi))],
        scratch_shapes=[pltpu.VMEM((tc, w), F32), pltpu.VMEM((tc, w), F32),
                        pltpu.VMEM((tx, w), F32), pltpu.VMEM((tx, w), F32),
                        pltpu.VMEM((n_chain, HEAD_DIM, HEAD_DIM), F32),
                        pltpu.VMEM((n_chain, RET_CHUNK, RET_CHUNK), F32),
                        pltpu.VMEM((n_chain, RET_CHUNK, HEAD_DIM), F32),
                        pltpu.VMEM((n_chain, RET_CHUNK, HEAD_DIM), F32)])
    return pl.pallas_call(
        _retention_kernel,
        grid_spec=grid_spec,
        out_shape=[jax.ShapeDtypeStruct((b, tc, h * HEAD_DIM), BF16),
                   jax.ShapeDtypeStruct((b, tx, h * HEAD_DIM), BF16)],
        compiler_params=_cparams("parallel", "parallel"),
        name="retention",
    )(log_gamma, proj_c, proj_c, proj_c, proj_c, proj_x, proj_x, proj_x, proj_x, cos2, sin2)


def _hgrn_levels(cs):
    return [cs >> (i + 1) for i in range(int(math.log2(cs)))]


def _hgrn_constants(cs):
    t = np.arange(cs)[:, None]
    s = np.arange(cs)[None, :]
    tri = np.stack([(s <= t), (s >= t)]).astype(np.float32)
    masks = []
    for reverse in (False, True):
        lv = []
        for h in _hgrn_levels(cs):
            same = (t // (2 * h)) == (s // (2 * h))
            t_hi = (t % (2 * h)) >= h
            s_hi = (s % (2 * h)) >= h
            lv.append(same & (~t_hi & s_hi if reverse else t_hi & ~s_hi))
        lv.append(t == s)
        masks.append(np.stack(lv))
    return jnp.asarray(tri, BF16), jnp.asarray(np.stack(masks).astype(np.float32))


def _split3(x):
    hi = x.astype(BF16)
    r = x - hi.astype(F32)
    mid = r.astype(BF16)
    lo = (r - mid.astype(F32)).astype(BF16)
    return hi, mid, lo


def _level_reference(b, h, reverse):
    cs = b.shape[0]
    ref_local = h if reverse else h - 1
    if 2 * h >= 8:
        b3 = b.reshape(cs // (2 * h), 2 * h, b.shape[1])
        r = b3[:, ref_local:ref_local + 1, :]
        return jnp.broadcast_to(r, b3.shape).reshape(b.shape)
    local = lax.broadcasted_iota(jnp.int32, b.shape, 0) % (2 * h)
    out = b
    for m in range(2 * h):
        if m != ref_local:
            out = jnp.where(local == m, pltpu.roll(b, (m - ref_local) % cs, 0), out)
    return out


def _hgrn_kernel(lb_ref, tri_ref, mask_ref,
                 qc_ref, vc_ref, gc_ref, zfc_ref, zbc_ref,
                 qx_ref, vx_ref, gx_ref, zfx_ref, zbx_ref,
                 yc_ref, yx_ref, ofc_ref, obc_ref, ofx_ref, obx_ref, st_ref):
    cs = HGRN_CHUNK
    hp = SCAN_HEADS
    levels = _hgrn_levels(cs)
    st_ref[...] = jnp.zeros_like(st_ref)

    for (q_ref, v_ref, zf_ref, zb_ref, of_ref, ob_ref) in (
            (qc_ref, vc_ref, zfc_ref, zbc_ref, ofc_ref, obc_ref),
            (qx_ref, vx_ref, zfx_ref, zbx_ref, ofx_ref, obx_ref)):
        n_chunks = q_ref.shape[0] // cs

        def step(i, carry, q_ref=q_ref, v_ref=v_ref, zf_ref=zf_ref, zb_ref=zb_ref, of_ref=of_ref,
                 ob_ref=ob_ref, n_chunks=n_chunks):
            for d in (0, 1):
                reverse = d == 1
                start = pl.multiple_of(((n_chunks - 1 - i) if d else i) * cs, cs)
                z_ref = zb_ref if d else zf_ref
                o_ref = ob_ref if d else of_ref
                for hh in range(hp):
                    c = 2 * hh + d
                    sl = _head_slice(hh)
                    lb = lb_ref[hh]
                    q = _silu(q_ref[pl.ds(start, cs), sl].astype(F32))
                    z = z_ref[pl.ds(start, cs), sl]
                    v = v_ref[pl.ds(start, cs), sl]
                    f = lb + (1.0 - lb) * (1.0 / (1.0 + jnp.exp(-z)))
                    k = 1.0 - f
                    logf = jnp.log(f)
                    tri = tri_ref[d]
                    hi, mid, lo = _split3(logf)
                    bsum = (jnp.dot(tri, hi, preferred_element_type=F32)
                            + jnp.dot(tri, mid, preferred_element_type=F32)
                            + jnp.dot(tri, lo, preferred_element_type=F32))
                    att = _mm_nt(q, k) * mask_ref[d, len(levels)]
                    for li, h in enumerate(levels):
                        w = jnp.exp(-jnp.abs(bsum - _level_reference(bsum, h, reverse)))
                        att += _mm_nt(q * w, k * w) * mask_ref[d, li]
                    state_t = st_ref[c]
                    o_ref[pl.ds(start, cs), sl] = _mm(att, v) + _mm_nt(q * jnp.exp(bsum), state_t)
                    b_end = bsum[0:1, :] if reverse else bsum[cs - 1:cs, :]
                    st_ref[c] = state_t * jnp.exp(b_end) + _mm_tn(v, k * jnp.exp(b_end - bsum))
            return carry

        lax.fori_loop(0, n_chunks, step, 0)

    _finish_scan(ofc_ref, obc_ref, gc_ref, yc_ref, cs)
    _finish_scan(ofx_ref, obx_ref, gx_ref, yx_ref, cs)


def _hgrn(proj_c, proj_x, z_c, z_x, lb, first):
    b, tc, _ = proj_c.shape
    tx = proj_x.shape[1]
    h, hp = HGRN_HEADS, SCAN_HEADS
    w = hp * HEAD_DIM
    tri, masks = _hgrn_constants(HGRN_CHUNK)

    def col(t, group):
        return pl.BlockSpec((None, t, w), lambda bi, hi: (bi, 0, (first + group * h) // hp + hi))

    def zcol(t, group):
        return pl.BlockSpec((None, t, w), lambda bi, hi: (bi, 0, group * (h // hp) + hi))

    ins = [col(tc, 0), col(tc, 1), col(tc, 2), zcol(tc, 0), zcol(tc, 1),
           col(tx, 0), col(tx, 1), col(tx, 2), zcol(tx, 0), zcol(tx, 1)]
    return pl.pallas_call(
        _hgrn_kernel,
        grid=(b, h // hp),
        in_specs=[pl.BlockSpec((hp, 1, HEAD_DIM), lambda bi, hi: (hi, 0, 0)),
                  pl.BlockSpec(tri.shape, lambda bi, hi: (0, 0, 0)),
                  pl.BlockSpec(masks.shape, lambda bi, hi: (0, 0, 0, 0))] + ins,
        out_specs=[pl.BlockSpec((None, tc, w), lambda bi, hi: (bi, 0, hi)),
                   pl.BlockSpec((None, tx, w), lambda bi, hi: (bi, 0, hi))],
        out_shape=[jax.ShapeDtypeStruct((b, tc, h * HEAD_DIM), BF16),
                   jax.ShapeDtypeStruct((b, tx, h * HEAD_DIM), BF16)],
        scratch_shapes=[pltpu.VMEM((tc, w), F32), pltpu.VMEM((tc, w), F32),
                        pltpu.VMEM((tx, w), F32), pltpu.VMEM((tx, w), F32),
                        pltpu.VMEM((2 * hp, HEAD_DIM, HEAD_DIM), F32)],
        compiler_params=_cparams("parallel", "parallel"),
        name="hgrn2",
    )(lb.reshape(h, 1, HEAD_DIM), tri, masks, proj_c, proj_c, proj_c, z_c, z_c,
      proj_x, proj_x, proj_x, z_x, z_x)


MLA_QW = 2 * LANE
MLA_DOWN_PAD = MLA_Q_RANK + MLA_KV_RANK + LANE


def _mla_proj_kernel(x_ref, mod_ref, g_ref, wd_ref, gq_ref, gkv_ref, wq_ref, wkv_ref, cos_ref, sin_ref,
                     *out_refs, rope, want_q):
    h = _modnorm(x_ref[...], g_ref[...], mod_ref, 0, 1).astype(BF16)
    down = jnp.dot(h, wd_ref[...], preferred_element_type=F32)
    if rope:
        cos = cos_ref[...]
        sin = sin_ref[...]

    def rot(v):
        return v * cos + pltpu.roll(v, LANE // 2, 1) * sin if rope else v

    if want_q:
        q_ref, kn_ref, kr_ref, v_ref = out_refs
        qn = (_rms(down[:, :MLA_Q_RANK]) * gq_ref[...]).astype(BF16)
        q = jnp.dot(qn, wq_ref[...], preferred_element_type=F32)
        for hd in range(MLA_HEADS):
            lo = hd * MLA_QW
            q_ref[:, lo:lo + LANE] = q[:, lo:lo + LANE].astype(q_ref.dtype)
            q_ref[:, lo + LANE:lo + 2 * LANE] = rot(q[:, lo + LANE:lo + 2 * LANE]).astype(q_ref.dtype)
    else:
        kn_ref, kr_ref, v_ref = out_refs
    kvn = (_rms(down[:, MLA_Q_RANK:MLA_Q_RANK + MLA_KV_RANK]) * gkv_ref[...]).astype(BF16)
    kv = jnp.dot(kvn, wkv_ref[...], preferred_element_type=F32)
    nk = MLA_HEADS * MLA_NOPE
    kn_ref[...] = kv[:, :nk].astype(kn_ref.dtype)
    v_ref[...] = kv[:, nk:].astype(v_ref.dtype)
    kr_ref[...] = rot(down[:, MLA_Q_RANK + MLA_KV_RANK:]).astype(kr_ref.dtype)


def _mla_proj(x, mod, g, wd, gq, gkv, wq, wkv, cos, sin, rope, want_q, tm=512):
    b, t, d = x.shape
    tm = _pick(t, tm)
    const = lambda bi, i: (0, 0)
    tok = lambda bi, i: (bi, i, 0)
    nk, nv = MLA_HEADS * MLA_NOPE, MLA_HEADS * MLA_V
    out_shape = [jax.ShapeDtypeStruct((b, t, nk), BF16), jax.ShapeDtypeStruct((b, t, LANE), BF16),
                 jax.ShapeDtypeStruct((b, t, nv), BF16)]
    out_specs = [pl.BlockSpec((None, tm, nk), tok), pl.BlockSpec((None, tm, LANE), tok),
                 pl.BlockSpec((None, tm, nv), tok)]
    if want_q:
        out_shape = [jax.ShapeDtypeStruct((b, t, MLA_HEADS * MLA_QW), BF16)] + out_shape
        out_specs = [pl.BlockSpec((None, tm, MLA_HEADS * MLA_QW), tok)] + out_specs
    return pl.pallas_call(
        functools.partial(_mla_proj_kernel, rope=rope, want_q=want_q),
        grid=(b, t // tm),
        in_specs=[pl.BlockSpec((None, tm, d), tok), _mod_spec(mod), pl.BlockSpec((1, d), const),
                  pl.BlockSpec(wd.shape, const), pl.BlockSpec((1, MLA_Q_RANK), const),
                  pl.BlockSpec((1, MLA_KV_RANK), const), pl.BlockSpec(wq.shape, const),
                  pl.BlockSpec(wkv.shape, const),
                  pl.BlockSpec((tm, LANE), lambda bi, i: (i, 0)),
                  pl.BlockSpec((tm, LANE), lambda bi, i: (i, 0))],
        out_specs=out_specs,
        out_shape=out_shape,
        compiler_params=_cparams("parallel", "parallel"),
        name="mla_proj",
    )(x, mod, g.reshape(1, d), wd, gq.reshape(1, -1), gkv.reshape(1, -1), wq, wkv, cos, sin)


def _lane_fold(x, op):
    out = x[:, :LANE]
    for j in range(1, x.shape[1] // LANE):
        out = op(out, x[:, j * LANE:(j + 1) * LANE])
    return out


def _attention_kernel(q_ref, knc_ref, krc_ref, vc_ref, knx_ref, krx_ref, vx_ref, o_ref,
                      kc_ref, kx_ref, sc0_ref, sc1_ref, sx0_ref, sx1_ref, m0_ref, m1_ref, *, tk, exp_scale):
    step = pl.program_id(2)

    @pl.when(step == 0)
    def _():
        kc_ref[:, :LANE] = knc_ref[...]
        kc_ref[:, LANE:] = krc_ref[...]
        kx_ref[:, :LANE] = knx_ref[...]
        kx_ref[:, LANE:] = krx_ref[...]
        sc1_ref[...] = jnp.zeros(sc1_ref.shape, F32)
        sx1_ref[...] = jnp.zeros(sx1_ref.shape, F32)
        m1_ref[...] = jnp.zeros(m1_ref.shape, F32)

    n_blocks = kx_ref.shape[0] // tk
    nt = (((1,), (1,)), ((), ()))

    def body(sc_cur, sx_cur, m_cur, sc_prev, sx_prev, m_prev):
        q = q_ref[...]
        m = jnp.max(m_prev[...], axis=-1, keepdims=True)
        s = lax.dot_general(q, kc_ref[...], nt, preferred_element_type=F32)
        sc_cur[...] = s
        m_part = _lane_fold(s, jnp.maximum)
        p = jnp.exp2((sc_prev[...] - m) * exp_scale)
        l_part = _lane_fold(p, jnp.add)
        acc = jnp.dot(p.astype(BF16), vc_ref[...], preferred_element_type=F32)
        for i in range(n_blocks):
            s = lax.dot_general(q, kx_ref[i * tk:(i + 1) * tk, :], nt, preferred_element_type=F32)
            sx_cur[i] = s
            m_part = jnp.maximum(m_part, _lane_fold(s, jnp.maximum))
            p = jnp.exp2((sx_prev[i] - m) * exp_scale)
            l_part = l_part + _lane_fold(p, jnp.add)
            acc = acc + jnp.dot(p.astype(BF16), vx_ref[i * tk:(i + 1) * tk, :],
                                preferred_element_type=F32)
        m_cur[...] = m_part
        l = jnp.sum(l_part, axis=-1, keepdims=True)
        o_ref[...] = (acc / l).astype(o_ref.dtype)

    @pl.when(step % 2 == 0)
    def _():
        body(sc0_ref, sx0_ref, m0_ref, sc1_ref, sx1_ref, m1_ref)

    @pl.when(step % 2 == 1)
    def _():
        body(sc1_ref, sx1_ref, m1_ref, sc0_ref, sx0_ref, m0_ref)


def _attention(q, kn_c, kr_c, v_c, kn_x, kr_x, v_x, scale, tq=256, tk=512):
    b, t, _ = q.shape
    tc = kn_c.shape[1]
    tq = _pick(t, tq)
    tk = _pick(t, tk)
    nq = t // tq
    head_c = lambda bi, hi, i: (bi, 0, hi)
    shared = lambda bi, hi, i: (bi, 0, 0)
    return pl.pallas_call(
        functools.partial(_attention_kernel, tk=tk, exp_scale=scale * math.log2(math.e)),
        grid=(b, MLA_HEADS, nq + 1),
        in_specs=[pl.BlockSpec((None, tq, MLA_QW), lambda bi, hi, i: (bi, jnp.minimum(i, nq - 1), hi)),
                  pl.BlockSpec((None, tc, LANE), head_c), pl.BlockSpec((None, tc, LANE), shared),
                  pl.BlockSpec((None, tc, LANE), head_c),
                  pl.BlockSpec((None, t, LANE), head_c), pl.BlockSpec((None, t, LANE), shared),
                  pl.BlockSpec((None, t, LANE), head_c)],
        out_specs=pl.BlockSpec((None, tq, MLA_V), lambda bi, hi, i: (bi, jnp.maximum(i - 1, 0), hi)),
        out_shape=jax.ShapeDtypeStruct((b, t, MLA_HEADS * MLA_V), BF16),
        scratch_shapes=[pltpu.VMEM((tc, MLA_QW), BF16), pltpu.VMEM((t, MLA_QW), BF16),
                        pltpu.VMEM((tq, tc), F32), pltpu.VMEM((tq, tc), F32),
                        pltpu.VMEM((t // tk, tq, tk), F32), pltpu.VMEM((t // tk, tq, tk), F32),
                        pltpu.VMEM((tq, LANE), F32), pltpu.VMEM((tq, LANE), F32)],
        compiler_params=_cparams("parallel", "parallel", "arbitrary"),
        name="mla_attention",
    )(q, kn_c, kr_c, v_c, kn_x, kr_x, v_x)


def _router_kernel(x_ref, mod_ref, g_ref, wr_ref, h_ref, w_ref, e_ref):
    h = _modnorm(x_ref[...], g_ref[...], mod_ref, 3, 4)
    h_ref[...] = _pack_halves(h)
    logits = jnp.dot(h, wr_ref[...], preferred_element_type=F32, precision=lax.Precision.HIGHEST)
    lane = lax.broadcasted_iota(jnp.int32, logits.shape, 1)
    neg = jnp.float32(-jnp.inf)
    lg = jnp.where(lane < N_EXPERTS, logits, neg)
    m1 = jnp.max(lg, axis=-1, keepdims=True)
    i1 = jnp.min(jnp.where(lg == m1, lane, LANE), axis=-1, keepdims=True)
    lg2 = jnp.where(lane == i1, neg, lg)
    m2 = jnp.max(lg2, axis=-1, keepdims=True)
    i2 = jnp.min(jnp.where(lg2 == m2, lane, LANE), axis=-1, keepdims=True)
    e2 = jnp.exp(m2 - m1)
    w1 = 1.0 / (1.0 + e2)
    w2 = e2 / (1.0 + e2)
    w_ref[...] = jnp.where(lane == 0, w1, jnp.where(lane == 1, w2, 0.0))
    e_ref[...] = jnp.where(lane == 0, i1, jnp.where(lane == 1, i2, 0))


def _router(x, mod, g, router_pad, tm=512):
    b, t, d = x.shape
    tm = _pick(t, tm)
    tok = lambda bi, i: (bi, i, 0)
    const = lambda bi, i: (0, 0)
    return pl.pallas_call(
        _router_kernel,
        grid=(b, t // tm),
        in_specs=[pl.BlockSpec((None, tm, d), tok), _mod_spec(mod), pl.BlockSpec((1, d), const),
                  pl.BlockSpec((d, LANE), const)],
        out_specs=[pl.BlockSpec((None, tm, d // 2), tok), pl.BlockSpec((None, tm, LANE), tok),
                   pl.BlockSpec((None, tm, LANE), tok)],
        out_shape=[jax.ShapeDtypeStruct((b, t, d // 2), jnp.int32), jax.ShapeDtypeStruct((b, t, LANE), F32),
                   jax.ShapeDtypeStruct((b, t, LANE), jnp.int32)],
        compiler_params=_cparams("parallel", "parallel"),
        name="moe_router",
    )(x, mod, g.reshape(1, d), router_pad)


SC_GATHER_ROWS = 64


def _gather_rows(table, idx):
    info = plsc.get_sparse_core_info()
    n_workers = info.num_cores * info.num_subcores
    n, width = idx.shape[0], table.shape[1]
    chunk = SC_GATHER_ROWS
    assert n % (n_workers * chunk) == 0
    n_chunks = n // (n_workers * chunk)
    mesh = plsc.VectorSubcoreMesh(core_axis_name="c", subcore_axis_name="s")

    def body(table_hbm, idx_hbm, out_hbm, idx_v, rows_v, sem):
        wid = lax.axis_index("s") * info.num_cores + lax.axis_index("c")
        pltpu.sync_copy(idx_hbm.at[wid], idx_v)

        @pl.loop(0, n_chunks)
        def _(j):
            pltpu.async_copy(table_hbm.at[idx_v.at[j]], rows_v, sem).wait()
            pltpu.sync_copy(rows_v, out_hbm.at[pl.ds((wid * n_chunks + j) * chunk, chunk)])

    return pl.kernel(
        body,
        out_type=jax.ShapeDtypeStruct((n, width), table.dtype),
        mesh=mesh,
        scratch_types=[pltpu.VMEM((n_chunks, chunk), jnp.int32), pltpu.VMEM((chunk, width), table.dtype),
                       pltpu.SemaphoreType.DMA],
        name="sc_gather_rows",
    )(table, idx.reshape(n_workers, n_chunks, chunk))


def _pack_halves(x):
    k = x.shape[1] // 2
    bits = lambda v: lax.bitcast_convert_type(v.astype(BF16).astype(F32), jnp.uint32)
    word = lax.shift_right_logical(bits(x[:, :k]), jnp.uint32(16)) | bits(x[:, k:])
    return lax.bitcast_convert_type(word, jnp.int32)


def _unpack_halves(w):
    u = lax.bitcast_convert_type(w, jnp.uint32)
    lo = lax.bitcast_convert_type(lax.shift_left(u, jnp.uint32(16)), F32)
    hi = lax.bitcast_convert_type(u & jnp.uint32(0xFFFF0000), F32)
    return jnp.concatenate([lo, hi], axis=-1)


def _expert_ffn_kernel(te_ref, tv_ref, x_ref, wrow_ref, wa_ref, wb_ref, wo_ref, o_ref, acc_ref):
    j = pl.program_id(0)
    f = pl.program_id(1)
    last = pl.num_programs(1) - 1
    valid = tv_ref[j] != 0

    @pl.when(valid)
    def _():
        @pl.when(f == 0)
        def _():
            acc_ref[...] = jnp.zeros_like(acc_ref)

        h = _unpack_halves(x_ref[...]).astype(BF16)
        a = jnp.dot(h, wa_ref[...], preferred_element_type=F32)
        bb = jnp.dot(h, wb_ref[...], preferred_element_type=F32)
        acc_ref[...] += jnp.dot((_silu(a) * bb).astype(BF16), wo_ref[...], preferred_element_type=F32)

        @pl.when(f == last)
        def _():
            o_ref[...] = _pack_halves(acc_ref[...] * wrow_ref[...])

    @pl.when(jnp.logical_not(valid) & (f == last))
    def _():
        o_ref[...] = jnp.zeros_like(o_ref)


def _expert_ffn(xs, w_rows, tile_expert, tile_valid, w_in, w_out, tm, tf=1408):
    p = xs.shape[0]
    n_exp, fdim, d = w_out.shape
    tf = _pick(fdim, tf)
    nf = fdim // tf

    def f_eff(j, f, tv):
        return jnp.where(tv[j] != 0, f, nf - 1)

    grid_spec = pltpu.PrefetchScalarGridSpec(
        num_scalar_prefetch=2,
        grid=(p // tm, nf),
        in_specs=[pl.BlockSpec((tm, d // 2), lambda j, f, te, tv: (j, 0)),
                  pl.BlockSpec((tm, 1), lambda j, f, te, tv: (j, 0)),
                  pl.BlockSpec((None, d, tf), lambda j, f, te, tv: (te[j], 0, f_eff(j, f, tv))),
                  pl.BlockSpec((None, d, tf), lambda j, f, te, tv: (te[j], 0, f_eff(j, f, tv) + nf)),
                  pl.BlockSpec((None, tf, d), lambda j, f, te, tv: (te[j], f_eff(j, f, tv), 0))],
        out_specs=pl.BlockSpec((tm, d // 2), lambda j, f, te, tv: (j, 0)),
        scratch_shapes=[pltpu.VMEM((tm, d), F32)])
    return pl.pallas_call(
        _expert_ffn_kernel,
        grid_spec=grid_spec,
        out_shape=jax.ShapeDtypeStruct((p, d // 2), jnp.int32),
        compiler_params=_cparams("parallel", "arbitrary"),
        name="moe_expert_ffn",
    )(tile_expert, tile_valid, xs, w_rows, w_in, w_in, w_out)


def _moe_combine_kernel(x_ref, y_ref, mod_ref, g_ref, o_ref):
    k = x_ref.shape[-1] // 2
    y = _unpack_halves(y_ref[:, :k]) + _unpack_halves(y_ref[:, k:])
    o_ref[...] = x_ref[...] + mod_ref[5:6, :] * (_rms(y) * g_ref[...])


def _moe_combine(x, y2, mod, g, tm=512):
    b, t, d = x.shape
    tm = _pick(t, tm)
    tok = lambda bi, i: (bi, i, 0)
    return pl.pallas_call(
        _moe_combine_kernel,
        grid=(b, t // tm),
        in_specs=[pl.BlockSpec((None, tm, d), tok), pl.BlockSpec((None, tm, d), tok), _mod_spec(mod),
                  pl.BlockSpec((1, d), lambda bi, i: (0, 0))],
        out_specs=pl.BlockSpec((None, tm, d), tok),
        out_shape=jax.ShapeDtypeStruct((b, t, d), F32),
        compiler_params=_cparams("parallel", "parallel"),
        name="moe_combine",
    )(x, y2, mod, g.reshape(1, d))


def _moe(x, mod, g_in, g_out, router_pad, w_in, w_out, tm=512):
    b, t, d = x.shape
    n_exp = w_out.shape[0]
    n = b * t
    h, w, e = _router(x, mod, g_in, router_pad)
    e_flat = e[..., :2].reshape(2 * n)
    w_flat = w[..., :2].reshape(2 * n)
    onehot = (e_flat[:, None] == jnp.arange(n_exp, dtype=jnp.int32)).astype(jnp.int32)
    csum = jnp.cumsum(onehot, axis=0)
    rank = jnp.sum((csum - onehot) * onehot, axis=-1)
    counts = csum[-1]
    ends = jnp.cumsum((counts + tm - 1) // tm * tm)
    dest = (ends - (counts + tm - 1) // tm * tm)[e_flat] + rank
    p = 2 * n + n_exp * tm
    src = jnp.zeros((p,), jnp.int32).at[dest].set(jnp.arange(2 * n, dtype=jnp.int32) // 2)
    w_rows = jnp.zeros((p,), F32).at[dest].set(w_flat)
    tile_start = jnp.arange(p // tm, dtype=jnp.int32) * tm
    tile_expert = jnp.minimum(jnp.searchsorted(ends, tile_start, side="right"), n_exp - 1).astype(jnp.int32)
    tile_valid = (tile_start < ends[-1]).astype(jnp.int32)

    xs = _gather_rows(h.reshape(n, d // 2), src)
    ys = _expert_ffn(xs, w_rows.reshape(p, 1), tile_expert, tile_valid, w_in, w_out, tm)
    y2 = _gather_rows(ys, dest)
    return _moe_combine(x, y2.reshape(b, t, d), mod, g_out)


def _axial_angles(n_tokens, dim):
    t = jnp.arange(n_tokens)
    rows = (t // GRID_W).astype(F32)
    cols = (t % GRID_W).astype(F32)
    n_freq = dim // 4
    inv = ROPE_BASE ** (-jnp.arange(n_freq, dtype=F32) / n_freq)
    return jnp.concatenate([rows[:, None] * inv, cols[:, None] * inv], axis=-1)


def _rope_tables(n_tokens, dim):
    ang = _axial_angles(n_tokens, dim)
    cos, sin = jnp.cos(ang), jnp.sin(ang)
    if dim == LANE:
        return jnp.concatenate([cos, cos], -1), jnp.concatenate([-sin, sin], -1)
    z = jnp.zeros_like(cos)
    return jnp.concatenate([cos, z, cos, z], -1), jnp.concatenate([-sin, z, sin, z], -1)


def _spread_rope_cols(w):
    half = MLA_ROPE // 2
    z = jnp.zeros(w.shape[:-1] + (half,), w.dtype)
    return jnp.concatenate([w[..., :half], z, w[..., half:], z], axis=-1)


def kernel(x, c, ctx, c_ctx, mod_w, mod_b, norm_g, mix_in_w, ret_decay_logit, hgrn_lb_logit, mix_out_w,
           ffn_in_w, ffn_out_w, mla_down_w, mla_q_norm_g, mla_kv_norm_g, mla_uq_w, mla_ukv_w, mla_out_w,
           router_w, moe_in_w, moe_out_w):
    batch, seq, d = x.shape
    assert mod_w.shape[0] == 2, "two layers: retention/HGRN2 then MLA/MoE"

    rows = -(-(batch + 1) // 8) * 8
    cond = jnp.zeros((rows, d), F32).at[:batch].set(c).at[batch].set(c_ctx)
    mod = _modulation(cond, mod_w, mod_b)
    mod_x = [mod[l, :batch].reshape(batch, 6, d) for l in range(2)]
    mod_c = [mod[l, batch:batch + 1].reshape(1, 6, d) for l in range(2)]

    g = norm_g[0]
    ret_w = RET_HEADS * HEAD_DIM
    z_lo, z_hi = 5 * ret_w, 7 * ret_w
    w_in = mix_in_w[0].astype(BF16)
    w_main = jnp.concatenate([w_in[:, :z_lo], w_in[:, z_hi:]], axis=1)
    w_z = w_in[:, z_lo:z_hi]
    proj_x = _norm_matmul(x, mod_x[0], g[0], w_main, 0, 1, BF16)
    proj_c = _norm_matmul(ctx, mod_c[0], g[0], w_main, 0, 1, BF16)
    z_x = _norm_matmul(x, mod_x[0], g[0], w_z, 0, 1, F32)
    z_c = _norm_matmul(ctx, mod_c[0], g[0], w_z, 0, 1, F32)
    cos2, sin2 = _rope_tables(seq, HEAD_DIM)
    log_gamma = jax.nn.log_sigmoid(ret_decay_logit[0].astype(F32))
    lb = jnp.cumsum(jax.nn.softmax(hgrn_lb_logit.astype(F32), axis=0), axis=0)[0]
    ret_c, ret_x = _retention(proj_c, proj_x, log_gamma, cos2, sin2)
    hg_c, hg_x = _hgrn(proj_c, proj_x, z_c, z_x, lb, first=4 * RET_HEADS)
    w_out = mix_out_w[0].astype(BF16)
    w_outs = [w_out[:ret_w], w_out[ret_w:]]
    x = _proj_residual([ret_x, hg_x], w_outs, x, mod_x[0], g[1], 2)
    ctx = _proj_residual([ret_c, hg_c], w_outs, ctx, mod_c[0], g[1], 2)
    f_in, f_out = ffn_in_w[0].astype(BF16), ffn_out_w[0].astype(BF16)
    x = _ffn(x, mod_x[0], g[2], g[3], f_in, f_out)
    ctx = _ffn(ctx, mod_c[0], g[2], g[3], f_in, f_out)

    g = norm_g[1]
    dq = MLA_NOPE + MLA_ROPE
    wd = mla_down_w[0]
    wd = jnp.concatenate([wd[:, :MLA_Q_RANK + MLA_KV_RANK],
                          _spread_rope_cols(wd[:, MLA_Q_RANK + MLA_KV_RANK:])], axis=-1).astype(BF16)
    wq = mla_uq_w[0].reshape(MLA_Q_RANK, MLA_HEADS, dq)
    wq = jnp.concatenate([wq[..., :MLA_NOPE], _spread_rope_cols(wq[..., MLA_NOPE:])], axis=-1)
    wq = wq.reshape(MLA_Q_RANK, MLA_HEADS * MLA_QW).astype(BF16)
    wkv = mla_ukv_w[0].reshape(MLA_KV_RANK, MLA_HEADS, MLA_NOPE + MLA_V)
    wkv = jnp.concatenate([wkv[..., :MLA_NOPE].reshape(MLA_KV_RANK, -1),
                           wkv[..., MLA_NOPE:].reshape(MLA_KV_RANK, -1)], axis=-1).astype(BF16)
    cos_m, sin_m = _rope_tables(seq, MLA_ROPE)
    q, kn_x, kr_x, v_x = _mla_proj(x, mod_x[1], g[0], wd, mla_q_norm_g[0], mla_kv_norm_g[0], wq, wkv,
                                   cos_m, sin_m, rope=True, want_q=True)
    tc = ctx.shape[1]
    kn_c, kr_c, v_c = _mla_proj(ctx, mod_c[1], g[0], wd, mla_q_norm_g[0], mla_kv_norm_g[0], wq, wkv,
                                cos_m[:tc], sin_m[:tc], rope=False, want_q=False)
    att = _attention(q, kn_c, kr_c, v_c, kn_x, kr_x, v_x, dq ** -0.5)
    x = _proj_residual([att], [mla_out_w[0].astype(BF16)], x, mod_x[1], g[1], 2)
    router_pad = jnp.zeros((d, LANE), F32).at[:, :N_EXPERTS].set(router_w[0])
    x = _moe(x, mod_x[1], g[2], g[3], router_pad, moe_in_w[0].astype(BF16), moe_out_w[0].astype(BF16))
    return x
```

```python
import functools
import math

import jax
import jax.numpy as jnp
import numpy as np
from jax import lax
from jax.experimental import pallas as pl
from jax.experimental.pallas import tpu as pltpu
from jax.experimental.pallas import tpu_sc as plsc

F32 = jnp.float32
BF16 = jnp.bfloat16

NORM_EPS = 1e-6
ROPE_BASE = 10000.0
GRID_W = 64
LANE = 128

RET_HEADS = 4
HGRN_HEADS = 4
HEAD_DIM = 128
RET_CHUNK = 128
HGRN_CHUNK = 128

MLA_HEADS = 8
MLA_NOPE = 128
MLA_ROPE = 64
MLA_V = 128
MLA_Q_RANK = 384
MLA_KV_RANK = 256
N_EXPERTS = 8

VMEM_LIMIT = 56 * 1024 * 1024


def _cparams(*sem):
    return pltpu.CompilerParams(dimension_semantics=sem, vmem_limit_bytes=VMEM_LIMIT)


def _mm(a, b):
    return jnp.dot(a.astype(BF16), b.astype(BF16), preferred_element_type=F32)


def _mm_nt(a, b):
    return lax.dot_general(a.astype(BF16), b.astype(BF16), (((1,), (1,)), ((), ())),
                           preferred_element_type=F32)


def _mm_tn(a, b):
    return lax.dot_general(a.astype(BF16), b.astype(BF16), (((0,), (0,)), ((), ())),
                           preferred_element_type=F32)


def _rms(x):
    return x * lax.rsqrt(jnp.mean(x * x, axis=-1, keepdims=True) + NORM_EPS)


def _silu(x):
    return x * (1.0 / (1.0 + jnp.exp(-x)))


def _modnorm(x, g, mod_ref, shift_row, scale_row):
    shift = mod_ref[shift_row:shift_row + 1, :]
    scale = mod_ref[scale_row:scale_row + 1, :]
    return _rms(x) * g * (1.0 + scale) + shift


def _mod_spec(mod):
    d = mod.shape[-1]
    if mod.shape[0] == 1:
        return pl.BlockSpec((None, 6, d), lambda b, *_: (0, 0, 0))
    return pl.BlockSpec((None, 6, d), lambda b, *_: (b, 0, 0))


def _pick(n, pref):
    t = min(pref, n)
    while n % t:
        t -= LANE
    return t


def _mod_kernel(c_ref, w_ref, b_ref, o_ref):
    a = _silu(c_ref[...])
    o_ref[...] = jnp.dot(a, w_ref[...], preferred_element_type=F32,
                         precision=lax.Precision.HIGHEST) + b_ref[...]


def _modulation(cond, mod_w, mod_b):
    n_layers, d, n = mod_w.shape
    r = cond.shape[0]
    tn = _pick(n, 1536)
    return pl.pallas_call(
        _mod_kernel,
        grid=(n_layers, n // tn),
        in_specs=[pl.BlockSpec((r, d), lambda l, j: (0, 0)),
                  pl.BlockSpec((None, d, tn), lambda l, j: (l, 0, j)),
                  pl.BlockSpec((None, 1, tn), lambda l, j: (l, 0, j))],
        out_specs=pl.BlockSpec((None, r, tn), lambda l, j: (l, 0, j)),
        out_shape=jax.ShapeDtypeStruct((n_layers, r, n), F32),
        compiler_params=_cparams("parallel", "parallel"),
        name="modulation",
    )(cond, mod_w, mod_b.reshape(n_layers, 1, n))


def _norm_matmul_kernel(x_ref, mod_ref, g_ref, w_ref, o_ref, h_ref, *, shift_row, scale_row):
    @pl.when(pl.program_id(2) == 0)
    def _():
        h_ref[...] = _modnorm(x_ref[...], g_ref[...], mod_ref, shift_row, scale_row).astype(BF16)

    o_ref[...] = jnp.dot(h_ref[...], w_ref[...], preferred_element_type=F32).astype(o_ref.dtype)


def _norm_matmul(x, mod, g, w, shift_row, scale_row, out_dtype, tm=1024, tn=1792):
    b, t, d = x.shape
    n = w.shape[1]
    tm = _pick(t, tm)
    tn = _pick(n, tn)
    return pl.pallas_call(
        functools.partial(_norm_matmul_kernel, shift_row=shift_row, scale_row=scale_row),
        grid=(b, t // tm, n // tn),
        in_specs=[pl.BlockSpec((None, tm, d), lambda bi, i, j: (bi, i, 0)),
                  _mod_spec(mod),
                  pl.BlockSpec((1, d), lambda bi, i, j: (0, 0)),
                  pl.BlockSpec((d, tn), lambda bi, i, j: (0, j))],
        out_specs=pl.BlockSpec((None, tm, tn), lambda bi, i, j: (bi, i, j)),
        out_shape=jax.ShapeDtypeStruct((b, t, n), out_dtype),
        scratch_shapes=[pltpu.VMEM((tm, d), BF16)],
        compiler_params=_cparams("parallel", "parallel", "arbitrary"),
        name="norm_matmul",
    )(x, mod, g.reshape(1, d), w)


def _proj_residual_kernel(*refs, n_lhs, gate_row):
    lhs = refs[:n_lhs]
    ws = refs[n_lhs:2 * n_lhs]
    x_ref, mod_ref, g_ref, o_ref = refs[2 * n_lhs:]
    y = jnp.dot(lhs[0][...], ws[0][...], preferred_element_type=F32)
    for a_ref, w_ref in zip(lhs[1:], ws[1:]):
        y += jnp.dot(a_ref[...], w_ref[...], preferred_element_type=F32)
    gate = mod_ref[gate_row:gate_row + 1, :]
    o_ref[...] = x_ref[...] + gate * (_rms(y) * g_ref[...])


def _proj_residual(lhs, ws, x, mod, g, gate_row, tm=512):
    b, t, d = x.shape
    tm = _pick(t, tm)
    n_lhs = len(lhs)
    in_specs = [pl.BlockSpec((None, tm, a.shape[-1]), lambda bi, i: (bi, i, 0)) for a in lhs]
    in_specs += [pl.BlockSpec(w.shape, lambda bi, i: (0, 0)) for w in ws]
    in_specs += [pl.BlockSpec((None, tm, d), lambda bi, i: (bi, i, 0)),
                 _mod_spec(mod),
                 pl.BlockSpec((1, d), lambda bi, i: (0, 0))]
    return pl.pallas_call(
        functools.partial(_proj_residual_kernel, n_lhs=n_lhs, gate_row=gate_row),
        grid=(b, t // tm),
        in_specs=in_specs,
        out_specs=pl.BlockSpec((None, tm, d), lambda bi, i: (bi, i, 0)),
        out_shape=jax.ShapeDtypeStruct((b, t, d), F32),
        compiler_params=_cparams("parallel", "parallel"),
        name="proj_residual",
    )(*lhs, *ws, x, mod, g.reshape(1, d))


def _ffn_kernel(x_ref, mod_ref, g_in_ref, g_out_ref, wa_ref, wb_ref, wo_ref, o_ref, h_ref, acc_ref):
    f = pl.program_id(2)

    @pl.when(f == 0)
    def _():
        h_ref[...] = _modnorm(x_ref[...], g_in_ref[...], mod_ref, 3, 4).astype(BF16)
        acc_ref[...] = jnp.zeros_like(acc_ref)

    h = h_ref[...]
    a = jnp.dot(h, wa_ref[...], preferred_element_type=F32)
    bb = jnp.dot(h, wb_ref[...], preferred_element_type=F32)
    acc_ref[...] += jnp.dot((_silu(a) * bb).astype(BF16), wo_ref[...], preferred_element_type=F32)

    @pl.when(f == pl.num_programs(2) - 1)
    def _():
        gate = mod_ref[5:6, :]
        o_ref[...] = x_ref[...] + gate * (_rms(acc_ref[...]) * g_out_ref[...])


def _ffn(x, mod, g_in, g_out, w_in, w_out, tm=512, tf=1408):
    b, t, d = x.shape
    fdim = w_out.shape[0]
    tm = _pick(t, tm)
    tf = _pick(fdim, tf)
    nf = fdim // tf
    return pl.pallas_call(
        _ffn_kernel,
        grid=(b, t // tm, nf),
        in_specs=[pl.BlockSpec((None, tm, d), lambda bi, i, f: (bi, i, 0)),
                  _mod_spec(mod),
                  pl.BlockSpec((1, d), lambda bi, i, f: (0, 0)),
                  pl.BlockSpec((1, d), lambda bi, i, f: (0, 0)),
                  pl.BlockSpec((d, tf), lambda bi, i, f: (0, f)),
                  pl.BlockSpec((d, tf), lambda bi, i, f: (0, f + nf)),
                  pl.BlockSpec((tf, d), lambda bi, i, f: (f, 0))],
        out_specs=pl.BlockSpec((None, tm, d), lambda bi, i, f: (bi, i, 0)),
        out_shape=jax.ShapeDtypeStruct((b, t, d), F32),
        scratch_shapes=[pltpu.VMEM((tm, d), BF16), pltpu.VMEM((tm, d), F32)],
        compiler_params=_cparams("parallel", "parallel", "arbitrary"),
        name="ffn",
    )(x, mod, g_in.reshape(1, d), g_out.reshape(1, d), w_in, w_in, w_out)


SCAN_HEADS = 2


def _head_slice(hh):
    return slice(hh * HEAD_DIM, (hh + 1) * HEAD_DIM)


def _finish_scan(of_ref, ob_ref, g_ref, y_ref, cs):
    def step(i, carry):
        start = pl.multiple_of(i * cs, cs)
        for hh in range(y_ref.shape[1] // HEAD_DIM):
            sl = _head_slice(hh)
            o = of_ref[pl.ds(start, cs), sl] + ob_ref[pl.ds(start, cs), sl]
            gate = g_ref[pl.ds(start, cs), sl].astype(F32)
            y_ref[pl.ds(start, cs), sl] = (_rms(o) * _silu(gate)).astype(y_ref.dtype)
        return carry

    lax.fori_loop(0, y_ref.shape[0] // cs, step, 0)


def _retention_kernel(lg_ref, qc_ref, kc_ref, vc_ref, gc_ref, qx_ref, kx_ref, vx_ref, gx_ref,
                      cos_ref, sin_ref, yc_ref, yx_ref,
                      ofc_ref, obc_ref, ofx_ref, obx_ref, s_ref, intra_ref, qd_ref, kd_ref):
    cs = RET_CHUNK
    hp = SCAN_HEADS
    head0 = pl.program_id(1) * hp
    row = lax.broadcasted_iota(jnp.int32, (cs, cs), 0).astype(F32)
    col = lax.broadcasted_iota(jnp.int32, (cs, cs), 1).astype(F32)
    pos = lax.broadcasted_iota(jnp.int32, (cs, HEAD_DIM), 0).astype(F32)
    k_scale = HEAD_DIM ** -0.5

    for hh in range(hp):
        for d in (0, 1):
            c = 2 * hh + d
            lg = lg_ref[d, head0 + hh]
            rel = (col - row) if d else (row - col)
            intra_ref[c] = jnp.where(rel >= 0, jnp.exp(lg * jnp.maximum(rel, 0.0)), 0.0)
            p = (cs - 1.0 - pos) if d else pos
            qd_ref[c] = jnp.exp(lg * (p + 1.0))
            kd_ref[c] = jnp.exp(lg * (cs - 1.0 - p))
    s_ref[...] = jnp.zeros_like(s_ref)

    for (q_ref, k_ref, v_ref, of_ref, ob_ref, rope) in (
            (qc_ref, kc_ref, vc_ref, ofc_ref, obc_ref, False),
            (qx_ref, kx_ref, vx_ref, ofx_ref, obx_ref, True)):
        n_chunks = q_ref.shape[0] // cs

        def step(i, carry, q_ref=q_ref, k_ref=k_ref, v_ref=v_ref, of_ref=of_ref, ob_ref=ob_ref,
                 rope=rope, n_chunks=n_chunks):
            for d in (0, 1):
                start = pl.multiple_of(((n_chunks - 1 - i) if d else i) * cs, cs)
                o_ref = ob_ref if d else of_ref
                if rope:
                    cos = cos_ref[pl.ds(start, cs), :]
                    sin = sin_ref[pl.ds(start, cs), :]
                for hh in range(hp):
                    c = 2 * hh + d
                    sl = _head_slice(hh)
                    q = q_ref[pl.ds(start, cs), sl].astype(F32)
                    k = k_ref[pl.ds(start, cs), sl].astype(F32) * k_scale
                    if rope:
                        q = q * cos + pltpu.roll(q, HEAD_DIM // 2, 1) * sin
                        k = k * cos + pltpu.roll(k, HEAD_DIM // 2, 1) * sin
                    v = v_ref[pl.ds(start, cs), sl]
                    state = s_ref[c]
                    c_dec = jnp.exp(jnp.full((1, HEAD_DIM), lg_ref[d, head0 + hh] * cs, F32))
                    att = _mm_nt(q, k) * intra_ref[c]
                    o_ref[pl.ds(start, cs), sl] = _mm(att, v) + _mm(q * qd_ref[c], state)
                    s_ref[c] = state * c_dec + _mm_tn(k * kd_ref[c], v)
            return carry

        lax.fori_loop(0, n_chunks, step, 0)

    _finish_scan(ofc_ref, obc_ref, gc_ref, yc_ref, cs)
    _finish_scan(ofx_ref, obx_ref, gx_ref, yx_ref, cs)


def _retention(proj_c, proj_x, log_gamma, cos2, sin2):
    b, tc, _ = proj_c.shape
    tx = proj_x.shape[1]
    h, hp = RET_HEADS, SCAN_HEADS
    w = hp * HEAD_DIM

    def col(t, group):
        return pl.BlockSpec((None, t, w), lambda bi, hi, lg: (bi, 0, group * (h // hp) + hi))

    n_chain = 2 * hp
    grid_spec = pltpu.PrefetchScalarGridSpec(
        num_scalar_prefetch=1,
        grid=(b, h // hp),
        in_specs=[col(tc, 0), col(tc, 1), col(tc, 2), col(tc, 3),
                  col(tx, 0), col(tx, 1), col(tx, 2), col(tx, 3),
                  pl.BlockSpec((tx, HEAD_DIM), lambda bi, hi, lg: (0, 0)),
                  pl.BlockSpec((tx, HEAD_DIM), lambda bi, hi, lg: (0, 0))],
        out_specs=[pl.BlockSpec((None, tc, w), lambda bi, hi, lg: (bi, 0, hi)),
                   pl.BlockSpec((None, tx, w), lambda bi, hi, lg: (bi, 0, hi))],
        scratch_shapes=[pltpu.VMEM((tc, w), F32), pltpu.VMEM((tc, w), F32),
                        pltpu.VMEM((tx, w), F32), pltpu.VMEM((tx, w), F32),
                        pltpu.VMEM((n_chain, HEAD_DIM, HEAD_DIM), F32),
                        pltpu.VMEM((n_chain, RET_CHUNK, RET_CHUNK), F32),
                        pltpu.VMEM((n_chain, RET_CHUNK, HEAD_DIM), F32),
                        pltpu.VMEM((n_chain, RET_CHUNK, HEAD_DIM), F32)])
    return pl.pallas_call(
        _retention_kernel,
        grid_spec=grid_spec,
        out_shape=[jax.ShapeDtypeStruct((b, tc, h * HEAD_DIM), BF16),
                   jax.ShapeDtypeStruct((b, tx, h * HEAD_DIM), BF16)],
        compiler_params=_cparams("parallel", "parallel"),
        name="retention",
    )(log_gamma, proj_c, proj_c, proj_c, proj_c, proj_x, proj_x, proj_x, proj_x, cos2, sin2)


def _hgrn_levels(cs):
    return [cs >> (i + 1) for i in range(int(math.log2(cs)))]


def _hgrn_constants(cs):
    t = np.arange(cs)[:, None]
    s = np.arange(cs)[None, :]
    tri = np.stack([(s <= t), (s >= t)]).astype(np.float32)
    masks = []
    for reverse in (False, True):
        lv = []
        for h in _hgrn_levels(cs):
            same = (t // (2 * h)) == (s // (2 * h))
            t_hi = (t % (2 * h)) >= h
            s_hi = (s % (2 * h)) >= h
            lv.append(same & (~t_hi & s_hi if reverse else t_hi & ~s_hi))
        lv.append(t == s)
        masks.append(np.stack(lv))
    return jnp.asarray(tri, BF16), jnp.asarray(np.stack(masks).astype(np.float32))


def _split3(x):
    hi = x.astype(BF16)
    r = x - hi.astype(F32)
    mid = r.astype(BF16)
    lo = (r - mid.astype(F32)).astype(BF16)
    return hi, mid, lo


def _level_reference(b, h, reverse):
    cs = b.shape[0]
    ref_local = h if reverse else h - 1
    if 2 * h >= 8:
        b3 = b.reshape(cs // (2 * h), 2 * h, b.shape[1])
        r = b3[:, ref_local:ref_local + 1, :]
        return jnp.broadcast_to(r, b3.shape).reshape(b.shape)
    local = lax.broadcasted_iota(jnp.int32, b.shape, 0) % (2 * h)
    out = b
    for m in range(2 * h):
        if m != ref_local:
            out = jnp.where(local == m, pltpu.roll(b, (m - ref_local) % cs, 0), out)
    return out


def _hgrn_kernel(lb_ref, tri_ref, mask_ref,
                 qc_ref, vc_ref, gc_ref, zfc_ref, zbc_ref,
                 qx_ref, vx_ref, gx_ref, zfx_ref, zbx_ref,
                 yc_ref, yx_ref, ofc_ref, obc_ref, ofx_ref, obx_ref, st_ref):
    cs = HGRN_CHUNK
    hp = SCAN_HEADS
    levels = _hgrn_levels(cs)
    st_ref[...] = jnp.zeros_like(st_ref)

    for (q_ref, v_ref, zf_ref, zb_ref, of_ref, ob_ref) in (
            (qc_ref, vc_ref, zfc_ref, zbc_ref, ofc_ref, obc_ref),
            (qx_ref, vx_ref, zfx_ref, zbx_ref, ofx_ref, obx_ref)):
        n_chunks = q_ref.shape[0] // cs

        def step(i, carry, q_ref=q_ref, v_ref=v_ref, zf_ref=zf_ref, zb_ref=zb_ref, of_ref=of_ref,
                 ob_ref=ob_ref, n_chunks=n_chunks):
            for d in (0, 1):
                reverse = d == 1
                start = pl.multiple_of(((n_chunks - 1 - i) if d else i) * cs, cs)
                z_ref = zb_ref if d else zf_ref
                o_ref = ob_ref if d else of_ref
                for hh in range(hp):
                    c = 2 * hh + d
                    sl = _head_slice(hh)
                    lb = lb_ref[hh]
                    q = _silu(q_ref[pl.ds(start, cs), sl].astype(F32))
                    z = z_ref[pl.ds(start, cs), sl]
                    v = v_ref[pl.ds(start, cs), sl]
                    f = lb + (1.0 - lb) * (1.0 / (1.0 + jnp.exp(-z)))
                    k = 1.0 - f
                    logf = jnp.log(f)
                    tri = tri_ref[d]
                    hi, mid, lo = _split3(logf)
                    bsum = (jnp.dot(tri, hi, preferred_element_type=F32)
                            + jnp.dot(tri, mid, preferred_element_type=F32)
                            + jnp.dot(tri, lo, preferred_element_type=F32))
                    att = _mm_nt(q, k) * mask_ref[d, len(levels)]
                    for li, h in enumerate(levels):
                        w = jnp.exp(-jnp.abs(bsum - _level_reference(bsum, h, reverse)))
                        att += _mm_nt(q * w, k * w) * mask_ref[d, li]
                    state_t = st_ref[c]
                    o_ref[pl.ds(start, cs), sl] = _mm(att, v) + _mm_nt(q * jnp.exp(bsum), state_t)
                    b_end = bsum[0:1, :] if reverse else bsum[cs - 1:cs, :]
                    st_ref[c] = state_t * jnp.exp(b_end) + _mm_tn(v, k * jnp.exp(b_end - bsum))
            return carry

        lax.fori_loop(0, n_chunks, step, 0)

    _finish_scan(ofc_ref, obc_ref, gc_ref, yc_ref, cs)
    _finish_scan(ofx_ref, obx_ref, gx_ref, yx_ref, cs)


def _hgrn(proj_c, proj_x, z_c, z_x, lb, first):
    b, tc, _ = proj_c.shape
    tx = proj_x.shape[1]
    h, hp = HGRN_HEADS, SCAN_HEADS
    w = hp * HEAD_DIM
    tri, masks = _hgrn_constants(HGRN_CHUNK)

    def col(t, group):
        return pl.BlockSpec((None, t, w), lambda bi, hi: (bi, 0, (first + group * h) // hp + hi))

    def zcol(t, group):
        return pl.BlockSpec((None, t, w), lambda bi, hi: (bi, 0, group * (h // hp) + hi))

    ins = [col(tc, 0), col(tc, 1), col(tc, 2), zcol(tc, 0), zcol(tc, 1),
           col(tx, 0), col(tx, 1), col(tx, 2), zcol(tx, 0), zcol(tx, 1)]
    return pl.pallas_call(
        _hgrn_kernel,
        grid=(b, h // hp),
        in_specs=[pl.BlockSpec((hp, 1, HEAD_DIM), lambda bi, hi: (hi, 0, 0)),
                  pl.BlockSpec(tri.shape, lambda bi, hi: (0, 0, 0)),
                  pl.BlockSpec(masks.shape, lambda bi, hi: (0, 0, 0, 0))] + ins,
        out_specs=[pl.BlockSpec((None, tc, w), lambda bi, hi: (bi, 0, hi)),
                   pl.BlockSpec((None, tx, w), lambda bi, hi: (bi, 0, hi))],
        out_shape=[jax.ShapeDtypeStruct((b, tc, h * HEAD_DIM), BF16),
                   jax.ShapeDtypeStruct((b, tx, h * HEAD_DIM), BF16)],
        scratch_shapes=[pltpu.VMEM((tc, w), F32), pltpu.VMEM((tc, w), F32),
                        pltpu.VMEM((tx, w), F32), pltpu.VMEM((tx, w), F32),
                        pltpu.VMEM((2 * hp, HEAD_DIM, HEAD_DIM), F32)],
        compiler_params=_cparams("parallel", "parallel"),
        name="hgrn2",
    )(lb.reshape(h, 1, HEAD_DIM), tri, masks, proj_c, proj_c, proj_c, z_c, z_c,
      proj_x, proj_x, proj_x, z_x, z_x)


MLA_QW = 2 * LANE
MLA_DOWN_PAD = MLA_Q_RANK + MLA_KV_RANK + LANE


def _mla_proj_kernel(x_ref, mod_ref, g_ref, wd_ref, gq_ref, gkv_ref, wq_ref, wkv_ref, cos_ref, sin_ref,
                     *out_refs, rope, want_q):
    h = _modnorm(x_ref[...], g_ref[...], mod_ref, 0, 1).astype(BF16)
    down = jnp.dot(h, wd_ref[...], preferred_element_type=F32)
    if rope:
        cos = cos_ref[...]
        sin = sin_ref[...]

    def rot(v):
        return v * cos + pltpu.roll(v, LANE // 2, 1) * sin if rope else v

    if want_q:
        q_ref, kn_ref, kr_ref, v_ref = out_refs
        qn = (_rms(down[:, :MLA_Q_RANK]) * gq_ref[...]).astype(BF16)
        q = jnp.dot(qn, wq_ref[...], preferred_element_type=F32)
        for hd in range(MLA_HEADS):
            lo = hd * MLA_QW
            q_ref[:, lo:lo + LANE] = q[:, lo:lo + LANE].astype(q_ref.dtype)
            q_ref[:, lo + LANE:lo + 2 * LANE] = rot(q[:, lo + LANE:lo + 2 * LANE]).astype(q_ref.dtype)
    else:
        kn_ref, kr_ref, v_ref = out_refs
    kvn = (_rms(down[:, MLA_Q_RANK:MLA_Q_RANK + MLA_KV_RANK]) * gkv_ref[...]).astype(BF16)
    kv = jnp.dot(kvn, wkv_ref[...], preferred_element_type=F32)
    nk = MLA_HEADS * MLA_NOPE
    kn_ref[...] = kv[:, :nk].astype(kn_ref.dtype)
    v_ref[...] = kv[:, nk:].astype(v_ref.dtype)
    kr_ref[...] = rot(down[:, MLA_Q_RANK + MLA_KV_RANK:]).astype(kr_ref.dtype)


def _mla_proj(x, mod, g, wd, gq, gkv, wq, wkv, cos, sin, rope, want_q, tm=512):
    b, t, d = x.shape
    tm = _pick(t, tm)
    const = lambda bi, i: (0, 0)
    tok = lambda bi, i: (bi, i, 0)
    nk, nv = MLA_HEADS * MLA_NOPE, MLA_HEADS * MLA_V
    out_shape = [jax.ShapeDtypeStruct((b, t, nk), BF16), jax.ShapeDtypeStruct((b, t, LANE), BF16),
                 jax.ShapeDtypeStruct((b, t, nv), BF16)]
    out_specs = [pl.BlockSpec((None, tm, nk), tok), pl.BlockSpec((None, tm, LANE), tok),
                 pl.BlockSpec((None, tm, nv), tok)]
    if want_q:
        out_shape = [jax.ShapeDtypeStruct((b, t, MLA_HEADS * MLA_QW), BF16)] + out_shape
        out_specs = [pl.BlockSpec((None, tm, MLA_HEADS * MLA_QW), tok)] + out_specs
    return pl.pallas_call(
        functools.partial(_mla_proj_kernel, rope=rope, want_q=want_q),
        grid=(b, t // tm),
        in_specs=[pl.BlockSpec((None, tm, d), tok), _mod_spec(mod), pl.BlockSpec((1, d), const),
                  pl.BlockSpec(wd.shape, const), pl.BlockSpec((1, MLA_Q_RANK), const),
                  pl.BlockSpec((1, MLA_KV_RANK), const), pl.BlockSpec(wq.shape, const),
                  pl.BlockSpec(wkv.shape, const),
                  pl.BlockSpec((tm, LANE), lambda bi, i: (i, 0)),
                  pl.BlockSpec((tm, LANE), lambda bi, i: (i, 0))],
        out_specs=out_specs,
        out_shape=out_shape,
        compiler_params=_cparams("parallel", "parallel"),
        name="mla_proj",
    )(x, mod, g.reshape(1, d), wd, gq.reshape(1, -1), gkv.reshape(1, -1), wq, wkv, cos, sin)


def _lane_fold(x, op):
    out = x[:, :LANE]
    for j in range(1, x.shape[1] // LANE):
        out = op(out, x[:, j * LANE:(j + 1) * LANE])
    return out


def _attention_kernel(q_ref, knc_ref, krc_ref, vc_ref, knx_ref, krx_ref, vx_ref, o_ref,
                      kc_ref, kx_ref, sc0_ref, sc1_ref, sx0_ref, sx1_ref, m0_ref, m1_ref, *, tk, exp_scale):
    step = pl.program_id(2)

    @pl.when(step == 0)
    def _():
        kc_ref[:, :LANE] = knc_ref[...]
        kc_ref[:, LANE:] = krc_ref[...]
        kx_ref[:, :LANE] = knx_ref[...]
        kx_ref[:, LANE:] = krx_ref[...]
        sc1_ref[...] = jnp.zeros(sc1_ref.shape, F32)
        sx1_ref[...] = jnp.zeros(sx1_ref.shape, F32)
        m1_ref[...] = jnp.zeros(m1_ref.shape, F32)

    n_blocks = kx_ref.shape[0] // tk
    nt = (((1,), (1,)), ((), ()))

    def body(sc_cur, sx_cur, m_cur, sc_prev, sx_prev, m_prev):
        q = q_ref[...]
        m = jnp.max(m_prev[...], axis=-1, keepdims=True)
        s = lax.dot_general(q, kc_ref[...], nt, preferred_element_type=F32)
        sc_cur[...] = s
        m_part = _lane_fold(s, jnp.maximum)
        p = jnp.exp2((sc_prev[...] - m) * exp_scale)
        l_part = _lane_fold(p, jnp.add)
        acc = jnp.dot(p.astype(BF16), vc_ref[...], preferred_element_type=F32)
        for i in range(n_blocks):
            s = lax.dot_general(q, kx_ref[i * tk:(i + 1) * tk, :], nt, preferred_element_type=F32)
            sx_cur[i] = s
            m_part = jnp.maximum(m_part, _lane_fold(s, jnp.maximum))
            p = jnp.exp2((sx_prev[i] - m) * exp_scale)
            l_part = l_part + _lane_fold(p, jnp.add)
            acc = acc + jnp.dot(p.astype(BF16), vx_ref[i * tk:(i + 1) * tk, :],
                                preferred_element_type=F32)
        m_cur[...] = m_part
        l = jnp.sum(l_part, axis=-1, keepdims=True)
        o_ref[...] = (acc / l).astype(o_ref.dtype)

    @pl.when(step % 2 == 0)
    def _():
        body(sc0_ref, sx0_ref, m0_ref, sc1_ref, sx1_ref, m1_ref)

    @pl.when(step % 2 == 1)
    def _():
        body(sc1_ref, sx1_ref, m1_ref, sc0_ref, sx0_ref, m0_ref)


def _attention(q, kn_c, kr_c, v_c, kn_x, kr_x, v_x, scale, tq=256, tk=512):
    b, t, _ = q.shape
    tc = kn_c.shape[1]
    tq = _pick(t, tq)
    tk = _pick(t, tk)
    nq = t // tq
    head_c = lambda bi, hi, i: (bi, 0, hi)
    shared = lambda bi, hi, i: (bi, 0, 0)
    return pl.pallas_call(
        functools.partial(_attention_kernel, tk=tk, exp_scale=scale * math.log2(math.e)),
        grid=(b, MLA_HEADS, nq + 1),
        in_specs=[pl.BlockSpec((None, tq, MLA_QW), lambda bi, hi, i: (bi, jnp.minimum(i, nq - 1), hi)),
                  pl.BlockSpec((None, tc, LANE), head_c), pl.BlockSpec((None, tc, LANE), shared),
                  pl.BlockSpec((None, tc, LANE), head_c),
                  pl.BlockSpec((None, t, LANE), head_c), pl.BlockSpec((None, t, LANE), shared),
                  pl.BlockSpec((None, t, LANE), head_c)],
        out_specs=pl.BlockSpec((None, tq, MLA_V), lambda bi, hi, i: (bi, jnp.maximum(i - 1, 0), hi)),
        out_shape=jax.ShapeDtypeStruct((b, t, MLA_HEADS * MLA_V), BF16),
        scratch_shapes=[pltpu.VMEM((tc, MLA_QW), BF16), pltpu.VMEM((t, MLA_QW), BF16),
                        pltpu.VMEM((tq, tc), F32), pltpu.VMEM((tq, tc), F32),
                        pltpu.VMEM((t // tk, tq, tk), F32), pltpu.VMEM((t // tk, tq, tk), F32),
                        pltpu.VMEM((tq, LANE), F32), pltpu.VMEM((tq, LANE), F32)],
        compiler_params=_cparams("parallel", "parallel", "arbitrary"),
        name="mla_attention",
    )(q, kn_c, kr_c, v_c, kn_x, kr_x, v_x)


def _router_kernel(x_ref, mod_ref, g_ref, wr_ref, h_ref, w_ref, e_ref):
    h = _modnorm(x_ref[...], g_ref[...], mod_ref, 3, 4)
    h_ref[...] = _pack_halves(h)
    logits = jnp.dot(h, wr_ref[...], preferred_element_type=F32, precision=lax.Precision.HIGHEST)
    lane = lax.broadcasted_iota(jnp.int32, logits.shape, 1)
    neg = jnp.float32(-jnp.inf)
    lg = jnp.where(lane < N_EXPERTS, logits, neg)
    m1 = jnp.max(lg, axis=-1, keepdims=True)
    i1 = jnp.min(jnp.where(lg == m1, lane, LANE), axis=-1, keepdims=True)
    lg2 = jnp.where(lane == i1, neg, lg)
    m2 = jnp.max(lg2, axis=-1, keepdims=True)
    i2 = jnp.min(jnp.where(lg2 == m2, lane, LANE), axis=-1, keepdims=True)
    e2 = jnp.exp(m2 - m1)
    w1 = 1.0 / (1.0 + e2)
    w2 = e2 / (1.0 + e2)
    w_ref[...] = jnp.where(lane == 0, w1, jnp.where(lane == 1, w2, 0.0))
    e_ref[...] = jnp.where(lane == 0, i1, jnp.where(lane == 1, i2, 0))


def _router(x, mod, g, router_pad, tm=512):
    b, t, d = x.shape
    tm = _pick(t, tm)
    tok = lambda bi, i: (bi, i, 0)
    const = lambda bi, i: (0, 0)
    return pl.pallas_call(
        _router_kernel,
        grid=(b, t // tm),
        in_specs=[pl.BlockSpec((None, tm, d), tok), _mod_spec(mod), pl.BlockSpec((1, d), const),
                  pl.BlockSpec((d, LANE), const)],
        out_specs=[pl.BlockSpec((None, tm, d // 2), tok), pl.BlockSpec((None, tm, LANE), tok),
                   pl.BlockSpec((None, tm, LANE), tok)],
        out_shape=[jax.ShapeDtypeStruct((b, t, d // 2), jnp.int32), jax.ShapeDtypeStruct((b, t, LANE), F32),
                   jax.ShapeDtypeStruct((b, t, LANE), jnp.int32)],
        compiler_params=_cparams("parallel", "parallel"),
        name="moe_router",
    )(x, mod, g.reshape(1, d), router_pad)


SC_GATHER_ROWS = 64


def _gather_rows(table, idx):
    info = plsc.get_sparse_core_info()
    n_workers = info.num_cores * info.num_subcores
    n, width = idx.shape[0], table.shape[1]
    chunk = SC_GATHER_ROWS
    assert n % (n_workers * chunk) == 0
    n_chunks = n // (n_workers * chunk)
    mesh = plsc.VectorSubcoreMesh(core_axis_name="c", subcore_axis_name="s")

    def body(table_hbm, idx_hbm, out_hbm, idx_v, rows_v, sem):
        wid = lax.axis_index("s") * info.num_cores + lax.axis_index("c")
        pltpu.sync_copy(idx_hbm.at[wid], idx_v)

        @pl.loop(0, n_chunks)
        def _(j):
            pltpu.async_copy(table_hbm.at[idx_v.at[j]], rows_v, sem).wait()
            pltpu.sync_copy(rows_v, out_hbm.at[pl.ds((wid * n_chunks + j) * chunk, chunk)])

    return pl.kernel(
        body,
        out_type=jax.ShapeDtypeStruct((n, width), table.dtype),
        mesh=mesh,
        scratch_types=[pltpu.VMEM((n_chunks, chunk), jnp.int32), pltpu.VMEM((chunk, width), table.dtype),
                       pltpu.SemaphoreType.DMA],
        name="sc_gather_rows",
    )(table, idx.reshape(n_workers, n_chunks, chunk))


def _scatter_rows_twice(table, idx0, idx1, n_out):
    info = plsc.get_sparse_core_info()
    n_workers = info.num_cores * info.num_subcores
    n, width = table.shape
    chunk = SC_GATHER_ROWS
    assert n % (n_workers * chunk) == 0
    n_chunks = n // (n_workers * chunk)
    mesh = plsc.VectorSubcoreMesh(core_axis_name="c", subcore_axis_name="s")

    def body(table_hbm, i0_hbm, i1_hbm, out_hbm, i0_v, i1_v, rows_v):
        wid = lax.axis_index("s") * info.num_cores + lax.axis_index("c")
        pltpu.sync_copy(i0_hbm.at[wid], i0_v)
        pltpu.sync_copy(i1_hbm.at[wid], i1_v)

        @pl.loop(0, n_chunks)
        def _(j):
            pltpu.sync_copy(table_hbm.at[pl.ds((wid * n_chunks + j) * chunk, chunk)], rows_v)
            pltpu.sync_copy(rows_v, out_hbm.at[i0_v.at[j]])
            pltpu.sync_copy(rows_v, out_hbm.at[i1_v.at[j]])

    shape3 = (n_workers, n_chunks, chunk)
    return pl.kernel(
        body,
        out_type=jax.ShapeDtypeStruct((n_out, width), table.dtype),
        mesh=mesh,
        scratch_types=[pltpu.VMEM((n_chunks, chunk), jnp.int32), pltpu.VMEM((n_chunks, chunk), jnp.int32),
                       pltpu.VMEM((chunk, width), table.dtype)],
        name="sc_scatter_rows",
    )(table, idx0.reshape(shape3), idx1.reshape(shape3))


def _pack_halves(x):
    k = x.shape[1] // 2
    bits = lambda v: lax.bitcast_convert_type(v.astype(BF16).astype(F32), jnp.uint32)
    word = lax.shift_right_logical(bits(x[:, :k]), jnp.uint32(16)) | bits(x[:, k:])
    return lax.bitcast_convert_type(word, jnp.int32)


def _unpack_halves(w):
    u = lax.bitcast_convert_type(w, jnp.uint32)
    lo = lax.bitcast_convert_type(lax.shift_left(u, jnp.uint32(16)), F32)
    hi = lax.bitcast_convert_type(u & jnp.uint32(0xFFFF0000), F32)
    return jnp.concatenate([lo, hi], axis=-1)


def _expert_ffn_kernel(te_ref, tv_ref, x_ref, wa_ref, wb_ref, wo_ref, o_ref, acc_ref):
    j = pl.program_id(0)
    f = pl.program_id(1)
    last = pl.num_programs(1) - 1
    valid = tv_ref[j] != 0

    @pl.when(valid)
    def _():
        @pl.when(f == 0)
        def _():
            acc_ref[...] = jnp.zeros_like(acc_ref)

        h = _unpack_halves(x_ref[...]).astype(BF16)
        a = jnp.dot(h, wa_ref[...], preferred_element_type=F32)
        bb = jnp.dot(h, wb_ref[...], preferred_element_type=F32)
        acc_ref[...] += jnp.dot((_silu(a) * bb).astype(BF16), wo_ref[...], preferred_element_type=F32)

        @pl.when(f == last)
        def _():
            o_ref[...] = _pack_halves(acc_ref[...])

    @pl.when(jnp.logical_not(valid) & (f == last))
    def _():
        o_ref[...] = jnp.zeros_like(o_ref)


def _expert_ffn(xs, tile_expert, tile_valid, w_in, w_out, tm, tf=1408):
    p = xs.shape[0]
    n_exp, fdim, d = w_out.shape
    tf = _pick(fdim, tf)
    nf = fdim // tf

    def f_eff(j, f, tv):
        return jnp.where(tv[j] != 0, f, nf - 1)

    grid_spec = pltpu.PrefetchScalarGridSpec(
        num_scalar_prefetch=2,
        grid=(p // tm, nf),
        in_specs=[pl.BlockSpec((tm, d // 2), lambda j, f, te, tv: (j, 0)),
                  pl.BlockSpec((None, d, tf), lambda j, f, te, tv: (te[j], 0, f_eff(j, f, tv))),
                  pl.BlockSpec((None, d, tf), lambda j, f, te, tv: (te[j], 0, f_eff(j, f, tv) + nf)),
                  pl.BlockSpec((None, tf, d), lambda j, f, te, tv: (te[j], f_eff(j, f, tv), 0))],
        out_specs=pl.BlockSpec((tm, d // 2), lambda j, f, te, tv: (j, 0)),
        scratch_shapes=[pltpu.VMEM((tm, d), F32)])
    return pl.pallas_call(
        _expert_ffn_kernel,
        grid_spec=grid_spec,
        out_shape=jax.ShapeDtypeStruct((p, d // 2), jnp.int32),
        compiler_params=_cparams("parallel", "arbitrary"),
        name="moe_expert_ffn",
    )(tile_expert, tile_valid, xs, w_in, w_in, w_out)


def _moe_combine_kernel(x_ref, y_ref, w_ref, mod_ref, g_ref, o_ref):
    k = x_ref.shape[-1] // 2
    y = w_ref[:, 0:1] * _unpack_halves(y_ref[:, :k]) + w_ref[:, 1:2] * _unpack_halves(y_ref[:, k:])
    o_ref[...] = x_ref[...] + mod_ref[5:6, :] * (_rms(y) * g_ref[...])


def _moe_combine(x, y2, w, mod, g, tm=512):
    b, t, d = x.shape
    tm = _pick(t, tm)
    tok = lambda bi, i: (bi, i, 0)
    return pl.pallas_call(
        _moe_combine_kernel,
        grid=(b, t // tm),
        in_specs=[pl.BlockSpec((None, tm, d), tok), pl.BlockSpec((None, tm, d), tok),
                  pl.BlockSpec((None, tm, LANE), tok), _mod_spec(mod),
                  pl.BlockSpec((1, d), lambda bi, i: (0, 0))],
        out_specs=pl.BlockSpec((None, tm, d), tok),
        out_shape=jax.ShapeDtypeStruct((b, t, d), F32),
        compiler_params=_cparams("parallel", "parallel"),
        name="moe_combine",
    )(x, y2, w, mod, g.reshape(1, d))


def _moe(x, mod, g_in, g_out, router_pad, w_in, w_out, tm=512):
    b, t, d = x.shape
    n_exp = w_out.shape[0]
    n = b * t
    h, w, e = _router(x, mod, g_in, router_pad)
    e_flat = e[..., :2].reshape(2 * n)
    onehot = (e_flat[:, None] == jnp.arange(n_exp, dtype=jnp.int32)).astype(jnp.int32)
    csum = jnp.cumsum(onehot, axis=0)
    rank = jnp.sum((csum - onehot) * onehot, axis=-1)
    counts = csum[-1]
    ends = jnp.cumsum((counts + tm - 1) // tm * tm)
    dest = (ends - (counts + tm - 1) // tm * tm)[e_flat] + rank
    p = 2 * n + n_exp * tm
    tile_start = jnp.arange(p // tm, dtype=jnp.int32) * tm
    tile_expert = jnp.minimum(jnp.searchsorted(ends, tile_start, side="right"), n_exp - 1).astype(jnp.int32)
    tile_valid = (tile_start < ends[-1]).astype(jnp.int32)

    dest2 = dest.reshape(n, 2)
    xs = _scatter_rows_twice(h.reshape(n, d // 2), dest2[:, 0], dest2[:, 1], p)
    ys = _expert_ffn(xs, tile_expert, tile_valid, w_in, w_out, tm)
    y2 = _gather_rows(ys, dest)
    return _moe_combine(x, y2.reshape(b, t, d), w, mod, g_out)


def _axial_angles(n_tokens, dim):
    t = jnp.arange(n_tokens)
    rows = (t // GRID_W).astype(F32)
    cols = (t % GRID_W).astype(F32)
    n_freq = dim // 4
    inv = ROPE_BASE ** (-jnp.arange(n_freq, dtype=F32) / n_freq)
    return jnp.concatenate([rows[:, None] * inv, cols[:, None] * inv], axis=-1)


def _rope_tables(n_tokens, dim):
    ang = _axial_angles(n_tokens, dim)
    cos, sin = jnp.cos(ang), jnp.sin(ang)
    if dim == LANE:
        return jnp.concatenate([cos, cos], -1), jnp.concatenate([-sin, sin], -1)
    z = jnp.zeros_like(cos)
    return jnp.concatenate([cos, z, cos, z], -1), jnp.concatenate([-sin, z, sin, z], -1)


def _spread_rope_cols(w):
    half = MLA_ROPE // 2
    z = jnp.zeros(w.shape[:-1] + (half,), w.dtype)
    return jnp.concatenate([w[..., :half], z, w[..., half:], z], axis=-1)


def kernel(x, c, ctx, c_ctx, mod_w, mod_b, norm_g, mix_in_w, ret_decay_logit, hgrn_lb_logit, mix_out_w,
           ffn_in_w, ffn_out_w, mla_down_w, mla_q_norm_g, mla_kv_norm_g, mla_uq_w, mla_ukv_w, mla_out_w,
           router_w, moe_in_w, moe_out_w):
    batch, seq, d = x.shape
    assert mod_w.shape[0] == 2, "two layers: retention/HGRN2 then MLA/MoE"

    rows = -(-(batch + 1) // 8) * 8
    cond = jnp.zeros((rows, d), F32).at[:batch].set(c).at[batch].set(c_ctx)
    mod = _modulation(cond, mod_w, mod_b)
    mod_x = [mod[l, :batch].reshape(batch, 6, d) for l in range(2)]
    mod_c = [mod[l, batch:batch + 1].reshape(1, 6, d) for l in range(2)]

    g = norm_g[0]
    ret_w = RET_HEADS * HEAD_DIM
    z_lo, z_hi = 5 * ret_w, 7 * ret_w
    w_in = mix_in_w[0].astype(BF16)
    w_main = jnp.concatenate([w_in[:, :z_lo], w_in[:, z_hi:]], axis=1)
    w_z = w_in[:, z_lo:z_hi]
    proj_x = _norm_matmul(x, mod_x[0], g[0], w_main, 0, 1, BF16)
    proj_c = _norm_matmul(ctx, mod_c[0], g[0], w_main, 0, 1, BF16)
    z_x = _norm_matmul(x, mod_x[0], g[0], w_z, 0, 1, F32)
    z_c = _norm_matmul(ctx, mod_c[0], g[0], w_z, 0, 1, F32)
    cos2, sin2 = _rope_tables(seq, HEAD_DIM)
    log_gamma = jax.nn.log_sigmoid(ret_decay_logit[0].astype(F32))
    lb = jnp.cumsum(jax.nn.softmax(hgrn_lb_logit.astype(F32), axis=0), axis=0)[0]
    ret_c, ret_x = _retention(proj_c, proj_x, log_gamma, cos2, sin2)
    hg_c, hg_x = _hgrn(proj_c, proj_x, z_c, z_x, lb, first=4 * RET_HEADS)
    w_out = mix_out_w[0].astype(BF16)
    w_outs = [w_out[:ret_w], w_out[ret_w:]]
    x = _proj_residual([ret_x, hg_x], w_outs, x, mod_x[0], g[1], 2)
    ctx = _proj_residual([ret_c, hg_c], w_outs, ctx, mod_c[0], g[1], 2)
    f_in, f_out = ffn_in_w[0].astype(BF16), ffn_out_w[0].astype(BF16)
    x = _ffn(x, mod_x[0], g[2], g[3], f_in, f_out)
    ctx = _ffn(ctx, mod_c[0], g[2], g[3], f_in, f_out)

    g = norm_g[1]
    dq = MLA_NOPE + MLA_ROPE
    wd = mla_down_w[0]
    wd = jnp.concatenate([wd[:, :MLA_Q_RANK + MLA_KV_RANK],
                          _spread_rope_cols(wd[:, MLA_Q_RANK + MLA_KV_RANK:])], axis=-1).astype(BF16)
    wq = mla_uq_w[0].reshape(MLA_Q_RANK, MLA_HEADS, dq)
    wq = jnp.concatenate([wq[..., :MLA_NOPE], _spread_rope_cols(wq[..., MLA_NOPE:])], axis=-1)
    wq = wq.reshape(MLA_Q_RANK, MLA_HEADS * MLA_QW).astype(BF16)
    wkv = mla_ukv_w[0].reshape(MLA_KV_RANK, MLA_HEADS, MLA_NOPE + MLA_V)
    wkv = jnp.concatenate([wkv[..., :MLA_NOPE].reshape(MLA_KV_RANK, -1),
                           wkv[..., MLA_NOPE:].reshape(MLA_KV_RANK, -1)], axis=-1).astype(BF16)
    cos_m, sin_m = _rope_tables(seq, MLA_ROPE)
    q, kn_x, kr_x, v_x = _mla_proj(x, mod_x[1], g[0], wd, mla_q_norm_g[0], mla_kv_norm_g[0], wq, wkv,
                                   cos_m, sin_m, rope=True, want_q=True)
    tc = ctx.shape[1]
    kn_c, kr_c, v_c = _mla_proj(ctx, mod_c[1], g[0], wd, mla_q_norm_g[0], mla_kv_norm_g[0], wq, wkv,
                                cos_m[:tc], sin_m[:tc], rope=False, want_q=False)
    att = _attention(q, kn_c, kr_c, v_c, kn_x, kr_x, v_x, dq ** -0.5)
    x = _proj_residual([att], [mla_out_w[0].astype(BF16)], x, mod_x[1], g[1], 2)
    router_pad = jnp.zeros((d, LANE), F32).at[:, :N_EXPERTS].set(router_w[0])
    x = _moe(x, mod_x[1], g[2], g[3], router_pad, moe_in_w[0].astype(BF16), moe_out_w[0].astype(BF16))
    return x
```

```python
import functools
import math

import jax
import jax.numpy as jnp
import numpy as np
from jax import lax
from jax.experimental import pallas as pl
from jax.experimental.pallas import tpu as pltpu
from jax.experimental.pallas import tpu_sc as plsc

F32 = jnp.float32
BF16 = jnp.bfloat16

NORM_EPS = 1e-6
ROPE_BASE = 10000.0
GRID_W = 64
LANE = 128

RET_HEADS = 4
HGRN_HEADS = 4
HEAD_DIM = 128
RET_CHUNK = 128
HGRN_CHUNK = 128

MLA_HEADS = 8
MLA_NOPE = 128
MLA_ROPE = 64
MLA_V = 128
MLA_Q_RANK = 384
MLA_KV_RANK = 256
N_EXPERTS = 8

VMEM_LIMIT = 56 * 1024 * 1024


def _cparams(*sem):
    return pltpu.CompilerParams(dimension_semantics=sem, vmem_limit_bytes=VMEM_LIMIT)


def _mm(a, b):
    return jnp.dot(a.astype(BF16), b.astype(BF16), preferred_element_type=F32)


def _mm_nt(a, b):
    return lax.dot_general(a.astype(BF16), b.astype(BF16), (((1,), (1,)), ((), ())),
                           preferred_element_type=F32)


def _mm_tn(a, b):
    return lax.dot_general(a.astype(BF16), b.astype(BF16), (((0,), (0,)), ((), ())),
                           preferred_element_type=F32)


def _rms(x):
    return x * lax.rsqrt(jnp.mean(x * x, axis=-1, keepdims=True) + NORM_EPS)


def _silu(x):
    return x * (1.0 / (1.0 + jnp.exp(-x)))


def _modnorm(x, g, mod_ref, shift_row, scale_row):
    shift = mod_ref[shift_row:shift_row + 1, :]
    scale = mod_ref[scale_row:scale_row + 1, :]
    return _rms(x) * g * (1.0 + scale) + shift


def _mod_spec(mod):
    d = mod.shape[-1]
    if mod.shape[0] == 1:
        return pl.BlockSpec((None, 6, d), lambda b, *_: (0, 0, 0))
    return pl.BlockSpec((None, 6, d), lambda b, *_: (b, 0, 0))


def _pick(n, pref):
    t = min(pref, n)
    while n % t:
        t -= LANE
    return t


def _mod_kernel(c_ref, w_ref, b_ref, o_ref):
    a = _silu(c_ref[...])
    o_ref[...] = jnp.dot(a, w_ref[...], preferred_element_type=F32,
                         precision=lax.Precision.HIGHEST) + b_ref[...]


def _modulation(cond, mod_w, mod_b):
    n_layers, d, n = mod_w.shape
    r = cond.shape[0]
    tn = _pick(n, 1536)
    return pl.pallas_call(
        _mod_kernel,
        grid=(n_layers, n // tn),
        in_specs=[pl.BlockSpec((r, d), lambda l, j: (0, 0)),
                  pl.BlockSpec((None, d, tn), lambda l, j: (l, 0, j)),
                  pl.BlockSpec((None, 1, tn), lambda l, j: (l, 0, j))],
        out_specs=pl.BlockSpec((None, r, tn), lambda l, j: (l, 0, j)),
        out_shape=jax.ShapeDtypeStruct((n_layers, r, n), F32),
        compiler_params=_cparams("parallel", "parallel"),
        name="modulation",
    )(cond, mod_w, mod_b.reshape(n_layers, 1, n))


def _norm_matmul_kernel(x_ref, mod_ref, g_ref, w_ref, o_ref, h_ref, *, shift_row, scale_row):
    @pl.when(pl.program_id(2) == 0)
    def _():
        h_ref[...] = _modnorm(x_ref[...], g_ref[...], mod_ref, shift_row, scale_row).astype(BF16)

    o_ref[...] = jnp.dot(h_ref[...], w_ref[...], preferred_element_type=F32).astype(o_ref.dtype)


def _norm_matmul(x, mod, g, w, shift_row, scale_row, out_dtype, tm=1024, tn=1792):
    b, t, d = x.shape
    n = w.shape[1]
    tm = _pick(t, tm)
    tn = _pick(n, tn)
    return pl.pallas_call(
        functools.partial(_norm_matmul_kernel, shift_row=shift_row, scale_row=scale_row),
        grid=(b, t // tm, n // tn),
        in_specs=[pl.BlockSpec((None, tm, d), lambda bi, i, j: (bi, i, 0)),
                  _mod_spec(mod),
                  pl.BlockSpec((1, d), lambda bi, i, j: (0, 0)),
                  pl.BlockSpec((d, tn), lambda bi, i, j: (0, j))],
        out_specs=pl.BlockSpec((None, tm, tn), lambda bi, i, j: (bi, i, j)),
        out_shape=jax.ShapeDtypeStruct((b, t, n), out_dtype),
        scratch_shapes=[pltpu.VMEM((tm, d), BF16)],
        compiler_params=_cparams("parallel", "parallel", "arbitrary"),
        name="norm_matmul",
    )(x, mod, g.reshape(1, d), w)


def _proj_residual_kernel(*refs, n_lhs, gate_row):
    lhs = refs[:n_lhs]
    ws = refs[n_lhs:2 * n_lhs]
    x_ref, mod_ref, g_ref, o_ref = refs[2 * n_lhs:]
    y = jnp.dot(lhs[0][...], ws[0][...], preferred_element_type=F32)
    for a_ref, w_ref in zip(lhs[1:], ws[1:]):
        y += jnp.dot(a_ref[...], w_ref[...], preferred_element_type=F32)
    gate = mod_ref[gate_row:gate_row + 1, :]
    o_ref[...] = x_ref[...] + gate * (_rms(y) * g_ref[...])


def _proj_residual(lhs, ws, x, mod, g, gate_row, tm=512):
    b, t, d = x.shape
    tm = _pick(t, tm)
    n_lhs = len(lhs)
    in_specs = [pl.BlockSpec((None, tm, a.shape[-1]), lambda bi, i: (bi, i, 0)) for a in lhs]
    in_specs += [pl.BlockSpec(w.shape, lambda bi, i: (0, 0)) for w in ws]
    in_specs += [pl.BlockSpec((None, tm, d), lambda bi, i: (bi, i, 0)),
                 _mod_spec(mod),
                 pl.BlockSpec((1, d), lambda bi, i: (0, 0))]
    return pl.pallas_call(
        functools.partial(_proj_residual_kernel, n_lhs=n_lhs, gate_row=gate_row),
        grid=(b, t // tm),
        in_specs=in_specs,
        out_specs=pl.BlockSpec((None, tm, d), lambda bi, i: (bi, i, 0)),
        out_shape=jax.ShapeDtypeStruct((b, t, d), F32),
        compiler_params=_cparams("parallel", "parallel"),
        name="proj_residual",
    )(*lhs, *ws, x, mod, g.reshape(1, d))


def _ffn_kernel(x_ref, mod_ref, g_in_ref, g_out_ref, wa_ref, wb_ref, wo_ref, o_ref, h_ref, acc_ref):
    f = pl.program_id(2)

    @pl.when(f == 0)
    def _():
        h_ref[...] = _modnorm(x_ref[...], g_in_ref[...], mod_ref, 3, 4).astype(BF16)
        acc_ref[...] = jnp.zeros_like(acc_ref)

    h = h_ref[...]
    a = jnp.dot(h, wa_ref[...], preferred_element_type=F32)
    bb = jnp.dot(h, wb_ref[...], preferred_element_type=F32)
    acc_ref[...] += jnp.dot((_silu(a) * bb).astype(BF16), wo_ref[...], preferred_element_type=F32)

    @pl.when(f == pl.num_programs(2) - 1)
    def _():
        gate = mod_ref[5:6, :]
        o_ref[...] = x_ref[...] + gate * (_rms(acc_ref[...]) * g_out_ref[...])


def _ffn(x, mod, g_in, g_out, w_in, w_out, tm=512, tf=1408):
    b, t, d = x.shape
    fdim = w_out.shape[0]
    tm = _pick(t, tm)
    tf = _pick(fdim, tf)
    nf = fdim // tf
    return pl.pallas_call(
        _ffn_kernel,
        grid=(b, t // tm, nf),
        in_specs=[pl.BlockSpec((None, tm, d), lambda bi, i, f: (bi, i, 0)),
                  _mod_spec(mod),
                  pl.BlockSpec((1, d), lambda bi, i, f: (0, 0)),
                  pl.BlockSpec((1, d), lambda bi, i, f: (0, 0)),
                  pl.BlockSpec((d, tf), lambda bi, i, f: (0, f)),
                  pl.BlockSpec((d, tf), lambda bi, i, f: (0, f + nf)),
                  pl.BlockSpec((tf, d), lambda bi, i, f: (f, 0))],
        out_specs=pl.BlockSpec((None, tm, d), lambda bi, i, f: (bi, i, 0)),
        out_shape=jax.ShapeDtypeStruct((b, t, d), F32),
        scratch_shapes=[pltpu.VMEM((tm, d), BF16), pltpu.VMEM((tm, d), F32)],
        compiler_params=_cparams("parallel", "parallel", "arbitrary"),
        name="ffn",
    )(x, mod, g_in.reshape(1, d), g_out.reshape(1, d), w_in, w_in, w_out)


SCAN_HEADS = 2


def _head_slice(hh):
    return slice(hh * HEAD_DIM, (hh + 1) * HEAD_DIM)


def _finish_scan(of_ref, ob_ref, g_ref, y_ref, cs):
    def step(i, carry):
        start = pl.multiple_of(i * cs, cs)
        for hh in range(y_ref.shape[1] // HEAD_DIM):
            sl = _head_slice(hh)
            o = of_ref[pl.ds(start, cs), sl] + ob_ref[pl.ds(start, cs), sl]
            gate = g_ref[pl.ds(start, cs), sl].astype(F32)
            y_ref[pl.ds(start, cs), sl] = (_rms(o) * _silu(gate)).astype(y_ref.dtype)
        return carry

    lax.fori_loop(0, y_ref.shape[0] // cs, step, 0)


def _retention_kernel(lg_ref, qc_ref, kc_ref, vc_ref, gc_ref, qx_ref, kx_ref, vx_ref, gx_ref,
                      cos_ref, sin_ref, yc_ref, yx_ref,
                      ofc_ref, obc_ref, ofx_ref, obx_ref, s_ref, intra_ref, qd_ref, kd_ref):
    cs = RET_CHUNK
    hp = SCAN_HEADS
    head0 = pl.program_id(1) * hp
    row = lax.broadcasted_iota(jnp.int32, (cs, cs), 0).astype(F32)
    col = lax.broadcasted_iota(jnp.int32, (cs, cs), 1).astype(F32)
    pos = lax.broadcasted_iota(jnp.int32, (cs, HEAD_DIM), 0).astype(F32)
    k_scale = HEAD_DIM ** -0.5

    for hh in range(hp):
        for d in (0, 1):
            c = 2 * hh + d
            lg = lg_ref[d, head0 + hh]
            rel = (col - row) if d else (row - col)
            intra_ref[c] = jnp.where(rel >= 0, jnp.exp(lg * jnp.maximum(rel, 0.0)), 0.0)
            p = (cs - 1.0 - pos) if d else pos
            qd_ref[c] = jnp.exp(lg * (p + 1.0))
            kd_ref[c] = jnp.exp(lg * (cs - 1.0 - p))
    s_ref[...] = jnp.zeros_like(s_ref)

    for (q_ref, k_ref, v_ref, of_ref, ob_ref, rope) in (
            (qc_ref, kc_ref, vc_ref, ofc_ref, obc_ref, False),
            (qx_ref, kx_ref, vx_ref, ofx_ref, obx_ref, True)):
        n_chunks = q_ref.shape[0] // cs

        def step(i, carry, q_ref=q_ref, k_ref=k_ref, v_ref=v_ref, of_ref=of_ref, ob_ref=ob_ref,
                 rope=rope, n_chunks=n_chunks):
            for d in (0, 1):
                start = pl.multiple_of(((n_chunks - 1 - i) if d else i) * cs, cs)
                o_ref = ob_ref if d else of_ref
                if rope:
                    cos = cos_ref[pl.ds(start, cs), :]
                    sin = sin_ref[pl.ds(start, cs), :]
                for hh in range(hp):
                    c = 2 * hh + d
                    sl = _head_slice(hh)
                    q = q_ref[pl.ds(start, cs), sl].astype(F32)
                    k = k_ref[pl.ds(start, cs), sl].astype(F32) * k_scale
                    if rope:
                        q = q * cos + pltpu.roll(q, HEAD_DIM // 2, 1) * sin
                        k = k * cos + pltpu.roll(k, HEAD_DIM // 2, 1) * sin
                    v = v_ref[pl.ds(start, cs), sl]
                    state = s_ref[c]
                    c_dec = jnp.exp(jnp.full((1, HEAD_DIM), lg_ref[d, head0 + hh] * cs, F32))
                    att = _mm_nt(q, k) * intra_ref[c]
                    o_ref[pl.ds(start, cs), sl] = _mm(att, v) + _mm(q * qd_ref[c], state)
                    s_ref[c] = state * c_dec + _mm_tn(k * kd_ref[c], v)
            return carry

        lax.fori_loop(0, n_chunks, step, 0)

    _finish_scan(ofc_ref, obc_ref, gc_ref, yc_ref, cs)
    _finish_scan(ofx_ref, obx_ref, gx_ref, yx_ref, cs)


def _retention(proj_c, proj_x, log_gamma, cos2, sin2):
    b, tc, _ = proj_c.shape
    tx = proj_x.shape[1]
    h, hp = RET_HEADS, SCAN_HEADS
    w = hp * HEAD_DIM

    def col(t, group):
        return pl.BlockSpec((None, t, w), lambda bi, hi, lg: (bi, 0, group * (h // hp) + hi))

    n_chain = 2 * hp
    grid_spec = pltpu.PrefetchScalarGridSpec(
        num_scalar_prefetch=1,
        grid=(b, h // hp),
        in_specs=[col(tc, 0), col(tc, 1), col(tc, 2), col(tc, 3),
                  col(tx, 0), col(tx, 1), col(tx, 2), col(tx, 3),
                  pl.BlockSpec((tx, HEAD_DIM), lambda bi, hi, lg: (0, 0)),
                  pl.BlockSpec((tx, HEAD_DIM), lambda bi, hi, lg: (0, 0))],
        out_specs=[pl.BlockSpec((None, tc, w), lambda bi, hi, lg: (bi, 0, hi)),
                   pl.BlockSpec((None, tx, w), lambda bi, hi, lg: (bi, 0, hi))],
        scratch_shapes=[pltpu.VMEM((tc, w), F32), pltpu.VMEM((tc, w), F32),
                        pltpu.VMEM((tx, w), F32), pltpu.VMEM((tx, w), F32),
                        pltpu.VMEM((n_chain, HEAD_DIM, HEAD_DIM), F32),
                        pltpu.VMEM((n_chain, RET_CHUNK, RET_CHUNK), F32),
                        pltpu.VMEM((n_chain, RET_CHUNK, HEAD_DIM), F32),
                        pltpu.VMEM((n_chain, RET_CHUNK, HEAD_DIM), F32)])
    return pl.pallas_call(
        _retention_kernel,
        grid_spec=grid_spec,
        out_shape=[jax.ShapeDtypeStruct((b, tc, h * HEAD_DIM), BF16),
                   jax.ShapeDtypeStruct((b, tx, h * HEAD_DIM), BF16)],
        compiler_params=_cparams("parallel", "parallel"),
        name="retention",
    )(log_gamma, proj_c, proj_c, proj_c, proj_c, proj_x, proj_x, proj_x, proj_x, cos2, sin2)


def _hgrn_levels(cs):
    return [cs >> (i + 1) for i in range(int(math.log2(cs)))]


def _hgrn_constants(cs):
    t = np.arange(cs)[:, None]
    s = np.arange(cs)[None, :]
    tri = np.stack([(s <= t), (s >= t)]).astype(np.float32)
    masks = []
    for reverse in (False, True):
        lv = []
        for h in _hgrn_levels(cs):
            same = (t // (2 * h)) == (s // (2 * h))
            t_hi = (t % (2 * h)) >= h
            s_hi = (s % (2 * h)) >= h
            lv.append(same & (~t_hi & s_hi if reverse else t_hi & ~s_hi))
        lv.append(t == s)
        masks.append(np.stack(lv))
    return jnp.asarray(tri, BF16), jnp.asarray(np.stack(masks).astype(np.float32), BF16)


def _split3(x):
    hi = x.astype(BF16)
    r = x - hi.astype(F32)
    mid = r.astype(BF16)
    lo = (r - mid.astype(F32)).astype(BF16)
    return hi, mid, lo


def _neg_abs(x):
    bits = lax.bitcast_convert_type(x, jnp.uint32) | jnp.uint32(0x80000000)
    return lax.bitcast_convert_type(bits, F32)


def _level_reference(b, h, reverse):
    cs = b.shape[0]
    ref_local = h if reverse else h - 1
    if 2 * h >= 8:
        b3 = b.reshape(cs // (2 * h), 2 * h, b.shape[1])
        r = b3[:, ref_local:ref_local + 1, :]
        return jnp.broadcast_to(r, b3.shape).reshape(b.shape)
    local = lax.broadcasted_iota(jnp.int32, b.shape, 0) % (2 * h)
    out = b
    for m in range(2 * h):
        if m != ref_local:
            out = jnp.where(local == m, pltpu.roll(b, (m - ref_local) % cs, 0), out)
    return out


def _hgrn_kernel(lb_ref, tri_ref, mask_ref,
                 qc_ref, vc_ref, gc_ref, zfc_ref, zbc_ref,
                 qx_ref, vx_ref, gx_ref, zfx_ref, zbx_ref,
                 yc_ref, yx_ref, ofc_ref, obc_ref, ofx_ref, obx_ref, st_ref):
    cs = HGRN_CHUNK
    hp = SCAN_HEADS
    levels = _hgrn_levels(cs)
    st_ref[...] = jnp.zeros_like(st_ref)

    for (q_ref, v_ref, zf_ref, zb_ref, of_ref, ob_ref) in (
            (qc_ref, vc_ref, zfc_ref, zbc_ref, ofc_ref, obc_ref),
            (qx_ref, vx_ref, zfx_ref, zbx_ref, ofx_ref, obx_ref)):
        n_chunks = q_ref.shape[0] // cs

        def step(i, carry, q_ref=q_ref, v_ref=v_ref, zf_ref=zf_ref, zb_ref=zb_ref, of_ref=of_ref,
                 ob_ref=ob_ref, n_chunks=n_chunks):
            for d in (0, 1):
                reverse = d == 1
                start = pl.multiple_of(((n_chunks - 1 - i) if d else i) * cs, cs)
                z_ref = zb_ref if d else zf_ref
                o_ref = ob_ref if d else of_ref
                for hh in range(hp):
                    c = 2 * hh + d
                    sl = _head_slice(hh)
                    lb = lb_ref[hh]
                    q = _silu(q_ref[pl.ds(start, cs), sl].astype(F32))
                    z = z_ref[pl.ds(start, cs), sl]
                    v = v_ref[pl.ds(start, cs), sl]
                    f = lb + (1.0 - lb) * (1.0 / (1.0 + jnp.exp(-z)))
                    k = 1.0 - f
                    tri = tri_ref[d]
                    hi, mid, lo = _split3(jnp.log2(f))
                    bsum = (jnp.dot(tri, hi, preferred_element_type=F32)
                            + jnp.dot(tri, mid, preferred_element_type=F32)
                            + jnp.dot(tri, lo, preferred_element_type=F32))
                    qb, kb = q.astype(BF16), k.astype(BF16)
                    att = _mm_nt(qb, kb).astype(BF16) * mask_ref[d, len(levels)]
                    for li, h in enumerate(levels):
                        w = jnp.exp2(_neg_abs(bsum - _level_reference(bsum, h, reverse))).astype(BF16)
                        att += _mm_nt(qb * w, kb * w).astype(BF16) * mask_ref[d, li]
                    state_t = st_ref[c]
                    o_ref[pl.ds(start, cs), sl] = _mm(att, v) + _mm_nt(q * jnp.exp2(bsum), state_t)
                    b_end = bsum[0:1, :] if reverse else bsum[cs - 1:cs, :]
                    st_ref[c] = state_t * jnp.exp2(b_end) + _mm_tn(v, k * jnp.exp2(b_end - bsum))
            return carry

        lax.fori_loop(0, n_chunks, step, 0)

    _finish_scan(ofc_ref, obc_ref, gc_ref, yc_ref, cs)
    _finish_scan(ofx_ref, obx_ref, gx_ref, yx_ref, cs)


def _hgrn(proj_c, proj_x, z_c, z_x, lb, first):
    b, tc, _ = proj_c.shape
    tx = proj_x.shape[1]
    h, hp = HGRN_HEADS, SCAN_HEADS
    w = hp * HEAD_DIM
    tri, masks = _hgrn_constants(HGRN_CHUNK)

    def col(t, group):
        return pl.BlockSpec((None, t, w), lambda bi, hi: (bi, 0, (first + group * h) // hp + hi))

    def zcol(t, group):
        return pl.BlockSpec((None, t, w), lambda bi, hi: (bi, 0, group * (h // hp) + hi))

    ins = [col(tc, 0), col(tc, 1), col(tc, 2), zcol(tc, 0), zcol(tc, 1),
           col(tx, 0), col(tx, 1), col(tx, 2), zcol(tx, 0), zcol(tx, 1)]
    return pl.pallas_call(
        _hgrn_kernel,
        grid=(b, h // hp),
        in_specs=[pl.BlockSpec((hp, 1, HEAD_DIM), lambda bi, hi: (hi, 0, 0)),
                  pl.BlockSpec(tri.shape, lambda bi, hi: (0, 0, 0)),
                  pl.BlockSpec(masks.shape, lambda bi, hi: (0, 0, 0, 0))] + ins,
        out_specs=[pl.BlockSpec((None, tc, w), lambda bi, hi: (bi, 0, hi)),
                   pl.BlockSpec((None, tx, w), lambda bi, hi: (bi, 0, hi))],
        out_shape=[jax.ShapeDtypeStruct((b, tc, h * HEAD_DIM), BF16),
                   jax.ShapeDtypeStruct((b, tx, h * HEAD_DIM), BF16)],
        scratch_shapes=[pltpu.VMEM((tc, w), F32), pltpu.VMEM((tc, w), F32),
                        pltpu.VMEM((tx, w), F32), pltpu.VMEM((tx, w), F32),
                        pltpu.VMEM((2 * hp, HEAD_DIM, HEAD_DIM), F32)],
        compiler_params=_cparams("parallel", "parallel"),
        name="hgrn2",
    )(lb.reshape(h, 1, HEAD_DIM), tri, masks, proj_c, proj_c, proj_c, z_c, z_c,
      proj_x, proj_x, proj_x, z_x, z_x)


MLA_QW = 2 * LANE
MLA_SCORE_SCALE = (MLA_NOPE + MLA_ROPE) ** -0.5 * math.log2(math.e)
MLA_DOWN_PAD = MLA_Q_RANK + MLA_KV_RANK + LANE


def _mla_proj_kernel(x_ref, mod_ref, g_ref, wd_ref, gq_ref, gkv_ref, wq_ref, wkv_ref, cos_ref, sin_ref,
                     *out_refs, rope, want_q):
    h = _modnorm(x_ref[...], g_ref[...], mod_ref, 0, 1).astype(BF16)
    down = jnp.dot(h, wd_ref[...], preferred_element_type=F32)
    if rope:
        cos = cos_ref[...]
        sin = sin_ref[...]

    def rot(v):
        return v * cos + pltpu.roll(v, LANE // 2, 1) * sin if rope else v

    if want_q:
        q_ref, kn_ref, kr_ref, v_ref = out_refs
        qn = (_rms(down[:, :MLA_Q_RANK]) * gq_ref[...]).astype(BF16)
        q = jnp.dot(qn, wq_ref[...], preferred_element_type=F32) * MLA_SCORE_SCALE
        for hd in range(MLA_HEADS):
            lo = hd * MLA_QW
            q_ref[:, lo:lo + LANE] = q[:, lo:lo + LANE].astype(q_ref.dtype)
            q_ref[:, lo + LANE:lo + 2 * LANE] = rot(q[:, lo + LANE:lo + 2 * LANE]).astype(q_ref.dtype)
    else:
        kn_ref, kr_ref, v_ref = out_refs
    kvn = (_rms(down[:, MLA_Q_RANK:MLA_Q_RANK + MLA_KV_RANK]) * gkv_ref[...]).astype(BF16)
    kv = jnp.dot(kvn, wkv_ref[...], preferred_element_type=F32)
    nk = MLA_HEADS * MLA_NOPE
    kn_ref[...] = kv[:, :nk].astype(kn_ref.dtype)
    v_ref[...] = kv[:, nk:].astype(v_ref.dtype)
    kr_ref[...] = rot(down[:, MLA_Q_RANK + MLA_KV_RANK:]).astype(kr_ref.dtype)


def _mla_proj(x, mod, g, wd, gq, gkv, wq, wkv, cos, sin, rope, want_q, tm=512):
    b, t, d = x.shape
    tm = _pick(t, tm)
    const = lambda bi, i: (0, 0)
    tok = lambda bi, i: (bi, i, 0)
    nk, nv = MLA_HEADS * MLA_NOPE, MLA_HEADS * MLA_V
    out_shape = [jax.ShapeDtypeStruct((b, t, nk), BF16), jax.ShapeDtypeStruct((b, t, LANE), BF16),
                 jax.ShapeDtypeStruct((b, t, nv), BF16)]
    out_specs = [pl.BlockSpec((None, tm, nk), tok), pl.BlockSpec((None, tm, LANE), tok),
                 pl.BlockSpec((None, tm, nv), tok)]
    if want_q:
        out_shape = [jax.ShapeDtypeStruct((b, t, MLA_HEADS * MLA_QW), BF16)] + out_shape
        out_specs = [pl.BlockSpec((None, tm, MLA_HEADS * MLA_QW), tok)] + out_specs
    return pl.pallas_call(
        functools.partial(_mla_proj_kernel, rope=rope, want_q=want_q),
        grid=(b, t // tm),
        in_specs=[pl.BlockSpec((None, tm, d), tok), _mod_spec(mod), pl.BlockSpec((1, d), const),
                  pl.BlockSpec(wd.shape, const), pl.BlockSpec((1, MLA_Q_RANK), const),
                  pl.BlockSpec((1, MLA_KV_RANK), const), pl.BlockSpec(wq.shape, const),
                  pl.BlockSpec(wkv.shape, const),
                  pl.BlockSpec((tm, LANE), lambda bi, i: (i, 0)),
                  pl.BlockSpec((tm, LANE), lambda bi, i: (i, 0))],
        out_specs=out_specs,
        out_shape=out_shape,
        compiler_params=_cparams("parallel", "parallel"),
        name="mla_proj",
    )(x, mod, g.reshape(1, d), wd, gq.reshape(1, -1), gkv.reshape(1, -1), wq, wkv, cos, sin)


def _lane_fold(x, op):
    out = x[:, :LANE]
    for j in range(1, x.shape[1] // LANE):
        out = op(out, x[:, j * LANE:(j + 1) * LANE])
    return out


def _attention_kernel(q_ref, knc_ref, krc_ref, vc_ref, knx_ref, krx_ref, vx_ref, o_ref,
                      kc_ref, kx_ref, vc2_ref, vx2_ref, sc0_ref, sc1_ref, sx0_ref, sx1_ref, m0_ref, m1_ref,
                      *, tk):
    step = pl.program_id(2)

    @pl.when(step == 0)
    def _():
        kc_ref[:, :LANE] = knc_ref[...]
        kc_ref[:, LANE:] = krc_ref[...]
        kx_ref[:, :LANE] = knx_ref[...]
        kx_ref[:, LANE:] = krx_ref[...]
        vc2_ref[:, :LANE] = vc_ref[...]
        vx2_ref[:, :LANE] = vx_ref[...]
        for ref in (vc2_ref, vx2_ref):
            lane = lax.broadcasted_iota(jnp.int32, (ref.shape[0], LANE), 1)
            ref[:, LANE:] = jnp.where(lane == 0, 1.0, 0.0).astype(ref.dtype)
        sc1_ref[...] = jnp.zeros(sc1_ref.shape, F32)
        sx1_ref[...] = jnp.zeros(sx1_ref.shape, F32)
        m1_ref[...] = jnp.zeros(m1_ref.shape, F32)

    n_blocks = kx_ref.shape[0] // tk
    nt = (((1,), (1,)), ((), ()))

    def body(sc_cur, sx_cur, m_cur, sc_prev, sx_prev, m_prev):
        q = q_ref[...]
        m = jnp.max(m_prev[...], axis=-1, keepdims=True)
        s = lax.dot_general(q, kc_ref[...], nt, preferred_element_type=F32)
        sc_cur[...] = s
        m_part = _lane_fold(s, jnp.maximum)
        p = jnp.exp2(sc_prev[...] - m)
        acc = jnp.dot(p.astype(BF16), vc2_ref[...], preferred_element_type=F32)
        for i in range(n_blocks):
            s = lax.dot_general(q, kx_ref[i * tk:(i + 1) * tk, :], nt, preferred_element_type=F32)
            sx_cur[i] = s
            m_part = jnp.maximum(m_part, _lane_fold(s, jnp.maximum))
            p = jnp.exp2(sx_prev[i] - m)
            acc = acc + jnp.dot(p.astype(BF16), vx2_ref[i * tk:(i + 1) * tk, :],
                                preferred_element_type=F32)
        m_cur[...] = m_part
        o_ref[...] = (acc[:, :MLA_V] / acc[:, MLA_V:MLA_V + 1]).astype(o_ref.dtype)

    @pl.when(step % 2 == 0)
    def _():
        body(sc0_ref, sx0_ref, m0_ref, sc1_ref, sx1_ref, m1_ref)

    @pl.when(step % 2 == 1)
    def _():
        body(sc1_ref, sx1_ref, m1_ref, sc0_ref, sx0_ref, m0_ref)


def _attention(q, kn_c, kr_c, v_c, kn_x, kr_x, v_x, tq=256, tk=256):
    b, t, _ = q.shape
    tc = kn_c.shape[1]
    tq = _pick(t, tq)
    tk = _pick(t, tk)
    nq = t // tq
    head_c = lambda bi, hi, i: (bi, 0, hi)
    shared = lambda bi, hi, i: (bi, 0, 0)
    return pl.pallas_call(
        functools.partial(_attention_kernel, tk=tk),
        grid=(b, MLA_HEADS, nq + 1),
        in_specs=[pl.BlockSpec((None, tq, MLA_QW), lambda bi, hi, i: (bi, jnp.minimum(i, nq - 1), hi)),
                  pl.BlockSpec((None, tc, LANE), head_c), pl.BlockSpec((None, tc, LANE), shared),
                  pl.BlockSpec((None, tc, LANE), head_c),
                  pl.BlockSpec((None, t, LANE), head_c), pl.BlockSpec((None, t, LANE), shared),
                  pl.BlockSpec((None, t, LANE), head_c)],
        out_specs=pl.BlockSpec((None, tq, MLA_V), lambda bi, hi, i: (bi, jnp.maximum(i - 1, 0), hi)),
        out_shape=jax.ShapeDtypeStruct((b, t, MLA_HEADS * MLA_V), BF16),
        scratch_shapes=[pltpu.VMEM((tc, MLA_QW), BF16), pltpu.VMEM((t, MLA_QW), BF16),
                        pltpu.VMEM((tc, 2 * MLA_V), BF16), pltpu.VMEM((t, 2 * MLA_V), BF16),
                        pltpu.VMEM((tq, tc), F32), pltpu.VMEM((tq, tc), F32),
                        pltpu.VMEM((t // tk, tq, tk), F32), pltpu.VMEM((t // tk, tq, tk), F32),
                        pltpu.VMEM((tq, LANE), F32), pltpu.VMEM((tq, LANE), F32)],
        compiler_params=_cparams("parallel", "parallel", "arbitrary"),
        name="mla_attention",
    )(q, kn_c, kr_c, v_c, kn_x, kr_x, v_x)


def _router_kernel(x_ref, mod_ref, g_ref, wr_ref, h_ref, w_ref, e_ref):
    h = _modnorm(x_ref[...], g_ref[...], mod_ref, 3, 4)
    h_ref[...] = _pack_halves(h)
    logits = jnp.dot(h, wr_ref[...], preferred_element_type=F32, precision=lax.Precision.HIGHEST)
    lane = lax.broadcasted_iota(jnp.int32, logits.shape, 1)
    neg = jnp.float32(-jnp.inf)
    lg = jnp.where(lane < N_EXPERTS, logits, neg)
    m1 = jnp.max(lg, axis=-1, keepdims=True)
    i1 = jnp.min(jnp.where(lg == m1, lane, LANE), axis=-1, keepdims=True)
    lg2 = jnp.where(lane == i1, neg, lg)
    m2 = jnp.max(lg2, axis=-1, keepdims=True)
    i2 = jnp.min(jnp.where(lg2 == m2, lane, LANE), axis=-1, keepdims=True)
    e2 = jnp.exp(m2 - m1)
    w1 = 1.0 / (1.0 + e2)
    w2 = e2 / (1.0 + e2)
    w_ref[...] = jnp.where(lane == 0, w1, jnp.where(lane == 1, w2, 0.0))
    e_ref[...] = jnp.where(lane == 0, i1, jnp.where(lane == 1, i2, 0))


def _router(x, mod, g, router_pad, tm=512):
    b, t, d = x.shape
    tm = _pick(t, tm)
    tok = lambda bi, i: (bi, i, 0)
    const = lambda bi, i: (0, 0)
    return pl.pallas_call(
        _router_kernel,
        grid=(b, t // tm),
        in_specs=[pl.BlockSpec((None, tm, d), tok), _mod_spec(mod), pl.BlockSpec((1, d), const),
                  pl.BlockSpec((d, LANE), const)],
        out_specs=[pl.BlockSpec((None, tm, d // 2), tok), pl.BlockSpec((None, tm, LANE), tok),
                   pl.BlockSpec((None, tm, LANE), tok)],
        out_shape=[jax.ShapeDtypeStruct((b, t, d // 2), jnp.int32), jax.ShapeDtypeStruct((b, t, LANE), F32),
                   jax.ShapeDtypeStruct((b, t, LANE), jnp.int32)],
        compiler_params=_cparams("parallel", "parallel"),
        name="moe_router",
    )(x, mod, g.reshape(1, d), router_pad)


SC_GATHER_ROWS = 64


def _gather_rows(table, idx):
    info = plsc.get_sparse_core_info()
    n_workers = info.num_cores * info.num_subcores
    n, width = idx.shape[0], table.shape[1]
    chunk = SC_GATHER_ROWS
    assert n % (n_workers * chunk) == 0
    n_chunks = n // (n_workers * chunk)
    mesh = plsc.VectorSubcoreMesh(core_axis_name="c", subcore_axis_name="s")

    def body(table_hbm, idx_hbm, out_hbm, idx_v, rows_v, sem):
        wid = lax.axis_index("s") * info.num_cores + lax.axis_index("c")
        pltpu.sync_copy(idx_hbm.at[wid], idx_v)

        @pl.loop(0, n_chunks)
        def _(j):
            pltpu.async_copy(table_hbm.at[idx_v.at[j]], rows_v, sem).wait()
            pltpu.sync_copy(rows_v, out_hbm.at[pl.ds((wid * n_chunks + j) * chunk, chunk)])

    return pl.kernel(
        body,
        out_type=jax.ShapeDtypeStruct((n, width), table.dtype),
        mesh=mesh,
        scratch_types=[pltpu.VMEM((n_chunks, chunk), jnp.int32), pltpu.VMEM((chunk, width), table.dtype),
                       pltpu.SemaphoreType.DMA],
        name="sc_gather_rows",
    )(table, idx.reshape(n_workers, n_chunks, chunk))


def _scatter_rows_twice(table, idx0, idx1, n_out):
    info = plsc.get_sparse_core_info()
    n_workers = info.num_cores * info.num_subcores
    n, width = table.shape
    chunk = SC_GATHER_ROWS
    assert n % (n_workers * chunk) == 0
    n_chunks = n // (n_workers * chunk)
    mesh = plsc.VectorSubcoreMesh(core_axis_name="c", subcore_axis_name="s")

    def body(table_hbm, i0_hbm, i1_hbm, out_hbm, i0_v, i1_v, rows_v):
        wid = lax.axis_index("s") * info.num_cores + lax.axis_index("c")
        pltpu.sync_copy(i0_hbm.at[wid], i0_v)
        pltpu.sync_copy(i1_hbm.at[wid], i1_v)

        @pl.loop(0, n_chunks)
        def _(j):
            pltpu.sync_copy(table_hbm.at[pl.ds((wid * n_chunks + j) * chunk, chunk)], rows_v)
            pltpu.sync_copy(rows_v, out_hbm.at[i0_v.at[j]])
            pltpu.sync_copy(rows_v, out_hbm.at[i1_v.at[j]])

    shape3 = (n_workers, n_chunks, chunk)
    return pl.kernel(
        body,
        out_type=jax.ShapeDtypeStruct((n_out, width), table.dtype),
        mesh=mesh,
        scratch_types=[pltpu.VMEM((n_chunks, chunk), jnp.int32), pltpu.VMEM((n_chunks, chunk), jnp.int32),
                       pltpu.VMEM((chunk, width), table.dtype)],
        name="sc_scatter_rows",
    )(table, idx0.reshape(shape3), idx1.reshape(shape3))


def _pack_halves(x):
    k = x.shape[1] // 2
    bits = lambda v: lax.bitcast_convert_type(v.astype(BF16).astype(F32), jnp.uint32)
    word = lax.shift_right_logical(bits(x[:, :k]), jnp.uint32(16)) | bits(x[:, k:])
    return lax.bitcast_convert_type(word, jnp.int32)


def _unpack_halves(w):
    u = lax.bitcast_convert_type(w, jnp.uint32)
    lo = lax.bitcast_convert_type(lax.shift_left(u, jnp.uint32(16)), F32)
    hi = lax.bitcast_convert_type(u & jnp.uint32(0xFFFF0000), F32)
    return jnp.concatenate([lo, hi], axis=-1)


def _expert_ffn_kernel(te_ref, tv_ref, x_ref, wa_ref, wb_ref, wo_ref, o_ref, acc_ref):
    j = pl.program_id(0)
    f = pl.program_id(1)
    last = pl.num_programs(1) - 1
    valid = tv_ref[j] != 0

    @pl.when(valid)
    def _():
        @pl.when(f == 0)
        def _():
            acc_ref[...] = jnp.zeros_like(acc_ref)

        h = _unpack_halves(x_ref[...]).astype(BF16)
        a = jnp.dot(h, wa_ref[...], preferred_element_type=F32)
        bb = jnp.dot(h, wb_ref[...], preferred_element_type=F32)
        acc_ref[...] += jnp.dot((_silu(a) * bb).astype(BF16), wo_ref[...], preferred_element_type=F32)

        @pl.when(f == last)
        def _():
            o_ref[...] = _pack_halves(acc_ref[...])

    @pl.when(jnp.logical_not(valid) & (f == last))
    def _():
        o_ref[...] = jnp.zeros_like(o_ref)


def _expert_ffn(xs, tile_expert, tile_valid, w_in, w_out, tm, tf=1408):
    p = xs.shape[0]
    n_exp, fdim, d = w_out.shape
    tf = _pick(fdim, tf)
    nf = fdim // tf

    def f_eff(j, f, tv):
        return jnp.where(tv[j] != 0, f, nf - 1)

    grid_spec = pltpu.PrefetchScalarGridSpec(
        num_scalar_prefetch=2,
        grid=(p // tm, nf),
        in_specs=[pl.BlockSpec((tm, d // 2), lambda j, f, te, tv: (j, 0)),
                  pl.BlockSpec((None, d, tf), lambda j, f, te, tv: (te[j], 0, f_eff(j, f, tv))),
                  pl.BlockSpec((None, d, tf), lambda j, f, te, tv: (te[j], 0, f_eff(j, f, tv) + nf)),
                  pl.BlockSpec((None, tf, d), lambda j, f, te, tv: (te[j], f_eff(j, f, tv), 0))],
        out_specs=pl.BlockSpec((tm, d // 2), lambda j, f, te, tv: (j, 0)),
        scratch_shapes=[pltpu.VMEM((tm, d), F32)])
    return pl.pallas_call(
        _expert_ffn_kernel,
        grid_spec=grid_spec,
        out_shape=jax.ShapeDtypeStruct((p, d // 2), jnp.int32),
        compiler_params=_cparams("parallel", "arbitrary"),
        name="moe_expert_ffn",
    )(tile_expert, tile_valid, xs, w_in, w_in, w_out)


def _moe_combine_kernel(x_ref, y_ref, w_ref, mod_ref, g_ref, o_ref):
    k = x_ref.shape[-1] // 2
    y = w_ref[:, 0:1] * _unpack_halves(y_ref[:, :k]) + w_ref[:, 1:2] * _unpack_halves(y_ref[:, k:])
    o_ref[...] = x_ref[...] + mod_ref[5:6, :] * (_rms(y) * g_ref[...])


def _moe_combine(x, y2, w, mod, g, tm=512):
    b, t, d = x.shape
    tm = _pick(t, tm)
    tok = lambda bi, i: (bi, i, 0)
    return pl.pallas_call(
        _moe_combine_kernel,
        grid=(b, t // tm),
        in_specs=[pl.BlockSpec((None, tm, d), tok), pl.BlockSpec((None, tm, d), tok),
                  pl.BlockSpec((None, tm, LANE), tok), _mod_spec(mod),
                  pl.BlockSpec((1, d), lambda bi, i: (0, 0))],
        out_specs=pl.BlockSpec((None, tm, d), tok),
        out_shape=jax.ShapeDtypeStruct((b, t, d), F32),
        compiler_params=_cparams("parallel", "parallel"),
        name="moe_combine",
    )(x, y2, w, mod, g.reshape(1, d))


def _moe(x, mod, g_in, g_out, router_pad, w_in, w_out, tm=512):
    b, t, d = x.shape
    n_exp = w_out.shape[0]
    n = b * t
    h, w, e = _router(x, mod, g_in, router_pad)
    e_flat = e[..., :2].reshape(2 * n)
    onehot = (e_flat[:, None] == jnp.arange(n_exp, dtype=jnp.int32)).astype(jnp.int32)
    csum = jnp.cumsum(onehot, axis=0)
    rank = jnp.sum((csum - onehot) * onehot, axis=-1)
    counts = csum[-1]
    ends = jnp.cumsum((counts + tm - 1) // tm * tm)
    dest = (ends - (counts + tm - 1) // tm * tm)[e_flat] + rank
    p = 2 * n + n_exp * tm
    tile_start = jnp.arange(p // tm, dtype=jnp.int32) * tm
    tile_expert = jnp.minimum(jnp.searchsorted(ends, tile_start, side="right"), n_exp - 1).astype(jnp.int32)
    tile_valid = (tile_start < ends[-1]).astype(jnp.int32)

    dest2 = dest.reshape(n, 2)
    xs = _scatter_rows_twice(h.reshape(n, d // 2), dest2[:, 0], dest2[:, 1], p)
    ys = _expert_ffn(xs, tile_expert, tile_valid, w_in, w_out, tm)
    y2 = _gather_rows(ys, dest)
    return _moe_combine(x, y2.reshape(b, t, d), w, mod, g_out)


def _axial_angles(n_tokens, dim):
    t = jnp.arange(n_tokens)
    rows = (t // GRID_W).astype(F32)
    cols = (t % GRID_W).astype(F32)
    n_freq = dim // 4
    inv = ROPE_BASE ** (-jnp.arange(n_freq, dtype=F32) / n_freq)
    return jnp.concatenate([rows[:, None] * inv, cols[:, None] * inv], axis=-1)


def _rope_tables(n_tokens, dim):
    ang = _axial_angles(n_tokens, dim)
    cos, sin = jnp.cos(ang), jnp.sin(ang)
    if dim == LANE:
        return jnp.concatenate([cos, cos], -1), jnp.concatenate([-sin, sin], -1)
    z = jnp.zeros_like(cos)
    return jnp.concatenate([cos, z, cos, z], -1), jnp.concatenate([-sin, z, sin, z], -1)


def _spread_rope_cols(w):
    half = MLA_ROPE // 2
    z = jnp.zeros(w.shape[:-1] + (half,), w.dtype)
    return jnp.concatenate([w[..., :half], z, w[..., half:], z], axis=-1)


def kernel(x, c, ctx, c_ctx, mod_w, mod_b, norm_g, mix_in_w, ret_decay_logit, hgrn_lb_logit, mix_out_w,
           ffn_in_w, ffn_out_w, mla_down_w, mla_q_norm_g, mla_kv_norm_g, mla_uq_w, mla_ukv_w, mla_out_w,
           router_w, moe_in_w, moe_out_w):
    batch, seq, d = x.shape
    assert mod_w.shape[0] == 2, "two layers: retention/HGRN2 then MLA/MoE"

    rows = -(-(batch + 1) // 8) * 8
    cond = jnp.zeros((rows, d), F32).at[:batch].set(c).at[batch].set(c_ctx)
    mod = _modulation(cond, mod_w, mod_b)
    mod_x = [mod[l, :batch].reshape(batch, 6, d) for l in range(2)]
    mod_c = [mod[l, batch:batch + 1].reshape(1, 6, d) for l in range(2)]

    g = norm_g[0]
    ret_w = RET_HEADS * HEAD_DIM
    z_lo, z_hi = 5 * ret_w, 7 * ret_w
    w_in = mix_in_w[0].astype(BF16)
    w_main = jnp.concatenate([w_in[:, :z_lo], w_in[:, z_hi:]], axis=1)
    w_z = w_in[:, z_lo:z_hi]
    proj_x = _norm_matmul(x, mod_x[0], g[0], w_main, 0, 1, BF16)
    proj_c = _norm_matmul(ctx, mod_c[0], g[0], w_main, 0, 1, BF16)
    z_x = _norm_matmul(x, mod_x[0], g[0], w_z, 0, 1, F32)
    z_c = _norm_matmul(ctx, mod_c[0], g[0], w_z, 0, 1, F32)
    cos2, sin2 = _rope_tables(seq, HEAD_DIM)
    log_gamma = jax.nn.log_sigmoid(ret_decay_logit[0].astype(F32))
    lb = jnp.cumsum(jax.nn.softmax(hgrn_lb_logit.astype(F32), axis=0), axis=0)[0]
    ret_c, ret_x = _retention(proj_c, proj_x, log_gamma, cos2, sin2)
    hg_c, hg_x = _hgrn(proj_c, proj_x, z_c, z_x, lb, first=4 * RET_HEADS)
    w_out = mix_out_w[0].astype(BF16)
    w_outs = [w_out[:ret_w], w_out[ret_w:]]
    x = _proj_residual([ret_x, hg_x], w_outs, x, mod_x[0], g[1], 2)
    ctx = _proj_residual([ret_c, hg_c], w_outs, ctx, mod_c[0], g[1], 2)
    f_in, f_out = ffn_in_w[0].astype(BF16), ffn_out_w[0].astype(BF16)
    x = _ffn(x, mod_x[0], g[2], g[3], f_in, f_out)
    ctx = _ffn(ctx, mod_c[0], g[2], g[3], f_in, f_out)

    g = norm_g[1]
    dq = MLA_NOPE + MLA_ROPE
    wd = mla_down_w[0]
    wd = jnp.concatenate([wd[:, :MLA_Q_RANK + MLA_KV_RANK],
                          _spread_rope_cols(wd[:, MLA_Q_RANK + MLA_KV_RANK:])], axis=-1).astype(BF16)
    wq = mla_uq_w[0].reshape(MLA_Q_RANK, MLA_HEADS, dq)
    wq = jnp.concatenate([wq[..., :MLA_NOPE], _spread_rope_cols(wq[..., MLA_NOPE:])], axis=-1)
    wq = wq.reshape(MLA_Q_RANK, MLA_HEADS * MLA_QW).astype(BF16)
    wkv = mla_ukv_w[0].reshape(MLA_KV_RANK, MLA_HEADS, MLA_NOPE + MLA_V)
    wkv = jnp.concatenate([wkv[..., :MLA_NOPE].reshape(MLA_KV_RANK, -1),
                           wkv[..., MLA_NOPE:].reshape(MLA_KV_RANK, -1)], axis=-1).astype(BF16)
    cos_m, sin_m = _rope_tables(seq, MLA_ROPE)
    q, kn_x, kr_x, v_x = _mla_proj(x, mod_x[1], g[0], wd, mla_q_norm_g[0], mla_kv_norm_g[0], wq, wkv,
                                   cos_m, sin_m, rope=True, want_q=True)
    tc = ctx.shape[1]
    kn_c, kr_c, v_c = _mla_proj(ctx, mod_c[1], g[0], wd, mla_q_norm_g[0], mla_kv_norm_g[0], wq, wkv,
                                cos_m[:tc], sin_m[:tc], rope=False, want_q=False)
    att = _attention(q, kn_c, kr_c, v_c, kn_x, kr_x, v_x)
    x = _proj_residual([att], [mla_out_w[0].astype(BF16)], x, mod_x[1], g[1], 2)
    router_pad = jnp.zeros((d, LANE), F32).at[:, :N_EXPERTS].set(router_w[0])
    x = _moe(x, mod_x[1], g[2], g[3], router_pad, moe_in_w[0].astype(BF16), moe_out_w[0].astype(BF16))
    return x
```

```python
import functools
import math

import jax
import jax.numpy as jnp
import numpy as np
from jax import lax
from jax.experimental import pallas as pl
from jax.experimental.pallas import tpu as pltpu
from jax.experimental.pallas import tpu_sc as plsc

F32 = jnp.float32
BF16 = jnp.bfloat16

NORM_EPS = 1e-6
ROPE_BASE = 10000.0
GRID_W = 64
LANE = 128

RET_HEADS = 4
HGRN_HEADS = 4
HEAD_DIM = 128
RET_CHUNK = 128
HGRN_CHUNK = 128

MLA_HEADS = 8
MLA_NOPE = 128
MLA_ROPE = 64
MLA_V = 128
MLA_Q_RANK = 384
MLA_KV_RANK = 256
N_EXPERTS = 8

VMEM_LIMIT = 56 * 1024 * 1024


def _cparams(*sem):
    return pltpu.CompilerParams(dimension_semantics=sem, vmem_limit_bytes=VMEM_LIMIT)


def _mm(a, b):
    return jnp.dot(a.astype(BF16), b.astype(BF16), preferred_element_type=F32)


def _mm_nt(a, b):
    return lax.dot_general(a.astype(BF16), b.astype(BF16), (((1,), (1,)), ((), ())),
                           preferred_element_type=F32)


def _mm_tn(a, b):
    return lax.dot_general(a.astype(BF16), b.astype(BF16), (((0,), (0,)), ((), ())),
                           preferred_element_type=F32)


def _rms(x):
    return x * lax.rsqrt(jnp.mean(x * x, axis=-1, keepdims=True) + NORM_EPS)


def _silu(x):
    return x * (1.0 / (1.0 + jnp.exp(-x)))


def _modnorm(x, g, mod_ref, shift_row, scale_row):
    shift = mod_ref[shift_row:shift_row + 1, :]
    scale = mod_ref[scale_row:scale_row + 1, :]
    return _rms(x) * g * (1.0 + scale) + shift


def _mod_spec(mod):
    d = mod.shape[-1]
    if mod.shape[0] == 1:
        return pl.BlockSpec((None, 6, d), lambda b, *_: (0, 0, 0))
    return pl.BlockSpec((None, 6, d), lambda b, *_: (b, 0, 0))


def _pick(n, pref):
    t = min(pref, n)
    while n % t:
        t -= LANE
    return t


def _mod_kernel(c_ref, w_ref, b_ref, o_ref):
    a = _silu(c_ref[...])
    o_ref[...] = jnp.dot(a, w_ref[...], preferred_element_type=F32,
                         precision=lax.Precision.HIGHEST) + b_ref[...]


def _modulation(cond, mod_w, mod_b):
    n_layers, d, n = mod_w.shape
    r = cond.shape[0]
    tn = _pick(n, 1536)
    return pl.pallas_call(
        _mod_kernel,
        grid=(n_layers, n // tn),
        in_specs=[pl.BlockSpec((r, d), lambda l, j: (0, 0)),
                  pl.BlockSpec((None, d, tn), lambda l, j: (l, 0, j)),
                  pl.BlockSpec((None, 1, tn), lambda l, j: (l, 0, j))],
        out_specs=pl.BlockSpec((None, r, tn), lambda l, j: (l, 0, j)),
        out_shape=jax.ShapeDtypeStruct((n_layers, r, n), F32),
        compiler_params=_cparams("parallel", "parallel"),
        name="modulation",
    )(cond, mod_w, mod_b.reshape(n_layers, 1, n))


def _norm_matmul_kernel(x_ref, mod_ref, g_ref, w_ref, o_ref, h_ref, *, shift_row, scale_row):
    @pl.when(pl.program_id(2) == 0)
    def _():
        h_ref[...] = _modnorm(x_ref[...], g_ref[...], mod_ref, shift_row, scale_row).astype(BF16)

    o_ref[...] = jnp.dot(h_ref[...], w_ref[...], preferred_element_type=F32).astype(o_ref.dtype)


def _norm_matmul(x, mod, g, w, shift_row, scale_row, out_dtype, tm=1024, tn=1792):
    b, t, d = x.shape
    n = w.shape[1]
    tm = _pick(t, tm)
    tn = _pick(n, tn)
    return pl.pallas_call(
        functools.partial(_norm_matmul_kernel, shift_row=shift_row, scale_row=scale_row),
        grid=(b, t // tm, n // tn),
        in_specs=[pl.BlockSpec((None, tm, d), lambda bi, i, j: (bi, i, 0)),
                  _mod_spec(mod),
                  pl.BlockSpec((1, d), lambda bi, i, j: (0, 0)),
                  pl.BlockSpec((d, tn), lambda bi, i, j: (0, j))],
        out_specs=pl.BlockSpec((None, tm, tn), lambda bi, i, j: (bi, i, j)),
        out_shape=jax.ShapeDtypeStruct((b, t, n), out_dtype),
        scratch_shapes=[pltpu.VMEM((tm, d), BF16)],
        compiler_params=_cparams("parallel", "parallel", "arbitrary"),
        name="norm_matmul",
    )(x, mod, g.reshape(1, d), w)


def _proj_residual_kernel(*refs, n_lhs, gate_row):
    lhs = refs[:n_lhs]
    ws = refs[n_lhs:2 * n_lhs]
    x_ref, mod_ref, g_ref, o_ref = refs[2 * n_lhs:]
    y = jnp.dot(lhs[0][...], ws[0][...], preferred_element_type=F32)
    for a_ref, w_ref in zip(lhs[1:], ws[1:]):
        y += jnp.dot(a_ref[...], w_ref[...], preferred_element_type=F32)
    gate = mod_ref[gate_row:gate_row + 1, :]
    o_ref[...] = x_ref[...] + gate * (_rms(y) * g_ref[...])


def _proj_residual(lhs, ws, x, mod, g, gate_row, tm=512):
    b, t, d = x.shape
    tm = _pick(t, tm)
    n_lhs = len(lhs)
    in_specs = [pl.BlockSpec((None, tm, a.shape[-1]), lambda bi, i: (bi, i, 0)) for a in lhs]
    in_specs += [pl.BlockSpec(w.shape, lambda bi, i: (0, 0)) for w in ws]
    in_specs += [pl.BlockSpec((None, tm, d), lambda bi, i: (bi, i, 0)),
                 _mod_spec(mod),
                 pl.BlockSpec((1, d), lambda bi, i: (0, 0))]
    return pl.pallas_call(
        functools.partial(_proj_residual_kernel, n_lhs=n_lhs, gate_row=gate_row),
        grid=(b, t // tm),
        in_specs=in_specs,
        out_specs=pl.BlockSpec((None, tm, d), lambda bi, i: (bi, i, 0)),
        out_shape=jax.ShapeDtypeStruct((b, t, d), F32),
        compiler_params=_cparams("parallel", "parallel"),
        name="proj_residual",
    )(*lhs, *ws, x, mod, g.reshape(1, d))


def _ffn_kernel(x_ref, mod_ref, g_in_ref, g_out_ref, wa_ref, wb_ref, wo_ref, o_ref, h_ref, acc_ref):
    f = pl.program_id(2)

    @pl.when(f == 0)
    def _():
        h_ref[...] = _modnorm(x_ref[...], g_in_ref[...], mod_ref, 3, 4).astype(BF16)
        acc_ref[...] = jnp.zeros_like(acc_ref)

    h = h_ref[...]
    a = jnp.dot(h, wa_ref[...], preferred_element_type=F32)
    bb = jnp.dot(h, wb_ref[...], preferred_element_type=F32)
    acc_ref[...] += jnp.dot((_silu(a) * bb).astype(BF16), wo_ref[...], preferred_element_type=F32)

    @pl.when(f == pl.num_programs(2) - 1)
    def _():
        gate = mod_ref[5:6, :]
        o_ref[...] = x_ref[...] + gate * (_rms(acc_ref[...]) * g_out_ref[...])


def _ffn(x, mod, g_in, g_out, w_in, w_out, tm=512, tf=1408):
    b, t, d = x.shape
    fdim = w_out.shape[0]
    tm = _pick(t, tm)
    tf = _pick(fdim, tf)
    nf = fdim // tf
    return pl.pallas_call(
        _ffn_kernel,
        grid=(b, t // tm, nf),
        in_specs=[pl.BlockSpec((None, tm, d), lambda bi, i, f: (bi, i, 0)),
                  _mod_spec(mod),
                  pl.BlockSpec((1, d), lambda bi, i, f: (0, 0)),
                  pl.BlockSpec((1, d), lambda bi, i, f: (0, 0)),
                  pl.BlockSpec((d, tf), lambda bi, i, f: (0, f)),
                  pl.BlockSpec((d, tf), lambda bi, i, f: (0, f + nf)),
                  pl.BlockSpec((tf, d), lambda bi, i, f: (f, 0))],
        out_specs=pl.BlockSpec((None, tm, d), lambda bi, i, f: (bi, i, 0)),
        out_shape=jax.ShapeDtypeStruct((b, t, d), F32),
        scratch_shapes=[pltpu.VMEM((tm, d), BF16), pltpu.VMEM((tm, d), F32)],
        compiler_params=_cparams("parallel", "parallel", "arbitrary"),
        name="ffn",
    )(x, mod, g_in.reshape(1, d), g_out.reshape(1, d), w_in, w_in, w_out)


SCAN_HEADS = 4


def _head_slice(hh):
    return slice(hh * HEAD_DIM, (hh + 1) * HEAD_DIM)


def _finish_scan(of_ref, ob_ref, g_ref, y_ref, cs):
    def step(i, carry):
        start = pl.multiple_of(i * cs, cs)
        for hh in range(y_ref.shape[1] // HEAD_DIM):
            sl = _head_slice(hh)
            o = of_ref[pl.ds(start, cs), sl].astype(F32) + ob_ref[pl.ds(start, cs), sl].astype(F32)
            gate = g_ref[pl.ds(start, cs), sl].astype(F32)
            y_ref[pl.ds(start, cs), sl] = (_rms(o) * _silu(gate)).astype(y_ref.dtype)
        return carry

    lax.fori_loop(0, y_ref.shape[0] // cs, step, 0)


def _retention_kernel(lg_ref, qc_ref, kc_ref, vc_ref, gc_ref, qx_ref, kx_ref, vx_ref, gx_ref,
                      cos_ref, sin_ref, yc_ref, yx_ref,
                      ofc_ref, obc_ref, ofx_ref, obx_ref, s_ref, intra_ref, qd_ref, kd_ref):
    cs = RET_CHUNK
    hp = SCAN_HEADS
    head0 = pl.program_id(1) * hp
    row = lax.broadcasted_iota(jnp.int32, (cs, cs), 0).astype(F32)
    col = lax.broadcasted_iota(jnp.int32, (cs, cs), 1).astype(F32)
    pos = lax.broadcasted_iota(jnp.int32, (cs, HEAD_DIM), 0).astype(F32)
    k_scale = HEAD_DIM ** -0.5

    for hh in range(hp):
        for d in (0, 1):
            c = 2 * hh + d
            lg = lg_ref[d, head0 + hh]
            rel = (col - row) if d else (row - col)
            intra_ref[c] = jnp.where(rel >= 0, jnp.exp(lg * jnp.maximum(rel, 0.0)), 0.0)
            p = (cs - 1.0 - pos) if d else pos
            qd_ref[c] = jnp.exp(lg * (p + 1.0))
            kd_ref[c] = jnp.exp(lg * (cs - 1.0 - p))
    s_ref[...] = jnp.zeros_like(s_ref)

    for (q_ref, k_ref, v_ref, of_ref, ob_ref, rope) in (
            (qc_ref, kc_ref, vc_ref, ofc_ref, obc_ref, False),
            (qx_ref, kx_ref, vx_ref, ofx_ref, obx_ref, True)):
        n_chunks = q_ref.shape[0] // cs

        def step(i, carry, q_ref=q_ref, k_ref=k_ref, v_ref=v_ref, of_ref=of_ref, ob_ref=ob_ref,
                 rope=rope, n_chunks=n_chunks):
            chains = []
            for d in (0, 1):
                start = pl.multiple_of(((n_chunks - 1 - i) if d else i) * cs, cs)
                if rope:
                    cos = cos_ref[pl.ds(start, cs), :]
                    sin = sin_ref[pl.ds(start, cs), :]
                for hh in range(hp):
                    sl = _head_slice(hh)
                    q = q_ref[pl.ds(start, cs), sl].astype(F32)
                    k = k_ref[pl.ds(start, cs), sl].astype(F32) * k_scale
                    if rope:
                        q = q * cos + pltpu.roll(q, HEAD_DIM // 2, 1) * sin
                        k = k * cos + pltpu.roll(k, HEAD_DIM // 2, 1) * sin
                    chains.append(dict(d=d, hh=hh, c=2 * hh + d, start=start, sl=sl, q=q, k=k,
                                       v=v_ref[pl.ds(start, cs), sl], o_ref=ob_ref if d else of_ref))
            for ch in chains:
                ch["att"] = _mm_nt(ch["q"], ch["k"]) * intra_ref[ch["c"]]
            for ch in chains:
                c = ch["c"]
                ch["state"] = s_ref[c]
                ch["o_ref"][pl.ds(ch["start"], cs), ch["sl"]] = (
                    _mm(ch["att"], ch["v"]) + _mm(ch["q"] * qd_ref[c], ch["state"]))
            for ch in chains:
                c = ch["c"]
                c_dec = jnp.exp(jnp.full((1, HEAD_DIM), lg_ref[ch["d"], head0 + ch["hh"]] * cs, F32))
                s_ref[c] = ch["state"] * c_dec + _mm_tn(ch["k"] * kd_ref[c], ch["v"])
            return carry

        lax.fori_loop(0, n_chunks, step, 0)

    _finish_scan(ofc_ref, obc_ref, gc_ref, yc_ref, cs)
    _finish_scan(ofx_ref, obx_ref, gx_ref, yx_ref, cs)


def _retention(proj_c, proj_x, log_gamma, cos2, sin2):
    b, tc, _ = proj_c.shape
    tx = proj_x.shape[1]
    h, hp = RET_HEADS, SCAN_HEADS
    w = hp * HEAD_DIM

    def col(t, group):
        return pl.BlockSpec((None, t, w), lambda bi, hi, lg: (bi, 0, group * (h // hp) + hi),
                            pipeline_mode=pl.Buffered(1))

    n_chain = 2 * hp
    grid_spec = pltpu.PrefetchScalarGridSpec(
        num_scalar_prefetch=1,
        grid=(b, h // hp),
        in_specs=[col(tc, 0), col(tc, 1), col(tc, 2), col(tc, 3),
                  col(tx, 0), col(tx, 1), col(tx, 2), col(tx, 3),
                  pl.BlockSpec((tx, HEAD_DIM), lambda bi, hi, lg: (0, 0), pipeline_mode=pl.Buffered(1)),
                  pl.BlockSpec((tx, HEAD_DIM), lambda bi, hi, lg: (0, 0), pipeline_mode=pl.Buffered(1))],
        out_specs=[pl.BlockSpec((None, tc, w), lambda bi, hi, lg: (bi, 0, hi)),
                   pl.BlockSpec((None, tx, w), lambda bi, hi, lg: (bi, 0, hi))],
        scratch_shapes=[pltpu.VMEM((tc, w), F32), pltpu.VMEM((tc, w), F32),
                        pltpu.VMEM((tx, w), F32), pltpu.VMEM((tx, w), F32),
                        pltpu.VMEM((n_chain, HEAD_DIM, HEAD_DIM), F32),
                        pltpu.VMEM((n_chain, RET_CHUNK, RET_CHUNK), F32),
                        pltpu.VMEM((n_chain, RET_CHUNK, HEAD_DIM), F32),
                        pltpu.VMEM((n_chain, RET_CHUNK, HEAD_DIM), F32)])
    return pl.pallas_call(
        _retention_kernel,
        grid_spec=grid_spec,
        out_shape=[jax.ShapeDtypeStruct((b, tc, h * HEAD_DIM), BF16),
                   jax.ShapeDtypeStruct((b, tx, h * HEAD_DIM), BF16)],
        compiler_params=_cparams("parallel", "parallel"),
        name="retention",
    )(log_gamma, proj_c, proj_c, proj_c, proj_c, proj_x, proj_x, proj_x, proj_x, cos2, sin2)


def _hgrn_levels(cs):
    return [cs >> (i + 1) for i in range(int(math.log2(cs)))]


def _hgrn_constants(cs):
    t = np.arange(cs)[:, None]
    s = np.arange(cs)[None, :]
    tri = np.stack([(s <= t), (s >= t)]).astype(np.float32)
    masks = []
    for reverse in (False, True):
        lv = []
        for h in _hgrn_levels(cs):
            same = (t // (2 * h)) == (s // (2 * h))
            t_hi = (t % (2 * h)) >= h
            s_hi = (s % (2 * h)) >= h
            lv.append(same & (~t_hi & s_hi if reverse else t_hi & ~s_hi))
        lv.append(t == s)
        masks.append(np.stack(lv))
    return jnp.asarray(tri, BF16), jnp.asarray(np.stack(masks).astype(np.float32), BF16)


def _split3(x):
    hi = x.astype(BF16)
    r = x - hi.astype(F32)
    mid = r.astype(BF16)
    lo = (r - mid.astype(F32)).astype(BF16)
    return hi, mid, lo


def _neg_abs(x):
    bits = lax.bitcast_convert_type(x, jnp.uint32) | jnp.uint32(0x80000000)
    return lax.bitcast_convert_type(bits, F32)


def _level_reference(b, h, reverse):
    cs = b.shape[0]
    ref_local = h if reverse else h - 1
    if 2 * h >= 8:
        b3 = b.reshape(cs // (2 * h), 2 * h, b.shape[1])
        r = b3[:, ref_local:ref_local + 1, :]
        return jnp.broadcast_to(r, b3.shape).reshape(b.shape)
    local = lax.broadcasted_iota(jnp.int32, b.shape, 0) % (2 * h)
    out = b
    for m in range(2 * h):
        if m != ref_local:
            out = jnp.where(local == m, pltpu.roll(b, (m - ref_local) % cs, 0), out)
    return out


def _hgrn_kernel(lb_ref, tri_ref, mask_ref,
                 qc_ref, vc_ref, gc_ref, zfc_ref, zbc_ref,
                 qx_ref, vx_ref, gx_ref, zfx_ref, zbx_ref,
                 yc_ref, yx_ref, ofc_ref, obc_ref, ofx_ref, obx_ref, st_ref):
    cs = HGRN_CHUNK
    hp = SCAN_HEADS
    levels = _hgrn_levels(cs)
    st_ref[...] = jnp.zeros_like(st_ref)

    for (q_ref, v_ref, zf_ref, zb_ref, of_ref, ob_ref) in (
            (qc_ref, vc_ref, zfc_ref, zbc_ref, ofc_ref, obc_ref),
            (qx_ref, vx_ref, zfx_ref, zbx_ref, ofx_ref, obx_ref)):
        n_chunks = q_ref.shape[0] // cs

        def step(i, carry, q_ref=q_ref, v_ref=v_ref, zf_ref=zf_ref, zb_ref=zb_ref, of_ref=of_ref,
                 ob_ref=ob_ref, n_chunks=n_chunks):
            chains = []
            for d in (0, 1):
                start = pl.multiple_of(((n_chunks - 1 - i) if d else i) * cs, cs)
                z_ref = zb_ref if d else zf_ref
                for hh in range(hp):
                    sl = _head_slice(hh)
                    lb = lb_ref[hh]
                    q = _silu(q_ref[pl.ds(start, cs), sl].astype(F32))
                    f = lb + (1.0 - lb) * (1.0 / (1.0 + jnp.exp(-z_ref[pl.ds(start, cs), sl])))
                    chains.append(dict(d=d, c=2 * hh + d, start=start, sl=sl, q=q, k=1.0 - f,
                                       parts=_split3(jnp.log2(f)), v=v_ref[pl.ds(start, cs), sl],
                                       o_ref=ob_ref if d else of_ref))
            for ch in chains:
                tri = tri_ref[ch["d"]]
                hi, mid, lo = ch["parts"]
                ch["bsum"] = (jnp.dot(tri, hi, preferred_element_type=F32)
                              + jnp.dot(tri, mid, preferred_element_type=F32)
                              + jnp.dot(tri, lo, preferred_element_type=F32))
            for ch in chains:
                ch["qb"], ch["kb"] = ch["q"].astype(BF16), ch["k"].astype(BF16)
                ch["att"] = _mm_nt(ch["qb"], ch["kb"]).astype(BF16) * mask_ref[ch["d"], len(levels)]
            for li, h in enumerate(levels):
                for ch in chains:
                    ref = _level_reference(ch["bsum"], h, ch["d"] == 1)
                    w = jnp.exp2(_neg_abs(ch["bsum"] - ref)).astype(BF16)
                    ch["att"] += _mm_nt(ch["qb"] * w, ch["kb"] * w).astype(BF16) * mask_ref[ch["d"], li]
            for ch in chains:
                ch["state"] = st_ref[ch["c"]]
                o = _mm(ch["att"], ch["v"]) + _mm_nt(ch["q"] * jnp.exp2(ch["bsum"]), ch["state"])
                ch["o_ref"][pl.ds(ch["start"], cs), ch["sl"]] = o.astype(ch["o_ref"].dtype)
            for ch in chains:
                bsum = ch["bsum"]
                b_end = bsum[0:1, :] if ch["d"] else bsum[cs - 1:cs, :]
                st_ref[ch["c"]] = (ch["state"] * jnp.exp2(b_end)
                                   + _mm_tn(ch["v"], ch["k"] * jnp.exp2(b_end - bsum)))
            return carry

        lax.fori_loop(0, n_chunks, step, 0)

    _finish_scan(ofc_ref, obc_ref, gc_ref, yc_ref, cs)
    _finish_scan(ofx_ref, obx_ref, gx_ref, yx_ref, cs)


def _hgrn(proj_c, proj_x, z_c, z_x, lb, first):
    b, tc, _ = proj_c.shape
    tx = proj_x.shape[1]
    h, hp = HGRN_HEADS, SCAN_HEADS
    w = hp * HEAD_DIM
    tri, masks = _hgrn_constants(HGRN_CHUNK)

    def col(t, group):
        return pl.BlockSpec((None, t, w), lambda bi, hi: (bi, 0, (first + group * h) // hp + hi),
                            pipeline_mode=pl.Buffered(1))

    def zcol(t, group):
        return pl.BlockSpec((None, t, w), lambda bi, hi: (bi, 0, group * (h // hp) + hi),
                            pipeline_mode=pl.Buffered(1))

    ins = [col(tc, 0), col(tc, 1), col(tc, 2), zcol(tc, 0), zcol(tc, 1),
           col(tx, 0), col(tx, 1), col(tx, 2), zcol(tx, 0), zcol(tx, 1)]
    return pl.pallas_call(
        _hgrn_kernel,
        grid=(b, h // hp),
        in_specs=[pl.BlockSpec((hp, 1, HEAD_DIM), lambda bi, hi: (hi, 0, 0)),
                  pl.BlockSpec(tri.shape, lambda bi, hi: (0, 0, 0)),
                  pl.BlockSpec(masks.shape, lambda bi, hi: (0, 0, 0, 0))] + ins,
        out_specs=[pl.BlockSpec((None, tc, w), lambda bi, hi: (bi, 0, hi)),
                   pl.BlockSpec((None, tx, w), lambda bi, hi: (bi, 0, hi))],
        out_shape=[jax.ShapeDtypeStruct((b, tc, h * HEAD_DIM), BF16),
                   jax.ShapeDtypeStruct((b, tx, h * HEAD_DIM), BF16)],
        scratch_shapes=[pltpu.VMEM((tc, w), BF16), pltpu.VMEM((tc, w), BF16),
                        pltpu.VMEM((tx, w), BF16), pltpu.VMEM((tx, w), BF16),
                        pltpu.VMEM((2 * hp, HEAD_DIM, HEAD_DIM), F32)],
        compiler_params=_cparams("parallel", "parallel"),
        name="hgrn2",
    )(lb.reshape(h, 1, HEAD_DIM), tri, masks, proj_c, proj_c, proj_c, z_c, z_c,
      proj_x, proj_x, proj_x, z_x, z_x)


MLA_QW = 2 * LANE
MLA_SCORE_SCALE = (MLA_NOPE + MLA_ROPE) ** -0.5 * math.log2(math.e)
MLA_DOWN_PAD = MLA_Q_RANK + MLA_KV_RANK + LANE


def _mla_proj_kernel(x_ref, mod_ref, g_ref, wd_ref, gq_ref, gkv_ref, wq_ref, wkv_ref, cos_ref, sin_ref,
                     *out_refs, rope, want_q):
    h = _modnorm(x_ref[...], g_ref[...], mod_ref, 0, 1).astype(BF16)
    down = jnp.dot(h, wd_ref[...], preferred_element_type=F32)
    if rope:
        cos = cos_ref[...]
        sin = sin_ref[...]

    def rot(v):
        return v * cos + pltpu.roll(v, LANE // 2, 1) * sin if rope else v

    if want_q:
        q_ref, kn_ref, kr_ref, v_ref = out_refs
        qn = (_rms(down[:, :MLA_Q_RANK]) * gq_ref[...]).astype(BF16)
        q = jnp.dot(qn, wq_ref[...], preferred_element_type=F32) * MLA_SCORE_SCALE
        for hd in range(MLA_HEADS):
            lo = hd * MLA_QW
            q_ref[:, lo:lo + LANE] = q[:, lo:lo + LANE].astype(q_ref.dtype)
            q_ref[:, lo + LANE:lo + 2 * LANE] = rot(q[:, lo + LANE:lo + 2 * LANE]).astype(q_ref.dtype)
    else:
        kn_ref, kr_ref, v_ref = out_refs
    kvn = (_rms(down[:, MLA_Q_RANK:MLA_Q_RANK + MLA_KV_RANK]) * gkv_ref[...]).astype(BF16)
    kv = jnp.dot(kvn, wkv_ref[...], preferred_element_type=F32)
    nk = MLA_HEADS * MLA_NOPE
    kn_ref[...] = kv[:, :nk].astype(kn_ref.dtype)
    v_ref[...] = kv[:, nk:].astype(v_ref.dtype)
    kr_ref[...] = rot(down[:, MLA_Q_RANK + MLA_KV_RANK:]).astype(kr_ref.dtype)


def _mla_proj(x, mod, g, wd, gq, gkv, wq, wkv, cos, sin, rope, want_q, tm=512):
    b, t, d = x.shape
    tm = _pick(t, tm)
    const = lambda bi, i: (0, 0)
    tok = lambda bi, i: (bi, i, 0)
    nk, nv = MLA_HEADS * MLA_NOPE, MLA_HEADS * MLA_V
    out_shape = [jax.ShapeDtypeStruct((b, t, nk), BF16), jax.ShapeDtypeStruct((b, t, LANE), BF16),
                 jax.ShapeDtypeStruct((b, t, nv), BF16)]
    out_specs = [pl.BlockSpec((None, tm, nk), tok), pl.BlockSpec((None, tm, LANE), tok),
                 pl.BlockSpec((None, tm, nv), tok)]
    if want_q:
        out_shape = [jax.ShapeDtypeStruct((b, t, MLA_HEADS * MLA_QW), BF16)] + out_shape
        out_specs = [pl.BlockSpec((None, tm, MLA_HEADS * MLA_QW), tok)] + out_specs
    return pl.pallas_call(
        functools.partial(_mla_proj_kernel, rope=rope, want_q=want_q),
        grid=(b, t // tm),
        in_specs=[pl.BlockSpec((None, tm, d), tok), _mod_spec(mod), pl.BlockSpec((1, d), const),
                  pl.BlockSpec(wd.shape, const), pl.BlockSpec((1, MLA_Q_RANK), const),
                  pl.BlockSpec((1, MLA_KV_RANK), const), pl.BlockSpec(wq.shape, const),
                  pl.BlockSpec(wkv.shape, const),
                  pl.BlockSpec((tm, LANE), lambda bi, i: (i, 0)),
                  pl.BlockSpec((tm, LANE), lambda bi, i: (i, 0))],
        out_specs=out_specs,
        out_shape=out_shape,
        compiler_params=_cparams("parallel", "parallel"),
        name="mla_proj",
    )(x, mod, g.reshape(1, d), wd, gq.reshape(1, -1), gkv.reshape(1, -1), wq, wkv, cos, sin)


def _lane_fold(x, op):
    out = x[:, :LANE]
    for j in range(1, x.shape[1] // LANE):
        out = op(out, x[:, j * LANE:(j + 1) * LANE])
    return out


def _attention_kernel(q_ref, knc_ref, krc_ref, vc_ref, knx_ref, krx_ref, vx_ref, o_ref,
                      kc_ref, kx_ref, vc2_ref, vx2_ref, sc0_ref, sc1_ref, sx0_ref, sx1_ref, m0_ref, m1_ref,
                      *, tk):
    step = pl.program_id(2)

    @pl.when(step == 0)
    def _():
        kc_ref[:, :LANE] = knc_ref[...]
        kc_ref[:, LANE:] = krc_ref[...]
        kx_ref[:, :LANE] = knx_ref[...]
        kx_ref[:, LANE:] = krx_ref[...]
        vc2_ref[:, :LANE] = vc_ref[...]
        vx2_ref[:, :LANE] = vx_ref[...]
        for ref in (vc2_ref, vx2_ref):
            lane = lax.broadcasted_iota(jnp.int32, (ref.shape[0], LANE), 1)
            ref[:, LANE:] = jnp.where(lane == 0, 1.0, 0.0).astype(ref.dtype)
        sc1_ref[...] = jnp.zeros(sc1_ref.shape, F32)
        sx1_ref[...] = jnp.zeros(sx1_ref.shape, F32)
        m1_ref[...] = jnp.zeros(m1_ref.shape, F32)

    n_blocks = kx_ref.shape[0] // tk
    nt = (((1,), (1,)), ((), ()))

    def body(sc_cur, sx_cur, m_cur, sc_prev, sx_prev, m_prev):
        q = q_ref[...]
        m = jnp.max(m_prev[...], axis=-1, keepdims=True)
        s = lax.dot_general(q, kc_ref[...], nt, preferred_element_type=F32)
        sc_cur[...] = s
        m_part = _lane_fold(s, jnp.maximum)
        p = jnp.exp2(sc_prev[...] - m)
        acc = jnp.dot(p.astype(BF16), vc2_ref[...], preferred_element_type=F32)
        for i in range(n_blocks):
            s = lax.dot_general(q, kx_ref[i * tk:(i + 1) * tk, :], nt, preferred_element_type=F32)
            sx_cur[i] = s
            m_part = jnp.maximum(m_part, _lane_fold(s, jnp.maximum))
            p = jnp.exp2(sx_prev[i] - m)
            acc = acc + jnp.dot(p.astype(BF16), vx2_ref[i * tk:(i + 1) * tk, :],
                                preferred_element_type=F32)
        m_cur[...] = m_part
        o_ref[...] = (acc[:, :MLA_V] / acc[:, MLA_V:MLA_V + 1]).astype(o_ref.dtype)

    @pl.when(step % 2 == 0)
    def _():
        body(sc0_ref, sx0_ref, m0_ref, sc1_ref, sx1_ref, m1_ref)

    @pl.when(step % 2 == 1)
    def _():
        body(sc1_ref, sx1_ref, m1_ref, sc0_ref, sx0_ref, m0_ref)


def _attention(q, kn_c, kr_c, v_c, kn_x, kr_x, v_x, tq=256, tk=256):
    b, t, _ = q.shape
    tc = kn_c.shape[1]
    tq = _pick(t, tq)
    tk = _pick(t, tk)
    nq = t // tq
    head_c = lambda bi, hi, i: (bi, 0, hi)
    shared = lambda bi, hi, i: (bi, 0, 0)
    return pl.pallas_call(
        functools.partial(_attention_kernel, tk=tk),
        grid=(b, MLA_HEADS, nq + 1),
        in_specs=[pl.BlockSpec((None, tq, MLA_QW), lambda bi, hi, i: (bi, jnp.minimum(i, nq - 1), hi)),
                  pl.BlockSpec((None, tc, LANE), head_c), pl.BlockSpec((None, tc, LANE), shared),
                  pl.BlockSpec((None, tc, LANE), head_c),
                  pl.BlockSpec((None, t, LANE), head_c), pl.BlockSpec((None, t, LANE), shared),
                  pl.BlockSpec((None, t, LANE), head_c)],
        out_specs=pl.BlockSpec((None, tq, MLA_V), lambda bi, hi, i: (bi, jnp.maximum(i - 1, 0), hi)),
        out_shape=jax.ShapeDtypeStruct((b, t, MLA_HEADS * MLA_V), BF16),
        scratch_shapes=[pltpu.VMEM((tc, MLA_QW), BF16), pltpu.VMEM((t, MLA_QW), BF16),
                        pltpu.VMEM((tc, 2 * MLA_V), BF16), pltpu.VMEM((t, 2 * MLA_V), BF16),
                        pltpu.VMEM((tq, tc), F32), pltpu.VMEM((tq, tc), F32),
                        pltpu.VMEM((t // tk, tq, tk), F32), pltpu.VMEM((t // tk, tq, tk), F32),
                        pltpu.VMEM((tq, LANE), F32), pltpu.VMEM((tq, LANE), F32)],
        compiler_params=_cparams("parallel", "parallel", "arbitrary"),
        name="mla_attention",
    )(q, kn_c, kr_c, v_c, kn_x, kr_x, v_x)


def _router_kernel(x_ref, mod_ref, g_ref, wr_ref, h_ref, w_ref, e_ref):
    h = _modnorm(x_ref[...], g_ref[...], mod_ref, 3, 4)
    h_ref[...] = _pack_halves(h)
    logits = jnp.dot(h, wr_ref[...], preferred_element_type=F32, precision=lax.Precision.HIGHEST)
    lane = lax.broadcasted_iota(jnp.int32, logits.shape, 1)
    neg = jnp.float32(-jnp.inf)
    lg = jnp.where(lane < N_EXPERTS, logits, neg)
    m1 = jnp.max(lg, axis=-1, keepdims=True)
    i1 = jnp.min(jnp.where(lg == m1, lane, LANE), axis=-1, keepdims=True)
    lg2 = jnp.where(lane == i1, neg, lg)
    m2 = jnp.max(lg2, axis=-1, keepdims=True)
    i2 = jnp.min(jnp.where(lg2 == m2, lane, LANE), axis=-1, keepdims=True)
    e2 = jnp.exp(m2 - m1)
    w1 = 1.0 / (1.0 + e2)
    w2 = e2 / (1.0 + e2)
    w_ref[...] = jnp.where(lane == 0, w1, jnp.where(lane == 1, w2, 0.0))
    e_ref[...] = jnp.where(lane == 0, i1, jnp.where(lane == 1, i2, 0))


def _router(x, mod, g, router_pad, tm=512):
    b, t, d = x.shape
    tm = _pick(t, tm)
    tok = lambda bi, i: (bi, i, 0)
    const = lambda bi, i: (0, 0)
    return pl.pallas_call(
        _router_kernel,
        grid=(b, t // tm),
        in_specs=[pl.BlockSpec((None, tm, d), tok), _mod_spec(mod), pl.BlockSpec((1, d), const),
                  pl.BlockSpec((d, LANE), const)],
        out_specs=[pl.BlockSpec((None, tm, d // 2), tok), pl.BlockSpec((None, tm, LANE), tok),
                   pl.BlockSpec((None, tm, LANE), tok)],
        out_shape=[jax.ShapeDtypeStruct((b, t, d // 2), jnp.int32), jax.ShapeDtypeStruct((b, t, LANE), F32),
                   jax.ShapeDtypeStruct((b, t, LANE), jnp.int32)],
        compiler_params=_cparams("parallel", "parallel"),
        name="moe_router",
    )(x, mod, g.reshape(1, d), router_pad)


SC_GATHER_ROWS = 64


def _gather_rows(table, idx):
    info = plsc.get_sparse_core_info()
    n_workers = info.num_cores * info.num_subcores
    n, width = idx.shape[0], table.shape[1]
    chunk = SC_GATHER_ROWS
    assert n % (n_workers * chunk) == 0
    n_chunks = n // (n_workers * chunk)
    mesh = plsc.VectorSubcoreMesh(core_axis_name="c", subcore_axis_name="s")

    def body(table_hbm, idx_hbm, out_hbm, idx_v, rows_v, sem):
        wid = lax.axis_index("s") * info.num_cores + lax.axis_index("c")
        pltpu.sync_copy(idx_hbm.at[wid], idx_v)

        @pl.loop(0, n_chunks)
        def _(j):
            pltpu.async_copy(table_hbm.at[idx_v.at[j]], rows_v, sem).wait()
            pltpu.sync_copy(rows_v, out_hbm.at[pl.ds((wid * n_chunks + j) * chunk, chunk)])

    return pl.kernel(
        body,
        out_type=jax.ShapeDtypeStruct((n, width), table.dtype),
        mesh=mesh,
        scratch_types=[pltpu.VMEM((n_chunks, chunk), jnp.int32), pltpu.VMEM((chunk, width), table.dtype),
                       pltpu.SemaphoreType.DMA],
        name="sc_gather_rows",
    )(table, idx.reshape(n_workers, n_chunks, chunk))


def _scatter_rows_twice(table, idx0, idx1, n_out):
    info = plsc.get_sparse_core_info()
    n_workers = info.num_cores * info.num_subcores
    n, width = table.shape
    chunk = SC_GATHER_ROWS
    assert n % (n_workers * chunk) == 0
    n_chunks = n // (n_workers * chunk)
    mesh = plsc.VectorSubcoreMesh(core_axis_name="c", subcore_axis_name="s")

    def body(table_hbm, i0_hbm, i1_hbm, out_hbm, i0_v, i1_v, rows_v):
        wid = lax.axis_index("s") * info.num_cores + lax.axis_index("c")
        pltpu.sync_copy(i0_hbm.at[wid], i0_v)
        pltpu.sync_copy(i1_hbm.at[wid], i1_v)

        @pl.loop(0, n_chunks)
        def _(j):
            pltpu.sync_copy(table_hbm.at[pl.ds((wid * n_chunks + j) * chunk, chunk)], rows_v)
            pltpu.sync_copy(rows_v, out_hbm.at[i0_v.at[j]])
            pltpu.sync_copy(rows_v, out_hbm.at[i1_v.at[j]])

    shape3 = (n_workers, n_chunks, chunk)
    return pl.kernel(
        body,
        out_type=jax.ShapeDtypeStruct((n_out, width), table.dtype),
        mesh=mesh,
        scratch_types=[pltpu.VMEM((n_chunks, chunk), jnp.int32), pltpu.VMEM((n_chunks, chunk), jnp.int32),
                       pltpu.VMEM((chunk, width), table.dtype)],
        name="sc_scatter_rows",
    )(table, idx0.reshape(shape3), idx1.reshape(shape3))


def _pack_halves(x):
    k = x.shape[1] // 2
    bits = lambda v: lax.bitcast_convert_type(v.astype(BF16).astype(F32), jnp.uint32)
    word = lax.shift_right_logical(bits(x[:, :k]), jnp.uint32(16)) | bits(x[:, k:])
    return lax.bitcast_convert_type(word, jnp.int32)


def _unpack_halves(w):
    u = lax.bitcast_convert_type(w, jnp.uint32)
    lo = lax.bitcast_convert_type(lax.shift_left(u, jnp.uint32(16)), F32)
    hi = lax.bitcast_convert_type(u & jnp.uint32(0xFFFF0000), F32)
    return jnp.concatenate([lo, hi], axis=-1)


def _expert_ffn_kernel(te_ref, tv_ref, x_ref, wa_ref, wb_ref, wo_ref, o_ref, acc_ref):
    j = pl.program_id(0)
    f = pl.program_id(1)
    last = pl.num_programs(1) - 1
    valid = tv_ref[j] != 0

    @pl.when(valid)
    def _():
        @pl.when(f == 0)
        def _():
            acc_ref[...] = jnp.zeros_like(acc_ref)

        h = _unpack_halves(x_ref[...]).astype(BF16)
        a = jnp.dot(h, wa_ref[...], preferred_element_type=F32)
        bb = jnp.dot(h, wb_ref[...], preferred_element_type=F32)
        acc_ref[...] += jnp.dot((_silu(a) * bb).astype(BF16), wo_ref[...], preferred_element_type=F32)

        @pl.when(f == last)
        def _():
            o_ref[...] = _pack_halves(acc_ref[...])

    @pl.when(jnp.logical_not(valid) & (f == last))
    def _():
        o_ref[...] = jnp.zeros_like(o_ref)


def _expert_ffn(xs, tile_expert, tile_valid, w_in, w_out, tm, tf=1408):
    p = xs.shape[0]
    n_exp, fdim, d = w_out.shape
    tf = _pick(fdim, tf)
    nf = fdim // tf

    def f_eff(j, f, tv):
        return jnp.where(tv[j] != 0, f, nf - 1)

    grid_spec = pltpu.PrefetchScalarGridSpec(
        num_scalar_prefetch=2,
        grid=(p // tm, nf),
        in_specs=[pl.BlockSpec((tm, d // 2), lambda j, f, te, tv: (j, 0)),
                  pl.BlockSpec((None, d, tf), lambda j, f, te, tv: (te[j], 0, f_eff(j, f, tv))),
                  pl.BlockSpec((None, d, tf), lambda j, f, te, tv: (te[j], 0, f_eff(j, f, tv) + nf)),
                  pl.BlockSpec((None, tf, d), lambda j, f, te, tv: (te[j], f_eff(j, f, tv), 0))],
        out_specs=pl.BlockSpec((tm, d // 2), lambda j, f, te, tv: (j, 0)),
        scratch_shapes=[pltpu.VMEM((tm, d), F32)])
    return pl.pallas_call(
        _expert_ffn_kernel,
        grid_spec=grid_spec,
        out_shape=jax.ShapeDtypeStruct((p, d // 2), jnp.int32),
        compiler_params=_cparams("parallel", "arbitrary"),
        name="moe_expert_ffn",
    )(tile_expert, tile_valid, xs, w_in, w_in, w_out)


def _moe_combine_kernel(x_ref, y_ref, w_ref, mod_ref, g_ref, o_ref):
    k = x_ref.shape[-1] // 2
    y = w_ref[:, 0:1] * _unpack_halves(y_ref[:, :k]) + w_ref[:, 1:2] * _unpack_halves(y_ref[:, k:])
    o_ref[...] = x_ref[...] + mod_ref[5:6, :] * (_rms(y) * g_ref[...])


def _moe_combine(x, y2, w, mod, g, tm=512):
    b, t, d = x.shape
    tm = _pick(t, tm)
    tok = lambda bi, i: (bi, i, 0)
    return pl.pallas_call(
        _moe_combine_kernel,
        grid=(b, t // tm),
        in_specs=[pl.BlockSpec((None, tm, d), tok), pl.BlockSpec((None, tm, d), tok),
                  pl.BlockSpec((None, tm, LANE), tok), _mod_spec(mod),
                  pl.BlockSpec((1, d), lambda bi, i: (0, 0))],
        out_specs=pl.BlockSpec((None, tm, d), tok),
        out_shape=jax.ShapeDtypeStruct((b, t, d), F32),
        compiler_params=_cparams("parallel", "parallel"),
        name="moe_combine",
    )(x, y2, w, mod, g.reshape(1, d))


def _moe(x, mod, g_in, g_out, router_pad, w_in, w_out, tm=512):
    b, t, d = x.shape
    n_exp = w_out.shape[0]
    n = b * t
    h, w, e = _router(x, mod, g_in, router_pad)
    e_flat = e[..., :2].reshape(2 * n)
    onehot = (e_flat[:, None] == jnp.arange(n_exp, dtype=jnp.int32)).astype(jnp.int32)
    csum = jnp.cumsum(onehot, axis=0)
    rank = jnp.sum((csum - onehot) * onehot, axis=-1)
    counts = csum[-1]
    ends = jnp.cumsum((counts + tm - 1) // tm * tm)
    dest = (ends - (counts + tm - 1) // tm * tm)[e_flat] + rank
    p = 2 * n + n_exp * tm
    tile_start = jnp.arange(p // tm, dtype=jnp.int32) * tm
    tile_expert = jnp.minimum(jnp.searchsorted(ends, tile_start, side="right"), n_exp - 1).astype(jnp.int32)
    tile_valid = (tile_start < ends[-1]).astype(jnp.int32)

    dest2 = dest.reshape(n, 2)
    xs = _scatter_rows_twice(h.reshape(n, d // 2), dest2[:, 0], dest2[:, 1], p)
    ys = _expert_ffn(xs, tile_expert, tile_valid, w_in, w_out, tm)
    y2 = _gather_rows(ys, dest)
    return _moe_combine(x, y2.reshape(b, t, d), w, mod, g_out)


def _axial_angles(n_tokens, dim):
    t = jnp.arange(n_tokens)
    rows = (t // GRID_W).astype(F32)
    cols = (t % GRID_W).astype(F32)
    n_freq = dim // 4
    inv = ROPE_BASE ** (-jnp.arange(n_freq, dtype=F32) / n_freq)
    return jnp.concatenate([rows[:, None] * inv, cols[:, None] * inv], axis=-1)


def _rope_tables(n_tokens, dim):
    ang = _axial_angles(n_tokens, dim)
    cos, sin = jnp.cos(ang), jnp.sin(ang)
    if dim == LANE:
        return jnp.concatenate([cos, cos], -1), jnp.concatenate([-sin, sin], -1)
    z = jnp.zeros_like(cos)
    return jnp.concatenate([cos, z, cos, z], -1), jnp.concatenate([-sin, z, sin, z], -1)


def _spread_rope_cols(w):
    half = MLA_ROPE // 2
    z = jnp.zeros(w.shape[:-1] + (half,), w.dtype)
    return jnp.concatenate([w[..., :half], z, w[..., half:], z], axis=-1)


def kernel(x, c, ctx, c_ctx, mod_w, mod_b, norm_g, mix_in_w, ret_decay_logit, hgrn_lb_logit, mix_out_w,
           ffn_in_w, ffn_out_w, mla_down_w, mla_q_norm_g, mla_kv_norm_g, mla_uq_w, mla_ukv_w, mla_out_w,
           router_w, moe_in_w, moe_out_w):
    batch, seq, d = x.shape
    assert mod_w.shape[0] == 2, "two layers: retention/HGRN2 then MLA/MoE"

    rows = -(-(batch + 1) // 8) * 8
    cond = jnp.zeros((rows, d), F32).at[:batch].set(c).at[batch].set(c_ctx)
    mod = _modulation(cond, mod_w, mod_b)
    mod_x = [mod[l, :batch].reshape(batch, 6, d) for l in range(2)]
    mod_c = [mod[l, batch:batch + 1].reshape(1, 6, d) for l in range(2)]

    g = norm_g[0]
    ret_w = RET_HEADS * HEAD_DIM
    z_lo, z_hi = 5 * ret_w, 7 * ret_w
    w_in = mix_in_w[0].astype(BF16)
    w_main = jnp.concatenate([w_in[:, :z_lo], w_in[:, z_hi:]], axis=1)
    w_z = w_in[:, z_lo:z_hi]
    proj_x = _norm_matmul(x, mod_x[0], g[0], w_main, 0, 1, BF16)
    proj_c = _norm_matmul(ctx, mod_c[0], g[0], w_main, 0, 1, BF16)
    z_x = _norm_matmul(x, mod_x[0], g[0], w_z, 0, 1, F32)
    z_c = _norm_matmul(ctx, mod_c[0], g[0], w_z, 0, 1, F32)
    cos2, sin2 = _rope_tables(seq, HEAD_DIM)
    log_gamma = jax.nn.log_sigmoid(ret_decay_logit[0].astype(F32))
    lb = jnp.cumsum(jax.nn.softmax(hgrn_lb_logit.astype(F32), axis=0), axis=0)[0]
    ret_c, ret_x = _retention(proj_c, proj_x, log_gamma, cos2, sin2)
    hg_c, hg_x = _hgrn(proj_c, proj_x, z_c, z_x, lb, first=4 * RET_HEADS)
    w_out = mix_out_w[0].astype(BF16)
    w_outs = [w_out[:ret_w], w_out[ret_w:]]
    x = _proj_residual([ret_x, hg_x], w_outs, x, mod_x[0], g[1], 2)
    ctx = _proj_residual([ret_c, hg_c], w_outs, ctx, mod_c[0], g[1], 2)
    f_in, f_out = ffn_in_w[0].astype(BF16), ffn_out_w[0].astype(BF16)
    x = _ffn(x, mod_x[0], g[2], g[3], f_in, f_out)
    ctx = _ffn(ctx, mod_c[0], g[2], g[3], f_in, f_out)

    g = norm_g[1]
    dq = MLA_NOPE + MLA_ROPE
    wd = mla_down_w[0]
    wd = jnp.concatenate([wd[:, :MLA_Q_RANK + MLA_KV_RANK],
                          _spread_rope_cols(wd[:, MLA_Q_RANK + MLA_KV_RANK:])], axis=-1).astype(BF16)
    wq = mla_uq_w[0].reshape(MLA_Q_RANK, MLA_HEADS, dq)
    wq = jnp.concatenate([wq[..., :MLA_NOPE], _spread_rope_cols(wq[..., MLA_NOPE:])], axis=-1)
    wq = wq.reshape(MLA_Q_RANK, MLA_HEADS * MLA_QW).astype(BF16)
    wkv = mla_ukv_w[0].reshape(MLA_KV_RANK, MLA_HEADS, MLA_NOPE + MLA_V)
    wkv = jnp.concatenate([wkv[..., :MLA_NOPE].reshape(MLA_KV_RANK, -1),
                           wkv[..., MLA_NOPE:].reshape(MLA_KV_RANK, -1)], axis=-1).astype(BF16)
    cos_m, sin_m = _rope_tables(seq, MLA_ROPE)
    q, kn_x, kr_x, v_x = _mla_proj(x, mod_x[1], g[0], wd, mla_q_norm_g[0], mla_kv_norm_g[0], wq, wkv,
                                   cos_m, sin_m, rope=True, want_q=True)
    tc = ctx.shape[1]
    kn_c, kr_c, v_c = _mla_proj(ctx, mod_c[1], g[0], wd, mla_q_norm_g[0], mla_kv_norm_g[0], wq, wkv,
                                cos_m[:tc], sin_m[:tc], rope=False, want_q=False)
    att = _attention(q, kn_c, kr_c, v_c, kn_x, kr_x, v_x)
    x = _proj_residual([att], [mla_out_w[0].astype(BF16)], x, mod_x[1], g[1], 2)
    router_pad = jnp.zeros((d, LANE), F32).at[:, :N_EXPERTS].set(router_w[0])
    x = _moe(x, mod_x[1], g[2], g[3], router_pad, moe_in_w[0].astype(BF16), moe_out_w[0].astype(BF16))
    return x
```

```python
import functools
import math

import jax
import jax.numpy as jnp
import numpy as np
from jax import lax
from jax.experimental import pallas as pl
from jax.experimental.pallas import tpu as pltpu
from jax.experimental.pallas import tpu_sc as plsc

F32 = jnp.float32
BF16 = jnp.bfloat16

NORM_EPS = 1e-6
ROPE_BASE = 10000.0
GRID_W = 64
LANE = 128

RET_HEADS = 4
HGRN_HEADS = 4
HEAD_DIM = 128
RET_CHUNK = 128
HGRN_CHUNK = 128

MLA_HEADS = 8
MLA_NOPE = 128
MLA_ROPE = 64
MLA_V = 128
MLA_Q_RANK = 384
MLA_KV_RANK = 256
N_EXPERTS = 8

VMEM_LIMIT = 56 * 1024 * 1024


def _cparams(*sem):
    return pltpu.CompilerParams(dimension_semantics=sem, vmem_limit_bytes=VMEM_LIMIT)


def _mm(a, b):
    return jnp.dot(a.astype(BF16), b.astype(BF16), preferred_element_type=F32)


def _mm_nt(a, b):
    return lax.dot_general(a.astype(BF16), b.astype(BF16), (((1,), (1,)), ((), ())),
                           preferred_element_type=F32)


def _mm_tn(a, b):
    return lax.dot_general(a.astype(BF16), b.astype(BF16), (((0,), (0,)), ((), ())),
                           preferred_element_type=F32)


def _rms(x):
    return x * lax.rsqrt(jnp.mean(x * x, axis=-1, keepdims=True) + NORM_EPS)


def _silu(x):
    return x * (1.0 / (1.0 + jnp.exp(-x)))


def _modnorm(x, g, mod_ref, shift_row, scale_row):
    shift = mod_ref[shift_row:shift_row + 1, :]
    scale = mod_ref[scale_row:scale_row + 1, :]
    return _rms(x) * g * (1.0 + scale) + shift


def _mod_spec(mod):
    d = mod.shape[-1]
    if mod.shape[0] == 1:
        return pl.BlockSpec((None, 6, d), lambda b, *_: (0, 0, 0))
    return pl.BlockSpec((None, 6, d), lambda b, *_: (b, 0, 0))


def _pick(n, pref):
    t = min(pref, n)
    while n % t:
        t -= LANE
    return t


def _mod_kernel(c_ref, w_ref, b_ref, o_ref):
    a = _silu(c_ref[...])
    o_ref[...] = jnp.dot(a, w_ref[...], preferred_element_type=F32,
                         precision=lax.Precision.HIGHEST) + b_ref[...]


def _modulation(cond, mod_w, mod_b):
    n_layers, d, n = mod_w.shape
    r = cond.shape[0]
    tn = _pick(n, 1536)
    return pl.pallas_call(
        _mod_kernel,
        grid=(n_layers, n // tn),
        in_specs=[pl.BlockSpec((r, d), lambda l, j: (0, 0)),
                  pl.BlockSpec((None, d, tn), lambda l, j: (l, 0, j)),
                  pl.BlockSpec((None, 1, tn), lambda l, j: (l, 0, j))],
        out_specs=pl.BlockSpec((None, r, tn), lambda l, j: (l, 0, j)),
        out_shape=jax.ShapeDtypeStruct((n_layers, r, n), F32),
        compiler_params=_cparams("parallel", "parallel"),
        name="modulation",
    )(cond, mod_w, mod_b.reshape(n_layers, 1, n))


def _norm_matmul_kernel(x_ref, mod_ref, g_ref, w_ref, o_ref, h_ref, *, shift_row, scale_row):
    @pl.when(pl.program_id(2) == 0)
    def _():
        h_ref[...] = _modnorm(x_ref[...], g_ref[...], mod_ref, shift_row, scale_row).astype(BF16)

    o_ref[...] = jnp.dot(h_ref[...], w_ref[...], preferred_element_type=F32).astype(o_ref.dtype)


def _norm_matmul(x, mod, g, w, shift_row, scale_row, out_dtype, tm=1024, tn=1792):
    b, t, d = x.shape
    n = w.shape[1]
    tm = _pick(t, tm)
    tn = _pick(n, tn)
    return pl.pallas_call(
        functools.partial(_norm_matmul_kernel, shift_row=shift_row, scale_row=scale_row),
        grid=(b, t // tm, n // tn),
        in_specs=[pl.BlockSpec((None, tm, d), lambda bi, i, j: (bi, i, 0)),
                  _mod_spec(mod),
                  pl.BlockSpec((1, d), lambda bi, i, j: (0, 0)),
                  pl.BlockSpec((d, tn), lambda bi, i, j: (0, j))],
        out_specs=pl.BlockSpec((None, tm, tn), lambda bi, i, j: (bi, i, j)),
        out_shape=jax.ShapeDtypeStruct((b, t, n), out_dtype),
        scratch_shapes=[pltpu.VMEM((tm, d), BF16)],
        compiler_params=_cparams("parallel", "parallel", "arbitrary"),
        name="norm_matmul",
    )(x, mod, g.reshape(1, d), w)


def _proj_residual_kernel(*refs, n_lhs, gate_row):
    lhs = refs[:n_lhs]
    ws = refs[n_lhs:2 * n_lhs]
    x_ref, mod_ref, g_ref, o_ref = refs[2 * n_lhs:]
    y = jnp.dot(lhs[0][...], ws[0][...], preferred_element_type=F32)
    for a_ref, w_ref in zip(lhs[1:], ws[1:]):
        y += jnp.dot(a_ref[...], w_ref[...], preferred_element_type=F32)
    gate = mod_ref[gate_row:gate_row + 1, :]
    o_ref[...] = x_ref[...] + gate * (_rms(y) * g_ref[...])


def _proj_residual(lhs, ws, x, mod, g, gate_row, tm=512):
    b, t, d = x.shape
    tm = _pick(t, tm)
    n_lhs = len(lhs)
    in_specs = [pl.BlockSpec((None, tm, a.shape[-1]), lambda bi, i: (bi, i, 0)) for a in lhs]
    in_specs += [pl.BlockSpec(w.shape, lambda bi, i: (0, 0)) for w in ws]
    in_specs += [pl.BlockSpec((None, tm, d), lambda bi, i: (bi, i, 0)),
                 _mod_spec(mod),
                 pl.BlockSpec((1, d), lambda bi, i: (0, 0))]
    return pl.pallas_call(
        functools.partial(_proj_residual_kernel, n_lhs=n_lhs, gate_row=gate_row),
        grid=(b, t // tm),
        in_specs=in_specs,
        out_specs=pl.BlockSpec((None, tm, d), lambda bi, i: (bi, i, 0)),
        out_shape=jax.ShapeDtypeStruct((b, t, d), F32),
        compiler_params=_cparams("parallel", "parallel"),
        name="proj_residual",
    )(*lhs, *ws, x, mod, g.reshape(1, d))


def _ffn_kernel(x_ref, mod_ref, g_in_ref, g_out_ref, wa_ref, wb_ref, wo_ref, o_ref, h_ref, acc_ref):
    f = pl.program_id(2)

    @pl.when(f == 0)
    def _():
        h_ref[...] = _modnorm(x_ref[...], g_in_ref[...], mod_ref, 3, 4).astype(BF16)
        acc_ref[...] = jnp.zeros_like(acc_ref)

    h = h_ref[...]
    a = jnp.dot(h, wa_ref[...], preferred_element_type=F32)
    bb = jnp.dot(h, wb_ref[...], preferred_element_type=F32)
    acc_ref[...] += jnp.dot((_silu(a) * bb).astype(BF16), wo_ref[...], preferred_element_type=F32)

    @pl.when(f == pl.num_programs(2) - 1)
    def _():
        gate = mod_ref[5:6, :]
        o_ref[...] = x_ref[...] + gate * (_rms(acc_ref[...]) * g_out_ref[...])


def _ffn(x, mod, g_in, g_out, w_in, w_out, tm=512, tf=2816):
    b, t, d = x.shape
    fdim = w_out.shape[0]
    tm = _pick(t, tm)
    tf = _pick(fdim, tf)
    nf = fdim // tf
    return pl.pallas_call(
        _ffn_kernel,
        grid=(b, t // tm, nf),
        in_specs=[pl.BlockSpec((None, tm, d), lambda bi, i, f: (bi, i, 0)),
                  _mod_spec(mod),
                  pl.BlockSpec((1, d), lambda bi, i, f: (0, 0)),
                  pl.BlockSpec((1, d), lambda bi, i, f: (0, 0)),
                  pl.BlockSpec((d, tf), lambda bi, i, f: (0, f), pipeline_mode=pl.Buffered(1)),
                  pl.BlockSpec((d, tf), lambda bi, i, f: (0, f + nf), pipeline_mode=pl.Buffered(1)),
                  pl.BlockSpec((tf, d), lambda bi, i, f: (f, 0), pipeline_mode=pl.Buffered(1))],
        out_specs=pl.BlockSpec((None, tm, d), lambda bi, i, f: (bi, i, 0)),
        out_shape=jax.ShapeDtypeStruct((b, t, d), F32),
        scratch_shapes=[pltpu.VMEM((tm, d), BF16), pltpu.VMEM((tm, d), F32)],
        compiler_params=_cparams("parallel", "parallel", "arbitrary"),
        name="ffn",
    )(x, mod, g_in.reshape(1, d), g_out.reshape(1, d), w_in, w_in, w_out)


SCAN_HEADS = 4


def _head_slice(hh):
    return slice(hh * HEAD_DIM, (hh + 1) * HEAD_DIM)


def _finish_scan(of_ref, ob_ref, g_ref, y_ref, cs):
    def step(i, carry):
        start = pl.multiple_of(i * cs, cs)
        for hh in range(y_ref.shape[1] // HEAD_DIM):
            sl = _head_slice(hh)
            o = of_ref[pl.ds(start, cs), sl].astype(F32) + ob_ref[pl.ds(start, cs), sl].astype(F32)
            gate = g_ref[pl.ds(start, cs), sl].astype(F32)
            y_ref[pl.ds(start, cs), sl] = (_rms(o) * _silu(gate)).astype(y_ref.dtype)
        return carry

    lax.fori_loop(0, y_ref.shape[0] // cs, step, 0)


def _retention_kernel(lg_ref, qc_ref, kc_ref, vc_ref, gc_ref, qx_ref, kx_ref, vx_ref, gx_ref,
                      cos_ref, sin_ref, yc_ref, yx_ref,
                      ofc_ref, obc_ref, ofx_ref, obx_ref, s_ref, intra_ref, qd_ref, kd_ref):
    cs = RET_CHUNK
    hp = SCAN_HEADS
    head0 = pl.program_id(1) * hp
    row = lax.broadcasted_iota(jnp.int32, (cs, cs), 0).astype(F32)
    col = lax.broadcasted_iota(jnp.int32, (cs, cs), 1).astype(F32)
    pos = lax.broadcasted_iota(jnp.int32, (cs, HEAD_DIM), 0).astype(F32)
    k_scale = HEAD_DIM ** -0.5

    for hh in range(hp):
        for d in (0, 1):
            c = 2 * hh + d
            lg = lg_ref[d, head0 + hh]
            rel = (col - row) if d else (row - col)
            intra_ref[c] = jnp.where(rel >= 0, jnp.exp(lg * jnp.maximum(rel, 0.0)), 0.0)
            p = (cs - 1.0 - pos) if d else pos
            qd_ref[c] = jnp.exp(lg * (p + 1.0))
            kd_ref[c] = jnp.exp(lg * (cs - 1.0 - p))
    s_ref[...] = jnp.zeros_like(s_ref)

    for (q_ref, k_ref, v_ref, of_ref, ob_ref, rope) in (
            (qc_ref, kc_ref, vc_ref, ofc_ref, obc_ref, False),
            (qx_ref, kx_ref, vx_ref, ofx_ref, obx_ref, True)):
        n_chunks = q_ref.shape[0] // cs

        def step(i, carry, q_ref=q_ref, k_ref=k_ref, v_ref=v_ref, of_ref=of_ref, ob_ref=ob_ref,
                 rope=rope, n_chunks=n_chunks):
            chains = []
            for d in (0, 1):
                start = pl.multiple_of(((n_chunks - 1 - i) if d else i) * cs, cs)
                if rope:
                    cos = cos_ref[pl.ds(start, cs), :]
                    sin = sin_ref[pl.ds(start, cs), :]
                for hh in range(hp):
                    sl = _head_slice(hh)
                    q = q_ref[pl.ds(start, cs), sl].astype(F32)
                    k = k_ref[pl.ds(start, cs), sl].astype(F32) * k_scale
                    if rope:
                        q = q * cos + pltpu.roll(q, HEAD_DIM // 2, 1) * sin
                        k = k * cos + pltpu.roll(k, HEAD_DIM // 2, 1) * sin
                    chains.append(dict(d=d, hh=hh, c=2 * hh + d, start=start, sl=sl, q=q, k=k,
                                       v=v_ref[pl.ds(start, cs), sl], o_ref=ob_ref if d else of_ref))
            for ch in chains:
                ch["att"] = _mm_nt(ch["q"], ch["k"]) * intra_ref[ch["c"]]
            for ch in chains:
                c = ch["c"]
                ch["state"] = s_ref[c]
                ch["o_ref"][pl.ds(ch["start"], cs), ch["sl"]] = (
                    _mm(ch["att"], ch["v"]) + _mm(ch["q"] * qd_ref[c], ch["state"]))
            for ch in chains:
                c = ch["c"]
                c_dec = jnp.exp(jnp.full((1, HEAD_DIM), lg_ref[ch["d"], head0 + ch["hh"]] * cs, F32))
                s_ref[c] = ch["state"] * c_dec + _mm_tn(ch["k"] * kd_ref[c], ch["v"])
            return carry

        lax.fori_loop(0, n_chunks, step, 0)

    _finish_scan(ofc_ref, obc_ref, gc_ref, yc_ref, cs)
    _finish_scan(ofx_ref, obx_ref, gx_ref, yx_ref, cs)


def _retention(proj_c, proj_x, log_gamma, cos2, sin2):
    b, tc, _ = proj_c.shape
    tx = proj_x.shape[1]
    h, hp = RET_HEADS, SCAN_HEADS
    w = hp * HEAD_DIM

    def col(t, group):
        return pl.BlockSpec((None, t, w), lambda bi, hi, lg: (bi, 0, group * (h // hp) + hi),
                            pipeline_mode=pl.Buffered(1))

    n_chain = 2 * hp
    grid_spec = pltpu.PrefetchScalarGridSpec(
        num_scalar_prefetch=1,
        grid=(b, h // hp),
        in_specs=[col(tc, 0), col(tc, 1), col(tc, 2), col(tc, 3),
                  col(tx, 0), col(tx, 1), col(tx, 2), col(tx, 3),
                  pl.BlockSpec((tx, HEAD_DIM), lambda bi, hi, lg: (0, 0), pipeline_mode=pl.Buffered(1)),
                  pl.BlockSpec((tx, HEAD_DIM), lambda bi, hi, lg: (0, 0), pipeline_mode=pl.Buffered(1))],
        out_specs=[pl.BlockSpec((None, tc, w), lambda bi, hi, lg: (bi, 0, hi)),
                   pl.BlockSpec((None, tx, w), lambda bi, hi, lg: (bi, 0, hi))],
        scratch_shapes=[pltpu.VMEM((tc, w), F32), pltpu.VMEM((tc, w), F32),
                        pltpu.VMEM((tx, w), F32), pltpu.VMEM((tx, w), F32),
                        pltpu.VMEM((n_chain, HEAD_DIM, HEAD_DIM), F32),
                        pltpu.VMEM((n_chain, RET_CHUNK, RET_CHUNK), F32),
                        pltpu.VMEM((n_chain, RET_CHUNK, HEAD_DIM), F32),
                        pltpu.VMEM((n_chain, RET_CHUNK, HEAD_DIM), F32)])
    return pl.pallas_call(
        _retention_kernel,
        grid_spec=grid_spec,
        out_shape=[jax.ShapeDtypeStruct((b, tc, h * HEAD_DIM), BF16),
                   jax.ShapeDtypeStruct((b, tx, h * HEAD_DIM), BF16)],
        compiler_params=_cparams("parallel", "parallel"),
        name="retention",
    )(log_gamma, proj_c, proj_c, proj_c, proj_c, proj_x, proj_x, proj_x, proj_x, cos2, sin2)


def _hgrn_levels(cs):
    return [cs >> (i + 1) for i in range(int(math.log2(cs)))]


def _hgrn_constants(cs):
    t = np.arange(cs)[:, None]
    s = np.arange(cs)[None, :]
    tri = np.stack([(s <= t), (s >= t)]).astype(np.float32)
    masks = []
    for reverse in (False, True):
        lv = []
        for h in _hgrn_levels(cs):
            same = (t // (2 * h)) == (s // (2 * h))
            t_hi = (t % (2 * h)) >= h
            s_hi = (s % (2 * h)) >= h
            lv.append(same & (~t_hi & s_hi if reverse else t_hi & ~s_hi))
        lv.append(t == s)
        masks.append(np.stack(lv))
    return jnp.asarray(tri, BF16), jnp.asarray(np.stack(masks).astype(np.float32), BF16)


def _split3(x):
    hi = x.astype(BF16)
    r = x - hi.astype(F32)
    mid = r.astype(BF16)
    lo = (r - mid.astype(F32)).astype(BF16)
    return hi, mid, lo


def _neg_abs(x):
    bits = lax.bitcast_convert_type(x, jnp.uint32) | jnp.uint32(0x80000000)
    return lax.bitcast_convert_type(bits, F32)


def _level_reference(b, h, reverse):
    cs = b.shape[0]
    ref_local = h if reverse else h - 1
    if 2 * h >= 8:
        b3 = b.reshape(cs // (2 * h), 2 * h, b.shape[1])
        r = b3[:, ref_local:ref_local + 1, :]
        return jnp.broadcast_to(r, b3.shape).reshape(b.shape)
    local = lax.broadcasted_iota(jnp.int32, b.shape, 0) % (2 * h)
    out = b
    for m in range(2 * h):
        if m != ref_local:
            out = jnp.where(local == m, pltpu.roll(b, (m - ref_local) % cs, 0), out)
    return out


def _hgrn_kernel(lb_ref, tri_ref, mask_ref,
                 qc_ref, vc_ref, gc_ref, zfc_ref, zbc_ref,
                 qx_ref, vx_ref, gx_ref, zfx_ref, zbx_ref,
                 yc_ref, yx_ref, ofc_ref, obc_ref, ofx_ref, obx_ref, st_ref):
    cs = HGRN_CHUNK
    hp = SCAN_HEADS
    levels = _hgrn_levels(cs)
    st_ref[...] = jnp.zeros_like(st_ref)

    for (q_ref, v_ref, zf_ref, zb_ref, of_ref, ob_ref) in (
            (qc_ref, vc_ref, zfc_ref, zbc_ref, ofc_ref, obc_ref),
            (qx_ref, vx_ref, zfx_ref, zbx_ref, ofx_ref, obx_ref)):
        n_chunks = q_ref.shape[0] // cs

        def step(i, carry, q_ref=q_ref, v_ref=v_ref, zf_ref=zf_ref, zb_ref=zb_ref, of_ref=of_ref,
                 ob_ref=ob_ref, n_chunks=n_chunks):
            chains = []
            for d in (0, 1):
                start = pl.multiple_of(((n_chunks - 1 - i) if d else i) * cs, cs)
                z_ref = zb_ref if d else zf_ref
                for hh in range(hp):
                    sl = _head_slice(hh)
                    lb = lb_ref[hh]
                    q = _silu(q_ref[pl.ds(start, cs), sl].astype(F32))
                    f = lb + (1.0 - lb) * (1.0 / (1.0 + jnp.exp(-z_ref[pl.ds(start, cs), sl])))
                    chains.append(dict(d=d, c=2 * hh + d, start=start, sl=sl, q=q, k=1.0 - f,
                                       parts=_split3(jnp.log2(f)), v=v_ref[pl.ds(start, cs), sl],
                                       o_ref=ob_ref if d else of_ref))
            for ch in chains:
                tri = tri_ref[ch["d"]]
                hi, mid, lo = ch["parts"]
                ch["bsum"] = (jnp.dot(tri, hi, preferred_element_type=F32)
                              + jnp.dot(tri, mid, preferred_element_type=F32)
                              + jnp.dot(tri, lo, preferred_element_type=F32))
            for ch in chains:
                ch["qb"], ch["kb"] = ch["q"].astype(BF16), ch["k"].astype(BF16)
                ch["att"] = _mm_nt(ch["qb"], ch["kb"]).astype(BF16) * mask_ref[ch["d"], len(levels)]
            for li, h in enumerate(levels):
                for ch in chains:
                    ref = _level_reference(ch["bsum"], h, ch["d"] == 1)
                    w = jnp.exp2(_neg_abs(ch["bsum"] - ref)).astype(BF16)
                    ch["att"] += _mm_nt(ch["qb"] * w, ch["kb"] * w).astype(BF16) * mask_ref[ch["d"], li]
            for ch in chains:
                ch["state"] = st_ref[ch["c"]]
                o = _mm(ch["att"], ch["v"]) + _mm_nt(ch["q"] * jnp.exp2(ch["bsum"]), ch["state"])
                ch["o_ref"][pl.ds(ch["start"], cs), ch["sl"]] = o.astype(ch["o_ref"].dtype)
            for ch in chains:
                bsum = ch["bsum"]
                b_end = bsum[0:1, :] if ch["d"] else bsum[cs - 1:cs, :]
                st_ref[ch["c"]] = (ch["state"] * jnp.exp2(b_end)
                                   + _mm_tn(ch["v"], ch["k"] * jnp.exp2(b_end - bsum)))
            return carry

        lax.fori_loop(0, n_chunks, step, 0)

    _finish_scan(ofc_ref, obc_ref, gc_ref, yc_ref, cs)
    _finish_scan(ofx_ref, obx_ref, gx_ref, yx_ref, cs)


def _hgrn(proj_c, proj_x, z_c, z_x, lb, first):
    b, tc, _ = proj_c.shape
    tx = proj_x.shape[1]
    h, hp = HGRN_HEADS, SCAN_HEADS
    w = hp * HEAD_DIM
    tri, masks = _hgrn_constants(HGRN_CHUNK)

    def col(t, group):
        return pl.BlockSpec((None, t, w), lambda bi, hi: (bi, 0, (first + group * h) // hp + hi),
                            pipeline_mode=pl.Buffered(1))

    def zcol(t, group):
        return pl.BlockSpec((None, t, w), lambda bi, hi: (bi, 0, group * (h // hp) + hi),
                            pipeline_mode=pl.Buffered(1))

    ins = [col(tc, 0), col(tc, 1), col(tc, 2), zcol(tc, 0), zcol(tc, 1),
           col(tx, 0), col(tx, 1), col(tx, 2), zcol(tx, 0), zcol(tx, 1)]
    return pl.pallas_call(
        _hgrn_kernel,
        grid=(b, h // hp),
        in_specs=[pl.BlockSpec((hp, 1, HEAD_DIM), lambda bi, hi: (hi, 0, 0)),
                  pl.BlockSpec(tri.shape, lambda bi, hi: (0, 0, 0)),
                  pl.BlockSpec(masks.shape, lambda bi, hi: (0, 0, 0, 0))] + ins,
        out_specs=[pl.BlockSpec((None, tc, w), lambda bi, hi: (bi, 0, hi)),
                   pl.BlockSpec((None, tx, w), lambda bi, hi: (bi, 0, hi))],
        out_shape=[jax.ShapeDtypeStruct((b, tc, h * HEAD_DIM), BF16),
                   jax.ShapeDtypeStruct((b, tx, h * HEAD_DIM), BF16)],
        scratch_shapes=[pltpu.VMEM((tc, w), BF16), pltpu.VMEM((tc, w), BF16),
                        pltpu.VMEM((tx, w), BF16), pltpu.VMEM((tx, w), BF16),
                        pltpu.VMEM((2 * hp, HEAD_DIM, HEAD_DIM), F32)],
        compiler_params=_cparams("parallel", "parallel"),
        name="hgrn2",
    )(lb.reshape(h, 1, HEAD_DIM), tri, masks, proj_c, proj_c, proj_c, z_c, z_c,
      proj_x, proj_x, proj_x, z_x, z_x)


MLA_QW = 2 * LANE
MLA_SCORE_SCALE = (MLA_NOPE + MLA_ROPE) ** -0.5 * math.log2(math.e)
MLA_DOWN_PAD = MLA_Q_RANK + MLA_KV_RANK + LANE


def _mla_proj_kernel(x_ref, mod_ref, g_ref, wd_ref, gq_ref, gkv_ref, wq_ref, wkv_ref, cos_ref, sin_ref,
                     *out_refs, rope, want_q):
    h = _modnorm(x_ref[...], g_ref[...], mod_ref, 0, 1).astype(BF16)
    down = jnp.dot(h, wd_ref[...], preferred_element_type=F32)
    if rope:
        cos = cos_ref[...]
        sin = sin_ref[...]

    def rot(v):
        return v * cos + pltpu.roll(v, LANE // 2, 1) * sin if rope else v

    if want_q:
        q_ref, kn_ref, kr_ref, v_ref = out_refs
        qn = (_rms(down[:, :MLA_Q_RANK]) * gq_ref[...]).astype(BF16)
        q = jnp.dot(qn, wq_ref[...], preferred_element_type=F32) * MLA_SCORE_SCALE
        for hd in range(MLA_HEADS):
            lo = hd * MLA_QW
            q_ref[:, lo:lo + LANE] = q[:, lo:lo + LANE].astype(q_ref.dtype)
            q_ref[:, lo + LANE:lo + 2 * LANE] = rot(q[:, lo + LANE:lo + 2 * LANE]).astype(q_ref.dtype)
    else:
        kn_ref, kr_ref, v_ref = out_refs
    kvn = (_rms(down[:, MLA_Q_RANK:MLA_Q_RANK + MLA_KV_RANK]) * gkv_ref[...]).astype(BF16)
    kv = jnp.dot(kvn, wkv_ref[...], preferred_element_type=F32)
    nk = MLA_HEADS * MLA_NOPE
    kn_ref[...] = kv[:, :nk].astype(kn_ref.dtype)
    v_ref[...] = kv[:, nk:].astype(v_ref.dtype)
    kr_ref[...] = rot(down[:, MLA_Q_RANK + MLA_KV_RANK:]).astype(kr_ref.dtype)


def _mla_proj(x, mod, g, wd, gq, gkv, wq, wkv, cos, sin, rope, want_q, tm=512):
    b, t, d = x.shape
    tm = _pick(t, tm)
    const = lambda bi, i: (0, 0)
    tok = lambda bi, i: (bi, i, 0)
    nk, nv = MLA_HEADS * MLA_NOPE, MLA_HEADS * MLA_V
    out_shape = [jax.ShapeDtypeStruct((b, t, nk), BF16), jax.ShapeDtypeStruct((b, t, LANE), BF16),
                 jax.ShapeDtypeStruct((b, t, nv), BF16)]
    out_specs = [pl.BlockSpec((None, tm, nk), tok), pl.BlockSpec((None, tm, LANE), tok),
                 pl.BlockSpec((None, tm, nv), tok)]
    if want_q:
        out_shape = [jax.ShapeDtypeStruct((b, t, MLA_HEADS * MLA_QW), BF16)] + out_shape
        out_specs = [pl.BlockSpec((None, tm, MLA_HEADS * MLA_QW), tok)] + out_specs
    return pl.pallas_call(
        functools.partial(_mla_proj_kernel, rope=rope, want_q=want_q),
        grid=(b, t // tm),
        in_specs=[pl.BlockSpec((None, tm, d), tok), _mod_spec(mod), pl.BlockSpec((1, d), const),
                  pl.BlockSpec(wd.shape, const), pl.BlockSpec((1, MLA_Q_RANK), const),
                  pl.BlockSpec((1, MLA_KV_RANK), const), pl.BlockSpec(wq.shape, const),
                  pl.BlockSpec(wkv.shape, const),
                  pl.BlockSpec((tm, LANE), lambda bi, i: (i, 0)),
                  pl.BlockSpec((tm, LANE), lambda bi, i: (i, 0))],
        out_specs=out_specs,
        out_shape=out_shape,
        compiler_params=_cparams("parallel", "parallel"),
        name="mla_proj",
    )(x, mod, g.reshape(1, d), wd, gq.reshape(1, -1), gkv.reshape(1, -1), wq, wkv, cos, sin)


def _lane_fold(x, op):
    out = x[:, :LANE]
    for j in range(1, x.shape[1] // LANE):
        out = op(out, x[:, j * LANE:(j + 1) * LANE])
    return out


def _attention_kernel(q_ref, knc_ref, krc_ref, vc_ref, knx_ref, krx_ref, vx_ref, o_ref,
                      kc_ref, kx_ref, vc2_ref, vx2_ref, sc0_ref, sc1_ref, sx0_ref, sx1_ref, m0_ref, m1_ref,
                      *, tk):
    step = pl.program_id(2)

    @pl.when(step == 0)
    def _():
        kc_ref[:, :LANE] = knc_ref[...]
        kc_ref[:, LANE:] = krc_ref[...]
        kx_ref[:, :LANE] = knx_ref[...]
        kx_ref[:, LANE:] = krx_ref[...]
        vc2_ref[:, :LANE] = vc_ref[...]
        vx2_ref[:, :LANE] = vx_ref[...]
        for ref in (vc2_ref, vx2_ref):
            lane = lax.broadcasted_iota(jnp.int32, (ref.shape[0], LANE), 1)
            ref[:, LANE:] = jnp.where(lane == 0, 1.0, 0.0).astype(ref.dtype)
        sc1_ref[...] = jnp.zeros(sc1_ref.shape, F32)
        sx1_ref[...] = jnp.zeros(sx1_ref.shape, F32)
        m1_ref[...] = jnp.zeros(m1_ref.shape, F32)

    n_blocks = kx_ref.shape[0] // tk
    nt = (((1,), (1,)), ((), ()))

    def body(sc_cur, sx_cur, m_cur, sc_prev, sx_prev, m_prev):
        q = q_ref[...]
        m = jnp.max(m_prev[...], axis=-1, keepdims=True)
        s = lax.dot_general(q, kc_ref[...], nt, preferred_element_type=F32)
        sc_cur[...] = s
        m_part = _lane_fold(s, jnp.maximum)
        p = jnp.exp2(sc_prev[...] - m)
        acc = jnp.dot(p.astype(BF16), vc2_ref[...], preferred_element_type=F32)
        for i in range(n_blocks):
            s = lax.dot_general(q, kx_ref[i * tk:(i + 1) * tk, :], nt, preferred_element_type=F32)
            sx_cur[i] = s
            m_part = jnp.maximum(m_part, _lane_fold(s, jnp.maximum))
            p = jnp.exp2(sx_prev[i] - m)
            acc = acc + jnp.dot(p.astype(BF16), vx2_ref[i * tk:(i + 1) * tk, :],
                                preferred_element_type=F32)
        m_cur[...] = m_part
        o_ref[...] = (acc[:, :MLA_V] / acc[:, MLA_V:MLA_V + 1]).astype(o_ref.dtype)

    @pl.when(step % 2 == 0)
    def _():
        body(sc0_ref, sx0_ref, m0_ref, sc1_ref, sx1_ref, m1_ref)

    @pl.when(step % 2 == 1)
    def _():
        body(sc1_ref, sx1_ref, m1_ref, sc0_ref, sx0_ref, m0_ref)


def _attention(q, kn_c, kr_c, v_c, kn_x, kr_x, v_x, tq=256, tk=256):
    b, t, _ = q.shape
    tc = kn_c.shape[1]
    tq = _pick(t, tq)
    tk = _pick(t, tk)
    nq = t // tq
    head_c = lambda bi, hi, i: (bi, 0, hi)
    shared = lambda bi, hi, i: (bi, 0, 0)
    return pl.pallas_call(
        functools.partial(_attention_kernel, tk=tk),
        grid=(b, MLA_HEADS, nq + 1),
        in_specs=[pl.BlockSpec((None, tq, MLA_QW), lambda bi, hi, i: (bi, jnp.minimum(i, nq - 1), hi)),
                  pl.BlockSpec((None, tc, LANE), head_c), pl.BlockSpec((None, tc, LANE), shared),
                  pl.BlockSpec((None, tc, LANE), head_c),
                  pl.BlockSpec((None, t, LANE), head_c), pl.BlockSpec((None, t, LANE), shared),
                  pl.BlockSpec((None, t, LANE), head_c)],
        out_specs=pl.BlockSpec((None, tq, MLA_V), lambda bi, hi, i: (bi, jnp.maximum(i - 1, 0), hi)),
        out_shape=jax.ShapeDtypeStruct((b, t, MLA_HEADS * MLA_V), BF16),
        scratch_shapes=[pltpu.VMEM((tc, MLA_QW), BF16), pltpu.VMEM((t, MLA_QW), BF16),
                        pltpu.VMEM((tc, 2 * MLA_V), BF16), pltpu.VMEM((t, 2 * MLA_V), BF16),
                        pltpu.VMEM((tq, tc), F32), pltpu.VMEM((tq, tc), F32),
                        pltpu.VMEM((t // tk, tq, tk), F32), pltpu.VMEM((t // tk, tq, tk), F32),
                        pltpu.VMEM((tq, LANE), F32), pltpu.VMEM((tq, LANE), F32)],
        compiler_params=_cparams("parallel", "parallel", "arbitrary"),
        name="mla_attention",
    )(q, kn_c, kr_c, v_c, kn_x, kr_x, v_x)


def _router_kernel(x_ref, mod_ref, g_ref, wr_ref, h_ref, w_ref, e_ref):
    h = _modnorm(x_ref[...], g_ref[...], mod_ref, 3, 4)
    h_ref[...] = _pack_halves(h)
    logits = jnp.dot(h, wr_ref[...], preferred_element_type=F32, precision=lax.Precision.HIGHEST)
    lane = lax.broadcasted_iota(jnp.int32, logits.shape, 1)
    neg = jnp.float32(-jnp.inf)
    lg = jnp.where(lane < N_EXPERTS, logits, neg)
    m1 = jnp.max(lg, axis=-1, keepdims=True)
    i1 = jnp.min(jnp.where(lg == m1, lane, LANE), axis=-1, keepdims=True)
    lg2 = jnp.where(lane == i1, neg, lg)
    m2 = jnp.max(lg2, axis=-1, keepdims=True)
    i2 = jnp.min(jnp.where(lg2 == m2, lane, LANE), axis=-1, keepdims=True)
    e2 = jnp.exp(m2 - m1)
    w1 = 1.0 / (1.0 + e2)
    w2 = e2 / (1.0 + e2)
    w_ref[...] = jnp.where(lane == 0, w1, jnp.where(lane == 1, w2, 0.0))
    e_ref[...] = jnp.where(lane == 0, i1, jnp.where(lane == 1, i2, 0))


def _router(x, mod, g, router_pad, tm=512):
    b, t, d = x.shape
    tm = _pick(t, tm)
    tok = lambda bi, i: (bi, i, 0)
    const = lambda bi, i: (0, 0)
    return pl.pallas_call(
        _router_kernel,
        grid=(b, t // tm),
        in_specs=[pl.BlockSpec((None, tm, d), tok), _mod_spec(mod), pl.BlockSpec((1, d), const),
                  pl.BlockSpec((d, LANE), const)],
        out_specs=[pl.BlockSpec((None, tm, d // 2), tok), pl.BlockSpec((None, tm, LANE), tok),
                   pl.BlockSpec((None, tm, LANE), tok)],
        out_shape=[jax.ShapeDtypeStruct((b, t, d // 2), jnp.int32), jax.ShapeDtypeStruct((b, t, LANE), F32),
                   jax.ShapeDtypeStruct((b, t, LANE), jnp.int32)],
        compiler_params=_cparams("parallel", "parallel"),
        name="moe_router",
    )(x, mod, g.reshape(1, d), router_pad)


SC_GATHER_ROWS = 64


def _gather_rows(table, idx):
    info = plsc.get_sparse_core_info()
    n_workers = info.num_cores * info.num_subcores
    n, width = idx.shape[0], table.shape[1]
    chunk = SC_GATHER_ROWS
    assert n % (n_workers * chunk) == 0
    n_chunks = n // (n_workers * chunk)
    mesh = plsc.VectorSubcoreMesh(core_axis_name="c", subcore_axis_name="s")

    def body(table_hbm, idx_hbm, out_hbm, idx_v, rows_v, sem):
        wid = lax.axis_index("s") * info.num_cores + lax.axis_index("c")
        pltpu.sync_copy(idx_hbm.at[wid], idx_v)

        @pl.loop(0, n_chunks)
        def _(j):
            pltpu.async_copy(table_hbm.at[idx_v.at[j]], rows_v, sem).wait()
            pltpu.sync_copy(rows_v, out_hbm.at[pl.ds((wid * n_chunks + j) * chunk, chunk)])

    return pl.kernel(
        body,
        out_type=jax.ShapeDtypeStruct((n, width), table.dtype),
        mesh=mesh,
        scratch_types=[pltpu.VMEM((n_chunks, chunk), jnp.int32), pltpu.VMEM((chunk, width), table.dtype),
                       pltpu.SemaphoreType.DMA],
        name="sc_gather_rows",
    )(table, idx.reshape(n_workers, n_chunks, chunk))


def _scatter_rows_twice(table, idx0, idx1, n_out):
    info = plsc.get_sparse_core_info()
    n_workers = info.num_cores * info.num_subcores
    n, width = table.shape
    chunk = SC_GATHER_ROWS
    assert n % (n_workers * chunk) == 0
    n_chunks = n // (n_workers * chunk)
    mesh = plsc.VectorSubcoreMesh(core_axis_name="c", subcore_axis_name="s")

    def body(table_hbm, i0_hbm, i1_hbm, out_hbm, i0_v, i1_v, rows_v):
        wid = lax.axis_index("s") * info.num_cores + lax.axis_index("c")
        pltpu.sync_copy(i0_hbm.at[wid], i0_v)
        pltpu.sync_copy(i1_hbm.at[wid], i1_v)

        @pl.loop(0, n_chunks)
        def _(j):
            pltpu.sync_copy(table_hbm.at[pl.ds((wid * n_chunks + j) * chunk, chunk)], rows_v)
            pltpu.sync_copy(rows_v, out_hbm.at[i0_v.at[j]])
            pltpu.sync_copy(rows_v, out_hbm.at[i1_v.at[j]])

    shape3 = (n_workers, n_chunks, chunk)
    return pl.kernel(
        body,
        out_type=jax.ShapeDtypeStruct((n_out, width), table.dtype),
        mesh=mesh,
        scratch_types=[pltpu.VMEM((n_chunks, chunk), jnp.int32), pltpu.VMEM((n_chunks, chunk), jnp.int32),
                       pltpu.VMEM((chunk, width), table.dtype)],
        name="sc_scatter_rows",
    )(table, idx0.reshape(shape3), idx1.reshape(shape3))


def _pack_halves(x):
    k = x.shape[1] // 2
    bits = lambda v: lax.bitcast_convert_type(v.astype(BF16).astype(F32), jnp.uint32)
    word = lax.shift_right_logical(bits(x[:, :k]), jnp.uint32(16)) | bits(x[:, k:])
    return lax.bitcast_convert_type(word, jnp.int32)


def _unpack_halves(w):
    u = lax.bitcast_convert_type(w, jnp.uint32)
    lo = lax.bitcast_convert_type(lax.shift_left(u, jnp.uint32(16)), F32)
    hi = lax.bitcast_convert_type(u & jnp.uint32(0xFFFF0000), F32)
    return jnp.concatenate([lo, hi], axis=-1)


def _expert_ffn_kernel(te_ref, tv_ref, x_ref, wa_ref, wb_ref, wo_ref, o_ref, acc_ref):
    j = pl.program_id(0)
    f = pl.program_id(1)
    last = pl.num_programs(1) - 1
    valid = tv_ref[j] != 0

    @pl.when(valid)
    def _():
        @pl.when(f == 0)
        def _():
            acc_ref[...] = jnp.zeros_like(acc_ref)

        h = _unpack_halves(x_ref[...]).astype(BF16)
        a = jnp.dot(h, wa_ref[...], preferred_element_type=F32)
        bb = jnp.dot(h, wb_ref[...], preferred_element_type=F32)
        acc_ref[...] += jnp.dot((_silu(a) * bb).astype(BF16), wo_ref[...], preferred_element_type=F32)

        @pl.when(f == last)
        def _():
            o_ref[...] = _pack_halves(acc_ref[...])

    @pl.when(jnp.logical_not(valid) & (f == last))
    def _():
        o_ref[...] = jnp.zeros_like(o_ref)


def _expert_ffn(xs, tile_expert, tile_valid, w_in, w_out, tm, tf=2816):
    p = xs.shape[0]
    n_exp, fdim, d = w_out.shape
    tf = _pick(fdim, tf)
    nf = fdim // tf

    def f_eff(j, f, tv):
        return jnp.where(tv[j] != 0, f, nf - 1)

    grid_spec = pltpu.PrefetchScalarGridSpec(
        num_scalar_prefetch=2,
        grid=(p // tm, nf),
        in_specs=[pl.BlockSpec((tm, d // 2), lambda j, f, te, tv: (j, 0)),
                  pl.BlockSpec((None, d, tf), lambda j, f, te, tv: (te[j], 0, f_eff(j, f, tv)),
                               pipeline_mode=pl.Buffered(1)),
                  pl.BlockSpec((None, d, tf), lambda j, f, te, tv: (te[j], 0, f_eff(j, f, tv) + nf),
                               pipeline_mode=pl.Buffered(1)),
                  pl.BlockSpec((None, tf, d), lambda j, f, te, tv: (te[j], f_eff(j, f, tv), 0),
                               pipeline_mode=pl.Buffered(1))],
        out_specs=pl.BlockSpec((tm, d // 2), lambda j, f, te, tv: (j, 0)),
        scratch_shapes=[pltpu.VMEM((tm, d), F32)])
    return pl.pallas_call(
        _expert_ffn_kernel,
        grid_spec=grid_spec,
        out_shape=jax.ShapeDtypeStruct((p, d // 2), jnp.int32),
        compiler_params=_cparams("parallel", "arbitrary"),
        name="moe_expert_ffn",
    )(tile_expert, tile_valid, xs, w_in, w_in, w_out)


def _moe_combine_kernel(x_ref, y0_ref, y1_ref, w_ref, mod_ref, g_ref, o_ref):
    y = w_ref[:, 0:1] * _unpack_halves(y0_ref[...]) + w_ref[:, 1:2] * _unpack_halves(y1_ref[...])
    o_ref[...] = x_ref[...] + mod_ref[5:6, :] * (_rms(y) * g_ref[...])


def _moe_combine(x, y2, w, mod, g, tm=512):
    b, t, d = x.shape
    tm = _pick(t, tm)
    tok = lambda bi, i: (bi, i, 0)
    return pl.pallas_call(
        _moe_combine_kernel,
        grid=(b, t // tm),
        in_specs=[pl.BlockSpec((None, tm, d), tok),
                  pl.BlockSpec((None, None, tm, d // 2), lambda bi, i: (0, bi, i, 0)),
                  pl.BlockSpec((None, None, tm, d // 2), lambda bi, i: (1, bi, i, 0)),
                  pl.BlockSpec((None, tm, LANE), tok), _mod_spec(mod),
                  pl.BlockSpec((1, d), lambda bi, i: (0, 0))],
        out_specs=pl.BlockSpec((None, tm, d), tok),
        out_shape=jax.ShapeDtypeStruct((b, t, d), F32),
        compiler_params=_cparams("parallel", "parallel"),
        name="moe_combine",
    )(x, y2, y2, w, mod, g.reshape(1, d))


def _moe(x, mod, g_in, g_out, router_pad, w_in, w_out, tm=512):
    b, t, d = x.shape
    n_exp = w_out.shape[0]
    n = b * t
    h, w, e = _router(x, mod, g_in, router_pad)
    e_flat = jnp.concatenate([e[..., 0].reshape(n), e[..., 1].reshape(n)])
    onehot = (e_flat[:, None] == jnp.arange(n_exp, dtype=jnp.int32)).astype(jnp.int32)
    csum = jnp.cumsum(onehot, axis=0)
    rank = jnp.sum((csum - onehot) * onehot, axis=-1)
    counts = csum[-1]
    ends = jnp.cumsum((counts + tm - 1) // tm * tm)
    dest = (ends - (counts + tm - 1) // tm * tm)[e_flat] + rank
    p = 2 * n + n_exp * tm
    tile_start = jnp.arange(p // tm, dtype=jnp.int32) * tm
    tile_expert = jnp.minimum(jnp.searchsorted(ends, tile_start, side="right"), n_exp - 1).astype(jnp.int32)
    tile_valid = (tile_start < ends[-1]).astype(jnp.int32)

    xs = _scatter_rows_twice(h.reshape(n, d // 2), dest[:n], dest[n:], p)
    ys = _expert_ffn(xs, tile_expert, tile_valid, w_in, w_out, tm)
    y2 = _gather_rows(ys, dest)
    return _moe_combine(x, y2.reshape(2, b, t, d // 2), w, mod, g_out)


def _axial_angles(n_tokens, dim):
    t = jnp.arange(n_tokens)
    rows = (t // GRID_W).astype(F32)
    cols = (t % GRID_W).astype(F32)
    n_freq = dim // 4
    inv = ROPE_BASE ** (-jnp.arange(n_freq, dtype=F32) / n_freq)
    return jnp.concatenate([rows[:, None] * inv, cols[:, None] * inv], axis=-1)


def _rope_tables(n_tokens, dim):
    ang = _axial_angles(n_tokens, dim)
    cos, sin = jnp.cos(ang), jnp.sin(ang)
    if dim == LANE:
        return jnp.concatenate([cos, cos], -1), jnp.concatenate([-sin, sin], -1)
    z = jnp.zeros_like(cos)
    return jnp.concatenate([cos, z, cos, z], -1), jnp.concatenate([-sin, z, sin, z], -1)


def _spread_rope_cols(w):
    half = MLA_ROPE // 2
    z = jnp.zeros(w.shape[:-1] + (half,), w.dtype)
    return jnp.concatenate([w[..., :half], z, w[..., half:], z], axis=-1)


def kernel(x, c, ctx, c_ctx, mod_w, mod_b, norm_g, mix_in_w, ret_decay_logit, hgrn_lb_logit, mix_out_w,
           ffn_in_w, ffn_out_w, mla_down_w, mla_q_norm_g, mla_kv_norm_g, mla_uq_w, mla_ukv_w, mla_out_w,
           router_w, moe_in_w, moe_out_w):
    batch, seq, d = x.shape
    assert mod_w.shape[0] == 2, "two layers: retention/HGRN2 then MLA/MoE"

    rows = -(-(batch + 1) // 8) * 8
    cond = jnp.zeros((rows, d), F32).at[:batch].set(c).at[batch].set(c_ctx)
    mod = _modulation(cond, mod_w, mod_b)
    mod_x = [mod[l, :batch].reshape(batch, 6, d) for l in range(2)]
    mod_c = [mod[l, batch:batch + 1].reshape(1, 6, d) for l in range(2)]

    g = norm_g[0]
    ret_w = RET_HEADS * HEAD_DIM
    z_lo, z_hi = 5 * ret_w, 7 * ret_w
    w_in = mix_in_w[0].astype(BF16)
    w_main = jnp.concatenate([w_in[:, :z_lo], w_in[:, z_hi:]], axis=1)
    w_z = w_in[:, z_lo:z_hi]
    proj_x = _norm_matmul(x, mod_x[0], g[0], w_main, 0, 1, BF16)
    proj_c = _norm_matmul(ctx, mod_c[0], g[0], w_main, 0, 1, BF16)
    z_x = _norm_matmul(x, mod_x[0], g[0], w_z, 0, 1, F32)
    z_c = _norm_matmul(ctx, mod_c[0], g[0], w_z, 0, 1, F32)
    cos2, sin2 = _rope_tables(seq, HEAD_DIM)
    log_gamma = jax.nn.log_sigmoid(ret_decay_logit[0].astype(F32))
    lb = jnp.cumsum(jax.nn.softmax(hgrn_lb_logit.astype(F32), axis=0), axis=0)[0]
    ret_c, ret_x = _retention(proj_c, proj_x, log_gamma, cos2, sin2)
    hg_c, hg_x = _hgrn(proj_c, proj_x, z_c, z_x, lb, first=4 * RET_HEADS)
    w_out = mix_out_w[0].astype(BF16)
    w_outs = [w_out[:ret_w], w_out[ret_w:]]
    x = _proj_residual([ret_x, hg_x], w_outs, x, mod_x[0], g[1], 2)
    ctx = _proj_residual([ret_c, hg_c], w_outs, ctx, mod_c[0], g[1], 2)
    f_in, f_out = ffn_in_w[0].astype(BF16), ffn_out_w[0].astype(BF16)
    x = _ffn(x, mod_x[0], g[2], g[3], f_in, f_out)
    ctx = _ffn(ctx, mod_c[0], g[2], g[3], f_in, f_out)

    g = norm_g[1]
    dq = MLA_NOPE + MLA_ROPE
    wd = mla_down_w[0]
    wd = jnp.concatenate([wd[:, :MLA_Q_RANK + MLA_KV_RANK],
                          _spread_rope_cols(wd[:, MLA_Q_RANK + MLA_KV_RANK:])], axis=-1).astype(BF16)
    wq = mla_uq_w[0].reshape(MLA_Q_RANK, MLA_HEADS, dq)
    wq = jnp.concatenate([wq[..., :MLA_NOPE], _spread_rope_cols(wq[..., MLA_NOPE:])], axis=-1)
    wq = wq.reshape(MLA_Q_RANK, MLA_HEADS * MLA_QW).astype(BF16)
    wkv = mla_ukv_w[0].reshape(MLA_KV_RANK, MLA_HEADS, MLA_NOPE + MLA_V)
    wkv = jnp.concatenate([wkv[..., :MLA_NOPE].reshape(MLA_KV_RANK, -1),
                           wkv[..., MLA_NOPE:].reshape(MLA_KV_RANK, -1)], axis=-1).astype(BF16)
    cos_m, sin_m = _rope_tables(seq, MLA_ROPE)
    q, kn_x, kr_x, v_x = _mla_proj(x, mod_x[1], g[0], wd, mla_q_norm_g[0], mla_kv_norm_g[0], wq, wkv,
                                   cos_m, sin_m, rope=True, want_q=True)
    tc = ctx.shape[1]
    kn_c, kr_c, v_c = _mla_proj(ctx, mod_c[1], g[0], wd, mla_q_norm_g[0], mla_kv_norm_g[0], wq, wkv,
                                cos_m[:tc], sin_m[:tc], rope=False, want_q=False)
    att = _attention(q, kn_c, kr_c, v_c, kn_x, kr_x, v_x)
    x = _proj_residual([att], [mla_out_w[0].astype(BF16)], x, mod_x[1], g[1], 2)
    router_pad = jnp.zeros((d, LANE), F32).at[:, :N_EXPERTS].set(router_w[0])
    x = _moe(x, mod_x[1], g[2], g[3], router_pad, moe_in_w[0].astype(BF16), moe_out_w[0].astype(BF16))
    return x
```

```python
import functools
import math

import jax
import jax.numpy as jnp
import numpy as np
from jax import lax
from jax.experimental import pallas as pl
from jax.experimental.pallas import tpu as pltpu
from jax.experimental.pallas import tpu_sc as plsc

F32 = jnp.float32
BF16 = jnp.bfloat16

NORM_EPS = 1e-6
ROPE_BASE = 10000.0
GRID_W = 64
LANE = 128

RET_HEADS = 4
HGRN_HEADS = 4
HEAD_DIM = 128
RET_CHUNK = 128
HGRN_CHUNK = 128

MLA_HEADS = 8
MLA_NOPE = 128
MLA_ROPE = 64
MLA_V = 128
MLA_Q_RANK = 384
MLA_KV_RANK = 256
N_EXPERTS = 8

VMEM_LIMIT = 56 * 1024 * 1024


def _cparams(*sem):
    return pltpu.CompilerParams(dimension_semantics=sem, vmem_limit_bytes=VMEM_LIMIT)


def _mm(a, b):
    return jnp.dot(a.astype(BF16), b.astype(BF16), preferred_element_type=F32)


def _mm_nt(a, b):
    return lax.dot_general(a.astype(BF16), b.astype(BF16), (((1,), (1,)), ((), ())),
                           preferred_element_type=F32)


def _mm_tn(a, b):
    return lax.dot_general(a.astype(BF16), b.astype(BF16), (((0,), (0,)), ((), ())),
                           preferred_element_type=F32)


def _rms(x):
    return x * lax.rsqrt(jnp.mean(x * x, axis=-1, keepdims=True) + NORM_EPS)


def _silu(x):
    return x * (1.0 / (1.0 + jnp.exp(-x)))


def _modnorm(x, g, mod_ref, shift_row, scale_row):
    shift = mod_ref[shift_row:shift_row + 1, :]
    scale = mod_ref[scale_row:scale_row + 1, :]
    return _rms(x) * g * (1.0 + scale) + shift


def _mod_spec(mod):
    d = mod.shape[-1]
    if mod.shape[0] == 1:
        return pl.BlockSpec((None, 6, d), lambda b, *_: (0, 0, 0))
    return pl.BlockSpec((None, 6, d), lambda b, *_: (b, 0, 0))


def _pick(n, pref):
    t = min(pref, n)
    while n % t:
        t -= LANE
    return t


def _mod_kernel(c_ref, w_ref, b_ref, o_ref):
    a = _silu(c_ref[...])
    o_ref[...] = jnp.dot(a, w_ref[...], preferred_element_type=F32,
                         precision=lax.Precision.HIGHEST) + b_ref[...]


def _modulation(cond, mod_w, mod_b):
    n_layers, d, n = mod_w.shape
    r = cond.shape[0]
    tn = _pick(n, 1536)
    return pl.pallas_call(
        _mod_kernel,
        grid=(n_layers, n // tn),
        in_specs=[pl.BlockSpec((r, d), lambda l, j: (0, 0)),
                  pl.BlockSpec((None, d, tn), lambda l, j: (l, 0, j)),
                  pl.BlockSpec((None, 1, tn), lambda l, j: (l, 0, j))],
        out_specs=pl.BlockSpec((None, r, tn), lambda l, j: (l, 0, j)),
        out_shape=jax.ShapeDtypeStruct((n_layers, r, n), F32),
        compiler_params=_cparams("parallel", "parallel"),
        name="modulation",
    )(cond, mod_w, mod_b.reshape(n_layers, 1, n))


def _norm_matmul_kernel(x_ref, mod_ref, g_ref, w_ref, o_ref, h_ref, *, shift_row, scale_row):
    @pl.when(pl.program_id(2) == 0)
    def _():
        h_ref[...] = _modnorm(x_ref[...], g_ref[...], mod_ref, shift_row, scale_row).astype(BF16)

    o_ref[...] = jnp.dot(h_ref[...], w_ref[...], preferred_element_type=F32).astype(o_ref.dtype)


def _norm_matmul(x, mod, g, w, shift_row, scale_row, out_dtype, tm=1024, tn=1792):
    b, t, d = x.shape
    n = w.shape[1]
    tm = _pick(t, tm)
    tn = _pick(n, tn)
    return pl.pallas_call(
        functools.partial(_norm_matmul_kernel, shift_row=shift_row, scale_row=scale_row),
        grid=(b, t // tm, n // tn),
        in_specs=[pl.BlockSpec((None, tm, d), lambda bi, i, j: (bi, i, 0)),
                  _mod_spec(mod),
                  pl.BlockSpec((1, d), lambda bi, i, j: (0, 0)),
                  pl.BlockSpec((d, tn), lambda bi, i, j: (0, j))],
        out_specs=pl.BlockSpec((None, tm, tn), lambda bi, i, j: (bi, i, j)),
        out_shape=jax.ShapeDtypeStruct((b, t, n), out_dtype),
        scratch_shapes=[pltpu.VMEM((tm, d), BF16)],
        compiler_params=_cparams("parallel", "parallel", "arbitrary"),
        name="norm_matmul",
    )(x, mod, g.reshape(1, d), w)


def _proj_residual_kernel(*refs, n_lhs, gate_row):
    lhs = refs[:n_lhs]
    ws = refs[n_lhs:2 * n_lhs]
    x_ref, mod_ref, g_ref, o_ref = refs[2 * n_lhs:]
    y = jnp.dot(lhs[0][...], ws[0][...], preferred_element_type=F32)
    for a_ref, w_ref in zip(lhs[1:], ws[1:]):
        y += jnp.dot(a_ref[...], w_ref[...], preferred_element_type=F32)
    gate = mod_ref[gate_row:gate_row + 1, :]
    o_ref[...] = x_ref[...] + gate * (_rms(y) * g_ref[...])


def _proj_residual(lhs, ws, x, mod, g, gate_row, tm=512):
    b, t, d = x.shape
    tm = _pick(t, tm)
    n_lhs = len(lhs)
    in_specs = [pl.BlockSpec((None, tm, a.shape[-1]), lambda bi, i: (bi, i, 0)) for a in lhs]
    in_specs += [pl.BlockSpec(w.shape, lambda bi, i: (0, 0)) for w in ws]
    in_specs += [pl.BlockSpec((None, tm, d), lambda bi, i: (bi, i, 0)),
                 _mod_spec(mod),
                 pl.BlockSpec((1, d), lambda bi, i: (0, 0))]
    return pl.pallas_call(
        functools.partial(_proj_residual_kernel, n_lhs=n_lhs, gate_row=gate_row),
        grid=(b, t // tm),
        in_specs=in_specs,
        out_specs=pl.BlockSpec((None, tm, d), lambda bi, i: (bi, i, 0)),
        out_shape=jax.ShapeDtypeStruct((b, t, d), F32),
        compiler_params=_cparams("parallel", "parallel"),
        name="proj_residual",
    )(*lhs, *ws, x, mod, g.reshape(1, d))


def _ffn_kernel(x_ref, mod_ref, g_in_ref, g_out_ref, wa_ref, wb_ref, wo_ref, o_ref, h_ref, acc_ref):
    f = pl.program_id(2)

    @pl.when(f == 0)
    def _():
        h_ref[...] = _modnorm(x_ref[...], g_in_ref[...], mod_ref, 3, 4).astype(BF16)
        acc_ref[...] = jnp.zeros_like(acc_ref)

    h = h_ref[...]
    a = jnp.dot(h, wa_ref[...], preferred_element_type=F32)
    bb = jnp.dot(h, wb_ref[...], preferred_element_type=F32)
    acc_ref[...] += jnp.dot((_silu(a) * bb).astype(BF16), wo_ref[...], preferred_element_type=F32)

    @pl.when(f == pl.num_programs(2) - 1)
    def _():
        gate = mod_ref[5:6, :]
        o_ref[...] = x_ref[...] + gate * (_rms(acc_ref[...]) * g_out_ref[...])


def _ffn(x, mod, g_in, g_out, w_in, w_out, tm=512, tf=2816):
    b, t, d = x.shape
    fdim = w_out.shape[0]
    tm = _pick(t, tm)
    tf = _pick(fdim, tf)
    nf = fdim // tf
    return pl.pallas_call(
        _ffn_kernel,
        grid=(b, t // tm, nf),
        in_specs=[pl.BlockSpec((None, tm, d), lambda bi, i, f: (bi, i, 0)),
                  _mod_spec(mod),
                  pl.BlockSpec((1, d), lambda bi, i, f: (0, 0)),
                  pl.BlockSpec((1, d), lambda bi, i, f: (0, 0)),
                  pl.BlockSpec((d, tf), lambda bi, i, f: (0, f), pipeline_mode=pl.Buffered(1)),
                  pl.BlockSpec((d, tf), lambda bi, i, f: (0, f + nf), pipeline_mode=pl.Buffered(1)),
                  pl.BlockSpec((tf, d), lambda bi, i, f: (f, 0), pipeline_mode=pl.Buffered(1))],
        out_specs=pl.BlockSpec((None, tm, d), lambda bi, i, f: (bi, i, 0)),
        out_shape=jax.ShapeDtypeStruct((b, t, d), F32),
        scratch_shapes=[pltpu.VMEM((tm, d), BF16), pltpu.VMEM((tm, d), F32)],
        compiler_params=_cparams("parallel", "parallel", "arbitrary"),
        name="ffn",
    )(x, mod, g_in.reshape(1, d), g_out.reshape(1, d), w_in, w_in, w_out)


SCAN_HEADS = 4


def _head_slice(hh):
    return slice(hh * HEAD_DIM, (hh + 1) * HEAD_DIM)


def _finish_scan(of_ref, ob_ref, g_ref, y_ref, cs):
    def step(i, carry):
        start = pl.multiple_of(i * cs, cs)
        for hh in range(y_ref.shape[1] // HEAD_DIM):
            sl = _head_slice(hh)
            o = of_ref[pl.ds(start, cs), sl].astype(F32) + ob_ref[pl.ds(start, cs), sl].astype(F32)
            gate = g_ref[pl.ds(start, cs), sl].astype(F32)
            y_ref[pl.ds(start, cs), sl] = (_rms(o) * _silu(gate)).astype(y_ref.dtype)
        return carry

    lax.fori_loop(0, y_ref.shape[0] // cs, step, 0)


def _retention_kernel(lg_ref, qc_ref, kc_ref, vc_ref, gc_ref, qx_ref, kx_ref, vx_ref, gx_ref,
                      cos_ref, sin_ref, yc_ref, yx_ref,
                      ofc_ref, obc_ref, ofx_ref, obx_ref, s_ref, intra_ref, qd_ref, kd_ref):
    cs = RET_CHUNK
    hp = SCAN_HEADS
    head0 = pl.program_id(1) * hp
    row = lax.broadcasted_iota(jnp.int32, (cs, cs), 0).astype(F32)
    col = lax.broadcasted_iota(jnp.int32, (cs, cs), 1).astype(F32)
    pos = lax.broadcasted_iota(jnp.int32, (cs, HEAD_DIM), 0).astype(F32)
    k_scale = HEAD_DIM ** -0.5

    for hh in range(hp):
        for d in (0, 1):
            c = 2 * hh + d
            lg = lg_ref[d, head0 + hh]
            rel = (col - row) if d else (row - col)
            intra_ref[c] = jnp.where(rel >= 0, jnp.exp(lg * jnp.maximum(rel, 0.0)), 0.0)
            p = (cs - 1.0 - pos) if d else pos
            qd_ref[c] = jnp.exp(lg * (p + 1.0))
            kd_ref[c] = jnp.exp(lg * (cs - 1.0 - p))
    s_ref[...] = jnp.zeros_like(s_ref)

    for (q_ref, k_ref, v_ref, of_ref, ob_ref, rope) in (
            (qc_ref, kc_ref, vc_ref, ofc_ref, obc_ref, False),
            (qx_ref, kx_ref, vx_ref, ofx_ref, obx_ref, True)):
        n_chunks = q_ref.shape[0] // cs

        def step(i, carry, q_ref=q_ref, k_ref=k_ref, v_ref=v_ref, of_ref=of_ref, ob_ref=ob_ref,
                 rope=rope, n_chunks=n_chunks):
            chains = []
            for d in (0, 1):
                start = pl.multiple_of(((n_chunks - 1 - i) if d else i) * cs, cs)
                if rope:
                    cos = cos_ref[pl.ds(start, cs), :]
                    sin = sin_ref[pl.ds(start, cs), :]
                for hh in range(hp):
                    sl = _head_slice(hh)
                    q = q_ref[pl.ds(start, cs), sl].astype(F32)
                    k = k_ref[pl.ds(start, cs), sl].astype(F32) * k_scale
                    if rope:
                        q = q * cos + pltpu.roll(q, HEAD_DIM // 2, 1) * sin
                        k = k * cos + pltpu.roll(k, HEAD_DIM // 2, 1) * sin
                    chains.append(dict(d=d, hh=hh, c=2 * hh + d, start=start, sl=sl, q=q, k=k,
                                       v=v_ref[pl.ds(start, cs), sl], o_ref=ob_ref if d else of_ref))
            for ch in chains:
                ch["att"] = _mm_nt(ch["q"], ch["k"]) * intra_ref[ch["c"]]
            for ch in chains:
                c = ch["c"]
                ch["state"] = s_ref[c]
                ch["o_ref"][pl.ds(ch["start"], cs), ch["sl"]] = (
                    _mm(ch["att"], ch["v"]) + _mm(ch["q"] * qd_ref[c], ch["state"]))
            for ch in chains:
                c = ch["c"]
                c_dec = jnp.exp(jnp.full((1, HEAD_DIM), lg_ref[ch["d"], head0 + ch["hh"]] * cs, F32))
                s_ref[c] = ch["state"] * c_dec + _mm_tn(ch["k"] * kd_ref[c], ch["v"])
            return carry

        lax.fori_loop(0, n_chunks, step, 0)

    _finish_scan(ofc_ref, obc_ref, gc_ref, yc_ref, cs)
    _finish_scan(ofx_ref, obx_ref, gx_ref, yx_ref, cs)


def _retention(proj_c, proj_x, log_gamma, cos2, sin2):
    b, tc, _ = proj_c.shape
    tx = proj_x.shape[1]
    h, hp = RET_HEADS, SCAN_HEADS
    w = hp * HEAD_DIM

    def col(t, group):
        return pl.BlockSpec((None, t, w), lambda bi, hi, lg: (bi, 0, group * (h // hp) + hi),
                            pipeline_mode=pl.Buffered(1))

    n_chain = 2 * hp
    grid_spec = pltpu.PrefetchScalarGridSpec(
        num_scalar_prefetch=1,
        grid=(b, h // hp),
        in_specs=[col(tc, 0), col(tc, 1), col(tc, 2), col(tc, 3),
                  col(tx, 0), col(tx, 1), col(tx, 2), col(tx, 3),
                  pl.BlockSpec((tx, HEAD_DIM), lambda bi, hi, lg: (0, 0), pipeline_mode=pl.Buffered(1)),
                  pl.BlockSpec((tx, HEAD_DIM), lambda bi, hi, lg: (0, 0), pipeline_mode=pl.Buffered(1))],
        out_specs=[pl.BlockSpec((None, tc, w), lambda bi, hi, lg: (bi, 0, hi)),
                   pl.BlockSpec((None, tx, w), lambda bi, hi, lg: (bi, 0, hi))],
        scratch_shapes=[pltpu.VMEM((tc, w), F32), pltpu.VMEM((tc, w), F32),
                        pltpu.VMEM((tx, w), F32), pltpu.VMEM((tx, w), F32),
                        pltpu.VMEM((n_chain, HEAD_DIM, HEAD_DIM), F32),
                        pltpu.VMEM((n_chain, RET_CHUNK, RET_CHUNK), F32),
                        pltpu.VMEM((n_chain, RET_CHUNK, HEAD_DIM), F32),
                        pltpu.VMEM((n_chain, RET_CHUNK, HEAD_DIM), F32)])
    return pl.pallas_call(
        _retention_kernel,
        grid_spec=grid_spec,
        out_shape=[jax.ShapeDtypeStruct((b, tc, h * HEAD_DIM), BF16),
                   jax.ShapeDtypeStruct((b, tx, h * HEAD_DIM), BF16)],
        compiler_params=_cparams("parallel", "parallel"),
        name="retention",
    )(log_gamma, proj_c, proj_c, proj_c, proj_c, proj_x, proj_x, proj_x, proj_x, cos2, sin2)


def _hgrn_levels(cs):
    return [cs >> (i + 1) for i in range(int(math.log2(cs)))]


def _hgrn_constants(cs):
    t = np.arange(cs)[:, None]
    s = np.arange(cs)[None, :]
    tri = np.stack([(s <= t), (s >= t)]).astype(np.float32)
    masks = []
    for reverse in (False, True):
        lv = []
        for h in _hgrn_levels(cs):
            same = (t // (2 * h)) == (s // (2 * h))
            t_hi = (t % (2 * h)) >= h
            s_hi = (s % (2 * h)) >= h
            lv.append(same & (~t_hi & s_hi if reverse else t_hi & ~s_hi))
        lv.append(t == s)
        masks.append(np.stack(lv))
    return jnp.asarray(tri, BF16), jnp.asarray(np.stack(masks).astype(np.float32), BF16)


def _split3(x):
    hi = x.astype(BF16)
    r = x - hi.astype(F32)
    mid = r.astype(BF16)
    lo = (r - mid.astype(F32)).astype(BF16)
    return hi, mid, lo


def _neg_abs(x):
    bits = lax.bitcast_convert_type(x, jnp.uint32) | jnp.uint32(0x80000000)
    return lax.bitcast_convert_type(bits, F32)


def _level_reference(b, h, reverse):
    cs = b.shape[0]
    ref_local = h if reverse else h - 1
    if 2 * h >= 8:
        b3 = b.reshape(cs // (2 * h), 2 * h, b.shape[1])
        r = b3[:, ref_local:ref_local + 1, :]
        return jnp.broadcast_to(r, b3.shape).reshape(b.shape)
    local = lax.broadcasted_iota(jnp.int32, b.shape, 0) % (2 * h)
    out = b
    for m in range(2 * h):
        if m != ref_local:
            out = jnp.where(local == m, pltpu.roll(b, (m - ref_local) % cs, 0), out)
    return out


def _hgrn_kernel(lb_ref, tri_ref, mask_ref,
                 qc_ref, vc_ref, gc_ref, zfc_ref, zbc_ref,
                 qx_ref, vx_ref, gx_ref, zfx_ref, zbx_ref,
                 yc_ref, yx_ref, ofc_ref, obc_ref, ofx_ref, obx_ref, st_ref):
    cs = HGRN_CHUNK
    hp = SCAN_HEADS
    levels = _hgrn_levels(cs)
    st_ref[...] = jnp.zeros_like(st_ref)

    for (q_ref, v_ref, zf_ref, zb_ref, of_ref, ob_ref) in (
            (qc_ref, vc_ref, zfc_ref, zbc_ref, ofc_ref, obc_ref),
            (qx_ref, vx_ref, zfx_ref, zbx_ref, ofx_ref, obx_ref)):
        n_chunks = q_ref.shape[0] // cs

        def step(i, carry, q_ref=q_ref, v_ref=v_ref, zf_ref=zf_ref, zb_ref=zb_ref, of_ref=of_ref,
                 ob_ref=ob_ref, n_chunks=n_chunks):
            chains = []
            for d in (0, 1):
                start = pl.multiple_of(((n_chunks - 1 - i) if d else i) * cs, cs)
                z_ref = zb_ref if d else zf_ref
                for hh in range(hp):
                    sl = _head_slice(hh)
                    lb = lb_ref[hh]
                    q = _silu(q_ref[pl.ds(start, cs), sl].astype(F32))
                    f = lb + (1.0 - lb) * (1.0 / (1.0 + jnp.exp(-z_ref[pl.ds(start, cs), sl])))
                    chains.append(dict(d=d, c=2 * hh + d, start=start, sl=sl, q=q, k=1.0 - f,
                                       parts=_split3(jnp.log2(f)), v=v_ref[pl.ds(start, cs), sl],
                                       o_ref=ob_ref if d else of_ref))
            for ch in chains:
                tri = tri_ref[ch["d"]]
                hi, mid, lo = ch["parts"]
                ch["bsum"] = (jnp.dot(tri, hi, preferred_element_type=F32)
                              + jnp.dot(tri, mid, preferred_element_type=F32)
                              + jnp.dot(tri, lo, preferred_element_type=F32))
            for ch in chains:
                ch["qb"], ch["kb"] = ch["q"].astype(BF16), ch["k"].astype(BF16)
                ch["att"] = _mm_nt(ch["qb"], ch["kb"]).astype(BF16) * mask_ref[ch["d"], len(levels)]
            for li, h in enumerate(levels):
                for ch in chains:
                    ref = _level_reference(ch["bsum"], h, ch["d"] == 1)
                    w = jnp.exp2(_neg_abs(ch["bsum"] - ref)).astype(BF16)
                    ch["att"] += _mm_nt(ch["qb"] * w, ch["kb"] * w).astype(BF16) * mask_ref[ch["d"], li]
            for ch in chains:
                ch["state"] = st_ref[ch["c"]]
                o = _mm(ch["att"], ch["v"]) + _mm_nt(ch["q"] * jnp.exp2(ch["bsum"]), ch["state"])
                ch["o_ref"][pl.ds(ch["start"], cs), ch["sl"]] = o.astype(ch["o_ref"].dtype)
            for ch in chains:
                bsum = ch["bsum"]
                b_end = bsum[0:1, :] if ch["d"] else bsum[cs - 1:cs, :]
                st_ref[ch["c"]] = (ch["state"] * jnp.exp2(b_end)
                                   + _mm_tn(ch["v"], ch["k"] * jnp.exp2(b_end - bsum)))
            return carry

        lax.fori_loop(0, n_chunks, step, 0)

    _finish_scan(ofc_ref, obc_ref, gc_ref, yc_ref, cs)
    _finish_scan(ofx_ref, obx_ref, gx_ref, yx_ref, cs)


def _hgrn(proj_c, proj_x, z_c, z_x, lb, first):
    b, tc, _ = proj_c.shape
    tx = proj_x.shape[1]
    h, hp = HGRN_HEADS, SCAN_HEADS
    w = hp * HEAD_DIM
    tri, masks = _hgrn_constants(HGRN_CHUNK)

    def col(t, group):
        return pl.BlockSpec((None, t, w), lambda bi, hi: (bi, 0, (first + group * h) // hp + hi),
                            pipeline_mode=pl.Buffered(1))

    def zcol(t, group):
        return pl.BlockSpec((None, t, w), lambda bi, hi: (bi, 0, group * (h // hp) + hi),
                            pipeline_mode=pl.Buffered(1))

    ins = [col(tc, 0), col(tc, 1), col(tc, 2), zcol(tc, 0), zcol(tc, 1),
           col(tx, 0), col(tx, 1), col(tx, 2), zcol(tx, 0), zcol(tx, 1)]
    return pl.pallas_call(
        _hgrn_kernel,
        grid=(b, h // hp),
        in_specs=[pl.BlockSpec((hp, 1, HEAD_DIM), lambda bi, hi: (hi, 0, 0)),
                  pl.BlockSpec(tri.shape, lambda bi, hi: (0, 0, 0)),
                  pl.BlockSpec(masks.shape, lambda bi, hi: (0, 0, 0, 0))] + ins,
        out_specs=[pl.BlockSpec((None, tc, w), lambda bi, hi: (bi, 0, hi)),
                   pl.BlockSpec((None, tx, w), lambda bi, hi: (bi, 0, hi))],
        out_shape=[jax.ShapeDtypeStruct((b, tc, h * HEAD_DIM), BF16),
                   jax.ShapeDtypeStruct((b, tx, h * HEAD_DIM), BF16)],
        scratch_shapes=[pltpu.VMEM((tc, w), BF16), pltpu.VMEM((tc, w), BF16),
                        pltpu.VMEM((tx, w), BF16), pltpu.VMEM((tx, w), BF16),
                        pltpu.VMEM((2 * hp, HEAD_DIM, HEAD_DIM), F32)],
        compiler_params=_cparams("parallel", "parallel"),
        name="hgrn2",
    )(lb.reshape(h, 1, HEAD_DIM), tri, masks, proj_c, proj_c, proj_c, z_c, z_c,
      proj_x, proj_x, proj_x, z_x, z_x)


MLA_QW = 2 * LANE
MLA_SCORE_SCALE = (MLA_NOPE + MLA_ROPE) ** -0.5 * math.log2(math.e)
MLA_DOWN_PAD = MLA_Q_RANK + MLA_KV_RANK + LANE


def _mla_proj_kernel(x_ref, mod_ref, g_ref, wd_ref, gq_ref, gkv_ref, wq_ref, wkv_ref, cos_ref, sin_ref,
                     *out_refs, rope, want_q):
    h = _modnorm(x_ref[...], g_ref[...], mod_ref, 0, 1).astype(BF16)
    down = jnp.dot(h, wd_ref[...], preferred_element_type=F32)
    if rope:
        cos = cos_ref[...]
        sin = sin_ref[...]

    def rot(v):
        return v * cos + pltpu.roll(v, LANE // 2, 1) * sin if rope else v

    if want_q:
        q_ref, kn_ref, kr_ref, v_ref = out_refs
        qn = (_rms(down[:, :MLA_Q_RANK]) * gq_ref[...]).astype(BF16)
        q = jnp.dot(qn, wq_ref[...], preferred_element_type=F32) * MLA_SCORE_SCALE
        for hd in range(MLA_HEADS):
            lo = hd * MLA_QW
            q_ref[:, lo:lo + LANE] = q[:, lo:lo + LANE].astype(q_ref.dtype)
            q_ref[:, lo + LANE:lo + 2 * LANE] = rot(q[:, lo + LANE:lo + 2 * LANE]).astype(q_ref.dtype)
    else:
        kn_ref, kr_ref, v_ref = out_refs
    kvn = (_rms(down[:, MLA_Q_RANK:MLA_Q_RANK + MLA_KV_RANK]) * gkv_ref[...]).astype(BF16)
    kv = jnp.dot(kvn, wkv_ref[...], preferred_element_type=F32)
    nk = MLA_HEADS * MLA_NOPE
    kn_ref[...] = kv[:, :nk].astype(kn_ref.dtype)
    v_ref[...] = kv[:, nk:].astype(v_ref.dtype)
    kr_ref[...] = rot(down[:, MLA_Q_RANK + MLA_KV_RANK:]).astype(kr_ref.dtype)


def _mla_proj(x, mod, g, wd, gq, gkv, wq, wkv, cos, sin, rope, want_q, tm=512):
    b, t, d = x.shape
    tm = _pick(t, tm)
    const = lambda bi, i: (0, 0)
    tok = lambda bi, i: (bi, i, 0)
    nk, nv = MLA_HEADS * MLA_NOPE, MLA_HEADS * MLA_V
    out_shape = [jax.ShapeDtypeStruct((b, t, nk), BF16), jax.ShapeDtypeStruct((b, t, LANE), BF16),
                 jax.ShapeDtypeStruct((b, t, nv), BF16)]
    out_specs = [pl.BlockSpec((None, tm, nk), tok), pl.BlockSpec((None, tm, LANE), tok),
                 pl.BlockSpec((None, tm, nv), tok)]
    if want_q:
        out_shape = [jax.ShapeDtypeStruct((b, t, MLA_HEADS * MLA_QW), BF16)] + out_shape
        out_specs = [pl.BlockSpec((None, tm, MLA_HEADS * MLA_QW), tok)] + out_specs
    return pl.pallas_call(
        functools.partial(_mla_proj_kernel, rope=rope, want_q=want_q),
        grid=(b, t // tm),
        in_specs=[pl.BlockSpec((None, tm, d), tok), _mod_spec(mod), pl.BlockSpec((1, d), const),
                  pl.BlockSpec(wd.shape, const), pl.BlockSpec((1, MLA_Q_RANK), const),
                  pl.BlockSpec((1, MLA_KV_RANK), const), pl.BlockSpec(wq.shape, const),
                  pl.BlockSpec(wkv.shape, const),
                  pl.BlockSpec((tm, LANE), lambda bi, i: (i, 0)),
                  pl.BlockSpec((tm, LANE), lambda bi, i: (i, 0))],
        out_specs=out_specs,
        out_shape=out_shape,
        compiler_params=_cparams("parallel", "parallel"),
        name="mla_proj",
    )(x, mod, g.reshape(1, d), wd, gq.reshape(1, -1), gkv.reshape(1, -1), wq, wkv, cos, sin)


def _lane_fold(x, op):
    out = x[:, :LANE]
    for j in range(1, x.shape[1] // LANE):
        out = op(out, x[:, j * LANE:(j + 1) * LANE])
    return out


def _attention_kernel(q_ref, knc_ref, krc_ref, vc_ref, knx_ref, krx_ref, vx_ref, o_ref,
                      kc_ref, kx_ref, vc2_ref, vx2_ref, sc0_ref, sc1_ref, sx0_ref, sx1_ref, m0_ref, m1_ref,
                      *, tk):
    step = pl.program_id(2)

    @pl.when(step == 0)
    def _():
        kc_ref[:, :LANE] = knc_ref[...]
        kc_ref[:, LANE:] = krc_ref[...]
        kx_ref[:, :LANE] = knx_ref[...]
        kx_ref[:, LANE:] = krx_ref[...]
        vc2_ref[:, :LANE] = vc_ref[...]
        vx2_ref[:, :LANE] = vx_ref[...]
        for ref in (vc2_ref, vx2_ref):
            lane = lax.broadcasted_iota(jnp.int32, (ref.shape[0], LANE), 1)
            ref[:, LANE:] = jnp.where(lane == 0, 1.0, 0.0).astype(ref.dtype)

    n_blocks = kx_ref.shape[0] // tk
    n_steps = pl.num_programs(2)
    nt = (((1,), (1,)), ((), ()))

    def body(cur, prev):
        if cur is not None:
            sc_cur, sx_cur, m_cur = cur
            q = q_ref[...]
            s = lax.dot_general(q, kc_ref[...], nt, preferred_element_type=F32)
            sc_cur[...] = s
            m_part = _lane_fold(s, jnp.maximum)
        if prev is not None:
            sc_prev, sx_prev, m_prev = prev
            m = jnp.max(m_prev[...], axis=-1, keepdims=True)
            p = jnp.exp2(sc_prev[...] - m)
            acc = jnp.dot(p.astype(BF16), vc2_ref[...], preferred_element_type=F32)
        for i in range(n_blocks):
            if cur is not None:
                s = lax.dot_general(q, kx_ref[i * tk:(i + 1) * tk, :], nt, preferred_element_type=F32)
                sx_cur[i] = s
                m_part = jnp.maximum(m_part, _lane_fold(s, jnp.maximum))
            if prev is not None:
                p = jnp.exp2(sx_prev[i] - m)
                acc = acc + jnp.dot(p.astype(BF16), vx2_ref[i * tk:(i + 1) * tk, :],
                                    preferred_element_type=F32)
        if cur is not None:
            m_cur[...] = m_part
        if prev is not None:
            o_ref[...] = (acc[:, :MLA_V] / acc[:, MLA_V:MLA_V + 1]).astype(o_ref.dtype)

    bufs = ((sc0_ref, sx0_ref, m0_ref), (sc1_ref, sx1_ref, m1_ref))
    first = step == 0
    last = step == n_steps - 1
    inner = jnp.logical_not(first | last)

    @pl.when(first)
    def _():
        body(bufs[0], None)

    @pl.when(inner & (step % 2 == 0))
    def _():
        body(bufs[0], bufs[1])

    @pl.when(inner & (step % 2 == 1))
    def _():
        body(bufs[1], bufs[0])

    for parity in (0, 1):
        @pl.when(last & (step % 2 == parity))
        def _(parity=parity):
            body(None, bufs[1 - parity])


def _attention(q, kn_c, kr_c, v_c, kn_x, kr_x, v_x, tq=256, tk=256):
    b, t, _ = q.shape
    tc = kn_c.shape[1]
    tq = _pick(t, tq)
    tk = _pick(t, tk)
    nq = t // tq
    head_c = lambda bi, hi, i: (bi, 0, hi)
    shared = lambda bi, hi, i: (bi, 0, 0)
    return pl.pallas_call(
        functools.partial(_attention_kernel, tk=tk),
        grid=(b, MLA_HEADS, nq + 1),
        in_specs=[pl.BlockSpec((None, tq, MLA_QW), lambda bi, hi, i: (bi, jnp.minimum(i, nq - 1), hi)),
                  pl.BlockSpec((None, tc, LANE), head_c), pl.BlockSpec((None, tc, LANE), shared),
                  pl.BlockSpec((None, tc, LANE), head_c),
                  pl.BlockSpec((None, t, LANE), head_c), pl.BlockSpec((None, t, LANE), shared),
                  pl.BlockSpec((None, t, LANE), head_c)],
        out_specs=pl.BlockSpec((None, tq, MLA_V), lambda bi, hi, i: (bi, jnp.maximum(i - 1, 0), hi)),
        out_shape=jax.ShapeDtypeStruct((b, t, MLA_HEADS * MLA_V), BF16),
        scratch_shapes=[pltpu.VMEM((tc, MLA_QW), BF16), pltpu.VMEM((t, MLA_QW), BF16),
                        pltpu.VMEM((tc, 2 * MLA_V), BF16), pltpu.VMEM((t, 2 * MLA_V), BF16),
                        pltpu.VMEM((tq, tc), F32), pltpu.VMEM((tq, tc), F32),
                        pltpu.VMEM((t // tk, tq, tk), F32), pltpu.VMEM((t // tk, tq, tk), F32),
                        pltpu.VMEM((tq, LANE), F32), pltpu.VMEM((tq, LANE), F32)],
        compiler_params=_cparams("parallel", "parallel", "arbitrary"),
        name="mla_attention",
    )(q, kn_c, kr_c, v_c, kn_x, kr_x, v_x)


def _router_kernel(x_ref, mod_ref, g_ref, wr_ref, h_ref, w_ref, e_ref):
    h = _modnorm(x_ref[...], g_ref[...], mod_ref, 3, 4)
    h_ref[...] = _pack_halves(h)
    h_hi = h.astype(BF16)
    h_lo = (h - h_hi.astype(F32)).astype(BF16)
    w_hi = wr_ref[...].astype(BF16)
    w_lo = (wr_ref[...] - w_hi.astype(F32)).astype(BF16)
    logits = (jnp.dot(h_hi, w_hi, preferred_element_type=F32) + jnp.dot(h_lo, w_hi, preferred_element_type=F32)
              + jnp.dot(h_hi, w_lo, preferred_element_type=F32))
    lane = lax.broadcasted_iota(jnp.int32, logits.shape, 1)
    neg = jnp.float32(-jnp.inf)
    lg = jnp.where(lane < N_EXPERTS, logits, neg)
    m1 = jnp.max(lg, axis=-1, keepdims=True)
    i1 = jnp.min(jnp.where(lg == m1, lane, LANE), axis=-1, keepdims=True)
    lg2 = jnp.where(lane == i1, neg, lg)
    m2 = jnp.max(lg2, axis=-1, keepdims=True)
    i2 = jnp.min(jnp.where(lg2 == m2, lane, LANE), axis=-1, keepdims=True)
    e2 = jnp.exp(m2 - m1)
    w1 = 1.0 / (1.0 + e2)
    w2 = e2 / (1.0 + e2)
    w_ref[...] = jnp.where(lane == 0, w1, jnp.where(lane == 1, w2, 0.0))
    e_ref[...] = jnp.where(lane == 0, i1, jnp.where(lane == 1, i2, 0))


def _router(x, mod, g, router_pad, tm=512):
    b, t, d = x.shape
    tm = _pick(t, tm)
    tok = lambda bi, i: (bi, i, 0)
    const = lambda bi, i: (0, 0)
    return pl.pallas_call(
        _router_kernel,
        grid=(b, t // tm),
        in_specs=[pl.BlockSpec((None, tm, d), tok), _mod_spec(mod), pl.BlockSpec((1, d), const),
                  pl.BlockSpec((d, LANE), const)],
        out_specs=[pl.BlockSpec((None, tm, d // 2), tok), pl.BlockSpec((None, tm, LANE), tok),
                   pl.BlockSpec((None, tm, LANE), tok)],
        out_shape=[jax.ShapeDtypeStruct((b, t, d // 2), jnp.int32), jax.ShapeDtypeStruct((b, t, LANE), F32),
                   jax.ShapeDtypeStruct((b, t, LANE), jnp.int32)],
        compiler_params=_cparams("parallel", "parallel"),
        name="moe_router",
    )(x, mod, g.reshape(1, d), router_pad)


SC_GATHER_ROWS = 64


def _gather_rows(table, idx):
    info = plsc.get_sparse_core_info()
    n_workers = info.num_cores * info.num_subcores
    n, width = idx.shape[0], table.shape[1]
    chunk = SC_GATHER_ROWS
    assert n % (n_workers * chunk) == 0
    n_chunks = n // (n_workers * chunk)
    mesh = plsc.VectorSubcoreMesh(core_axis_name="c", subcore_axis_name="s")

    def body(table_hbm, idx_hbm, out_hbm, idx_v, rows_v, sem):
        wid = lax.axis_index("s") * info.num_cores + lax.axis_index("c")
        pltpu.sync_copy(idx_hbm.at[wid], idx_v)

        @pl.loop(0, n_chunks)
        def _(j):
            pltpu.async_copy(table_hbm.at[idx_v.at[j]], rows_v, sem).wait()
            pltpu.sync_copy(rows_v, out_hbm.at[pl.ds((wid * n_chunks + j) * chunk, chunk)])

    return pl.kernel(
        body,
        out_type=jax.ShapeDtypeStruct((n, width), table.dtype),
        mesh=mesh,
        scratch_types=[pltpu.VMEM((n_chunks, chunk), jnp.int32), pltpu.VMEM((chunk, width), table.dtype),
                       pltpu.SemaphoreType.DMA],
        name="sc_gather_rows",
    )(table, idx.reshape(n_workers, n_chunks, chunk))


def _scatter_rows_twice(table, idx0, idx1, n_out):
    info = plsc.get_sparse_core_info()
    n_workers = info.num_cores * info.num_subcores
    n, width = table.shape
    chunk = SC_GATHER_ROWS
    assert n % (n_workers * chunk) == 0
    n_chunks = n // (n_workers * chunk)
    mesh = plsc.VectorSubcoreMesh(core_axis_name="c", subcore_axis_name="s")

    def body(table_hbm, i0_hbm, i1_hbm, out_hbm, i0_v, i1_v, rows_v):
        wid = lax.axis_index("s") * info.num_cores + lax.axis_index("c")
        pltpu.sync_copy(i0_hbm.at[wid], i0_v)
        pltpu.sync_copy(i1_hbm.at[wid], i1_v)

        @pl.loop(0, n_chunks)
        def _(j):
            pltpu.sync_copy(table_hbm.at[pl.ds((wid * n_chunks + j) * chunk, chunk)], rows_v)
            pltpu.sync_copy(rows_v, out_hbm.at[i0_v.at[j]])
            pltpu.sync_copy(rows_v, out_hbm.at[i1_v.at[j]])

    shape3 = (n_workers, n_chunks, chunk)
    return pl.kernel(
        body,
        out_type=jax.ShapeDtypeStruct((n_out, width), table.dtype),
        mesh=mesh,
        scratch_types=[pltpu.VMEM((n_chunks, chunk), jnp.int32), pltpu.VMEM((n_chunks, chunk), jnp.int32),
                       pltpu.VMEM((chunk, width), table.dtype)],
        name="sc_scatter_rows",
    )(table, idx0.reshape(shape3), idx1.reshape(shape3))


def _pack_halves(x):
    k = x.shape[1] // 2
    bits = lambda v: lax.bitcast_convert_type(v.astype(BF16).astype(F32), jnp.uint32)
    word = lax.shift_right_logical(bits(x[:, :k]), jnp.uint32(16)) | bits(x[:, k:])
    return lax.bitcast_convert_type(word, jnp.int32)


def _unpack_halves(w):
    u = lax.bitcast_convert_type(w, jnp.uint32)
    lo = lax.bitcast_convert_type(lax.shift_left(u, jnp.uint32(16)), F32)
    hi = lax.bitcast_convert_type(u & jnp.uint32(0xFFFF0000), F32)
    return jnp.concatenate([lo, hi], axis=-1)


def _expert_ffn_kernel(te_ref, tv_ref, x_ref, wa_ref, wb_ref, wo_ref, o_ref, acc_ref):
    j = pl.program_id(0)
    f = pl.program_id(1)
    last = pl.num_programs(1) - 1
    valid = tv_ref[j] != 0

    @pl.when(valid)
    def _():
        @pl.when(f == 0)
        def _():
            acc_ref[...] = jnp.zeros_like(acc_ref)

        h = _unpack_halves(x_ref[...]).astype(BF16)
        a = jnp.dot(h, wa_ref[...], preferred_element_type=F32)
        bb = jnp.dot(h, wb_ref[...], preferred_element_type=F32)
        acc_ref[...] += jnp.dot((_silu(a) * bb).astype(BF16), wo_ref[...], preferred_element_type=F32)

        @pl.when(f == last)
        def _():
            o_ref[...] = _pack_halves(acc_ref[...])

    @pl.when(jnp.logical_not(valid) & (f == last))
    def _():
        o_ref[...] = jnp.zeros_like(o_ref)


def _expert_ffn(xs, tile_expert, tile_valid, w_in, w_out, tm, tf=2816):
    p = xs.shape[0]
    n_exp, fdim, d = w_out.shape
    tf = _pick(fdim, tf)
    nf = fdim // tf

    def f_eff(j, f, tv):
        return jnp.where(tv[j] != 0, f, nf - 1)

    grid_spec = pltpu.PrefetchScalarGridSpec(
        num_scalar_prefetch=2,
        grid=(p // tm, nf),
        in_specs=[pl.BlockSpec((tm, d // 2), lambda j, f, te, tv: (j, 0)),
                  pl.BlockSpec((None, d, tf), lambda j, f, te, tv: (te[j], 0, f_eff(j, f, tv)),
                               pipeline_mode=pl.Buffered(1)),
                  pl.BlockSpec((None, d, tf), lambda j, f, te, tv: (te[j], 0, f_eff(j, f, tv) + nf),
                               pipeline_mode=pl.Buffered(1)),
                  pl.BlockSpec((None, tf, d), lambda j, f, te, tv: (te[j], f_eff(j, f, tv), 0),
                               pipeline_mode=pl.Buffered(1))],
        out_specs=pl.BlockSpec((tm, d // 2), lambda j, f, te, tv: (j, 0)),
        scratch_shapes=[pltpu.VMEM((tm, d), F32)])
    return pl.pallas_call(
        _expert_ffn_kernel,
        grid_spec=grid_spec,
        out_shape=jax.ShapeDtypeStruct((p, d // 2), jnp.int32),
        compiler_params=_cparams("parallel", "arbitrary"),
        name="moe_expert_ffn",
    )(tile_expert, tile_valid, xs, w_in, w_in, w_out)


def _moe_combine_kernel(x_ref, y0_ref, y1_ref, w_ref, mod_ref, g_ref, o_ref):
    y = w_ref[:, 0:1] * _unpack_halves(y0_ref[...]) + w_ref[:, 1:2] * _unpack_halves(y1_ref[...])
    o_ref[...] = x_ref[...] + mod_ref[5:6, :] * (_rms(y) * g_ref[...])


def _moe_combine(x, y2, w, mod, g, tm=512):
    b, t, d = x.shape
    tm = _pick(t, tm)
    tok = lambda bi, i: (bi, i, 0)
    return pl.pallas_call(
        _moe_combine_kernel,
        grid=(b, t // tm),
        in_specs=[pl.BlockSpec((None, tm, d), tok),
                  pl.BlockSpec((None, None, tm, d // 2), lambda bi, i: (0, bi, i, 0)),
                  pl.BlockSpec((None, None, tm, d // 2), lambda bi, i: (1, bi, i, 0)),
                  pl.BlockSpec((None, tm, LANE), tok), _mod_spec(mod),
                  pl.BlockSpec((1, d), lambda bi, i: (0, 0))],
        out_specs=pl.BlockSpec((None, tm, d), tok),
        out_shape=jax.ShapeDtypeStruct((b, t, d), F32),
        compiler_params=_cparams("parallel", "parallel"),
        name="moe_combine",
    )(x, y2, y2, w, mod, g.reshape(1, d))


def _moe(x, mod, g_in, g_out, router_pad, w_in, w_out, tm=512):
    b, t, d = x.shape
    n_exp = w_out.shape[0]
    n = b * t
    h, w, e = _router(x, mod, g_in, router_pad)
    e_flat = jnp.concatenate([e[..., 0].reshape(n), e[..., 1].reshape(n)])
    onehot = (e_flat[:, None] == jnp.arange(n_exp, dtype=jnp.int32)).astype(jnp.int32)
    csum = jnp.cumsum(onehot, axis=0)
    rank = jnp.sum((csum - onehot) * onehot, axis=-1)
    counts = csum[-1]
    ends = jnp.cumsum((counts + tm - 1) // tm * tm)
    dest = (ends - (counts + tm - 1) // tm * tm)[e_flat] + rank
    p = 2 * n + n_exp * tm
    tile_start = jnp.arange(p // tm, dtype=jnp.int32) * tm
    tile_expert = jnp.minimum(jnp.searchsorted(ends, tile_start, side="right"), n_exp - 1).astype(jnp.int32)
    tile_valid = (tile_start < ends[-1]).astype(jnp.int32)

    xs = _scatter_rows_twice(h.reshape(n, d // 2), dest[:n], dest[n:], p)
    ys = _expert_ffn(xs, tile_expert, tile_valid, w_in, w_out, tm)
    y2 = _gather_rows(ys, dest)
    return _moe_combine(x, y2.reshape(2, b, t, d // 2), w, mod, g_out)


def _axial_angles(n_tokens, dim):
    t = jnp.arange(n_tokens)
    rows = (t // GRID_W).astype(F32)
    cols = (t % GRID_W).astype(F32)
    n_freq = dim // 4
    inv = ROPE_BASE ** (-jnp.arange(n_freq, dtype=F32) / n_freq)
    return jnp.concatenate([rows[:, None] * inv, cols[:, None] * inv], axis=-1)


def _rope_tables(n_tokens, dim):
    ang = _axial_angles(n_tokens, dim)
    cos, sin = jnp.cos(ang), jnp.sin(ang)
    if dim == LANE:
        return jnp.concatenate([cos, cos], -1), jnp.concatenate([-sin, sin], -1)
    z = jnp.zeros_like(cos)
    return jnp.concatenate([cos, z, cos, z], -1), jnp.concatenate([-sin, z, sin, z], -1)


def _spread_rope_cols(w):
    half = MLA_ROPE // 2
    z = jnp.zeros(w.shape[:-1] + (half,), w.dtype)
    return jnp.concatenate([w[..., :half], z, w[..., half:], z], axis=-1)


def kernel(x, c, ctx, c_ctx, mod_w, mod_b, norm_g, mix_in_w, ret_decay_logit, hgrn_lb_logit, mix_out_w,
           ffn_in_w, ffn_out_w, mla_down_w, mla_q_norm_g, mla_kv_norm_g, mla_uq_w, mla_ukv_w, mla_out_w,
           router_w, moe_in_w, moe_out_w):
    batch, seq, d = x.shape
    assert mod_w.shape[0] == 2, "two layers: retention/HGRN2 then MLA/MoE"

    rows = -(-(batch + 1) // 8) * 8
    cond = jnp.zeros((rows, d), F32).at[:batch].set(c).at[batch].set(c_ctx)
    mod = _modulation(cond, mod_w, mod_b)
    mod_x = [mod[l, :batch].reshape(batch, 6, d) for l in range(2)]
    mod_c = [mod[l, batch:batch + 1].reshape(1, 6, d) for l in range(2)]

    g = norm_g[0]
    ret_w = RET_HEADS * HEAD_DIM
    z_lo, z_hi = 5 * ret_w, 7 * ret_w
    w_in = mix_in_w[0].astype(BF16)
    w_main = jnp.concatenate([w_in[:, :z_lo], w_in[:, z_hi:]], axis=1)
    w_z = w_in[:, z_lo:z_hi]
    proj_x = _norm_matmul(x, mod_x[0], g[0], w_main, 0, 1, BF16)
    proj_c = _norm_matmul(ctx, mod_c[0], g[0], w_main, 0, 1, BF16)
    z_x = _norm_matmul(x, mod_x[0], g[0], w_z, 0, 1, F32)
    z_c = _norm_matmul(ctx, mod_c[0], g[0], w_z, 0, 1, F32)
    cos2, sin2 = _rope_tables(seq, HEAD_DIM)
    log_gamma = jax.nn.log_sigmoid(ret_decay_logit[0].astype(F32))
    lb = jnp.cumsum(jax.nn.softmax(hgrn_lb_logit.astype(F32), axis=0), axis=0)[0]
    ret_c, ret_x = _retention(proj_c, proj_x, log_gamma, cos2, sin2)
    hg_c, hg_x = _hgrn(proj_c, proj_x, z_c, z_x, lb, first=4 * RET_HEADS)
    w_out = mix_out_w[0].astype(BF16)
    w_outs = [w_out[:ret_w], w_out[ret_w:]]
    x = _proj_residual([ret_x, hg_x], w_outs, x, mod_x[0], g[1], 2)
    ctx = _proj_residual([ret_c, hg_c], w_outs, ctx, mod_c[0], g[1], 2)
    f_in, f_out = ffn_in_w[0].astype(BF16), ffn_out_w[0].astype(BF16)
    x = _ffn(x, mod_x[0], g[2], g[3], f_in, f_out)
    ctx = _ffn(ctx, mod_c[0], g[2], g[3], f_in, f_out)

    g = norm_g[1]
    dq = MLA_NOPE + MLA_ROPE
    wd = mla_down_w[0]
    wd = jnp.concatenate([wd[:, :MLA_Q_RANK + MLA_KV_RANK],
                          _spread_rope_cols(wd[:, MLA_Q_RANK + MLA_KV_RANK:])], axis=-1).astype(BF16)
    wq = mla_uq_w[0].reshape(MLA_Q_RANK, MLA_HEADS, dq)
    wq = jnp.concatenate([wq[..., :MLA_NOPE], _spread_rope_cols(wq[..., MLA_NOPE:])], axis=-1)
    wq = wq.reshape(MLA_Q_RANK, MLA_HEADS * MLA_QW).astype(BF16)
    wkv = mla_ukv_w[0].reshape(MLA_KV_RANK, MLA_HEADS, MLA_NOPE + MLA_V)
    wkv = jnp.concatenate([wkv[..., :MLA_NOPE].reshape(MLA_KV_RANK, -1),
                           wkv[..., MLA_NOPE:].reshape(MLA_KV_RANK, -1)], axis=-1).astype(BF16)
    cos_m, sin_m = _rope_tables(seq, MLA_ROPE)
    q, kn_x, kr_x, v_x = _mla_proj(x, mod_x[1], g[0], wd, mla_q_norm_g[0], mla_kv_norm_g[0], wq, wkv,
                                   cos_m, sin_m, rope=True, want_q=True)
    tc = ctx.shape[1]
    kn_c, kr_c, v_c = _mla_proj(ctx, mod_c[1], g[0], wd, mla_q_norm_g[0], mla_kv_norm_g[0], wq, wkv,
                                cos_m[:tc], sin_m[:tc], rope=False, want_q=False)
    att = _attention(q, kn_c, kr_c, v_c, kn_x, kr_x, v_x)
    x = _proj_residual([att], [mla_out_w[0].astype(BF16)], x, mod_x[1], g[1], 2)
    router_pad = jnp.zeros((d, LANE), F32).at[:, :N_EXPERTS].set(router_w[0])
    x = _moe(x, mod_x[1], g[2], g[3], router_pad, moe_in_w[0].astype(BF16), moe_out_w[0].astype(BF16))
    return x
```

```python
import functools
import math

import jax
import jax.numpy as jnp
import numpy as np
from jax import lax
from jax.experimental import pallas as pl
from jax.experimental.pallas import tpu as pltpu
from jax.experimental.pallas import tpu_sc as plsc

F32 = jnp.float32
BF16 = jnp.bfloat16

NORM_EPS = 1e-6
ROPE_BASE = 10000.0
GRID_W = 64
LANE = 128

RET_HEADS = 4
HGRN_HEADS = 4
HEAD_DIM = 128
RET_CHUNK = 128
HGRN_CHUNK = 128

MLA_HEADS = 8
MLA_NOPE = 128
MLA_ROPE = 64
MLA_V = 128
MLA_Q_RANK = 384
MLA_KV_RANK = 256
N_EXPERTS = 8

VMEM_LIMIT = 56 * 1024 * 1024


def _cparams(*sem):
    return pltpu.CompilerParams(dimension_semantics=sem, vmem_limit_bytes=VMEM_LIMIT)


def _mm(a, b):
    return jnp.dot(a.astype(BF16), b.astype(BF16), preferred_element_type=F32)


def _mm_nt(a, b):
    return lax.dot_general(a.astype(BF16), b.astype(BF16), (((1,), (1,)), ((), ())),
                           preferred_element_type=F32)


def _mm_tn(a, b):
    return lax.dot_general(a.astype(BF16), b.astype(BF16), (((0,), (0,)), ((), ())),
                           preferred_element_type=F32)


def _rms(x):
    return x * lax.rsqrt(jnp.mean(x * x, axis=-1, keepdims=True) + NORM_EPS)


def _silu(x):
    return x * (1.0 / (1.0 + jnp.exp(-x)))


def _modnorm(x, g, mod_ref, shift_row, scale_row):
    shift = mod_ref[shift_row:shift_row + 1, :]
    scale = mod_ref[scale_row:scale_row + 1, :]
    return _rms(x) * g * (1.0 + scale) + shift


def _mod_spec(mod):
    d = mod.shape[-1]
    if mod.shape[0] == 1:
        return pl.BlockSpec((None, 6, d), lambda b, *_: (0, 0, 0))
    return pl.BlockSpec((None, 6, d), lambda b, *_: (b, 0, 0))


def _pick(n, pref):
    t = min(pref, n)
    while n % t:
        t -= LANE
    return t


def _mod_kernel(c_ref, w_ref, b_ref, o_ref):
    a = _silu(c_ref[...])
    o_ref[...] = jnp.dot(a, w_ref[...], preferred_element_type=F32,
                         precision=lax.Precision.HIGHEST) + b_ref[...]


def _modulation(cond, mod_w, mod_b):
    n_layers, d, n = mod_w.shape
    r = cond.shape[0]
    tn = _pick(n, 1536)
    return pl.pallas_call(
        _mod_kernel,
        grid=(n_layers, n // tn),
        in_specs=[pl.BlockSpec((r, d), lambda l, j: (0, 0)),
                  pl.BlockSpec((None, d, tn), lambda l, j: (l, 0, j)),
                  pl.BlockSpec((None, 1, tn), lambda l, j: (l, 0, j))],
        out_specs=pl.BlockSpec((None, r, tn), lambda l, j: (l, 0, j)),
        out_shape=jax.ShapeDtypeStruct((n_layers, r, n), F32),
        compiler_params=_cparams("parallel", "parallel"),
        name="modulation",
    )(cond, mod_w, mod_b.reshape(n_layers, 1, n))


def _norm_matmul_kernel(x_ref, mod_ref, g_ref, w_ref, o_ref, h_ref, *, shift_row, scale_row):
    @pl.when(pl.program_id(2) == 0)
    def _():
        h_ref[...] = _modnorm(x_ref[...], g_ref[...], mod_ref, shift_row, scale_row).astype(BF16)

    o_ref[...] = jnp.dot(h_ref[...], w_ref[...], preferred_element_type=F32).astype(o_ref.dtype)


def _norm_matmul(x, mod, g, w, shift_row, scale_row, out_dtype, tm=1024, tn=1792):
    b, t, d = x.shape
    n = w.shape[1]
    tm = _pick(t, tm)
    tn = _pick(n, tn)
    return pl.pallas_call(
        functools.partial(_norm_matmul_kernel, shift_row=shift_row, scale_row=scale_row),
        grid=(b, t // tm, n // tn),
        in_specs=[pl.BlockSpec((None, tm, d), lambda bi, i, j: (bi, i, 0)),
                  _mod_spec(mod),
                  pl.BlockSpec((1, d), lambda bi, i, j: (0, 0)),
                  pl.BlockSpec((d, tn), lambda bi, i, j: (0, j))],
        out_specs=pl.BlockSpec((None, tm, tn), lambda bi, i, j: (bi, i, j)),
        out_shape=jax.ShapeDtypeStruct((b, t, n), out_dtype),
        scratch_shapes=[pltpu.VMEM((tm, d), BF16)],
        compiler_params=_cparams("parallel", "parallel", "arbitrary"),
        name="norm_matmul",
    )(x, mod, g.reshape(1, d), w)


def _proj_residual_kernel(*refs, n_lhs, gate_row):
    lhs = refs[:n_lhs]
    ws = refs[n_lhs:2 * n_lhs]
    x_ref, mod_ref, g_ref, o_ref = refs[2 * n_lhs:]
    y = jnp.dot(lhs[0][...], ws[0][...], preferred_element_type=F32)
    for a_ref, w_ref in zip(lhs[1:], ws[1:]):
        y += jnp.dot(a_ref[...], w_ref[...], preferred_element_type=F32)
    gate = mod_ref[gate_row:gate_row + 1, :]
    o_ref[...] = x_ref[...] + gate * (_rms(y) * g_ref[...])


def _proj_residual(lhs, ws, x, mod, g, gate_row, tm=512):
    b, t, d = x.shape
    tm = _pick(t, tm)
    n_lhs = len(lhs)
    in_specs = [pl.BlockSpec((None, tm, a.shape[-1]), lambda bi, i: (bi, i, 0)) for a in lhs]
    in_specs += [pl.BlockSpec(w.shape, lambda bi, i: (0, 0)) for w in ws]
    in_specs += [pl.BlockSpec((None, tm, d), lambda bi, i: (bi, i, 0)),
                 _mod_spec(mod),
                 pl.BlockSpec((1, d), lambda bi, i: (0, 0))]
    return pl.pallas_call(
        functools.partial(_proj_residual_kernel, n_lhs=n_lhs, gate_row=gate_row),
        grid=(b, t // tm),
        in_specs=in_specs,
        out_specs=pl.BlockSpec((None, tm, d), lambda bi, i: (bi, i, 0)),
        out_shape=jax.ShapeDtypeStruct((b, t, d), F32),
        compiler_params=_cparams("parallel", "parallel"),
        name="proj_residual",
    )(*lhs, *ws, x, mod, g.reshape(1, d))


def _ffn_kernel(x_ref, mod_ref, g_in_ref, g_out_ref, wa_ref, wb_ref, wo_ref, o_ref, h_ref, acc_ref):
    f = pl.program_id(2)

    @pl.when(f == 0)
    def _():
        h_ref[...] = _modnorm(x_ref[...], g_in_ref[...], mod_ref, 3, 4).astype(BF16)
        acc_ref[...] = jnp.zeros_like(acc_ref)

    h = h_ref[...]
    a = jnp.dot(h, wa_ref[...], preferred_element_type=F32)
    bb = jnp.dot(h, wb_ref[...], preferred_element_type=F32)
    acc_ref[...] += jnp.dot((_silu(a) * bb).astype(BF16), wo_ref[...], preferred_element_type=F32)

    @pl.when(f == pl.num_programs(2) - 1)
    def _():
        gate = mod_ref[5:6, :]
        o_ref[...] = x_ref[...] + gate * (_rms(acc_ref[...]) * g_out_ref[...])


def _ffn(x, mod, g_in, g_out, w_in, w_out, tm=512, tf=2816):
    b, t, d = x.shape
    fdim = w_out.shape[0]
    tm = _pick(t, tm)
    tf = _pick(fdim, tf)
    nf = fdim // tf
    return pl.pallas_call(
        _ffn_kernel,
        grid=(b, t // tm, nf),
        in_specs=[pl.BlockSpec((None, tm, d), lambda bi, i, f: (bi, i, 0)),
                  _mod_spec(mod),
                  pl.BlockSpec((1, d), lambda bi, i, f: (0, 0)),
                  pl.BlockSpec((1, d), lambda bi, i, f: (0, 0)),
                  pl.BlockSpec((d, tf), lambda bi, i, f: (0, f), pipeline_mode=pl.Buffered(1)),
                  pl.BlockSpec((d, tf), lambda bi, i, f: (0, f + nf), pipeline_mode=pl.Buffered(1)),
                  pl.BlockSpec((tf, d), lambda bi, i, f: (f, 0), pipeline_mode=pl.Buffered(1))],
        out_specs=pl.BlockSpec((None, tm, d), lambda bi, i, f: (bi, i, 0)),
        out_shape=jax.ShapeDtypeStruct((b, t, d), F32),
        scratch_shapes=[pltpu.VMEM((tm, d), BF16), pltpu.VMEM((tm, d), F32)],
        compiler_params=_cparams("parallel", "parallel", "arbitrary"),
        name="ffn",
    )(x, mod, g_in.reshape(1, d), g_out.reshape(1, d), w_in, w_in, w_out)


SCAN_HEADS = 4


def _head_slice(hh):
    return slice(hh * HEAD_DIM, (hh + 1) * HEAD_DIM)


def _finish_scan(of_ref, ob_ref, g_ref, y_ref, cs):
    def step(i, carry):
        start = pl.multiple_of(i * cs, cs)
        for hh in range(y_ref.shape[1] // HEAD_DIM):
            sl = _head_slice(hh)
            o = of_ref[pl.ds(start, cs), sl].astype(F32) + ob_ref[pl.ds(start, cs), sl].astype(F32)
            gate = g_ref[pl.ds(start, cs), sl].astype(F32)
            y_ref[pl.ds(start, cs), sl] = (_rms(o) * _silu(gate)).astype(y_ref.dtype)
        return carry

    lax.fori_loop(0, y_ref.shape[0] // cs, step, 0)


def _retention_kernel(lg_ref, qc_ref, kc_ref, vc_ref, gc_ref, qx_ref, kx_ref, vx_ref, gx_ref,
                      cos_ref, sin_ref, yc_ref, yx_ref,
                      ofc_ref, obc_ref, ofx_ref, obx_ref, s_ref, intra_ref, qd_ref, kd_ref):
    cs = RET_CHUNK
    hp = SCAN_HEADS
    head0 = pl.program_id(1) * hp
    row = lax.broadcasted_iota(jnp.int32, (cs, cs), 0).astype(F32)
    col = lax.broadcasted_iota(jnp.int32, (cs, cs), 1).astype(F32)
    pos = lax.broadcasted_iota(jnp.int32, (cs, HEAD_DIM), 0).astype(F32)
    k_scale = HEAD_DIM ** -0.5

    for hh in range(hp):
        for d in (0, 1):
            c = 2 * hh + d
            lg = lg_ref[d, head0 + hh]
            rel = (col - row) if d else (row - col)
            intra_ref[c] = jnp.where(rel >= 0, jnp.exp(lg * jnp.maximum(rel, 0.0)), 0.0)
            p = (cs - 1.0 - pos) if d else pos
            qd_ref[c] = jnp.exp(lg * (p + 1.0))
            kd_ref[c] = jnp.exp(lg * (cs - 1.0 - p))
    s_ref[...] = jnp.zeros_like(s_ref)

    for (q_ref, k_ref, v_ref, of_ref, ob_ref, rope) in (
            (qc_ref, kc_ref, vc_ref, ofc_ref, obc_ref, False),
            (qx_ref, kx_ref, vx_ref, ofx_ref, obx_ref, True)):
        n_chunks = q_ref.shape[0] // cs

        def step(i, carry, q_ref=q_ref, k_ref=k_ref, v_ref=v_ref, of_ref=of_ref, ob_ref=ob_ref,
                 rope=rope, n_chunks=n_chunks):
            chains = []
            for d in (0, 1):
                start = pl.multiple_of(((n_chunks - 1 - i) if d else i) * cs, cs)
                if rope:
                    cos = cos_ref[pl.ds(start, cs), :]
                    sin = sin_ref[pl.ds(start, cs), :]
                for hh in range(hp):
                    sl = _head_slice(hh)
                    q = q_ref[pl.ds(start, cs), sl].astype(F32)
                    k = k_ref[pl.ds(start, cs), sl].astype(F32) * k_scale
                    if rope:
                        q = q * cos + pltpu.roll(q, HEAD_DIM // 2, 1) * sin
                        k = k * cos + pltpu.roll(k, HEAD_DIM // 2, 1) * sin
                    chains.append(dict(d=d, hh=hh, c=2 * hh + d, start=start, sl=sl, q=q, k=k,
                                       v=v_ref[pl.ds(start, cs), sl], o_ref=ob_ref if d else of_ref))
            for ch in chains:
                ch["att"] = _mm_nt(ch["q"], ch["k"]) * intra_ref[ch["c"]]
            for ch in chains:
                c = ch["c"]
                ch["state"] = s_ref[c]
                ch["o_ref"][pl.ds(ch["start"], cs), ch["sl"]] = (
                    _mm(ch["att"], ch["v"]) + _mm(ch["q"] * qd_ref[c], ch["state"]))
            for ch in chains:
                c = ch["c"]
                c_dec = jnp.exp(jnp.full((1, HEAD_DIM), lg_ref[ch["d"], head0 + ch["hh"]] * cs, F32))
                s_ref[c] = ch["state"] * c_dec + _mm_tn(ch["k"] * kd_ref[c], ch["v"])
            return carry

        lax.fori_loop(0, n_chunks, step, 0)

    _finish_scan(ofc_ref, obc_ref, gc_ref, yc_ref, cs)
    _finish_scan(ofx_ref, obx_ref, gx_ref, yx_ref, cs)


def _retention(proj_c, proj_x, log_gamma, cos2, sin2):
    b, tc, _ = proj_c.shape
    tx = proj_x.shape[1]
    h, hp = RET_HEADS, SCAN_HEADS
    w = hp * HEAD_DIM

    def col(t, group):
        return pl.BlockSpec((None, t, w), lambda bi, hi, lg: (bi, 0, group * (h // hp) + hi),
                            pipeline_mode=pl.Buffered(1))

    n_chain = 2 * hp
    grid_spec = pltpu.PrefetchScalarGridSpec(
        num_scalar_prefetch=1,
        grid=(b, h // hp),
        in_specs=[col(tc, 0), col(tc, 1), col(tc, 2), col(tc, 3),
                  col(tx, 0), col(tx, 1), col(tx, 2), col(tx, 3),
                  pl.BlockSpec((tx, HEAD_DIM), lambda bi, hi, lg: (0, 0), pipeline_mode=pl.Buffered(1)),
                  pl.BlockSpec((tx, HEAD_DIM), lambda bi, hi, lg: (0, 0), pipeline_mode=pl.Buffered(1))],
        out_specs=[pl.BlockSpec((None, tc, w), lambda bi, hi, lg: (bi, 0, hi)),
                   pl.BlockSpec((None, tx, w), lambda bi, hi, lg: (bi, 0, hi))],
        scratch_shapes=[pltpu.VMEM((tc, w), F32), pltpu.VMEM((tc, w), F32),
                        pltpu.VMEM((tx, w), F32), pltpu.VMEM((tx, w), F32),
                        pltpu.VMEM((n_chain, HEAD_DIM, HEAD_DIM), F32),
                        pltpu.VMEM((n_chain, RET_CHUNK, RET_CHUNK), F32),
                        pltpu.VMEM((n_chain, RET_CHUNK, HEAD_DIM), F32),
                        pltpu.VMEM((n_chain, RET_CHUNK, HEAD_DIM), F32)])
    return pl.pallas_call(
        _retention_kernel,
        grid_spec=grid_spec,
        out_shape=[jax.ShapeDtypeStruct((b, tc, h * HEAD_DIM), BF16),
                   jax.ShapeDtypeStruct((b, tx, h * HEAD_DIM), BF16)],
        compiler_params=_cparams("parallel", "parallel"),
        name="retention",
    )(log_gamma, proj_c, proj_c, proj_c, proj_c, proj_x, proj_x, proj_x, proj_x, cos2, sin2)


def _hgrn_levels(cs):
    return [cs >> (i + 1) for i in range(int(math.log2(cs)))]


def _hgrn_constants(cs):
    t = np.arange(cs)[:, None]
    s = np.arange(cs)[None, :]
    tri = np.stack([(s <= t), (s >= t)]).astype(np.float32)
    masks = []
    for reverse in (False, True):
        lv = []
        for h in _hgrn_levels(cs):
            same = (t // (2 * h)) == (s // (2 * h))
            t_hi = (t % (2 * h)) >= h
            s_hi = (s % (2 * h)) >= h
            lv.append(same & (~t_hi & s_hi if reverse else t_hi & ~s_hi))
        lv.append(t == s)
        masks.append(np.stack(lv))
    return jnp.asarray(tri, BF16), jnp.asarray(np.stack(masks).astype(np.float32), BF16)


def _split3(x):
    hi = x.astype(BF16)
    r = x - hi.astype(F32)
    mid = r.astype(BF16)
    lo = (r - mid.astype(F32)).astype(BF16)
    return hi, mid, lo


def _neg_abs(x):
    bits = lax.bitcast_convert_type(x, jnp.uint32) | jnp.uint32(0x80000000)
    return lax.bitcast_convert_type(bits, F32)


def _level_reference(b, h, reverse):
    cs = b.shape[0]
    ref_local = h if reverse else h - 1
    if 2 * h >= 8:
        b3 = b.reshape(cs // (2 * h), 2 * h, b.shape[1])
        r = b3[:, ref_local:ref_local + 1, :]
        return jnp.broadcast_to(r, b3.shape).reshape(b.shape)
    local = lax.broadcasted_iota(jnp.int32, b.shape, 0) % (2 * h)
    out = b
    for m in range(2 * h):
        if m != ref_local:
            out = jnp.where(local == m, pltpu.roll(b, (m - ref_local) % cs, 0), out)
    return out


def _hgrn_kernel(lb_ref, tri_ref, mask_ref,
                 qc_ref, vc_ref, gc_ref, zfc_ref, zbc_ref,
                 qx_ref, vx_ref, gx_ref, zfx_ref, zbx_ref,
                 yc_ref, yx_ref, ofc_ref, obc_ref, ofx_ref, obx_ref, st_ref):
    cs = HGRN_CHUNK
    hp = SCAN_HEADS
    levels = _hgrn_levels(cs)
    st_ref[...] = jnp.zeros_like(st_ref)

    for (q_ref, v_ref, zf_ref, zb_ref, of_ref, ob_ref) in (
            (qc_ref, vc_ref, zfc_ref, zbc_ref, ofc_ref, obc_ref),
            (qx_ref, vx_ref, zfx_ref, zbx_ref, ofx_ref, obx_ref)):
        n_chunks = q_ref.shape[0] // cs

        def step(i, carry, q_ref=q_ref, v_ref=v_ref, zf_ref=zf_ref, zb_ref=zb_ref, of_ref=of_ref,
                 ob_ref=ob_ref, n_chunks=n_chunks):
            chains = []
            for d in (0, 1):
                start = pl.multiple_of(((n_chunks - 1 - i) if d else i) * cs, cs)
                z_ref = zb_ref if d else zf_ref
                for hh in range(hp):
                    sl = _head_slice(hh)
                    lb = lb_ref[hh]
                    q = _silu(q_ref[pl.ds(start, cs), sl].astype(F32))
                    f = lb + (1.0 - lb) * (1.0 / (1.0 + jnp.exp(-z_ref[pl.ds(start, cs), sl])))
                    chains.append(dict(d=d, c=2 * hh + d, start=start, sl=sl, q=q, k=1.0 - f,
                                       parts=_split3(jnp.log2(f)), v=v_ref[pl.ds(start, cs), sl],
                                       o_ref=ob_ref if d else of_ref))
            for ch in chains:
                tri = tri_ref[ch["d"]]
                hi, mid, lo = ch["parts"]
                ch["bsum"] = (jnp.dot(tri, hi, preferred_element_type=F32)
                              + jnp.dot(tri, mid, preferred_element_type=F32)
                              + jnp.dot(tri, lo, preferred_element_type=F32))
            for ch in chains:
                ch["qb"], ch["kb"] = ch["q"].astype(BF16), ch["k"].astype(BF16)
                ch["att"] = _mm_nt(ch["qb"], ch["kb"]).astype(BF16) * mask_ref[ch["d"], len(levels)]
            for li, h in enumerate(levels):
                for ch in chains:
                    ref = _level_reference(ch["bsum"], h, ch["d"] == 1)
                    w = jnp.exp2(_neg_abs(ch["bsum"] - ref)).astype(BF16)
                    ch["att"] += _mm_nt(ch["qb"] * w, ch["kb"] * w).astype(BF16) * mask_ref[ch["d"], li]
            for ch in chains:
                ch["state"] = st_ref[ch["c"]]
                o = _mm(ch["att"], ch["v"]) + _mm_nt(ch["q"] * jnp.exp2(ch["bsum"]), ch["state"])
                ch["o_ref"][pl.ds(ch["start"], cs), ch["sl"]] = o.astype(ch["o_ref"].dtype)
            for ch in chains:
                bsum = ch["bsum"]
                b_end = bsum[0:1, :] if ch["d"] else bsum[cs - 1:cs, :]
                st_ref[ch["c"]] = (ch["state"] * jnp.exp2(b_end)
                                   + _mm_tn(ch["v"], ch["k"] * jnp.exp2(b_end - bsum)))
            return carry

        lax.fori_loop(0, n_chunks, step, 0)

    _finish_scan(ofc_ref, obc_ref, gc_ref, yc_ref, cs)
    _finish_scan(ofx_ref, obx_ref, gx_ref, yx_ref, cs)


def _hgrn(proj_c, proj_x, z_c, z_x, lb, first):
    b, tc, _ = proj_c.shape
    tx = proj_x.shape[1]
    h, hp = HGRN_HEADS, SCAN_HEADS
    w = hp * HEAD_DIM
    tri, masks = _hgrn_constants(HGRN_CHUNK)

    def col(t, group):
        return pl.BlockSpec((None, t, w), lambda bi, hi: (bi, 0, (first + group * h) // hp + hi),
                            pipeline_mode=pl.Buffered(1))

    def zcol(t, group):
        return pl.BlockSpec((None, t, w), lambda bi, hi: (bi, 0, group * (h // hp) + hi),
                            pipeline_mode=pl.Buffered(1))

    ins = [col(tc, 0), col(tc, 1), col(tc, 2), zcol(tc, 0), zcol(tc, 1),
           col(tx, 0), col(tx, 1), col(tx, 2), zcol(tx, 0), zcol(tx, 1)]
    return pl.pallas_call(
        _hgrn_kernel,
        grid=(b, h // hp),
        in_specs=[pl.BlockSpec((hp, 1, HEAD_DIM), lambda bi, hi: (hi, 0, 0)),
                  pl.BlockSpec(tri.shape, lambda bi, hi: (0, 0, 0)),
                  pl.BlockSpec(masks.shape, lambda bi, hi: (0, 0, 0, 0))] + ins,
        out_specs=[pl.BlockSpec((None, tc, w), lambda bi, hi: (bi, 0, hi)),
                   pl.BlockSpec((None, tx, w), lambda bi, hi: (bi, 0, hi))],
        out_shape=[jax.ShapeDtypeStruct((b, tc, h * HEAD_DIM), BF16),
                   jax.ShapeDtypeStruct((b, tx, h * HEAD_DIM), BF16)],
        scratch_shapes=[pltpu.VMEM((tc, w), BF16), pltpu.VMEM((tc, w), BF16),
                        pltpu.VMEM((tx, w), BF16), pltpu.VMEM((tx, w), BF16),
                        pltpu.VMEM((2 * hp, HEAD_DIM, HEAD_DIM), F32)],
        compiler_params=_cparams("parallel", "parallel"),
        name="hgrn2",
    )(lb.reshape(h, 1, HEAD_DIM), tri, masks, proj_c, proj_c, proj_c, z_c, z_c,
      proj_x, proj_x, proj_x, z_x, z_x)


MLA_QW = 2 * LANE
MLA_SCORE_SCALE = (MLA_NOPE + MLA_ROPE) ** -0.5 * math.log2(math.e)
MLA_DOWN_PAD = MLA_Q_RANK + MLA_KV_RANK + LANE


def _mla_proj_kernel(x_ref, mod_ref, g_ref, wd_ref, gq_ref, gkv_ref, wq_ref, wkv_ref, cos_ref, sin_ref,
                     *out_refs, rope, want_q):
    h = _modnorm(x_ref[...], g_ref[...], mod_ref, 0, 1).astype(BF16)
    down = jnp.dot(h, wd_ref[...], preferred_element_type=F32)
    if rope:
        cos = cos_ref[...]
        sin = sin_ref[...]

    def rot(v):
        return v * cos + pltpu.roll(v, LANE // 2, 1) * sin if rope else v

    if want_q:
        q_ref, kn_ref, kr_ref, v_ref = out_refs
        qn = (_rms(down[:, :MLA_Q_RANK]) * gq_ref[...]).astype(BF16)
        q = jnp.dot(qn, wq_ref[...], preferred_element_type=F32) * MLA_SCORE_SCALE
        for hd in range(MLA_HEADS):
            lo = hd * MLA_QW
            q_ref[:, lo:lo + LANE] = q[:, lo:lo + LANE].astype(q_ref.dtype)
            q_ref[:, lo + LANE:lo + 2 * LANE] = rot(q[:, lo + LANE:lo + 2 * LANE]).astype(q_ref.dtype)
    else:
        kn_ref, kr_ref, v_ref = out_refs
    kvn = (_rms(down[:, MLA_Q_RANK:MLA_Q_RANK + MLA_KV_RANK]) * gkv_ref[...]).astype(BF16)
    kv = jnp.dot(kvn, wkv_ref[...], preferred_element_type=F32)
    nk = MLA_HEADS * MLA_NOPE
    kn_ref[...] = kv[:, :nk].astype(kn_ref.dtype)
    v_ref[...] = kv[:, nk:].astype(v_ref.dtype)
    kr_ref[...] = rot(down[:, MLA_Q_RANK + MLA_KV_RANK:]).astype(kr_ref.dtype)


def _mla_proj(x, mod, g, wd, gq, gkv, wq, wkv, cos, sin, rope, want_q, tm=512):
    b, t, d = x.shape
    tm = _pick(t, tm)
    const = lambda bi, i: (0, 0)
    tok = lambda bi, i: (bi, i, 0)
    nk, nv = MLA_HEADS * MLA_NOPE, MLA_HEADS * MLA_V
    out_shape = [jax.ShapeDtypeStruct((b, t, nk), BF16), jax.ShapeDtypeStruct((b, t, LANE), BF16),
                 jax.ShapeDtypeStruct((b, t, nv), BF16)]
    out_specs = [pl.BlockSpec((None, tm, nk), tok), pl.BlockSpec((None, tm, LANE), tok),
                 pl.BlockSpec((None, tm, nv), tok)]
    if want_q:
        out_shape = [jax.ShapeDtypeStruct((b, t, MLA_HEADS * MLA_QW), BF16)] + out_shape
        out_specs = [pl.BlockSpec((None, tm, MLA_HEADS * MLA_QW), tok)] + out_specs
    return pl.pallas_call(
        functools.partial(_mla_proj_kernel, rope=rope, want_q=want_q),
        grid=(b, t // tm),
        in_specs=[pl.BlockSpec((None, tm, d), tok), _mod_spec(mod), pl.BlockSpec((1, d), const),
                  pl.BlockSpec(wd.shape, const), pl.BlockSpec((1, MLA_Q_RANK), const),
                  pl.BlockSpec((1, MLA_KV_RANK), const), pl.BlockSpec(wq.shape, const),
                  pl.BlockSpec(wkv.shape, const),
                  pl.BlockSpec((tm, LANE), lambda bi, i: (i, 0)),
                  pl.BlockSpec((tm, LANE), lambda bi, i: (i, 0))],
        out_specs=out_specs,
        out_shape=out_shape,
        compiler_params=_cparams("parallel", "parallel"),
        name="mla_proj",
    )(x, mod, g.reshape(1, d), wd, gq.reshape(1, -1), gkv.reshape(1, -1), wq, wkv, cos, sin)


def _lane_fold(x, op):
    out = x[:, :LANE]
    for j in range(1, x.shape[1] // LANE):
        out = op(out, x[:, j * LANE:(j + 1) * LANE])
    return out


def _attention_kernel(q_ref, knc_ref, krc_ref, vc_ref, knx_ref, krx_ref, vx_ref, o_ref,
                      kc_ref, kx_ref, vc2_ref, vx2_ref, sc0_ref, sc1_ref, sx0_ref, sx1_ref, m0_ref, m1_ref,
                      *, tk):
    step = pl.program_id(2)

    @pl.when(step == 0)
    def _():
        kc_ref[:, :LANE] = knc_ref[...]
        kc_ref[:, LANE:] = krc_ref[...]
        kx_ref[:, :LANE] = knx_ref[...]
        kx_ref[:, LANE:] = krx_ref[...]
        vc2_ref[:, :LANE] = vc_ref[...]
        vx2_ref[:, :LANE] = vx_ref[...]
        for ref in (vc2_ref, vx2_ref):
            lane = lax.broadcasted_iota(jnp.int32, (ref.shape[0], LANE), 1)
            ref[:, LANE:] = jnp.where(lane == 0, 1.0, 0.0).astype(ref.dtype)

    n_blocks = kx_ref.shape[0] // tk
    n_steps = pl.num_programs(2)
    nt = (((1,), (1,)), ((), ()))

    def body(cur, prev):
        if cur is not None:
            sc_cur, sx_cur, m_cur = cur
            q = q_ref[...]
            s = lax.dot_general(q, kc_ref[...], nt, preferred_element_type=F32)
            sc_cur[...] = s
            m_part = _lane_fold(s, jnp.maximum)
        if prev is not None:
            sc_prev, sx_prev, m_prev = prev
            m = jnp.max(m_prev[...], axis=-1, keepdims=True)
            p = jnp.exp2(sc_prev[...] - m)
            acc = jnp.dot(p.astype(BF16), vc2_ref[...], preferred_element_type=F32)
        for i in range(n_blocks):
            if cur is not None:
                s = lax.dot_general(q, kx_ref[i * tk:(i + 1) * tk, :], nt, preferred_element_type=F32)
                sx_cur[i] = s
                m_part = jnp.maximum(m_part, _lane_fold(s, jnp.maximum))
            if prev is not None:
                p = jnp.exp2(sx_prev[i] - m)
                acc = acc + jnp.dot(p.astype(BF16), vx2_ref[i * tk:(i + 1) * tk, :],
                                    preferred_element_type=F32)
        if cur is not None:
            m_cur[...] = m_part
        if prev is not None:
            o_ref[...] = (acc[:, :MLA_V] / acc[:, MLA_V:MLA_V + 1]).astype(o_ref.dtype)

    bufs = ((sc0_ref, sx0_ref, m0_ref), (sc1_ref, sx1_ref, m1_ref))
    first = step == 0
    last = step == n_steps - 1
    inner = jnp.logical_not(first | last)

    @pl.when(first)
    def _():
        body(bufs[0], None)

    @pl.when(inner & (step % 2 == 0))
    def _():
        body(bufs[0], bufs[1])

    @pl.when(inner & (step % 2 == 1))
    def _():
        body(bufs[1], bufs[0])

    for parity in (0, 1):
        @pl.when(last & (step % 2 == parity))
        def _(parity=parity):
            body(None, bufs[1 - parity])


def _attention(q, kn_c, kr_c, v_c, kn_x, kr_x, v_x, tq=512, tk=256):
    b, t, _ = q.shape
    tc = kn_c.shape[1]
    tq = _pick(t, tq)
    tk = _pick(t, tk)
    nq = t // tq
    head_c = lambda bi, hi, i: (bi, 0, hi)
    shared = lambda bi, hi, i: (bi, 0, 0)
    return pl.pallas_call(
        functools.partial(_attention_kernel, tk=tk),
        grid=(b, MLA_HEADS, nq + 1),
        in_specs=[pl.BlockSpec((None, tq, MLA_QW), lambda bi, hi, i: (bi, jnp.minimum(i, nq - 1), hi)),
                  pl.BlockSpec((None, tc, LANE), head_c), pl.BlockSpec((None, tc, LANE), shared),
                  pl.BlockSpec((None, tc, LANE), head_c),
                  pl.BlockSpec((None, t, LANE), head_c), pl.BlockSpec((None, t, LANE), shared),
                  pl.BlockSpec((None, t, LANE), head_c)],
        out_specs=pl.BlockSpec((None, tq, MLA_V), lambda bi, hi, i: (bi, jnp.maximum(i - 1, 0), hi)),
        out_shape=jax.ShapeDtypeStruct((b, t, MLA_HEADS * MLA_V), BF16),
        scratch_shapes=[pltpu.VMEM((tc, MLA_QW), BF16), pltpu.VMEM((t, MLA_QW), BF16),
                        pltpu.VMEM((tc, 2 * MLA_V), BF16), pltpu.VMEM((t, 2 * MLA_V), BF16),
                        pltpu.VMEM((tq, tc), F32), pltpu.VMEM((tq, tc), F32),
                        pltpu.VMEM((t // tk, tq, tk), F32), pltpu.VMEM((t // tk, tq, tk), F32),
                        pltpu.VMEM((tq, LANE), F32), pltpu.VMEM((tq, LANE), F32)],
        compiler_params=_cparams("parallel", "parallel", "arbitrary"),
        name="mla_attention",
    )(q, kn_c, kr_c, v_c, kn_x, kr_x, v_x)


def _router_kernel(x_ref, mod_ref, g_ref, wr_ref, h_ref, w_ref, e_ref):
    h = _modnorm(x_ref[...], g_ref[...], mod_ref, 3, 4)
    h_ref[...] = _pack_halves(h)
    h_hi = h.astype(BF16)
    h_lo = (h - h_hi.astype(F32)).astype(BF16)
    w_hi = wr_ref[...].astype(BF16)
    w_lo = (wr_ref[...] - w_hi.astype(F32)).astype(BF16)
    logits = (jnp.dot(h_hi, w_hi, preferred_element_type=F32) + jnp.dot(h_lo, w_hi, preferred_element_type=F32)
              + jnp.dot(h_hi, w_lo, preferred_element_type=F32))
    lane = lax.broadcasted_iota(jnp.int32, logits.shape, 1)
    neg = jnp.float32(-jnp.inf)
    lg = jnp.where(lane < N_EXPERTS, logits, neg)
    m1 = jnp.max(lg, axis=-1, keepdims=True)
    i1 = jnp.min(jnp.where(lg == m1, lane, LANE), axis=-1, keepdims=True)
    lg2 = jnp.where(lane == i1, neg, lg)
    m2 = jnp.max(lg2, axis=-1, keepdims=True)
    i2 = jnp.min(jnp.where(lg2 == m2, lane, LANE), axis=-1, keepdims=True)
    e2 = jnp.exp(m2 - m1)
    w1 = 1.0 / (1.0 + e2)
    w2 = e2 / (1.0 + e2)
    w_ref[...] = jnp.where(lane == 0, w1, jnp.where(lane == 1, w2, 0.0))
    e_ref[...] = jnp.where(lane == 0, i1, jnp.where(lane == 1, i2, 0))


def _router(x, mod, g, router_pad, tm=512):
    b, t, d = x.shape
    tm = _pick(t, tm)
    tok = lambda bi, i: (bi, i, 0)
    const = lambda bi, i: (0, 0)
    return pl.pallas_call(
        _router_kernel,
        grid=(b, t // tm),
        in_specs=[pl.BlockSpec((None, tm, d), tok), _mod_spec(mod), pl.BlockSpec((1, d), const),
                  pl.BlockSpec((d, LANE), const)],
        out_specs=[pl.BlockSpec((None, tm, d // 2), tok), pl.BlockSpec((None, tm, LANE), tok),
                   pl.BlockSpec((None, tm, LANE), tok)],
        out_shape=[jax.ShapeDtypeStruct((b, t, d // 2), jnp.int32), jax.ShapeDtypeStruct((b, t, LANE), F32),
                   jax.ShapeDtypeStruct((b, t, LANE), jnp.int32)],
        compiler_params=_cparams("parallel", "parallel"),
        name="moe_router",
    )(x, mod, g.reshape(1, d), router_pad)


SC_GATHER_ROWS = 64


def _gather_rows(table, idx):
    info = plsc.get_sparse_core_info()
    n_workers = info.num_cores * info.num_subcores
    n, width = idx.shape[0], table.shape[1]
    chunk = SC_GATHER_ROWS
    assert n % (n_workers * chunk) == 0
    n_chunks = n // (n_workers * chunk)
    mesh = plsc.VectorSubcoreMesh(core_axis_name="c", subcore_axis_name="s")

    def body(table_hbm, idx_hbm, out_hbm, idx_v, rows_v, sem):
        wid = lax.axis_index("s") * info.num_cores + lax.axis_index("c")
        pltpu.sync_copy(idx_hbm.at[wid], idx_v)

        @pl.loop(0, n_chunks)
        def _(j):
            pltpu.async_copy(table_hbm.at[idx_v.at[j]], rows_v, sem).wait()
            pltpu.sync_copy(rows_v, out_hbm.at[pl.ds((wid * n_chunks + j) * chunk, chunk)])

    return pl.kernel(
        body,
        out_type=jax.ShapeDtypeStruct((n, width), table.dtype),
        mesh=mesh,
        scratch_types=[pltpu.VMEM((n_chunks, chunk), jnp.int32), pltpu.VMEM((chunk, width), table.dtype),
                       pltpu.SemaphoreType.DMA],
        name="sc_gather_rows",
    )(table, idx.reshape(n_workers, n_chunks, chunk))


def _scatter_rows_twice(table, idx0, idx1, n_out):
    info = plsc.get_sparse_core_info()
    n_workers = info.num_cores * info.num_subcores
    n, width = table.shape
    chunk = SC_GATHER_ROWS
    assert n % (n_workers * chunk) == 0
    n_chunks = n // (n_workers * chunk)
    mesh = plsc.VectorSubcoreMesh(core_axis_name="c", subcore_axis_name="s")

    def body(table_hbm, i0_hbm, i1_hbm, out_hbm, i0_v, i1_v, rows_v):
        wid = lax.axis_index("s") * info.num_cores + lax.axis_index("c")
        pltpu.sync_copy(i0_hbm.at[wid], i0_v)
        pltpu.sync_copy(i1_hbm.at[wid], i1_v)

        @pl.loop(0, n_chunks)
        def _(j):
            pltpu.sync_copy(table_hbm.at[pl.ds((wid * n_chunks + j) * chunk, chunk)], rows_v)
            pltpu.sync_copy(rows_v, out_hbm.at[i0_v.at[j]])
            pltpu.sync_copy(rows_v, out_hbm.at[i1_v.at[j]])

    shape3 = (n_workers, n_chunks, chunk)
    return pl.kernel(
        body,
        out_type=jax.ShapeDtypeStruct((n_out, width), table.dtype),
        mesh=mesh,
        scratch_types=[pltpu.VMEM((n_chunks, chunk), jnp.int32), pltpu.VMEM((n_chunks, chunk), jnp.int32),
                       pltpu.VMEM((chunk, width), table.dtype)],
        name="sc_scatter_rows",
    )(table, idx0.reshape(shape3), idx1.reshape(shape3))


def _pack_halves(x):
    k = x.shape[1] // 2
    bits = lambda v: lax.bitcast_convert_type(v.astype(BF16).astype(F32), jnp.uint32)
    word = lax.shift_right_logical(bits(x[:, :k]), jnp.uint32(16)) | bits(x[:, k:])
    return lax.bitcast_convert_type(word, jnp.int32)


def _unpack_halves(w):
    u = lax.bitcast_convert_type(w, jnp.uint32)
    lo = lax.bitcast_convert_type(lax.shift_left(u, jnp.uint32(16)), F32)
    hi = lax.bitcast_convert_type(u & jnp.uint32(0xFFFF0000), F32)
    return jnp.concatenate([lo, hi], axis=-1)


def _expert_ffn_kernel(te_ref, tv_ref, x_ref, wa_ref, wb_ref, wo_ref, o_ref, acc_ref):
    j = pl.program_id(0)
    f = pl.program_id(1)
    last = pl.num_programs(1) - 1
    valid = tv_ref[j] != 0

    @pl.when(valid)
    def _():
        @pl.when(f == 0)
        def _():
            acc_ref[...] = jnp.zeros_like(acc_ref)

        h = _unpack_halves(x_ref[...]).astype(BF16)
        a = jnp.dot(h, wa_ref[...], preferred_element_type=F32)
        bb = jnp.dot(h, wb_ref[...], preferred_element_type=F32)
        acc_ref[...] += jnp.dot((_silu(a) * bb).astype(BF16), wo_ref[...], preferred_element_type=F32)

        @pl.when(f == last)
        def _():
            o_ref[...] = _pack_halves(acc_ref[...])

    @pl.when(jnp.logical_not(valid) & (f == last))
    def _():
        o_ref[...] = jnp.zeros_like(o_ref)


def _expert_ffn(xs, tile_expert, tile_valid, w_in, w_out, tm, tf=2816):
    p = xs.shape[0]
    n_exp, fdim, d = w_out.shape
    tf = _pick(fdim, tf)
    nf = fdim // tf

    def f_eff(j, f, tv):
        return jnp.where(tv[j] != 0, f, nf - 1)

    grid_spec = pltpu.PrefetchScalarGridSpec(
        num_scalar_prefetch=2,
        grid=(p // tm, nf),
        in_specs=[pl.BlockSpec((tm, d // 2), lambda j, f, te, tv: (j, 0)),
                  pl.BlockSpec((None, d, tf), lambda j, f, te, tv: (te[j], 0, f_eff(j, f, tv)),
                               pipeline_mode=pl.Buffered(1)),
                  pl.BlockSpec((None, d, tf), lambda j, f, te, tv: (te[j], 0, f_eff(j, f, tv) + nf),
                               pipeline_mode=pl.Buffered(1)),
                  pl.BlockSpec((None, tf, d), lambda j, f, te, tv: (te[j], f_eff(j, f, tv), 0),
                               pipeline_mode=pl.Buffered(1))],
        out_specs=pl.BlockSpec((tm, d // 2), lambda j, f, te, tv: (j, 0)),
        scratch_shapes=[pltpu.VMEM((tm, d), F32)])
    return pl.pallas_call(
        _expert_ffn_kernel,
        grid_spec=grid_spec,
        out_shape=jax.ShapeDtypeStruct((p, d // 2), jnp.int32),
        compiler_params=_cparams("parallel", "arbitrary"),
        name="moe_expert_ffn",
    )(tile_expert, tile_valid, xs, w_in, w_in, w_out)


def _moe_combine_kernel(x_ref, y0_ref, y1_ref, w_ref, mod_ref, g_ref, o_ref):
    y = w_ref[:, 0:1] * _unpack_halves(y0_ref[...]) + w_ref[:, 1:2] * _unpack_halves(y1_ref[...])
    o_ref[...] = x_ref[...] + mod_ref[5:6, :] * (_rms(y) * g_ref[...])


def _moe_combine(x, y2, w, mod, g, tm=512):
    b, t, d = x.shape
    tm = _pick(t, tm)
    tok = lambda bi, i: (bi, i, 0)
    return pl.pallas_call(
        _moe_combine_kernel,
        grid=(b, t // tm),
        in_specs=[pl.BlockSpec((None, tm, d), tok),
                  pl.BlockSpec((None, None, tm, d // 2), lambda bi, i: (0, bi, i, 0)),
                  pl.BlockSpec((None, None, tm, d // 2), lambda bi, i: (1, bi, i, 0)),
                  pl.BlockSpec((None, tm, LANE), tok), _mod_spec(mod),
                  pl.BlockSpec((1, d), lambda bi, i: (0, 0))],
        out_specs=pl.BlockSpec((None, tm, d), tok),
        out_shape=jax.ShapeDtypeStruct((b, t, d), F32),
        compiler_params=_cparams("parallel", "parallel"),
        name="moe_combine",
    )(x, y2, y2, w, mod, g.reshape(1, d))


def _moe(x, mod, g_in, g_out, router_pad, w_in, w_out, tm=512):
    b, t, d = x.shape
    n_exp = w_out.shape[0]
    n = b * t
    h, w, e = _router(x, mod, g_in, router_pad)
    e_flat = jnp.concatenate([e[..., 0].reshape(n), e[..., 1].reshape(n)])
    onehot = (e_flat[:, None] == jnp.arange(n_exp, dtype=jnp.int32)).astype(jnp.int32)
    csum = jnp.cumsum(onehot, axis=0)
    rank = jnp.sum((csum - onehot) * onehot, axis=-1)
    counts = csum[-1]
    ends = jnp.cumsum((counts + tm - 1) // tm * tm)
    dest = (ends - (counts + tm - 1) // tm * tm)[e_flat] + rank
    p = 2 * n + n_exp * tm
    tile_start = jnp.arange(p // tm, dtype=jnp.int32) * tm
    tile_expert = jnp.minimum(jnp.searchsorted(ends, tile_start, side="right"), n_exp - 1).astype(jnp.int32)
    tile_valid = (tile_start < ends[-1]).astype(jnp.int32)

    xs = _scatter_rows_twice(h.reshape(n, d // 2), dest[:n], dest[n:], p)
    ys = _expert_ffn(xs, tile_expert, tile_valid, w_in, w_out, tm)
    y2 = _gather_rows(ys, dest)
    return _moe_combine(x, y2.reshape(2, b, t, d // 2), w, mod, g_out)


def _axial_angles(n_tokens, dim):
    t = jnp.arange(n_tokens)
    rows = (t // GRID_W).astype(F32)
    cols = (t % GRID_W).astype(F32)
    n_freq = dim // 4
    inv = ROPE_BASE ** (-jnp.arange(n_freq, dtype=F32) / n_freq)
    return jnp.concatenate([rows[:, None] * inv, cols[:, None] * inv], axis=-1)


def _rope_tables(n_tokens, dim):
    ang = _axial_angles(n_tokens, dim)
    cos, sin = jnp.cos(ang), jnp.sin(ang)
    if dim == LANE:
        return jnp.concatenate([cos, cos], -1), jnp.concatenate([-sin, sin], -1)
    z = jnp.zeros_like(cos)
    return jnp.concatenate([cos, z, cos, z], -1), jnp.concatenate([-sin, z, sin, z], -1)


def _spread_rope_cols(w):
    half = MLA_ROPE // 2
    z = jnp.zeros(w.shape[:-1] + (half,), w.dtype)
    return jnp.concatenate([w[..., :half], z, w[..., half:], z], axis=-1)


def kernel(x, c, ctx, c_ctx, mod_w, mod_b, norm_g, mix_in_w, ret_decay_logit, hgrn_lb_logit, mix_out_w,
           ffn_in_w, ffn_out_w, mla_down_w, mla_q_norm_g, mla_kv_norm_g, mla_uq_w, mla_ukv_w, mla_out_w,
           router_w, moe_in_w, moe_out_w):
    batch, seq, d = x.shape
    assert mod_w.shape[0] == 2, "two layers: retention/HGRN2 then MLA/MoE"

    rows = -(-(batch + 1) // 8) * 8
    cond = jnp.zeros((rows, d), F32).at[:batch].set(c).at[batch].set(c_ctx)
    mod = _modulation(cond, mod_w, mod_b)
    mod_x = [mod[l, :batch].reshape(batch, 6, d) for l in range(2)]
    mod_c = [mod[l, batch:batch + 1].reshape(1, 6, d) for l in range(2)]

    g = norm_g[0]
    ret_w = RET_HEADS * HEAD_DIM
    z_lo, z_hi = 5 * ret_w, 7 * ret_w
    w_in = mix_in_w[0].astype(BF16)
    w_main = jnp.concatenate([w_in[:, :z_lo], w_in[:, z_hi:]], axis=1)
    w_z = w_in[:, z_lo:z_hi]
    proj_x = _norm_matmul(x, mod_x[0], g[0], w_main, 0, 1, BF16)
    proj_c = _norm_matmul(ctx, mod_c[0], g[0], w_main, 0, 1, BF16)
    z_x = _norm_matmul(x, mod_x[0], g[0], w_z, 0, 1, F32)
    z_c = _norm_matmul(ctx, mod_c[0], g[0], w_z, 0, 1, F32)
    cos2, sin2 = _rope_tables(seq, HEAD_DIM)
    log_gamma = jax.nn.log_sigmoid(ret_decay_logit[0].astype(F32))
    lb = jnp.cumsum(jax.nn.softmax(hgrn_lb_logit.astype(F32), axis=0), axis=0)[0]
    ret_c, ret_x = _retention(proj_c, proj_x, log_gamma, cos2, sin2)
    hg_c, hg_x = _hgrn(proj_c, proj_x, z_c, z_x, lb, first=4 * RET_HEADS)
    w_out = mix_out_w[0].astype(BF16)
    w_outs = [w_out[:ret_w], w_out[ret_w:]]
    x = _proj_residual([ret_x, hg_x], w_outs, x, mod_x[0], g[1], 2)
    ctx = _proj_residual([ret_c, hg_c], w_outs, ctx, mod_c[0], g[1], 2)
    f_in, f_out = ffn_in_w[0].astype(BF16), ffn_out_w[0].astype(BF16)
    x = _ffn(x, mod_x[0], g[2], g[3], f_in, f_out)
    ctx = _ffn(ctx, mod_c[0], g[2], g[3], f_in, f_out)

    g = norm_g[1]
    dq = MLA_NOPE + MLA_ROPE
    wd = mla_down_w[0]
    wd = jnp.concatenate([wd[:, :MLA_Q_RANK + MLA_KV_RANK],
                          _spread_rope_cols(wd[:, MLA_Q_RANK + MLA_KV_RANK:])], axis=-1).astype(BF16)
    wq = mla_uq_w[0].reshape(MLA_Q_RANK, MLA_HEADS, dq)
    wq = jnp.concatenate([wq[..., :MLA_NOPE], _spread_rope_cols(wq[..., MLA_NOPE:])], axis=-1)
    wq = wq.reshape(MLA_Q_RANK, MLA_HEADS * MLA_QW).astype(BF16)
    wkv = mla_ukv_w[0].reshape(MLA_KV_RANK, MLA_HEADS, MLA_NOPE + MLA_V)
    wkv = jnp.concatenate([wkv[..., :MLA_NOPE].reshape(MLA_KV_RANK, -1),
                           wkv[..., MLA_NOPE:].reshape(MLA_KV_RANK, -1)], axis=-1).astype(BF16)
    cos_m, sin_m = _rope_tables(seq, MLA_ROPE)
    q, kn_x, kr_x, v_x = _mla_proj(x, mod_x[1], g[0], wd, mla_q_norm_g[0], mla_kv_norm_g[0], wq, wkv,
                                   cos_m, sin_m, rope=True, want_q=True)
    tc = ctx.shape[1]
    kn_c, kr_c, v_c = _mla_proj(ctx, mod_c[1], g[0], wd, mla_q_norm_g[0], mla_kv_norm_g[0], wq, wkv,
                                cos_m[:tc], sin_m[:tc], rope=False, want_q=False)
    att = _attention(q, kn_c, kr_c, v_c, kn_x, kr_x, v_x)
    x = _proj_residual([att], [mla_out_w[0].astype(BF16)], x, mod_x[1], g[1], 2)
    router_pad = jnp.zeros((d, LANE), F32).at[:, :N_EXPERTS].set(router_w[0])
    x = _moe(x, mod_x[1], g[2], g[3], router_pad, moe_in_w[0].astype(BF16), moe_out_w[0].astype(BF16))
    return x
```

```python
import functools
import math

import jax
import jax.numpy as jnp
import numpy as np
from jax import lax
from jax.experimental import pallas as pl
from jax.experimental.pallas import tpu as pltpu
from jax.experimental.pallas import tpu_sc as plsc

F32 = jnp.float32
BF16 = jnp.bfloat16

NORM_EPS = 1e-6
ROPE_BASE = 10000.0
GRID_W = 64
LANE = 128

RET_HEADS = 4
HGRN_HEADS = 4
HEAD_DIM = 128
RET_CHUNK = 128
HGRN_CHUNK = 128

MLA_HEADS = 8
MLA_NOPE = 128
MLA_ROPE = 64
MLA_V = 128
MLA_Q_RANK = 384
MLA_KV_RANK = 256
N_EXPERTS = 8

VMEM_LIMIT = 56 * 1024 * 1024


def _cparams(*sem):
    return pltpu.CompilerParams(dimension_semantics=sem, vmem_limit_bytes=VMEM_LIMIT)


def _mm(a, b):
    return jnp.dot(a.astype(BF16), b.astype(BF16), preferred_element_type=F32)


def _mm_nt(a, b):
    return lax.dot_general(a.astype(BF16), b.astype(BF16), (((1,), (1,)), ((), ())),
                           preferred_element_type=F32)


def _mm_tn(a, b):
    return lax.dot_general(a.astype(BF16), b.astype(BF16), (((0,), (0,)), ((), ())),
                           preferred_element_type=F32)


def _rms(x):
    return x * lax.rsqrt(jnp.mean(x * x, axis=-1, keepdims=True) + NORM_EPS)


def _silu(x):
    return x * (1.0 / (1.0 + jnp.exp(-x)))


def _modnorm(x, g, mod_ref, shift_row, scale_row):
    shift = mod_ref[shift_row:shift_row + 1, :]
    scale = mod_ref[scale_row:scale_row + 1, :]
    return _rms(x) * g * (1.0 + scale) + shift


def _mod_spec(mod):
    d = mod.shape[-1]
    if mod.shape[0] == 1:
        return pl.BlockSpec((None, 6, d), lambda b, *_: (0, 0, 0))
    return pl.BlockSpec((None, 6, d), lambda b, *_: (b, 0, 0))


def _pick(n, pref):
    t = min(pref, n)
    while n % t:
        t -= LANE
    return t


def _mod_kernel(c_ref, w_ref, b_ref, o_ref):
    a = _silu(c_ref[...])
    o_ref[...] = jnp.dot(a, w_ref[...], preferred_element_type=F32,
                         precision=lax.Precision.HIGHEST) + b_ref[...]


def _modulation(cond, mod_w, mod_b):
    n_layers, d, n = mod_w.shape
    r = cond.shape[0]
    tn = _pick(n, 1536)
    return pl.pallas_call(
        _mod_kernel,
        grid=(n_layers, n // tn),
        in_specs=[pl.BlockSpec((r, d), lambda l, j: (0, 0)),
                  pl.BlockSpec((None, d, tn), lambda l, j: (l, 0, j)),
                  pl.BlockSpec((None, 1, tn), lambda l, j: (l, 0, j))],
        out_specs=pl.BlockSpec((None, r, tn), lambda l, j: (l, 0, j)),
        out_shape=jax.ShapeDtypeStruct((n_layers, r, n), F32),
        compiler_params=_cparams("parallel", "parallel"),
        name="modulation",
    )(cond, mod_w, mod_b.reshape(n_layers, 1, n))


def _norm_matmul_kernel(x_ref, mod_ref, g_ref, w_ref, o_ref, h_ref, *, shift_row, scale_row):
    @pl.when(pl.program_id(2) == 0)
    def _():
        h_ref[...] = _modnorm(x_ref[...], g_ref[...], mod_ref, shift_row, scale_row).astype(BF16)

    o_ref[...] = jnp.dot(h_ref[...], w_ref[...], preferred_element_type=F32).astype(o_ref.dtype)


def _norm_matmul(x, mod, g, w, shift_row, scale_row, out_dtype, tm=1024, tn=1792):
    b, t, d = x.shape
    n = w.shape[1]
    tm = _pick(t, tm)
    tn = _pick(n, tn)
    return pl.pallas_call(
        functools.partial(_norm_matmul_kernel, shift_row=shift_row, scale_row=scale_row),
        grid=(b, t // tm, n // tn),
        in_specs=[pl.BlockSpec((None, tm, d), lambda bi, i, j: (bi, i, 0)),
                  _mod_spec(mod),
                  pl.BlockSpec((1, d), lambda bi, i, j: (0, 0)),
                  pl.BlockSpec((d, tn), lambda bi, i, j: (0, j))],
        out_specs=pl.BlockSpec((None, tm, tn), lambda bi, i, j: (bi, i, j)),
        out_shape=jax.ShapeDtypeStruct((b, t, n), out_dtype),
        scratch_shapes=[pltpu.VMEM((tm, d), BF16)],
        compiler_params=_cparams("parallel", "parallel", "arbitrary"),
        name="norm_matmul",
    )(x, mod, g.reshape(1, d), w)


def _proj_residual_kernel(*refs, n_lhs, gate_row):
    lhs = refs[:n_lhs]
    ws = refs[n_lhs:2 * n_lhs]
    x_ref, mod_ref, g_ref, o_ref = refs[2 * n_lhs:]
    y = jnp.dot(lhs[0][...], ws[0][...], preferred_element_type=F32)
    for a_ref, w_ref in zip(lhs[1:], ws[1:]):
        y += jnp.dot(a_ref[...], w_ref[...], preferred_element_type=F32)
    gate = mod_ref[gate_row:gate_row + 1, :]
    o_ref[...] = x_ref[...] + gate * (_rms(y) * g_ref[...])


def _proj_residual(lhs, ws, x, mod, g, gate_row, tm=1024):
    b, t, d = x.shape
    tm = _pick(t, tm)
    n_lhs = len(lhs)
    in_specs = [pl.BlockSpec((None, tm, a.shape[-1]), lambda bi, i: (bi, i, 0)) for a in lhs]
    in_specs += [pl.BlockSpec(w.shape, lambda bi, i: (0, 0)) for w in ws]
    in_specs += [pl.BlockSpec((None, tm, d), lambda bi, i: (bi, i, 0)),
                 _mod_spec(mod),
                 pl.BlockSpec((1, d), lambda bi, i: (0, 0))]
    return pl.pallas_call(
        functools.partial(_proj_residual_kernel, n_lhs=n_lhs, gate_row=gate_row),
        grid=(b, t // tm),
        in_specs=in_specs,
        out_specs=pl.BlockSpec((None, tm, d), lambda bi, i: (bi, i, 0)),
        out_shape=jax.ShapeDtypeStruct((b, t, d), F32),
        compiler_params=_cparams("parallel", "parallel"),
        name="proj_residual",
    )(*lhs, *ws, x, mod, g.reshape(1, d))


def _ffn_kernel(x_ref, mod_ref, g_in_ref, g_out_ref, wa_ref, wb_ref, wo_ref, o_ref, h_ref, acc_ref):
    f = pl.program_id(2)

    @pl.when(f == 0)
    def _():
        h_ref[...] = _modnorm(x_ref[...], g_in_ref[...], mod_ref, 3, 4).astype(BF16)
        acc_ref[...] = jnp.zeros_like(acc_ref)

    h = h_ref[...]
    a = jnp.dot(h, wa_ref[...], preferred_element_type=F32)
    bb = jnp.dot(h, wb_ref[...], preferred_element_type=F32)
    acc_ref[...] += jnp.dot((_silu(a) * bb).astype(BF16), wo_ref[...], preferred_element_type=F32)

    @pl.when(f == pl.num_programs(2) - 1)
    def _():
        gate = mod_ref[5:6, :]
        o_ref[...] = x_ref[...] + gate * (_rms(acc_ref[...]) * g_out_ref[...])


def _ffn(x, mod, g_in, g_out, w_in, w_out, tm=512, tf=2816):
    b, t, d = x.shape
    fdim = w_out.shape[0]
    tm = _pick(t, tm)
    tf = _pick(fdim, tf)
    nf = fdim // tf
    return pl.pallas_call(
        _ffn_kernel,
        grid=(b, t // tm, nf),
        in_specs=[pl.BlockSpec((None, tm, d), lambda bi, i, f: (bi, i, 0)),
                  _mod_spec(mod),
                  pl.BlockSpec((1, d), lambda bi, i, f: (0, 0)),
                  pl.BlockSpec((1, d), lambda bi, i, f: (0, 0)),
                  pl.BlockSpec((d, tf), lambda bi, i, f: (0, f), pipeline_mode=pl.Buffered(1)),
                  pl.BlockSpec((d, tf), lambda bi, i, f: (0, f + nf), pipeline_mode=pl.Buffered(1)),
                  pl.BlockSpec((tf, d), lambda bi, i, f: (f, 0), pipeline_mode=pl.Buffered(1))],
        out_specs=pl.BlockSpec((None, tm, d), lambda bi, i, f: (bi, i, 0)),
        out_shape=jax.ShapeDtypeStruct((b, t, d), F32),
        scratch_shapes=[pltpu.VMEM((tm, d), BF16), pltpu.VMEM((tm, d), F32)],
        compiler_params=_cparams("parallel", "parallel", "arbitrary"),
        name="ffn",
    )(x, mod, g_in.reshape(1, d), g_out.reshape(1, d), w_in, w_in, w_out)


SCAN_HEADS = 4


def _head_slice(hh):
    return slice(hh * HEAD_DIM, (hh + 1) * HEAD_DIM)


def _finish_scan(of_ref, ob_ref, g_ref, y_ref, cs):
    def step(i, carry):
        start = pl.multiple_of(i * cs, cs)
        for hh in range(y_ref.shape[1] // HEAD_DIM):
            sl = _head_slice(hh)
            o = of_ref[pl.ds(start, cs), sl].astype(F32) + ob_ref[pl.ds(start, cs), sl].astype(F32)
            gate = g_ref[pl.ds(start, cs), sl].astype(F32)
            y_ref[pl.ds(start, cs), sl] = (_rms(o) * _silu(gate)).astype(y_ref.dtype)
        return carry

    lax.fori_loop(0, y_ref.shape[0] // cs, step, 0)


def _retention_kernel(lg_ref, qc_ref, kc_ref, vc_ref, gc_ref, qx_ref, kx_ref, vx_ref, gx_ref,
                      cos_ref, sin_ref, yc_ref, yx_ref,
                      ofc_ref, obc_ref, ofx_ref, obx_ref, s_ref, intra_ref, qd_ref, kd_ref):
    cs = RET_CHUNK
    hp = SCAN_HEADS
    head0 = pl.program_id(1) * hp
    row = lax.broadcasted_iota(jnp.int32, (cs, cs), 0).astype(F32)
    col = lax.broadcasted_iota(jnp.int32, (cs, cs), 1).astype(F32)
    pos = lax.broadcasted_iota(jnp.int32, (cs, HEAD_DIM), 0).astype(F32)
    k_scale = HEAD_DIM ** -0.5

    for hh in range(hp):
        for d in (0, 1):
            c = 2 * hh + d
            lg = lg_ref[d, head0 + hh]
            rel = (col - row) if d else (row - col)
            intra_ref[c] = jnp.where(rel >= 0, jnp.exp(lg * jnp.maximum(rel, 0.0)), 0.0)
            p = (cs - 1.0 - pos) if d else pos
            qd_ref[c] = jnp.exp(lg * (p + 1.0))
            kd_ref[c] = jnp.exp(lg * (cs - 1.0 - p))
    s_ref[...] = jnp.zeros_like(s_ref)

    for (q_ref, k_ref, v_ref, of_ref, ob_ref, rope) in (
            (qc_ref, kc_ref, vc_ref, ofc_ref, obc_ref, False),
            (qx_ref, kx_ref, vx_ref, ofx_ref, obx_ref, True)):
        n_chunks = q_ref.shape[0] // cs

        def step(i, carry, q_ref=q_ref, k_ref=k_ref, v_ref=v_ref, of_ref=of_ref, ob_ref=ob_ref,
                 rope=rope, n_chunks=n_chunks):
            chains = []
            for d in (0, 1):
                start = pl.multiple_of(((n_chunks - 1 - i) if d else i) * cs, cs)
                if rope:
                    cos = cos_ref[pl.ds(start, cs), :]
                    sin = sin_ref[pl.ds(start, cs), :]
                for hh in range(hp):
                    sl = _head_slice(hh)
                    q = q_ref[pl.ds(start, cs), sl].astype(F32)
                    k = k_ref[pl.ds(start, cs), sl].astype(F32) * k_scale
                    if rope:
                        q = q * cos + pltpu.roll(q, HEAD_DIM // 2, 1) * sin
                        k = k * cos + pltpu.roll(k, HEAD_DIM // 2, 1) * sin
                    chains.append(dict(d=d, hh=hh, c=2 * hh + d, start=start, sl=sl, q=q, k=k,
                                       v=v_ref[pl.ds(start, cs), sl], o_ref=ob_ref if d else of_ref))
            for ch in chains:
                ch["att"] = _mm_nt(ch["q"], ch["k"]) * intra_ref[ch["c"]]
            for ch in chains:
                c = ch["c"]
                ch["state"] = s_ref[c]
                ch["o_ref"][pl.ds(ch["start"], cs), ch["sl"]] = (
                    _mm(ch["att"], ch["v"]) + _mm(ch["q"] * qd_ref[c], ch["state"]))
            for ch in chains:
                c = ch["c"]
                c_dec = jnp.exp(jnp.full((1, HEAD_DIM), lg_ref[ch["d"], head0 + ch["hh"]] * cs, F32))
                s_ref[c] = ch["state"] * c_dec + _mm_tn(ch["k"] * kd_ref[c], ch["v"])
            return carry

        lax.fori_loop(0, n_chunks, step, 0)

    _finish_scan(ofc_ref, obc_ref, gc_ref, yc_ref, cs)
    _finish_scan(ofx_ref, obx_ref, gx_ref, yx_ref, cs)


def _retention(proj_c, proj_x, log_gamma, cos2, sin2):
    b, tc, _ = proj_c.shape
    tx = proj_x.shape[1]
    h, hp = RET_HEADS, SCAN_HEADS
    w = hp * HEAD_DIM

    def col(t, group):
        return pl.BlockSpec((None, t, w), lambda bi, hi, lg: (bi, 0, group * (h // hp) + hi),
                            pipeline_mode=pl.Buffered(1))

    n_chain = 2 * hp
    grid_spec = pltpu.PrefetchScalarGridSpec(
        num_scalar_prefetch=1,
        grid=(b, h // hp),
        in_specs=[col(tc, 0), col(tc, 1), col(tc, 2), col(tc, 3),
                  col(tx, 0), col(tx, 1), col(tx, 2), col(tx, 3),
                  pl.BlockSpec((tx, HEAD_DIM), lambda bi, hi, lg: (0, 0), pipeline_mode=pl.Buffered(1)),
                  pl.BlockSpec((tx, HEAD_DIM), lambda bi, hi, lg: (0, 0), pipeline_mode=pl.Buffered(1))],
        out_specs=[pl.BlockSpec((None, tc, w), lambda bi, hi, lg: (bi, 0, hi)),
                   pl.BlockSpec((None, tx, w), lambda bi, hi, lg: (bi, 0, hi))],
        scratch_shapes=[pltpu.VMEM((tc, w), F32), pltpu.VMEM((tc, w), F32),
                        pltpu.VMEM((tx, w), F32), pltpu.VMEM((tx, w), F32),
                        pltpu.VMEM((n_chain, HEAD_DIM, HEAD_DIM), F32),
                        pltpu.VMEM((n_chain, RET_CHUNK, RET_CHUNK), F32),
                        pltpu.VMEM((n_chain, RET_CHUNK, HEAD_DIM), F32),
                        pltpu.VMEM((n_chain, RET_CHUNK, HEAD_DIM), F32)])
    return pl.pallas_call(
        _retention_kernel,
        grid_spec=grid_spec,
        out_shape=[jax.ShapeDtypeStruct((b, tc, h * HEAD_DIM), BF16),
                   jax.ShapeDtypeStruct((b, tx, h * HEAD_DIM), BF16)],
        compiler_params=_cparams("parallel", "parallel"),
        name="retention",
    )(log_gamma, proj_c, proj_c, proj_c, proj_c, proj_x, proj_x, proj_x, proj_x, cos2, sin2)


def _hgrn_levels(cs):
    return [cs >> (i + 1) for i in range(int(math.log2(cs)))]


def _hgrn_constants(cs):
    t = np.arange(cs)[:, None]
    s = np.arange(cs)[None, :]
    tri = np.stack([(s <= t), (s >= t)]).astype(np.float32)
    masks = []
    for reverse in (False, True):
        lv = []
        for h in _hgrn_levels(cs):
            same = (t // (2 * h)) == (s // (2 * h))
            t_hi = (t % (2 * h)) >= h
            s_hi = (s % (2 * h)) >= h
            lv.append(same & (~t_hi & s_hi if reverse else t_hi & ~s_hi))
        lv.append(t == s)
        masks.append(np.stack(lv))
    return jnp.asarray(tri, BF16), jnp.asarray(np.stack(masks).astype(np.float32), BF16)


def _split3(x):
    hi = x.astype(BF16)
    r = x - hi.astype(F32)
    mid = r.astype(BF16)
    lo = (r - mid.astype(F32)).astype(BF16)
    return hi, mid, lo


def _neg_abs(x):
    bits = lax.bitcast_convert_type(x, jnp.uint32) | jnp.uint32(0x80000000)
    return lax.bitcast_convert_type(bits, F32)


def _level_reference(b, h, reverse):
    cs = b.shape[0]
    ref_local = h if reverse else h - 1
    if 2 * h >= 8:
        b3 = b.reshape(cs // (2 * h), 2 * h, b.shape[1])
        r = b3[:, ref_local:ref_local + 1, :]
        return jnp.broadcast_to(r, b3.shape).reshape(b.shape)
    local = lax.broadcasted_iota(jnp.int32, b.shape, 0) % (2 * h)
    out = b
    for m in range(2 * h):
        if m != ref_local:
            out = jnp.where(local == m, pltpu.roll(b, (m - ref_local) % cs, 0), out)
    return out


def _hgrn_kernel(lb_ref, tri_ref, mask_ref,
                 qc_ref, vc_ref, gc_ref, zfc_ref, zbc_ref,
                 qx_ref, vx_ref, gx_ref, zfx_ref, zbx_ref,
                 yc_ref, yx_ref, ofc_ref, obc_ref, ofx_ref, obx_ref, st_ref):
    cs = HGRN_CHUNK
    hp = SCAN_HEADS
    levels = _hgrn_levels(cs)
    st_ref[...] = jnp.zeros_like(st_ref)

    for (q_ref, v_ref, zf_ref, zb_ref, of_ref, ob_ref) in (
            (qc_ref, vc_ref, zfc_ref, zbc_ref, ofc_ref, obc_ref),
            (qx_ref, vx_ref, zfx_ref, zbx_ref, ofx_ref, obx_ref)):
        n_chunks = q_ref.shape[0] // cs

        def step(i, carry, q_ref=q_ref, v_ref=v_ref, zf_ref=zf_ref, zb_ref=zb_ref, of_ref=of_ref,
                 ob_ref=ob_ref, n_chunks=n_chunks):
            chains = []
            for d in (0, 1):
                start = pl.multiple_of(((n_chunks - 1 - i) if d else i) * cs, cs)
                z_ref = zb_ref if d else zf_ref
                for hh in range(hp):
                    sl = _head_slice(hh)
                    lb = lb_ref[hh]
                    q = _silu(q_ref[pl.ds(start, cs), sl].astype(F32))
                    f = lb + (1.0 - lb) * (1.0 / (1.0 + jnp.exp(-z_ref[pl.ds(start, cs), sl])))
                    chains.append(dict(d=d, c=2 * hh + d, start=start, sl=sl, q=q, k=1.0 - f,
                                       parts=_split3(jnp.log2(f)), v=v_ref[pl.ds(start, cs), sl],
                                       o_ref=ob_ref if d else of_ref))
            for ch in chains:
                tri = tri_ref[ch["d"]]
                hi, mid, lo = ch["parts"]
                ch["bsum"] = (jnp.dot(tri, hi, preferred_element_type=F32)
                              + jnp.dot(tri, mid, preferred_element_type=F32)
                              + jnp.dot(tri, lo, preferred_element_type=F32))
            for ch in chains:
                ch["qb"], ch["kb"] = ch["q"].astype(BF16), ch["k"].astype(BF16)
                ch["att"] = _mm_nt(ch["qb"], ch["kb"]).astype(BF16) * mask_ref[ch["d"], len(levels)]
            for li, h in enumerate(levels):
                for ch in chains:
                    ref = _level_reference(ch["bsum"], h, ch["d"] == 1)
                    w = jnp.exp2(_neg_abs(ch["bsum"] - ref)).astype(BF16)
                    ch["att"] += _mm_nt(ch["qb"] * w, ch["kb"] * w).astype(BF16) * mask_ref[ch["d"], li]
            for ch in chains:
                ch["state"] = st_ref[ch["c"]]
                o = _mm(ch["att"], ch["v"]) + _mm_nt(ch["q"] * jnp.exp2(ch["bsum"]), ch["state"])
                ch["o_ref"][pl.ds(ch["start"], cs), ch["sl"]] = o.astype(ch["o_ref"].dtype)
            for ch in chains:
                bsum = ch["bsum"]
                b_end = bsum[0:1, :] if ch["d"] else bsum[cs - 1:cs, :]
                st_ref[ch["c"]] = (ch["state"] * jnp.exp2(b_end)
                                   + _mm_tn(ch["v"], ch["k"] * jnp.exp2(b_end - bsum)))
            return carry

        lax.fori_loop(0, n_chunks, step, 0)

    _finish_scan(ofc_ref, obc_ref, gc_ref, yc_ref, cs)
    _finish_scan(ofx_ref, obx_ref, gx_ref, yx_ref, cs)


def _hgrn(proj_c, proj_x, z_c, z_x, lb, first):
    b, tc, _ = proj_c.shape
    tx = proj_x.shape[1]
    h, hp = HGRN_HEADS, SCAN_HEADS
    w = hp * HEAD_DIM
    tri, masks = _hgrn_constants(HGRN_CHUNK)

    def col(t, group):
        return pl.BlockSpec((None, t, w), lambda bi, hi: (bi, 0, (first + group * h) // hp + hi),
                            pipeline_mode=pl.Buffered(1))

    def zcol(t, group):
        return pl.BlockSpec((None, t, w), lambda bi, hi: (bi, 0, group * (h // hp) + hi),
                            pipeline_mode=pl.Buffered(1))

    ins = [col(tc, 0), col(tc, 1), col(tc, 2), zcol(tc, 0), zcol(tc, 1),
           col(tx, 0), col(tx, 1), col(tx, 2), zcol(tx, 0), zcol(tx, 1)]
    return pl.pallas_call(
        _hgrn_kernel,
        grid=(b, h // hp),
        in_specs=[pl.BlockSpec((hp, 1, HEAD_DIM), lambda bi, hi: (hi, 0, 0)),
                  pl.BlockSpec(tri.shape, lambda bi, hi: (0, 0, 0)),
                  pl.BlockSpec(masks.shape, lambda bi, hi: (0, 0, 0, 0))] + ins,
        out_specs=[pl.BlockSpec((None, tc, w), lambda bi, hi: (bi, 0, hi)),
                   pl.BlockSpec((None, tx, w), lambda bi, hi: (bi, 0, hi))],
        out_shape=[jax.ShapeDtypeStruct((b, tc, h * HEAD_DIM), BF16),
                   jax.ShapeDtypeStruct((b, tx, h * HEAD_DIM), BF16)],
        scratch_shapes=[pltpu.VMEM((tc, w), BF16), pltpu.VMEM((tc, w), BF16),
                        pltpu.VMEM((tx, w), BF16), pltpu.VMEM((tx, w), BF16),
                        pltpu.VMEM((2 * hp, HEAD_DIM, HEAD_DIM), F32)],
        compiler_params=_cparams("parallel", "parallel"),
        name="hgrn2",
    )(lb.reshape(h, 1, HEAD_DIM), tri, masks, proj_c, proj_c, proj_c, z_c, z_c,
      proj_x, proj_x, proj_x, z_x, z_x)


MLA_QW = 2 * LANE
MLA_SCORE_SCALE = (MLA_NOPE + MLA_ROPE) ** -0.5 * math.log2(math.e)
MLA_DOWN_PAD = MLA_Q_RANK + MLA_KV_RANK + LANE


def _mla_proj_kernel(x_ref, mod_ref, g_ref, wd_ref, gq_ref, gkv_ref, wq_ref, wkv_ref, cos_ref, sin_ref,
                     *out_refs, rope, want_q):
    h = _modnorm(x_ref[...], g_ref[...], mod_ref, 0, 1).astype(BF16)
    down = jnp.dot(h, wd_ref[...], preferred_element_type=F32)
    if rope:
        cos = cos_ref[...]
        sin = sin_ref[...]

    def rot(v):
        return v * cos + pltpu.roll(v, LANE // 2, 1) * sin if rope else v

    if want_q:
        q_ref, kn_ref, kr_ref, v_ref = out_refs
        qn = (_rms(down[:, :MLA_Q_RANK]) * gq_ref[...]).astype(BF16)
        q = jnp.dot(qn, wq_ref[...], preferred_element_type=F32) * MLA_SCORE_SCALE
        for hd in range(MLA_HEADS):
            lo = hd * MLA_QW
            q_ref[:, lo:lo + LANE] = q[:, lo:lo + LANE].astype(q_ref.dtype)
            q_ref[:, lo + LANE:lo + 2 * LANE] = rot(q[:, lo + LANE:lo + 2 * LANE]).astype(q_ref.dtype)
    else:
        kn_ref, kr_ref, v_ref = out_refs
    kvn = (_rms(down[:, MLA_Q_RANK:MLA_Q_RANK + MLA_KV_RANK]) * gkv_ref[...]).astype(BF16)
    kv = jnp.dot(kvn, wkv_ref[...], preferred_element_type=F32)
    nk = MLA_HEADS * MLA_NOPE
    kn_ref[...] = kv[:, :nk].astype(kn_ref.dtype)
    v_ref[...] = kv[:, nk:].astype(v_ref.dtype)
    kr_ref[...] = rot(down[:, MLA_Q_RANK + MLA_KV_RANK:]).astype(kr_ref.dtype)


def _mla_proj(x, mod, g, wd, gq, gkv, wq, wkv, cos, sin, rope, want_q, tm=1024):
    b, t, d = x.shape
    tm = _pick(t, tm)
    const = lambda bi, i: (0, 0)
    tok = lambda bi, i: (bi, i, 0)
    nk, nv = MLA_HEADS * MLA_NOPE, MLA_HEADS * MLA_V
    out_shape = [jax.ShapeDtypeStruct((b, t, nk), BF16), jax.ShapeDtypeStruct((b, t, LANE), BF16),
                 jax.ShapeDtypeStruct((b, t, nv), BF16)]
    out_specs = [pl.BlockSpec((None, tm, nk), tok), pl.BlockSpec((None, tm, LANE), tok),
                 pl.BlockSpec((None, tm, nv), tok)]
    if want_q:
        out_shape = [jax.ShapeDtypeStruct((b, t, MLA_HEADS * MLA_QW), BF16)] + out_shape
        out_specs = [pl.BlockSpec((None, tm, MLA_HEADS * MLA_QW), tok)] + out_specs
    return pl.pallas_call(
        functools.partial(_mla_proj_kernel, rope=rope, want_q=want_q),
        grid=(b, t // tm),
        in_specs=[pl.BlockSpec((None, tm, d), tok), _mod_spec(mod), pl.BlockSpec((1, d), const),
                  pl.BlockSpec(wd.shape, const), pl.BlockSpec((1, MLA_Q_RANK), const),
                  pl.BlockSpec((1, MLA_KV_RANK), const), pl.BlockSpec(wq.shape, const),
                  pl.BlockSpec(wkv.shape, const),
                  pl.BlockSpec((tm, LANE), lambda bi, i: (i, 0)),
                  pl.BlockSpec((tm, LANE), lambda bi, i: (i, 0))],
        out_specs=out_specs,
        out_shape=out_shape,
        compiler_params=_cparams("parallel", "parallel"),
        name="mla_proj",
    )(x, mod, g.reshape(1, d), wd, gq.reshape(1, -1), gkv.reshape(1, -1), wq, wkv, cos, sin)


def _lane_fold(x, op):
    out = x[:, :LANE]
    for j in range(1, x.shape[1] // LANE):
        out = op(out, x[:, j * LANE:(j + 1) * LANE])
    return out


def _attention_kernel(q_ref, knc_ref, krc_ref, vc_ref, knx_ref, krx_ref, vx_ref, o_ref,
                      kc_ref, kx_ref, vc2_ref, vx2_ref, sc0_ref, sc1_ref, sx0_ref, sx1_ref, m0_ref, m1_ref,
                      *, tk):
    step = pl.program_id(2)

    @pl.when(step == 0)
    def _():
        kc_ref[:, :LANE] = knc_ref[...]
        kc_ref[:, LANE:] = krc_ref[...]
        kx_ref[:, :LANE] = knx_ref[...]
        kx_ref[:, LANE:] = krx_ref[...]
        vc2_ref[:, :LANE] = vc_ref[...]
        vx2_ref[:, :LANE] = vx_ref[...]
        for ref in (vc2_ref, vx2_ref):
            lane = lax.broadcasted_iota(jnp.int32, (ref.shape[0], LANE), 1)
            ref[:, LANE:] = jnp.where(lane == 0, 1.0, 0.0).astype(ref.dtype)

    n_blocks = kx_ref.shape[0] // tk
    n_steps = pl.num_programs(2)
    nt = (((1,), (1,)), ((), ()))

    def body(cur, prev):
        if cur is not None:
            sc_cur, sx_cur, m_cur = cur
            q = q_ref[...]
            s = lax.dot_general(q, kc_ref[...], nt, preferred_element_type=F32)
            sc_cur[...] = s
            m_part = _lane_fold(s, jnp.maximum)
        if prev is not None:
            sc_prev, sx_prev, m_prev = prev
            m = jnp.max(m_prev[...], axis=-1, keepdims=True)
            p = jnp.exp2(sc_prev[...] - m)
            acc = jnp.dot(p.astype(BF16), vc2_ref[...], preferred_element_type=F32)
        for i in range(n_blocks):
            if cur is not None:
                s = lax.dot_general(q, kx_ref[i * tk:(i + 1) * tk, :], nt, preferred_element_type=F32)
                sx_cur[i] = s
                m_part = jnp.maximum(m_part, _lane_fold(s, jnp.maximum))
            if prev is not None:
                p = jnp.exp2(sx_prev[i] - m)
                acc = acc + jnp.dot(p.astype(BF16), vx2_ref[i * tk:(i + 1) * tk, :],
                                    preferred_element_type=F32)
        if cur is not None:
            m_cur[...] = m_part
        if prev is not None:
            o_ref[...] = (acc[:, :MLA_V] / acc[:, MLA_V:MLA_V + 1]).astype(o_ref.dtype)

    bufs = ((sc0_ref, sx0_ref, m0_ref), (sc1_ref, sx1_ref, m1_ref))
    first = step == 0
    last = step == n_steps - 1
    inner = jnp.logical_not(first | last)

    @pl.when(first)
    def _():
        body(bufs[0], None)

    @pl.when(inner & (step % 2 == 0))
    def _():
        body(bufs[0], bufs[1])

    @pl.when(inner & (step % 2 == 1))
    def _():
        body(bufs[1], bufs[0])

    for parity in (0, 1):
        @pl.when(last & (step % 2 == parity))
        def _(parity=parity):
            body(None, bufs[1 - parity])


def _attention(q, kn_c, kr_c, v_c, kn_x, kr_x, v_x, tq=1024, tk=256):
    b, t, _ = q.shape
    tc = kn_c.shape[1]
    tq = _pick(t, tq)
    tk = _pick(t, tk)
    nq = t // tq
    head_c = lambda bi, hi, i: (bi, 0, hi)
    shared = lambda bi, hi, i: (bi, 0, 0)
    return pl.pallas_call(
        functools.partial(_attention_kernel, tk=tk),
        grid=(b, MLA_HEADS, nq + 1),
        in_specs=[pl.BlockSpec((None, tq, MLA_QW), lambda bi, hi, i: (bi, jnp.minimum(i, nq - 1), hi)),
                  pl.BlockSpec((None, tc, LANE), head_c), pl.BlockSpec((None, tc, LANE), shared),
                  pl.BlockSpec((None, tc, LANE), head_c),
                  pl.BlockSpec((None, t, LANE), head_c), pl.BlockSpec((None, t, LANE), shared),
                  pl.BlockSpec((None, t, LANE), head_c)],
        out_specs=pl.BlockSpec((None, tq, MLA_V), lambda bi, hi, i: (bi, jnp.maximum(i - 1, 0), hi)),
        out_shape=jax.ShapeDtypeStruct((b, t, MLA_HEADS * MLA_V), BF16),
        scratch_shapes=[pltpu.VMEM((tc, MLA_QW), BF16), pltpu.VMEM((t, MLA_QW), BF16),
                        pltpu.VMEM((tc, 2 * MLA_V), BF16), pltpu.VMEM((t, 2 * MLA_V), BF16),
                        pltpu.VMEM((tq, tc), F32), pltpu.VMEM((tq, tc), F32),
                        pltpu.VMEM((t // tk, tq, tk), F32), pltpu.VMEM((t // tk, tq, tk), F32),
                        pltpu.VMEM((tq, LANE), F32), pltpu.VMEM((tq, LANE), F32)],
        compiler_params=_cparams("parallel", "parallel", "arbitrary"),
        name="mla_attention",
    )(q, kn_c, kr_c, v_c, kn_x, kr_x, v_x)


def _router_kernel(x_ref, mod_ref, g_ref, wr_ref, h_ref, w_ref, e_ref):
    h = _modnorm(x_ref[...], g_ref[...], mod_ref, 3, 4)
    h_ref[...] = _pack_halves(h)
    h_hi = h.astype(BF16)
    h_lo = (h - h_hi.astype(F32)).astype(BF16)
    w_hi = wr_ref[...].astype(BF16)
    w_lo = (wr_ref[...] - w_hi.astype(F32)).astype(BF16)
    logits = (jnp.dot(h_hi, w_hi, preferred_element_type=F32) + jnp.dot(h_lo, w_hi, preferred_element_type=F32)
              + jnp.dot(h_hi, w_lo, preferred_element_type=F32))
    lane = lax.broadcasted_iota(jnp.int32, logits.shape, 1)
    neg = jnp.float32(-jnp.inf)
    lg = jnp.where(lane < N_EXPERTS, logits, neg)
    m1 = jnp.max(lg, axis=-1, keepdims=True)
    i1 = jnp.min(jnp.where(lg == m1, lane, LANE), axis=-1, keepdims=True)
    lg2 = jnp.where(lane == i1, neg, lg)
    m2 = jnp.max(lg2, axis=-1, keepdims=True)
    i2 = jnp.min(jnp.where(lg2 == m2, lane, LANE), axis=-1, keepdims=True)
    e2 = jnp.exp(m2 - m1)
    w1 = 1.0 / (1.0 + e2)
    w2 = e2 / (1.0 + e2)
    w_ref[...] = jnp.where(lane == 0, w1, jnp.where(lane == 1, w2, 0.0))
    e_ref[...] = jnp.where(lane == 0, i1, jnp.where(lane == 1, i2, 0))


def _router(x, mod, g, router_pad, tm=1024):
    b, t, d = x.shape
    tm = _pick(t, tm)
    tok = lambda bi, i: (bi, i, 0)
    const = lambda bi, i: (0, 0)
    return pl.pallas_call(
        _router_kernel,
        grid=(b, t // tm),
        in_specs=[pl.BlockSpec((None, tm, d), tok), _mod_spec(mod), pl.BlockSpec((1, d), const),
                  pl.BlockSpec((d, LANE), const)],
        out_specs=[pl.BlockSpec((None, tm, d // 2), tok), pl.BlockSpec((None, tm, LANE), tok),
                   pl.BlockSpec((None, tm, LANE), tok)],
        out_shape=[jax.ShapeDtypeStruct((b, t, d // 2), jnp.int32), jax.ShapeDtypeStruct((b, t, LANE), F32),
                   jax.ShapeDtypeStruct((b, t, LANE), jnp.int32)],
        compiler_params=_cparams("parallel", "parallel"),
        name="moe_router",
    )(x, mod, g.reshape(1, d), router_pad)


SC_GATHER_ROWS = 64


def _gather_rows(table, idx):
    info = plsc.get_sparse_core_info()
    n_workers = info.num_cores * info.num_subcores
    n, width = idx.shape[0], table.shape[1]
    chunk = SC_GATHER_ROWS
    assert n % (n_workers * chunk) == 0
    n_chunks = n // (n_workers * chunk)
    mesh = plsc.VectorSubcoreMesh(core_axis_name="c", subcore_axis_name="s")

    def body(table_hbm, idx_hbm, out_hbm, idx_v, rows_v, sem):
        wid = lax.axis_index("s") * info.num_cores + lax.axis_index("c")
        pltpu.sync_copy(idx_hbm.at[wid], idx_v)

        @pl.loop(0, n_chunks)
        def _(j):
            pltpu.async_copy(table_hbm.at[idx_v.at[j]], rows_v, sem).wait()
            pltpu.sync_copy(rows_v, out_hbm.at[pl.ds((wid * n_chunks + j) * chunk, chunk)])

    return pl.kernel(
        body,
        out_type=jax.ShapeDtypeStruct((n, width), table.dtype),
        mesh=mesh,
        scratch_types=[pltpu.VMEM((n_chunks, chunk), jnp.int32), pltpu.VMEM((chunk, width), table.dtype),
                       pltpu.SemaphoreType.DMA],
        name="sc_gather_rows",
    )(table, idx.reshape(n_workers, n_chunks, chunk))


def _scatter_rows_twice(table, idx0, idx1, n_out):
    info = plsc.get_sparse_core_info()
    n_workers = info.num_cores * info.num_subcores
    n, width = table.shape
    chunk = SC_GATHER_ROWS
    assert n % (n_workers * chunk) == 0
    n_chunks = n // (n_workers * chunk)
    mesh = plsc.VectorSubcoreMesh(core_axis_name="c", subcore_axis_name="s")

    def body(table_hbm, i0_hbm, i1_hbm, out_hbm, i0_v, i1_v, rows_v):
        wid = lax.axis_index("s") * info.num_cores + lax.axis_index("c")
        pltpu.sync_copy(i0_hbm.at[wid], i0_v)
        pltpu.sync_copy(i1_hbm.at[wid], i1_v)

        @pl.loop(0, n_chunks)
        def _(j):
            pltpu.sync_copy(table_hbm.at[pl.ds((wid * n_chunks + j) * chunk, chunk)], rows_v)
            pltpu.sync_copy(rows_v, out_hbm.at[i0_v.at[j]])
            pltpu.sync_copy(rows_v, out_hbm.at[i1_v.at[j]])

    shape3 = (n_workers, n_chunks, chunk)
    return pl.kernel(
        body,
        out_type=jax.ShapeDtypeStruct((n_out, width), table.dtype),
        mesh=mesh,
        scratch_types=[pltpu.VMEM((n_chunks, chunk), jnp.int32), pltpu.VMEM((n_chunks, chunk), jnp.int32),
                       pltpu.VMEM((chunk, width), table.dtype)],
        name="sc_scatter_rows",
    )(table, idx0.reshape(shape3), idx1.reshape(shape3))


def _pack_halves(x):
    k = x.shape[1] // 2
    bits = lambda v: lax.bitcast_convert_type(v.astype(BF16).astype(F32), jnp.uint32)
    word = lax.shift_right_logical(bits(x[:, :k]), jnp.uint32(16)) | bits(x[:, k:])
    return lax.bitcast_convert_type(word, jnp.int32)


def _unpack_halves(w):
    u = lax.bitcast_convert_type(w, jnp.uint32)
    lo = lax.bitcast_convert_type(lax.shift_left(u, jnp.uint32(16)), F32)
    hi = lax.bitcast_convert_type(u & jnp.uint32(0xFFFF0000), F32)
    return jnp.concatenate([lo, hi], axis=-1)


def _expert_ffn_kernel(te_ref, tv_ref, x_ref, wa_ref, wb_ref, wo_ref, o_ref, acc_ref):
    j = pl.program_id(0)
    f = pl.program_id(1)
    last = pl.num_programs(1) - 1
    valid = tv_ref[j] != 0

    @pl.when(valid)
    def _():
        @pl.when(f == 0)
        def _():
            acc_ref[...] = jnp.zeros_like(acc_ref)

        h = _unpack_halves(x_ref[...]).astype(BF16)
        a = jnp.dot(h, wa_ref[...], preferred_element_type=F32)
        bb = jnp.dot(h, wb_ref[...], preferred_element_type=F32)
        acc_ref[...] += jnp.dot((_silu(a) * bb).astype(BF16), wo_ref[...], preferred_element_type=F32)

        @pl.when(f == last)
        def _():
            o_ref[...] = _pack_halves(acc_ref[...])

    @pl.when(jnp.logical_not(valid) & (f == last))
    def _():
        o_ref[...] = jnp.zeros_like(o_ref)


def _expert_ffn(xs, tile_expert, tile_valid, w_in, w_out, tm, tf=2816):
    p = xs.shape[0]
    n_exp, fdim, d = w_out.shape
    tf = _pick(fdim, tf)
    nf = fdim // tf

    def f_eff(j, f, tv):
        return jnp.where(tv[j] != 0, f, nf - 1)

    grid_spec = pltpu.PrefetchScalarGridSpec(
        num_scalar_prefetch=2,
        grid=(p // tm, nf),
        in_specs=[pl.BlockSpec((tm, d // 2), lambda j, f, te, tv: (j, 0)),
                  pl.BlockSpec((None, d, tf), lambda j, f, te, tv: (te[j], 0, f_eff(j, f, tv)),
                               pipeline_mode=pl.Buffered(1)),
                  pl.BlockSpec((None, d, tf), lambda j, f, te, tv: (te[j], 0, f_eff(j, f, tv) + nf),
                               pipeline_mode=pl.Buffered(1)),
                  pl.BlockSpec((None, tf, d), lambda j, f, te, tv: (te[j], f_eff(j, f, tv), 0),
                               pipeline_mode=pl.Buffered(1))],
        out_specs=pl.BlockSpec((tm, d // 2), lambda j, f, te, tv: (j, 0)),
        scratch_shapes=[pltpu.VMEM((tm, d), F32)])
    return pl.pallas_call(
        _expert_ffn_kernel,
        grid_spec=grid_spec,
        out_shape=jax.ShapeDtypeStruct((p, d // 2), jnp.int32),
        compiler_params=_cparams("parallel", "arbitrary"),
        name="moe_expert_ffn",
    )(tile_expert, tile_valid, xs, w_in, w_in, w_out)


def _moe_combine_kernel(x_ref, y0_ref, y1_ref, w_ref, mod_ref, g_ref, o_ref):
    y = w_ref[:, 0:1] * _unpack_halves(y0_ref[...]) + w_ref[:, 1:2] * _unpack_halves(y1_ref[...])
    o_ref[...] = x_ref[...] + mod_ref[5:6, :] * (_rms(y) * g_ref[...])


def _moe_combine(x, y2, w, mod, g, tm=1024):
    b, t, d = x.shape
    tm = _pick(t, tm)
    tok = lambda bi, i: (bi, i, 0)
    return pl.pallas_call(
        _moe_combine_kernel,
        grid=(b, t // tm),
        in_specs=[pl.BlockSpec((None, tm, d), tok),
                  pl.BlockSpec((None, None, tm, d // 2), lambda bi, i: (0, bi, i, 0)),
                  pl.BlockSpec((None, None, tm, d // 2), lambda bi, i: (1, bi, i, 0)),
                  pl.BlockSpec((None, tm, LANE), tok), _mod_spec(mod),
                  pl.BlockSpec((1, d), lambda bi, i: (0, 0))],
        out_specs=pl.BlockSpec((None, tm, d), tok),
        out_shape=jax.ShapeDtypeStruct((b, t, d), F32),
        compiler_params=_cparams("parallel", "parallel"),
        name="moe_combine",
    )(x, y2, y2, w, mod, g.reshape(1, d))


def _moe(x, mod, g_in, g_out, router_pad, w_in, w_out, tm=512):
    b, t, d = x.shape
    n_exp = w_out.shape[0]
    n = b * t
    h, w, e = _router(x, mod, g_in, router_pad)
    e_flat = jnp.concatenate([e[..., 0].reshape(n), e[..., 1].reshape(n)])
    onehot = (e_flat[:, None] == jnp.arange(n_exp, dtype=jnp.int32)).astype(jnp.int32)
    csum = jnp.cumsum(onehot, axis=0)
    rank = jnp.sum((csum - onehot) * onehot, axis=-1)
    counts = csum[-1]
    ends = jnp.cumsum((counts + tm - 1) // tm * tm)
    dest = (ends - (counts + tm - 1) // tm * tm)[e_flat] + rank
    p = 2 * n + n_exp * tm
    tile_start = jnp.arange(p // tm, dtype=jnp.int32) * tm
    tile_expert = jnp.minimum(jnp.searchsorted(ends, tile_start, side="right"), n_exp - 1).astype(jnp.int32)
    tile_valid = (tile_start < ends[-1]).astype(jnp.int32)

    xs = _scatter_rows_twice(h.reshape(n, d // 2), dest[:n], dest[n:], p)
    ys = _expert_ffn(xs, tile_expert, tile_valid, w_in, w_out, tm)
    y2 = _gather_rows(ys, dest)
    return _moe_combine(x, y2.reshape(2, b, t, d // 2), w, mod, g_out)


def _axial_angles(n_tokens, dim):
    t = jnp.arange(n_tokens)
    rows = (t // GRID_W).astype(F32)
    cols = (t % GRID_W).astype(F32)
    n_freq = dim // 4
    inv = ROPE_BASE ** (-jnp.arange(n_freq, dtype=F32) / n_freq)
    return jnp.concatenate([rows[:, None] * inv, cols[:, None] * inv], axis=-1)


def _rope_tables(n_tokens, dim):
    ang = _axial_angles(n_tokens, dim)
    cos, sin = jnp.cos(ang), jnp.sin(ang)
    if dim == LANE:
        return jnp.concatenate([cos, cos], -1), jnp.concatenate([-sin, sin], -1)
    z = jnp.zeros_like(cos)
    return jnp.concatenate([cos, z, cos, z], -1), jnp.concatenate([-sin, z, sin, z], -1)


def _spread_rope_cols(w):
    half = MLA_ROPE // 2
    z = jnp.zeros(w.shape[:-1] + (half,), w.dtype)
    return jnp.concatenate([w[..., :half], z, w[..., half:], z], axis=-1)


def kernel(x, c, ctx, c_ctx, mod_w, mod_b, norm_g, mix_in_w, ret_decay_logit, hgrn_lb_logit, mix_out_w,
           ffn_in_w, ffn_out_w, mla_down_w, mla_q_norm_g, mla_kv_norm_g, mla_uq_w, mla_ukv_w, mla_out_w,
           router_w, moe_in_w, moe_out_w):
    batch, seq, d = x.shape
    assert mod_w.shape[0] == 2, "two layers: retention/HGRN2 then MLA/MoE"

    rows = -(-(batch + 1) // 8) * 8
    cond = jnp.zeros((rows, d), F32).at[:batch].set(c).at[batch].set(c_ctx)
    mod = _modulation(cond, mod_w, mod_b)
    mod_x = [mod[l, :batch].reshape(batch, 6, d) for l in range(2)]
    mod_c = [mod[l, batch:batch + 1].reshape(1, 6, d) for l in range(2)]

    g = norm_g[0]
    ret_w = RET_HEADS * HEAD_DIM
    z_lo, z_hi = 5 * ret_w, 7 * ret_w
    w_in = mix_in_w[0].astype(BF16)
    w_main = jnp.concatenate([w_in[:, :z_lo], w_in[:, z_hi:]], axis=1)
    w_z = w_in[:, z_lo:z_hi]
    proj_x = _norm_matmul(x, mod_x[0], g[0], w_main, 0, 1, BF16)
    proj_c = _norm_matmul(ctx, mod_c[0], g[0], w_main, 0, 1, BF16)
    z_x = _norm_matmul(x, mod_x[0], g[0], w_z, 0, 1, F32)
    z_c = _norm_matmul(ctx, mod_c[0], g[0], w_z, 0, 1, F32)
    cos2, sin2 = _rope_tables(seq, HEAD_DIM)
    log_gamma = jax.nn.log_sigmoid(ret_decay_logit[0].astype(F32))
    lb = jnp.cumsum(jax.nn.softmax(hgrn_lb_logit.astype(F32), axis=0), axis=0)[0]
    ret_c, ret_x = _retention(proj_c, proj_x, log_gamma, cos2, sin2)
    hg_c, hg_x = _hgrn(proj_c, proj_x, z_c, z_x, lb, first=4 * RET_HEADS)
    w_out = mix_out_w[0].astype(BF16)
    w_outs = [w_out[:ret_w], w_out[ret_w:]]
    x = _proj_residual([ret_x, hg_x], w_outs, x, mod_x[0], g[1], 2)
    ctx = _proj_residual([ret_c, hg_c], w_outs, ctx, mod_c[0], g[1], 2)
    f_in, f_out = ffn_in_w[0].astype(BF16), ffn_out_w[0].astype(BF16)
    x = _ffn(x, mod_x[0], g[2], g[3], f_in, f_out)
    ctx = _ffn(ctx, mod_c[0], g[2], g[3], f_in, f_out)

    g = norm_g[1]
    dq = MLA_NOPE + MLA_ROPE
    wd = mla_down_w[0]
    wd = jnp.concatenate([wd[:, :MLA_Q_RANK + MLA_KV_RANK],
                          _spread_rope_cols(wd[:, MLA_Q_RANK + MLA_KV_RANK:])], axis=-1).astype(BF16)
    wq = mla_uq_w[0].reshape(MLA_Q_RANK, MLA_HEADS, dq)
    wq = jnp.concatenate([wq[..., :MLA_NOPE], _spread_rope_cols(wq[..., MLA_NOPE:])], axis=-1)
    wq = wq.reshape(MLA_Q_RANK, MLA_HEADS * MLA_QW).astype(BF16)
    wkv = mla_ukv_w[0].reshape(MLA_KV_RANK, MLA_HEADS, MLA_NOPE + MLA_V)
    wkv = jnp.concatenate([wkv[..., :MLA_NOPE].reshape(MLA_KV_RANK, -1),
                           wkv[..., MLA_NOPE:].reshape(MLA_KV_RANK, -1)], axis=-1).astype(BF16)
    cos_m, sin_m = _rope_tables(seq, MLA_ROPE)
    q, kn_x, kr_x, v_x = _mla_proj(x, mod_x[1], g[0], wd, mla_q_norm_g[0], mla_kv_norm_g[0], wq, wkv,
                                   cos_m, sin_m, rope=True, want_q=True)
    tc = ctx.shape[1]
    kn_c, kr_c, v_c = _mla_proj(ctx, mod_c[1], g[0], wd, mla_q_norm_g[0], mla_kv_norm_g[0], wq, wkv,
                                cos_m[:tc], sin_m[:tc], rope=False, want_q=False)
    att = _attention(q, kn_c, kr_c, v_c, kn_x, kr_x, v_x)
    x = _proj_residual([att], [mla_out_w[0].astype(BF16)], x, mod_x[1], g[1], 2)
    router_pad = jnp.zeros((d, LANE), F32).at[:, :N_EXPERTS].set(router_w[0])
    x = _moe(x, mod_x[1], g[2], g[3], router_pad, moe_in_w[0].astype(BF16), moe_out_w[0].astype(BF16))
    return x
```

```python
import functools
import math

import jax
import jax.numpy as jnp
import numpy as np
from jax import lax
from jax.experimental import pallas as pl
from jax.experimental.pallas import tpu as pltpu
from jax.experimental.pallas import tpu_sc as plsc

F32 = jnp.float32
BF16 = jnp.bfloat16

NORM_EPS = 1e-6
ROPE_BASE = 10000.0
GRID_W = 64
LANE = 128

RET_HEADS = 4
HGRN_HEADS = 4
HEAD_DIM = 128
RET_CHUNK = 128
HGRN_CHUNK = 128

MLA_HEADS = 8
MLA_NOPE = 128
MLA_ROPE = 64
MLA_V = 128
MLA_Q_RANK = 384
MLA_KV_RANK = 256
N_EXPERTS = 8

VMEM_LIMIT = 56 * 1024 * 1024


def _cparams(*sem):
    return pltpu.CompilerParams(dimension_semantics=sem, vmem_limit_bytes=VMEM_LIMIT)


def _mm(a, b):
    return jnp.dot(a.astype(BF16), b.astype(BF16), preferred_element_type=F32)


def _mm_nt(a, b):
    return lax.dot_general(a.astype(BF16), b.astype(BF16), (((1,), (1,)), ((), ())),
                           preferred_element_type=F32)


def _mm_tn(a, b):
    return lax.dot_general(a.astype(BF16), b.astype(BF16), (((0,), (0,)), ((), ())),
                           preferred_element_type=F32)


def _rms(x):
    return x * lax.rsqrt(jnp.mean(x * x, axis=-1, keepdims=True) + NORM_EPS)


def _silu(x):
    return x * (1.0 / (1.0 + jnp.exp(-x)))


def _modnorm(x, g, mod_ref, shift_row, scale_row):
    shift = mod_ref[shift_row:shift_row + 1, :]
    scale = mod_ref[scale_row:scale_row + 1, :]
    return _rms(x) * g * (1.0 + scale) + shift


def _mod_spec(mod):
    d = mod.shape[-1]
    if mod.shape[0] == 1:
        return pl.BlockSpec((None, 6, d), lambda b, *_: (0, 0, 0))
    return pl.BlockSpec((None, 6, d), lambda b, *_: (b, 0, 0))


def _pick(n, pref):
    t = min(pref, n)
    while n % t:
        t -= LANE
    return t


def _mod_kernel(c_ref, w_ref, b_ref, o_ref):
    a = _silu(c_ref[...])
    o_ref[...] = jnp.dot(a, w_ref[...], preferred_element_type=F32,
                         precision=lax.Precision.HIGHEST) + b_ref[...]


def _modulation(cond, mod_w, mod_b):
    n_layers, d, n = mod_w.shape
    r = cond.shape[0]
    tn = _pick(n, 1536)
    return pl.pallas_call(
        _mod_kernel,
        grid=(n_layers, n // tn),
        in_specs=[pl.BlockSpec((r, d), lambda l, j: (0, 0)),
                  pl.BlockSpec((None, d, tn), lambda l, j: (l, 0, j)),
                  pl.BlockSpec((None, 1, tn), lambda l, j: (l, 0, j))],
        out_specs=pl.BlockSpec((None, r, tn), lambda l, j: (l, 0, j)),
        out_shape=jax.ShapeDtypeStruct((n_layers, r, n), F32),
        compiler_params=_cparams("parallel", "parallel"),
        name="modulation",
    )(cond, mod_w, mod_b.reshape(n_layers, 1, n))


def _norm_matmul_kernel(x_ref, mod_ref, g_ref, w_ref, o_ref, h_ref, *, shift_row, scale_row):
    @pl.when(pl.program_id(2) == 0)
    def _():
        h_ref[...] = _modnorm(x_ref[...], g_ref[...], mod_ref, shift_row, scale_row).astype(BF16)

    o_ref[...] = jnp.dot(h_ref[...], w_ref[...], preferred_element_type=F32).astype(o_ref.dtype)


def _norm_matmul(x, mod, g, w, shift_row, scale_row, out_dtype, tm=1024, tn=1792):
    b, t, d = x.shape
    n = w.shape[1]
    tm = _pick(t, tm)
    tn = _pick(n, tn)
    return pl.pallas_call(
        functools.partial(_norm_matmul_kernel, shift_row=shift_row, scale_row=scale_row),
        grid=(b, t // tm, n // tn),
        in_specs=[pl.BlockSpec((None, tm, d), lambda bi, i, j: (bi, i, 0)),
                  _mod_spec(mod),
                  pl.BlockSpec((1, d), lambda bi, i, j: (0, 0)),
                  pl.BlockSpec((d, tn), lambda bi, i, j: (0, j))],
        out_specs=pl.BlockSpec((None, tm, tn), lambda bi, i, j: (bi, i, j)),
        out_shape=jax.ShapeDtypeStruct((b, t, n), out_dtype),
        scratch_shapes=[pltpu.VMEM((tm, d), BF16)],
        compiler_params=_cparams("parallel", "parallel", "arbitrary"),
        name="norm_matmul",
    )(x, mod, g.reshape(1, d), w)


def _mixer_ffn_kernel(ret_ref, hg_ref, w1_ref, w2_ref, x_ref, mod_ref, g_ref, wa_ref, wb_ref, wo_ref, o_ref):
    y = (jnp.dot(ret_ref[...], w1_ref[...], preferred_element_type=F32)
         + jnp.dot(hg_ref[...], w2_ref[...], preferred_element_type=F32))
    x1 = x_ref[...] + mod_ref[2:3, :] * (_rms(y) * g_ref[1:2, :])
    h = _modnorm(x1, g_ref[2:3, :], mod_ref, 3, 4).astype(BF16)
    a = jnp.dot(h, wa_ref[...], preferred_element_type=F32)
    bb = jnp.dot(h, wb_ref[...], preferred_element_type=F32)
    z = jnp.dot((_silu(a) * bb).astype(BF16), wo_ref[...], preferred_element_type=F32)
    o_ref[...] = x1 + mod_ref[5:6, :] * (_rms(z) * g_ref[3:4, :])


def _mixer_ffn(ret, hg, w1, w2, x, mod, g, w_in, w_out, tm=512):
    b, t, d = x.shape
    fdim = w_out.shape[0]
    tm = _pick(t, tm)
    tok = lambda bi, i: (bi, i, 0)
    resident = lambda shape, idx: pl.BlockSpec(shape, lambda bi, i: idx, pipeline_mode=pl.Buffered(1))
    return pl.pallas_call(
        _mixer_ffn_kernel,
        grid=(b, t // tm),
        in_specs=[pl.BlockSpec((None, tm, ret.shape[-1]), tok), pl.BlockSpec((None, tm, hg.shape[-1]), tok),
                  resident(w1.shape, (0, 0)), resident(w2.shape, (0, 0)),
                  pl.BlockSpec((None, tm, d), tok), _mod_spec(mod), pl.BlockSpec(g.shape, lambda bi, i: (0, 0)),
                  resident((d, fdim), (0, 0)), resident((d, fdim), (0, 1)), resident((fdim, d), (0, 0))],
        out_specs=pl.BlockSpec((None, tm, d), tok),
        out_shape=jax.ShapeDtypeStruct((b, t, d), F32),
        compiler_params=_cparams("parallel", "parallel"),
        name="mixer_ffn",
    )(ret, hg, w1, w2, x, mod, g, w_in, w_in, w_out)


SCAN_HEADS = 4


def _head_slice(hh):
    return slice(hh * HEAD_DIM, (hh + 1) * HEAD_DIM)


def _finish_scan(of_ref, ob_ref, g_ref, y_ref, cs):
    def step(i, carry):
        start = pl.multiple_of(i * cs, cs)
        for hh in range(y_ref.shape[1] // HEAD_DIM):
            sl = _head_slice(hh)
            o = of_ref[pl.ds(start, cs), sl].astype(F32) + ob_ref[pl.ds(start, cs), sl].astype(F32)
            gate = g_ref[pl.ds(start, cs), sl].astype(F32)
            y_ref[pl.ds(start, cs), sl] = (_rms(o) * _silu(gate)).astype(y_ref.dtype)
        return carry

    lax.fori_loop(0, y_ref.shape[0] // cs, step, 0)


def _retention_kernel(lg_ref, qc_ref, kc_ref, vc_ref, gc_ref, qx_ref, kx_ref, vx_ref, gx_ref,
                      cos_ref, sin_ref, yc_ref, yx_ref,
                      ofc_ref, obc_ref, ofx_ref, obx_ref, s_ref, intra_ref, qd_ref, kd_ref):
    cs = RET_CHUNK
    hp = SCAN_HEADS
    head0 = pl.program_id(1) * hp
    row = lax.broadcasted_iota(jnp.int32, (cs, cs), 0).astype(F32)
    col = lax.broadcasted_iota(jnp.int32, (cs, cs), 1).astype(F32)
    pos = lax.broadcasted_iota(jnp.int32, (cs, HEAD_DIM), 0).astype(F32)
    k_scale = HEAD_DIM ** -0.5

    for hh in range(hp):
        for d in (0, 1):
            c = 2 * hh + d
            lg = lg_ref[d, head0 + hh]
            rel = (col - row) if d else (row - col)
            intra_ref[c] = jnp.where(rel >= 0, jnp.exp(lg * jnp.maximum(rel, 0.0)), 0.0)
            p = (cs - 1.0 - pos) if d else pos
            qd_ref[c] = jnp.exp(lg * (p + 1.0))
            kd_ref[c] = jnp.exp(lg * (cs - 1.0 - p))
    s_ref[...] = jnp.zeros_like(s_ref)

    for (q_ref, k_ref, v_ref, of_ref, ob_ref, rope) in (
            (qc_ref, kc_ref, vc_ref, ofc_ref, obc_ref, False),
            (qx_ref, kx_ref, vx_ref, ofx_ref, obx_ref, True)):
        n_chunks = q_ref.shape[0] // cs

        def step(i, carry, q_ref=q_ref, k_ref=k_ref, v_ref=v_ref, of_ref=of_ref, ob_ref=ob_ref,
                 rope=rope, n_chunks=n_chunks):
            chains = []
            for d in (0, 1):
                start = pl.multiple_of(((n_chunks - 1 - i) if d else i) * cs, cs)
                if rope:
                    cos = cos_ref[pl.ds(start, cs), :]
                    sin = sin_ref[pl.ds(start, cs), :]
                for hh in range(hp):
                    sl = _head_slice(hh)
                    q = q_ref[pl.ds(start, cs), sl].astype(F32)
                    k = k_ref[pl.ds(start, cs), sl].astype(F32) * k_scale
                    if rope:
                        q = q * cos + pltpu.roll(q, HEAD_DIM // 2, 1) * sin
                        k = k * cos + pltpu.roll(k, HEAD_DIM // 2, 1) * sin
                    chains.append(dict(d=d, hh=hh, c=2 * hh + d, start=start, sl=sl, q=q, k=k,
                                       v=v_ref[pl.ds(start, cs), sl], o_ref=ob_ref if d else of_ref))
            for ch in chains:
                ch["att"] = _mm_nt(ch["q"], ch["k"]) * intra_ref[ch["c"]]
            for ch in chains:
                c = ch["c"]
                ch["state"] = s_ref[c]
                ch["o_ref"][pl.ds(ch["start"], cs), ch["sl"]] = (
                    _mm(ch["att"], ch["v"]) + _mm(ch["q"] * qd_ref[c], ch["state"]))
            for ch in chains:
                c = ch["c"]
                c_dec = jnp.exp(jnp.full((1, HEAD_DIM), lg_ref[ch["d"], head0 + ch["hh"]] * cs, F32))
                s_ref[c] = ch["state"] * c_dec + _mm_tn(ch["k"] * kd_ref[c], ch["v"])
            return carry

        lax.fori_loop(0, n_chunks, step, 0)

    _finish_scan(ofc_ref, obc_ref, gc_ref, yc_ref, cs)
    _finish_scan(ofx_ref, obx_ref, gx_ref, yx_ref, cs)


def _retention(proj_c, proj_x, log_gamma, cos2, sin2):
    b, tc, _ = proj_c.shape
    tx = proj_x.shape[1]
    h, hp = RET_HEADS, SCAN_HEADS
    w = hp * HEAD_DIM

    def col(t, group):
        return pl.BlockSpec((None, t, w), lambda bi, hi, lg: (bi, 0, group * (h // hp) + hi),
                            pipeline_mode=pl.Buffered(1))

    n_chain = 2 * hp
    grid_spec = pltpu.PrefetchScalarGridSpec(
        num_scalar_prefetch=1,
        grid=(b, h // hp),
        in_specs=[col(tc, 0), col(tc, 1), col(tc, 2), col(tc, 3),
                  col(tx, 0), col(tx, 1), col(tx, 2), col(tx, 3),
                  pl.BlockSpec((tx, HEAD_DIM), lambda bi, hi, lg: (0, 0), pipeline_mode=pl.Buffered(1)),
                  pl.BlockSpec((tx, HEAD_DIM), lambda bi, hi, lg: (0, 0), pipeline_mode=pl.Buffered(1))],
        out_specs=[pl.BlockSpec((None, tc, w), lambda bi, hi, lg: (bi, 0, hi)),
                   pl.BlockSpec((None, tx, w), lambda bi, hi, lg: (bi, 0, hi))],
        scratch_shapes=[pltpu.VMEM((tc, w), F32), pltpu.VMEM((tc, w), F32),
                        pltpu.VMEM((tx, w), F32), pltpu.VMEM((tx, w), F32),
                        pltpu.VMEM((n_chain, HEAD_DIM, HEAD_DIM), F32),
                        pltpu.VMEM((n_chain, RET_CHUNK, RET_CHUNK), F32),
                        pltpu.VMEM((n_chain, RET_CHUNK, HEAD_DIM), F32),
                        pltpu.VMEM((n_chain, RET_CHUNK, HEAD_DIM), F32)])
    return pl.pallas_call(
        _retention_kernel,
        grid_spec=grid_spec,
        out_shape=[jax.ShapeDtypeStruct((b, tc, h * HEAD_DIM), BF16),
                   jax.ShapeDtypeStruct((b, tx, h * HEAD_DIM), BF16)],
        compiler_params=_cparams("parallel", "parallel"),
        name="retention",
    )(log_gamma, proj_c, proj_c, proj_c, proj_c, proj_x, proj_x, proj_x, proj_x, cos2, sin2)


def _hgrn_levels(cs):
    return [cs >> (i + 1) for i in range(int(math.log2(cs)))]


def _hgrn_constants(cs):
    t = np.arange(cs)[:, None]
    s = np.arange(cs)[None, :]
    tri = np.stack([(s <= t), (s >= t)]).astype(np.float32)
    masks = []
    for reverse in (False, True):
        lv = []
        for h in _hgrn_levels(cs):
            same = (t // (2 * h)) == (s // (2 * h))
            t_hi = (t % (2 * h)) >= h
            s_hi = (s % (2 * h)) >= h
            lv.append(same & (~t_hi & s_hi if reverse else t_hi & ~s_hi))
        lv.append(t == s)
        masks.append(np.stack(lv))
    return jnp.asarray(tri, BF16), jnp.asarray(np.stack(masks).astype(np.float32), BF16)


def _split3(x):
    hi = x.astype(BF16)
    r = x - hi.astype(F32)
    mid = r.astype(BF16)
    lo = (r - mid.astype(F32)).astype(BF16)
    return hi, mid, lo


def _neg_abs(x):
    bits = lax.bitcast_convert_type(x, jnp.uint32) | jnp.uint32(0x80000000)
    return lax.bitcast_convert_type(bits, F32)


def _level_reference(b, h, reverse):
    cs = b.shape[0]
    ref_local = h if reverse else h - 1
    if 2 * h >= 8:
        b3 = b.reshape(cs // (2 * h), 2 * h, b.shape[1])
        r = b3[:, ref_local:ref_local + 1, :]
        return jnp.broadcast_to(r, b3.shape).reshape(b.shape)
    local = lax.broadcasted_iota(jnp.int32, b.shape, 0) % (2 * h)
    out = b
    for m in range(2 * h):
        if m != ref_local:
            out = jnp.where(local == m, pltpu.roll(b, (m - ref_local) % cs, 0), out)
    return out


def _hgrn_kernel(lb_ref, tri_ref, mask_ref,
                 qc_ref, vc_ref, gc_ref, zfc_ref, zbc_ref,
                 qx_ref, vx_ref, gx_ref, zfx_ref, zbx_ref,
                 yc_ref, yx_ref, ofc_ref, obc_ref, ofx_ref, obx_ref, st_ref):
    cs = HGRN_CHUNK
    hp = SCAN_HEADS
    levels = _hgrn_levels(cs)
    st_ref[...] = jnp.zeros_like(st_ref)

    for (q_ref, v_ref, zf_ref, zb_ref, of_ref, ob_ref) in (
            (qc_ref, vc_ref, zfc_ref, zbc_ref, ofc_ref, obc_ref),
            (qx_ref, vx_ref, zfx_ref, zbx_ref, ofx_ref, obx_ref)):
        n_chunks = q_ref.shape[0] // cs

        def step(i, carry, q_ref=q_ref, v_ref=v_ref, zf_ref=zf_ref, zb_ref=zb_ref, of_ref=of_ref,
                 ob_ref=ob_ref, n_chunks=n_chunks):
            chains = []
            for d in (0, 1):
                start = pl.multiple_of(((n_chunks - 1 - i) if d else i) * cs, cs)
                z_ref = zb_ref if d else zf_ref
                for hh in range(hp):
                    sl = _head_slice(hh)
                    lb = lb_ref[hh]
                    q = _silu(q_ref[pl.ds(start, cs), sl].astype(F32))
                    f = lb + (1.0 - lb) * (1.0 / (1.0 + jnp.exp(-z_ref[pl.ds(start, cs), sl])))
                    chains.append(dict(d=d, c=2 * hh + d, start=start, sl=sl, q=q, k=1.0 - f,
                                       parts=_split3(jnp.log2(f)), v=v_ref[pl.ds(start, cs), sl],
                                       o_ref=ob_ref if d else of_ref))
            for ch in chains:
                tri = tri_ref[ch["d"]]
                hi, mid, lo = ch["parts"]
                ch["bsum"] = (jnp.dot(tri, hi, preferred_element_type=F32)
                              + jnp.dot(tri, mid, preferred_element_type=F32)
                              + jnp.dot(tri, lo, preferred_element_type=F32))
            for ch in chains:
                ch["qb"], ch["kb"] = ch["q"].astype(BF16), ch["k"].astype(BF16)
                ch["att"] = _mm_nt(ch["qb"], ch["kb"]).astype(BF16) * mask_ref[ch["d"], len(levels)]
            for li, h in enumerate(levels):
                for ch in chains:
                    ref = _level_reference(ch["bsum"], h, ch["d"] == 1)
                    w = jnp.exp2(_neg_abs(ch["bsum"] - ref)).astype(BF16)
                    ch["att"] += _mm_nt(ch["qb"] * w, ch["kb"] * w).astype(BF16) * mask_ref[ch["d"], li]
            for ch in chains:
                ch["state"] = st_ref[ch["c"]]
                o = _mm(ch["att"], ch["v"]) + _mm_nt(ch["q"] * jnp.exp2(ch["bsum"]), ch["state"])
                ch["o_ref"][pl.ds(ch["start"], cs), ch["sl"]] = o.astype(ch["o_ref"].dtype)
            for ch in chains:
                bsum = ch["bsum"]
                b_end = bsum[0:1, :] if ch["d"] else bsum[cs - 1:cs, :]
                st_ref[ch["c"]] = (ch["state"] * jnp.exp2(b_end)
                                   + _mm_tn(ch["v"], ch["k"] * jnp.exp2(b_end - bsum)))
            return carry

        lax.fori_loop(0, n_chunks, step, 0)

    _finish_scan(ofc_ref, obc_ref, gc_ref, yc_ref, cs)
    _finish_scan(ofx_ref, obx_ref, gx_ref, yx_ref, cs)


def _hgrn(proj_c, proj_x, z_c, z_x, lb, first):
    b, tc, _ = proj_c.shape
    tx = proj_x.shape[1]
    h, hp = HGRN_HEADS, SCAN_HEADS
    w = hp * HEAD_DIM
    tri, masks = _hgrn_constants(HGRN_CHUNK)

    def col(t, group):
        return pl.BlockSpec((None, t, w), lambda bi, hi: (bi, 0, (first + group * h) // hp + hi),
                            pipeline_mode=pl.Buffered(1))

    def zcol(t, group):
        return pl.BlockSpec((None, t, w), lambda bi, hi: (bi, 0, group * (h // hp) + hi),
                            pipeline_mode=pl.Buffered(1))

    ins = [col(tc, 0), col(tc, 1), col(tc, 2), zcol(tc, 0), zcol(tc, 1),
           col(tx, 0), col(tx, 1), col(tx, 2), zcol(tx, 0), zcol(tx, 1)]
    return pl.pallas_call(
        _hgrn_kernel,
        grid=(b, h // hp),
        in_specs=[pl.BlockSpec((hp, 1, HEAD_DIM), lambda bi, hi: (hi, 0, 0)),
                  pl.BlockSpec(tri.shape, lambda bi, hi: (0, 0, 0)),
                  pl.BlockSpec(masks.shape, lambda bi, hi: (0, 0, 0, 0))] + ins,
        out_specs=[pl.BlockSpec((None, tc, w), lambda bi, hi: (bi, 0, hi)),
                   pl.BlockSpec((None, tx, w), lambda bi, hi: (bi, 0, hi))],
        out_shape=[jax.ShapeDtypeStruct((b, tc, h * HEAD_DIM), BF16),
                   jax.ShapeDtypeStruct((b, tx, h * HEAD_DIM), BF16)],
        scratch_shapes=[pltpu.VMEM((tc, w), BF16), pltpu.VMEM((tc, w), BF16),
                        pltpu.VMEM((tx, w), BF16), pltpu.VMEM((tx, w), BF16),
                        pltpu.VMEM((2 * hp, HEAD_DIM, HEAD_DIM), F32)],
        compiler_params=_cparams("parallel", "parallel"),
        name="hgrn2",
    )(lb.reshape(h, 1, HEAD_DIM), tri, masks, proj_c, proj_c, proj_c, z_c, z_c,
      proj_x, proj_x, proj_x, z_x, z_x)


MLA_QW = 2 * LANE
MLA_SCORE_SCALE = (MLA_NOPE + MLA_ROPE) ** -0.5 * math.log2(math.e)
MLA_DOWN_PAD = MLA_Q_RANK + MLA_KV_RANK + LANE


def _mla_proj_kernel(x_ref, mod_ref, g_ref, wd_ref, gq_ref, gkv_ref, wq_ref, wkv_ref, cos_ref, sin_ref,
                     *out_refs, rope, want_q):
    h = _modnorm(x_ref[...], g_ref[...], mod_ref, 0, 1).astype(BF16)
    down = jnp.dot(h, wd_ref[...], preferred_element_type=F32)
    if rope:
        cos = cos_ref[...]
        sin = sin_ref[...]

    def rot(v):
        return v * cos + pltpu.roll(v, LANE // 2, 1) * sin if rope else v

    if want_q:
        q_ref, kn_ref, kr_ref, v_ref = out_refs
        qn = (_rms(down[:, :MLA_Q_RANK]) * gq_ref[...]).astype(BF16)
        q = jnp.dot(qn, wq_ref[...], preferred_element_type=F32) * MLA_SCORE_SCALE
        for hd in range(MLA_HEADS):
            lo = hd * MLA_QW
            q_ref[:, lo:lo + LANE] = q[:, lo:lo + LANE].astype(q_ref.dtype)
            q_ref[:, lo + LANE:lo + 2 * LANE] = rot(q[:, lo + LANE:lo + 2 * LANE]).astype(q_ref.dtype)
    else:
        kn_ref, kr_ref, v_ref = out_refs
    kvn = (_rms(down[:, MLA_Q_RANK:MLA_Q_RANK + MLA_KV_RANK]) * gkv_ref[...]).astype(BF16)
    kv = jnp.dot(kvn, wkv_ref[...], preferred_element_type=F32)
    nk = MLA_HEADS * MLA_NOPE
    kn_ref[...] = kv[:, :nk].astype(kn_ref.dtype)
    v_ref[...] = kv[:, nk:].astype(v_ref.dtype)
    kr_ref[...] = rot(down[:, MLA_Q_RANK + MLA_KV_RANK:]).astype(kr_ref.dtype)


def _mla_proj(x, mod, g, wd, gq, gkv, wq, wkv, cos, sin, rope, want_q, tm=1024):
    b, t, d = x.shape
    tm = _pick(t, tm)
    const = lambda bi, i: (0, 0)
    tok = lambda bi, i: (bi, i, 0)
    nk, nv = MLA_HEADS * MLA_NOPE, MLA_HEADS * MLA_V
    out_shape = [jax.ShapeDtypeStruct((b, t, nk), BF16), jax.ShapeDtypeStruct((b, t, LANE), BF16),
                 jax.ShapeDtypeStruct((b, t, nv), BF16)]
    out_specs = [pl.BlockSpec((None, tm, nk), tok), pl.BlockSpec((None, tm, LANE), tok),
                 pl.BlockSpec((None, tm, nv), tok)]
    if want_q:
        out_shape = [jax.ShapeDtypeStruct((b, t, MLA_HEADS * MLA_QW), BF16)] + out_shape
        out_specs = [pl.BlockSpec((None, tm, MLA_HEADS * MLA_QW), tok)] + out_specs
    return pl.pallas_call(
        functools.partial(_mla_proj_kernel, rope=rope, want_q=want_q),
        grid=(b, t // tm),
        in_specs=[pl.BlockSpec((None, tm, d), tok), _mod_spec(mod), pl.BlockSpec((1, d), const),
                  pl.BlockSpec(wd.shape, const), pl.BlockSpec((1, MLA_Q_RANK), const),
                  pl.BlockSpec((1, MLA_KV_RANK), const), pl.BlockSpec(wq.shape, const),
                  pl.BlockSpec(wkv.shape, const),
                  pl.BlockSpec((tm, LANE), lambda bi, i: (i, 0)),
                  pl.BlockSpec((tm, LANE), lambda bi, i: (i, 0))],
        out_specs=out_specs,
        out_shape=out_shape,
        compiler_params=_cparams("parallel", "parallel"),
        name="mla_proj",
    )(x, mod, g.reshape(1, d), wd, gq.reshape(1, -1), gkv.reshape(1, -1), wq, wkv, cos, sin)


def _lane_fold(x, op):
    out = x[:, :LANE]
    for j in range(1, x.shape[1] // LANE):
        out = op(out, x[:, j * LANE:(j + 1) * LANE])
    return out


def _attention_kernel(q_ref, knc_ref, krc_ref, vc_ref, knx_ref, krx_ref, vx_ref, o_ref,
                      kc_ref, kx_ref, vc2_ref, vx2_ref, sc0_ref, sc1_ref, sx0_ref, sx1_ref, m0_ref, m1_ref,
                      *, tk):
    step = pl.program_id(2)

    @pl.when(step == 0)
    def _():
        kc_ref[:, :LANE] = knc_ref[...]
        kc_ref[:, LANE:] = krc_ref[...]
        kx_ref[:, :LANE] = knx_ref[...]
        kx_ref[:, LANE:] = krx_ref[...]
        vc2_ref[:, :LANE] = vc_ref[...]
        vx2_ref[:, :LANE] = vx_ref[...]
        for ref in (vc2_ref, vx2_ref):
            lane = lax.broadcasted_iota(jnp.int32, (ref.shape[0], LANE), 1)
            ref[:, LANE:] = jnp.where(lane == 0, 1.0, 0.0).astype(ref.dtype)

    n_blocks = kx_ref.shape[0] // tk
    n_steps = pl.num_programs(2)
    nt = (((1,), (1,)), ((), ()))

    def body(cur, prev):
        if cur is not None:
            sc_cur, sx_cur, m_cur = cur
            q = q_ref[...]
            s = lax.dot_general(q, kc_ref[...], nt, preferred_element_type=F32)
            sc_cur[...] = s
            m_part = _lane_fold(s, jnp.maximum)
        if prev is not None:
            sc_prev, sx_prev, m_prev = prev
            m = jnp.max(m_prev[...], axis=-1, keepdims=True)
            p = jnp.exp2(sc_prev[...] - m)
            acc = jnp.dot(p.astype(BF16), vc2_ref[...], preferred_element_type=F32)
        for i in range(n_blocks):
            if cur is not None:
                s = lax.dot_general(q, kx_ref[i * tk:(i + 1) * tk, :], nt, preferred_element_type=F32)
                sx_cur[i] = s
                m_part = jnp.maximum(m_part, _lane_fold(s, jnp.maximum))
            if prev is not None:
                p = jnp.exp2(sx_prev[i] - m)
                acc = acc + jnp.dot(p.astype(BF16), vx2_ref[i * tk:(i + 1) * tk, :],
                                    preferred_element_type=F32)
        if cur is not None:
            m_cur[...] = m_part
        if prev is not None:
            o_ref[...] = (acc[:, :MLA_V] / acc[:, MLA_V:MLA_V + 1]).astype(o_ref.dtype)

    bufs = ((sc0_ref, sx0_ref, m0_ref), (sc1_ref, sx1_ref, m1_ref))
    first = step == 0
    last = step == n_steps - 1
    inner = jnp.logical_not(first | last)

    @pl.when(first)
    def _():
        body(bufs[0], None)

    @pl.when(inner & (step % 2 == 0))
    def _():
        body(bufs[0], bufs[1])

    @pl.when(inner & (step % 2 == 1))
    def _():
        body(bufs[1], bufs[0])

    for parity in (0, 1):
        @pl.when(last & (step % 2 == parity))
        def _(parity=parity):
            body(None, bufs[1 - parity])


def _attention(q, kn_c, kr_c, v_c, kn_x, kr_x, v_x, tq=1024, tk=256):
    b, t, _ = q.shape
    tc = kn_c.shape[1]
    tq = _pick(t, tq)
    tk = _pick(t, tk)
    nq = t // tq
    head_c = lambda bi, hi, i: (bi, 0, hi)
    shared = lambda bi, hi, i: (bi, 0, 0)
    return pl.pallas_call(
        functools.partial(_attention_kernel, tk=tk),
        grid=(b, MLA_HEADS, nq + 1),
        in_specs=[pl.BlockSpec((None, tq, MLA_QW), lambda bi, hi, i: (bi, jnp.minimum(i, nq - 1), hi)),
                  pl.BlockSpec((None, tc, LANE), head_c), pl.BlockSpec((None, tc, LANE), shared),
                  pl.BlockSpec((None, tc, LANE), head_c),
                  pl.BlockSpec((None, t, LANE), head_c), pl.BlockSpec((None, t, LANE), shared),
                  pl.BlockSpec((None, t, LANE), head_c)],
        out_specs=pl.BlockSpec((None, tq, MLA_V), lambda bi, hi, i: (bi, jnp.maximum(i - 1, 0), hi)),
        out_shape=jax.ShapeDtypeStruct((b, t, MLA_HEADS * MLA_V), BF16),
        scratch_shapes=[pltpu.VMEM((tc, MLA_QW), BF16), pltpu.VMEM((t, MLA_QW), BF16),
                        pltpu.VMEM((tc, 2 * MLA_V), BF16), pltpu.VMEM((t, 2 * MLA_V), BF16),
                        pltpu.VMEM((tq, tc), F32), pltpu.VMEM((tq, tc), F32),
                        pltpu.VMEM((t // tk, tq, tk), F32), pltpu.VMEM((t // tk, tq, tk), F32),
                        pltpu.VMEM((tq, LANE), F32), pltpu.VMEM((tq, LANE), F32)],
        compiler_params=_cparams("parallel", "parallel", "arbitrary"),
        name="mla_attention",
    )(q, kn_c, kr_c, v_c, kn_x, kr_x, v_x)


def _router_kernel(att_ref, wo_ref, x_ref, mod_ref, g_ref, wr_ref, xo_ref, h_ref, w_ref, e_ref):
    y = jnp.dot(att_ref[...], wo_ref[...], preferred_element_type=F32)
    x1 = x_ref[...] + mod_ref[2:3, :] * (_rms(y) * g_ref[1:2, :])
    xo_ref[...] = x1
    h = _modnorm(x1, g_ref[2:3, :], mod_ref, 3, 4)
    h_ref[...] = _pack_halves(h)
    h_hi = h.astype(BF16)
    h_lo = (h - h_hi.astype(F32)).astype(BF16)
    w_hi = wr_ref[...].astype(BF16)
    w_lo = (wr_ref[...] - w_hi.astype(F32)).astype(BF16)
    logits = (jnp.dot(h_hi, w_hi, preferred_element_type=F32) + jnp.dot(h_lo, w_hi, preferred_element_type=F32)
              + jnp.dot(h_hi, w_lo, preferred_element_type=F32))
    lane = lax.broadcasted_iota(jnp.int32, logits.shape, 1)
    neg = jnp.float32(-jnp.inf)
    lg = jnp.where(lane < N_EXPERTS, logits, neg)
    m1 = jnp.max(lg, axis=-1, keepdims=True)
    i1 = jnp.min(jnp.where(lg == m1, lane, LANE), axis=-1, keepdims=True)
    lg2 = jnp.where(lane == i1, neg, lg)
    m2 = jnp.max(lg2, axis=-1, keepdims=True)
    i2 = jnp.min(jnp.where(lg2 == m2, lane, LANE), axis=-1, keepdims=True)
    e2 = jnp.exp(m2 - m1)
    w1 = 1.0 / (1.0 + e2)
    w2 = e2 / (1.0 + e2)
    w_ref[...] = jnp.where(lane == 0, w1, jnp.where(lane == 1, w2, 0.0))
    e_ref[...] = jnp.where(lane == 0, i1, jnp.where(lane == 1, i2, 0))


def _router(att, w_o, x, mod, g, router_pad, tm=1024):
    b, t, d = x.shape
    tm = _pick(t, tm)
    tok = lambda bi, i: (bi, i, 0)
    const = lambda bi, i: (0, 0)
    return pl.pallas_call(
        _router_kernel,
        grid=(b, t // tm),
        in_specs=[pl.BlockSpec((None, tm, att.shape[-1]), tok), pl.BlockSpec(w_o.shape, const),
                  pl.BlockSpec((None, tm, d), tok), _mod_spec(mod), pl.BlockSpec(g.shape, const),
                  pl.BlockSpec((d, LANE), const)],
        out_specs=[pl.BlockSpec((None, tm, d), tok), pl.BlockSpec((None, tm, d // 2), tok),
                   pl.BlockSpec((None, tm, LANE), tok), pl.BlockSpec((None, tm, LANE), tok)],
        out_shape=[jax.ShapeDtypeStruct((b, t, d), F32), jax.ShapeDtypeStruct((b, t, d // 2), jnp.int32),
                   jax.ShapeDtypeStruct((b, t, LANE), F32), jax.ShapeDtypeStruct((b, t, LANE), jnp.int32)],
        compiler_params=_cparams("parallel", "parallel"),
        name="moe_router",
    )(att, w_o, x, mod, g, router_pad)


SC_GATHER_ROWS = 64


def _gather_rows(table, idx):
    info = plsc.get_sparse_core_info()
    n_workers = info.num_cores * info.num_subcores
    n, width = idx.shape[0], table.shape[1]
    chunk = SC_GATHER_ROWS
    assert n % (n_workers * chunk) == 0
    n_chunks = n // (n_workers * chunk)
    mesh = plsc.VectorSubcoreMesh(core_axis_name="c", subcore_axis_name="s")

    def body(table_hbm, idx_hbm, out_hbm, idx_v, rows_v, sem):
        wid = lax.axis_index("s") * info.num_cores + lax.axis_index("c")
        pltpu.sync_copy(idx_hbm.at[wid], idx_v)

        @pl.loop(0, n_chunks)
        def _(j):
            pltpu.async_copy(table_hbm.at[idx_v.at[j]], rows_v, sem).wait()
            pltpu.sync_copy(rows_v, out_hbm.at[pl.ds((wid * n_chunks + j) * chunk, chunk)])

    return pl.kernel(
        body,
        out_type=jax.ShapeDtypeStruct((n, width), table.dtype),
        mesh=mesh,
        scratch_types=[pltpu.VMEM((n_chunks, chunk), jnp.int32), pltpu.VMEM((chunk, width), table.dtype),
                       pltpu.SemaphoreType.DMA],
        name="sc_gather_rows",
    )(table, idx.reshape(n_workers, n_chunks, chunk))


def _scatter_rows_twice(table, idx0, idx1, n_out):
    info = plsc.get_sparse_core_info()
    n_workers = info.num_cores * info.num_subcores
    n, width = table.shape
    chunk = SC_GATHER_ROWS
    assert n % (n_workers * chunk) == 0
    n_chunks = n // (n_workers * chunk)
    mesh = plsc.VectorSubcoreMesh(core_axis_name="c", subcore_axis_name="s")

    def body(table_hbm, i0_hbm, i1_hbm, out_hbm, i0_v, i1_v, rows_v):
        wid = lax.axis_index("s") * info.num_cores + lax.axis_index("c")
        pltpu.sync_copy(i0_hbm.at[wid], i0_v)
        pltpu.sync_copy(i1_hbm.at[wid], i1_v)

        @pl.loop(0, n_chunks)
        def _(j):
            pltpu.sync_copy(table_hbm.at[pl.ds((wid * n_chunks + j) * chunk, chunk)], rows_v)
            pltpu.sync_copy(rows_v, out_hbm.at[i0_v.at[j]])
            pltpu.sync_copy(rows_v, out_hbm.at[i1_v.at[j]])

    shape3 = (n_workers, n_chunks, chunk)
    return pl.kernel(
        body,
        out_type=jax.ShapeDtypeStruct((n_out, width), table.dtype),
        mesh=mesh,
        scratch_types=[pltpu.VMEM((n_chunks, chunk), jnp.int32), pltpu.VMEM((n_chunks, chunk), jnp.int32),
                       pltpu.VMEM((chunk, width), table.dtype)],
        name="sc_scatter_rows",
    )(table, idx0.reshape(shape3), idx1.reshape(shape3))


def _pack_halves(x):
    k = x.shape[1] // 2
    bits = lambda v: lax.bitcast_convert_type(v.astype(BF16).astype(F32), jnp.uint32)
    word = lax.shift_right_logical(bits(x[:, :k]), jnp.uint32(16)) | bits(x[:, k:])
    return lax.bitcast_convert_type(word, jnp.int32)


def _unpack_halves(w):
    u = lax.bitcast_convert_type(w, jnp.uint32)
    lo = lax.bitcast_convert_type(lax.shift_left(u, jnp.uint32(16)), F32)
    hi = lax.bitcast_convert_type(u & jnp.uint32(0xFFFF0000), F32)
    return jnp.concatenate([lo, hi], axis=-1)


def _expert_ffn_kernel(te_ref, tv_ref, x_ref, wa_ref, wb_ref, wo_ref, o_ref, acc_ref):
    j = pl.program_id(0)
    f = pl.program_id(1)
    last = pl.num_programs(1) - 1
    valid = tv_ref[j] != 0

    @pl.when(valid)
    def _():
        @pl.when(f == 0)
        def _():
            acc_ref[...] = jnp.zeros_like(acc_ref)

        h = _unpack_halves(x_ref[...]).astype(BF16)
        a = jnp.dot(h, wa_ref[...], preferred_element_type=F32)
        bb = jnp.dot(h, wb_ref[...], preferred_element_type=F32)
        acc_ref[...] += jnp.dot((_silu(a) * bb).astype(BF16), wo_ref[...], preferred_element_type=F32)

        @pl.when(f == last)
        def _():
            o_ref[...] = _pack_halves(acc_ref[...])

    @pl.when(jnp.logical_not(valid) & (f == last))
    def _():
        o_ref[...] = jnp.zeros_like(o_ref)


def _expert_ffn(xs, tile_expert, tile_valid, w_in, w_out, tm, tf=2816):
    p = xs.shape[0]
    n_exp, fdim, d = w_out.shape
    tf = _pick(fdim, tf)
    nf = fdim // tf

    def f_eff(j, f, tv):
        return jnp.where(tv[j] != 0, f, nf - 1)

    grid_spec = pltpu.PrefetchScalarGridSpec(
        num_scalar_prefetch=2,
        grid=(p // tm, nf),
        in_specs=[pl.BlockSpec((tm, d // 2), lambda j, f, te, tv: (j, 0)),
                  pl.BlockSpec((None, d, tf), lambda j, f, te, tv: (te[j], 0, f_eff(j, f, tv)),
                               pipeline_mode=pl.Buffered(1)),
                  pl.BlockSpec((None, d, tf), lambda j, f, te, tv: (te[j], 0, f_eff(j, f, tv) + nf),
                               pipeline_mode=pl.Buffered(1)),
                  pl.BlockSpec((None, tf, d), lambda j, f, te, tv: (te[j], f_eff(j, f, tv), 0),
                               pipeline_mode=pl.Buffered(1))],
        out_specs=pl.BlockSpec((tm, d // 2), lambda j, f, te, tv: (j, 0)),
        scratch_shapes=[pltpu.VMEM((tm, d), F32)])
    return pl.pallas_call(
        _expert_ffn_kernel,
        grid_spec=grid_spec,
        out_shape=jax.ShapeDtypeStruct((p, d // 2), jnp.int32),
        compiler_params=_cparams("parallel", "arbitrary"),
        name="moe_expert_ffn",
    )(tile_expert, tile_valid, xs, w_in, w_in, w_out)


def _moe_combine_kernel(x_ref, y0_ref, y1_ref, w_ref, mod_ref, g_ref, o_ref):
    y = w_ref[:, 0:1] * _unpack_halves(y0_ref[...]) + w_ref[:, 1:2] * _unpack_halves(y1_ref[...])
    o_ref[...] = x_ref[...] + mod_ref[5:6, :] * (_rms(y) * g_ref[...])


def _moe_combine(x, y2, w, mod, g, tm=1024):
    b, t, d = x.shape
    tm = _pick(t, tm)
    tok = lambda bi, i: (bi, i, 0)
    return pl.pallas_call(
        _moe_combine_kernel,
        grid=(b, t // tm),
        in_specs=[pl.BlockSpec((None, tm, d), tok),
                  pl.BlockSpec((None, None, tm, d // 2), lambda bi, i: (0, bi, i, 0)),
                  pl.BlockSpec((None, None, tm, d // 2), lambda bi, i: (1, bi, i, 0)),
                  pl.BlockSpec((None, tm, LANE), tok), _mod_spec(mod),
                  pl.BlockSpec((1, d), lambda bi, i: (0, 0))],
        out_specs=pl.BlockSpec((None, tm, d), tok),
        out_shape=jax.ShapeDtypeStruct((b, t, d), F32),
        compiler_params=_cparams("parallel", "parallel"),
        name="moe_combine",
    )(x, y2, y2, w, mod, g.reshape(1, d))


def _moe(x, h, w, e, mod, g_out, w_in, w_out, tm=512):
    b, t, d = x.shape
    n_exp = w_out.shape[0]
    n = b * t
    e_flat = jnp.concatenate([e[..., 0].reshape(n), e[..., 1].reshape(n)])
    onehot = (e_flat[:, None] == jnp.arange(n_exp, dtype=jnp.int32)).astype(jnp.int32)
    csum = jnp.cumsum(onehot, axis=0)
    rank = jnp.sum((csum - onehot) * onehot, axis=-1)
    counts = csum[-1]
    ends = jnp.cumsum((counts + tm - 1) // tm * tm)
    dest = (ends - (counts + tm - 1) // tm * tm)[e_flat] + rank
    p = 2 * n + n_exp * tm
    tile_start = jnp.arange(p // tm, dtype=jnp.int32) * tm
    tile_expert = jnp.minimum(jnp.searchsorted(ends, tile_start, side="right"), n_exp - 1).astype(jnp.int32)
    tile_valid = (tile_start < ends[-1]).astype(jnp.int32)

    xs = _scatter_rows_twice(h.reshape(n, d // 2), dest[:n], dest[n:], p)
    ys = _expert_ffn(xs, tile_expert, tile_valid, w_in, w_out, tm)
    y2 = _gather_rows(ys, dest)
    return _moe_combine(x, y2.reshape(2, b, t, d // 2), w, mod, g_out)


def _axial_angles(n_tokens, dim):
    t = jnp.arange(n_tokens)
    rows = (t // GRID_W).astype(F32)
    cols = (t % GRID_W).astype(F32)
    n_freq = dim // 4
    inv = ROPE_BASE ** (-jnp.arange(n_freq, dtype=F32) / n_freq)
    return jnp.concatenate([rows[:, None] * inv, cols[:, None] * inv], axis=-1)


def _rope_tables(n_tokens, dim):
    ang = _axial_angles(n_tokens, dim)
    cos, sin = jnp.cos(ang), jnp.sin(ang)
    if dim == LANE:
        return jnp.concatenate([cos, cos], -1), jnp.concatenate([-sin, sin], -1)
    z = jnp.zeros_like(cos)
    return jnp.concatenate([cos, z, cos, z], -1), jnp.concatenate([-sin, z, sin, z], -1)


def _spread_rope_cols(w):
    half = MLA_ROPE // 2
    z = jnp.zeros(w.shape[:-1] + (half,), w.dtype)
    return jnp.concatenate([w[..., :half], z, w[..., half:], z], axis=-1)


def kernel(x, c, ctx, c_ctx, mod_w, mod_b, norm_g, mix_in_w, ret_decay_logit, hgrn_lb_logit, mix_out_w,
           ffn_in_w, ffn_out_w, mla_down_w, mla_q_norm_g, mla_kv_norm_g, mla_uq_w, mla_ukv_w, mla_out_w,
           router_w, moe_in_w, moe_out_w):
    batch, seq, d = x.shape
    assert mod_w.shape[0] == 2, "two layers: retention/HGRN2 then MLA/MoE"

    rows = -(-(batch + 1) // 8) * 8
    cond = jnp.zeros((rows, d), F32).at[:batch].set(c).at[batch].set(c_ctx)
    mod = _modulation(cond, mod_w, mod_b)
    mod_x = [mod[l, :batch].reshape(batch, 6, d) for l in range(2)]
    mod_c = [mod[l, batch:batch + 1].reshape(1, 6, d) for l in range(2)]

    g = norm_g[0]
    ret_w = RET_HEADS * HEAD_DIM
    z_lo, z_hi = 5 * ret_w, 7 * ret_w
    w_in = mix_in_w[0].astype(BF16)
    w_main = jnp.concatenate([w_in[:, :z_lo], w_in[:, z_hi:]], axis=1)
    w_z = w_in[:, z_lo:z_hi]
    proj_x = _norm_matmul(x, mod_x[0], g[0], w_main, 0, 1, BF16)
    proj_c = _norm_matmul(ctx, mod_c[0], g[0], w_main, 0, 1, BF16)
    z_x = _norm_matmul(x, mod_x[0], g[0], w_z, 0, 1, F32)
    z_c = _norm_matmul(ctx, mod_c[0], g[0], w_z, 0, 1, F32)
    cos2, sin2 = _rope_tables(seq, HEAD_DIM)
    log_gamma = jax.nn.log_sigmoid(ret_decay_logit[0].astype(F32))
    lb = jnp.cumsum(jax.nn.softmax(hgrn_lb_logit.astype(F32), axis=0), axis=0)[0]
    ret_c, ret_x = _retention(proj_c, proj_x, log_gamma, cos2, sin2)
    hg_c, hg_x = _hgrn(proj_c, proj_x, z_c, z_x, lb, first=4 * RET_HEADS)
    w_out = mix_out_w[0].astype(BF16)
    w_outs = [w_out[:ret_w], w_out[ret_w:]]
    f_in, f_out = ffn_in_w[0].astype(BF16), ffn_out_w[0].astype(BF16)
    x = _mixer_ffn(ret_x, hg_x, *w_outs, x, mod_x[0], g, f_in, f_out)
    ctx = _mixer_ffn(ret_c, hg_c, *w_outs, ctx, mod_c[0], g, f_in, f_out)

    g = norm_g[1]
    dq = MLA_NOPE + MLA_ROPE
    wd = mla_down_w[0]
    wd = jnp.concatenate([wd[:, :MLA_Q_RANK + MLA_KV_RANK],
                          _spread_rope_cols(wd[:, MLA_Q_RANK + MLA_KV_RANK:])], axis=-1).astype(BF16)
    wq = mla_uq_w[0].reshape(MLA_Q_RANK, MLA_HEADS, dq)
    wq = jnp.concatenate([wq[..., :MLA_NOPE], _spread_rope_cols(wq[..., MLA_NOPE:])], axis=-1)
    wq = wq.reshape(MLA_Q_RANK, MLA_HEADS * MLA_QW).astype(BF16)
    wkv = mla_ukv_w[0].reshape(MLA_KV_RANK, MLA_HEADS, MLA_NOPE + MLA_V)
    wkv = jnp.concatenate([wkv[..., :MLA_NOPE].reshape(MLA_KV_RANK, -1),
                           wkv[..., MLA_NOPE:].reshape(MLA_KV_RANK, -1)], axis=-1).astype(BF16)
    cos_m, sin_m = _rope_tables(seq, MLA_ROPE)
    q, kn_x, kr_x, v_x = _mla_proj(x, mod_x[1], g[0], wd, mla_q_norm_g[0], mla_kv_norm_g[0], wq, wkv,
                                   cos_m, sin_m, rope=True, want_q=True)
    tc = ctx.shape[1]
    kn_c, kr_c, v_c = _mla_proj(ctx, mod_c[1], g[0], wd, mla_q_norm_g[0], mla_kv_norm_g[0], wq, wkv,
                                cos_m[:tc], sin_m[:tc], rope=False, want_q=False)
    att = _attention(q, kn_c, kr_c, v_c, kn_x, kr_x, v_x)
    router_pad = jnp.zeros((d, LANE), F32).at[:, :N_EXPERTS].set(router_w[0])
    x, h, w, e = _router(att, mla_out_w[0].astype(BF16), x, mod_x[1], g, router_pad)
    x = _moe(x, h, w, e, mod_x[1], g[3], moe_in_w[0].astype(BF16), moe_out_w[0].astype(BF16))
    return x
```

```python
import functools
import math

import jax
import jax.numpy as jnp
import numpy as np
from jax import lax
from jax.experimental import pallas as pl
from jax.experimental.pallas import tpu as pltpu
from jax.experimental.pallas import tpu_sc as plsc

F32 = jnp.float32
BF16 = jnp.bfloat16

NORM_EPS = 1e-6
ROPE_BASE = 10000.0
GRID_W = 64
LANE = 128

RET_HEADS = 4
HGRN_HEADS = 4
HEAD_DIM = 128
RET_CHUNK = 128
HGRN_CHUNK = 128

MLA_HEADS = 8
MLA_NOPE = 128
MLA_ROPE = 64
MLA_V = 128
MLA_Q_RANK = 384
MLA_KV_RANK = 256
N_EXPERTS = 8

VMEM_LIMIT = 56 * 1024 * 1024


def _cparams(*sem):
    return pltpu.CompilerParams(dimension_semantics=sem, vmem_limit_bytes=VMEM_LIMIT)


def _mm(a, b):
    return jnp.dot(a.astype(BF16), b.astype(BF16), preferred_element_type=F32)


def _mm_nt(a, b):
    return lax.dot_general(a.astype(BF16), b.astype(BF16), (((1,), (1,)), ((), ())),
                           preferred_element_type=F32)


def _mm_tn(a, b):
    return lax.dot_general(a.astype(BF16), b.astype(BF16), (((0,), (0,)), ((), ())),
                           preferred_element_type=F32)


def _rms(x):
    return x * lax.rsqrt(jnp.mean(x * x, axis=-1, keepdims=True) + NORM_EPS)


def _silu(x):
    return x * (1.0 / (1.0 + jnp.exp(-x)))


def _modnorm(x, g, mod_ref, shift_row, scale_row):
    shift = mod_ref[shift_row:shift_row + 1, :]
    scale = mod_ref[scale_row:scale_row + 1, :]
    return _rms(x) * g * (1.0 + scale) + shift


def _mod_spec(mod):
    d = mod.shape[-1]
    if mod.shape[0] == 1:
        return pl.BlockSpec((None, 6, d), lambda b, *_: (0, 0, 0))
    return pl.BlockSpec((None, 6, d), lambda b, *_: (b, 0, 0))


ROW_BLOCK = 256


def _row_blocks(n_rows):
    size = min(ROW_BLOCK, n_rows)
    return [dict(rows=pl.ds(r, size)) for r in range(0, n_rows, size)]


def _pick(n, pref):
    t = min(pref, n)
    while n % t:
        t -= LANE
    return t


def _mod_kernel(c_ref, w_ref, b_ref, o_ref):
    a = _silu(c_ref[...])
    o_ref[...] = jnp.dot(a, w_ref[...], preferred_element_type=F32,
                         precision=lax.Precision.HIGHEST) + b_ref[...]


def _modulation(cond, mod_w, mod_b):
    n_layers, d, n = mod_w.shape
    r = cond.shape[0]
    tn = _pick(n, 1536)
    return pl.pallas_call(
        _mod_kernel,
        grid=(n_layers, n // tn),
        in_specs=[pl.BlockSpec((r, d), lambda l, j: (0, 0)),
                  pl.BlockSpec((None, d, tn), lambda l, j: (l, 0, j)),
                  pl.BlockSpec((None, 1, tn), lambda l, j: (l, 0, j))],
        out_specs=pl.BlockSpec((None, r, tn), lambda l, j: (l, 0, j)),
        out_shape=jax.ShapeDtypeStruct((n_layers, r, n), F32),
        compiler_params=_cparams("parallel", "parallel"),
        name="modulation",
    )(cond, mod_w, mod_b.reshape(n_layers, 1, n))


def _norm_matmul_kernel(x_ref, mod_ref, g_ref, w_ref, o_ref, h_ref, *, shift_row, scale_row):
    @pl.when(pl.program_id(2) == 0)
    def _():
        h_ref[...] = _modnorm(x_ref[...], g_ref[...], mod_ref, shift_row, scale_row).astype(BF16)

    o_ref[...] = jnp.dot(h_ref[...], w_ref[...], preferred_element_type=F32).astype(o_ref.dtype)


def _norm_matmul(x, mod, g, w, shift_row, scale_row, out_dtype, tm=1024, tn=1792):
    b, t, d = x.shape
    n = w.shape[1]
    tm = _pick(t, tm)
    tn = _pick(n, tn)
    return pl.pallas_call(
        functools.partial(_norm_matmul_kernel, shift_row=shift_row, scale_row=scale_row),
        grid=(b, t // tm, n // tn),
        in_specs=[pl.BlockSpec((None, tm, d), lambda bi, i, j: (bi, i, 0)),
                  _mod_spec(mod),
                  pl.BlockSpec((1, d), lambda bi, i, j: (0, 0)),
                  pl.BlockSpec((d, tn), lambda bi, i, j: (0, j))],
        out_specs=pl.BlockSpec((None, tm, tn), lambda bi, i, j: (bi, i, j)),
        out_shape=jax.ShapeDtypeStruct((b, t, n), out_dtype),
        scratch_shapes=[pltpu.VMEM((tm, d), BF16)],
        compiler_params=_cparams("parallel", "parallel", "arbitrary"),
        name="norm_matmul",
    )(x, mod, g.reshape(1, d), w)


def _mixer_ffn_kernel(ret_ref, hg_ref, w1_ref, w2_ref, x_ref, mod_ref, g_ref, wa_ref, wb_ref, wo_ref, o_ref):
    blocks = _row_blocks(x_ref.shape[0])
    for blk in blocks:
        rows = blk["rows"]
        blk["y"] = (jnp.dot(ret_ref[rows, :], w1_ref[...], preferred_element_type=F32)
                    + jnp.dot(hg_ref[rows, :], w2_ref[...], preferred_element_type=F32))
    for blk in blocks:
        blk["x1"] = x_ref[blk["rows"], :] + mod_ref[2:3, :] * (_rms(blk["y"]) * g_ref[1:2, :])
        blk["h"] = _modnorm(blk["x1"], g_ref[2:3, :], mod_ref, 3, 4).astype(BF16)
    for blk in blocks:
        a = jnp.dot(blk["h"], wa_ref[...], preferred_element_type=F32)
        bb = jnp.dot(blk["h"], wb_ref[...], preferred_element_type=F32)
        blk["u"] = (_silu(a) * bb).astype(BF16)
    for blk in blocks:
        z = jnp.dot(blk["u"], wo_ref[...], preferred_element_type=F32)
        o_ref[blk["rows"], :] = blk["x1"] + mod_ref[5:6, :] * (_rms(z) * g_ref[3:4, :])


def _mixer_ffn(ret, hg, w1, w2, x, mod, g, w_in, w_out, tm=512):
    b, t, d = x.shape
    fdim = w_out.shape[0]
    tm = _pick(t, tm)
    tok = lambda bi, i: (bi, i, 0)
    resident = lambda shape, idx: pl.BlockSpec(shape, lambda bi, i: idx, pipeline_mode=pl.Buffered(1))
    return pl.pallas_call(
        _mixer_ffn_kernel,
        grid=(b, t // tm),
        in_specs=[pl.BlockSpec((None, tm, ret.shape[-1]), tok), pl.BlockSpec((None, tm, hg.shape[-1]), tok),
                  resident(w1.shape, (0, 0)), resident(w2.shape, (0, 0)),
                  pl.BlockSpec((None, tm, d), tok), _mod_spec(mod), pl.BlockSpec(g.shape, lambda bi, i: (0, 0)),
                  resident((d, fdim), (0, 0)), resident((d, fdim), (0, 1)), resident((fdim, d), (0, 0))],
        out_specs=pl.BlockSpec((None, tm, d), tok),
        out_shape=jax.ShapeDtypeStruct((b, t, d), F32),
        compiler_params=_cparams("parallel", "parallel"),
        name="mixer_ffn",
    )(ret, hg, w1, w2, x, mod, g, w_in, w_in, w_out)


SCAN_HEADS = 4


def _head_slice(hh):
    return slice(hh * HEAD_DIM, (hh + 1) * HEAD_DIM)


def _finish_scan(of_ref, ob_ref, g_ref, y_ref, cs):
    def step(i, carry):
        start = pl.multiple_of(i * cs, cs)
        for hh in range(y_ref.shape[1] // HEAD_DIM):
            sl = _head_slice(hh)
            o = of_ref[pl.ds(start, cs), sl].astype(F32) + ob_ref[pl.ds(start, cs), sl].astype(F32)
            gate = g_ref[pl.ds(start, cs), sl].astype(F32)
            y_ref[pl.ds(start, cs), sl] = (_rms(o) * _silu(gate)).astype(y_ref.dtype)
        return carry

    lax.fori_loop(0, y_ref.shape[0] // cs, step, 0)


def _retention_kernel(lg_ref, qc_ref, kc_ref, vc_ref, gc_ref, qx_ref, kx_ref, vx_ref, gx_ref,
                      cos_ref, sin_ref, yc_ref, yx_ref,
                      ofc_ref, obc_ref, ofx_ref, obx_ref, s_ref, intra_ref, qd_ref, kd_ref):
    cs = RET_CHUNK
    hp = SCAN_HEADS
    head0 = pl.program_id(1) * hp
    row = lax.broadcasted_iota(jnp.int32, (cs, cs), 0).astype(F32)
    col = lax.broadcasted_iota(jnp.int32, (cs, cs), 1).astype(F32)
    pos = lax.broadcasted_iota(jnp.int32, (cs, HEAD_DIM), 0).astype(F32)
    k_scale = HEAD_DIM ** -0.5

    for hh in range(hp):
        for d in (0, 1):
            c = 2 * hh + d
            lg = lg_ref[d, head0 + hh]
            rel = (col - row) if d else (row - col)
            intra_ref[c] = jnp.where(rel >= 0, jnp.exp(lg * jnp.maximum(rel, 0.0)), 0.0)
            p = (cs - 1.0 - pos) if d else pos
            qd_ref[c] = jnp.exp(lg * (p + 1.0))
            kd_ref[c] = jnp.exp(lg * (cs - 1.0 - p))
    s_ref[...] = jnp.zeros_like(s_ref)

    for (q_ref, k_ref, v_ref, of_ref, ob_ref, rope) in (
            (qc_ref, kc_ref, vc_ref, ofc_ref, obc_ref, False),
            (qx_ref, kx_ref, vx_ref, ofx_ref, obx_ref, True)):
        n_chunks = q_ref.shape[0] // cs

        def step(i, carry, q_ref=q_ref, k_ref=k_ref, v_ref=v_ref, of_ref=of_ref, ob_ref=ob_ref,
                 rope=rope, n_chunks=n_chunks):
            chains = []
            for d in (0, 1):
                start = pl.multiple_of(((n_chunks - 1 - i) if d else i) * cs, cs)
                if rope:
                    cos = cos_ref[pl.ds(start, cs), :]
                    sin = sin_ref[pl.ds(start, cs), :]
                for hh in range(hp):
                    sl = _head_slice(hh)
                    q = q_ref[pl.ds(start, cs), sl].astype(F32)
                    k = k_ref[pl.ds(start, cs), sl].astype(F32) * k_scale
                    if rope:
                        q = q * cos + pltpu.roll(q, HEAD_DIM // 2, 1) * sin
                        k = k * cos + pltpu.roll(k, HEAD_DIM // 2, 1) * sin
                    chains.append(dict(d=d, hh=hh, c=2 * hh + d, start=start, sl=sl, q=q, k=k,
                                       v=v_ref[pl.ds(start, cs), sl], o_ref=ob_ref if d else of_ref))
            for ch in chains:
                ch["att"] = _mm_nt(ch["q"], ch["k"]) * intra_ref[ch["c"]]
            for ch in chains:
                c = ch["c"]
                ch["state"] = s_ref[c]
                ch["o_ref"][pl.ds(ch["start"], cs), ch["sl"]] = (
                    _mm(ch["att"], ch["v"]) + _mm(ch["q"] * qd_ref[c], ch["state"]))
            for ch in chains:
                c = ch["c"]
                c_dec = jnp.exp(jnp.full((1, HEAD_DIM), lg_ref[ch["d"], head0 + ch["hh"]] * cs, F32))
                s_ref[c] = ch["state"] * c_dec + _mm_tn(ch["k"] * kd_ref[c], ch["v"])
            return carry

        lax.fori_loop(0, n_chunks, step, 0)

    _finish_scan(ofc_ref, obc_ref, gc_ref, yc_ref, cs)
    _finish_scan(ofx_ref, obx_ref, gx_ref, yx_ref, cs)


def _retention(proj_c, proj_x, log_gamma, cos2, sin2):
    b, tc, _ = proj_c.shape
    tx = proj_x.shape[1]
    h, hp = RET_HEADS, SCAN_HEADS
    w = hp * HEAD_DIM

    def col(t, group):
        return pl.BlockSpec((None, t, w), lambda bi, hi, lg: (bi, 0, group * (h // hp) + hi),
                            pipeline_mode=pl.Buffered(1))

    n_chain = 2 * hp
    grid_spec = pltpu.PrefetchScalarGridSpec(
        num_scalar_prefetch=1,
        grid=(b, h // hp),
        in_specs=[col(tc, 0), col(tc, 1), col(tc, 2), col(tc, 3),
                  col(tx, 0), col(tx, 1), col(tx, 2), col(tx, 3),
                  pl.BlockSpec((tx, HEAD_DIM), lambda bi, hi, lg: (0, 0), pipeline_mode=pl.Buffered(1)),
                  pl.BlockSpec((tx, HEAD_DIM), lambda bi, hi, lg: (0, 0), pipeline_mode=pl.Buffered(1))],
        out_specs=[pl.BlockSpec((None, tc, w), lambda bi, hi, lg: (bi, 0, hi)),
                   pl.BlockSpec((None, tx, w), lambda bi, hi, lg: (bi, 0, hi))],
        scratch_shapes=[pltpu.VMEM((tc, w), F32), pltpu.VMEM((tc, w), F32),
                        pltpu.VMEM((tx, w), F32), pltpu.VMEM((tx, w), F32),
                        pltpu.VMEM((n_chain, HEAD_DIM, HEAD_DIM), F32),
                        pltpu.VMEM((n_chain, RET_CHUNK, RET_CHUNK), F32),
                        pltpu.VMEM((n_chain, RET_CHUNK, HEAD_DIM), F32),
                        pltpu.VMEM((n_chain, RET_CHUNK, HEAD_DIM), F32)])
    return pl.pallas_call(
        _retention_kernel,
        grid_spec=grid_spec,
        out_shape=[jax.ShapeDtypeStruct((b, tc, h * HEAD_DIM), BF16),
                   jax.ShapeDtypeStruct((b, tx, h * HEAD_DIM), BF16)],
        compiler_params=_cparams("parallel", "parallel"),
        name="retention",
    )(log_gamma, proj_c, proj_c, proj_c, proj_c, proj_x, proj_x, proj_x, proj_x, cos2, sin2)


def _hgrn_levels(cs):
    return [cs >> (i + 1) for i in range(int(math.log2(cs)))]


def _hgrn_constants(cs):
    t = np.arange(cs)[:, None]
    s = np.arange(cs)[None, :]
    tri = np.stack([(s <= t), (s >= t)]).astype(np.float32)
    masks = []
    for reverse in (False, True):
        lv = []
        for h in _hgrn_levels(cs):
            same = (t // (2 * h)) == (s // (2 * h))
            t_hi = (t % (2 * h)) >= h
            s_hi = (s % (2 * h)) >= h
            lv.append(same & (~t_hi & s_hi if reverse else t_hi & ~s_hi))
        lv.append(t == s)
        masks.append(np.stack(lv))
    return jnp.asarray(tri, BF16), jnp.asarray(np.stack(masks).astype(np.float32), BF16)


def _split3(x):
    hi = x.astype(BF16)
    r = x - hi.astype(F32)
    mid = r.astype(BF16)
    lo = (r - mid.astype(F32)).astype(BF16)
    return hi, mid, lo


def _neg_abs(x):
    bits = lax.bitcast_convert_type(x, jnp.uint32) | jnp.uint32(0x80000000)
    return lax.bitcast_convert_type(bits, F32)


def _level_reference(b, h, reverse):
    cs = b.shape[0]
    ref_local = h if reverse else h - 1
    if 2 * h >= 8:
        b3 = b.reshape(cs // (2 * h), 2 * h, b.shape[1])
        r = b3[:, ref_local:ref_local + 1, :]
        return jnp.broadcast_to(r, b3.shape).reshape(b.shape)
    local = lax.broadcasted_iota(jnp.int32, b.shape, 0) % (2 * h)
    out = b
    for m in range(2 * h):
        if m != ref_local:
            out = jnp.where(local == m, pltpu.roll(b, (m - ref_local) % cs, 0), out)
    return out


def _hgrn_kernel(lb_ref, tri_ref, mask_ref,
                 qc_ref, vc_ref, gc_ref, zfc_ref, zbc_ref,
                 qx_ref, vx_ref, gx_ref, zfx_ref, zbx_ref,
                 yc_ref, yx_ref, ofc_ref, obc_ref, ofx_ref, obx_ref, st_ref):
    cs = HGRN_CHUNK
    hp = SCAN_HEADS
    levels = _hgrn_levels(cs)
    st_ref[...] = jnp.zeros_like(st_ref)

    for (q_ref, v_ref, zf_ref, zb_ref, of_ref, ob_ref) in (
            (qc_ref, vc_ref, zfc_ref, zbc_ref, ofc_ref, obc_ref),
            (qx_ref, vx_ref, zfx_ref, zbx_ref, ofx_ref, obx_ref)):
        n_chunks = q_ref.shape[0] // cs

        def step(i, carry, q_ref=q_ref, v_ref=v_ref, zf_ref=zf_ref, zb_ref=zb_ref, of_ref=of_ref,
                 ob_ref=ob_ref, n_chunks=n_chunks):
            chains = []
            for d in (0, 1):
                start = pl.multiple_of(((n_chunks - 1 - i) if d else i) * cs, cs)
                z_ref = zb_ref if d else zf_ref
                for hh in range(hp):
                    sl = _head_slice(hh)
                    lb = lb_ref[hh]
                    q = _silu(q_ref[pl.ds(start, cs), sl].astype(F32))
                    f = lb + (1.0 - lb) * (1.0 / (1.0 + jnp.exp(-z_ref[pl.ds(start, cs), sl])))
                    chains.append(dict(d=d, c=2 * hh + d, start=start, sl=sl, q=q, k=1.0 - f,
                                       parts=_split3(jnp.log2(f)), v=v_ref[pl.ds(start, cs), sl],
                                       o_ref=ob_ref if d else of_ref))
            for ch in chains:
                tri = tri_ref[ch["d"]]
                hi, mid, lo = ch["parts"]
                ch["bsum"] = (jnp.dot(tri, hi, preferred_element_type=F32)
                              + jnp.dot(tri, mid, preferred_element_type=F32)
                              + jnp.dot(tri, lo, preferred_element_type=F32))
            for ch in chains:
                ch["qb"], ch["kb"] = ch["q"].astype(BF16), ch["k"].astype(BF16)
                ch["att"] = _mm_nt(ch["qb"], ch["kb"]).astype(BF16) * mask_ref[ch["d"], len(levels)]
            for li, h in enumerate(levels):
                for ch in chains:
                    ref = _level_reference(ch["bsum"], h, ch["d"] == 1)
                    w = jnp.exp2(_neg_abs(ch["bsum"] - ref)).astype(BF16)
                    ch["att"] += _mm_nt(ch["qb"] * w, ch["kb"] * w).astype(BF16) * mask_ref[ch["d"], li]
            for ch in chains:
                ch["state"] = st_ref[ch["c"]]
                o = _mm(ch["att"], ch["v"]) + _mm_nt(ch["q"] * jnp.exp2(ch["bsum"]), ch["state"])
                ch["o_ref"][pl.ds(ch["start"], cs), ch["sl"]] = o.astype(ch["o_ref"].dtype)
            for ch in chains:
                bsum = ch["bsum"]
                b_end = bsum[0:1, :] if ch["d"] else bsum[cs - 1:cs, :]
                st_ref[ch["c"]] = (ch["state"] * jnp.exp2(b_end)
                                   + _mm_tn(ch["v"], ch["k"] * jnp.exp2(b_end - bsum)))
            return carry

        lax.fori_loop(0, n_chunks, step, 0)

    _finish_scan(ofc_ref, obc_ref, gc_ref, yc_ref, cs)
    _finish_scan(ofx_ref, obx_ref, gx_ref, yx_ref, cs)


def _hgrn(proj_c, proj_x, z_c, z_x, lb, first):
    b, tc, _ = proj_c.shape
    tx = proj_x.shape[1]
    h, hp = HGRN_HEADS, SCAN_HEADS
    w = hp * HEAD_DIM
    tri, masks = _hgrn_constants(HGRN_CHUNK)

    def col(t, group):
        return pl.BlockSpec((None, t, w), lambda bi, hi: (bi, 0, (first + group * h) // hp + hi),
                            pipeline_mode=pl.Buffered(1))

    def zcol(t, group):
        return pl.BlockSpec((None, t, w), lambda bi, hi: (bi, 0, group * (h // hp) + hi),
                            pipeline_mode=pl.Buffered(1))

    ins = [col(tc, 0), col(tc, 1), col(tc, 2), zcol(tc, 0), zcol(tc, 1),
           col(tx, 0), col(tx, 1), col(tx, 2), zcol(tx, 0), zcol(tx, 1)]
    return pl.pallas_call(
        _hgrn_kernel,
        grid=(b, h // hp),
        in_specs=[pl.BlockSpec((hp, 1, HEAD_DIM), lambda bi, hi: (hi, 0, 0)),
                  pl.BlockSpec(tri.shape, lambda bi, hi: (0, 0, 0)),
                  pl.BlockSpec(masks.shape, lambda bi, hi: (0, 0, 0, 0))] + ins,
        out_specs=[pl.BlockSpec((None, tc, w), lambda bi, hi: (bi, 0, hi)),
                   pl.BlockSpec((None, tx, w), lambda bi, hi: (bi, 0, hi))],
        out_shape=[jax.ShapeDtypeStruct((b, tc, h * HEAD_DIM), BF16),
                   jax.ShapeDtypeStruct((b, tx, h * HEAD_DIM), BF16)],
        scratch_shapes=[pltpu.VMEM((tc, w), BF16), pltpu.VMEM((tc, w), BF16),
                        pltpu.VMEM((tx, w), BF16), pltpu.VMEM((tx, w), BF16),
                        pltpu.VMEM((2 * hp, HEAD_DIM, HEAD_DIM), F32)],
        compiler_params=_cparams("parallel", "parallel"),
        name="hgrn2",
    )(lb.reshape(h, 1, HEAD_DIM), tri, masks, proj_c, proj_c, proj_c, z_c, z_c,
      proj_x, proj_x, proj_x, z_x, z_x)


MLA_QW = 2 * LANE
MLA_SCORE_SCALE = (MLA_NOPE + MLA_ROPE) ** -0.5 * math.log2(math.e)
MLA_DOWN_PAD = MLA_Q_RANK + MLA_KV_RANK + LANE


def _mla_proj_kernel(x_ref, mod_ref, g_ref, wd_ref, gq_ref, gkv_ref, wq_ref, wkv_ref, cos_ref, sin_ref,
                     *out_refs, rope, want_q):
    if want_q:
        q_ref, kn_ref, kr_ref, v_ref = out_refs
    else:
        kn_ref, kr_ref, v_ref = out_refs
    nk = MLA_HEADS * MLA_NOPE

    def rot(v, rows):
        return v * cos_ref[rows, :] + pltpu.roll(v, LANE // 2, 1) * sin_ref[rows, :] if rope else v

    blocks = _row_blocks(x_ref.shape[0])
    for blk in blocks:
        blk["h"] = _modnorm(x_ref[blk["rows"], :], g_ref[...], mod_ref, 0, 1).astype(BF16)
    for blk in blocks:
        down = jnp.dot(blk["h"], wd_ref[...], preferred_element_type=F32)
        if want_q:
            blk["qn"] = (_rms(down[:, :MLA_Q_RANK]) * gq_ref[...]).astype(BF16)
        blk["kvn"] = (_rms(down[:, MLA_Q_RANK:MLA_Q_RANK + MLA_KV_RANK]) * gkv_ref[...]).astype(BF16)
        kr_ref[blk["rows"], :] = rot(down[:, MLA_Q_RANK + MLA_KV_RANK:], blk["rows"]).astype(kr_ref.dtype)
    for blk in blocks:
        rows = blk["rows"]
        kv = jnp.dot(blk["kvn"], wkv_ref[...], preferred_element_type=F32)
        kn_ref[rows, :] = kv[:, :nk].astype(kn_ref.dtype)
        v_ref[rows, :] = kv[:, nk:].astype(v_ref.dtype)
    if want_q:
        for blk in blocks:
            rows = blk["rows"]
            q = jnp.dot(blk["qn"], wq_ref[...], preferred_element_type=F32) * MLA_SCORE_SCALE
            for hd in range(MLA_HEADS):
                lo = hd * MLA_QW
                q_ref[rows, lo:lo + LANE] = q[:, lo:lo + LANE].astype(q_ref.dtype)
                q_ref[rows, lo + LANE:lo + 2 * LANE] = rot(q[:, lo + LANE:lo + 2 * LANE], rows
                                                           ).astype(q_ref.dtype)


def _mla_proj(x, mod, g, wd, gq, gkv, wq, wkv, cos, sin, rope, want_q, tm=1024):
    b, t, d = x.shape
    tm = _pick(t, tm)
    const = lambda bi, i: (0, 0)
    tok = lambda bi, i: (bi, i, 0)
    nk, nv = MLA_HEADS * MLA_NOPE, MLA_HEADS * MLA_V
    out_shape = [jax.ShapeDtypeStruct((b, t, nk), BF16), jax.ShapeDtypeStruct((b, t, LANE), BF16),
                 jax.ShapeDtypeStruct((b, t, nv), BF16)]
    out_specs = [pl.BlockSpec((None, tm, nk), tok), pl.BlockSpec((None, tm, LANE), tok),
                 pl.BlockSpec((None, tm, nv), tok)]
    if want_q:
        out_shape = [jax.ShapeDtypeStruct((b, t, MLA_HEADS * MLA_QW), BF16)] + out_shape
        out_specs = [pl.BlockSpec((None, tm, MLA_HEADS * MLA_QW), tok)] + out_specs
    return pl.pallas_call(
        functools.partial(_mla_proj_kernel, rope=rope, want_q=want_q),
        grid=(b, t // tm),
        in_specs=[pl.BlockSpec((None, tm, d), tok), _mod_spec(mod), pl.BlockSpec((1, d), const),
                  pl.BlockSpec(wd.shape, const), pl.BlockSpec((1, MLA_Q_RANK), const),
                  pl.BlockSpec((1, MLA_KV_RANK), const), pl.BlockSpec(wq.shape, const),
                  pl.BlockSpec(wkv.shape, const),
                  pl.BlockSpec((tm, LANE), lambda bi, i: (i, 0)),
                  pl.BlockSpec((tm, LANE), lambda bi, i: (i, 0))],
        out_specs=out_specs,
        out_shape=out_shape,
        compiler_params=_cparams("parallel", "parallel"),
        name="mla_proj",
    )(x, mod, g.reshape(1, d), wd, gq.reshape(1, -1), gkv.reshape(1, -1), wq, wkv, cos, sin)


def _lane_fold(x, op):
    out = x[:, :LANE]
    for j in range(1, x.shape[1] // LANE):
        out = op(out, x[:, j * LANE:(j + 1) * LANE])
    return out


def _attention_kernel(q_ref, knc_ref, krc_ref, vc_ref, knx_ref, krx_ref, vx_ref, o_ref,
                      kc_ref, kx_ref, vc2_ref, vx2_ref, sc0_ref, sc1_ref, sx0_ref, sx1_ref, m0_ref, m1_ref,
                      *, tk):
    step = pl.program_id(2)

    @pl.when(step == 0)
    def _():
        kc_ref[:, :LANE] = knc_ref[...]
        kc_ref[:, LANE:] = krc_ref[...]
        kx_ref[:, :LANE] = knx_ref[...]
        kx_ref[:, LANE:] = krx_ref[...]
        vc2_ref[:, :LANE] = vc_ref[...]
        vx2_ref[:, :LANE] = vx_ref[...]
        for ref in (vc2_ref, vx2_ref):
            lane = lax.broadcasted_iota(jnp.int32, (ref.shape[0], LANE), 1)
            ref[:, LANE:] = jnp.where(lane == 0, 1.0, 0.0).astype(ref.dtype)

    n_blocks = kx_ref.shape[0] // tk
    n_steps = pl.num_programs(2)
    nt = (((1,), (1,)), ((), ()))

    def body(cur, prev):
        if cur is not None:
            sc_cur, sx_cur, m_cur = cur
            q = q_ref[...]
            s = lax.dot_general(q, kc_ref[...], nt, preferred_element_type=F32)
            sc_cur[...] = s
            m_part = _lane_fold(s, jnp.maximum)
        if prev is not None:
            sc_prev, sx_prev, m_prev = prev
            m = jnp.max(m_prev[...], axis=-1, keepdims=True)
            p = jnp.exp2(sc_prev[...] - m)
            acc = jnp.dot(p.astype(BF16), vc2_ref[...], preferred_element_type=F32)
        for i in range(n_blocks):
            if cur is not None:
                s = lax.dot_general(q, kx_ref[i * tk:(i + 1) * tk, :], nt, preferred_element_type=F32)
                sx_cur[i] = s
                m_part = jnp.maximum(m_part, _lane_fold(s, jnp.maximum))
            if prev is not None:
                p = jnp.exp2(sx_prev[i] - m)
                acc = acc + jnp.dot(p.astype(BF16), vx2_ref[i * tk:(i + 1) * tk, :],
                                    preferred_element_type=F32)
        if cur is not None:
            m_cur[...] = m_part
        if prev is not None:
            o_ref[...] = (acc[:, :MLA_V] / acc[:, MLA_V:MLA_V + 1]).astype(o_ref.dtype)

    bufs = ((sc0_ref, sx0_ref, m0_ref), (sc1_ref, sx1_ref, m1_ref))
    first = step == 0
    last = step == n_steps - 1
    inner = jnp.logical_not(first | last)

    @pl.when(first)
    def _():
        body(bufs[0], None)

    @pl.when(inner & (step % 2 == 0))
    def _():
        body(bufs[0], bufs[1])

    @pl.when(inner & (step % 2 == 1))
    def _():
        body(bufs[1], bufs[0])

    for parity in (0, 1):
        @pl.when(last & (step % 2 == parity))
        def _(parity=parity):
            body(None, bufs[1 - parity])


def _attention(q, kn_c, kr_c, v_c, kn_x, kr_x, v_x, tq=1024, tk=256):
    b, t, _ = q.shape
    tc = kn_c.shape[1]
    tq = _pick(t, tq)
    tk = _pick(t, tk)
    nq = t // tq
    head_c = lambda bi, hi, i: (bi, 0, hi)
    shared = lambda bi, hi, i: (bi, 0, 0)
    return pl.pallas_call(
        functools.partial(_attention_kernel, tk=tk),
        grid=(b, MLA_HEADS, nq + 1),
        in_specs=[pl.BlockSpec((None, tq, MLA_QW), lambda bi, hi, i: (bi, jnp.minimum(i, nq - 1), hi)),
                  pl.BlockSpec((None, tc, LANE), head_c), pl.BlockSpec((None, tc, LANE), shared),
                  pl.BlockSpec((None, tc, LANE), head_c),
                  pl.BlockSpec((None, t, LANE), head_c), pl.BlockSpec((None, t, LANE), shared),
                  pl.BlockSpec((None, t, LANE), head_c)],
        out_specs=pl.BlockSpec((None, tq, MLA_V), lambda bi, hi, i: (bi, jnp.maximum(i - 1, 0), hi)),
        out_shape=jax.ShapeDtypeStruct((b, t, MLA_HEADS * MLA_V), BF16),
        scratch_shapes=[pltpu.VMEM((tc, MLA_QW), BF16), pltpu.VMEM((t, MLA_QW), BF16),
                        pltpu.VMEM((tc, 2 * MLA_V), BF16), pltpu.VMEM((t, 2 * MLA_V), BF16),
                        pltpu.VMEM((tq, tc), F32), pltpu.VMEM((tq, tc), F32),
                        pltpu.VMEM((t // tk, tq, tk), F32), pltpu.VMEM((t // tk, tq, tk), F32),
                        pltpu.VMEM((tq, LANE), F32), pltpu.VMEM((tq, LANE), F32)],
        compiler_params=_cparams("parallel", "parallel", "arbitrary"),
        name="mla_attention",
    )(q, kn_c, kr_c, v_c, kn_x, kr_x, v_x)


def _router_kernel(att_ref, wo_ref, x_ref, mod_ref, g_ref, wr_ref, xo_ref, h_ref, w_ref, e_ref):
    w_hi = wr_ref[...].astype(BF16)
    w_lo = (wr_ref[...] - w_hi.astype(F32)).astype(BF16)
    w_hilo = jnp.concatenate([w_hi, w_lo], axis=1)
    blocks = _row_blocks(x_ref.shape[0])
    for blk in blocks:
        blk["y"] = jnp.dot(att_ref[blk["rows"], :], wo_ref[...], preferred_element_type=F32)
    for blk in blocks:
        rows = blk["rows"]
        x1 = x_ref[rows, :] + mod_ref[2:3, :] * (_rms(blk["y"]) * g_ref[1:2, :])
        xo_ref[rows, :] = x1
        h = _modnorm(x1, g_ref[2:3, :], mod_ref, 3, 4)
        h_ref[rows, :] = _pack_halves(h)
        blk["h_hi"] = h.astype(BF16)
        blk["h_lo"] = (h - blk["h_hi"].astype(F32)).astype(BF16)
    for blk in blocks:
        both = jnp.dot(blk["h_hi"], w_hilo, preferred_element_type=F32)
        blk["logits"] = (both[:, :LANE] + both[:, LANE:]
                         + jnp.dot(blk["h_lo"], w_hi, preferred_element_type=F32))
    for blk in blocks:
        logits = blk["logits"]
        lane = lax.broadcasted_iota(jnp.int32, logits.shape, 1)
        neg = jnp.float32(-jnp.inf)
        lg = jnp.where(lane < N_EXPERTS, logits, neg)
        m1 = jnp.max(lg, axis=-1, keepdims=True)
        i1 = jnp.min(jnp.where(lg == m1, lane, LANE), axis=-1, keepdims=True)
        lg2 = jnp.where(lane == i1, neg, lg)
        m2 = jnp.max(lg2, axis=-1, keepdims=True)
        i2 = jnp.min(jnp.where(lg2 == m2, lane, LANE), axis=-1, keepdims=True)
        e2 = jnp.exp(m2 - m1)
        w1 = 1.0 / (1.0 + e2)
        w2 = e2 / (1.0 + e2)
        w_ref[blk["rows"], :] = jnp.where(lane == 0, w1, jnp.where(lane == 1, w2, 0.0))
        e_ref[blk["rows"], :] = jnp.where(lane == 0, i1, jnp.where(lane == 1, i2, 0))


def _router(att, w_o, x, mod, g, router_pad, tm=1024):
    b, t, d = x.shape
    tm = _pick(t, tm)
    tok = lambda bi, i: (bi, i, 0)
    const = lambda bi, i: (0, 0)
    return pl.pallas_call(
        _router_kernel,
        grid=(b, t // tm),
        in_specs=[pl.BlockSpec((None, tm, att.shape[-1]), tok), pl.BlockSpec(w_o.shape, const),
                  pl.BlockSpec((None, tm, d), tok), _mod_spec(mod), pl.BlockSpec(g.shape, const),
                  pl.BlockSpec((d, LANE), const)],
        out_specs=[pl.BlockSpec((None, tm, d), tok), pl.BlockSpec((None, tm, d // 2), tok),
                   pl.BlockSpec((None, tm, LANE), tok), pl.BlockSpec((None, tm, LANE), tok)],
        out_shape=[jax.ShapeDtypeStruct((b, t, d), F32), jax.ShapeDtypeStruct((b, t, d // 2), jnp.int32),
                   jax.ShapeDtypeStruct((b, t, LANE), F32), jax.ShapeDtypeStruct((b, t, LANE), jnp.int32)],
        compiler_params=_cparams("parallel", "parallel"),
        name="moe_router",
    )(att, w_o, x, mod, g, router_pad)


SC_GATHER_ROWS = 64


def _gather_rows(table, idx):
    info = plsc.get_sparse_core_info()
    n_workers = info.num_cores * info.num_subcores
    n, width = idx.shape[0], table.shape[1]
    chunk = SC_GATHER_ROWS
    assert n % (n_workers * chunk) == 0
    n_chunks = n // (n_workers * chunk)
    mesh = plsc.VectorSubcoreMesh(core_axis_name="c", subcore_axis_name="s")

    def body(table_hbm, idx_hbm, out_hbm, idx_v, rows_v, sem):
        wid = lax.axis_index("s") * info.num_cores + lax.axis_index("c")
        pltpu.sync_copy(idx_hbm.at[wid], idx_v)

        @pl.loop(0, n_chunks)
        def _(j):
            pltpu.async_copy(table_hbm.at[idx_v.at[j]], rows_v, sem).wait()
            pltpu.sync_copy(rows_v, out_hbm.at[pl.ds((wid * n_chunks + j) * chunk, chunk)])

    return pl.kernel(
        body,
        out_type=jax.ShapeDtypeStruct((n, width), table.dtype),
        mesh=mesh,
        scratch_types=[pltpu.VMEM((n_chunks, chunk), jnp.int32), pltpu.VMEM((chunk, width), table.dtype),
                       pltpu.SemaphoreType.DMA],
        name="sc_gather_rows",
    )(table, idx.reshape(n_workers, n_chunks, chunk))


def _scatter_rows_twice(table, idx0, idx1, n_out):
    info = plsc.get_sparse_core_info()
    n_workers = info.num_cores * info.num_subcores
    n, width = table.shape
    chunk = SC_GATHER_ROWS
    assert n % (n_workers * chunk) == 0
    n_chunks = n // (n_workers * chunk)
    mesh = plsc.VectorSubcoreMesh(core_axis_name="c", subcore_axis_name="s")

    def body(table_hbm, i0_hbm, i1_hbm, out_hbm, i0_v, i1_v, rows_v):
        wid = lax.axis_index("s") * info.num_cores + lax.axis_index("c")
        pltpu.sync_copy(i0_hbm.at[wid], i0_v)
        pltpu.sync_copy(i1_hbm.at[wid], i1_v)

        @pl.loop(0, n_chunks)
        def _(j):
            pltpu.sync_copy(table_hbm.at[pl.ds((wid * n_chunks + j) * chunk, chunk)], rows_v)
            pltpu.sync_copy(rows_v, out_hbm.at[i0_v.at[j]])
            pltpu.sync_copy(rows_v, out_hbm.at[i1_v.at[j]])

    shape3 = (n_workers, n_chunks, chunk)
    return pl.kernel(
        body,
        out_type=jax.ShapeDtypeStruct((n_out, width), table.dtype),
        mesh=mesh,
        scratch_types=[pltpu.VMEM((n_chunks, chunk), jnp.int32), pltpu.VMEM((n_chunks, chunk), jnp.int32),
                       pltpu.VMEM((chunk, width), table.dtype)],
        name="sc_scatter_rows",
    )(table, idx0.reshape(shape3), idx1.reshape(shape3))


def _pack_halves(x):
    k = x.shape[1] // 2
    bits = lambda v: lax.bitcast_convert_type(v.astype(BF16).astype(F32), jnp.uint32)
    word = lax.shift_right_logical(bits(x[:, :k]), jnp.uint32(16)) | bits(x[:, k:])
    return lax.bitcast_convert_type(word, jnp.int32)


def _unpack_halves(w):
    u = lax.bitcast_convert_type(w, jnp.uint32)
    lo = lax.bitcast_convert_type(lax.shift_left(u, jnp.uint32(16)), F32)
    hi = lax.bitcast_convert_type(u & jnp.uint32(0xFFFF0000), F32)
    return jnp.concatenate([lo, hi], axis=-1)


def _expert_ffn_kernel(te_ref, tv_ref, x_ref, wa_ref, wb_ref, wo_ref, o_ref):
    valid = tv_ref[pl.program_id(0)] != 0

    @pl.when(valid)
    def _():
        blocks = _row_blocks(x_ref.shape[0])
        for blk in blocks:
            h = _unpack_halves(x_ref[blk["rows"], :]).astype(BF16)
            a = jnp.dot(h, wa_ref[...], preferred_element_type=F32)
            bb = jnp.dot(h, wb_ref[...], preferred_element_type=F32)
            blk["u"] = (_silu(a) * bb).astype(BF16)
        for blk in blocks:
            o_ref[blk["rows"], :] = _pack_halves(jnp.dot(blk["u"], wo_ref[...], preferred_element_type=F32))

    @pl.when(jnp.logical_not(valid))
    def _():
        o_ref[...] = jnp.zeros_like(o_ref)


def _expert_ffn(xs, tile_expert, tile_valid, w_in, w_out, tm):
    p = xs.shape[0]
    n_exp, fdim, d = w_out.shape
    resident = lambda shape, col: pl.BlockSpec(shape, lambda j, te, tv: (te[j], 0, col),
                                               pipeline_mode=pl.Buffered(1))
    grid_spec = pltpu.PrefetchScalarGridSpec(
        num_scalar_prefetch=2,
        grid=(p // tm,),
        in_specs=[pl.BlockSpec((tm, d // 2), lambda j, te, tv: (j, 0)),
                  resident((None, d, fdim), 0), resident((None, d, fdim), 1), resident((None, fdim, d), 0)],
        out_specs=pl.BlockSpec((tm, d // 2), lambda j, te, tv: (j, 0)))
    return pl.pallas_call(
        _expert_ffn_kernel,
        grid_spec=grid_spec,
        out_shape=jax.ShapeDtypeStruct((p, d // 2), jnp.int32),
        compiler_params=_cparams("parallel"),
        name="moe_expert_ffn",
    )(tile_expert, tile_valid, xs, w_in, w_in, w_out)


def _moe_combine_kernel(x_ref, y0_ref, y1_ref, w_ref, mod_ref, g_ref, o_ref):
    y = w_ref[:, 0:1] * _unpack_halves(y0_ref[...]) + w_ref[:, 1:2] * _unpack_halves(y1_ref[...])
    o_ref[...] = x_ref[...] + mod_ref[5:6, :] * (_rms(y) * g_ref[...])


def _moe_combine(x, y2, w, mod, g, tm=1024):
    b, t, d = x.shape
    tm = _pick(t, tm)
    tok = lambda bi, i: (bi, i, 0)
    return pl.pallas_call(
        _moe_combine_kernel,
        grid=(b, t // tm),
        in_specs=[pl.BlockSpec((None, tm, d), tok),
                  pl.BlockSpec((None, None, tm, d // 2), lambda bi, i: (0, bi, i, 0)),
                  pl.BlockSpec((None, None, tm, d // 2), lambda bi, i: (1, bi, i, 0)),
                  pl.BlockSpec((None, tm, LANE), tok), _mod_spec(mod),
                  pl.BlockSpec((1, d), lambda bi, i: (0, 0))],
        out_specs=pl.BlockSpec((None, tm, d), tok),
        out_shape=jax.ShapeDtypeStruct((b, t, d), F32),
        compiler_params=_cparams("parallel", "parallel"),
        name="moe_combine",
    )(x, y2, y2, w, mod, g.reshape(1, d))


def _moe(x, h, w, e, mod, g_out, w_in, w_out, tm=512):
    b, t, d = x.shape
    n_exp = w_out.shape[0]
    n = b * t
    e_flat = jnp.concatenate([e[..., 0].reshape(n), e[..., 1].reshape(n)])
    onehot = (e_flat[:, None] == jnp.arange(n_exp, dtype=jnp.int32)).astype(jnp.int32)
    csum = jnp.cumsum(onehot, axis=0)
    rank = jnp.sum((csum - onehot) * onehot, axis=-1)
    counts = csum[-1]
    ends = jnp.cumsum((counts + tm - 1) // tm * tm)
    dest = (ends - (counts + tm - 1) // tm * tm)[e_flat] + rank
    p = 2 * n + n_exp * tm
    tile_start = jnp.arange(p // tm, dtype=jnp.int32) * tm
    tile_expert = jnp.minimum(jnp.searchsorted(ends, tile_start, side="right"), n_exp - 1).astype(jnp.int32)
    tile_valid = (tile_start < ends[-1]).astype(jnp.int32)

    xs = _scatter_rows_twice(h.reshape(n, d // 2), dest[:n], dest[n:], p)
    ys = _expert_ffn(xs, tile_expert, tile_valid, w_in, w_out, tm)
    y2 = _gather_rows(ys, dest)
    return _moe_combine(x, y2.reshape(2, b, t, d // 2), w, mod, g_out)


def _axial_angles(n_tokens, dim):
    t = jnp.arange(n_tokens)
    rows = (t // GRID_W).astype(F32)
    cols = (t % GRID_W).astype(F32)
    n_freq = dim // 4
    inv = ROPE_BASE ** (-jnp.arange(n_freq, dtype=F32) / n_freq)
    return jnp.concatenate([rows[:, None] * inv, cols[:, None] * inv], axis=-1)


def _rope_tables(n_tokens, dim):
    ang = _axial_angles(n_tokens, dim)
    cos, sin = jnp.cos(ang), jnp.sin(ang)
    if dim == LANE:
        return jnp.concatenate([cos, cos], -1), jnp.concatenate([-sin, sin], -1)
    z = jnp.zeros_like(cos)
    return jnp.concatenate([cos, z, cos, z], -1), jnp.concatenate([-sin, z, sin, z], -1)


def _spread_rope_cols(w):
    half = MLA_ROPE // 2
    z = jnp.zeros(w.shape[:-1] + (half,), w.dtype)
    return jnp.concatenate([w[..., :half], z, w[..., half:], z], axis=-1)


def kernel(x, c, ctx, c_ctx, mod_w, mod_b, norm_g, mix_in_w, ret_decay_logit, hgrn_lb_logit, mix_out_w,
           ffn_in_w, ffn_out_w, mla_down_w, mla_q_norm_g, mla_kv_norm_g, mla_uq_w, mla_ukv_w, mla_out_w,
           router_w, moe_in_w, moe_out_w):
    batch, seq, d = x.shape
    assert mod_w.shape[0] == 2, "two layers: retention/HGRN2 then MLA/MoE"

    rows = -(-(batch + 1) // 8) * 8
    cond = jnp.zeros((rows, d), F32).at[:batch].set(c).at[batch].set(c_ctx)
    mod = _modulation(cond, mod_w, mod_b)
    mod_x = [mod[l, :batch].reshape(batch, 6, d) for l in range(2)]
    mod_c = [mod[l, batch:batch + 1].reshape(1, 6, d) for l in range(2)]

    g = norm_g[0]
    ret_w = RET_HEADS * HEAD_DIM
    z_lo, z_hi = 5 * ret_w, 7 * ret_w
    w_in = mix_in_w[0].astype(BF16)
    w_main = jnp.concatenate([w_in[:, :z_lo], w_in[:, z_hi:]], axis=1)
    w_z = w_in[:, z_lo:z_hi]
    proj_x = _norm_matmul(x, mod_x[0], g[0], w_main, 0, 1, BF16)
    proj_c = _norm_matmul(ctx, mod_c[0], g[0], w_main, 0, 1, BF16)
    z_x = _norm_matmul(x, mod_x[0], g[0], w_z, 0, 1, F32)
    z_c = _norm_matmul(ctx, mod_c[0], g[0], w_z, 0, 1, F32)
    cos2, sin2 = _rope_tables(seq, HEAD_DIM)
    log_gamma = jax.nn.log_sigmoid(ret_decay_logit[0].astype(F32))
    lb = jnp.cumsum(jax.nn.softmax(hgrn_lb_logit.astype(F32), axis=0), axis=0)[0]
    ret_c, ret_x = _retention(proj_c, proj_x, log_gamma, cos2, sin2)
    hg_c, hg_x = _hgrn(proj_c, proj_x, z_c, z_x, lb, first=4 * RET_HEADS)
    w_out = mix_out_w[0].astype(BF16)
    w_outs = [w_out[:ret_w], w_out[ret_w:]]
    f_in, f_out = ffn_in_w[0].astype(BF16), ffn_out_w[0].astype(BF16)
    x = _mixer_ffn(ret_x, hg_x, *w_outs, x, mod_x[0], g, f_in, f_out)
    ctx = _mixer_ffn(ret_c, hg_c, *w_outs, ctx, mod_c[0], g, f_in, f_out)

    g = norm_g[1]
    dq = MLA_NOPE + MLA_ROPE
    wd = mla_down_w[0]
    wd = jnp.concatenate([wd[:, :MLA_Q_RANK + MLA_KV_RANK],
                          _spread_rope_cols(wd[:, MLA_Q_RANK + MLA_KV_RANK:])], axis=-1).astype(BF16)
    wq = mla_uq_w[0].reshape(MLA_Q_RANK, MLA_HEADS, dq)
    wq = jnp.concatenate([wq[..., :MLA_NOPE], _spread_rope_cols(wq[..., MLA_NOPE:])], axis=-1)
    wq = wq.reshape(MLA_Q_RANK, MLA_HEADS * MLA_QW).astype(BF16)
    wkv = mla_ukv_w[0].reshape(MLA_KV_RANK, MLA_HEADS, MLA_NOPE + MLA_V)
    wkv = jnp.concatenate([wkv[..., :MLA_NOPE].reshape(MLA_KV_RANK, -1),
                           wkv[..., MLA_NOPE:].reshape(MLA_KV_RANK, -1)], axis=-1).astype(BF16)
    cos_m, sin_m = _rope_tables(seq, MLA_ROPE)
    q, kn_x, kr_x, v_x = _mla_proj(x, mod_x[1], g[0], wd, mla_q_norm_g[0], mla_kv_norm_g[0], wq, wkv,
                                   cos_m, sin_m, rope=True, want_q=True)
    tc = ctx.shape[1]
    kn_c, kr_c, v_c = _mla_proj(ctx, mod_c[1], g[0], wd, mla_q_norm_g[0], mla_kv_norm_g[0], wq, wkv,
                                cos_m[:tc], sin_m[:tc], rope=False, want_q=False)
    att = _attention(q, kn_c, kr_c, v_c, kn_x, kr_x, v_x)
    router_pad = jnp.zeros((d, LANE), F32).at[:, :N_EXPERTS].set(router_w[0])
    x, h, w, e = _router(att, mla_out_w[0].astype(BF16), x, mod_x[1], g, router_pad)
    x = _moe(x, h, w, e, mod_x[1], g[3], moe_in_w[0].astype(BF16), moe_out_w[0].astype(BF16))
    return x
```

```python
import functools
import math

import jax
import jax.numpy as jnp
import numpy as np
from jax import lax
from jax.experimental import pallas as pl
from jax.experimental.pallas import tpu as pltpu
from jax.experimental.pallas import tpu_sc as plsc

F32 = jnp.float32
BF16 = jnp.bfloat16

NORM_EPS = 1e-6
ROPE_BASE = 10000.0
GRID_W = 64
LANE = 128

RET_HEADS = 4
HGRN_HEADS = 4
HEAD_DIM = 128
RET_CHUNK = 128
HGRN_CHUNK = 128

MLA_HEADS = 8
MLA_NOPE = 128
MLA_ROPE = 64
MLA_V = 128
MLA_Q_RANK = 384
MLA_KV_RANK = 256
N_EXPERTS = 8

VMEM_LIMIT = 56 * 1024 * 1024


def _cparams(*sem):
    return pltpu.CompilerParams(dimension_semantics=sem, vmem_limit_bytes=VMEM_LIMIT)


def _mm(a, b):
    return jnp.dot(a.astype(BF16), b.astype(BF16), preferred_element_type=F32)


def _mm_nt(a, b):
    return lax.dot_general(a.astype(BF16), b.astype(BF16), (((1,), (1,)), ((), ())),
                           preferred_element_type=F32)


def _mm_tn(a, b):
    return lax.dot_general(a.astype(BF16), b.astype(BF16), (((0,), (0,)), ((), ())),
                           preferred_element_type=F32)


def _rms(x):
    return x * lax.rsqrt(jnp.mean(x * x, axis=-1, keepdims=True) + NORM_EPS)


def _silu(x):
    return x * (1.0 / (1.0 + jnp.exp(-x)))


def _modnorm(x, g, mod_ref, shift_row, scale_row):
    shift = mod_ref[shift_row:shift_row + 1, :]
    scale = mod_ref[scale_row:scale_row + 1, :]
    return _rms(x) * g * (1.0 + scale) + shift


def _mod_spec(mod):
    d = mod.shape[-1]
    if mod.shape[0] == 1:
        return pl.BlockSpec((None, 6, d), lambda b, *_: (0, 0, 0))
    return pl.BlockSpec((None, 6, d), lambda b, *_: (b, 0, 0))


ROW_BLOCK = 256


def _row_blocks(n_rows):
    size = min(ROW_BLOCK, n_rows)
    return [dict(rows=pl.ds(r, size)) for r in range(0, n_rows, size)]


def _pick(n, pref):
    t = min(pref, n)
    while n % t:
        t -= LANE
    return t


def _mod_kernel(c_ref, w_ref, b_ref, o_ref):
    a = _silu(c_ref[...])
    o_ref[...] = jnp.dot(a, w_ref[...], preferred_element_type=F32,
                         precision=lax.Precision.HIGHEST) + b_ref[...]


def _modulation(cond, mod_w, mod_b):
    n_layers, d, n = mod_w.shape
    r = cond.shape[0]
    tn = _pick(n, 1536)
    return pl.pallas_call(
        _mod_kernel,
        grid=(n_layers, n // tn),
        in_specs=[pl.BlockSpec((r, d), lambda l, j: (0, 0)),
                  pl.BlockSpec((None, d, tn), lambda l, j: (l, 0, j)),
                  pl.BlockSpec((None, 1, tn), lambda l, j: (l, 0, j))],
        out_specs=pl.BlockSpec((None, r, tn), lambda l, j: (l, 0, j)),
        out_shape=jax.ShapeDtypeStruct((n_layers, r, n), F32),
        compiler_params=_cparams("parallel", "parallel"),
        name="modulation",
    )(cond, mod_w, mod_b.reshape(n_layers, 1, n))


def _in_proj_kernel(x_ref, mod_ref, g_ref, wm_ref, wz_ref, om_ref, oz_ref):
    blocks = _row_blocks(x_ref.shape[0])
    for blk in blocks:
        blk["h"] = _modnorm(x_ref[blk["rows"], :], g_ref[...], mod_ref, 0, 1).astype(BF16)
    for blk in blocks:
        om_ref[blk["rows"], :] = jnp.dot(blk["h"], wm_ref[...], preferred_element_type=F32).astype(om_ref.dtype)
    for blk in blocks:
        oz_ref[blk["rows"], :] = jnp.dot(blk["h"], wz_ref[...], preferred_element_type=F32)


def _in_proj(x, mod, g, w_main, w_z, tm=1024):
    b, t, d = x.shape
    tm = _pick(t, tm)
    tok = lambda bi, i: (bi, i, 0)
    resident = lambda w: pl.BlockSpec(w.shape, lambda bi, i: (0, 0), pipeline_mode=pl.Buffered(1))
    n_main, n_z = w_main.shape[1], w_z.shape[1]
    return pl.pallas_call(
        _in_proj_kernel,
        grid=(b, t // tm),
        in_specs=[pl.BlockSpec((None, tm, d), tok), _mod_spec(mod), pl.BlockSpec((1, d), lambda bi, i: (0, 0)),
                  resident(w_main), resident(w_z)],
        out_specs=[pl.BlockSpec((None, tm, n_main), tok), pl.BlockSpec((None, tm, n_z), tok)],
        out_shape=[jax.ShapeDtypeStruct((b, t, n_main), BF16), jax.ShapeDtypeStruct((b, t, n_z), F32)],
        compiler_params=_cparams("parallel", "parallel"),
        name="in_proj",
    )(x, mod, g.reshape(1, d), w_main, w_z)


def _mixer_ffn_kernel(ret_ref, hg_ref, w1_ref, w2_ref, x_ref, mod_ref, g_ref, wa_ref, wb_ref, wo_ref, o_ref):
    blocks = _row_blocks(x_ref.shape[0])
    for blk in blocks:
        rows = blk["rows"]
        blk["y"] = (jnp.dot(ret_ref[rows, :], w1_ref[...], preferred_element_type=F32)
                    + jnp.dot(hg_ref[rows, :], w2_ref[...], preferred_element_type=F32))
    for blk in blocks:
        blk["x1"] = x_ref[blk["rows"], :] + mod_ref[2:3, :] * (_rms(blk["y"]) * g_ref[1:2, :])
        blk["h"] = _modnorm(blk["x1"], g_ref[2:3, :], mod_ref, 3, 4).astype(BF16)
    for blk in blocks:
        a = jnp.dot(blk["h"], wa_ref[...], preferred_element_type=F32)
        bb = jnp.dot(blk["h"], wb_ref[...], preferred_element_type=F32)
        blk["u"] = (_silu(a) * bb).astype(BF16)
    for blk in blocks:
        z = jnp.dot(blk["u"], wo_ref[...], preferred_element_type=F32)
        o_ref[blk["rows"], :] = blk["x1"] + mod_ref[5:6, :] * (_rms(z) * g_ref[3:4, :])


def _mixer_ffn(ret, hg, w1, w2, x, mod, g, w_in, w_out, tm=512):
    b, t, d = x.shape
    fdim = w_out.shape[0]
    tm = _pick(t, tm)
    tok = lambda bi, i: (bi, i, 0)
    resident = lambda shape, idx: pl.BlockSpec(shape, lambda bi, i: idx, pipeline_mode=pl.Buffered(1))
    return pl.pallas_call(
        _mixer_ffn_kernel,
        grid=(b, t // tm),
        in_specs=[pl.BlockSpec((None, tm, ret.shape[-1]), tok), pl.BlockSpec((None, tm, hg.shape[-1]), tok),
                  resident(w1.shape, (0, 0)), resident(w2.shape, (0, 0)),
                  pl.BlockSpec((None, tm, d), tok), _mod_spec(mod), pl.BlockSpec(g.shape, lambda bi, i: (0, 0)),
                  resident((d, fdim), (0, 0)), resident((d, fdim), (0, 1)), resident((fdim, d), (0, 0))],
        out_specs=pl.BlockSpec((None, tm, d), tok),
        out_shape=jax.ShapeDtypeStruct((b, t, d), F32),
        compiler_params=_cparams("parallel", "parallel"),
        name="mixer_ffn",
    )(ret, hg, w1, w2, x, mod, g, w_in, w_in, w_out)


SCAN_HEADS = 4


def _head_slice(hh):
    return slice(hh * HEAD_DIM, (hh + 1) * HEAD_DIM)


def _finish_scan(of_ref, ob_ref, g_ref, y_ref, cs):
    def step(i, carry):
        start = pl.multiple_of(i * cs, cs)
        for hh in range(y_ref.shape[1] // HEAD_DIM):
            sl = _head_slice(hh)
            o = of_ref[pl.ds(start, cs), sl].astype(F32) + ob_ref[pl.ds(start, cs), sl].astype(F32)
            gate = g_ref[pl.ds(start, cs), sl].astype(F32)
            y_ref[pl.ds(start, cs), sl] = (_rms(o) * _silu(gate)).astype(y_ref.dtype)
        return carry

    lax.fori_loop(0, y_ref.shape[0] // cs, step, 0)


def _retention_kernel(lg_ref, qc_ref, kc_ref, vc_ref, gc_ref, qx_ref, kx_ref, vx_ref, gx_ref,
                      cos_ref, sin_ref, yc_ref, yx_ref,
                      ofc_ref, obc_ref, ofx_ref, obx_ref, s_ref, intra_ref, qd_ref, kd_ref):
    cs = RET_CHUNK
    hp = SCAN_HEADS
    head0 = pl.program_id(1) * hp
    row = lax.broadcasted_iota(jnp.int32, (cs, cs), 0).astype(F32)
    col = lax.broadcasted_iota(jnp.int32, (cs, cs), 1).astype(F32)
    pos = lax.broadcasted_iota(jnp.int32, (cs, HEAD_DIM), 0).astype(F32)
    k_scale = HEAD_DIM ** -0.5

    for hh in range(hp):
        for d in (0, 1):
            c = 2 * hh + d
            lg = lg_ref[d, head0 + hh]
            rel = (col - row) if d else (row - col)
            intra_ref[c] = jnp.where(rel >= 0, jnp.exp(lg * jnp.maximum(rel, 0.0)), 0.0)
            p = (cs - 1.0 - pos) if d else pos
            qd_ref[c] = jnp.exp(lg * (p + 1.0))
            kd_ref[c] = jnp.exp(lg * (cs - 1.0 - p))
    s_ref[...] = jnp.zeros_like(s_ref)

    for (q_ref, k_ref, v_ref, of_ref, ob_ref, rope) in (
            (qc_ref, kc_ref, vc_ref, ofc_ref, obc_ref, False),
            (qx_ref, kx_ref, vx_ref, ofx_ref, obx_ref, True)):
        n_chunks = q_ref.shape[0] // cs

        def step(i, carry, q_ref=q_ref, k_ref=k_ref, v_ref=v_ref, of_ref=of_ref, ob_ref=ob_ref,
                 rope=rope, n_chunks=n_chunks):
            chains = []
            for d in (0, 1):
                start = pl.multiple_of(((n_chunks - 1 - i) if d else i) * cs, cs)
                if rope:
                    cos = cos_ref[pl.ds(start, cs), :]
                    sin = sin_ref[pl.ds(start, cs), :]
                for hh in range(hp):
                    sl = _head_slice(hh)
                    q = q_ref[pl.ds(start, cs), sl].astype(F32)
                    k = k_ref[pl.ds(start, cs), sl].astype(F32) * k_scale
                    if rope:
                        q = q * cos + pltpu.roll(q, HEAD_DIM // 2, 1) * sin
                        k = k * cos + pltpu.roll(k, HEAD_DIM // 2, 1) * sin
                    chains.append(dict(d=d, hh=hh, c=2 * hh + d, start=start, sl=sl, q=q, k=k,
                                       v=v_ref[pl.ds(start, cs), sl], o_ref=ob_ref if d else of_ref))
            for ch in chains:
                ch["att"] = _mm_nt(ch["q"], ch["k"]) * intra_ref[ch["c"]]
            for ch in chains:
                c = ch["c"]
                ch["state"] = s_ref[c]
                ch["o_ref"][pl.ds(ch["start"], cs), ch["sl"]] = (
                    _mm(ch["att"], ch["v"]) + _mm(ch["q"] * qd_ref[c], ch["state"]))
            for ch in chains:
                c = ch["c"]
                c_dec = jnp.exp(jnp.full((1, HEAD_DIM), lg_ref[ch["d"], head0 + ch["hh"]] * cs, F32))
                s_ref[c] = ch["state"] * c_dec + _mm_tn(ch["k"] * kd_ref[c], ch["v"])
            return carry

        lax.fori_loop(0, n_chunks, step, 0)

    _finish_scan(ofc_ref, obc_ref, gc_ref, yc_ref, cs)
    _finish_scan(ofx_ref, obx_ref, gx_ref, yx_ref, cs)


def _retention(proj_c, proj_x, log_gamma, cos2, sin2):
    b, tc, _ = proj_c.shape
    tx = proj_x.shape[1]
    h, hp = RET_HEADS, SCAN_HEADS
    w = hp * HEAD_DIM

    def col(t, group):
        return pl.BlockSpec((None, t, w), lambda bi, hi, lg: (bi, 0, group * (h // hp) + hi),
                            pipeline_mode=pl.Buffered(1))

    n_chain = 2 * hp
    grid_spec = pltpu.PrefetchScalarGridSpec(
        num_scalar_prefetch=1,
        grid=(b, h // hp),
        in_specs=[col(tc, 0), col(tc, 1), col(tc, 2), col(tc, 3),
                  col(tx, 0), col(tx, 1), col(tx, 2), col(tx, 3),
                  pl.BlockSpec((tx, HEAD_DIM), lambda bi, hi, lg: (0, 0), pipeline_mode=pl.Buffered(1)),
                  pl.BlockSpec((tx, HEAD_DIM), lambda bi, hi, lg: (0, 0), pipeline_mode=pl.Buffered(1))],
        out_specs=[pl.BlockSpec((None, tc, w), lambda bi, hi, lg: (bi, 0, hi)),
                   pl.BlockSpec((None, tx, w), lambda bi, hi, lg: (bi, 0, hi))],
        scratch_shapes=[pltpu.VMEM((tc, w), F32), pltpu.VMEM((tc, w), F32),
                        pltpu.VMEM((tx, w), F32), pltpu.VMEM((tx, w), F32),
                        pltpu.VMEM((n_chain, HEAD_DIM, HEAD_DIM), F32),
                        pltpu.VMEM((n_chain, RET_CHUNK, RET_CHUNK), F32),
                        pltpu.VMEM((n_chain, RET_CHUNK, HEAD_DIM), F32),
                        pltpu.VMEM((n_chain, RET_CHUNK, HEAD_DIM), F32)])
    return pl.pallas_call(
        _retention_kernel,
        grid_spec=grid_spec,
        out_shape=[jax.ShapeDtypeStruct((b, tc, h * HEAD_DIM), BF16),
                   jax.ShapeDtypeStruct((b, tx, h * HEAD_DIM), BF16)],
        compiler_params=_cparams("parallel", "parallel"),
        name="retention",
    )(log_gamma, proj_c, proj_c, proj_c, proj_c, proj_x, proj_x, proj_x, proj_x, cos2, sin2)


def _hgrn_levels(cs):
    return [cs >> (i + 1) for i in range(int(math.log2(cs)))]


def _hgrn_constants(cs):
    t = np.arange(cs)[:, None]
    s = np.arange(cs)[None, :]
    tri = np.stack([(s <= t), (s >= t)]).astype(np.float32)
    masks = []
    for reverse in (False, True):
        lv = []
        for h in _hgrn_levels(cs):
            same = (t // (2 * h)) == (s // (2 * h))
            t_hi = (t % (2 * h)) >= h
            s_hi = (s % (2 * h)) >= h
            lv.append(same & (~t_hi & s_hi if reverse else t_hi & ~s_hi))
        lv.append(t == s)
        masks.append(np.stack(lv))
    return jnp.asarray(tri, BF16), jnp.asarray(np.stack(masks).astype(np.float32), BF16)


def _split3(x):
    hi = x.astype(BF16)
    r = x - hi.astype(F32)
    mid = r.astype(BF16)
    lo = (r - mid.astype(F32)).astype(BF16)
    return hi, mid, lo


def _neg_abs(x):
    bits = lax.bitcast_convert_type(x, jnp.uint32) | jnp.uint32(0x80000000)
    return lax.bitcast_convert_type(bits, F32)


def _level_reference(b, h, reverse):
    cs = b.shape[0]
    ref_local = h if reverse else h - 1
    if 2 * h >= 8:
        b3 = b.reshape(cs // (2 * h), 2 * h, b.shape[1])
        r = b3[:, ref_local:ref_local + 1, :]
        return jnp.broadcast_to(r, b3.shape).reshape(b.shape)
    local = lax.broadcasted_iota(jnp.int32, b.shape, 0) % (2 * h)
    out = b
    for m in range(2 * h):
        if m != ref_local:
            out = jnp.where(local == m, pltpu.roll(b, (m - ref_local) % cs, 0), out)
    return out


def _near_level_weight(f, h, reverse):
    cs = f.shape[0]
    local = lax.broadcasted_iota(jnp.int32, f.shape, 0) % (2 * h)
    before = pltpu.roll(f, 1, 0)
    after = pltpu.roll(f, cs - 1, 0)
    if h == 1:
        return jnp.where(local == (0 if reverse else 1), f, 1.0)
    if reverse:
        return jnp.where(local == 0, f * after, jnp.where(local == 1, f, jnp.where(local == 2, 1.0, before)))
    return jnp.where(local == 0, after, jnp.where(local == 1, 1.0, jnp.where(local == 2, f, f * before)))


def _hgrn_kernel(lb_ref, tri_ref, mask_ref,
                 qc_ref, vc_ref, gc_ref, zfc_ref, zbc_ref,
                 qx_ref, vx_ref, gx_ref, zfx_ref, zbx_ref,
                 yc_ref, yx_ref, ofc_ref, obc_ref, ofx_ref, obx_ref, st_ref):
    cs = HGRN_CHUNK
    hp = SCAN_HEADS
    levels = _hgrn_levels(cs)
    st_ref[...] = jnp.zeros_like(st_ref)

    for (q_ref, v_ref, zf_ref, zb_ref, of_ref, ob_ref) in (
            (qc_ref, vc_ref, zfc_ref, zbc_ref, ofc_ref, obc_ref),
            (qx_ref, vx_ref, zfx_ref, zbx_ref, ofx_ref, obx_ref)):
        n_chunks = q_ref.shape[0] // cs

        def step(i, carry, q_ref=q_ref, v_ref=v_ref, zf_ref=zf_ref, zb_ref=zb_ref, of_ref=of_ref,
                 ob_ref=ob_ref, n_chunks=n_chunks):
            chains = []
            for d in (0, 1):
                start = pl.multiple_of(((n_chunks - 1 - i) if d else i) * cs, cs)
                z_ref = zb_ref if d else zf_ref
                for hh in range(hp):
                    sl = _head_slice(hh)
                    lb = lb_ref[hh]
                    q = _silu(q_ref[pl.ds(start, cs), sl].astype(F32))
                    f = lb + (1.0 - lb) * (1.0 / (1.0 + jnp.exp(-z_ref[pl.ds(start, cs), sl])))
                    chains.append(dict(d=d, c=2 * hh + d, start=start, sl=sl, q=q, k=1.0 - f, f=f,
                                       parts=_split3(jnp.log2(f)), v=v_ref[pl.ds(start, cs), sl],
                                       o_ref=ob_ref if d else of_ref))
            for ch in chains:
                tri = tri_ref[ch["d"]]
                hi, mid, lo = ch["parts"]
                ch["bsum"] = (jnp.dot(tri, hi, preferred_element_type=F32)
                              + jnp.dot(tri, mid, preferred_element_type=F32)
                              + jnp.dot(tri, lo, preferred_element_type=F32))
            for ch in chains:
                ch["qb"], ch["kb"] = ch["q"].astype(BF16), ch["k"].astype(BF16)
                ch["att"] = _mm_nt(ch["qb"], ch["kb"]).astype(BF16) * mask_ref[ch["d"], len(levels)]
            for li, h in enumerate(levels):
                for ch in chains:
                    if h <= 2:
                        w = _near_level_weight(ch["f"], h, ch["d"] == 1).astype(BF16)
                    else:
                        ref = _level_reference(ch["bsum"], h, ch["d"] == 1)
                        w = jnp.exp2(_neg_abs(ch["bsum"] - ref)).astype(BF16)
                    ch["att"] += _mm_nt(ch["qb"] * w, ch["kb"] * w).astype(BF16) * mask_ref[ch["d"], li]
            for ch in chains:
                ch["state"] = st_ref[ch["c"]]
                o = _mm(ch["att"], ch["v"]) + _mm_nt(ch["q"] * jnp.exp2(ch["bsum"]), ch["state"])
                ch["o_ref"][pl.ds(ch["start"], cs), ch["sl"]] = o.astype(ch["o_ref"].dtype)
            for ch in chains:
                bsum = ch["bsum"]
                b_end = bsum[0:1, :] if ch["d"] else bsum[cs - 1:cs, :]
                st_ref[ch["c"]] = (ch["state"] * jnp.exp2(b_end)
                                   + _mm_tn(ch["v"], ch["k"] * jnp.exp2(b_end - bsum)))
            return carry

        lax.fori_loop(0, n_chunks, step, 0)

    _finish_scan(ofc_ref, obc_ref, gc_ref, yc_ref, cs)
    _finish_scan(ofx_ref, obx_ref, gx_ref, yx_ref, cs)


def _hgrn(proj_c, proj_x, z_c, z_x, lb, first):
    b, tc, _ = proj_c.shape
    tx = proj_x.shape[1]
    h, hp = HGRN_HEADS, SCAN_HEADS
    w = hp * HEAD_DIM
    tri, masks = _hgrn_constants(HGRN_CHUNK)

    def col(t, group):
        return pl.BlockSpec((None, t, w), lambda bi, hi: (bi, 0, (first + group * h) // hp + hi),
                            pipeline_mode=pl.Buffered(1))

    def zcol(t, group):
        return pl.BlockSpec((None, t, w), lambda bi, hi: (bi, 0, group * (h // hp) + hi),
                            pipeline_mode=pl.Buffered(1))

    ins = [col(tc, 0), col(tc, 1), col(tc, 2), zcol(tc, 0), zcol(tc, 1),
           col(tx, 0), col(tx, 1), col(tx, 2), zcol(tx, 0), zcol(tx, 1)]
    return pl.pallas_call(
        _hgrn_kernel,
        grid=(b, h // hp),
        in_specs=[pl.BlockSpec((hp, 1, HEAD_DIM), lambda bi, hi: (hi, 0, 0)),
                  pl.BlockSpec(tri.shape, lambda bi, hi: (0, 0, 0)),
                  pl.BlockSpec(masks.shape, lambda bi, hi: (0, 0, 0, 0))] + ins,
        out_specs=[pl.BlockSpec((None, tc, w), lambda bi, hi: (bi, 0, hi)),
                   pl.BlockSpec((None, tx, w), lambda bi, hi: (bi, 0, hi))],
        out_shape=[jax.ShapeDtypeStruct((b, tc, h * HEAD_DIM), BF16),
                   jax.ShapeDtypeStruct((b, tx, h * HEAD_DIM), BF16)],
        scratch_shapes=[pltpu.VMEM((tc, w), BF16), pltpu.VMEM((tc, w), BF16),
                        pltpu.VMEM((tx, w), BF16), pltpu.VMEM((tx, w), BF16),
                        pltpu.VMEM((2 * hp, HEAD_DIM, HEAD_DIM), F32)],
        compiler_params=_cparams("parallel", "parallel"),
        name="hgrn2",
    )(lb.reshape(h, 1, HEAD_DIM), tri, masks, proj_c, proj_c, proj_c, z_c, z_c,
      proj_x, proj_x, proj_x, z_x, z_x)


MLA_QW = 2 * LANE
MLA_SCORE_SCALE = (MLA_NOPE + MLA_ROPE) ** -0.5 * math.log2(math.e)
MLA_DOWN_PAD = MLA_Q_RANK + MLA_KV_RANK + LANE


def _mla_proj_kernel(x_ref, mod_ref, g_ref, wd_ref, gq_ref, gkv_ref, wq_ref, wkv_ref, cos_ref, sin_ref,
                     *out_refs, rope, want_q):
    if want_q:
        q_ref, kn_ref, kr_ref, v_ref = out_refs
    else:
        kn_ref, kr_ref, v_ref = out_refs
    nk = MLA_HEADS * MLA_NOPE

    def rot(v, rows):
        return v * cos_ref[rows, :] + pltpu.roll(v, LANE // 2, 1) * sin_ref[rows, :] if rope else v

    blocks = _row_blocks(x_ref.shape[0])
    for blk in blocks:
        blk["h"] = _modnorm(x_ref[blk["rows"], :], g_ref[...], mod_ref, 0, 1).astype(BF16)
    for blk in blocks:
        down = jnp.dot(blk["h"], wd_ref[...], preferred_element_type=F32)
        if want_q:
            blk["qn"] = (_rms(down[:, :MLA_Q_RANK]) * gq_ref[...]).astype(BF16)
        blk["kvn"] = (_rms(down[:, MLA_Q_RANK:MLA_Q_RANK + MLA_KV_RANK]) * gkv_ref[...]).astype(BF16)
        kr_ref[blk["rows"], :] = rot(down[:, MLA_Q_RANK + MLA_KV_RANK:], blk["rows"]).astype(kr_ref.dtype)
    for blk in blocks:
        rows = blk["rows"]
        kv = jnp.dot(blk["kvn"], wkv_ref[...], preferred_element_type=F32)
        kn_ref[rows, :] = kv[:, :nk].astype(kn_ref.dtype)
        v_ref[rows, :] = kv[:, nk:].astype(v_ref.dtype)
    if want_q:
        for blk in blocks:
            rows = blk["rows"]
            q = jnp.dot(blk["qn"], wq_ref[...], preferred_element_type=F32) * MLA_SCORE_SCALE
            for hd in range(MLA_HEADS):
                lo = hd * MLA_QW
                q_ref[rows, lo:lo + LANE] = q[:, lo:lo + LANE].astype(q_ref.dtype)
                q_ref[rows, lo + LANE:lo + 2 * LANE] = rot(q[:, lo + LANE:lo + 2 * LANE], rows
                                                           ).astype(q_ref.dtype)


def _mla_proj(x, mod, g, wd, gq, gkv, wq, wkv, cos, sin, rope, want_q, tm=1024):
    b, t, d = x.shape
    tm = _pick(t, tm)
    const = lambda bi, i: (0, 0)
    tok = lambda bi, i: (bi, i, 0)
    nk, nv = MLA_HEADS * MLA_NOPE, MLA_HEADS * MLA_V
    out_shape = [jax.ShapeDtypeStruct((b, t, nk), BF16), jax.ShapeDtypeStruct((b, t, LANE), BF16),
                 jax.ShapeDtypeStruct((b, t, nv), BF16)]
    out_specs = [pl.BlockSpec((None, tm, nk), tok), pl.BlockSpec((None, tm, LANE), tok),
                 pl.BlockSpec((None, tm, nv), tok)]
    if want_q:
        out_shape = [jax.ShapeDtypeStruct((b, t, MLA_HEADS * MLA_QW), BF16)] + out_shape
        out_specs = [pl.BlockSpec((None, tm, MLA_HEADS * MLA_QW), tok)] + out_specs
    return pl.pallas_call(
        functools.partial(_mla_proj_kernel, rope=rope, want_q=want_q),
        grid=(b, t // tm),
        in_specs=[pl.BlockSpec((None, tm, d), tok), _mod_spec(mod), pl.BlockSpec((1, d), const),
                  pl.BlockSpec(wd.shape, const), pl.BlockSpec((1, MLA_Q_RANK), const),
                  pl.BlockSpec((1, MLA_KV_RANK), const), pl.BlockSpec(wq.shape, const),
                  pl.BlockSpec(wkv.shape, const),
                  pl.BlockSpec((tm, LANE), lambda bi, i: (i, 0)),
                  pl.BlockSpec((tm, LANE), lambda bi, i: (i, 0))],
        out_specs=out_specs,
        out_shape=out_shape,
        compiler_params=_cparams("parallel", "parallel"),
        name="mla_proj",
    )(x, mod, g.reshape(1, d), wd, gq.reshape(1, -1), gkv.reshape(1, -1), wq, wkv, cos, sin)


def _lane_fold(x, op):
    out = x[:, :LANE]
    for j in range(1, x.shape[1] // LANE):
        out = op(out, x[:, j * LANE:(j + 1) * LANE])
    return out


def _attention_kernel(q_ref, knc_ref, krc_ref, vc_ref, knx_ref, krx_ref, vx_ref, o_ref,
                      kc_ref, kx_ref, vc2_ref, vx2_ref, sc0_ref, sc1_ref, sx0_ref, sx1_ref, m0_ref, m1_ref,
                      *, tk):
    step = pl.program_id(2)

    @pl.when(step == 0)
    def _():
        kc_ref[:, :LANE] = knc_ref[...]
        kc_ref[:, LANE:] = krc_ref[...]
        kx_ref[:, :LANE] = knx_ref[...]
        kx_ref[:, LANE:] = krx_ref[...]
        vc2_ref[:, :LANE] = vc_ref[...]
        vx2_ref[:, :LANE] = vx_ref[...]
        for ref in (vc2_ref, vx2_ref):
            lane = lax.broadcasted_iota(jnp.int32, (ref.shape[0], LANE), 1)
            ref[:, LANE:] = jnp.where(lane == 0, 1.0, 0.0).astype(ref.dtype)

    n_blocks = kx_ref.shape[0] // tk
    n_steps = pl.num_programs(2)
    nt = (((1,), (1,)), ((), ()))

    def body(cur, prev):
        if cur is not None:
            sc_cur, sx_cur, m_cur = cur
            q = q_ref[...]
            s = lax.dot_general(q, kc_ref[...], nt, preferred_element_type=F32)
            sc_cur[...] = s
            m_part = _lane_fold(s, jnp.maximum)
        if prev is not None:
            sc_prev, sx_prev, m_prev = prev
            m = jnp.max(m_prev[...], axis=-1, keepdims=True)
            p = jnp.exp2(sc_prev[...] - m)
            acc = jnp.dot(p.astype(BF16), vc2_ref[...], preferred_element_type=F32)
        for i in range(n_blocks):
            if cur is not None:
                s = lax.dot_general(q, kx_ref[i * tk:(i + 1) * tk, :], nt, preferred_element_type=F32)
                sx_cur[i] = s
                m_part = jnp.maximum(m_part, _lane_fold(s, jnp.maximum))
            if prev is not None:
                p = jnp.exp2(sx_prev[i] - m)
                acc = acc + jnp.dot(p.astype(BF16), vx2_ref[i * tk:(i + 1) * tk, :],
                                    preferred_element_type=F32)
        if cur is not None:
            m_cur[...] = m_part
        if prev is not None:
            o_ref[...] = (acc[:, :MLA_V] / acc[:, MLA_V:MLA_V + 1]).astype(o_ref.dtype)

    bufs = ((sc0_ref, sx0_ref, m0_ref), (sc1_ref, sx1_ref, m1_ref))
    first = step == 0
    last = step == n_steps - 1
    inner = jnp.logical_not(first | last)

    @pl.when(first)
    def _():
        body(bufs[0], None)

    @pl.when(inner & (step % 2 == 0))
    def _():
        body(bufs[0], bufs[1])

    @pl.when(inner & (step % 2 == 1))
    def _():
        body(bufs[1], bufs[0])

    for parity in (0, 1):
        @pl.when(last & (step % 2 == parity))
        def _(parity=parity):
            body(None, bufs[1 - parity])


def _attention(q, kn_c, kr_c, v_c, kn_x, kr_x, v_x, tq=1024, tk=256):
    b, t, _ = q.shape
    tc = kn_c.shape[1]
    tq = _pick(t, tq)
    tk = _pick(t, tk)
    nq = t // tq
    head_c = lambda bi, hi, i: (bi, 0, hi)
    shared = lambda bi, hi, i: (bi, 0, 0)
    return pl.pallas_call(
        functools.partial(_attention_kernel, tk=tk),
        grid=(b, MLA_HEADS, nq + 1),
        in_specs=[pl.BlockSpec((None, tq, MLA_QW), lambda bi, hi, i: (bi, jnp.minimum(i, nq - 1), hi)),
                  pl.BlockSpec((None, tc, LANE), head_c), pl.BlockSpec((None, tc, LANE), shared),
                  pl.BlockSpec((None, tc, LANE), head_c),
                  pl.BlockSpec((None, t, LANE), head_c), pl.BlockSpec((None, t, LANE), shared),
                  pl.BlockSpec((None, t, LANE), head_c)],
        out_specs=pl.BlockSpec((None, tq, MLA_V), lambda bi, hi, i: (bi, jnp.maximum(i - 1, 0), hi)),
        out_shape=jax.ShapeDtypeStruct((b, t, MLA_HEADS * MLA_V), BF16),
        scratch_shapes=[pltpu.VMEM((tc, MLA_QW), BF16), pltpu.VMEM((t, MLA_QW), BF16),
                        pltpu.VMEM((tc, 2 * MLA_V), BF16), pltpu.VMEM((t, 2 * MLA_V), BF16),
                        pltpu.VMEM((tq, tc), F32), pltpu.VMEM((tq, tc), F32),
                        pltpu.VMEM((t // tk, tq, tk), F32), pltpu.VMEM((t // tk, tq, tk), F32),
                        pltpu.VMEM((tq, LANE), F32), pltpu.VMEM((tq, LANE), F32)],
        compiler_params=_cparams("parallel", "parallel", "arbitrary"),
        name="mla_attention",
    )(q, kn_c, kr_c, v_c, kn_x, kr_x, v_x)


def _router_kernel(att_ref, wo_ref, x_ref, mod_ref, g_ref, wr_ref, xo_ref, h_ref, w_ref, e_ref):
    w_hi = wr_ref[...].astype(BF16)
    w_lo = (wr_ref[...] - w_hi.astype(F32)).astype(BF16)
    w_hilo = jnp.concatenate([w_hi, w_lo], axis=1)
    blocks = _row_blocks(x_ref.shape[0])
    for blk in blocks:
        blk["y"] = jnp.dot(att_ref[blk["rows"], :], wo_ref[...], preferred_element_type=F32)
    for blk in blocks:
        rows = blk["rows"]
        x1 = x_ref[rows, :] + mod_ref[2:3, :] * (_rms(blk["y"]) * g_ref[1:2, :])
        xo_ref[rows, :] = x1
        h = _modnorm(x1, g_ref[2:3, :], mod_ref, 3, 4)
        h_ref[rows, :] = _pack_halves(h)
        blk["h_hi"] = h.astype(BF16)
        blk["h_lo"] = (h - blk["h_hi"].astype(F32)).astype(BF16)
    for blk in blocks:
        both = jnp.dot(blk["h_hi"], w_hilo, preferred_element_type=F32)
        blk["logits"] = (both[:, :LANE] + both[:, LANE:]
                         + jnp.dot(blk["h_lo"], w_hi, preferred_element_type=F32))
    for blk in blocks:
        logits = blk["logits"]
        lane = lax.broadcasted_iota(jnp.int32, logits.shape, 1)
        neg = jnp.float32(-jnp.inf)
        lg = jnp.where(lane < N_EXPERTS, logits, neg)
        m1 = jnp.max(lg, axis=-1, keepdims=True)
        i1 = jnp.min(jnp.where(lg == m1, lane, LANE), axis=-1, keepdims=True)
        lg2 = jnp.where(lane == i1, neg, lg)
        m2 = jnp.max(lg2, axis=-1, keepdims=True)
        i2 = jnp.min(jnp.where(lg2 == m2, lane, LANE), axis=-1, keepdims=True)
        e2 = jnp.exp(m2 - m1)
        w1 = 1.0 / (1.0 + e2)
        w2 = e2 / (1.0 + e2)
        w_ref[blk["rows"], :] = jnp.where(lane == 0, w1, jnp.where(lane == 1, w2, 0.0))
        e_ref[blk["rows"], :] = jnp.where(lane == 0, i1, jnp.where(lane == 1, i2, 0))


def _router(att, w_o, x, mod, g, router_pad, tm=1024):
    b, t, d = x.shape
    tm = _pick(t, tm)
    tok = lambda bi, i: (bi, i, 0)
    const = lambda bi, i: (0, 0)
    return pl.pallas_call(
        _router_kernel,
        grid=(b, t // tm),
        in_specs=[pl.BlockSpec((None, tm, att.shape[-1]), tok), pl.BlockSpec(w_o.shape, const),
                  pl.BlockSpec((None, tm, d), tok), _mod_spec(mod), pl.BlockSpec(g.shape, const),
                  pl.BlockSpec((d, LANE), const)],
        out_specs=[pl.BlockSpec((None, tm, d), tok), pl.BlockSpec((None, tm, d // 2), tok),
                   pl.BlockSpec((None, tm, LANE), tok), pl.BlockSpec((None, tm, LANE), tok)],
        out_shape=[jax.ShapeDtypeStruct((b, t, d), F32), jax.ShapeDtypeStruct((b, t, d // 2), jnp.int32),
                   jax.ShapeDtypeStruct((b, t, LANE), F32), jax.ShapeDtypeStruct((b, t, LANE), jnp.int32)],
        compiler_params=_cparams("parallel", "parallel"),
        name="moe_router",
    )(att, w_o, x, mod, g, router_pad)


SC_GATHER_ROWS = 64


def _gather_rows(table, idx):
    info = plsc.get_sparse_core_info()
    n_workers = info.num_cores * info.num_subcores
    n, width = idx.shape[0], table.shape[1]
    chunk = SC_GATHER_ROWS
    assert n % (n_workers * chunk) == 0
    n_chunks = n // (n_workers * chunk)
    mesh = plsc.VectorSubcoreMesh(core_axis_name="c", subcore_axis_name="s")

    def body(table_hbm, idx_hbm, out_hbm, idx_v, rows_v, sem):
        wid = lax.axis_index("s") * info.num_cores + lax.axis_index("c")
        pltpu.sync_copy(idx_hbm.at[wid], idx_v)

        @pl.loop(0, n_chunks)
        def _(j):
            pltpu.async_copy(table_hbm.at[idx_v.at[j]], rows_v, sem).wait()
            pltpu.sync_copy(rows_v, out_hbm.at[pl.ds((wid * n_chunks + j) * chunk, chunk)])

    return pl.kernel(
        body,
        out_type=jax.ShapeDtypeStruct((n, width), table.dtype),
        mesh=mesh,
        scratch_types=[pltpu.VMEM((n_chunks, chunk), jnp.int32), pltpu.VMEM((chunk, width), table.dtype),
                       pltpu.SemaphoreType.DMA],
        name="sc_gather_rows",
    )(table, idx.reshape(n_workers, n_chunks, chunk))


def _scatter_rows_twice(table, idx0, idx1, n_out):
    info = plsc.get_sparse_core_info()
    n_workers = info.num_cores * info.num_subcores
    n, width = table.shape
    chunk = SC_GATHER_ROWS
    assert n % (n_workers * chunk) == 0
    n_chunks = n // (n_workers * chunk)
    mesh = plsc.VectorSubcoreMesh(core_axis_name="c", subcore_axis_name="s")

    def body(table_hbm, i0_hbm, i1_hbm, out_hbm, i0_v, i1_v, rows_v):
        wid = lax.axis_index("s") * info.num_cores + lax.axis_index("c")
        pltpu.sync_copy(i0_hbm.at[wid], i0_v)
        pltpu.sync_copy(i1_hbm.at[wid], i1_v)

        @pl.loop(0, n_chunks)
        def _(j):
            pltpu.sync_copy(table_hbm.at[pl.ds((wid * n_chunks + j) * chunk, chunk)], rows_v)
            pltpu.sync_copy(rows_v, out_hbm.at[i0_v.at[j]])
            pltpu.sync_copy(rows_v, out_hbm.at[i1_v.at[j]])

    shape3 = (n_workers, n_chunks, chunk)
    return pl.kernel(
        body,
        out_type=jax.ShapeDtypeStruct((n_out, width), table.dtype),
        mesh=mesh,
        scratch_types=[pltpu.VMEM((n_chunks, chunk), jnp.int32), pltpu.VMEM((n_chunks, chunk), jnp.int32),
                       pltpu.VMEM((chunk, width), table.dtype)],
        name="sc_scatter_rows",
    )(table, idx0.reshape(shape3), idx1.reshape(shape3))


def _pack_halves(x):
    k = x.shape[1] // 2
    bits = lambda v: lax.bitcast_convert_type(v.astype(BF16).astype(F32), jnp.uint32)
    word = lax.shift_right_logical(bits(x[:, :k]), jnp.uint32(16)) | bits(x[:, k:])
    return lax.bitcast_convert_type(word, jnp.int32)


def _unpack_halves(w):
    u = lax.bitcast_convert_type(w, jnp.uint32)
    lo = lax.bitcast_convert_type(lax.shift_left(u, jnp.uint32(16)), F32)
    hi = lax.bitcast_convert_type(u & jnp.uint32(0xFFFF0000), F32)
    return jnp.concatenate([lo, hi], axis=-1)


def _expert_ffn_kernel(te_ref, tv_ref, x_ref, wa_ref, wb_ref, wo_ref, o_ref):
    valid = tv_ref[pl.program_id(0)] != 0

    @pl.when(valid)
    def _():
        blocks = _row_blocks(x_ref.shape[0])
        for blk in blocks:
            h = _unpack_halves(x_ref[blk["rows"], :]).astype(BF16)
            a = jnp.dot(h, wa_ref[...], preferred_element_type=F32)
            bb = jnp.dot(h, wb_ref[...], preferred_element_type=F32)
            blk["u"] = (_silu(a) * bb).astype(BF16)
        for blk in blocks:
            o_ref[blk["rows"], :] = _pack_halves(jnp.dot(blk["u"], wo_ref[...], preferred_element_type=F32))

    @pl.when(jnp.logical_not(valid))
    def _():
        o_ref[...] = jnp.zeros_like(o_ref)


def _expert_ffn(xs, tile_expert, tile_valid, w_in, w_out, tm):
    p = xs.shape[0]
    n_exp, fdim, d = w_out.shape
    resident = lambda shape, col: pl.BlockSpec(shape, lambda j, te, tv: (te[j], 0, col),
                                               pipeline_mode=pl.Buffered(1))
    grid_spec = pltpu.PrefetchScalarGridSpec(
        num_scalar_prefetch=2,
        grid=(p // tm,),
        in_specs=[pl.BlockSpec((tm, d // 2), lambda j, te, tv: (j, 0)),
                  resident((None, d, fdim), 0), resident((None, d, fdim), 1), resident((None, fdim, d), 0)],
        out_specs=pl.BlockSpec((tm, d // 2), lambda j, te, tv: (j, 0)))
    return pl.pallas_call(
        _expert_ffn_kernel,
        grid_spec=grid_spec,
        out_shape=jax.ShapeDtypeStruct((p, d // 2), jnp.int32),
        compiler_params=_cparams("parallel"),
        name="moe_expert_ffn",
    )(tile_expert, tile_valid, xs, w_in, w_in, w_out)


def _moe_combine_kernel(x_ref, y0_ref, y1_ref, w_ref, mod_ref, g_ref, o_ref):
    y = w_ref[:, 0:1] * _unpack_halves(y0_ref[...]) + w_ref[:, 1:2] * _unpack_halves(y1_ref[...])
    o_ref[...] = x_ref[...] + mod_ref[5:6, :] * (_rms(y) * g_ref[...])


def _moe_combine(x, y2, w, mod, g, tm=1024):
    b, t, d = x.shape
    tm = _pick(t, tm)
    tok = lambda bi, i: (bi, i, 0)
    return pl.pallas_call(
        _moe_combine_kernel,
        grid=(b, t // tm),
        in_specs=[pl.BlockSpec((None, tm, d), tok),
                  pl.BlockSpec((None, None, tm, d // 2), lambda bi, i: (0, bi, i, 0)),
                  pl.BlockSpec((None, None, tm, d // 2), lambda bi, i: (1, bi, i, 0)),
                  pl.BlockSpec((None, tm, LANE), tok), _mod_spec(mod),
                  pl.BlockSpec((1, d), lambda bi, i: (0, 0))],
        out_specs=pl.BlockSpec((None, tm, d), tok),
        out_shape=jax.ShapeDtypeStruct((b, t, d), F32),
        compiler_params=_cparams("parallel", "parallel"),
        name="moe_combine",
    )(x, y2, y2, w, mod, g.reshape(1, d))


def _moe(x, h, w, e, mod, g_out, w_in, w_out, tm=512):
    b, t, d = x.shape
    n_exp = w_out.shape[0]
    n = b * t
    e_flat = jnp.concatenate([e[..., 0].reshape(n), e[..., 1].reshape(n)])
    onehot = (e_flat[:, None] == jnp.arange(n_exp, dtype=jnp.int32)).astype(jnp.int32)
    csum = jnp.cumsum(onehot, axis=0)
    rank = jnp.sum((csum - onehot) * onehot, axis=-1)
    counts = csum[-1]
    ends = jnp.cumsum((counts + tm - 1) // tm * tm)
    dest = (ends - (counts + tm - 1) // tm * tm)[e_flat] + rank
    p = 2 * n + n_exp * tm
    tile_start = jnp.arange(p // tm, dtype=jnp.int32) * tm
    tile_expert = jnp.minimum(jnp.searchsorted(ends, tile_start, side="right"), n_exp - 1).astype(jnp.int32)
    tile_valid = (tile_start < ends[-1]).astype(jnp.int32)

    xs = _scatter_rows_twice(h.reshape(n, d // 2), dest[:n], dest[n:], p)
    ys = _expert_ffn(xs, tile_expert, tile_valid, w_in, w_out, tm)
    y2 = _gather_rows(ys, dest)
    return _moe_combine(x, y2.reshape(2, b, t, d // 2), w, mod, g_out)


def _axial_angles(n_tokens, dim):
    t = jnp.arange(n_tokens)
    rows = (t // GRID_W).astype(F32)
    cols = (t % GRID_W).astype(F32)
    n_freq = dim // 4
    inv = ROPE_BASE ** (-jnp.arange(n_freq, dtype=F32) / n_freq)
    return jnp.concatenate([rows[:, None] * inv, cols[:, None] * inv], axis=-1)


def _rope_tables(n_tokens, dim):
    ang = _axial_angles(n_tokens, dim)
    cos, sin = jnp.cos(ang), jnp.sin(ang)
    if dim == LANE:
        return jnp.concatenate([cos, cos], -1), jnp.concatenate([-sin, sin], -1)
    z = jnp.zeros_like(cos)
    return jnp.concatenate([cos, z, cos, z], -1), jnp.concatenate([-sin, z, sin, z], -1)


def _spread_rope_cols(w):
    half = MLA_ROPE // 2
    z = jnp.zeros(w.shape[:-1] + (half,), w.dtype)
    return jnp.concatenate([w[..., :half], z, w[..., half:], z], axis=-1)


def kernel(x, c, ctx, c_ctx, mod_w, mod_b, norm_g, mix_in_w, ret_decay_logit, hgrn_lb_logit, mix_out_w,
           ffn_in_w, ffn_out_w, mla_down_w, mla_q_norm_g, mla_kv_norm_g, mla_uq_w, mla_ukv_w, mla_out_w,
           router_w, moe_in_w, moe_out_w):
    batch, seq, d = x.shape
    assert mod_w.shape[0] == 2, "two layers: retention/HGRN2 then MLA/MoE"

    rows = -(-(batch + 1) // 8) * 8
    cond = jnp.zeros((rows, d), F32).at[:batch].set(c).at[batch].set(c_ctx)
    mod = _modulation(cond, mod_w, mod_b)
    mod_x = [mod[l, :batch].reshape(batch, 6, d) for l in range(2)]
    mod_c = [mod[l, batch:batch + 1].reshape(1, 6, d) for l in range(2)]

    g = norm_g[0]
    ret_w = RET_HEADS * HEAD_DIM
    z_lo, z_hi = 5 * ret_w, 7 * ret_w
    w_in = mix_in_w[0].astype(BF16)
    w_main = jnp.concatenate([w_in[:, :z_lo], w_in[:, z_hi:]], axis=1)
    w_z = w_in[:, z_lo:z_hi]
    proj_x, z_x = _in_proj(x, mod_x[0], g[0], w_main, w_z)
    proj_c, z_c = _in_proj(ctx, mod_c[0], g[0], w_main, w_z)
    cos2, sin2 = _rope_tables(seq, HEAD_DIM)
    log_gamma = jax.nn.log_sigmoid(ret_decay_logit[0].astype(F32))
    lb = jnp.cumsum(jax.nn.softmax(hgrn_lb_logit.astype(F32), axis=0), axis=0)[0]
    ret_c, ret_x = _retention(proj_c, proj_x, log_gamma, cos2, sin2)
    hg_c, hg_x = _hgrn(proj_c, proj_x, z_c, z_x, lb, first=4 * RET_HEADS)
    w_out = mix_out_w[0].astype(BF16)
    w_outs = [w_out[:ret_w], w_out[ret_w:]]
    f_in, f_out = ffn_in_w[0].astype(BF16), ffn_out_w[0].astype(BF16)
    x = _mixer_ffn(ret_x, hg_x, *w_outs, x, mod_x[0], g, f_in, f_out)
    ctx = _mixer_ffn(ret_c, hg_c, *w_outs, ctx, mod_c[0], g, f_in, f_out)

    g = norm_g[1]
    dq = MLA_NOPE + MLA_ROPE
    wd = mla_down_w[0]
    wd = jnp.concatenate([wd[:, :MLA_Q_RANK + MLA_KV_RANK],
                          _spread_rope_cols(wd[:, MLA_Q_RANK + MLA_KV_RANK:])], axis=-1).astype(BF16)
    wq = mla_uq_w[0].reshape(MLA_Q_RANK, MLA_HEADS, dq)
    wq = jnp.concatenate([wq[..., :MLA_NOPE], _spread_rope_cols(wq[..., MLA_NOPE:])], axis=-1)
    wq = wq.reshape(MLA_Q_RANK, MLA_HEADS * MLA_QW).astype(BF16)
    wkv = mla_ukv_w[0].reshape(MLA_KV_RANK, MLA_HEADS, MLA_NOPE + MLA_V)
    wkv = jnp.concatenate([wkv[..., :MLA_NOPE].reshape(MLA_KV_RANK, -1),
                           wkv[..., MLA_NOPE:].reshape(MLA_KV_RANK, -1)], axis=-1).astype(BF16)
    cos_m, sin_m = _rope_tables(seq, MLA_ROPE)
    q, kn_x, kr_x, v_x = _mla_proj(x, mod_x[1], g[0], wd, mla_q_norm_g[0], mla_kv_norm_g[0], wq, wkv,
                                   cos_m, sin_m, rope=True, want_q=True)
    tc = ctx.shape[1]
    kn_c, kr_c, v_c = _mla_proj(ctx, mod_c[1], g[0], wd, mla_q_norm_g[0], mla_kv_norm_g[0], wq, wkv,
                                cos_m[:tc], sin_m[:tc], rope=False, want_q=False)
    att = _attention(q, kn_c, kr_c, v_c, kn_x, kr_x, v_x)
    router_pad = jnp.zeros((d, LANE), F32).at[:, :N_EXPERTS].set(router_w[0])
    x, h, w, e = _router(att, mla_out_w[0].astype(BF16), x, mod_x[1], g, router_pad)
    x = _moe(x, h, w, e, mod_x[1], g[3], moe_in_w[0].astype(BF16), moe_out_w[0].astype(BF16))
    return x
```

```python
import functools
import math

import jax
import jax.numpy as jnp
import numpy as np
from jax import lax
from jax.experimental import pallas as pl
from jax.experimental.pallas import tpu as pltpu
from jax.experimental.pallas import tpu_sc as plsc

F32 = jnp.float32
BF16 = jnp.bfloat16

NORM_EPS = 1e-6
ROPE_BASE = 10000.0
GRID_W = 64
LANE = 128

RET_HEADS = 4
HGRN_HEADS = 4
HEAD_DIM = 128
RET_CHUNK = 128
HGRN_CHUNK = 128

MLA_HEADS = 8
MLA_NOPE = 128
MLA_ROPE = 64
MLA_V = 128
MLA_Q_RANK = 384
MLA_KV_RANK = 256
N_EXPERTS = 8

VMEM_LIMIT = 56 * 1024 * 1024


def _cparams(*sem):
    return pltpu.CompilerParams(dimension_semantics=sem, vmem_limit_bytes=VMEM_LIMIT)


def _mm(a, b):
    return jnp.dot(a.astype(BF16), b.astype(BF16), preferred_element_type=F32)


def _mm_nt(a, b):
    return lax.dot_general(a.astype(BF16), b.astype(BF16), (((1,), (1,)), ((), ())),
                           preferred_element_type=F32)


def _mm_tn(a, b):
    return lax.dot_general(a.astype(BF16), b.astype(BF16), (((0,), (0,)), ((), ())),
                           preferred_element_type=F32)


def _rms(x):
    return x * lax.rsqrt(jnp.mean(x * x, axis=-1, keepdims=True) + NORM_EPS)


def _silu(x):
    return x * (1.0 / (1.0 + jnp.exp(-x)))


def _modnorm(x, g, mod_ref, shift_row, scale_row):
    shift = mod_ref[shift_row:shift_row + 1, :]
    scale = mod_ref[scale_row:scale_row + 1, :]
    return _rms(x) * g * (1.0 + scale) + shift


def _mod_spec(mod):
    d = mod.shape[-1]
    if mod.shape[0] == 1:
        return pl.BlockSpec((None, 6, d), lambda b, *_: (0, 0, 0))
    return pl.BlockSpec((None, 6, d), lambda b, *_: (b, 0, 0))


ROW_BLOCK = 256


def _row_blocks(n_rows):
    size = min(ROW_BLOCK, n_rows)
    return [dict(rows=pl.ds(r, size)) for r in range(0, n_rows, size)]


def _pick(n, pref):
    t = min(pref, n)
    while n % t:
        t -= LANE
    return t


def _mod_kernel(c_ref, w_ref, b_ref, o_ref):
    a = _silu(c_ref[...])
    o_ref[...] = jnp.dot(a, w_ref[...], preferred_element_type=F32,
                         precision=lax.Precision.HIGHEST) + b_ref[...]


def _modulation(cond, mod_w, mod_b):
    n_layers, d, n = mod_w.shape
    r = cond.shape[0]
    tn = _pick(n, 1536)
    return pl.pallas_call(
        _mod_kernel,
        grid=(n_layers, n // tn),
        in_specs=[pl.BlockSpec((r, d), lambda l, j: (0, 0)),
                  pl.BlockSpec((None, d, tn), lambda l, j: (l, 0, j)),
                  pl.BlockSpec((None, 1, tn), lambda l, j: (l, 0, j))],
        out_specs=pl.BlockSpec((None, r, tn), lambda l, j: (l, 0, j)),
        out_shape=jax.ShapeDtypeStruct((n_layers, r, n), F32),
        compiler_params=_cparams("parallel", "parallel"),
        name="modulation",
    )(cond, mod_w, mod_b.reshape(n_layers, 1, n))


def _in_proj_kernel(x_ref, mod_ref, g_ref, wm_ref, wz_ref, om_ref, oz_ref):
    blocks = _row_blocks(x_ref.shape[0])
    for blk in blocks:
        blk["h"] = _modnorm(x_ref[blk["rows"], :], g_ref[...], mod_ref, 0, 1).astype(BF16)
    for blk in blocks:
        om_ref[blk["rows"], :] = jnp.dot(blk["h"], wm_ref[...], preferred_element_type=F32).astype(om_ref.dtype)
    for blk in blocks:
        oz_ref[blk["rows"], :] = jnp.dot(blk["h"], wz_ref[...], preferred_element_type=F32)


def _in_proj(x, mod, g, w_main, w_z, tm=1024):
    b, t, d = x.shape
    tm = _pick(t, tm)
    tok = lambda bi, i: (bi, i, 0)
    resident = lambda w: pl.BlockSpec(w.shape, lambda bi, i: (0, 0), pipeline_mode=pl.Buffered(1))
    n_main, n_z = w_main.shape[1], w_z.shape[1]
    return pl.pallas_call(
        _in_proj_kernel,
        grid=(b, t // tm),
        in_specs=[pl.BlockSpec((None, tm, d), tok), _mod_spec(mod), pl.BlockSpec((1, d), lambda bi, i: (0, 0)),
                  resident(w_main), resident(w_z)],
        out_specs=[pl.BlockSpec((None, tm, n_main), tok), pl.BlockSpec((None, tm, n_z), tok)],
        out_shape=[jax.ShapeDtypeStruct((b, t, n_main), BF16), jax.ShapeDtypeStruct((b, t, n_z), F32)],
        compiler_params=_cparams("parallel", "parallel"),
        name="in_proj",
    )(x, mod, g.reshape(1, d), w_main, w_z)


def _mixer_ffn_kernel(ret_ref, hg_ref, w1_ref, w2_ref, x_ref, mod_ref, g_ref, wa_ref, wb_ref, wo_ref, o_ref):
    blocks = _row_blocks(x_ref.shape[0])
    for blk in blocks:
        rows = blk["rows"]
        blk["y"] = (jnp.dot(ret_ref[rows, :], w1_ref[...], preferred_element_type=F32)
                    + jnp.dot(hg_ref[rows, :], w2_ref[...], preferred_element_type=F32))
    for blk in blocks:
        blk["x1"] = x_ref[blk["rows"], :] + mod_ref[2:3, :] * (_rms(blk["y"]) * g_ref[1:2, :])
        blk["h"] = _modnorm(blk["x1"], g_ref[2:3, :], mod_ref, 3, 4).astype(BF16)
    for blk in blocks:
        a = jnp.dot(blk["h"], wa_ref[...], preferred_element_type=F32)
        bb = jnp.dot(blk["h"], wb_ref[...], preferred_element_type=F32)
        blk["u"] = (_silu(a) * bb).astype(BF16)
    for blk in blocks:
        z = jnp.dot(blk["u"], wo_ref[...], preferred_element_type=F32)
        o_ref[blk["rows"], :] = blk["x1"] + mod_ref[5:6, :] * (_rms(z) * g_ref[3:4, :])


def _mixer_ffn(ret, hg, w1, w2, x, mod, g, w_in, w_out, tm=1024):
    b, t, d = x.shape
    fdim = w_out.shape[0]
    tm = _pick(t, tm)
    tok = lambda bi, i: (bi, i, 0)
    resident = lambda shape, idx: pl.BlockSpec(shape, lambda bi, i: idx, pipeline_mode=pl.Buffered(1))
    return pl.pallas_call(
        _mixer_ffn_kernel,
        grid=(b, t // tm),
        in_specs=[pl.BlockSpec((None, tm, ret.shape[-1]), tok), pl.BlockSpec((None, tm, hg.shape[-1]), tok),
                  resident(w1.shape, (0, 0)), resident(w2.shape, (0, 0)),
                  pl.BlockSpec((None, tm, d), tok), _mod_spec(mod), pl.BlockSpec(g.shape, lambda bi, i: (0, 0)),
                  resident((d, fdim), (0, 0)), resident((d, fdim), (0, 1)), resident((fdim, d), (0, 0))],
        out_specs=pl.BlockSpec((None, tm, d), tok),
        out_shape=jax.ShapeDtypeStruct((b, t, d), F32),
        compiler_params=_cparams("parallel", "parallel"),
        name="mixer_ffn",
    )(ret, hg, w1, w2, x, mod, g, w_in, w_in, w_out)


SCAN_HEADS = 4


def _head_slice(hh):
    return slice(hh * HEAD_DIM, (hh + 1) * HEAD_DIM)


def _finish_scan(of_ref, ob_ref, g_ref, y_ref, cs):
    def step(i, carry):
        start = pl.multiple_of(i * cs, cs)
        for hh in range(y_ref.shape[1] // HEAD_DIM):
            sl = _head_slice(hh)
            o = of_ref[pl.ds(start, cs), sl].astype(F32) + ob_ref[pl.ds(start, cs), sl].astype(F32)
            gate = g_ref[pl.ds(start, cs), sl].astype(F32)
            y_ref[pl.ds(start, cs), sl] = (_rms(o) * _silu(gate)).astype(y_ref.dtype)
        return carry

    lax.fori_loop(0, y_ref.shape[0] // cs, step, 0)


def _retention_kernel(lg_ref, qc_ref, kc_ref, vc_ref, gc_ref, qx_ref, kx_ref, vx_ref, gx_ref,
                      cos_ref, sin_ref, yc_ref, yx_ref,
                      ofc_ref, obc_ref, ofx_ref, obx_ref, s_ref, intra_ref, qd_ref, kd_ref):
    cs = intra_ref.shape[1]
    hp = SCAN_HEADS
    head0 = pl.program_id(1) * hp
    row = lax.broadcasted_iota(jnp.int32, (cs, cs), 0).astype(F32)
    col = lax.broadcasted_iota(jnp.int32, (cs, cs), 1).astype(F32)
    pos = lax.broadcasted_iota(jnp.int32, (cs, HEAD_DIM), 0).astype(F32)
    k_scale = HEAD_DIM ** -0.5

    for hh in range(hp):
        for d in (0, 1):
            c = 2 * hh + d
            lg = lg_ref[d, head0 + hh]
            rel = (col - row) if d else (row - col)
            intra_ref[c] = jnp.where(rel >= 0, jnp.exp(lg * jnp.maximum(rel, 0.0)), 0.0)
            p = (cs - 1.0 - pos) if d else pos
            qd_ref[c] = jnp.exp(lg * (p + 1.0))
            kd_ref[c] = jnp.exp(lg * (cs - 1.0 - p))
    s_ref[...] = jnp.zeros_like(s_ref)

    for (q_ref, k_ref, v_ref, of_ref, ob_ref, rope) in (
            (qc_ref, kc_ref, vc_ref, ofc_ref, obc_ref, False),
            (qx_ref, kx_ref, vx_ref, ofx_ref, obx_ref, True)):
        n_chunks = q_ref.shape[0] // cs

        def step(i, carry, q_ref=q_ref, k_ref=k_ref, v_ref=v_ref, of_ref=of_ref, ob_ref=ob_ref,
                 rope=rope, n_chunks=n_chunks):
            chains = []
            for d in (0, 1):
                start = pl.multiple_of(((n_chunks - 1 - i) if d else i) * cs, cs)
                if rope:
                    cos = cos_ref[pl.ds(start, cs), :]
                    sin = sin_ref[pl.ds(start, cs), :]
                for hh in range(hp):
                    sl = _head_slice(hh)
                    q = q_ref[pl.ds(start, cs), sl].astype(F32)
                    k = k_ref[pl.ds(start, cs), sl].astype(F32) * k_scale
                    if rope:
                        q = q * cos + pltpu.roll(q, HEAD_DIM // 2, 1) * sin
                        k = k * cos + pltpu.roll(k, HEAD_DIM // 2, 1) * sin
                    chains.append(dict(d=d, hh=hh, c=2 * hh + d, start=start, sl=sl, q=q, k=k,
                                       v=v_ref[pl.ds(start, cs), sl], o_ref=ob_ref if d else of_ref))
            for ch in chains:
                ch["att"] = _mm_nt(ch["q"], ch["k"]) * intra_ref[ch["c"]]
            for ch in chains:
                c = ch["c"]
                ch["state"] = s_ref[c]
                ch["o_ref"][pl.ds(ch["start"], cs), ch["sl"]] = (
                    _mm(ch["att"], ch["v"]) + _mm(ch["q"] * qd_ref[c], ch["state"]))
            for ch in chains:
                c = ch["c"]
                c_dec = jnp.exp(jnp.full((1, HEAD_DIM), lg_ref[ch["d"], head0 + ch["hh"]] * cs, F32))
                s_ref[c] = ch["state"] * c_dec + _mm_tn(ch["k"] * kd_ref[c], ch["v"])
            return carry

        lax.fori_loop(0, n_chunks, step, 0)

    _finish_scan(ofc_ref, obc_ref, gc_ref, yc_ref, cs)
    _finish_scan(ofx_ref, obx_ref, gx_ref, yx_ref, cs)


def _retention(proj_c, proj_x, log_gamma, cos2, sin2):
    b, tc, _ = proj_c.shape
    tx = proj_x.shape[1]
    h, hp = RET_HEADS, SCAN_HEADS
    w = hp * HEAD_DIM

    def col(t, group):
        return pl.BlockSpec((None, t, w), lambda bi, hi, lg: (bi, 0, group * (h // hp) + hi),
                            pipeline_mode=pl.Buffered(1))

    n_chain = 2 * hp
    cs = math.gcd(RET_CHUNK, tc, tx)
    grid_spec = pltpu.PrefetchScalarGridSpec(
        num_scalar_prefetch=1,
        grid=(b, h // hp),
        in_specs=[col(tc, 0), col(tc, 1), col(tc, 2), col(tc, 3),
                  col(tx, 0), col(tx, 1), col(tx, 2), col(tx, 3),
                  pl.BlockSpec((tx, HEAD_DIM), lambda bi, hi, lg: (0, 0), pipeline_mode=pl.Buffered(1)),
                  pl.BlockSpec((tx, HEAD_DIM), lambda bi, hi, lg: (0, 0), pipeline_mode=pl.Buffered(1))],
        out_specs=[pl.BlockSpec((None, tc, w), lambda bi, hi, lg: (bi, 0, hi)),
                   pl.BlockSpec((None, tx, w), lambda bi, hi, lg: (bi, 0, hi))],
        scratch_shapes=[pltpu.VMEM((tc, w), F32), pltpu.VMEM((tc, w), F32),
                        pltpu.VMEM((tx, w), F32), pltpu.VMEM((tx, w), F32),
                        pltpu.VMEM((n_chain, HEAD_DIM, HEAD_DIM), F32),
                        pltpu.VMEM((n_chain, cs, cs), F32),
                        pltpu.VMEM((n_chain, cs, HEAD_DIM), F32),
                        pltpu.VMEM((n_chain, cs, HEAD_DIM), F32)])
    return pl.pallas_call(
        _retention_kernel,
        grid_spec=grid_spec,
        out_shape=[jax.ShapeDtypeStruct((b, tc, h * HEAD_DIM), BF16),
                   jax.ShapeDtypeStruct((b, tx, h * HEAD_DIM), BF16)],
        compiler_params=_cparams("parallel", "parallel"),
        name="retention",
    )(log_gamma, proj_c, proj_c, proj_c, proj_c, proj_x, proj_x, proj_x, proj_x, cos2, sin2)


def _hgrn_levels(cs):
    return [cs >> (i + 1) for i in range(int(math.log2(cs)))]


def _hgrn_constants(cs):
    t = np.arange(cs)[:, None]
    s = np.arange(cs)[None, :]
    tri = np.stack([(s <= t), (s >= t)]).astype(np.float32)
    masks = []
    for reverse in (False, True):
        lv = []
        for h in _hgrn_levels(cs):
            same = (t // (2 * h)) == (s // (2 * h))
            t_hi = (t % (2 * h)) >= h
            s_hi = (s % (2 * h)) >= h
            lv.append(same & (~t_hi & s_hi if reverse else t_hi & ~s_hi))
        lv.append(t == s)
        masks.append(np.stack(lv))
    return jnp.asarray(tri, BF16), jnp.asarray(np.stack(masks).astype(np.float32), BF16)


def _split3(x):
    hi = x.astype(BF16)
    r = x - hi.astype(F32)
    mid = r.astype(BF16)
    lo = (r - mid.astype(F32)).astype(BF16)
    return hi, mid, lo


def _neg_abs(x):
    bits = lax.bitcast_convert_type(x, jnp.uint32) | jnp.uint32(0x80000000)
    return lax.bitcast_convert_type(bits, F32)


def _level_reference(b, h, reverse):
    cs = b.shape[0]
    ref_local = h if reverse else h - 1
    if 2 * h >= 8:
        b3 = b.reshape(cs // (2 * h), 2 * h, b.shape[1])
        r = b3[:, ref_local:ref_local + 1, :]
        return jnp.broadcast_to(r, b3.shape).reshape(b.shape)
    local = lax.broadcasted_iota(jnp.int32, b.shape, 0) % (2 * h)
    out = b
    for m in range(2 * h):
        if m != ref_local:
            out = jnp.where(local == m, pltpu.roll(b, (m - ref_local) % cs, 0), out)
    return out


def _near_level_weight(f, h, reverse):
    cs = f.shape[0]
    local = lax.broadcasted_iota(jnp.int32, f.shape, 0) % (2 * h)
    before = pltpu.roll(f, 1, 0)
    after = pltpu.roll(f, cs - 1, 0)
    if h == 1:
        return jnp.where(local == (0 if reverse else 1), f, 1.0)
    if reverse:
        return jnp.where(local == 0, f * after, jnp.where(local == 1, f, jnp.where(local == 2, 1.0, before)))
    return jnp.where(local == 0, after, jnp.where(local == 1, 1.0, jnp.where(local == 2, f, f * before)))


def _hgrn_kernel(lb_ref, tri_ref, mask_ref,
                 qc_ref, vc_ref, gc_ref, zfc_ref, zbc_ref,
                 qx_ref, vx_ref, gx_ref, zfx_ref, zbx_ref,
                 yc_ref, yx_ref, ofc_ref, obc_ref, ofx_ref, obx_ref, st_ref):
    cs = HGRN_CHUNK
    hp = SCAN_HEADS
    levels = _hgrn_levels(cs)
    st_ref[...] = jnp.zeros_like(st_ref)

    for (q_ref, v_ref, zf_ref, zb_ref, of_ref, ob_ref) in (
            (qc_ref, vc_ref, zfc_ref, zbc_ref, ofc_ref, obc_ref),
            (qx_ref, vx_ref, zfx_ref, zbx_ref, ofx_ref, obx_ref)):
        n_chunks = q_ref.shape[0] // cs

        def step(i, carry, q_ref=q_ref, v_ref=v_ref, zf_ref=zf_ref, zb_ref=zb_ref, of_ref=of_ref,
                 ob_ref=ob_ref, n_chunks=n_chunks):
            chains = []
            for d in (0, 1):
                start = pl.multiple_of(((n_chunks - 1 - i) if d else i) * cs, cs)
                z_ref = zb_ref if d else zf_ref
                for hh in range(hp):
                    sl = _head_slice(hh)
                    lb = lb_ref[hh]
                    q = _silu(q_ref[pl.ds(start, cs), sl].astype(F32))
                    f = lb + (1.0 - lb) * (1.0 / (1.0 + jnp.exp(-z_ref[pl.ds(start, cs), sl])))
                    chains.append(dict(d=d, c=2 * hh + d, start=start, sl=sl, q=q, k=1.0 - f, f=f,
                                       parts=_split3(jnp.log2(f)), v=v_ref[pl.ds(start, cs), sl],
                                       o_ref=ob_ref if d else of_ref))
            for ch in chains:
                tri = tri_ref[ch["d"]]
                hi, mid, lo = ch["parts"]
                ch["bsum"] = (jnp.dot(tri, hi, preferred_element_type=F32)
                              + jnp.dot(tri, mid, preferred_element_type=F32)
                              + jnp.dot(tri, lo, preferred_element_type=F32))
            for ch in chains:
                ch["qb"], ch["kb"] = ch["q"].astype(BF16), ch["k"].astype(BF16)
                ch["att"] = _mm_nt(ch["qb"], ch["kb"]).astype(BF16) * mask_ref[ch["d"], len(levels)]
            for li, h in enumerate(levels):
                for ch in chains:
                    if h <= 2:
                        w = _near_level_weight(ch["f"], h, ch["d"] == 1).astype(BF16)
                    else:
                        ref = _level_reference(ch["bsum"], h, ch["d"] == 1)
                        w = jnp.exp2(_neg_abs(ch["bsum"] - ref)).astype(BF16)
                    ch["att"] += _mm_nt(ch["qb"] * w, ch["kb"] * w).astype(BF16) * mask_ref[ch["d"], li]
            for ch in chains:
                ch["state"] = st_ref[ch["c"]]
                o = _mm(ch["att"], ch["v"]) + _mm_nt(ch["q"] * jnp.exp2(ch["bsum"]), ch["state"])
                ch["o_ref"][pl.ds(ch["start"], cs), ch["sl"]] = o.astype(ch["o_ref"].dtype)
            for ch in chains:
                bsum = ch["bsum"]
                b_end = bsum[0:1, :] if ch["d"] else bsum[cs - 1:cs, :]
                st_ref[ch["c"]] = (ch["state"] * jnp.exp2(b_end)
                                   + _mm_tn(ch["v"], ch["k"] * jnp.exp2(b_end - bsum)))
            return carry

        lax.fori_loop(0, n_chunks, step, 0)

    _finish_scan(ofc_ref, obc_ref, gc_ref, yc_ref, cs)
    _finish_scan(ofx_ref, obx_ref, gx_ref, yx_ref, cs)


def _hgrn(proj_c, proj_x, z_c, z_x, lb, first):
    b, tc, _ = proj_c.shape
    tx = proj_x.shape[1]
    h, hp = HGRN_HEADS, SCAN_HEADS
    w = hp * HEAD_DIM
    tri, masks = _hgrn_constants(HGRN_CHUNK)

    def col(t, group):
        return pl.BlockSpec((None, t, w), lambda bi, hi: (bi, 0, (first + group * h) // hp + hi),
                            pipeline_mode=pl.Buffered(1))

    def zcol(t, group):
        return pl.BlockSpec((None, t, w), lambda bi, hi: (bi, 0, group * (h // hp) + hi),
                            pipeline_mode=pl.Buffered(1))

    ins = [col(tc, 0), col(tc, 1), col(tc, 2), zcol(tc, 0), zcol(tc, 1),
           col(tx, 0), col(tx, 1), col(tx, 2), zcol(tx, 0), zcol(tx, 1)]
    return pl.pallas_call(
        _hgrn_kernel,
        grid=(b, h // hp),
        in_specs=[pl.BlockSpec((hp, 1, HEAD_DIM), lambda bi, hi: (hi, 0, 0)),
                  pl.BlockSpec(tri.shape, lambda bi, hi: (0, 0, 0)),
                  pl.BlockSpec(masks.shape, lambda bi, hi: (0, 0, 0, 0))] + ins,
        out_specs=[pl.BlockSpec((None, tc, w), lambda bi, hi: (bi, 0, hi)),
                   pl.BlockSpec((None, tx, w), lambda bi, hi: (bi, 0, hi))],
        out_shape=[jax.ShapeDtypeStruct((b, tc, h * HEAD_DIM), BF16),
                   jax.ShapeDtypeStruct((b, tx, h * HEAD_DIM), BF16)],
        scratch_shapes=[pltpu.VMEM((tc, w), BF16), pltpu.VMEM((tc, w), BF16),
                        pltpu.VMEM((tx, w), BF16), pltpu.VMEM((tx, w), BF16),
                        pltpu.VMEM((2 * hp, HEAD_DIM, HEAD_DIM), F32)],
        compiler_params=_cparams("parallel", "parallel"),
        name="hgrn2",
    )(lb.reshape(h, 1, HEAD_DIM), tri, masks, proj_c, proj_c, proj_c, z_c, z_c,
      proj_x, proj_x, proj_x, z_x, z_x)


MLA_QW = 2 * LANE
MLA_SCORE_SCALE = (MLA_NOPE + MLA_ROPE) ** -0.5 * math.log2(math.e)
MLA_DOWN_PAD = MLA_Q_RANK + MLA_KV_RANK + LANE


def _mla_proj_kernel(x_ref, mod_ref, g_ref, wd_ref, gq_ref, gkv_ref, wq_ref, wkv_ref, cos_ref, sin_ref,
                     *out_refs, rope, want_q):
    if want_q:
        q_ref, kn_ref, kr_ref, v_ref = out_refs
    else:
        kn_ref, kr_ref, v_ref = out_refs
    nk = MLA_HEADS * MLA_NOPE

    def rot(v, rows):
        return v * cos_ref[rows, :] + pltpu.roll(v, LANE // 2, 1) * sin_ref[rows, :] if rope else v

    blocks = _row_blocks(x_ref.shape[0])
    for blk in blocks:
        blk["h"] = _modnorm(x_ref[blk["rows"], :], g_ref[...], mod_ref, 0, 1).astype(BF16)
    for blk in blocks:
        down = jnp.dot(blk["h"], wd_ref[...], preferred_element_type=F32)
        if want_q:
            blk["qn"] = (_rms(down[:, :MLA_Q_RANK]) * gq_ref[...]).astype(BF16)
        blk["kvn"] = (_rms(down[:, MLA_Q_RANK:MLA_Q_RANK + MLA_KV_RANK]) * gkv_ref[...]).astype(BF16)
        kr_ref[blk["rows"], :] = rot(down[:, MLA_Q_RANK + MLA_KV_RANK:], blk["rows"]).astype(kr_ref.dtype)
    for blk in blocks:
        rows = blk["rows"]
        kv = jnp.dot(blk["kvn"], wkv_ref[...], preferred_element_type=F32)
        kn_ref[rows, :] = kv[:, :nk].astype(kn_ref.dtype)
        v_ref[rows, :] = kv[:, nk:].astype(v_ref.dtype)
    if want_q:
        for blk in blocks:
            rows = blk["rows"]
            q = jnp.dot(blk["qn"], wq_ref[...], preferred_element_type=F32) * MLA_SCORE_SCALE
            for hd in range(MLA_HEADS):
                lo = hd * MLA_QW
                q_ref[rows, lo:lo + LANE] = q[:, lo:lo + LANE].astype(q_ref.dtype)
                q_ref[rows, lo + LANE:lo + 2 * LANE] = rot(q[:, lo + LANE:lo + 2 * LANE], rows
                                                           ).astype(q_ref.dtype)


def _mla_proj(x, mod, g, wd, gq, gkv, wq, wkv, cos, sin, rope, want_q, tm=1024):
    b, t, d = x.shape
    tm = _pick(t, tm)
    const = lambda bi, i: (0, 0)
    tok = lambda bi, i: (bi, i, 0)
    nk, nv = MLA_HEADS * MLA_NOPE, MLA_HEADS * MLA_V
    out_shape = [jax.ShapeDtypeStruct((b, t, nk), BF16), jax.ShapeDtypeStruct((b, t, LANE), BF16),
                 jax.ShapeDtypeStruct((b, t, nv), BF16)]
    out_specs = [pl.BlockSpec((None, tm, nk), tok), pl.BlockSpec((None, tm, LANE), tok),
                 pl.BlockSpec((None, tm, nv), tok)]
    if want_q:
        out_shape = [jax.ShapeDtypeStruct((b, t, MLA_HEADS * MLA_QW), BF16)] + out_shape
        out_specs = [pl.BlockSpec((None, tm, MLA_HEADS * MLA_QW), tok)] + out_specs
    return pl.pallas_call(
        functools.partial(_mla_proj_kernel, rope=rope, want_q=want_q),
        grid=(b, t // tm),
        in_specs=[pl.BlockSpec((None, tm, d), tok), _mod_spec(mod), pl.BlockSpec((1, d), const),
                  pl.BlockSpec(wd.shape, const), pl.BlockSpec((1, MLA_Q_RANK), const),
                  pl.BlockSpec((1, MLA_KV_RANK), const), pl.BlockSpec(wq.shape, const),
                  pl.BlockSpec(wkv.shape, const),
                  pl.BlockSpec((tm, LANE), lambda bi, i: (i, 0)),
                  pl.BlockSpec((tm, LANE), lambda bi, i: (i, 0))],
        out_specs=out_specs,
        out_shape=out_shape,
        compiler_params=_cparams("parallel", "parallel"),
        name="mla_proj",
    )(x, mod, g.reshape(1, d), wd, gq.reshape(1, -1), gkv.reshape(1, -1), wq, wkv, cos, sin)


def _lane_fold(x, op):
    out = x[:, :LANE]
    for j in range(1, x.shape[1] // LANE):
        out = op(out, x[:, j * LANE:(j + 1) * LANE])
    return out


def _attention_kernel(q_ref, knc_ref, krc_ref, vc_ref, knx_ref, krx_ref, vx_ref, o_ref,
                      kc_ref, kx_ref, vc2_ref, vx2_ref, sc0_ref, sc1_ref, sx0_ref, sx1_ref, m0_ref, m1_ref,
                      *, tk):
    step = pl.program_id(2)

    @pl.when(step == 0)
    def _():
        kc_ref[:, :LANE] = knc_ref[...]
        kc_ref[:, LANE:] = krc_ref[...]
        kx_ref[:, :LANE] = knx_ref[...]
        kx_ref[:, LANE:] = krx_ref[...]
        vc2_ref[:, :LANE] = vc_ref[...]
        vx2_ref[:, :LANE] = vx_ref[...]
        for ref in (vc2_ref, vx2_ref):
            lane = lax.broadcasted_iota(jnp.int32, (ref.shape[0], LANE), 1)
            ref[:, LANE:] = jnp.where(lane == 0, 1.0, 0.0).astype(ref.dtype)

    n_blocks = kx_ref.shape[0] // tk
    n_steps = pl.num_programs(2)
    nt = (((1,), (1,)), ((), ()))

    def body(cur, prev):
        if cur is not None:
            sc_cur, sx_cur, m_cur = cur
            q = q_ref[...]
            s = lax.dot_general(q, kc_ref[...], nt, preferred_element_type=F32)
            sc_cur[...] = s
            m_part = _lane_fold(s, jnp.maximum)
        if prev is not None:
            sc_prev, sx_prev, m_prev = prev
            m = jnp.max(m_prev[...], axis=-1, keepdims=True)
            p = jnp.exp2(sc_prev[...] - m)
            acc = jnp.dot(p.astype(BF16), vc2_ref[...], preferred_element_type=F32)
        for i in range(n_blocks):
            if cur is not None:
                s = lax.dot_general(q, kx_ref[i * tk:(i + 1) * tk, :], nt, preferred_element_type=F32)
                sx_cur[i] = s
                m_part = jnp.maximum(m_part, _lane_fold(s, jnp.maximum))
            if prev is not None:
                p = jnp.exp2(sx_prev[i] - m)
                acc = acc + jnp.dot(p.astype(BF16), vx2_ref[i * tk:(i + 1) * tk, :],
                                    preferred_element_type=F32)
        if cur is not None:
            m_cur[...] = m_part
        if prev is not None:
            o_ref[...] = (acc[:, :MLA_V] / acc[:, MLA_V:MLA_V + 1]).astype(o_ref.dtype)

    bufs = ((sc0_ref, sx0_ref, m0_ref), (sc1_ref, sx1_ref, m1_ref))
    first = step == 0
    last = step == n_steps - 1
    inner = jnp.logical_not(first | last)

    @pl.when(first)
    def _():
        body(bufs[0], None)

    @pl.when(inner & (step % 2 == 0))
    def _():
        body(bufs[0], bufs[1])

    @pl.when(inner & (step % 2 == 1))
    def _():
        body(bufs[1], bufs[0])

    for parity in (0, 1):
        @pl.when(last & (step % 2 == parity))
        def _(parity=parity):
            body(None, bufs[1 - parity])


def _attention(q, kn_c, kr_c, v_c, kn_x, kr_x, v_x, tq=1024, tk=256):
    b, t, _ = q.shape
    tc = kn_c.shape[1]
    tq = _pick(t, tq)
    tk = _pick(t, tk)
    nq = t // tq
    head_c = lambda bi, hi, i: (bi, 0, hi)
    shared = lambda bi, hi, i: (bi, 0, 0)
    return pl.pallas_call(
        functools.partial(_attention_kernel, tk=tk),
        grid=(b, MLA_HEADS, nq + 1),
        in_specs=[pl.BlockSpec((None, tq, MLA_QW), lambda bi, hi, i: (bi, jnp.minimum(i, nq - 1), hi)),
                  pl.BlockSpec((None, tc, LANE), head_c), pl.BlockSpec((None, tc, LANE), shared),
                  pl.BlockSpec((None, tc, LANE), head_c),
                  pl.BlockSpec((None, t, LANE), head_c), pl.BlockSpec((None, t, LANE), shared),
                  pl.BlockSpec((None, t, LANE), head_c)],
        out_specs=pl.BlockSpec((None, tq, MLA_V), lambda bi, hi, i: (bi, jnp.maximum(i - 1, 0), hi)),
        out_shape=jax.ShapeDtypeStruct((b, t, MLA_HEADS * MLA_V), BF16),
        scratch_shapes=[pltpu.VMEM((tc, MLA_QW), BF16), pltpu.VMEM((t, MLA_QW), BF16),
                        pltpu.VMEM((tc, 2 * MLA_V), BF16), pltpu.VMEM((t, 2 * MLA_V), BF16),
                        pltpu.VMEM((tq, tc), F32), pltpu.VMEM((tq, tc), F32),
                        pltpu.VMEM((t // tk, tq, tk), F32), pltpu.VMEM((t // tk, tq, tk), F32),
                        pltpu.VMEM((tq, LANE), F32), pltpu.VMEM((tq, LANE), F32)],
        compiler_params=_cparams("parallel", "parallel", "arbitrary"),
        name="mla_attention",
    )(q, kn_c, kr_c, v_c, kn_x, kr_x, v_x)


def _router_kernel(att_ref, wo_ref, x_ref, mod_ref, g_ref, wr_ref, xo_ref, h_ref, w_ref, e_ref):
    w_hi = wr_ref[...].astype(BF16)
    w_lo = (wr_ref[...] - w_hi.astype(F32)).astype(BF16)
    w_hilo = jnp.concatenate([w_hi, w_lo], axis=1)
    blocks = _row_blocks(x_ref.shape[0])
    for blk in blocks:
        blk["y"] = jnp.dot(att_ref[blk["rows"], :], wo_ref[...], preferred_element_type=F32)
    for blk in blocks:
        rows = blk["rows"]
        x1 = x_ref[rows, :] + mod_ref[2:3, :] * (_rms(blk["y"]) * g_ref[1:2, :])
        xo_ref[rows, :] = x1
        h = _modnorm(x1, g_ref[2:3, :], mod_ref, 3, 4)
        h_ref[rows, :] = _pack_halves(h)
        blk["h_hi"] = h.astype(BF16)
        blk["h_lo"] = (h - blk["h_hi"].astype(F32)).astype(BF16)
    for blk in blocks:
        both = jnp.dot(blk["h_hi"], w_hilo, preferred_element_type=F32)
        blk["logits"] = (both[:, :LANE] + both[:, LANE:]
                         + jnp.dot(blk["h_lo"], w_hi, preferred_element_type=F32))
    for blk in blocks:
        logits = blk["logits"]
        lane = lax.broadcasted_iota(jnp.int32, logits.shape, 1)
        neg = jnp.float32(-jnp.inf)
        lg = jnp.where(lane < N_EXPERTS, logits, neg)
        m1 = jnp.max(lg, axis=-1, keepdims=True)
        i1 = jnp.min(jnp.where(lg == m1, lane, LANE), axis=-1, keepdims=True)
        lg2 = jnp.where(lane == i1, neg, lg)
        m2 = jnp.max(lg2, axis=-1, keepdims=True)
        i2 = jnp.min(jnp.where(lg2 == m2, lane, LANE), axis=-1, keepdims=True)
        e2 = jnp.exp(m2 - m1)
        w1 = 1.0 / (1.0 + e2)
        w2 = e2 / (1.0 + e2)
        w_ref[blk["rows"], :] = jnp.where(lane == 0, w1, jnp.where(lane == 1, w2, 0.0))
        e_ref[blk["rows"], :] = jnp.where(lane == 0, i1, jnp.where(lane == 1, i2, 0))


def _router(att, w_o, x, mod, g, router_pad, tm=1024):
    b, t, d = x.shape
    tm = _pick(t, tm)
    tok = lambda bi, i: (bi, i, 0)
    const = lambda bi, i: (0, 0)
    return pl.pallas_call(
        _router_kernel,
        grid=(b, t // tm),
        in_specs=[pl.BlockSpec((None, tm, att.shape[-1]), tok), pl.BlockSpec(w_o.shape, const),
                  pl.BlockSpec((None, tm, d), tok), _mod_spec(mod), pl.BlockSpec(g.shape, const),
                  pl.BlockSpec((d, LANE), const)],
        out_specs=[pl.BlockSpec((None, tm, d), tok), pl.BlockSpec((None, tm, d // 2), tok),
                   pl.BlockSpec((None, tm, LANE), tok), pl.BlockSpec((None, tm, LANE), tok)],
        out_shape=[jax.ShapeDtypeStruct((b, t, d), F32), jax.ShapeDtypeStruct((b, t, d // 2), jnp.int32),
                   jax.ShapeDtypeStruct((b, t, LANE), F32), jax.ShapeDtypeStruct((b, t, LANE), jnp.int32)],
        compiler_params=_cparams("parallel", "parallel"),
        name="moe_router",
    )(att, w_o, x, mod, g, router_pad)


SC_GATHER_ROWS = 64


def _gather_rows(table, idx):
    info = plsc.get_sparse_core_info()
    n_workers = info.num_cores * info.num_subcores
    n, width = idx.shape[0], table.shape[1]
    chunk = SC_GATHER_ROWS
    assert n % (n_workers * chunk) == 0
    n_chunks = n // (n_workers * chunk)
    mesh = plsc.VectorSubcoreMesh(core_axis_name="c", subcore_axis_name="s")

    assert n_chunks % 2 == 0

    def body(table_hbm, idx_hbm, out_hbm, idx_v, rows0, rows1, sem0, sem1):
        wid = lax.axis_index("s") * info.num_cores + lax.axis_index("c")
        pltpu.sync_copy(idx_hbm.at[wid], idx_v)
        bufs = ((rows0, sem0), (rows1, sem1))

        def gather(j, k):
            return pltpu.make_async_copy(table_hbm.at[idx_v.at[j]], bufs[k][0], bufs[k][1])

        gather(0, 0).start()

        @pl.loop(0, n_chunks, step=2)
        def _(j0):
            for k in (0, 1):
                j = j0 + k
                gather(j, k).wait()

                @pl.when(j + 1 < n_chunks)
                def _():
                    gather(j + 1, 1 - k).start()

                pltpu.sync_copy(bufs[k][0], out_hbm.at[pl.ds((wid * n_chunks + j) * chunk, chunk)])

    return pl.kernel(
        body,
        out_type=jax.ShapeDtypeStruct((n, width), table.dtype),
        mesh=mesh,
        scratch_types=[pltpu.VMEM((n_chunks, chunk), jnp.int32), pltpu.VMEM((chunk, width), table.dtype),
                       pltpu.VMEM((chunk, width), table.dtype), pltpu.SemaphoreType.DMA,
                       pltpu.SemaphoreType.DMA],
        name="sc_gather_rows",
    )(table, idx.reshape(n_workers, n_chunks, chunk))


def _scatter_rows_twice(table, idx0, idx1, n_out):
    info = plsc.get_sparse_core_info()
    n_workers = info.num_cores * info.num_subcores
    n, width = table.shape
    chunk = SC_GATHER_ROWS
    assert n % (n_workers * chunk) == 0
    n_chunks = n // (n_workers * chunk)
    mesh = plsc.VectorSubcoreMesh(core_axis_name="c", subcore_axis_name="s")

    def body(table_hbm, i0_hbm, i1_hbm, out_hbm, i0_v, i1_v, rows_v):
        wid = lax.axis_index("s") * info.num_cores + lax.axis_index("c")
        pltpu.sync_copy(i0_hbm.at[wid], i0_v)
        pltpu.sync_copy(i1_hbm.at[wid], i1_v)

        @pl.loop(0, n_chunks)
        def _(j):
            pltpu.sync_copy(table_hbm.at[pl.ds((wid * n_chunks + j) * chunk, chunk)], rows_v)
            pltpu.sync_copy(rows_v, out_hbm.at[i0_v.at[j]])
            pltpu.sync_copy(rows_v, out_hbm.at[i1_v.at[j]])

    shape3 = (n_workers, n_chunks, chunk)
    return pl.kernel(
        body,
        out_type=jax.ShapeDtypeStruct((n_out, width), table.dtype),
        mesh=mesh,
        scratch_types=[pltpu.VMEM((n_chunks, chunk), jnp.int32), pltpu.VMEM((n_chunks, chunk), jnp.int32),
                       pltpu.VMEM((chunk, width), table.dtype)],
        name="sc_scatter_rows",
    )(table, idx0.reshape(shape3), idx1.reshape(shape3))


def _pack_halves(x):
    k = x.shape[1] // 2
    bits = lambda v: lax.bitcast_convert_type(v.astype(BF16).astype(F32), jnp.uint32)
    word = lax.shift_right_logical(bits(x[:, :k]), jnp.uint32(16)) | bits(x[:, k:])
    return lax.bitcast_convert_type(word, jnp.int32)


def _unpack_halves(w):
    u = lax.bitcast_convert_type(w, jnp.uint32)
    lo = lax.bitcast_convert_type(lax.shift_left(u, jnp.uint32(16)), F32)
    hi = lax.bitcast_convert_type(u & jnp.uint32(0xFFFF0000), F32)
    return jnp.concatenate([lo, hi], axis=-1)


def _expert_ffn_kernel(te_ref, tv_ref, x_ref, wa_ref, wb_ref, wo_ref, o_ref):
    valid = tv_ref[pl.program_id(0)] != 0

    @pl.when(valid)
    def _():
        blocks = _row_blocks(x_ref.shape[0])
        for blk in blocks:
            h = _unpack_halves(x_ref[blk["rows"], :]).astype(BF16)
            a = jnp.dot(h, wa_ref[...], preferred_element_type=F32)
            bb = jnp.dot(h, wb_ref[...], preferred_element_type=F32)
            blk["u"] = (_silu(a) * bb).astype(BF16)
        for blk in blocks:
            o_ref[blk["rows"], :] = _pack_halves(jnp.dot(blk["u"], wo_ref[...], preferred_element_type=F32))

    @pl.when(jnp.logical_not(valid))
    def _():
        o_ref[...] = jnp.zeros_like(o_ref)


def _expert_ffn(xs, tile_expert, tile_valid, w_in, w_out, tm):
    p = xs.shape[0]
    n_exp, fdim, d = w_out.shape
    resident = lambda shape, col: pl.BlockSpec(shape, lambda j, te, tv: (te[j], 0, col))
    grid_spec = pltpu.PrefetchScalarGridSpec(
        num_scalar_prefetch=2,
        grid=(p // tm,),
        in_specs=[pl.BlockSpec((tm, d // 2), lambda j, te, tv: (j, 0)),
                  resident((None, d, fdim), 0), resident((None, d, fdim), 1), resident((None, fdim, d), 0)],
        out_specs=pl.BlockSpec((tm, d // 2), lambda j, te, tv: (j, 0)))
    return pl.pallas_call(
        _expert_ffn_kernel,
        grid_spec=grid_spec,
        out_shape=jax.ShapeDtypeStruct((p, d // 2), jnp.int32),
        compiler_params=_cparams("parallel"),
        name="moe_expert_ffn",
    )(tile_expert, tile_valid, xs, w_in, w_in, w_out)


def _moe_combine_kernel(x_ref, y0_ref, y1_ref, w_ref, mod_ref, g_ref, o_ref):
    y = w_ref[:, 0:1] * _unpack_halves(y0_ref[...]) + w_ref[:, 1:2] * _unpack_halves(y1_ref[...])
    o_ref[...] = x_ref[...] + mod_ref[5:6, :] * (_rms(y) * g_ref[...])


def _moe_combine(x, y2, w, mod, g, tm=1024):
    b, t, d = x.shape
    tm = _pick(t, tm)
    tok = lambda bi, i: (bi, i, 0)
    return pl.pallas_call(
        _moe_combine_kernel,
        grid=(b, t // tm),
        in_specs=[pl.BlockSpec((None, tm, d), tok),
                  pl.BlockSpec((None, None, tm, d // 2), lambda bi, i: (0, bi, i, 0)),
                  pl.BlockSpec((None, None, tm, d // 2), lambda bi, i: (1, bi, i, 0)),
                  pl.BlockSpec((None, tm, LANE), tok), _mod_spec(mod),
                  pl.BlockSpec((1, d), lambda bi, i: (0, 0))],
        out_specs=pl.BlockSpec((None, tm, d), tok),
        out_shape=jax.ShapeDtypeStruct((b, t, d), F32),
        compiler_params=_cparams("parallel", "parallel"),
        name="moe_combine",
    )(x, y2, y2, w, mod, g.reshape(1, d))


def _moe(x, h, w, e, mod, g_out, w_in, w_out, tm=512):
    b, t, d = x.shape
    n_exp = w_out.shape[0]
    n = b * t
    e_flat = jnp.concatenate([e[..., 0].reshape(n), e[..., 1].reshape(n)])
    onehot = (e_flat[:, None] == jnp.arange(n_exp, dtype=jnp.int32)).astype(jnp.int32)
    csum = jnp.cumsum(onehot, axis=0)
    rank = jnp.sum((csum - onehot) * onehot, axis=-1)
    counts = csum[-1]
    ends = jnp.cumsum((counts + tm - 1) // tm * tm)
    dest = (ends - (counts + tm - 1) // tm * tm)[e_flat] + rank
    p = 2 * n + n_exp * tm
    tile_start = jnp.arange(p // tm, dtype=jnp.int32) * tm
    tile_expert = jnp.minimum(jnp.searchsorted(ends, tile_start, side="right"), n_exp - 1).astype(jnp.int32)
    tile_valid = (tile_start < ends[-1]).astype(jnp.int32)

    xs = _scatter_rows_twice(h.reshape(n, d // 2), dest[:n], dest[n:], p)
    ys = _expert_ffn(xs, tile_expert, tile_valid, w_in, w_out, tm)
    y2 = _gather_rows(ys, dest)
    return _moe_combine(x, y2.reshape(2, b, t, d // 2), w, mod, g_out)


def _axial_angles(n_tokens, dim):
    t = jnp.arange(n_tokens)
    rows = (t // GRID_W).astype(F32)
    cols = (t % GRID_W).astype(F32)
    n_freq = dim // 4
    inv = ROPE_BASE ** (-jnp.arange(n_freq, dtype=F32) / n_freq)
    return jnp.concatenate([rows[:, None] * inv, cols[:, None] * inv], axis=-1)


def _rope_tables(n_tokens, dim):
    ang = _axial_angles(n_tokens, dim)
    cos, sin = jnp.cos(ang), jnp.sin(ang)
    if dim == LANE:
        return jnp.concatenate([cos, cos], -1), jnp.concatenate([-sin, sin], -1)
    z = jnp.zeros_like(cos)
    return jnp.concatenate([cos, z, cos, z], -1), jnp.concatenate([-sin, z, sin, z], -1)


def _spread_rope_cols(w):
    half = MLA_ROPE // 2
    z = jnp.zeros(w.shape[:-1] + (half,), w.dtype)
    return jnp.concatenate([w[..., :half], z, w[..., half:], z], axis=-1)


def kernel(x, c, ctx, c_ctx, mod_w, mod_b, norm_g, mix_in_w, ret_decay_logit, hgrn_lb_logit, mix_out_w,
           ffn_in_w, ffn_out_w, mla_down_w, mla_q_norm_g, mla_kv_norm_g, mla_uq_w, mla_ukv_w, mla_out_w,
           router_w, moe_in_w, moe_out_w):
    batch, seq, d = x.shape
    assert mod_w.shape[0] == 2, "two layers: retention/HGRN2 then MLA/MoE"

    rows = -(-(batch + 1) // 8) * 8
    cond = jnp.zeros((rows, d), F32).at[:batch].set(c).at[batch].set(c_ctx)
    mod = _modulation(cond, mod_w, mod_b)
    mod_x = [mod[l, :batch].reshape(batch, 6, d) for l in range(2)]
    mod_c = [mod[l, batch:batch + 1].reshape(1, 6, d) for l in range(2)]

    g = norm_g[0]
    ret_w = RET_HEADS * HEAD_DIM
    z_lo, z_hi = 5 * ret_w, 7 * ret_w
    w_in = mix_in_w[0].astype(BF16)
    w_main = jnp.concatenate([w_in[:, :z_lo], w_in[:, z_hi:]], axis=1)
    w_z = w_in[:, z_lo:z_hi]
    proj_x, z_x = _in_proj(x, mod_x[0], g[0], w_main, w_z)
    proj_c, z_c = _in_proj(ctx, mod_c[0], g[0], w_main, w_z)
    cos2, sin2 = _rope_tables(seq, HEAD_DIM)
    log_gamma = jax.nn.log_sigmoid(ret_decay_logit[0].astype(F32))
    lb = jnp.cumsum(jax.nn.softmax(hgrn_lb_logit.astype(F32), axis=0), axis=0)[0]
    ret_c, ret_x = _retention(proj_c, proj_x, log_gamma, cos2, sin2)
    hg_c, hg_x = _hgrn(proj_c, proj_x, z_c, z_x, lb, first=4 * RET_HEADS)
    w_out = mix_out_w[0].astype(BF16)
    w_outs = [w_out[:ret_w], w_out[ret_w:]]
    f_in, f_out = ffn_in_w[0].astype(BF16), ffn_out_w[0].astype(BF16)
    x = _mixer_ffn(ret_x, hg_x, *w_outs, x, mod_x[0], g, f_in, f_out)
    ctx = _mixer_ffn(ret_c, hg_c, *w_outs, ctx, mod_c[0], g, f_in, f_out)

    g = norm_g[1]
    dq = MLA_NOPE + MLA_ROPE
    wd = mla_down_w[0]
    wd = jnp.concatenate([wd[:, :MLA_Q_RANK + MLA_KV_RANK],
                          _spread_rope_cols(wd[:, MLA_Q_RANK + MLA_KV_RANK:])], axis=-1).astype(BF16)
    wq = mla_uq_w[0].reshape(MLA_Q_RANK, MLA_HEADS, dq)
    wq = jnp.concatenate([wq[..., :MLA_NOPE], _spread_rope_cols(wq[..., MLA_NOPE:])], axis=-1)
    wq = wq.reshape(MLA_Q_RANK, MLA_HEADS * MLA_QW).astype(BF16)
    wkv = mla_ukv_w[0].reshape(MLA_KV_RANK, MLA_HEADS, MLA_NOPE + MLA_V)
    wkv = jnp.concatenate([wkv[..., :MLA_NOPE].reshape(MLA_KV_RANK, -1),
                           wkv[..., MLA_NOPE:].reshape(MLA_KV_RANK, -1)], axis=-1).astype(BF16)
    cos_m, sin_m = _rope_tables(seq, MLA_ROPE)
    q, kn_x, kr_x, v_x = _mla_proj(x, mod_x[1], g[0], wd, mla_q_norm_g[0], mla_kv_norm_g[0], wq, wkv,
                                   cos_m, sin_m, rope=True, want_q=True)
    tc = ctx.shape[1]
    kn_c, kr_c, v_c = _mla_proj(ctx, mod_c[1], g[0], wd, mla_q_norm_g[0], mla_kv_norm_g[0], wq, wkv,
                                cos_m[:tc], sin_m[:tc], rope=False, want_q=False)
    att = _attention(q, kn_c, kr_c, v_c, kn_x, kr_x, v_x)
    router_pad = jnp.zeros((d, LANE), F32).at[:, :N_EXPERTS].set(router_w[0])
    x, h, w, e = _router(att, mla_out_w[0].astype(BF16), x, mod_x[1], g, router_pad)
    x = _moe(x, h, w, e, mod_x[1], g[3], moe_in_w[0].astype(BF16), moe_out_w[0].astype(BF16))
    return x
```

```python
import functools
import math

import jax
import jax.numpy as jnp
import numpy as np
from jax import lax
from jax.experimental import pallas as pl
from jax.experimental.pallas import tpu as pltpu
from jax.experimental.pallas import tpu_sc as plsc

F32 = jnp.float32
BF16 = jnp.bfloat16

NORM_EPS = 1e-6
ROPE_BASE = 10000.0
GRID_W = 64
LANE = 128

RET_HEADS = 4
HGRN_HEADS = 4
HEAD_DIM = 128
RET_CHUNK = 128
HGRN_CHUNK = 128

MLA_HEADS = 8
MLA_NOPE = 128
MLA_ROPE = 64
MLA_V = 128
MLA_Q_RANK = 384
MLA_KV_RANK = 256
N_EXPERTS = 8

VMEM_LIMIT = 56 * 1024 * 1024


def _cparams(*sem):
    return pltpu.CompilerParams(dimension_semantics=sem, vmem_limit_bytes=VMEM_LIMIT)


def _mm(a, b):
    return jnp.dot(a.astype(BF16), b.astype(BF16), preferred_element_type=F32)


def _mm_nt(a, b):
    return lax.dot_general(a.astype(BF16), b.astype(BF16), (((1,), (1,)), ((), ())),
                           preferred_element_type=F32)


def _mm_tn(a, b):
    return lax.dot_general(a.astype(BF16), b.astype(BF16), (((0,), (0,)), ((), ())),
                           preferred_element_type=F32)


def _rms(x):
    return x * lax.rsqrt(jnp.mean(x * x, axis=-1, keepdims=True) + NORM_EPS)


def _silu(x):
    return x * (1.0 / (1.0 + jnp.exp(-x)))


def _modnorm(x, g, mod_ref, shift_row, scale_row):
    shift = mod_ref[shift_row:shift_row + 1, :]
    scale = mod_ref[scale_row:scale_row + 1, :]
    return _rms(x) * g * (1.0 + scale) + shift


def _mod_spec(mod):
    d = mod.shape[-1]
    if mod.shape[0] == 1:
        return pl.BlockSpec((None, 6, d), lambda b, *_: (0, 0, 0))
    return pl.BlockSpec((None, 6, d), lambda b, *_: (b, 0, 0))


ROW_BLOCK = 256


def _row_blocks(n_rows):
    size = min(ROW_BLOCK, n_rows)
    return [dict(rows=pl.ds(r, size)) for r in range(0, n_rows, size)]


def _pick(n, pref):
    t = min(pref, n)
    while n % t:
        t -= LANE
    return t


def _mod_kernel(c_ref, w_ref, b_ref, o_ref):
    a = _silu(c_ref[...])
    o_ref[...] = jnp.dot(a, w_ref[...], preferred_element_type=F32,
                         precision=lax.Precision.HIGHEST) + b_ref[...]


def _modulation(cond, mod_w, mod_b):
    n_layers, d, n = mod_w.shape
    r = cond.shape[0]
    tn = _pick(n, 1536)
    return pl.pallas_call(
        _mod_kernel,
        grid=(n_layers, n // tn),
        in_specs=[pl.BlockSpec((r, d), lambda l, j: (0, 0)),
                  pl.BlockSpec((None, d, tn), lambda l, j: (l, 0, j)),
                  pl.BlockSpec((None, 1, tn), lambda l, j: (l, 0, j))],
        out_specs=pl.BlockSpec((None, r, tn), lambda l, j: (l, 0, j)),
        out_shape=jax.ShapeDtypeStruct((n_layers, r, n), F32),
        compiler_params=_cparams("parallel", "parallel"),
        name="modulation",
    )(cond, mod_w, mod_b.reshape(n_layers, 1, n))


def _in_proj_kernel(x_ref, mod_ref, g_ref, wm_ref, wz_ref, om_ref, oz_ref):
    blocks = _row_blocks(x_ref.shape[0])
    for blk in blocks:
        blk["h"] = _modnorm(x_ref[blk["rows"], :], g_ref[...], mod_ref, 0, 1).astype(BF16)
    for blk in blocks:
        om_ref[blk["rows"], :] = jnp.dot(blk["h"], wm_ref[...], preferred_element_type=F32).astype(om_ref.dtype)
    for blk in blocks:
        oz_ref[blk["rows"], :] = jnp.dot(blk["h"], wz_ref[...], preferred_element_type=F32)


def _in_proj(x, mod, g, w_main, w_z, tm=1024):
    b, t, d = x.shape
    tm = _pick(t, tm)
    tok = lambda bi, i: (bi, i, 0)
    resident = lambda w: pl.BlockSpec(w.shape, lambda bi, i: (0, 0), pipeline_mode=pl.Buffered(1))
    n_main, n_z = w_main.shape[1], w_z.shape[1]
    return pl.pallas_call(
        _in_proj_kernel,
        grid=(b, t // tm),
        in_specs=[pl.BlockSpec((None, tm, d), tok), _mod_spec(mod), pl.BlockSpec((1, d), lambda bi, i: (0, 0)),
                  resident(w_main), resident(w_z)],
        out_specs=[pl.BlockSpec((None, tm, n_main), tok), pl.BlockSpec((None, tm, n_z), tok)],
        out_shape=[jax.ShapeDtypeStruct((b, t, n_main), BF16), jax.ShapeDtypeStruct((b, t, n_z), F32)],
        compiler_params=_cparams("parallel", "parallel"),
        name="in_proj",
    )(x, mod, g.reshape(1, d), w_main, w_z)


def _mixer_ffn_kernel(ret_ref, hg_ref, w1_ref, w2_ref, x_ref, mod_ref, g_ref, wa_ref, wb_ref, wo_ref, o_ref):
    blocks = _row_blocks(x_ref.shape[0])
    for blk in blocks:
        rows = blk["rows"]
        blk["y"] = (jnp.dot(ret_ref[rows, :], w1_ref[...], preferred_element_type=F32)
                    + jnp.dot(hg_ref[rows, :], w2_ref[...], preferred_element_type=F32))
    for blk in blocks:
        blk["x1"] = x_ref[blk["rows"], :] + mod_ref[2:3, :] * (_rms(blk["y"]) * g_ref[1:2, :])
        blk["h"] = _modnorm(blk["x1"], g_ref[2:3, :], mod_ref, 3, 4).astype(BF16)
    for blk in blocks:
        a = jnp.dot(blk["h"], wa_ref[...], preferred_element_type=F32)
        bb = jnp.dot(blk["h"], wb_ref[...], preferred_element_type=F32)
        blk["u"] = (_silu(a) * bb).astype(BF16)
    for blk in blocks:
        z = jnp.dot(blk["u"], wo_ref[...], preferred_element_type=F32)
        o_ref[blk["rows"], :] = blk["x1"] + mod_ref[5:6, :] * (_rms(z) * g_ref[3:4, :])


def _mixer_ffn(ret, hg, w1, w2, x, mod, g, w_in, w_out, tm=1024):
    b, t, d = x.shape
    fdim = w_out.shape[0]
    tm = _pick(t, tm)
    tok = lambda bi, i: (bi, i, 0)
    resident = lambda shape, idx: pl.BlockSpec(shape, lambda bi, i: idx, pipeline_mode=pl.Buffered(1))
    return pl.pallas_call(
        _mixer_ffn_kernel,
        grid=(b, t // tm),
        in_specs=[pl.BlockSpec((None, tm, ret.shape[-1]), tok), pl.BlockSpec((None, tm, hg.shape[-1]), tok),
                  resident(w1.shape, (0, 0)), resident(w2.shape, (0, 0)),
                  pl.BlockSpec((None, tm, d), tok), _mod_spec(mod), pl.BlockSpec(g.shape, lambda bi, i: (0, 0)),
                  resident((d, fdim), (0, 0)), resident((d, fdim), (0, 1)), resident((fdim, d), (0, 0))],
        out_specs=pl.BlockSpec((None, tm, d), tok),
        out_shape=jax.ShapeDtypeStruct((b, t, d), F32),
        compiler_params=_cparams("parallel", "parallel"),
        name="mixer_ffn",
    )(ret, hg, w1, w2, x, mod, g, w_in, w_in, w_out)


SCAN_HEADS = 4
HGRN_SCAN_HEADS = 4


def _head_slice(hh):
    return slice(hh * HEAD_DIM, (hh + 1) * HEAD_DIM)


def _finish_scan(of_ref, ob_ref, g_ref, y_ref, cs):
    def step(i, carry):
        start = pl.multiple_of(i * cs, cs)
        for hh in range(y_ref.shape[1] // HEAD_DIM):
            sl = _head_slice(hh)
            o = of_ref[pl.ds(start, cs), sl].astype(F32) + ob_ref[pl.ds(start, cs), sl].astype(F32)
            gate = g_ref[pl.ds(start, cs), sl].astype(F32)
            y_ref[pl.ds(start, cs), sl] = (_rms(o) * _silu(gate)).astype(y_ref.dtype)
        return carry

    lax.fori_loop(0, y_ref.shape[0] // cs, step, 0)


def _emit_scan_output(ch, o, cs, finish, g_ref, y_ref):
    rows = pl.ds(ch["start"], cs)
    if finish:
        other = ch["other_ref"][rows, ch["sl"]].astype(F32)
        gate = g_ref[rows, ch["sl"]].astype(F32)
        y_ref[rows, ch["sl"]] = (_rms(o + other) * _silu(gate)).astype(y_ref.dtype)
    else:
        ch["o_ref"][rows, ch["sl"]] = o.astype(ch["o_ref"].dtype)


def _run_scan(step, n_chunks, finish_separately):
    if n_chunks % 2 == 0:
        lax.fori_loop(0, n_chunks // 2, functools.partial(step, finish=False), 0)
        lax.fori_loop(n_chunks // 2, n_chunks, functools.partial(step, finish=True), 0)
    else:
        lax.fori_loop(0, n_chunks, functools.partial(step, finish=False), 0)
        finish_separately()


def _retention_kernel(lg_ref, qc_ref, kc_ref, vc_ref, gc_ref, qx_ref, kx_ref, vx_ref, gx_ref,
                      cos_ref, sin_ref, yc_ref, yx_ref,
                      ofc_ref, obc_ref, ofx_ref, obx_ref, s_ref, intra_ref, qd_ref, kd_ref):
    cs = intra_ref.shape[1]
    hp = SCAN_HEADS
    head0 = pl.program_id(1) * hp
    row = lax.broadcasted_iota(jnp.int32, (cs, cs), 0).astype(F32)
    col = lax.broadcasted_iota(jnp.int32, (cs, cs), 1).astype(F32)
    pos = lax.broadcasted_iota(jnp.int32, (cs, HEAD_DIM), 0).astype(F32)
    k_scale = HEAD_DIM ** -0.5

    for hh in range(hp):
        for d in (0, 1):
            c = 2 * hh + d
            lg = lg_ref[d, head0 + hh]
            rel = (col - row) if d else (row - col)
            intra_ref[c] = jnp.where(rel >= 0, jnp.exp(lg * jnp.maximum(rel, 0.0)), 0.0)
            p = (cs - 1.0 - pos) if d else pos
            qd_ref[c] = jnp.exp(lg * (p + 1.0))
            kd_ref[c] = jnp.exp(lg * (cs - 1.0 - p))
    s_ref[...] = jnp.zeros_like(s_ref)

    for (q_ref, k_ref, v_ref, g_ref, y_ref, of_ref, ob_ref, rope) in (
            (qc_ref, kc_ref, vc_ref, gc_ref, yc_ref, ofc_ref, obc_ref, False),
            (qx_ref, kx_ref, vx_ref, gx_ref, yx_ref, ofx_ref, obx_ref, True)):
        n_chunks = q_ref.shape[0] // cs

        def step(i, carry, *, finish, q_ref=q_ref, k_ref=k_ref, v_ref=v_ref, g_ref=g_ref, y_ref=y_ref,
                 of_ref=of_ref, ob_ref=ob_ref, rope=rope, n_chunks=n_chunks):
            chains = []
            for d in (0, 1):
                start = pl.multiple_of(((n_chunks - 1 - i) if d else i) * cs, cs)
                if rope:
                    cos = cos_ref[pl.ds(start, cs), :]
                    sin = sin_ref[pl.ds(start, cs), :]
                for hh in range(hp):
                    sl = _head_slice(hh)
                    q = q_ref[pl.ds(start, cs), sl].astype(F32)
                    k = k_ref[pl.ds(start, cs), sl].astype(F32) * k_scale
                    if rope:
                        q = q * cos + pltpu.roll(q, HEAD_DIM // 2, 1) * sin
                        k = k * cos + pltpu.roll(k, HEAD_DIM // 2, 1) * sin
                    chains.append(dict(d=d, hh=hh, c=2 * hh + d, start=start, sl=sl, q=q, k=k,
                                       v=v_ref[pl.ds(start, cs), sl], o_ref=ob_ref if d else of_ref,
                                       other_ref=of_ref if d else ob_ref))
            for ch in chains:
                ch["att"] = _mm_nt(ch["q"], ch["k"]) * intra_ref[ch["c"]]
            for ch in chains:
                c = ch["c"]
                ch["state"] = s_ref[c]
                ch["o"] = _mm(ch["att"], ch["v"]) + _mm(ch["q"] * qd_ref[c], ch["state"])
            for ch in chains:
                c = ch["c"]
                c_dec = jnp.exp(jnp.full((1, HEAD_DIM), lg_ref[ch["d"], head0 + ch["hh"]] * cs, F32))
                s_ref[c] = ch["state"] * c_dec + _mm_tn(ch["k"] * kd_ref[c], ch["v"])
            for ch in chains:
                _emit_scan_output(ch, ch["o"], cs, finish, g_ref, y_ref)
            return carry

        _run_scan(step, n_chunks, functools.partial(_finish_scan, of_ref, ob_ref, g_ref, y_ref, cs))


def _retention(proj_c, proj_x, log_gamma, cos2, sin2):
    b, tc, _ = proj_c.shape
    tx = proj_x.shape[1]
    h, hp = RET_HEADS, SCAN_HEADS
    w = hp * HEAD_DIM

    def col(t, group):
        return pl.BlockSpec((None, t, w), lambda bi, hi, lg: (bi, 0, group * (h // hp) + hi),
                            pipeline_mode=pl.Buffered(1))

    n_chain = 2 * hp
    cs = math.gcd(RET_CHUNK, tc, tx)
    grid_spec = pltpu.PrefetchScalarGridSpec(
        num_scalar_prefetch=1,
        grid=(b, h // hp),
        in_specs=[col(tc, 0), col(tc, 1), col(tc, 2), col(tc, 3),
                  col(tx, 0), col(tx, 1), col(tx, 2), col(tx, 3),
                  pl.BlockSpec((tx, HEAD_DIM), lambda bi, hi, lg: (0, 0), pipeline_mode=pl.Buffered(1)),
                  pl.BlockSpec((tx, HEAD_DIM), lambda bi, hi, lg: (0, 0), pipeline_mode=pl.Buffered(1))],
        out_specs=[pl.BlockSpec((None, tc, w), lambda bi, hi, lg: (bi, 0, hi)),
                   pl.BlockSpec((None, tx, w), lambda bi, hi, lg: (bi, 0, hi))],
        scratch_shapes=[pltpu.VMEM((tc, w), F32), pltpu.VMEM((tc, w), F32),
                        pltpu.VMEM((tx, w), F32), pltpu.VMEM((tx, w), F32),
                        pltpu.VMEM((n_chain, HEAD_DIM, HEAD_DIM), F32),
                        pltpu.VMEM((n_chain, cs, cs), F32),
                        pltpu.VMEM((n_chain, cs, HEAD_DIM), F32),
                        pltpu.VMEM((n_chain, cs, HEAD_DIM), F32)])
    return pl.pallas_call(
        _retention_kernel,
        grid_spec=grid_spec,
        out_shape=[jax.ShapeDtypeStruct((b, tc, h * HEAD_DIM), BF16),
                   jax.ShapeDtypeStruct((b, tx, h * HEAD_DIM), BF16)],
        compiler_params=_cparams("parallel", "parallel"),
        name="retention",
    )(log_gamma, proj_c, proj_c, proj_c, proj_c, proj_x, proj_x, proj_x, proj_x, cos2, sin2)


def _hgrn_levels(cs):
    return [cs >> (i + 1) for i in range(int(math.log2(cs)))]


def _hgrn_constants(cs):
    t = np.arange(cs)[:, None]
    s = np.arange(cs)[None, :]
    tri = np.stack([(s <= t), (s >= t)]).astype(np.float32)
    masks = []
    for reverse in (False, True):
        lv = []
        for h in _hgrn_levels(cs):
            same = (t // (2 * h)) == (s // (2 * h))
            t_hi = (t % (2 * h)) >= h
            s_hi = (s % (2 * h)) >= h
            lv.append(same & (~t_hi & s_hi if reverse else t_hi & ~s_hi))
        lv.append(t == s)
        masks.append(np.stack(lv))
    return jnp.asarray(tri, BF16), jnp.asarray(np.stack(masks).astype(np.float32), BF16)


def _split3(x):
    hi = x.astype(BF16)
    r = x - hi.astype(F32)
    mid = r.astype(BF16)
    lo = (r - mid.astype(F32)).astype(BF16)
    return hi, mid, lo


def _neg_abs(x):
    bits = lax.bitcast_convert_type(x, jnp.uint32) | jnp.uint32(0x80000000)
    return lax.bitcast_convert_type(bits, F32)


def _level_reference(b, h, reverse):
    cs = b.shape[0]
    ref_local = h if reverse else h - 1
    if 2 * h >= 8:
        b3 = b.reshape(cs // (2 * h), 2 * h, b.shape[1])
        r = b3[:, ref_local:ref_local + 1, :]
        return jnp.broadcast_to(r, b3.shape).reshape(b.shape)
    local = lax.broadcasted_iota(jnp.int32, b.shape, 0) % (2 * h)
    out = b
    for m in range(2 * h):
        if m != ref_local:
            out = jnp.where(local == m, pltpu.roll(b, (m - ref_local) % cs, 0), out)
    return out


def _near_level_weight(f, h, reverse):
    cs = f.shape[0]
    local = lax.broadcasted_iota(jnp.int32, f.shape, 0) % (2 * h)
    before = pltpu.roll(f, 1, 0)
    after = pltpu.roll(f, cs - 1, 0)
    if h == 1:
        return jnp.where(local == (0 if reverse else 1), f, 1.0)
    if reverse:
        return jnp.where(local == 0, f * after, jnp.where(local == 1, f, jnp.where(local == 2, 1.0, before)))
    return jnp.where(local == 0, after, jnp.where(local == 1, 1.0, jnp.where(local == 2, f, f * before)))


def _hgrn_kernel(lb_ref, tri_ref, mask_ref,
                 qc_ref, vc_ref, gc_ref, zfc_ref, zbc_ref,
                 qx_ref, vx_ref, gx_ref, zfx_ref, zbx_ref,
                 yc_ref, yx_ref, ofc_ref, obc_ref, ofx_ref, obx_ref, st_ref):
    cs = HGRN_CHUNK
    hp = HGRN_SCAN_HEADS
    levels = _hgrn_levels(cs)
    st_ref[...] = jnp.zeros_like(st_ref)

    for (q_ref, v_ref, g_ref, y_ref, zf_ref, zb_ref, of_ref, ob_ref) in (
            (qc_ref, vc_ref, gc_ref, yc_ref, zfc_ref, zbc_ref, ofc_ref, obc_ref),
            (qx_ref, vx_ref, gx_ref, yx_ref, zfx_ref, zbx_ref, ofx_ref, obx_ref)):
        n_chunks = q_ref.shape[0] // cs

        def step(i, carry, *, finish, q_ref=q_ref, v_ref=v_ref, g_ref=g_ref, y_ref=y_ref, zf_ref=zf_ref,
                 zb_ref=zb_ref, of_ref=of_ref, ob_ref=ob_ref, n_chunks=n_chunks):
            chains = []
            for d in (0, 1):
                start = pl.multiple_of(((n_chunks - 1 - i) if d else i) * cs, cs)
                z_ref = zb_ref if d else zf_ref
                for hh in range(hp):
                    sl = _head_slice(hh)
                    lb = lb_ref[hh]
                    q = _silu(q_ref[pl.ds(start, cs), sl].astype(F32))
                    f = lb + (1.0 - lb) * (1.0 / (1.0 + jnp.exp(-z_ref[pl.ds(start, cs), sl])))
                    chains.append(dict(d=d, c=2 * hh + d, start=start, sl=sl, q=q, k=1.0 - f, f=f,
                                       parts=_split3(jnp.log2(f)), v=v_ref[pl.ds(start, cs), sl],
                                       o_ref=ob_ref if d else of_ref, other_ref=of_ref if d else ob_ref))
            for ch in chains:
                tri = tri_ref[ch["d"]]
                hi, mid, lo = ch["parts"]
                ch["bsum"] = (jnp.dot(tri, hi, preferred_element_type=F32)
                              + jnp.dot(tri, mid, preferred_element_type=F32)
                              + jnp.dot(tri, lo, preferred_element_type=F32))
            for ch in chains:
                ch["qb"], ch["kb"] = ch["q"].astype(BF16), ch["k"].astype(BF16)
                ch["att"] = _mm_nt(ch["qb"], ch["kb"]).astype(BF16) * mask_ref[ch["d"], len(levels)]
            for li, h in enumerate(levels):
                for ch in chains:
                    if h <= 2:
                        w = _near_level_weight(ch["f"], h, ch["d"] == 1).astype(BF16)
                    else:
                        ref = _level_reference(ch["bsum"], h, ch["d"] == 1)
                        w = jnp.exp2(_neg_abs(ch["bsum"] - ref)).astype(BF16)
                    ch["att"] += _mm_nt(ch["qb"] * w, ch["kb"] * w).astype(BF16) * mask_ref[ch["d"], li]
            for ch in chains:
                ch["state"] = st_ref[ch["c"]]
                ch["o"] = _mm(ch["att"], ch["v"]) + _mm_nt(ch["q"] * jnp.exp2(ch["bsum"]), ch["state"])
            for ch in chains:
                bsum = ch["bsum"]
                b_end = bsum[0:1, :] if ch["d"] else bsum[cs - 1:cs, :]
                st_ref[ch["c"]] = (ch["state"] * jnp.exp2(b_end)
                                   + _mm_tn(ch["v"], ch["k"] * jnp.exp2(b_end - bsum)))
            for ch in chains:
                _emit_scan_output(ch, ch["o"], cs, finish, g_ref, y_ref)
            return carry

        _run_scan(step, n_chunks, functools.partial(_finish_scan, of_ref, ob_ref, g_ref, y_ref, cs))


def _hgrn(proj_c, proj_x, z_c, z_x, lb, first):
    b, tc, _ = proj_c.shape
    tx = proj_x.shape[1]
    h, hp = HGRN_HEADS, HGRN_SCAN_HEADS
    w = hp * HEAD_DIM
    tri, masks = _hgrn_constants(HGRN_CHUNK)

    def col(t, group):
        return pl.BlockSpec((None, t, w), lambda bi, hi: (bi, 0, (first + group * h) // hp + hi),
                            pipeline_mode=pl.Buffered(1))

    def zcol(t, group):
        return pl.BlockSpec((None, t, w), lambda bi, hi: (bi, 0, group * (h // hp) + hi),
                            pipeline_mode=pl.Buffered(1))

    ins = [col(tc, 0), col(tc, 1), col(tc, 2), zcol(tc, 0), zcol(tc, 1),
           col(tx, 0), col(tx, 1), col(tx, 2), zcol(tx, 0), zcol(tx, 1)]
    return pl.pallas_call(
        _hgrn_kernel,
        grid=(b, h // hp),
        in_specs=[pl.BlockSpec((hp, 1, HEAD_DIM), lambda bi, hi: (hi, 0, 0)),
                  pl.BlockSpec(tri.shape, lambda bi, hi: (0, 0, 0)),
                  pl.BlockSpec(masks.shape, lambda bi, hi: (0, 0, 0, 0))] + ins,
        out_specs=[pl.BlockSpec((None, tc, w), lambda bi, hi: (bi, 0, hi)),
                   pl.BlockSpec((None, tx, w), lambda bi, hi: (bi, 0, hi))],
        out_shape=[jax.ShapeDtypeStruct((b, tc, h * HEAD_DIM), BF16),
                   jax.ShapeDtypeStruct((b, tx, h * HEAD_DIM), BF16)],
        scratch_shapes=[pltpu.VMEM((tc, w), BF16), pltpu.VMEM((tc, w), BF16),
                        pltpu.VMEM((tx, w), BF16), pltpu.VMEM((tx, w), BF16),
                        pltpu.VMEM((2 * hp, HEAD_DIM, HEAD_DIM), F32)],
        compiler_params=_cparams("parallel", "parallel"),
        name="hgrn2",
    )(lb.reshape(h, 1, HEAD_DIM), tri, masks, proj_c, proj_c, proj_c, z_c, z_c,
      proj_x, proj_x, proj_x, z_x, z_x)


MLA_QW = 2 * LANE
MLA_SCORE_SCALE = (MLA_NOPE + MLA_ROPE) ** -0.5 * math.log2(math.e)


def _mla_proj_kernel(x_ref, mod_ref, g_ref, wd_ref, gq_ref, gkv_ref, wq_ref, wkv_ref, cos_ref, sin_ref,
                     *out_refs, rope, want_q):
    if want_q:
        q_ref, kn_ref, kr_ref, v_ref = out_refs
    else:
        kn_ref, kr_ref, v_ref = out_refs
    nk = MLA_HEADS * MLA_NOPE

    def rot(v, rows):
        return v * cos_ref[rows, :] + pltpu.roll(v, LANE // 2, 1) * sin_ref[rows, :] if rope else v

    blocks = _row_blocks(x_ref.shape[0])
    for blk in blocks:
        blk["h"] = _modnorm(x_ref[blk["rows"], :], g_ref[...], mod_ref, 0, 1).astype(BF16)
    for blk in blocks:
        down = jnp.dot(blk["h"], wd_ref[...], preferred_element_type=F32)
        if want_q:
            blk["qn"] = (_rms(down[:, :MLA_Q_RANK]) * gq_ref[...]).astype(BF16)
        blk["kvn"] = (_rms(down[:, MLA_Q_RANK:MLA_Q_RANK + MLA_KV_RANK]) * gkv_ref[...]).astype(BF16)
        kr_ref[blk["rows"], :] = rot(down[:, MLA_Q_RANK + MLA_KV_RANK:], blk["rows"]).astype(kr_ref.dtype)
    for blk in blocks:
        rows = blk["rows"]
        kv = jnp.dot(blk["kvn"], wkv_ref[...], preferred_element_type=F32)
        kn_ref[rows, :] = kv[:, :nk].astype(kn_ref.dtype)
        v_ref[rows, :] = kv[:, nk:].astype(v_ref.dtype)
    if want_q:
        for blk in blocks:
            rows = blk["rows"]
            q = jnp.dot(blk["qn"], wq_ref[...], preferred_element_type=F32) * MLA_SCORE_SCALE
            for hd in range(MLA_HEADS):
                lo = hd * MLA_QW
                q_ref[rows, lo:lo + LANE] = q[:, lo:lo + LANE].astype(q_ref.dtype)
                q_ref[rows, lo + LANE:lo + 2 * LANE] = rot(q[:, lo + LANE:lo + 2 * LANE], rows
                                                           ).astype(q_ref.dtype)


def _mla_proj(x, mod, g, wd, gq, gkv, wq, wkv, cos, sin, rope, want_q, tm=1024):
    b, t, d = x.shape
    tm = _pick(t, tm)
    const = lambda bi, i: (0, 0)
    tok = lambda bi, i: (bi, i, 0)
    nk, nv = MLA_HEADS * MLA_NOPE, MLA_HEADS * MLA_V
    out_shape = [jax.ShapeDtypeStruct((b, t, nk), BF16), jax.ShapeDtypeStruct((b, t, LANE), BF16),
                 jax.ShapeDtypeStruct((b, t, nv), BF16)]
    out_specs = [pl.BlockSpec((None, tm, nk), tok), pl.BlockSpec((None, tm, LANE), tok),
                 pl.BlockSpec((None, tm, nv), tok)]
    if want_q:
        out_shape = [jax.ShapeDtypeStruct((b, t, MLA_HEADS * MLA_QW), BF16)] + out_shape
        out_specs = [pl.BlockSpec((None, tm, MLA_HEADS * MLA_QW), tok)] + out_specs
    return pl.pallas_call(
        functools.partial(_mla_proj_kernel, rope=rope, want_q=want_q),
        grid=(b, t // tm),
        in_specs=[pl.BlockSpec((None, tm, d), tok), _mod_spec(mod), pl.BlockSpec((1, d), const),
                  pl.BlockSpec(wd.shape, const), pl.BlockSpec((1, MLA_Q_RANK), const),
                  pl.BlockSpec((1, MLA_KV_RANK), const), pl.BlockSpec(wq.shape, const),
                  pl.BlockSpec(wkv.shape, const),
                  pl.BlockSpec((tm, LANE), lambda bi, i: (i, 0)),
                  pl.BlockSpec((tm, LANE), lambda bi, i: (i, 0))],
        out_specs=out_specs,
        out_shape=out_shape,
        compiler_params=_cparams("parallel", "parallel"),
        name="mla_proj",
    )(x, mod, g.reshape(1, d), wd, gq.reshape(1, -1), gkv.reshape(1, -1), wq, wkv, cos, sin)


def _lane_fold(x, op):
    out = x[:, :LANE]
    for j in range(1, x.shape[1] // LANE):
        out = op(out, x[:, j * LANE:(j + 1) * LANE])
    return out


def _attention_kernel(q_ref, knc_ref, krc_ref, vc_ref, knx_ref, krx_ref, vx_ref, o_ref,
                      kc_ref, kx_ref, vc2_ref, vx2_ref, sc0_ref, sc1_ref, sx0_ref, sx1_ref, m0_ref, m1_ref,
                      *, tk):
    step = pl.program_id(2)

    @pl.when(step == 0)
    def _():
        kc_ref[:, :LANE] = knc_ref[...]
        kc_ref[:, LANE:] = krc_ref[...]
        kx_ref[:, :LANE] = knx_ref[...]
        kx_ref[:, LANE:] = krx_ref[...]
        vc2_ref[:, :LANE] = vc_ref[...]
        vx2_ref[:, :LANE] = vx_ref[...]
        for ref in (vc2_ref, vx2_ref):
            lane = lax.broadcasted_iota(jnp.int32, (ref.shape[0], LANE), 1)
            ref[:, LANE:] = jnp.where(lane == 0, 1.0, 0.0).astype(ref.dtype)

    n_blocks = kx_ref.shape[0] // tk
    n_steps = pl.num_programs(2)
    nt = (((1,), (1,)), ((), ()))

    def body(cur, prev):
        if cur is not None:
            sc_cur, sx_cur, m_cur = cur
            q = q_ref[...]
            s = lax.dot_general(q, kc_ref[...], nt, preferred_element_type=F32)
            sc_cur[...] = s
            m_part = _lane_fold(s, jnp.maximum)
        if prev is not None:
            sc_prev, sx_prev, m_prev = prev
            m = jnp.max(m_prev[...], axis=-1, keepdims=True)
            p = jnp.exp2(sc_prev[...] - m)
            acc = jnp.dot(p.astype(BF16), vc2_ref[...], preferred_element_type=F32)
        for i in range(n_blocks):
            if cur is not None:
                s = lax.dot_general(q, kx_ref[i * tk:(i + 1) * tk, :], nt, preferred_element_type=F32)
                sx_cur[i] = s
                m_part = jnp.maximum(m_part, _lane_fold(s, jnp.maximum))
            if prev is not None:
                p = jnp.exp2(sx_prev[i] - m)
                acc = acc + jnp.dot(p.astype(BF16), vx2_ref[i * tk:(i + 1) * tk, :],
                                    preferred_element_type=F32)
        if cur is not None:
            m_cur[...] = m_part
        if prev is not None:
            o_ref[...] = (acc[:, :MLA_V] / acc[:, MLA_V:MLA_V + 1]).astype(o_ref.dtype)

    bufs = ((sc0_ref, sx0_ref, m0_ref), (sc1_ref, sx1_ref, m1_ref))
    first = step == 0
    last = step == n_steps - 1
    inner = jnp.logical_not(first | last)

    @pl.when(first)
    def _():
        body(bufs[0], None)

    @pl.when(inner & (step % 2 == 0))
    def _():
        body(bufs[0], bufs[1])

    @pl.when(inner & (step % 2 == 1))
    def _():
        body(bufs[1], bufs[0])

    for parity in (0, 1):
        @pl.when(last & (step % 2 == parity))
        def _(parity=parity):
            body(None, bufs[1 - parity])


def _attention(q, kn_c, kr_c, v_c, kn_x, kr_x, v_x, tq=1024, tk=256):
    b, t, _ = q.shape
    tc = kn_c.shape[1]
    tq = _pick(t, tq)
    tk = _pick(t, tk)
    nq = t // tq
    head_c = lambda bi, hi, i: (bi, 0, hi)
    shared = lambda bi, hi, i: (bi, 0, 0)
    return pl.pallas_call(
        functools.partial(_attention_kernel, tk=tk),
        grid=(b, MLA_HEADS, nq + 1),
        in_specs=[pl.BlockSpec((None, tq, MLA_QW), lambda bi, hi, i: (bi, jnp.minimum(i, nq - 1), hi)),
                  pl.BlockSpec((None, tc, LANE), head_c), pl.BlockSpec((None, tc, LANE), shared),
                  pl.BlockSpec((None, tc, LANE), head_c),
                  pl.BlockSpec((None, t, LANE), head_c), pl.BlockSpec((None, t, LANE), shared),
                  pl.BlockSpec((None, t, LANE), head_c)],
        out_specs=pl.BlockSpec((None, tq, MLA_V), lambda bi, hi, i: (bi, jnp.maximum(i - 1, 0), hi)),
        out_shape=jax.ShapeDtypeStruct((b, t, MLA_HEADS * MLA_V), BF16),
        scratch_shapes=[pltpu.VMEM((tc, MLA_QW), BF16), pltpu.VMEM((t, MLA_QW), BF16),
                        pltpu.VMEM((tc, 2 * MLA_V), BF16), pltpu.VMEM((t, 2 * MLA_V), BF16),
                        pltpu.VMEM((tq, tc), F32), pltpu.VMEM((tq, tc), F32),
                        pltpu.VMEM((t // tk, tq, tk), F32), pltpu.VMEM((t // tk, tq, tk), F32),
                        pltpu.VMEM((tq, LANE), F32), pltpu.VMEM((tq, LANE), F32)],
        compiler_params=_cparams("parallel", "parallel", "arbitrary"),
        name="mla_attention",
    )(q, kn_c, kr_c, v_c, kn_x, kr_x, v_x)


def _router_kernel(att_ref, wo_ref, x_ref, mod_ref, g_ref, wr_ref, xo_ref, h_ref, w_ref, e_ref):
    w_hi = wr_ref[...].astype(BF16)
    w_lo = (wr_ref[...] - w_hi.astype(F32)).astype(BF16)
    w_hilo = jnp.concatenate([w_hi, w_lo], axis=1)
    blocks = _row_blocks(x_ref.shape[0])
    for blk in blocks:
        blk["y"] = jnp.dot(att_ref[blk["rows"], :], wo_ref[...], preferred_element_type=F32)
    for blk in blocks:
        rows = blk["rows"]
        x1 = x_ref[rows, :] + mod_ref[2:3, :] * (_rms(blk["y"]) * g_ref[1:2, :])
        xo_ref[rows, :] = x1
        h = _modnorm(x1, g_ref[2:3, :], mod_ref, 3, 4)
        h_ref[rows, :] = _pack_halves(h)
        blk["h_hi"] = h.astype(BF16)
        blk["h_lo"] = (h - blk["h_hi"].astype(F32)).astype(BF16)
    for blk in blocks:
        both = jnp.dot(blk["h_hi"], w_hilo, preferred_element_type=F32)
        blk["logits"] = (both[:, :LANE] + both[:, LANE:]
                         + jnp.dot(blk["h_lo"], w_hi, preferred_element_type=F32))
    for blk in blocks:
        logits = blk["logits"]
        lane = lax.broadcasted_iota(jnp.int32, logits.shape, 1)
        neg = jnp.float32(-jnp.inf)
        lg = jnp.where(lane < N_EXPERTS, logits, neg)
        m1 = jnp.max(lg, axis=-1, keepdims=True)
        i1 = jnp.min(jnp.where(lg == m1, lane, LANE), axis=-1, keepdims=True)
        lg2 = jnp.where(lane == i1, neg, lg)
        m2 = jnp.max(lg2, axis=-1, keepdims=True)
        i2 = jnp.min(jnp.where(lg2 == m2, lane, LANE), axis=-1, keepdims=True)
        e2 = jnp.exp(m2 - m1)
        w1 = 1.0 / (1.0 + e2)
        w2 = e2 / (1.0 + e2)
        w_ref[blk["rows"], :] = jnp.where(lane == 0, w1, jnp.where(lane == 1, w2, 0.0))
        e_ref[blk["rows"], :] = jnp.where(lane == 0, i1, jnp.where(lane == 1, i2, 0))


def _router(att, w_o, x, mod, g, router_pad, tm=1024):
    b, t, d = x.shape
    tm = _pick(t, tm)
    tok = lambda bi, i: (bi, i, 0)
    const = lambda bi, i: (0, 0)
    return pl.pallas_call(
        _router_kernel,
        grid=(b, t // tm),
        in_specs=[pl.BlockSpec((None, tm, att.shape[-1]), tok), pl.BlockSpec(w_o.shape, const),
                  pl.BlockSpec((None, tm, d), tok), _mod_spec(mod), pl.BlockSpec(g.shape, const),
                  pl.BlockSpec((d, LANE), const)],
        out_specs=[pl.BlockSpec((None, tm, d), tok), pl.BlockSpec((None, tm, d // 2), tok),
                   pl.BlockSpec((None, tm, LANE), tok), pl.BlockSpec((None, tm, LANE), tok)],
        out_shape=[jax.ShapeDtypeStruct((b, t, d), F32), jax.ShapeDtypeStruct((b, t, d // 2), jnp.int32),
                   jax.ShapeDtypeStruct((b, t, LANE), F32), jax.ShapeDtypeStruct((b, t, LANE), jnp.int32)],
        compiler_params=_cparams("parallel", "parallel"),
        name="moe_router",
    )(att, w_o, x, mod, g, router_pad)


SC_GATHER_ROWS = 64


def _gather_rows(table, idx):
    info = plsc.get_sparse_core_info()
    n_workers = info.num_cores * info.num_subcores
    n, width = idx.shape[0], table.shape[1]
    chunk = SC_GATHER_ROWS
    assert n % (n_workers * chunk) == 0
    n_chunks = n // (n_workers * chunk)
    mesh = plsc.VectorSubcoreMesh(core_axis_name="c", subcore_axis_name="s")

    assert n_chunks % 2 == 0

    def body(table_hbm, idx_hbm, out_hbm, idx_v, rows0, rows1, sem0, sem1):
        wid = lax.axis_index("s") * info.num_cores + lax.axis_index("c")
        pltpu.sync_copy(idx_hbm.at[wid], idx_v)
        bufs = ((rows0, sem0), (rows1, sem1))

        def gather(j, k):
            return pltpu.make_async_copy(table_hbm.at[idx_v.at[j]], bufs[k][0], bufs[k][1])

        gather(0, 0).start()

        @pl.loop(0, n_chunks, step=2)
        def _(j0):
            for k in (0, 1):
                j = j0 + k
                gather(j, k).wait()

                @pl.when(j + 1 < n_chunks)
                def _():
                    gather(j + 1, 1 - k).start()

                pltpu.sync_copy(bufs[k][0], out_hbm.at[pl.ds((wid * n_chunks + j) * chunk, chunk)])

    return pl.kernel(
        body,
        out_type=jax.ShapeDtypeStruct((n, width), table.dtype),
        mesh=mesh,
        scratch_types=[pltpu.VMEM((n_chunks, chunk), jnp.int32), pltpu.VMEM((chunk, width), table.dtype),
                       pltpu.VMEM((chunk, width), table.dtype), pltpu.SemaphoreType.DMA,
                       pltpu.SemaphoreType.DMA],
        name="sc_gather_rows",
    )(table, idx.reshape(n_workers, n_chunks, chunk))


def _scatter_rows_twice(table, idx0, idx1, n_out):
    info = plsc.get_sparse_core_info()
    n_workers = info.num_cores * info.num_subcores
    n, width = table.shape
    chunk = SC_GATHER_ROWS
    assert n % (n_workers * chunk) == 0
    n_chunks = n // (n_workers * chunk)
    mesh = plsc.VectorSubcoreMesh(core_axis_name="c", subcore_axis_name="s")

    def body(table_hbm, i0_hbm, i1_hbm, out_hbm, i0_v, i1_v, rows_v):
        wid = lax.axis_index("s") * info.num_cores + lax.axis_index("c")
        pltpu.sync_copy(i0_hbm.at[wid], i0_v)
        pltpu.sync_copy(i1_hbm.at[wid], i1_v)

        @pl.loop(0, n_chunks)
        def _(j):
            pltpu.sync_copy(table_hbm.at[pl.ds((wid * n_chunks + j) * chunk, chunk)], rows_v)
            pltpu.sync_copy(rows_v, out_hbm.at[i0_v.at[j]])
            pltpu.sync_copy(rows_v, out_hbm.at[i1_v.at[j]])

    shape3 = (n_workers, n_chunks, chunk)
    return pl.kernel(
        body,
        out_type=jax.ShapeDtypeStruct((n_out, width), table.dtype),
        mesh=mesh,
        scratch_types=[pltpu.VMEM((n_chunks, chunk), jnp.int32), pltpu.VMEM((n_chunks, chunk), jnp.int32),
                       pltpu.VMEM((chunk, width), table.dtype)],
        name="sc_scatter_rows",
    )(table, idx0.reshape(shape3), idx1.reshape(shape3))


def _pack_halves(x):
    k = x.shape[1] // 2
    bits = lambda v: lax.bitcast_convert_type(v.astype(BF16).astype(F32), jnp.uint32)
    word = lax.shift_right_logical(bits(x[:, :k]), jnp.uint32(16)) | bits(x[:, k:])
    return lax.bitcast_convert_type(word, jnp.int32)


def _unpack_halves(w):
    u = lax.bitcast_convert_type(w, jnp.uint32)
    lo = lax.bitcast_convert_type(lax.shift_left(u, jnp.uint32(16)), F32)
    hi = lax.bitcast_convert_type(u & jnp.uint32(0xFFFF0000), F32)
    return jnp.concatenate([lo, hi], axis=-1)


def _expert_ffn_kernel(te_ref, tv_ref, x_ref, wa_ref, wb_ref, wo_ref, o_ref):
    valid = tv_ref[pl.program_id(0)] != 0

    @pl.when(valid)
    def _():
        blocks = _row_blocks(x_ref.shape[0])
        for blk in blocks:
            h = _unpack_halves(x_ref[blk["rows"], :]).astype(BF16)
            a = jnp.dot(h, wa_ref[...], preferred_element_type=F32)
            bb = jnp.dot(h, wb_ref[...], preferred_element_type=F32)
            blk["u"] = (_silu(a) * bb).astype(BF16)
        for blk in blocks:
            o_ref[blk["rows"], :] = _pack_halves(jnp.dot(blk["u"], wo_ref[...], preferred_element_type=F32))

    @pl.when(jnp.logical_not(valid))
    def _():
        o_ref[...] = jnp.zeros_like(o_ref)


def _expert_ffn(xs, tile_expert, tile_valid, w_in, w_out, tm):
    p = xs.shape[0]
    n_exp, fdim, d = w_out.shape
    resident = lambda shape, col: pl.BlockSpec(shape, lambda j, te, tv: (te[j], 0, col))
    grid_spec = pltpu.PrefetchScalarGridSpec(
        num_scalar_prefetch=2,
        grid=(p // tm,),
        in_specs=[pl.BlockSpec((tm, d // 2), lambda j, te, tv: (j, 0)),
                  resident((None, d, fdim), 0), resident((None, d, fdim), 1), resident((None, fdim, d), 0)],
        out_specs=pl.BlockSpec((tm, d // 2), lambda j, te, tv: (j, 0)))
    return pl.pallas_call(
        _expert_ffn_kernel,
        grid_spec=grid_spec,
        out_shape=jax.ShapeDtypeStruct((p, d // 2), jnp.int32),
        compiler_params=_cparams("parallel"),
        name="moe_expert_ffn",
    )(tile_expert, tile_valid, xs, w_in, w_in, w_out)


def _moe_combine_kernel(x_ref, y0_ref, y1_ref, w_ref, mod_ref, g_ref, o_ref):
    y = w_ref[:, 0:1] * _unpack_halves(y0_ref[...]) + w_ref[:, 1:2] * _unpack_halves(y1_ref[...])
    o_ref[...] = x_ref[...] + mod_ref[5:6, :] * (_rms(y) * g_ref[...])


def _moe_combine(x, y2, w, mod, g, tm=1024):
    b, t, d = x.shape
    tm = _pick(t, tm)
    tok = lambda bi, i: (bi, i, 0)
    return pl.pallas_call(
        _moe_combine_kernel,
        grid=(b, t // tm),
        in_specs=[pl.BlockSpec((None, tm, d), tok),
                  pl.BlockSpec((None, None, tm, d // 2), lambda bi, i: (0, bi, i, 0)),
                  pl.BlockSpec((None, None, tm, d // 2), lambda bi, i: (1, bi, i, 0)),
                  pl.BlockSpec((None, tm, LANE), tok), _mod_spec(mod),
                  pl.BlockSpec((1, d), lambda bi, i: (0, 0))],
        out_specs=pl.BlockSpec((None, tm, d), tok),
        out_shape=jax.ShapeDtypeStruct((b, t, d), F32),
        compiler_params=_cparams("parallel", "parallel"),
        name="moe_combine",
    )(x, y2, y2, w, mod, g.reshape(1, d))


def _moe(x, h, w, e, mod, g_out, w_in, w_out, tm=512):
    b, t, d = x.shape
    n_exp = w_out.shape[0]
    n = b * t
    e_flat = jnp.concatenate([e[..., 0].reshape(n), e[..., 1].reshape(n)])
    onehot = (e_flat[:, None] == jnp.arange(n_exp, dtype=jnp.int32)).astype(jnp.int32)
    csum = jnp.cumsum(onehot, axis=0)
    rank = jnp.sum((csum - onehot) * onehot, axis=-1)
    counts = csum[-1]
    ends = jnp.cumsum((counts + tm - 1) // tm * tm)
    dest = (ends - (counts + tm - 1) // tm * tm)[e_flat] + rank
    p = 2 * n + n_exp * tm
    tile_start = jnp.arange(p // tm, dtype=jnp.int32) * tm
    tile_expert = jnp.minimum(jnp.searchsorted(ends, tile_start, side="right"), n_exp - 1).astype(jnp.int32)
    tile_valid = (tile_start < ends[-1]).astype(jnp.int32)

    xs = _scatter_rows_twice(h.reshape(n, d // 2), dest[:n], dest[n:], p)
    ys = _expert_ffn(xs, tile_expert, tile_valid, w_in, w_out, tm)
    y2 = _gather_rows(ys, dest)
    return _moe_combine(x, y2.reshape(2, b, t, d // 2), w, mod, g_out)


def _axial_angles(n_tokens, dim):
    t = jnp.arange(n_tokens)
    rows = (t // GRID_W).astype(F32)
    cols = (t % GRID_W).astype(F32)
    n_freq = dim // 4
    inv = ROPE_BASE ** (-jnp.arange(n_freq, dtype=F32) / n_freq)
    return jnp.concatenate([rows[:, None] * inv, cols[:, None] * inv], axis=-1)


def _rope_tables(n_tokens, dim):
    ang = _axial_angles(n_tokens, dim)
    cos, sin = jnp.cos(ang), jnp.sin(ang)
    if dim == LANE:
        return jnp.concatenate([cos, cos], -1), jnp.concatenate([-sin, sin], -1)
    z = jnp.zeros_like(cos)
    return jnp.concatenate([cos, z, cos, z], -1), jnp.concatenate([-sin, z, sin, z], -1)


def _spread_rope_cols(w):
    half = MLA_ROPE // 2
    z = jnp.zeros(w.shape[:-1] + (half,), w.dtype)
    return jnp.concatenate([w[..., :half], z, w[..., half:], z], axis=-1)


def kernel(x, c, ctx, c_ctx, mod_w, mod_b, norm_g, mix_in_w, ret_decay_logit, hgrn_lb_logit, mix_out_w,
           ffn_in_w, ffn_out_w, mla_down_w, mla_q_norm_g, mla_kv_norm_g, mla_uq_w, mla_ukv_w, mla_out_w,
           router_w, moe_in_w, moe_out_w):
    batch, seq, d = x.shape
    assert mod_w.shape[0] == 2, "two layers: retention/HGRN2 then MLA/MoE"

    rows = -(-(batch + 1) // 8) * 8
    cond = jnp.zeros((rows, d), F32).at[:batch].set(c).at[batch].set(c_ctx)
    mod = _modulation(cond, mod_w, mod_b)
    mod_x = [mod[l, :batch].reshape(batch, 6, d) for l in range(2)]
    mod_c = [mod[l, batch:batch + 1].reshape(1, 6, d) for l in range(2)]

    g = norm_g[0]
    ret_w = RET_HEADS * HEAD_DIM
    z_lo, z_hi = 5 * ret_w, 7 * ret_w
    w_in = mix_in_w[0].astype(BF16)
    w_main = jnp.concatenate([w_in[:, :z_lo], w_in[:, z_hi:]], axis=1)
    w_z = w_in[:, z_lo:z_hi]
    proj_x, z_x = _in_proj(x, mod_x[0], g[0], w_main, w_z)
    proj_c, z_c = _in_proj(ctx, mod_c[0], g[0], w_main, w_z)
    cos2, sin2 = _rope_tables(seq, HEAD_DIM)
    log_gamma = jax.nn.log_sigmoid(ret_decay_logit[0].astype(F32))
    lb = jnp.cumsum(jax.nn.softmax(hgrn_lb_logit.astype(F32), axis=0), axis=0)[0]
    ret_c, ret_x = _retention(proj_c, proj_x, log_gamma, cos2, sin2)
    hg_c, hg_x = _hgrn(proj_c, proj_x, z_c, z_x, lb, first=4 * RET_HEADS)
    w_out = mix_out_w[0].astype(BF16)
    w_outs = [w_out[:ret_w], w_out[ret_w:]]
    f_in, f_out = ffn_in_w[0].astype(BF16), ffn_out_w[0].astype(BF16)
    x = _mixer_ffn(ret_x, hg_x, *w_outs, x, mod_x[0], g, f_in, f_out)
    ctx = _mixer_ffn(ret_c, hg_c, *w_outs, ctx, mod_c[0], g, f_in, f_out)

    g = norm_g[1]
    dq = MLA_NOPE + MLA_ROPE
    wd = mla_down_w[0]
    wd = jnp.concatenate([wd[:, :MLA_Q_RANK + MLA_KV_RANK],
                          _spread_rope_cols(wd[:, MLA_Q_RANK + MLA_KV_RANK:])], axis=-1).astype(BF16)
    wq = mla_uq_w[0].reshape(MLA_Q_RANK, MLA_HEADS, dq)
    wq = jnp.concatenate([wq[..., :MLA_NOPE], _spread_rope_cols(wq[..., MLA_NOPE:])], axis=-1)
    wq = wq.reshape(MLA_Q_RANK, MLA_HEADS * MLA_QW).astype(BF16)
    wkv = mla_ukv_w[0].reshape(MLA_KV_RANK, MLA_HEADS, MLA_NOPE + MLA_V)
    wkv = jnp.concatenate([wkv[..., :MLA_NOPE].reshape(MLA_KV_RANK, -1),
                           wkv[..., MLA_NOPE:].reshape(MLA_KV_RANK, -1)], axis=-1).astype(BF16)
    cos_m, sin_m = _rope_tables(seq, MLA_ROPE)
    q, kn_x, kr_x, v_x = _mla_proj(x, mod_x[1], g[0], wd, mla_q_norm_g[0], mla_kv_norm_g[0], wq, wkv,
                                   cos_m, sin_m, rope=True, want_q=True)
    tc = ctx.shape[1]
    kn_c, kr_c, v_c = _mla_proj(ctx, mod_c[1], g[0], wd, mla_q_norm_g[0], mla_kv_norm_g[0], wq, wkv,
                                cos_m[:tc], sin_m[:tc], rope=False, want_q=False)
    att = _attention(q, kn_c, kr_c, v_c, kn_x, kr_x, v_x)
    router_pad = jnp.zeros((d, LANE), F32).at[:, :N_EXPERTS].set(router_w[0])
    x, h, w, e = _router(att, mla_out_w[0].astype(BF16), x, mod_x[1], g, router_pad)
    x = _moe(x, h, w, e, mod_x[1], g[3], moe_in_w[0].astype(BF16), moe_out_w[0].astype(BF16))
    return x
```

```python
import functools
import math

import jax
import jax.numpy as jnp
import numpy as np
from jax import lax
from jax.experimental import pallas as pl
from jax.experimental.pallas import tpu as pltpu
from jax.experimental.pallas import tpu_sc as plsc

F32 = jnp.float32
BF16 = jnp.bfloat16

NORM_EPS = 1e-6
ROPE_BASE = 10000.0
GRID_W = 64
LANE = 128

RET_HEADS = 4
HGRN_HEADS = 4
HEAD_DIM = 128
RET_CHUNK = 128
HGRN_CHUNK = 128

MLA_HEADS = 8
MLA_NOPE = 128
MLA_ROPE = 64
MLA_V = 128
MLA_Q_RANK = 384
MLA_KV_RANK = 256
N_EXPERTS = 8

VMEM_LIMIT = 56 * 1024 * 1024


def _cparams(*sem):
    return pltpu.CompilerParams(dimension_semantics=sem, vmem_limit_bytes=VMEM_LIMIT)


def _mm(a, b):
    return jnp.dot(a.astype(BF16), b.astype(BF16), preferred_element_type=F32)


def _mm_nt(a, b):
    return lax.dot_general(a.astype(BF16), b.astype(BF16), (((1,), (1,)), ((), ())),
                           preferred_element_type=F32)


def _mm_tn(a, b):
    return lax.dot_general(a.astype(BF16), b.astype(BF16), (((0,), (0,)), ((), ())),
                           preferred_element_type=F32)


def _rms(x):
    return x * lax.rsqrt(jnp.mean(x * x, axis=-1, keepdims=True) + NORM_EPS)


def _silu(x):
    return x * (1.0 / (1.0 + jnp.exp(-x)))


def _modnorm(x, g, mod_ref, shift_row, scale_row):
    shift = mod_ref[shift_row:shift_row + 1, :]
    scale = mod_ref[scale_row:scale_row + 1, :]
    return _rms(x) * g * (1.0 + scale) + shift


def _mod_spec(mod):
    d = mod.shape[-1]
    if mod.shape[0] == 1:
        return pl.BlockSpec((None, 6, d), lambda b, *_: (0, 0, 0))
    return pl.BlockSpec((None, 6, d), lambda b, *_: (b, 0, 0))


ROW_BLOCK = 256


def _row_blocks(n_rows):
    size = min(ROW_BLOCK, n_rows)
    return [dict(rows=pl.ds(r, size)) for r in range(0, n_rows, size)]


def _pick(n, pref):
    t = min(pref, n)
    while n % t:
        t -= LANE
    return t


def _mod_kernel(c_ref, w_ref, b_ref, o_ref):
    a = _silu(c_ref[...])
    o_ref[...] = jnp.dot(a, w_ref[...], preferred_element_type=F32,
                         precision=lax.Precision.HIGHEST) + b_ref[...]


def _modulation(cond, mod_w, mod_b):
    n_layers, d, n = mod_w.shape
    r = cond.shape[0]
    tn = _pick(n, 1536)
    return pl.pallas_call(
        _mod_kernel,
        grid=(n_layers, n // tn),
        in_specs=[pl.BlockSpec((r, d), lambda l, j: (0, 0)),
                  pl.BlockSpec((None, d, tn), lambda l, j: (l, 0, j)),
                  pl.BlockSpec((None, 1, tn), lambda l, j: (l, 0, j))],
        out_specs=pl.BlockSpec((None, r, tn), lambda l, j: (l, 0, j)),
        out_shape=jax.ShapeDtypeStruct((n_layers, r, n), F32),
        compiler_params=_cparams("parallel", "parallel"),
        name="modulation",
    )(cond, mod_w, mod_b.reshape(n_layers, 1, n))


def _in_proj_kernel(x_ref, mod_ref, g_ref, wm_ref, wz_ref, om_ref, oz_ref):
    blocks = _row_blocks(x_ref.shape[0])
    for blk in blocks:
        blk["h"] = _modnorm(x_ref[blk["rows"], :], g_ref[...], mod_ref, 0, 1).astype(BF16)
    for blk in blocks:
        om_ref[blk["rows"], :] = jnp.dot(blk["h"], wm_ref[...], preferred_element_type=F32).astype(om_ref.dtype)
    for blk in blocks:
        oz_ref[blk["rows"], :] = jnp.dot(blk["h"], wz_ref[...], preferred_element_type=F32)


def _in_proj(x, mod, g, w_main, w_z, tm=1024):
    b, t, d = x.shape
    tm = _pick(t, tm)
    tok = lambda bi, i: (bi, i, 0)
    resident = lambda w: pl.BlockSpec(w.shape, lambda bi, i: (0, 0), pipeline_mode=pl.Buffered(1))
    n_main, n_z = w_main.shape[1], w_z.shape[1]
    return pl.pallas_call(
        _in_proj_kernel,
        grid=(b, t // tm),
        in_specs=[pl.BlockSpec((None, tm, d), tok), _mod_spec(mod), pl.BlockSpec((1, d), lambda bi, i: (0, 0)),
                  resident(w_main), resident(w_z)],
        out_specs=[pl.BlockSpec((None, tm, n_main), tok), pl.BlockSpec((None, tm, n_z), tok)],
        out_shape=[jax.ShapeDtypeStruct((b, t, n_main), BF16), jax.ShapeDtypeStruct((b, t, n_z), F32)],
        compiler_params=_cparams("parallel", "parallel"),
        name="in_proj",
    )(x, mod, g.reshape(1, d), w_main, w_z)


def _mixer_ffn_kernel(ret_ref, hg_ref, w1_ref, w2_ref, x_ref, mod_ref, g_ref, wa_ref, wb_ref, wo_ref, o_ref):
    blocks = _row_blocks(x_ref.shape[0])
    for blk in blocks:
        rows = blk["rows"]
        blk["y"] = (jnp.dot(ret_ref[rows, :], w1_ref[...], preferred_element_type=F32)
                    + jnp.dot(hg_ref[rows, :], w2_ref[...], preferred_element_type=F32))
    for blk in blocks:
        blk["x1"] = x_ref[blk["rows"], :] + mod_ref[2:3, :] * (_rms(blk["y"]) * g_ref[1:2, :])
        blk["h"] = _modnorm(blk["x1"], g_ref[2:3, :], mod_ref, 3, 4).astype(BF16)
    for blk in blocks:
        a = jnp.dot(blk["h"], wa_ref[...], preferred_element_type=F32)
        bb = jnp.dot(blk["h"], wb_ref[...], preferred_element_type=F32)
        blk["u"] = (_silu(a) * bb).astype(BF16)
    for blk in blocks:
        z = jnp.dot(blk["u"], wo_ref[...], preferred_element_type=F32)
        o_ref[blk["rows"], :] = blk["x1"] + mod_ref[5:6, :] * (_rms(z) * g_ref[3:4, :])


def _mixer_ffn(ret, hg, w1, w2, x, mod, g, w_in, w_out, tm=1024):
    b, t, d = x.shape
    fdim = w_out.shape[0]
    tm = _pick(t, tm)
    tok = lambda bi, i: (bi, i, 0)
    resident = lambda shape, idx: pl.BlockSpec(shape, lambda bi, i: idx, pipeline_mode=pl.Buffered(1))
    return pl.pallas_call(
        _mixer_ffn_kernel,
        grid=(b, t // tm),
        in_specs=[pl.BlockSpec((None, tm, ret.shape[-1]), tok), pl.BlockSpec((None, tm, hg.shape[-1]), tok),
                  resident(w1.shape, (0, 0)), resident(w2.shape, (0, 0)),
                  pl.BlockSpec((None, tm, d), tok), _mod_spec(mod), pl.BlockSpec(g.shape, lambda bi, i: (0, 0)),
                  resident((d, fdim), (0, 0)), resident((d, fdim), (0, 1)), resident((fdim, d), (0, 0))],
        out_specs=pl.BlockSpec((None, tm, d), tok),
        out_shape=jax.ShapeDtypeStruct((b, t, d), F32),
        compiler_params=_cparams("parallel", "parallel"),
        name="mixer_ffn",
    )(ret, hg, w1, w2, x, mod, g, w_in, w_in, w_out)


SCAN_HEADS = 4
HGRN_SCAN_HEADS = 4


def _head_slice(hh):
    return slice(hh * HEAD_DIM, (hh + 1) * HEAD_DIM)


def _emit_scan_output(ch, o, cs, finish, g_ref, y_ref):
    rows = pl.ds(ch["start"], cs)
    if finish:
        other = ch["other_ref"][rows, ch["sl"]].astype(F32)
        gate = g_ref[rows, ch["sl"]].astype(F32)
        y_ref[rows, ch["sl"]] = (_rms(o + other) * _silu(gate)).astype(y_ref.dtype)
    else:
        ch["o_ref"][rows, ch["sl"]] = o.astype(ch["o_ref"].dtype)


def _run_scan(step, n_chunks):
    assert n_chunks % 2 == 0, "sequence lengths must be an even number of chunks"
    lax.fori_loop(0, n_chunks // 2, functools.partial(step, finish=False), 0)
    lax.fori_loop(n_chunks // 2, n_chunks, functools.partial(step, finish=True), 0)


def _retention_kernel(lg_ref, qc_ref, kc_ref, vc_ref, gc_ref, qx_ref, kx_ref, vx_ref, gx_ref,
                      cos_ref, sin_ref, yc_ref, yx_ref,
                      oc_ref, ox_ref, s_ref, intra_ref, qd_ref, kd_ref):
    cs = intra_ref.shape[1]
    hp = SCAN_HEADS
    head0 = pl.program_id(1) * hp
    row = lax.broadcasted_iota(jnp.int32, (cs, cs), 0).astype(F32)
    col = lax.broadcasted_iota(jnp.int32, (cs, cs), 1).astype(F32)
    pos = lax.broadcasted_iota(jnp.int32, (cs, HEAD_DIM), 0).astype(F32)
    k_scale = HEAD_DIM ** -0.5

    for hh in range(hp):
        for d in (0, 1):
            c = 2 * hh + d
            lg = lg_ref[d, head0 + hh]
            rel = (col - row) if d else (row - col)
            intra_ref[c] = jnp.where(rel >= 0, jnp.exp(lg * jnp.maximum(rel, 0.0)), 0.0)
            p = (cs - 1.0 - pos) if d else pos
            qd_ref[c] = jnp.exp(lg * (p + 1.0))
            kd_ref[c] = jnp.exp(lg * (cs - 1.0 - p))
    s_ref[...] = jnp.zeros_like(s_ref)

    for (q_ref, k_ref, v_ref, g_ref, y_ref, of_ref, ob_ref, rope) in (
            (qc_ref, kc_ref, vc_ref, gc_ref, yc_ref, oc_ref, oc_ref, False),
            (qx_ref, kx_ref, vx_ref, gx_ref, yx_ref, ox_ref, ox_ref, True)):
        n_chunks = q_ref.shape[0] // cs

        def step(i, carry, *, finish, q_ref=q_ref, k_ref=k_ref, v_ref=v_ref, g_ref=g_ref, y_ref=y_ref,
                 of_ref=of_ref, ob_ref=ob_ref, rope=rope, n_chunks=n_chunks):
            chains = []
            for d in (0, 1):
                start = pl.multiple_of(((n_chunks - 1 - i) if d else i) * cs, cs)
                if rope:
                    cos = cos_ref[pl.ds(start, cs), :]
                    sin = sin_ref[pl.ds(start, cs), :]
                for hh in range(hp):
                    sl = _head_slice(hh)
                    q = q_ref[pl.ds(start, cs), sl].astype(F32)
                    k = k_ref[pl.ds(start, cs), sl].astype(F32) * k_scale
                    if rope:
                        q = q * cos + pltpu.roll(q, HEAD_DIM // 2, 1) * sin
                        k = k * cos + pltpu.roll(k, HEAD_DIM // 2, 1) * sin
                    chains.append(dict(d=d, hh=hh, c=2 * hh + d, start=start, sl=sl, q=q, k=k,
                                       v=v_ref[pl.ds(start, cs), sl], o_ref=ob_ref if d else of_ref,
                                       other_ref=of_ref if d else ob_ref))
            for ch in chains:
                ch["att"] = _mm_nt(ch["q"], ch["k"]) * intra_ref[ch["c"]]
            for ch in chains:
                c = ch["c"]
                ch["state"] = s_ref[c]
                ch["o"] = _mm(ch["att"], ch["v"]) + _mm(ch["q"] * qd_ref[c], ch["state"])
            for ch in chains:
                c = ch["c"]
                c_dec = jnp.exp(jnp.full((1, HEAD_DIM), lg_ref[ch["d"], head0 + ch["hh"]] * cs, F32))
                s_ref[c] = ch["state"] * c_dec + _mm_tn(ch["k"] * kd_ref[c], ch["v"])
            for ch in chains:
                _emit_scan_output(ch, ch["o"], cs, finish, g_ref, y_ref)
            return carry

        _run_scan(step, n_chunks)


def _retention(proj_c, proj_x, log_gamma, cos2, sin2):
    b, tc, _ = proj_c.shape
    tx = proj_x.shape[1]
    h, hp = RET_HEADS, SCAN_HEADS
    w = hp * HEAD_DIM

    def col(t, group):
        return pl.BlockSpec((None, t, w), lambda bi, hi, lg: (bi, 0, group * (h // hp) + hi))

    n_chain = 2 * hp
    cs = math.gcd(RET_CHUNK, tc, tx)
    grid_spec = pltpu.PrefetchScalarGridSpec(
        num_scalar_prefetch=1,
        grid=(b, h // hp),
        in_specs=[col(tc, 0), col(tc, 1), col(tc, 2), col(tc, 3),
                  col(tx, 0), col(tx, 1), col(tx, 2), col(tx, 3),
                  pl.BlockSpec((tx, HEAD_DIM), lambda bi, hi, lg: (0, 0), pipeline_mode=pl.Buffered(1)),
                  pl.BlockSpec((tx, HEAD_DIM), lambda bi, hi, lg: (0, 0), pipeline_mode=pl.Buffered(1))],
        out_specs=[pl.BlockSpec((None, tc, w), lambda bi, hi, lg: (bi, 0, hi)),
                   pl.BlockSpec((None, tx, w), lambda bi, hi, lg: (bi, 0, hi))],
        scratch_shapes=[pltpu.VMEM((tc, w), BF16), pltpu.VMEM((tx, w), BF16),
                        pltpu.VMEM((n_chain, HEAD_DIM, HEAD_DIM), F32),
                        pltpu.VMEM((n_chain, cs, cs), F32),
                        pltpu.VMEM((n_chain, cs, HEAD_DIM), F32),
                        pltpu.VMEM((n_chain, cs, HEAD_DIM), F32)])
    return pl.pallas_call(
        _retention_kernel,
        grid_spec=grid_spec,
        out_shape=[jax.ShapeDtypeStruct((b, tc, h * HEAD_DIM), BF16),
                   jax.ShapeDtypeStruct((b, tx, h * HEAD_DIM), BF16)],
        compiler_params=_cparams("parallel", "parallel"),
        name="retention",
    )(log_gamma, proj_c, proj_c, proj_c, proj_c, proj_x, proj_x, proj_x, proj_x, cos2, sin2)


def _hgrn_levels(cs):
    return [cs >> (i + 1) for i in range(int(math.log2(cs)))]


def _hgrn_constants(cs):
    t = np.arange(cs)[:, None]
    s = np.arange(cs)[None, :]
    tri = np.stack([(s <= t), (s >= t)]).astype(np.float32)
    masks = []
    for reverse in (False, True):
        lv = []
        for h in _hgrn_levels(cs):
            same = (t // (2 * h)) == (s // (2 * h))
            t_hi = (t % (2 * h)) >= h
            s_hi = (s % (2 * h)) >= h
            lv.append(same & (~t_hi & s_hi if reverse else t_hi & ~s_hi))
        lv.append(t == s)
        masks.append(np.stack(lv))
    return jnp.asarray(tri, BF16), jnp.asarray(np.stack(masks).astype(np.float32), BF16)


def _split3(x):
    hi = x.astype(BF16)
    r = x - hi.astype(F32)
    mid = r.astype(BF16)
    lo = (r - mid.astype(F32)).astype(BF16)
    return hi, mid, lo


def _neg_abs(x):
    bits = lax.bitcast_convert_type(x, jnp.uint32) | jnp.uint32(0x80000000)
    return lax.bitcast_convert_type(bits, F32)


def _level_reference(b, h, reverse):
    cs = b.shape[0]
    ref_local = h if reverse else h - 1
    if 2 * h >= 8:
        b3 = b.reshape(cs // (2 * h), 2 * h, b.shape[1])
        r = b3[:, ref_local:ref_local + 1, :]
        return jnp.broadcast_to(r, b3.shape).reshape(b.shape)
    local = lax.broadcasted_iota(jnp.int32, b.shape, 0) % (2 * h)
    out = b
    for m in range(2 * h):
        if m != ref_local:
            out = jnp.where(local == m, pltpu.roll(b, (m - ref_local) % cs, 0), out)
    return out


def _near_level_weight(f, h, reverse):
    cs = f.shape[0]
    local = lax.broadcasted_iota(jnp.int32, f.shape, 0) % (2 * h)
    before = pltpu.roll(f, 1, 0)
    after = pltpu.roll(f, cs - 1, 0)
    if h == 1:
        return jnp.where(local == (0 if reverse else 1), f, 1.0)
    if reverse:
        return jnp.where(local == 0, f * after, jnp.where(local == 1, f, jnp.where(local == 2, 1.0, before)))
    return jnp.where(local == 0, after, jnp.where(local == 1, 1.0, jnp.where(local == 2, f, f * before)))


def _hgrn_kernel(lb_ref, tri_ref, mask_ref,
                 qc_ref, vc_ref, gc_ref, zfc_ref, zbc_ref,
                 qx_ref, vx_ref, gx_ref, zfx_ref, zbx_ref,
                 yc_ref, yx_ref, oc_ref, ox_ref, st_ref):
    cs = HGRN_CHUNK
    hp = HGRN_SCAN_HEADS
    levels = _hgrn_levels(cs)
    st_ref[...] = jnp.zeros_like(st_ref)

    for (q_ref, v_ref, g_ref, y_ref, zf_ref, zb_ref, of_ref, ob_ref) in (
            (qc_ref, vc_ref, gc_ref, yc_ref, zfc_ref, zbc_ref, oc_ref, oc_ref),
            (qx_ref, vx_ref, gx_ref, yx_ref, zfx_ref, zbx_ref, ox_ref, ox_ref)):
        n_chunks = q_ref.shape[0] // cs

        def step(i, carry, *, finish, q_ref=q_ref, v_ref=v_ref, g_ref=g_ref, y_ref=y_ref, zf_ref=zf_ref,
                 zb_ref=zb_ref, of_ref=of_ref, ob_ref=ob_ref, n_chunks=n_chunks):
            chains = []
            for d in (0, 1):
                start = pl.multiple_of(((n_chunks - 1 - i) if d else i) * cs, cs)
                z_ref = zb_ref if d else zf_ref
                for hh in range(hp):
                    sl = _head_slice(hh)
                    lb = lb_ref[hh]
                    q = _silu(q_ref[pl.ds(start, cs), sl].astype(F32))
                    f = lb + (1.0 - lb) * (1.0 / (1.0 + jnp.exp(-z_ref[pl.ds(start, cs), sl])))
                    chains.append(dict(d=d, c=2 * hh + d, start=start, sl=sl, q=q, k=1.0 - f, f=f,
                                       parts=_split3(jnp.log2(f)), v=v_ref[pl.ds(start, cs), sl],
                                       o_ref=ob_ref if d else of_ref, other_ref=of_ref if d else ob_ref))
            for ch in chains:
                tri = tri_ref[ch["d"]]
                hi, mid, lo = ch["parts"]
                ch["bsum"] = (jnp.dot(tri, hi, preferred_element_type=F32)
                              + jnp.dot(tri, mid, preferred_element_type=F32)
                              + jnp.dot(tri, lo, preferred_element_type=F32))
            for ch in chains:
                ch["qb"], ch["kb"] = ch["q"].astype(BF16), ch["k"].astype(BF16)
                ch["att"] = _mm_nt(ch["qb"], ch["kb"]).astype(BF16) * mask_ref[ch["d"], len(levels)]
            for li, h in enumerate(levels):
                for ch in chains:
                    if h <= 2:
                        w = _near_level_weight(ch["f"], h, ch["d"] == 1).astype(BF16)
                    else:
                        ref = _level_reference(ch["bsum"], h, ch["d"] == 1)
                        w = jnp.exp2(_neg_abs(ch["bsum"] - ref)).astype(BF16)
                    ch["att"] += _mm_nt(ch["qb"] * w, ch["kb"] * w).astype(BF16) * mask_ref[ch["d"], li]
            for ch in chains:
                ch["state"] = st_ref[ch["c"]]
                ch["o"] = _mm(ch["att"], ch["v"]) + _mm_nt(ch["q"] * jnp.exp2(ch["bsum"]), ch["state"])
            for ch in chains:
                bsum = ch["bsum"]
                b_end = bsum[0:1, :] if ch["d"] else bsum[cs - 1:cs, :]
                st_ref[ch["c"]] = (ch["state"] * jnp.exp2(b_end)
                                   + _mm_tn(ch["v"], ch["k"] * jnp.exp2(b_end - bsum)))
            for ch in chains:
                _emit_scan_output(ch, ch["o"], cs, finish, g_ref, y_ref)
            return carry

        _run_scan(step, n_chunks)


def _hgrn(proj_c, proj_x, z_c, z_x, lb, first):
    b, tc, _ = proj_c.shape
    tx = proj_x.shape[1]
    h, hp = HGRN_HEADS, HGRN_SCAN_HEADS
    w = hp * HEAD_DIM
    tri, masks = _hgrn_constants(HGRN_CHUNK)

    def col(t, group):
        return pl.BlockSpec((None, t, w), lambda bi, hi: (bi, 0, (first + group * h) // hp + hi),
                            pipeline_mode=pl.Buffered(1))

    def zcol(t, group):
        return pl.BlockSpec((None, t, w), lambda bi, hi: (bi, 0, group * (h // hp) + hi),
                            pipeline_mode=pl.Buffered(1))

    ins = [col(tc, 0), col(tc, 1), col(tc, 2), zcol(tc, 0), zcol(tc, 1),
           col(tx, 0), col(tx, 1), col(tx, 2), zcol(tx, 0), zcol(tx, 1)]
    return pl.pallas_call(
        _hgrn_kernel,
        grid=(b, h // hp),
        in_specs=[pl.BlockSpec((hp, 1, HEAD_DIM), lambda bi, hi: (hi, 0, 0)),
                  pl.BlockSpec(tri.shape, lambda bi, hi: (0, 0, 0)),
                  pl.BlockSpec(masks.shape, lambda bi, hi: (0, 0, 0, 0))] + ins,
        out_specs=[pl.BlockSpec((None, tc, w), lambda bi, hi: (bi, 0, hi)),
                   pl.BlockSpec((None, tx, w), lambda bi, hi: (bi, 0, hi))],
        out_shape=[jax.ShapeDtypeStruct((b, tc, h * HEAD_DIM), BF16),
                   jax.ShapeDtypeStruct((b, tx, h * HEAD_DIM), BF16)],
        scratch_shapes=[pltpu.VMEM((tc, w), BF16), pltpu.VMEM((tx, w), BF16),
                        pltpu.VMEM((2 * hp, HEAD_DIM, HEAD_DIM), F32)],
        compiler_params=_cparams("parallel", "parallel"),
        name="hgrn2",
    )(lb.reshape(h, 1, HEAD_DIM), tri, masks, proj_c, proj_c, proj_c, z_c, z_c,
      proj_x, proj_x, proj_x, z_x, z_x)


MLA_QW = 2 * LANE
MLA_SCORE_SCALE = (MLA_NOPE + MLA_ROPE) ** -0.5 * math.log2(math.e)


def _mla_proj_kernel(x_ref, mod_ref, g_ref, wd_ref, gq_ref, gkv_ref, wq_ref, wkv_ref, cos_ref, sin_ref,
                     *out_refs, rope, want_q):
    if want_q:
        q_ref, kn_ref, kr_ref, v_ref = out_refs
    else:
        kn_ref, kr_ref, v_ref = out_refs
    nk = MLA_HEADS * MLA_NOPE

    def rot(v, rows):
        return v * cos_ref[rows, :] + pltpu.roll(v, LANE // 2, 1) * sin_ref[rows, :] if rope else v

    blocks = _row_blocks(x_ref.shape[0])
    for blk in blocks:
        blk["h"] = _modnorm(x_ref[blk["rows"], :], g_ref[...], mod_ref, 0, 1).astype(BF16)
    for blk in blocks:
        down = jnp.dot(blk["h"], wd_ref[...], preferred_element_type=F32)
        if want_q:
            blk["qn"] = (_rms(down[:, :MLA_Q_RANK]) * gq_ref[...]).astype(BF16)
        blk["kvn"] = (_rms(down[:, MLA_Q_RANK:MLA_Q_RANK + MLA_KV_RANK]) * gkv_ref[...]).astype(BF16)
        kr_ref[blk["rows"], :] = rot(down[:, MLA_Q_RANK + MLA_KV_RANK:], blk["rows"]).astype(kr_ref.dtype)
    for blk in blocks:
        rows = blk["rows"]
        kv = jnp.dot(blk["kvn"], wkv_ref[...], preferred_element_type=F32)
        kn_ref[rows, :] = kv[:, :nk].astype(kn_ref.dtype)
        v_ref[rows, :] = kv[:, nk:].astype(v_ref.dtype)
    if want_q:
        for blk in blocks:
            rows = blk["rows"]
            q = jnp.dot(blk["qn"], wq_ref[...], preferred_element_type=F32) * MLA_SCORE_SCALE
            for hd in range(MLA_HEADS):
                lo = hd * MLA_QW
                q_ref[rows, lo:lo + LANE] = q[:, lo:lo + LANE].astype(q_ref.dtype)
                q_ref[rows, lo + LANE:lo + 2 * LANE] = rot(q[:, lo + LANE:lo + 2 * LANE], rows
                                                           ).astype(q_ref.dtype)


def _mla_proj(x, mod, g, wd, gq, gkv, wq, wkv, cos, sin, rope, want_q, tm=1024):
    b, t, d = x.shape
    tm = _pick(t, tm)
    const = lambda bi, i: (0, 0)
    tok = lambda bi, i: (bi, i, 0)
    nk, nv = MLA_HEADS * MLA_NOPE, MLA_HEADS * MLA_V
    out_shape = [jax.ShapeDtypeStruct((b, t, nk), BF16), jax.ShapeDtypeStruct((b, t, LANE), BF16),
                 jax.ShapeDtypeStruct((b, t, nv), BF16)]
    out_specs = [pl.BlockSpec((None, tm, nk), tok), pl.BlockSpec((None, tm, LANE), tok),
                 pl.BlockSpec((None, tm, nv), tok)]
    if want_q:
        out_shape = [jax.ShapeDtypeStruct((b, t, MLA_HEADS * MLA_QW), BF16)] + out_shape
        out_specs = [pl.BlockSpec((None, tm, MLA_HEADS * MLA_QW), tok)] + out_specs
    return pl.pallas_call(
        functools.partial(_mla_proj_kernel, rope=rope, want_q=want_q),
        grid=(b, t // tm),
        in_specs=[pl.BlockSpec((None, tm, d), tok), _mod_spec(mod), pl.BlockSpec((1, d), const),
                  pl.BlockSpec(wd.shape, const), pl.BlockSpec((1, MLA_Q_RANK), const),
                  pl.BlockSpec((1, MLA_KV_RANK), const), pl.BlockSpec(wq.shape, const),
                  pl.BlockSpec(wkv.shape, const),
                  pl.BlockSpec((tm, LANE), lambda bi, i: (i, 0)),
                  pl.BlockSpec((tm, LANE), lambda bi, i: (i, 0))],
        out_specs=out_specs,
        out_shape=out_shape,
        compiler_params=_cparams("parallel", "parallel"),
        name="mla_proj",
    )(x, mod, g.reshape(1, d), wd, gq.reshape(1, -1), gkv.reshape(1, -1), wq, wkv, cos, sin)


def _lane_fold(x, op):
    out = x[:, :LANE]
    for j in range(1, x.shape[1] // LANE):
        out = op(out, x[:, j * LANE:(j + 1) * LANE])
    return out


def _attention_kernel(q_ref, knc_ref, krc_ref, vc_ref, knx_ref, krx_ref, vx_ref, o_ref,
                      kc_ref, kx_ref, vc2_ref, vx2_ref, sc0_ref, sc1_ref, sx0_ref, sx1_ref, m0_ref, m1_ref,
                      *, tk):
    step = pl.program_id(2)

    @pl.when(step == 0)
    def _():
        kc_ref[:, :LANE] = knc_ref[...]
        kc_ref[:, LANE:] = krc_ref[...]
        kx_ref[:, :LANE] = knx_ref[...]
        kx_ref[:, LANE:] = krx_ref[...]
        vc2_ref[:, :LANE] = vc_ref[...]
        vx2_ref[:, :LANE] = vx_ref[...]
        for ref in (vc2_ref, vx2_ref):
            lane = lax.broadcasted_iota(jnp.int32, (ref.shape[0], LANE), 1)
            ref[:, LANE:] = jnp.where(lane == 0, 1.0, 0.0).astype(ref.dtype)

    n_blocks = kx_ref.shape[0] // tk
    n_steps = pl.num_programs(2)
    nt = (((1,), (1,)), ((), ()))

    def body(cur, prev):
        if cur is not None:
            sc_cur, sx_cur, m_cur = cur
            q = q_ref[...]
            s = lax.dot_general(q, kc_ref[...], nt, preferred_element_type=F32)
            sc_cur[...] = s
            m_part = _lane_fold(s, jnp.maximum)
        if prev is not None:
            sc_prev, sx_prev, m_prev = prev
            m = jnp.max(m_prev[...], axis=-1, keepdims=True)
            p = jnp.exp2(sc_prev[...] - m)
            acc = jnp.dot(p.astype(BF16), vc2_ref[...], preferred_element_type=F32)
        for i in range(n_blocks):
            if cur is not None:
                s = lax.dot_general(q, kx_ref[i * tk:(i + 1) * tk, :], nt, preferred_element_type=F32)
                sx_cur[i] = s
                m_part = jnp.maximum(m_part, _lane_fold(s, jnp.maximum))
            if prev is not None:
                p = jnp.exp2(sx_prev[i] - m)
                acc = acc + jnp.dot(p.astype(BF16), vx2_ref[i * tk:(i + 1) * tk, :],
                                    preferred_element_type=F32)
        if cur is not None:
            m_cur[...] = m_part
        if prev is not None:
            o_ref[...] = (acc[:, :MLA_V] / acc[:, MLA_V:MLA_V + 1]).astype(o_ref.dtype)

    bufs = ((sc0_ref, sx0_ref, m0_ref), (sc1_ref, sx1_ref, m1_ref))
    first = step == 0
    last = step == n_steps - 1
    inner = jnp.logical_not(first | last)

    @pl.when(first)
    def _():
        body(bufs[0], None)

    @pl.when(inner & (step % 2 == 0))
    def _():
        body(bufs[0], bufs[1])

    @pl.when(inner & (step % 2 == 1))
    def _():
        body(bufs[1], bufs[0])

    for parity in (0, 1):
        @pl.when(last & (step % 2 == parity))
        def _(parity=parity):
            body(None, bufs[1 - parity])


def _attention(q, kn_c, kr_c, v_c, kn_x, kr_x, v_x, tq=1024, tk=256):
    b, t, _ = q.shape
    tc = kn_c.shape[1]
    tq = _pick(t, tq)
    tk = _pick(t, tk)
    nq = t // tq
    head_c = lambda bi, hi, i: (bi, 0, hi)
    shared = lambda bi, hi, i: (bi, 0, 0)
    return pl.pallas_call(
        functools.partial(_attention_kernel, tk=tk),
        grid=(b, MLA_HEADS, nq + 1),
        in_specs=[pl.BlockSpec((None, tq, MLA_QW), lambda bi, hi, i: (bi, jnp.minimum(i, nq - 1), hi)),
                  pl.BlockSpec((None, tc, LANE), head_c), pl.BlockSpec((None, tc, LANE), shared),
                  pl.BlockSpec((None, tc, LANE), head_c),
                  pl.BlockSpec((None, t, LANE), head_c), pl.BlockSpec((None, t, LANE), shared),
                  pl.BlockSpec((None, t, LANE), head_c)],
        out_specs=pl.BlockSpec((None, tq, MLA_V), lambda bi, hi, i: (bi, jnp.maximum(i - 1, 0), hi)),
        out_shape=jax.ShapeDtypeStruct((b, t, MLA_HEADS * MLA_V), BF16),
        scratch_shapes=[pltpu.VMEM((tc, MLA_QW), BF16), pltpu.VMEM((t, MLA_QW), BF16),
                        pltpu.VMEM((tc, 2 * MLA_V), BF16), pltpu.VMEM((t, 2 * MLA_V), BF16),
                        pltpu.VMEM((tq, tc), F32), pltpu.VMEM((tq, tc), F32),
                        pltpu.VMEM((t // tk, tq, tk), F32), pltpu.VMEM((t // tk, tq, tk), F32),
                        pltpu.VMEM((tq, LANE), F32), pltpu.VMEM((tq, LANE), F32)],
        compiler_params=_cparams("parallel", "parallel", "arbitrary"),
        name="mla_attention",
    )(q, kn_c, kr_c, v_c, kn_x, kr_x, v_x)


def _router_kernel(att_ref, wo_ref, x_ref, mod_ref, g_ref, wr_ref, xo_ref, h_ref, w_ref, e_ref):
    w_hi = wr_ref[...].astype(BF16)
    w_lo = (wr_ref[...] - w_hi.astype(F32)).astype(BF16)
    w_hilo = jnp.concatenate([w_hi, w_lo], axis=1)
    blocks = _row_blocks(x_ref.shape[0])
    for blk in blocks:
        blk["y"] = jnp.dot(att_ref[blk["rows"], :], wo_ref[...], preferred_element_type=F32)
    for blk in blocks:
        rows = blk["rows"]
        x1 = x_ref[rows, :] + mod_ref[2:3, :] * (_rms(blk["y"]) * g_ref[1:2, :])
        xo_ref[rows, :] = x1
        h = _modnorm(x1, g_ref[2:3, :], mod_ref, 3, 4)
        h_ref[rows, :] = _pack_halves(h)
        blk["h_hi"] = h.astype(BF16)
        blk["h_lo"] = (h - blk["h_hi"].astype(F32)).astype(BF16)
    for blk in blocks:
        both = jnp.dot(blk["h_hi"], w_hilo, preferred_element_type=F32)
        blk["logits"] = (both[:, :LANE] + both[:, LANE:]
                         + jnp.dot(blk["h_lo"], w_hi, preferred_element_type=F32))
    for blk in blocks:
        logits = blk["logits"]
        lane = lax.broadcasted_iota(jnp.int32, logits.shape, 1)
        neg = jnp.float32(-jnp.inf)
        lg = jnp.where(lane < N_EXPERTS, logits, neg)
        m1 = jnp.max(lg, axis=-1, keepdims=True)
        i1 = jnp.min(jnp.where(lg == m1, lane, LANE), axis=-1, keepdims=True)
        lg2 = jnp.where(lane == i1, neg, lg)
        m2 = jnp.max(lg2, axis=-1, keepdims=True)
        i2 = jnp.min(jnp.where(lg2 == m2, lane, LANE), axis=-1, keepdims=True)
        e2 = jnp.exp(m2 - m1)
        w1 = 1.0 / (1.0 + e2)
        w2 = e2 / (1.0 + e2)
        w_ref[blk["rows"], :] = jnp.where(lane == 0, w1, jnp.where(lane == 1, w2, 0.0))
        e_ref[blk["rows"], :] = jnp.where(lane == 0, i1, jnp.where(lane == 1, i2, 0))


def _router(att, w_o, x, mod, g, router_pad, tm=1024):
    b, t, d = x.shape
    tm = _pick(t, tm)
    tok = lambda bi, i: (bi, i, 0)
    const = lambda bi, i: (0, 0)
    return pl.pallas_call(
        _router_kernel,
        grid=(b, t // tm),
        in_specs=[pl.BlockSpec((None, tm, att.shape[-1]), tok), pl.BlockSpec(w_o.shape, const),
                  pl.BlockSpec((None, tm, d), tok), _mod_spec(mod), pl.BlockSpec(g.shape, const),
                  pl.BlockSpec((d, LANE), const)],
        out_specs=[pl.BlockSpec((None, tm, d), tok), pl.BlockSpec((None, tm, d // 2), tok),
                   pl.BlockSpec((None, tm, LANE), tok), pl.BlockSpec((None, tm, LANE), tok)],
        out_shape=[jax.ShapeDtypeStruct((b, t, d), F32), jax.ShapeDtypeStruct((b, t, d // 2), jnp.int32),
                   jax.ShapeDtypeStruct((b, t, LANE), F32), jax.ShapeDtypeStruct((b, t, LANE), jnp.int32)],
        compiler_params=_cparams("parallel", "parallel"),
        name="moe_router",
    )(att, w_o, x, mod, g, router_pad)


SC_GATHER_ROWS = 64


def _gather_rows(table, idx):
    info = plsc.get_sparse_core_info()
    n_workers = info.num_cores * info.num_subcores
    n, width = idx.shape[0], table.shape[1]
    chunk = SC_GATHER_ROWS
    assert n % (n_workers * chunk) == 0
    n_chunks = n // (n_workers * chunk)
    mesh = plsc.VectorSubcoreMesh(core_axis_name="c", subcore_axis_name="s")

    assert n_chunks % 2 == 0

    def body(table_hbm, idx_hbm, out_hbm, idx_v, rows0, rows1, sem0, sem1):
        wid = lax.axis_index("s") * info.num_cores + lax.axis_index("c")
        pltpu.sync_copy(idx_hbm.at[wid], idx_v)
        bufs = ((rows0, sem0), (rows1, sem1))

        def gather(j, k):
            return pltpu.make_async_copy(table_hbm.at[idx_v.at[j]], bufs[k][0], bufs[k][1])

        gather(0, 0).start()

        @pl.loop(0, n_chunks, step=2)
        def _(j0):
            for k in (0, 1):
                j = j0 + k
                gather(j, k).wait()

                @pl.when(j + 1 < n_chunks)
                def _():
                    gather(j + 1, 1 - k).start()

                pltpu.sync_copy(bufs[k][0], out_hbm.at[pl.ds((wid * n_chunks + j) * chunk, chunk)])

    return pl.kernel(
        body,
        out_type=jax.ShapeDtypeStruct((n, width), table.dtype),
        mesh=mesh,
        scratch_types=[pltpu.VMEM((n_chunks, chunk), jnp.int32), pltpu.VMEM((chunk, width), table.dtype),
                       pltpu.VMEM((chunk, width), table.dtype), pltpu.SemaphoreType.DMA,
                       pltpu.SemaphoreType.DMA],
        name="sc_gather_rows",
    )(table, idx.reshape(n_workers, n_chunks, chunk))


def _scatter_rows_twice(table, idx0, idx1, n_out):
    info = plsc.get_sparse_core_info()
    n_workers = info.num_cores * info.num_subcores
    n, width = table.shape
    chunk = SC_GATHER_ROWS
    assert n % (n_workers * chunk) == 0
    n_chunks = n // (n_workers * chunk)
    mesh = plsc.VectorSubcoreMesh(core_axis_name="c", subcore_axis_name="s")

    def body(table_hbm, i0_hbm, i1_hbm, out_hbm, i0_v, i1_v, rows_v):
        wid = lax.axis_index("s") * info.num_cores + lax.axis_index("c")
        pltpu.sync_copy(i0_hbm.at[wid], i0_v)
        pltpu.sync_copy(i1_hbm.at[wid], i1_v)

        @pl.loop(0, n_chunks)
        def _(j):
            pltpu.sync_copy(table_hbm.at[pl.ds((wid * n_chunks + j) * chunk, chunk)], rows_v)
            pltpu.sync_copy(rows_v, out_hbm.at[i0_v.at[j]])
            pltpu.sync_copy(rows_v, out_hbm.at[i1_v.at[j]])

    shape3 = (n_workers, n_chunks, chunk)
    return pl.kernel(
        body,
        out_type=jax.ShapeDtypeStruct((n_out, width), table.dtype),
        mesh=mesh,
        scratch_types=[pltpu.VMEM((n_chunks, chunk), jnp.int32), pltpu.VMEM((n_chunks, chunk), jnp.int32),
                       pltpu.VMEM((chunk, width), table.dtype)],
        name="sc_scatter_rows",
    )(table, idx0.reshape(shape3), idx1.reshape(shape3))


def _pack_halves(x):
    k = x.shape[1] // 2
    bits = lambda v: lax.bitcast_convert_type(v.astype(BF16).astype(F32), jnp.uint32)
    word = lax.shift_right_logical(bits(x[:, :k]), jnp.uint32(16)) | bits(x[:, k:])
    return lax.bitcast_convert_type(word, jnp.int32)


def _unpack_halves(w):
    u = lax.bitcast_convert_type(w, jnp.uint32)
    lo = lax.bitcast_convert_type(lax.shift_left(u, jnp.uint32(16)), F32)
    hi = lax.bitcast_convert_type(u & jnp.uint32(0xFFFF0000), F32)
    return jnp.concatenate([lo, hi], axis=-1)


def _expert_ffn_kernel(te_ref, tv_ref, x_ref, wa_ref, wb_ref, wo_ref, o_ref):
    valid = tv_ref[pl.program_id(0)] != 0

    @pl.when(valid)
    def _():
        blocks = _row_blocks(x_ref.shape[0])
        for blk in blocks:
            h = _unpack_halves(x_ref[blk["rows"], :]).astype(BF16)
            a = jnp.dot(h, wa_ref[...], preferred_element_type=F32)
            bb = jnp.dot(h, wb_ref[...], preferred_element_type=F32)
            blk["u"] = (_silu(a) * bb).astype(BF16)
        for blk in blocks:
            o_ref[blk["rows"], :] = _pack_halves(jnp.dot(blk["u"], wo_ref[...], preferred_element_type=F32))

    @pl.when(jnp.logical_not(valid))
    def _():
        o_ref[...] = jnp.zeros_like(o_ref)


def _expert_ffn(xs, tile_expert, tile_valid, w_in, w_out, tm):
    p = xs.shape[0]
    n_exp, fdim, d = w_out.shape
    resident = lambda shape, col: pl.BlockSpec(shape, lambda j, te, tv: (te[j], 0, col))
    grid_spec = pltpu.PrefetchScalarGridSpec(
        num_scalar_prefetch=2,
        grid=(p // tm,),
        in_specs=[pl.BlockSpec((tm, d // 2), lambda j, te, tv: (j, 0)),
                  resident((None, d, fdim), 0), resident((None, d, fdim), 1), resident((None, fdim, d), 0)],
        out_specs=pl.BlockSpec((tm, d // 2), lambda j, te, tv: (j, 0)))
    return pl.pallas_call(
        _expert_ffn_kernel,
        grid_spec=grid_spec,
        out_shape=jax.ShapeDtypeStruct((p, d // 2), jnp.int32),
        compiler_params=_cparams("parallel"),
        name="moe_expert_ffn",
    )(tile_expert, tile_valid, xs, w_in, w_in, w_out)


def _moe_combine_kernel(x_ref, y0_ref, y1_ref, w_ref, mod_ref, g_ref, o_ref):
    y = w_ref[:, 0:1] * _unpack_halves(y0_ref[...]) + w_ref[:, 1:2] * _unpack_halves(y1_ref[...])
    o_ref[...] = x_ref[...] + mod_ref[5:6, :] * (_rms(y) * g_ref[...])


def _moe_combine(x, y2, w, mod, g, tm=1024):
    b, t, d = x.shape
    tm = _pick(t, tm)
    tok = lambda bi, i: (bi, i, 0)
    return pl.pallas_call(
        _moe_combine_kernel,
        grid=(b, t // tm),
        in_specs=[pl.BlockSpec((None, tm, d), tok),
                  pl.BlockSpec((None, None, tm, d // 2), lambda bi, i: (0, bi, i, 0)),
                  pl.BlockSpec((None, None, tm, d // 2), lambda bi, i: (1, bi, i, 0)),
                  pl.BlockSpec((None, tm, LANE), tok), _mod_spec(mod),
                  pl.BlockSpec((1, d), lambda bi, i: (0, 0))],
        out_specs=pl.BlockSpec((None, tm, d), tok),
        out_shape=jax.ShapeDtypeStruct((b, t, d), F32),
        compiler_params=_cparams("parallel", "parallel"),
        name="moe_combine",
    )(x, y2, y2, w, mod, g.reshape(1, d))


def _moe(x, h, w, e, mod, g_out, w_in, w_out, tm=512):
    b, t, d = x.shape
    n_exp = w_out.shape[0]
    n = b * t
    e_flat = jnp.concatenate([e[..., 0].reshape(n), e[..., 1].reshape(n)])
    onehot = (e_flat[:, None] == jnp.arange(n_exp, dtype=jnp.int32)).astype(jnp.int32)
    csum = jnp.cumsum(onehot, axis=0)
    rank = jnp.sum((csum - onehot) * onehot, axis=-1)
    counts = csum[-1]
    ends = jnp.cumsum((counts + tm - 1) // tm * tm)
    dest = (ends - (counts + tm - 1) // tm * tm)[e_flat] + rank
    p = 2 * n + n_exp * tm
    tile_start = jnp.arange(p // tm, dtype=jnp.int32) * tm
    tile_expert = jnp.minimum(jnp.searchsorted(ends, tile_start, side="right"), n_exp - 1).astype(jnp.int32)
    tile_valid = (tile_start < ends[-1]).astype(jnp.int32)

    xs = _scatter_rows_twice(h.reshape(n, d // 2), dest[:n], dest[n:], p)
    ys = _expert_ffn(xs, tile_expert, tile_valid, w_in, w_out, tm)
    y2 = _gather_rows(ys, dest)
    return _moe_combine(x, y2.reshape(2, b, t, d // 2), w, mod, g_out)


def _axial_angles(n_tokens, dim):
    t = jnp.arange(n_tokens)
    rows = (t // GRID_W).astype(F32)
    cols = (t % GRID_W).astype(F32)
    n_freq = dim // 4
    inv = ROPE_BASE ** (-jnp.arange(n_freq, dtype=F32) / n_freq)
    return jnp.concatenate([rows[:, None] * inv, cols[:, None] * inv], axis=-1)


def _rope_tables(n_tokens, dim):
    ang = _axial_angles(n_tokens, dim)
    cos, sin = jnp.cos(ang), jnp.sin(ang)
    if dim == LANE:
        return jnp.concatenate([cos, cos], -1), jnp.concatenate([-sin, sin], -1)
    z = jnp.zeros_like(cos)
    return jnp.concatenate([cos, z, cos, z], -1), jnp.concatenate([-sin, z, sin, z], -1)


def _spread_rope_cols(w):
    half = MLA_ROPE // 2
    z = jnp.zeros(w.shape[:-1] + (half,), w.dtype)
    return jnp.concatenate([w[..., :half], z, w[..., half:], z], axis=-1)


def kernel(x, c, ctx, c_ctx, mod_w, mod_b, norm_g, mix_in_w, ret_decay_logit, hgrn_lb_logit, mix_out_w,
           ffn_in_w, ffn_out_w, mla_down_w, mla_q_norm_g, mla_kv_norm_g, mla_uq_w, mla_ukv_w, mla_out_w,
           router_w, moe_in_w, moe_out_w):
    batch, seq, d = x.shape
    assert mod_w.shape[0] == 2, "two layers: retention/HGRN2 then MLA/MoE"

    rows = -(-(batch + 1) // 8) * 8
    cond = jnp.zeros((rows, d), F32).at[:batch].set(c).at[batch].set(c_ctx)
    mod = _modulation(cond, mod_w, mod_b)
    mod_x = [mod[l, :batch].reshape(batch, 6, d) for l in range(2)]
    mod_c = [mod[l, batch:batch + 1].reshape(1, 6, d) for l in range(2)]

    g = norm_g[0]
    ret_w = RET_HEADS * HEAD_DIM
    z_lo, z_hi = 5 * ret_w, 7 * ret_w
    w_in = mix_in_w[0].astype(BF16)
    w_main = jnp.concatenate([w_in[:, :z_lo], w_in[:, z_hi:]], axis=1)
    w_z = w_in[:, z_lo:z_hi]
    proj_x, z_x = _in_proj(x, mod_x[0], g[0], w_main, w_z)
    proj_c, z_c = _in_proj(ctx, mod_c[0], g[0], w_main, w_z)
    cos2, sin2 = _rope_tables(seq, HEAD_DIM)
    log_gamma = jax.nn.log_sigmoid(ret_decay_logit[0].astype(F32))
    lb = jnp.cumsum(jax.nn.softmax(hgrn_lb_logit.astype(F32), axis=0), axis=0)[0]
    ret_c, ret_x = _retention(proj_c, proj_x, log_gamma, cos2, sin2)
    hg_c, hg_x = _hgrn(proj_c, proj_x, z_c, z_x, lb, first=4 * RET_HEADS)
    w_out = mix_out_w[0].astype(BF16)
    w_outs = [w_out[:ret_w], w_out[ret_w:]]
    f_in, f_out = ffn_in_w[0].astype(BF16), ffn_out_w[0].astype(BF16)
    x = _mixer_ffn(ret_x, hg_x, *w_outs, x, mod_x[0], g, f_in, f_out)
    ctx = _mixer_ffn(ret_c, hg_c, *w_outs, ctx, mod_c[0], g, f_in, f_out)

    g = norm_g[1]
    dq = MLA_NOPE + MLA_ROPE
    wd = mla_down_w[0]
    wd = jnp.concatenate([wd[:, :MLA_Q_RANK + MLA_KV_RANK],
                          _spread_rope_cols(wd[:, MLA_Q_RANK + MLA_KV_RANK:])], axis=-1).astype(BF16)
    wq = mla_uq_w[0].reshape(MLA_Q_RANK, MLA_HEADS, dq)
    wq = jnp.concatenate([wq[..., :MLA_NOPE], _spread_rope_cols(wq[..., MLA_NOPE:])], axis=-1)
    wq = wq.reshape(MLA_Q_RANK, MLA_HEADS * MLA_QW).astype(BF16)
    wkv = mla_ukv_w[0].reshape(MLA_KV_RANK, MLA_HEADS, MLA_NOPE + MLA_V)
    wkv = jnp.concatenate([wkv[..., :MLA_NOPE].reshape(MLA_KV_RANK, -1),
                           wkv[..., MLA_NOPE:].reshape(MLA_KV_RANK, -1)], axis=-1).astype(BF16)
    cos_m, sin_m = _rope_tables(seq, MLA_ROPE)
    q, kn_x, kr_x, v_x = _mla_proj(x, mod_x[1], g[0], wd, mla_q_norm_g[0], mla_kv_norm_g[0], wq, wkv,
                                   cos_m, sin_m, rope=True, want_q=True)
    tc = ctx.shape[1]
    kn_c, kr_c, v_c = _mla_proj(ctx, mod_c[1], g[0], wd, mla_q_norm_g[0], mla_kv_norm_g[0], wq, wkv,
                                cos_m[:tc], sin_m[:tc], rope=False, want_q=False)
    att = _attention(q, kn_c, kr_c, v_c, kn_x, kr_x, v_x)
    router_pad = jnp.zeros((d, LANE), F32).at[:, :N_EXPERTS].set(router_w[0])
    x, h, w, e = _router(att, mla_out_w[0].astype(BF16), x, mod_x[1], g, router_pad)
    x = _moe(x, h, w, e, mod_x[1], g[3], moe_in_w[0].astype(BF16), moe_out_w[0].astype(BF16))
    return x
```

```python
import functools
import math

import jax
import jax.numpy as jnp
import numpy as np
from jax import lax
from jax.experimental import pallas as pl
from jax.experimental.pallas import tpu as pltpu
from jax.experimental.pallas import tpu_sc as plsc

F32 = jnp.float32
BF16 = jnp.bfloat16

NORM_EPS = 1e-6
ROPE_BASE = 10000.0
GRID_W = 64
LANE = 128

RET_HEADS = 4
HGRN_HEADS = 4
HEAD_DIM = 128
RET_CHUNK = 128
HGRN_CHUNK = 128

MLA_HEADS = 8
MLA_NOPE = 128
MLA_ROPE = 64
MLA_V = 128
MLA_Q_RANK = 384
MLA_KV_RANK = 256
N_EXPERTS = 8

VMEM_LIMIT = 56 * 1024 * 1024


def _cparams(*sem):
    return pltpu.CompilerParams(dimension_semantics=sem, vmem_limit_bytes=VMEM_LIMIT)


def _mm(a, b):
    return jnp.dot(a.astype(BF16), b.astype(BF16), preferred_element_type=F32)


def _mm_nt(a, b):
    return lax.dot_general(a.astype(BF16), b.astype(BF16), (((1,), (1,)), ((), ())),
                           preferred_element_type=F32)


def _mm_tn(a, b):
    return lax.dot_general(a.astype(BF16), b.astype(BF16), (((0,), (0,)), ((), ())),
                           preferred_element_type=F32)


def _rms(x):
    return x * lax.rsqrt(jnp.mean(x * x, axis=-1, keepdims=True) + NORM_EPS)


def _silu(x):
    return x * (1.0 / (1.0 + jnp.exp(-x)))


def _modnorm(x, g, mod_ref, shift_row, scale_row):
    shift = mod_ref[shift_row:shift_row + 1, :]
    scale = mod_ref[scale_row:scale_row + 1, :]
    return _rms(x) * g * (1.0 + scale) + shift


def _mod_spec(mod):
    d = mod.shape[-1]
    if mod.shape[0] == 1:
        return pl.BlockSpec((None, 6, d), lambda b, *_: (0, 0, 0))
    return pl.BlockSpec((None, 6, d), lambda b, *_: (b, 0, 0))


ROW_BLOCK = 256


def _row_blocks(n_rows):
    size = min(ROW_BLOCK, n_rows)
    return [dict(rows=pl.ds(r, size)) for r in range(0, n_rows, size)]


def _pick(n, pref):
    t = min(pref, n)
    while n % t:
        t -= LANE
    return t


def _mod_kernel(c_ref, w_ref, b_ref, o_ref):
    a = _silu(c_ref[...])
    o_ref[...] = jnp.dot(a, w_ref[...], preferred_element_type=F32,
                         precision=lax.Precision.HIGHEST) + b_ref[...]


def _modulation(cond, mod_w, mod_b):
    n_layers, d, n = mod_w.shape
    r = cond.shape[0]
    tn = _pick(n, 1536)
    return pl.pallas_call(
        _mod_kernel,
        grid=(n_layers, n // tn),
        in_specs=[pl.BlockSpec((r, d), lambda l, j: (0, 0)),
                  pl.BlockSpec((None, d, tn), lambda l, j: (l, 0, j)),
                  pl.BlockSpec((None, 1, tn), lambda l, j: (l, 0, j))],
        out_specs=pl.BlockSpec((None, r, tn), lambda l, j: (l, 0, j)),
        out_shape=jax.ShapeDtypeStruct((n_layers, r, n), F32),
        compiler_params=_cparams("parallel", "parallel"),
        name="modulation",
    )(cond, mod_w, mod_b.reshape(n_layers, 1, n))


def _in_proj_kernel(x_ref, mod_ref, g_ref, wm_ref, wz_ref, *refs, rope):
    if rope:
        cos_ref, sin_ref, om_ref, oz_ref = refs
    else:
        om_ref, oz_ref = refs
    ret_w, hg_w = RET_HEADS * HEAD_DIM, HGRN_HEADS * HEAD_DIM
    k_scale = HEAD_DIM ** -0.5

    def main(blk, lo, hi):
        return jnp.dot(blk["h"], wm_ref[:, lo:hi], preferred_element_type=F32)

    blocks = _row_blocks(x_ref.shape[0])
    for blk in blocks:
        blk["h"] = _modnorm(x_ref[blk["rows"], :], g_ref[...], mod_ref, 0, 1).astype(BF16)
    for blk in blocks:
        rows = blk["rows"]
        qk = main(blk, 0, 2 * ret_w)
        for j in range(2 * RET_HEADS):
            v = qk[:, j * HEAD_DIM:(j + 1) * HEAD_DIM]
            if j >= RET_HEADS:
                v = v * k_scale
            if rope:
                v = v * cos_ref[rows, :] + pltpu.roll(v, HEAD_DIM // 2, 1) * sin_ref[rows, :]
            om_ref[rows, j * HEAD_DIM:(j + 1) * HEAD_DIM] = v.astype(om_ref.dtype)
    for blk in blocks:
        om_ref[blk["rows"], 2 * ret_w:4 * ret_w] = main(blk, 2 * ret_w, 4 * ret_w).astype(om_ref.dtype)
    for blk in blocks:
        om_ref[blk["rows"], 4 * ret_w:] = main(blk, 4 * ret_w, 4 * ret_w + 3 * hg_w).astype(om_ref.dtype)
    for blk in blocks:
        oz_ref[blk["rows"], :] = jnp.dot(blk["h"], wz_ref[...], preferred_element_type=F32)


def _in_proj(x, mod, g, w_main, w_z, rope_tables=None, tm=1024):
    b, t, d = x.shape
    tm = _pick(t, tm)
    tok = lambda bi, i: (bi, i, 0)
    resident = lambda w: pl.BlockSpec(w.shape, lambda bi, i: (0, 0), pipeline_mode=pl.Buffered(1))
    n_main, n_z = w_main.shape[1], w_z.shape[1]
    tables = () if rope_tables is None else tuple(rope_tables)
    return pl.pallas_call(
        functools.partial(_in_proj_kernel, rope=rope_tables is not None),
        grid=(b, t // tm),
        in_specs=[pl.BlockSpec((None, tm, d), tok), _mod_spec(mod), pl.BlockSpec((1, d), lambda bi, i: (0, 0)),
                  resident(w_main), resident(w_z)]
        + [pl.BlockSpec((tm, HEAD_DIM), lambda bi, i: (i, 0)) for _ in tables],
        out_specs=[pl.BlockSpec((None, tm, n_main), tok), pl.BlockSpec((None, tm, n_z), tok)],
        out_shape=[jax.ShapeDtypeStruct((b, t, n_main), BF16), jax.ShapeDtypeStruct((b, t, n_z), F32)],
        compiler_params=_cparams("parallel", "parallel"),
        name="in_proj",
    )(x, mod, g.reshape(1, d), w_main, w_z, *tables)


def _mixer_ffn_kernel(ret_ref, hg_ref, w1_ref, w2_ref, x_ref, mod_ref, g_ref, wa_ref, wb_ref, wo_ref, o_ref):
    blocks = _row_blocks(x_ref.shape[0])
    for blk in blocks:
        rows = blk["rows"]
        blk["y"] = (jnp.dot(ret_ref[rows, :], w1_ref[...], preferred_element_type=F32)
                    + jnp.dot(hg_ref[rows, :], w2_ref[...], preferred_element_type=F32))
    for blk in blocks:
        blk["x1"] = x_ref[blk["rows"], :] + mod_ref[2:3, :] * (_rms(blk["y"]) * g_ref[1:2, :])
        blk["h"] = _modnorm(blk["x1"], g_ref[2:3, :], mod_ref, 3, 4).astype(BF16)
    for blk in blocks:
        a = jnp.dot(blk["h"], wa_ref[...], preferred_element_type=F32)
        bb = jnp.dot(blk["h"], wb_ref[...], preferred_element_type=F32)
        blk["u"] = (_silu(a) * bb).astype(BF16)
    for blk in blocks:
        z = jnp.dot(blk["u"], wo_ref[...], preferred_element_type=F32)
        o_ref[blk["rows"], :] = blk["x1"] + mod_ref[5:6, :] * (_rms(z) * g_ref[3:4, :])


def _mixer_ffn(ret, hg, w1, w2, x, mod, g, w_in, w_out, tm=1024):
    b, t, d = x.shape
    fdim = w_out.shape[0]
    tm = _pick(t, tm)
    tok = lambda bi, i: (bi, i, 0)
    resident = lambda shape, idx: pl.BlockSpec(shape, lambda bi, i: idx, pipeline_mode=pl.Buffered(1))
    return pl.pallas_call(
        _mixer_ffn_kernel,
        grid=(b, t // tm),
        in_specs=[pl.BlockSpec((None, tm, ret.shape[-1]), tok), pl.BlockSpec((None, tm, hg.shape[-1]), tok),
                  resident(w1.shape, (0, 0)), resident(w2.shape, (0, 0)),
                  pl.BlockSpec((None, tm, d), tok), _mod_spec(mod), pl.BlockSpec(g.shape, lambda bi, i: (0, 0)),
                  resident((d, fdim), (0, 0)), resident((d, fdim), (0, 1)), resident((fdim, d), (0, 0))],
        out_specs=pl.BlockSpec((None, tm, d), tok),
        out_shape=jax.ShapeDtypeStruct((b, t, d), F32),
        compiler_params=_cparams("parallel", "parallel"),
        name="mixer_ffn",
    )(ret, hg, w1, w2, x, mod, g, w_in, w_in, w_out)


SCAN_HEADS = 4
HGRN_SCAN_HEADS = 4


def _head_slice(hh):
    return slice(hh * HEAD_DIM, (hh + 1) * HEAD_DIM)


def _emit_scan_output(ch, o, cs, finish, g_ref, y_ref):
    rows = pl.ds(ch["start"], cs)
    if finish:
        other = ch["other_ref"][rows, ch["sl"]].astype(F32)
        gate = g_ref[rows, ch["sl"]].astype(F32)
        y_ref[rows, ch["sl"]] = (_rms(o + other) * _silu(gate)).astype(y_ref.dtype)
    else:
        ch["o_ref"][rows, ch["sl"]] = o.astype(ch["o_ref"].dtype)


def _run_scan(step, n_chunks):
    assert n_chunks % 2 == 0, "sequence lengths must be an even number of chunks"
    lax.fori_loop(0, n_chunks // 2, functools.partial(step, finish=False), 0)
    lax.fori_loop(n_chunks // 2, n_chunks, functools.partial(step, finish=True), 0)


def _retention_kernel(lg_ref, qc_ref, kc_ref, vc_ref, gc_ref, qx_ref, kx_ref, vx_ref, gx_ref,
                      yc_ref, yx_ref,
                      oc_ref, ox_ref, s_ref, intra_ref, qd_ref, kd_ref):
    cs = intra_ref.shape[1]
    hp = SCAN_HEADS
    head0 = pl.program_id(1) * hp
    row = lax.broadcasted_iota(jnp.int32, (cs, cs), 0).astype(F32)
    col = lax.broadcasted_iota(jnp.int32, (cs, cs), 1).astype(F32)
    pos = lax.broadcasted_iota(jnp.int32, (cs, HEAD_DIM), 0).astype(F32)

    for hh in range(hp):
        for d in (0, 1):
            c = 2 * hh + d
            lg = lg_ref[d, head0 + hh]
            rel = (col - row) if d else (row - col)
            intra_ref[c] = jnp.where(rel >= 0, jnp.exp(lg * jnp.maximum(rel, 0.0)), 0.0)
            p = (cs - 1.0 - pos) if d else pos
            qd_ref[c] = jnp.exp(lg * (p + 1.0))
            kd_ref[c] = jnp.exp(lg * (cs - 1.0 - p))
    s_ref[...] = jnp.zeros_like(s_ref)

    for (q_ref, k_ref, v_ref, g_ref, y_ref, of_ref, ob_ref) in (
            (qc_ref, kc_ref, vc_ref, gc_ref, yc_ref, oc_ref, oc_ref),
            (qx_ref, kx_ref, vx_ref, gx_ref, yx_ref, ox_ref, ox_ref)):
        n_chunks = q_ref.shape[0] // cs

        def step(i, carry, *, finish, q_ref=q_ref, k_ref=k_ref, v_ref=v_ref, g_ref=g_ref, y_ref=y_ref,
                 of_ref=of_ref, ob_ref=ob_ref, n_chunks=n_chunks):
            chains = []
            for d in (0, 1):
                start = pl.multiple_of(((n_chunks - 1 - i) if d else i) * cs, cs)
                for hh in range(hp):
                    sl = _head_slice(hh)
                    q = q_ref[pl.ds(start, cs), sl]
                    k = k_ref[pl.ds(start, cs), sl]
                    chains.append(dict(d=d, hh=hh, c=2 * hh + d, start=start, sl=sl, q=q, k=k,
                                       v=v_ref[pl.ds(start, cs), sl], o_ref=ob_ref if d else of_ref,
                                       other_ref=of_ref if d else ob_ref))
            for ch in chains:
                ch["att"] = _mm_nt(ch["q"], ch["k"]) * intra_ref[ch["c"]]
            for ch in chains:
                c = ch["c"]
                ch["state"] = s_ref[c]
                ch["o"] = _mm(ch["att"], ch["v"]) + _mm(ch["q"].astype(F32) * qd_ref[c], ch["state"])
            for ch in chains:
                c = ch["c"]
                c_dec = jnp.exp(jnp.full((1, HEAD_DIM), lg_ref[ch["d"], head0 + ch["hh"]] * cs, F32))
                s_ref[c] = ch["state"] * c_dec + _mm_tn(ch["k"].astype(F32) * kd_ref[c], ch["v"])
            for ch in chains:
                _emit_scan_output(ch, ch["o"], cs, finish, g_ref, y_ref)
            return carry

        _run_scan(step, n_chunks)


def _retention(proj_c, proj_x, log_gamma):
    b, tc, _ = proj_c.shape
    tx = proj_x.shape[1]
    h, hp = RET_HEADS, SCAN_HEADS
    w = hp * HEAD_DIM

    def col(t, group):
        return pl.BlockSpec((None, t, w), lambda bi, hi, lg: (bi, 0, group * (h // hp) + hi))

    n_chain = 2 * hp
    cs = math.gcd(RET_CHUNK, tc, tx)
    grid_spec = pltpu.PrefetchScalarGridSpec(
        num_scalar_prefetch=1,
        grid=(b, h // hp),
        in_specs=[col(tc, 0), col(tc, 1), col(tc, 2), col(tc, 3),
                  col(tx, 0), col(tx, 1), col(tx, 2), col(tx, 3)],
        out_specs=[pl.BlockSpec((None, tc, w), lambda bi, hi, lg: (bi, 0, hi)),
                   pl.BlockSpec((None, tx, w), lambda bi, hi, lg: (bi, 0, hi))],
        scratch_shapes=[pltpu.VMEM((tc, w), BF16), pltpu.VMEM((tx, w), BF16),
                        pltpu.VMEM((n_chain, HEAD_DIM, HEAD_DIM), F32),
                        pltpu.VMEM((n_chain, cs, cs), F32),
                        pltpu.VMEM((n_chain, cs, HEAD_DIM), F32),
                        pltpu.VMEM((n_chain, cs, HEAD_DIM), F32)])
    return pl.pallas_call(
        _retention_kernel,
        grid_spec=grid_spec,
        out_shape=[jax.ShapeDtypeStruct((b, tc, h * HEAD_DIM), BF16),
                   jax.ShapeDtypeStruct((b, tx, h * HEAD_DIM), BF16)],
        compiler_params=_cparams("parallel", "parallel"),
        name="retention",
    )(log_gamma, proj_c, proj_c, proj_c, proj_c, proj_x, proj_x, proj_x, proj_x)


def _hgrn_levels(cs):
    return [cs >> (i + 1) for i in range(int(math.log2(cs)))]


def _hgrn_constants(cs):
    t = np.arange(cs)[:, None]
    s = np.arange(cs)[None, :]
    tri = np.stack([(s <= t), (s >= t)]).astype(np.float32)
    masks = []
    for reverse in (False, True):
        lv = []
        for h in _hgrn_levels(cs):
            same = (t // (2 * h)) == (s // (2 * h))
            t_hi = (t % (2 * h)) >= h
            s_hi = (s % (2 * h)) >= h
            lv.append(same & (~t_hi & s_hi if reverse else t_hi & ~s_hi))
        lv.append(t == s)
        masks.append(np.stack(lv))
    return jnp.asarray(tri, BF16), jnp.asarray(np.stack(masks).astype(np.float32), BF16)


def _split3(x):
    hi = x.astype(BF16)
    r = x - hi.astype(F32)
    mid = r.astype(BF16)
    lo = (r - mid.astype(F32)).astype(BF16)
    return hi, mid, lo


def _neg_abs(x):
    bits = lax.bitcast_convert_type(x, jnp.uint32) | jnp.uint32(0x80000000)
    return lax.bitcast_convert_type(bits, F32)


def _level_reference(b, h, reverse):
    cs = b.shape[0]
    ref_local = h if reverse else h - 1
    if 2 * h >= 8:
        b3 = b.reshape(cs // (2 * h), 2 * h, b.shape[1])
        r = b3[:, ref_local:ref_local + 1, :]
        return jnp.broadcast_to(r, b3.shape).reshape(b.shape)
    local = lax.broadcasted_iota(jnp.int32, b.shape, 0) % (2 * h)
    out = b
    for m in range(2 * h):
        if m != ref_local:
            out = jnp.where(local == m, pltpu.roll(b, (m - ref_local) % cs, 0), out)
    return out


def _near_level_weight(f, h, reverse):
    cs = f.shape[0]
    local = lax.broadcasted_iota(jnp.int32, f.shape, 0) % (2 * h)
    before = pltpu.roll(f, 1, 0)
    after = pltpu.roll(f, cs - 1, 0)
    if h == 1:
        return jnp.where(local == (0 if reverse else 1), f, 1.0)
    if reverse:
        return jnp.where(local == 0, f * after, jnp.where(local == 1, f, jnp.where(local == 2, 1.0, before)))
    return jnp.where(local == 0, after, jnp.where(local == 1, 1.0, jnp.where(local == 2, f, f * before)))


def _hgrn_kernel(lb_ref, tri_ref, mask_ref,
                 qc_ref, vc_ref, gc_ref, zfc_ref, zbc_ref,
                 qx_ref, vx_ref, gx_ref, zfx_ref, zbx_ref,
                 yc_ref, yx_ref, oc_ref, ox_ref, st_ref):
    cs = HGRN_CHUNK
    hp = HGRN_SCAN_HEADS
    levels = _hgrn_levels(cs)
    st_ref[...] = jnp.zeros_like(st_ref)

    for (q_ref, v_ref, g_ref, y_ref, zf_ref, zb_ref, of_ref, ob_ref) in (
            (qc_ref, vc_ref, gc_ref, yc_ref, zfc_ref, zbc_ref, oc_ref, oc_ref),
            (qx_ref, vx_ref, gx_ref, yx_ref, zfx_ref, zbx_ref, ox_ref, ox_ref)):
        n_chunks = q_ref.shape[0] // cs

        def step(i, carry, *, finish, q_ref=q_ref, v_ref=v_ref, g_ref=g_ref, y_ref=y_ref, zf_ref=zf_ref,
                 zb_ref=zb_ref, of_ref=of_ref, ob_ref=ob_ref, n_chunks=n_chunks):
            chains = []
            for d in (0, 1):
                start = pl.multiple_of(((n_chunks - 1 - i) if d else i) * cs, cs)
                z_ref = zb_ref if d else zf_ref
                for hh in range(hp):
                    sl = _head_slice(hh)
                    lb = lb_ref[hh]
                    q = _silu(q_ref[pl.ds(start, cs), sl].astype(F32))
                    f = lb + (1.0 - lb) * (1.0 / (1.0 + jnp.exp(-z_ref[pl.ds(start, cs), sl])))
                    chains.append(dict(d=d, c=2 * hh + d, start=start, sl=sl, q=q, k=1.0 - f, f=f,
                                       parts=_split3(jnp.log2(f)), v=v_ref[pl.ds(start, cs), sl],
                                       o_ref=ob_ref if d else of_ref, other_ref=of_ref if d else ob_ref))
            for ch in chains:
                tri = tri_ref[ch["d"]]
                hi, mid, lo = ch["parts"]
                ch["bsum"] = (jnp.dot(tri, hi, preferred_element_type=F32)
                              + jnp.dot(tri, mid, preferred_element_type=F32)
                              + jnp.dot(tri, lo, preferred_element_type=F32))
            for ch in chains:
                ch["qb"], ch["kb"] = ch["q"].astype(BF16), ch["k"].astype(BF16)
                ch["att"] = _mm_nt(ch["qb"], ch["kb"]).astype(BF16) * mask_ref[ch["d"], len(levels)]
            for li, h in enumerate(levels):
                for ch in chains:
                    if h <= 2:
                        w = _near_level_weight(ch["f"], h, ch["d"] == 1).astype(BF16)
                    else:
                        ref = _level_reference(ch["bsum"], h, ch["d"] == 1)
                        w = jnp.exp2(_neg_abs(ch["bsum"] - ref)).astype(BF16)
                    ch["att"] += _mm_nt(ch["qb"] * w, ch["kb"] * w).astype(BF16) * mask_ref[ch["d"], li]
            for ch in chains:
                ch["state"] = st_ref[ch["c"]]
                ch["o"] = _mm(ch["att"], ch["v"]) + _mm_nt(ch["q"] * jnp.exp2(ch["bsum"]), ch["state"])
            for ch in chains:
                bsum = ch["bsum"]
                b_end = bsum[0:1, :] if ch["d"] else bsum[cs - 1:cs, :]
                st_ref[ch["c"]] = (ch["state"] * jnp.exp2(b_end)
                                   + _mm_tn(ch["v"], ch["k"] * jnp.exp2(b_end - bsum)))
            for ch in chains:
                _emit_scan_output(ch, ch["o"], cs, finish, g_ref, y_ref)
            return carry

        _run_scan(step, n_chunks)


def _hgrn(proj_c, proj_x, z_c, z_x, lb, first):
    b, tc, _ = proj_c.shape
    tx = proj_x.shape[1]
    h, hp = HGRN_HEADS, HGRN_SCAN_HEADS
    w = hp * HEAD_DIM
    tri, masks = _hgrn_constants(HGRN_CHUNK)

    def col(t, group):
        return pl.BlockSpec((None, t, w), lambda bi, hi: (bi, 0, (first + group * h) // hp + hi),
                            pipeline_mode=pl.Buffered(1))

    def zcol(t, group):
        return pl.BlockSpec((None, t, w), lambda bi, hi: (bi, 0, group * (h // hp) + hi),
                            pipeline_mode=pl.Buffered(1))

    ins = [col(tc, 0), col(tc, 1), col(tc, 2), zcol(tc, 0), zcol(tc, 1),
           col(tx, 0), col(tx, 1), col(tx, 2), zcol(tx, 0), zcol(tx, 1)]
    return pl.pallas_call(
        _hgrn_kernel,
        grid=(b, h // hp),
        in_specs=[pl.BlockSpec((hp, 1, HEAD_DIM), lambda bi, hi: (hi, 0, 0)),
                  pl.BlockSpec(tri.shape, lambda bi, hi: (0, 0, 0)),
                  pl.BlockSpec(masks.shape, lambda bi, hi: (0, 0, 0, 0))] + ins,
        out_specs=[pl.BlockSpec((None, tc, w), lambda bi, hi: (bi, 0, hi)),
                   pl.BlockSpec((None, tx, w), lambda bi, hi: (bi, 0, hi))],
        out_shape=[jax.ShapeDtypeStruct((b, tc, h * HEAD_DIM), BF16),
                   jax.ShapeDtypeStruct((b, tx, h * HEAD_DIM), BF16)],
        scratch_shapes=[pltpu.VMEM((tc, w), BF16), pltpu.VMEM((tx, w), BF16),
                        pltpu.VMEM((2 * hp, HEAD_DIM, HEAD_DIM), F32)],
        compiler_params=_cparams("parallel", "parallel"),
        name="hgrn2",
    )(lb.reshape(h, 1, HEAD_DIM), tri, masks, proj_c, proj_c, proj_c, z_c, z_c,
      proj_x, proj_x, proj_x, z_x, z_x)


MLA_QW = 2 * LANE
MLA_SCORE_SCALE = (MLA_NOPE + MLA_ROPE) ** -0.5 * math.log2(math.e)


def _mla_proj_kernel(x_ref, mod_ref, g_ref, wd_ref, gq_ref, gkv_ref, wq_ref, wkv_ref, cos_ref, sin_ref,
                     *out_refs, rope, want_q):
    if want_q:
        q_ref, kn_ref, kr_ref, v_ref = out_refs
    else:
        kn_ref, kr_ref, v_ref = out_refs
    nk = MLA_HEADS * MLA_NOPE

    def rot(v, rows):
        return v * cos_ref[rows, :] + pltpu.roll(v, LANE // 2, 1) * sin_ref[rows, :] if rope else v

    blocks = _row_blocks(x_ref.shape[0])
    for blk in blocks:
        blk["h"] = _modnorm(x_ref[blk["rows"], :], g_ref[...], mod_ref, 0, 1).astype(BF16)
    for blk in blocks:
        down = jnp.dot(blk["h"], wd_ref[...], preferred_element_type=F32)
        if want_q:
            blk["qn"] = (_rms(down[:, :MLA_Q_RANK]) * gq_ref[...]).astype(BF16)
        blk["kvn"] = (_rms(down[:, MLA_Q_RANK:MLA_Q_RANK + MLA_KV_RANK]) * gkv_ref[...]).astype(BF16)
        kr_ref[blk["rows"], :] = rot(down[:, MLA_Q_RANK + MLA_KV_RANK:], blk["rows"]).astype(kr_ref.dtype)
    for blk in blocks:
        rows = blk["rows"]
        kv = jnp.dot(blk["kvn"], wkv_ref[...], preferred_element_type=F32)
        kn_ref[rows, :] = kv[:, :nk].astype(kn_ref.dtype)
        v_ref[rows, :] = kv[:, nk:].astype(v_ref.dtype)
    if want_q:
        for blk in blocks:
            rows = blk["rows"]
            q = jnp.dot(blk["qn"], wq_ref[...], preferred_element_type=F32) * MLA_SCORE_SCALE
            for hd in range(MLA_HEADS):
                lo = hd * MLA_QW
                q_ref[rows, lo:lo + LANE] = q[:, lo:lo + LANE].astype(q_ref.dtype)
                q_ref[rows, lo + LANE:lo + 2 * LANE] = rot(q[:, lo + LANE:lo + 2 * LANE], rows
                                                           ).astype(q_ref.dtype)


def _mla_proj(x, mod, g, wd, gq, gkv, wq, wkv, cos, sin, rope, want_q, tm=1024):
    b, t, d = x.shape
    tm = _pick(t, tm)
    const = lambda bi, i: (0, 0)
    tok = lambda bi, i: (bi, i, 0)
    nk, nv = MLA_HEADS * MLA_NOPE, MLA_HEADS * MLA_V
    out_shape = [jax.ShapeDtypeStruct((b, t, nk), BF16), jax.ShapeDtypeStruct((b, t, LANE), BF16),
                 jax.ShapeDtypeStruct((b, t, nv), BF16)]
    out_specs = [pl.BlockSpec((None, tm, nk), tok), pl.BlockSpec((None, tm, LANE), tok),
                 pl.BlockSpec((None, tm, nv), tok)]
    if want_q:
        out_shape = [jax.ShapeDtypeStruct((b, t, MLA_HEADS * MLA_QW), BF16)] + out_shape
        out_specs = [pl.BlockSpec((None, tm, MLA_HEADS * MLA_QW), tok)] + out_specs
    return pl.pallas_call(
        functools.partial(_mla_proj_kernel, rope=rope, want_q=want_q),
        grid=(b, t // tm),
        in_specs=[pl.BlockSpec((None, tm, d), tok), _mod_spec(mod), pl.BlockSpec((1, d), const),
                  pl.BlockSpec(wd.shape, const), pl.BlockSpec((1, MLA_Q_RANK), const),
                  pl.BlockSpec((1, MLA_KV_RANK), const), pl.BlockSpec(wq.shape, const),
                  pl.BlockSpec(wkv.shape, const),
                  pl.BlockSpec((tm, LANE), lambda bi, i: (i, 0)),
                  pl.BlockSpec((tm, LANE), lambda bi, i: (i, 0))],
        out_specs=out_specs,
        out_shape=out_shape,
        compiler_params=_cparams("parallel", "parallel"),
        name="mla_proj",
    )(x, mod, g.reshape(1, d), wd, gq.reshape(1, -1), gkv.reshape(1, -1), wq, wkv, cos, sin)


def _lane_fold(x, op):
    out = x[:, :LANE]
    for j in range(1, x.shape[1] // LANE):
        out = op(out, x[:, j * LANE:(j + 1) * LANE])
    return out


def _attention_kernel(q_ref, knc_ref, krc_ref, vc_ref, knx_ref, krx_ref, vx_ref, o_ref,
                      kc_ref, kx_ref, vc2_ref, vx2_ref, sc0_ref, sc1_ref, sx0_ref, sx1_ref, m0_ref, m1_ref,
                      *, tk):
    step = pl.program_id(2)

    @pl.when(step == 0)
    def _():
        kc_ref[:, :LANE] = knc_ref[...]
        kc_ref[:, LANE:] = krc_ref[...]
        kx_ref[:, :LANE] = knx_ref[...]
        kx_ref[:, LANE:] = krx_ref[...]
        vc2_ref[:, :LANE] = vc_ref[...]
        vx2_ref[:, :LANE] = vx_ref[...]
        for ref in (vc2_ref, vx2_ref):
            lane = lax.broadcasted_iota(jnp.int32, (ref.shape[0], LANE), 1)
            ref[:, LANE:] = jnp.where(lane == 0, 1.0, 0.0).astype(ref.dtype)

    n_blocks = kx_ref.shape[0] // tk
    n_steps = pl.num_programs(2)
    nt = (((1,), (1,)), ((), ()))

    def body(cur, prev):
        if cur is not None:
            sc_cur, sx_cur, m_cur = cur
            q = q_ref[...]
            s = lax.dot_general(q, kc_ref[...], nt, preferred_element_type=F32)
            sc_cur[...] = s
            m_part = _lane_fold(s, jnp.maximum)
        if prev is not None:
            sc_prev, sx_prev, m_prev = prev
            m = jnp.max(m_prev[...], axis=-1, keepdims=True)
            p = jnp.exp2(sc_prev[...] - m)
            acc = jnp.dot(p.astype(BF16), vc2_ref[...], preferred_element_type=F32)
        for i in range(n_blocks):
            if cur is not None:
                s = lax.dot_general(q, kx_ref[i * tk:(i + 1) * tk, :], nt, preferred_element_type=F32)
                sx_cur[i] = s
                m_part = jnp.maximum(m_part, _lane_fold(s, jnp.maximum))
            if prev is not None:
                p = jnp.exp2(sx_prev[i] - m)
                acc = acc + jnp.dot(p.astype(BF16), vx2_ref[i * tk:(i + 1) * tk, :],
                                    preferred_element_type=F32)
        if cur is not None:
            m_cur[...] = m_part
        if prev is not None:
            o_ref[...] = (acc[:, :MLA_V] / acc[:, MLA_V:MLA_V + 1]).astype(o_ref.dtype)

    bufs = ((sc0_ref, sx0_ref, m0_ref), (sc1_ref, sx1_ref, m1_ref))
    first = step == 0
    last = step == n_steps - 1
    inner = jnp.logical_not(first | last)

    @pl.when(first)
    def _():
        body(bufs[0], None)

    @pl.when(inner & (step % 2 == 0))
    def _():
        body(bufs[0], bufs[1])

    @pl.when(inner & (step % 2 == 1))
    def _():
        body(bufs[1], bufs[0])

    for parity in (0, 1):
        @pl.when(last & (step % 2 == parity))
        def _(parity=parity):
            body(None, bufs[1 - parity])


def _attention(q, kn_c, kr_c, v_c, kn_x, kr_x, v_x, tq=1024, tk=256):
    b, t, _ = q.shape
    tc = kn_c.shape[1]
    tq = _pick(t, tq)
    tk = _pick(t, tk)
    nq = t // tq
    head_c = lambda bi, hi, i: (bi, 0, hi)
    shared = lambda bi, hi, i: (bi, 0, 0)
    return pl.pallas_call(
        functools.partial(_attention_kernel, tk=tk),
        grid=(b, MLA_HEADS, nq + 1),
        in_specs=[pl.BlockSpec((None, tq, MLA_QW), lambda bi, hi, i: (bi, jnp.minimum(i, nq - 1), hi)),
                  pl.BlockSpec((None, tc, LANE), head_c), pl.BlockSpec((None, tc, LANE), shared),
                  pl.BlockSpec((None, tc, LANE), head_c),
                  pl.BlockSpec((None, t, LANE), head_c), pl.BlockSpec((None, t, LANE), shared),
                  pl.BlockSpec((None, t, LANE), head_c)],
        out_specs=pl.BlockSpec((None, tq, MLA_V), lambda bi, hi, i: (bi, jnp.maximum(i - 1, 0), hi)),
        out_shape=jax.ShapeDtypeStruct((b, t, MLA_HEADS * MLA_V), BF16),
        scratch_shapes=[pltpu.VMEM((tc, MLA_QW), BF16), pltpu.VMEM((t, MLA_QW), BF16),
                        pltpu.VMEM((tc, 2 * MLA_V), BF16), pltpu.VMEM((t, 2 * MLA_V), BF16),
                        pltpu.VMEM((tq, tc), F32), pltpu.VMEM((tq, tc), F32),
                        pltpu.VMEM((t // tk, tq, tk), F32), pltpu.VMEM((t // tk, tq, tk), F32),
                        pltpu.VMEM((tq, LANE), F32), pltpu.VMEM((tq, LANE), F32)],
        compiler_params=_cparams("parallel", "parallel", "arbitrary"),
        name="mla_attention",
    )(q, kn_c, kr_c, v_c, kn_x, kr_x, v_x)


def _router_kernel(att_ref, wo_ref, x_ref, mod_ref, g_ref, wr_ref, xo_ref, h_ref, w_ref, e_ref):
    w_hi = wr_ref[...].astype(BF16)
    w_lo = (wr_ref[...] - w_hi.astype(F32)).astype(BF16)
    w_hilo = jnp.concatenate([w_hi, w_lo], axis=1)
    blocks = _row_blocks(x_ref.shape[0])
    for blk in blocks:
        blk["y"] = jnp.dot(att_ref[blk["rows"], :], wo_ref[...], preferred_element_type=F32)
    for blk in blocks:
        rows = blk["rows"]
        x1 = x_ref[rows, :] + mod_ref[2:3, :] * (_rms(blk["y"]) * g_ref[1:2, :])
        xo_ref[rows, :] = x1
        h = _modnorm(x1, g_ref[2:3, :], mod_ref, 3, 4)
        h_ref[rows, :] = _pack_halves(h)
        blk["h_hi"] = h.astype(BF16)
        blk["h_lo"] = (h - blk["h_hi"].astype(F32)).astype(BF16)
    for blk in blocks:
        both = jnp.dot(blk["h_hi"], w_hilo, preferred_element_type=F32)
        blk["logits"] = (both[:, :LANE] + both[:, LANE:]
                         + jnp.dot(blk["h_lo"], w_hi, preferred_element_type=F32))
    for blk in blocks:
        logits = blk["logits"]
        lane = lax.broadcasted_iota(jnp.int32, logits.shape, 1)
        neg = jnp.float32(-jnp.inf)
        lg = jnp.where(lane < N_EXPERTS, logits, neg)
        m1 = jnp.max(lg, axis=-1, keepdims=True)
        i1 = jnp.min(jnp.where(lg == m1, lane, LANE), axis=-1, keepdims=True)
        lg2 = jnp.where(lane == i1, neg, lg)
        m2 = jnp.max(lg2, axis=-1, keepdims=True)
        i2 = jnp.min(jnp.where(lg2 == m2, lane, LANE), axis=-1, keepdims=True)
        e2 = jnp.exp(m2 - m1)
        w1 = 1.0 / (1.0 + e2)
        w2 = e2 / (1.0 + e2)
        w_ref[blk["rows"], :] = jnp.where(lane == 0, w1, jnp.where(lane == 1, w2, 0.0))
        e_ref[blk["rows"], :] = jnp.where(lane == 0, i1, jnp.where(lane == 1, i2, 0))


def _router(att, w_o, x, mod, g, router_pad, tm=1024):
    b, t, d = x.shape
    tm = _pick(t, tm)
    tok = lambda bi, i: (bi, i, 0)
    const = lambda bi, i: (0, 0)
    return pl.pallas_call(
        _router_kernel,
        grid=(b, t // tm),
        in_specs=[pl.BlockSpec((None, tm, att.shape[-1]), tok), pl.BlockSpec(w_o.shape, const),
                  pl.BlockSpec((None, tm, d), tok), _mod_spec(mod), pl.BlockSpec(g.shape, const),
                  pl.BlockSpec((d, LANE), const)],
        out_specs=[pl.BlockSpec((None, tm, d), tok), pl.BlockSpec((None, tm, d // 2), tok),
                   pl.BlockSpec((None, tm, LANE), tok), pl.BlockSpec((None, tm, LANE), tok)],
        out_shape=[jax.ShapeDtypeStruct((b, t, d), F32), jax.ShapeDtypeStruct((b, t, d // 2), jnp.int32),
                   jax.ShapeDtypeStruct((b, t, LANE), F32), jax.ShapeDtypeStruct((b, t, LANE), jnp.int32)],
        compiler_params=_cparams("parallel", "parallel"),
        name="moe_router",
    )(att, w_o, x, mod, g, router_pad)


SC_GATHER_ROWS = 64


def _gather_rows(table, idx):
    info = plsc.get_sparse_core_info()
    n_workers = info.num_cores * info.num_subcores
    n, width = idx.shape[0], table.shape[1]
    chunk = SC_GATHER_ROWS
    assert n % (n_workers * chunk) == 0
    n_chunks = n // (n_workers * chunk)
    mesh = plsc.VectorSubcoreMesh(core_axis_name="c", subcore_axis_name="s")

    assert n_chunks % 2 == 0

    def body(table_hbm, idx_hbm, out_hbm, idx_v, rows0, rows1, sem0, sem1):
        wid = lax.axis_index("s") * info.num_cores + lax.axis_index("c")
        pltpu.sync_copy(idx_hbm.at[wid], idx_v)
        bufs = ((rows0, sem0), (rows1, sem1))

        def gather(j, k):
            return pltpu.make_async_copy(table_hbm.at[idx_v.at[j]], bufs[k][0], bufs[k][1])

        gather(0, 0).start()

        @pl.loop(0, n_chunks, step=2)
        def _(j0):
            for k in (0, 1):
                j = j0 + k
                gather(j, k).wait()

                @pl.when(j + 1 < n_chunks)
                def _():
                    gather(j + 1, 1 - k).start()

                pltpu.sync_copy(bufs[k][0], out_hbm.at[pl.ds((wid * n_chunks + j) * chunk, chunk)])

    return pl.kernel(
        body,
        out_type=jax.ShapeDtypeStruct((n, width), table.dtype),
        mesh=mesh,
        scratch_types=[pltpu.VMEM((n_chunks, chunk), jnp.int32), pltpu.VMEM((chunk, width), table.dtype),
                       pltpu.VMEM((chunk, width), table.dtype), pltpu.SemaphoreType.DMA,
                       pltpu.SemaphoreType.DMA],
        name="sc_gather_rows",
    )(table, idx.reshape(n_workers, n_chunks, chunk))


def _scatter_rows_twice(table, idx0, idx1, n_out):
    info = plsc.get_sparse_core_info()
    n_workers = info.num_cores * info.num_subcores
    n, width = table.shape
    chunk = SC_GATHER_ROWS
    assert n % (n_workers * chunk) == 0
    n_chunks = n // (n_workers * chunk)
    mesh = plsc.VectorSubcoreMesh(core_axis_name="c", subcore_axis_name="s")

    def body(table_hbm, i0_hbm, i1_hbm, out_hbm, i0_v, i1_v, rows_v):
        wid = lax.axis_index("s") * info.num_cores + lax.axis_index("c")
        pltpu.sync_copy(i0_hbm.at[wid], i0_v)
        pltpu.sync_copy(i1_hbm.at[wid], i1_v)

        @pl.loop(0, n_chunks)
        def _(j):
            pltpu.sync_copy(table_hbm.at[pl.ds((wid * n_chunks + j) * chunk, chunk)], rows_v)
            pltpu.sync_copy(rows_v, out_hbm.at[i0_v.at[j]])
            pltpu.sync_copy(rows_v, out_hbm.at[i1_v.at[j]])

    shape3 = (n_workers, n_chunks, chunk)
    return pl.kernel(
        body,
        out_type=jax.ShapeDtypeStruct((n_out, width), table.dtype),
        mesh=mesh,
        scratch_types=[pltpu.VMEM((n_chunks, chunk), jnp.int32), pltpu.VMEM((n_chunks, chunk), jnp.int32),
                       pltpu.VMEM((chunk, width), table.dtype)],
        name="sc_scatter_rows",
    )(table, idx0.reshape(shape3), idx1.reshape(shape3))


def _pack_halves(x):
    k = x.shape[1] // 2
    bits = lambda v: lax.bitcast_convert_type(v.astype(BF16).astype(F32), jnp.uint32)
    word = lax.shift_right_logical(bits(x[:, :k]), jnp.uint32(16)) | bits(x[:, k:])
    return lax.bitcast_convert_type(word, jnp.int32)


def _unpack_halves(w):
    u = lax.bitcast_convert_type(w, jnp.uint32)
    lo = lax.bitcast_convert_type(lax.shift_left(u, jnp.uint32(16)), F32)
    hi = lax.bitcast_convert_type(u & jnp.uint32(0xFFFF0000), F32)
    return jnp.concatenate([lo, hi], axis=-1)


def _expert_ffn_kernel(te_ref, tv_ref, x_ref, wa_ref, wb_ref, wo_ref, o_ref):
    valid = tv_ref[pl.program_id(0)] != 0

    @pl.when(valid)
    def _():
        blocks = _row_blocks(x_ref.shape[0])
        for blk in blocks:
            h = _unpack_halves(x_ref[blk["rows"], :]).astype(BF16)
            a = jnp.dot(h, wa_ref[...], preferred_element_type=F32)
            bb = jnp.dot(h, wb_ref[...], preferred_element_type=F32)
            blk["u"] = (_silu(a) * bb).astype(BF16)
        for blk in blocks:
            o_ref[blk["rows"], :] = _pack_halves(jnp.dot(blk["u"], wo_ref[...], preferred_element_type=F32))

    @pl.when(jnp.logical_not(valid))
    def _():
        o_ref[...] = jnp.zeros_like(o_ref)


def _expert_ffn(xs, tile_expert, tile_valid, w_in, w_out, tm):
    p = xs.shape[0]
    n_exp, fdim, d = w_out.shape
    resident = lambda shape, col: pl.BlockSpec(shape, lambda j, te, tv: (te[j], 0, col))
    grid_spec = pltpu.PrefetchScalarGridSpec(
        num_scalar_prefetch=2,
        grid=(p // tm,),
        in_specs=[pl.BlockSpec((tm, d // 2), lambda j, te, tv: (j, 0)),
                  resident((None, d, fdim), 0), resident((None, d, fdim), 1), resident((None, fdim, d), 0)],
        out_specs=pl.BlockSpec((tm, d // 2), lambda j, te, tv: (j, 0)))
    return pl.pallas_call(
        _expert_ffn_kernel,
        grid_spec=grid_spec,
        out_shape=jax.ShapeDtypeStruct((p, d // 2), jnp.int32),
        compiler_params=_cparams("parallel"),
        name="moe_expert_ffn",
    )(tile_expert, tile_valid, xs, w_in, w_in, w_out)


def _moe_combine_kernel(x_ref, y0_ref, y1_ref, w_ref, mod_ref, g_ref, o_ref):
    y = w_ref[:, 0:1] * _unpack_halves(y0_ref[...]) + w_ref[:, 1:2] * _unpack_halves(y1_ref[...])
    o_ref[...] = x_ref[...] + mod_ref[5:6, :] * (_rms(y) * g_ref[...])


def _moe_combine(x, y2, w, mod, g, tm=1024):
    b, t, d = x.shape
    tm = _pick(t, tm)
    tok = lambda bi, i: (bi, i, 0)
    return pl.pallas_call(
        _moe_combine_kernel,
        grid=(b, t // tm),
        in_specs=[pl.BlockSpec((None, tm, d), tok),
                  pl.BlockSpec((None, None, tm, d // 2), lambda bi, i: (0, bi, i, 0)),
                  pl.BlockSpec((None, None, tm, d // 2), lambda bi, i: (1, bi, i, 0)),
                  pl.BlockSpec((None, tm, LANE), tok), _mod_spec(mod),
                  pl.BlockSpec((1, d), lambda bi, i: (0, 0))],
        out_specs=pl.BlockSpec((None, tm, d), tok),
        out_shape=jax.ShapeDtypeStruct((b, t, d), F32),
        compiler_params=_cparams("parallel", "parallel"),
        name="moe_combine",
    )(x, y2, y2, w, mod, g.reshape(1, d))


def _moe(x, h, w, e, mod, g_out, w_in, w_out, tm=512):
    b, t, d = x.shape
    n_exp = w_out.shape[0]
    n = b * t
    e_flat = jnp.concatenate([e[..., 0].reshape(n), e[..., 1].reshape(n)])
    onehot = (e_flat[:, None] == jnp.arange(n_exp, dtype=jnp.int32)).astype(jnp.int32)
    csum = jnp.cumsum(onehot, axis=0)
    rank = jnp.sum((csum - onehot) * onehot, axis=-1)
    counts = csum[-1]
    ends = jnp.cumsum((counts + tm - 1) // tm * tm)
    dest = (ends - (counts + tm - 1) // tm * tm)[e_flat] + rank
    p = 2 * n + n_exp * tm
    tile_start = jnp.arange(p // tm, dtype=jnp.int32) * tm
    tile_expert = jnp.minimum(jnp.searchsorted(ends, tile_start, side="right"), n_exp - 1).astype(jnp.int32)
    tile_valid = (tile_start < ends[-1]).astype(jnp.int32)

    xs = _scatter_rows_twice(h.reshape(n, d // 2), dest[:n], dest[n:], p)
    ys = _expert_ffn(xs, tile_expert, tile_valid, w_in, w_out, tm)
    y2 = _gather_rows(ys, dest)
    return _moe_combine(x, y2.reshape(2, b, t, d // 2), w, mod, g_out)


def _axial_angles(n_tokens, dim):
    t = jnp.arange(n_tokens)
    rows = (t // GRID_W).astype(F32)
    cols = (t % GRID_W).astype(F32)
    n_freq = dim // 4
    inv = ROPE_BASE ** (-jnp.arange(n_freq, dtype=F32) / n_freq)
    return jnp.concatenate([rows[:, None] * inv, cols[:, None] * inv], axis=-1)


def _rope_tables(n_tokens, dim):
    ang = _axial_angles(n_tokens, dim)
    cos, sin = jnp.cos(ang), jnp.sin(ang)
    if dim == LANE:
        return jnp.concatenate([cos, cos], -1), jnp.concatenate([-sin, sin], -1)
    z = jnp.zeros_like(cos)
    return jnp.concatenate([cos, z, cos, z], -1), jnp.concatenate([-sin, z, sin, z], -1)


def _spread_rope_cols(w):
    half = MLA_ROPE // 2
    z = jnp.zeros(w.shape[:-1] + (half,), w.dtype)
    return jnp.concatenate([w[..., :half], z, w[..., half:], z], axis=-1)


def kernel(x, c, ctx, c_ctx, mod_w, mod_b, norm_g, mix_in_w, ret_decay_logit, hgrn_lb_logit, mix_out_w,
           ffn_in_w, ffn_out_w, mla_down_w, mla_q_norm_g, mla_kv_norm_g, mla_uq_w, mla_ukv_w, mla_out_w,
           router_w, moe_in_w, moe_out_w):
    batch, seq, d = x.shape
    assert mod_w.shape[0] == 2, "two layers: retention/HGRN2 then MLA/MoE"

    rows = -(-(batch + 1) // 8) * 8
    cond = jnp.zeros((rows, d), F32).at[:batch].set(c).at[batch].set(c_ctx)
    mod = _modulation(cond, mod_w, mod_b)
    mod_x = [mod[l, :batch].reshape(batch, 6, d) for l in range(2)]
    mod_c = [mod[l, batch:batch + 1].reshape(1, 6, d) for l in range(2)]

    g = norm_g[0]
    ret_w = RET_HEADS * HEAD_DIM
    z_lo, z_hi = 5 * ret_w, 7 * ret_w
    w_in = mix_in_w[0].astype(BF16)
    w_main = jnp.concatenate([w_in[:, :z_lo], w_in[:, z_hi:]], axis=1)
    w_z = w_in[:, z_lo:z_hi]
    proj_x, z_x = _in_proj(x, mod_x[0], g[0], w_main, w_z, _rope_tables(seq, HEAD_DIM))
    proj_c, z_c = _in_proj(ctx, mod_c[0], g[0], w_main, w_z)
    log_gamma = jax.nn.log_sigmoid(ret_decay_logit[0].astype(F32))
    lb = jnp.cumsum(jax.nn.softmax(hgrn_lb_logit.astype(F32), axis=0), axis=0)[0]
    ret_c, ret_x = _retention(proj_c, proj_x, log_gamma)
    hg_c, hg_x = _hgrn(proj_c, proj_x, z_c, z_x, lb, first=4 * RET_HEADS)
    w_out = mix_out_w[0].astype(BF16)
    w_outs = [w_out[:ret_w], w_out[ret_w:]]
    f_in, f_out = ffn_in_w[0].astype(BF16), ffn_out_w[0].astype(BF16)
    x = _mixer_ffn(ret_x, hg_x, *w_outs, x, mod_x[0], g, f_in, f_out)
    ctx = _mixer_ffn(ret_c, hg_c, *w_outs, ctx, mod_c[0], g, f_in, f_out)

    g = norm_g[1]
    dq = MLA_NOPE + MLA_ROPE
    wd = mla_down_w[0]
    wd = jnp.concatenate([wd[:, :MLA_Q_RANK + MLA_KV_RANK],
                          _spread_rope_cols(wd[:, MLA_Q_RANK + MLA_KV_RANK:])], axis=-1).astype(BF16)
    wq = mla_uq_w[0].reshape(MLA_Q_RANK, MLA_HEADS, dq)
    wq = jnp.concatenate([wq[..., :MLA_NOPE], _spread_rope_cols(wq[..., MLA_NOPE:])], axis=-1)
    wq = wq.reshape(MLA_Q_RANK, MLA_HEADS * MLA_QW).astype(BF16)
    wkv = mla_ukv_w[0].reshape(MLA_KV_RANK, MLA_HEADS, MLA_NOPE + MLA_V)
    wkv = jnp.concatenate([wkv[..., :MLA_NOPE].reshape(MLA_KV_RANK, -1),
                           wkv[..., MLA_NOPE:].reshape(MLA_KV_RANK, -1)], axis=-1).astype(BF16)
    cos_m, sin_m = _rope_tables(seq, MLA_ROPE)
    q, kn_x, kr_x, v_x = _mla_proj(x, mod_x[1], g[0], wd, mla_q_norm_g[0], mla_kv_norm_g[0], wq, wkv,
                                   cos_m, sin_m, rope=True, want_q=True)
    tc = ctx.shape[1]
    kn_c, kr_c, v_c = _mla_proj(ctx, mod_c[1], g[0], wd, mla_q_norm_g[0], mla_kv_norm_g[0], wq, wkv,
                                cos_m[:tc], sin_m[:tc], rope=False, want_q=False)
    att = _attention(q, kn_c, kr_c, v_c, kn_x, kr_x, v_x)
    router_pad = jnp.zeros((d, LANE), F32).at[:, :N_EXPERTS].set(router_w[0])
    x, h, w, e = _router(att, mla_out_w[0].astype(BF16), x, mod_x[1], g, router_pad)
    x = _moe(x, h, w, e, mod_x[1], g[3], moe_in_w[0].astype(BF16), moe_out_w[0].astype(BF16))
    return x
```
